```python
import math
import jax, jax.numpy as jnp
from jax import lax
import numpy as np

D_MODEL = 1024
BATCH = 8
SEQ = 4096
DEPTH = 1
DEC_BATCH = 32
DEC_SEQ = 1
PAST_LEN = 16384
PAGE_SIZE = 128

SWA_CONFIGS = ((128, 1), (512, 4), (2048, 16))
N_SWA_GROUPS = 3
SWA_HEADS = 8
SWA_HEAD_DIM = 64
SWA_WIDTH = N_SWA_GROUPS * SWA_HEADS * SWA_HEAD_DIM
SWA_OUT = SWA_HEADS * SWA_HEAD_DIM
SWA_BLOCK = 128
DN_HEADS = 8
DN_DK = 128
DN_DV = 128
DN_CONV = 4
DN_CONV_DIM = DN_HEADS * (2 * DN_DK + DN_DV)
DN_CHUNK = 64
N_EXPERT_GROUPS = 4
EXPERTS_PER_GROUP = 8
N_EXPERTS = N_EXPERT_GROUPS * EXPERTS_PER_GROUP
TOP_K_IN_GROUP = 2
D_EXPERT = 512
MOE_BLOCK = 128
EPS = 1e-6

IN_SIZES = (SWA_WIDTH, SWA_WIDTH, SWA_WIDTH, DN_CONV_DIM, DN_HEADS, DN_HEADS, DN_HEADS * DN_DV, D_MODEL, D_MODEL)
IN_WIDTH = 3 * SWA_WIDTH + DN_CONV_DIM + 2 * DN_HEADS + DN_HEADS * DN_DV + 2 * D_MODEL

kernel_name = "dilated_swa_gdn_hmoe_step"


def _split_points():
    pts, acc = [], 0
    for s in IN_SIZES[:-1]:
        acc += s
        pts.append(acc)
    return pts


def _rmsnorm(x, g):
    xf = x.astype(jnp.float32)
    y = xf * lax.rsqrt(jnp.mean(xf * xf, axis=-1, keepdims=True) + EPS)
    return (y * g.astype(jnp.float32)).astype(x.dtype)


def _l2norm(x):
    return x * lax.rsqrt(jnp.sum(x * x, axis=-1, keepdims=True) + EPS)


def _dilated_window_prompt(q, k, v, win, dil):
    Bn, L, H, E = q.shape
    M = L // dil
    n_blk = -(-M // SWA_BLOCK)
    Mp = n_blk * SWA_BLOCK
    span = win // dil

    def to_blocks(a):
        a = a.astype(jnp.float32).reshape(Bn, M, dil, H, E)
        a = jnp.pad(a, ((0, 0), (0, Mp - M), (0, 0), (0, 0), (0, 0)))
        return a.reshape(Bn, n_blk, SWA_BLOCK, dil, H, E)

    def with_prev(a):
        prev = jnp.pad(a, ((0, 0), (1, 0), (0, 0), (0, 0), (0, 0), (0, 0)))[:, :-1]
        return jnp.concatenate([prev, a], axis=2)

    qb, kb, vb = to_blocks(q), to_blocks(k), to_blocks(v)
    kk, vv = with_prev(kb), with_prev(vb)
    s = jnp.einsum('bnqrhe,bnkrhe->bnrhqk', qb, kk) * (E ** -0.5)
    qi = jnp.arange(SWA_BLOCK)[:, None]
    ki = jnp.arange(2 * SWA_BLOCK)[None, :]
    dist = SWA_BLOCK + qi - ki
    band = (dist >= 0) & (dist <= span)
    after_start = (jnp.arange(n_blk)[:, None, None] > 0) | (ki >= SWA_BLOCK)[None]
    mask = band[None] & after_start
    s = jnp.where(mask[None, :, None, None], s, -jnp.inf)
    mx = jnp.max(s, axis=-1, keepdims=True)
    p = jnp.exp(s - mx)
    den = jnp.sum(p, axis=-1)
    o = jnp.einsum('bnrhqk,bnkrhe->bnqrhe', p, vv) / jnp.moveaxis(den, 4, 2)[..., None]
    lse = jnp.moveaxis(mx[..., 0] + jnp.log(den), 4, 2)
    o = o.reshape(Bn, Mp, dil, H, E)[:, :M].reshape(Bn, L, H, E)
    lse = lse.reshape(Bn, Mp, dil, H)[:, :M].reshape(Bn, L, H)
    return o, lse


def _dilated_window_step(q, k, v, buf, win, dil):
    Bn, T, H, E = q.shape
    Wb = buf.shape[1]
    keys = jnp.concatenate([buf[:, :, 0].astype(jnp.float32), k.astype(jnp.float32)], axis=1)
    vals = jnp.concatenate([buf[:, :, 1].astype(jnp.float32), v.astype(jnp.float32)], axis=1)
    j = jnp.arange(win // dil + 1)
    idx = Wb + jnp.arange(T)[:, None] - j[None, :] * dil
    valid = idx >= 0
    idx = jnp.maximum(idx, 0)
    kg = keys[:, idx]
    vg = vals[:, idx]
    s = jnp.einsum('bthe,btjhe->bthj', q.astype(jnp.float32), kg) * (E ** -0.5)
    s = jnp.where(valid[None, :, None, :], s, -jnp.inf)
    mx = jnp.max(s, axis=-1, keepdims=True)
    p = jnp.exp(s - mx)
    den = jnp.sum(p, axis=-1)
    o = jnp.einsum('bthj,btjhe->bthe', p, vg) / den[..., None]
    return o, mx[..., 0] + jnp.log(den)


def _dn_inputs(qkv_raw, conv_prev, conv_w, b_raw, a_raw, a_log, dt_bias):
    Bn, L, _ = qkv_raw.shape
    xp = jnp.concatenate([conv_prev.astype(qkv_raw.dtype), qkv_raw], axis=1)
    conv = conv_w[0] * xp[:, 0:L]
    for i in range(1, DN_CONV):
        conv = conv + conv_w[i] * xp[:, i:i + L]
    new_conv = xp[:, -(DN_CONV - 1):]
    act = jax.nn.silu(conv.astype(jnp.float32))
    qd, kd, vd = jnp.split(act, [DN_HEADS * DN_DK, 2 * DN_HEADS * DN_DK], axis=-1)
    qd = _l2norm(qd.reshape(Bn, L, DN_HEADS, DN_DK)) * (DN_DK ** -0.5)
    kd = _l2norm(kd.reshape(Bn, L, DN_HEADS, DN_DK))
    vd = vd.reshape(Bn, L, DN_HEADS, DN_DV)
    beta = jax.nn.sigmoid(b_raw.astype(jnp.float32))
    g = -jnp.exp(a_log.astype(jnp.float32)) * jax.nn.softplus(a_raw.astype(jnp.float32) + dt_bias.astype(jnp.float32))
    return qd, kd, vd, beta, g, new_conv


def _gated_delta_chunked(q, k, v, beta, g, s0):
    Bn, L, H, _ = q.shape
    C = min(DN_CHUNK, L)
    Lp = -(-L // C) * C
    n = Lp // C

    def chunks(a):
        a = jnp.pad(a, [(0, 0), (0, Lp - L)] + [(0, 0)] * (a.ndim - 2))
        return a.reshape((Bn, n, C) + a.shape[2:])

    q, k, v, beta, g = chunks(q), chunks(k), chunks(v), chunks(beta), chunks(g)
    gc = jnp.cumsum(g, axis=2)
    gch = jnp.moveaxis(gc, 3, 2)
    causal = jnp.tril(jnp.ones((C, C), dtype=bool))
    strict = jnp.tril(jnp.ones((C, C), dtype=bool), -1)
    decay = jnp.exp(jnp.where(causal, gch[..., :, None] - gch[..., None, :], -jnp.inf))
    kb = k * beta[..., None]
    lower = jnp.where(strict, jnp.einsum('bnihd,bnjhd->bnhij', kb, k) * decay, 0.0)
    eye = jnp.eye(C, dtype=jnp.float32)
    tmat = jax.lax.linalg.triangular_solve(lower + eye, jnp.broadcast_to(eye, lower.shape), left_side=True, lower=True)
    u = jnp.einsum('bnhij,bnjhd->bnihd', tmat, v * beta[..., None])
    w = jnp.einsum('bnhij,bnjhd->bnihd', tmat, kb * jnp.exp(gc)[..., None])
    a_intra = jnp.einsum('bnihd,bnjhd->bnhij', q, k) * decay
    q_dec = q * jnp.exp(gc)[..., None]
    k_dec = k * jnp.exp(gc[:, :, -1:] - gc)[..., None]
    g_tot = jnp.exp(gc[:, :, -1])
    xs = (jnp.moveaxis(u, 1, 0), jnp.moveaxis(w, 1, 0), jnp.moveaxis(a_intra, 1, 0),
          jnp.moveaxis(q_dec, 1, 0), jnp.moveaxis(k_dec, 1, 0), jnp.moveaxis(g_tot, 1, 0))

    def step(S, xc):
        u_i, w_i, a_i, qd_i, kd_i, gt_i = xc
        v_new = u_i - jnp.einsum('bihk,bhkv->bihv', w_i, S)
        o_i = jnp.einsum('bihk,bhkv->bihv', qd_i, S) + jnp.einsum('bhij,bjhv->bihv', a_i, v_new)
        S = S * gt_i[..., None, None] + jnp.einsum('bihk,bihv->bhkv', kd_i, v_new)
        return S, o_i

    S, o = lax.scan(step, s0, xs)
    o = jnp.moveaxis(o, 0, 1).reshape(Bn, Lp, H, DN_DV)[:, :L]
    return o, S


def _experts_dropless(h, experts, weights, w_gate, w_up, w_down):
    N, D = h.shape
    A = N * TOP_K_IN_GROUP
    e_flat = experts.reshape(-1)
    tok_flat = jnp.repeat(jnp.arange(N, dtype=jnp.int32), TOP_K_IN_GROUP)
    w_flat = weights.reshape(-1)
    order = jnp.argsort(e_flat)
    e_s, tok_s, w_s = e_flat[order], tok_flat[order], w_flat[order]
    counts = jax.ops.segment_sum(jnp.ones_like(e_flat), e_flat, num_segments=N_EXPERTS)
    start = jnp.cumsum(counts) - counts
    pcounts = (counts + MOE_BLOCK - 1) // MOE_BLOCK * MOE_BLOCK
    pend = jnp.cumsum(pcounts)
    pstart = pend - pcounts
    dest = pstart[e_s] + (jnp.arange(A, dtype=jnp.int32) - start[e_s])
    n_blocks = -(-A // MOE_BLOCK) + N_EXPERTS
    P = n_blocks * MOE_BLOCK
    row_tok = jnp.full((P,), N, dtype=jnp.int32).at[dest].set(tok_s)
    row_w = jnp.zeros((P,), jnp.float32).at[dest].set(w_s)
    blk_e = jnp.minimum(jnp.searchsorted(pend, jnp.arange(n_blocks) * MOE_BLOCK, side='right'), N_EXPERTS - 1)
    h_pad = jnp.concatenate([h, jnp.zeros((1, D), h.dtype)], axis=0)
    xb = h_pad[row_tok].reshape(n_blocks, MOE_BLOCK, D)

    def block_ffn(args):
        xblk, e = args
        return (jax.nn.silu(xblk @ w_gate[e]) * (xblk @ w_up[e])) @ w_down[e]

    yb = lax.map(block_ffn, (xb, blk_e)).reshape(P, D)
    y = jax.ops.segment_sum(yb.astype(jnp.float32) * row_w[:, None], row_tok, num_segments=N + 1)[:N]
    return y.astype(h.dtype)


def _hier_moe(x, ln2_g, w_rg, b_rg, w_re, b_re, w_e_gate, w_e_up, w_e_down):
    Bn, L, D = x.shape
    h = _rmsnorm(x, ln2_g).reshape(-1, D)
    N = h.shape[0]
    hf = h.astype(jnp.float32)
    lg = hf @ w_rg.astype(jnp.float32) + b_rg.astype(jnp.float32)
    pg = jax.nn.softmax(lg, axis=-1)
    gsel = jnp.argmax(lg, axis=-1)
    le = (hf @ w_re.astype(jnp.float32) + b_re.astype(jnp.float32)).reshape(N, N_EXPERT_GROUPS, EXPERTS_PER_GROUP)
    le_sel = jnp.take_along_axis(le, gsel[:, None, None], axis=1)[:, 0]
    top_v, top_i = lax.top_k(le_sel, TOP_K_IN_GROUP)
    w_sel = jax.nn.softmax(top_v, axis=-1) * jnp.take_along_axis(pg, gsel[:, None], axis=1)
    experts = (gsel[:, None] * EXPERTS_PER_GROUP + top_i).astype(jnp.int32)
    y = _experts_dropless(h, experts, w_sel, w_e_gate, w_e_up, w_e_down)
    return x + y.reshape(Bn, L, D)


def _layer(x, swa_bufs, conv_prev, s0, ln1_g, w_in, q_norm_g, k_norm_g, dn_conv_w, dn_a_log, dn_dt_bias,
           dn_norm_g, w_out_a, w_out_b, w_o, ln2_g, w_rg, b_rg, w_re, b_re, w_e_gate, w_e_up, w_e_down):
    Bn, L, _ = x.shape
    h = _rmsnorm(x, ln1_g)
    z = h @ w_in
    q_a, k_a, v_a, qkv_dn, b_raw, a_raw, z_out, gate_a, gate_b = jnp.split(z, _split_points(), axis=-1)
    shp = (Bn, L, N_SWA_GROUPS, SWA_HEADS, SWA_HEAD_DIM)
    q_a = _rmsnorm(q_a.reshape(shp), q_norm_g[:, None, :])
    k_a = _rmsnorm(k_a.reshape(shp), k_norm_g[:, None, :])
    v_a = v_a.reshape(shp)
    outs, lses, new_swa = [], [], []
    for gi in range(N_SWA_GROUPS):
        win, dil = SWA_CONFIGS[gi]
        qg, kg, vg = q_a[:, :, gi], k_a[:, :, gi], v_a[:, :, gi]
        if swa_bufs is None:
            o, lse = _dilated_window_prompt(qg, kg, vg, win, dil)
            keep = min(win, L)
            new_swa.append(jnp.stack([kg[:, L - keep:], vg[:, L - keep:]], axis=2))
        else:
            o, lse = _dilated_window_step(qg, kg, vg, swa_bufs[gi], win, dil)
            new_swa.append(jnp.stack([kg, vg], axis=2))
        outs.append(o)
        lses.append(lse)
    alpha = jax.nn.softmax(jnp.stack(lses, axis=0), axis=0)
    o_a = jnp.einsum('gblh,gblhe->blhe', alpha, jnp.stack(outs, axis=0)).reshape(Bn, L, SWA_OUT).astype(x.dtype)
    y_a = o_a @ w_out_a

    if conv_prev is None:
        conv_prev = jnp.zeros((Bn, DN_CONV - 1, DN_CONV_DIM), x.dtype)
    if s0 is None:
        s0 = jnp.zeros((Bn, DN_HEADS, DN_DK, DN_DV), jnp.float32)
    qd, kd, vd, beta, g, new_conv = _dn_inputs(qkv_dn, conv_prev, dn_conv_w, b_raw, a_raw, dn_a_log, dn_dt_bias)
    o_d, s_new = _gated_delta_chunked(qd, kd, vd, beta, g, s0.astype(jnp.float32))
    o_d = _rmsnorm(o_d, dn_norm_g) * jax.nn.silu(z_out.astype(jnp.float32).reshape(Bn, L, DN_HEADS, DN_DV))
    y_b = o_d.reshape(Bn, L, DN_HEADS * DN_DV).astype(x.dtype) @ w_out_b

    mix = jax.nn.sigmoid(gate_a) * y_a + jax.nn.sigmoid(gate_b) * y_b
    x = x + mix @ w_o
    x = _hier_moe(x, ln2_g, w_rg, b_rg, w_re, b_re, w_e_gate, w_e_up, w_e_down)
    return x, new_swa, new_conv, s_new


def setup_inputs(seed: int = 0) -> dict:
    key = jax.random.key(seed)
    ks = jax.random.split(key, 32)
    f32 = jnp.float32

    def nrm(k, shape, scale):
        return jax.random.normal(k, shape, f32) * scale

    dt = jnp.exp(jax.random.uniform(ks[13], (DEPTH, DN_HEADS), f32, math.log(1e-3), math.log(1e-1)))
    return {
        "x_prompt": nrm(ks[0], (BATCH, SEQ, D_MODEL), 1.0),
        "x_sample": nrm(ks[1], (DEC_BATCH, DEC_SEQ, D_MODEL), 1.0),
        "cache_swa0_kv": nrm(ks[2], (DEPTH, DEC_BATCH, min(SWA_CONFIGS[0][0], PAST_LEN), 2, SWA_HEADS, SWA_HEAD_DIM), 1.0),
        "cache_swa1_kv": nrm(ks[3], (DEPTH, DEC_BATCH, min(SWA_CONFIGS[1][0], PAST_LEN), 2, SWA_HEADS, SWA_HEAD_DIM), 1.0),
        "cache_swa2_kv": nrm(ks[4], (DEPTH, DEC_BATCH, min(SWA_CONFIGS[2][0], PAST_LEN), 2, SWA_HEADS, SWA_HEAD_DIM), 1.0),
        "state_dn_conv": nrm(ks[5], (DEPTH, DEC_BATCH, DN_CONV - 1, DN_CONV_DIM), 1.0),
        "state_dn_S": nrm(ks[6], (DEPTH, DEC_BATCH, DN_HEADS, DN_DK, DN_DV), 0.1),
        "ln1_g": 1.0 + nrm(ks[7], (DEPTH, D_MODEL), 0.02),
        "w_in": nrm(ks[8], (DEPTH, D_MODEL, IN_WIDTH), D_MODEL ** -0.5),
        "q_norm_g": 1.0 + nrm(ks[9], (DEPTH, N_SWA_GROUPS, SWA_HEAD_DIM), 0.02),
        "k_norm_g": 1.0 + nrm(ks[10], (DEPTH, N_SWA_GROUPS, SWA_HEAD_DIM), 0.02),
        "dn_conv_w": nrm(ks[11], (DEPTH, DN_CONV, DN_CONV_DIM), DN_CONV ** -0.5),
        "dn_a_log": jnp.log(jax.random.uniform(ks[12], (DEPTH, DN_HEADS), f32, 1.0, 16.0)),
        "dn_dt_bias": dt + jnp.log(-jnp.expm1(-dt)),
        "dn_norm_g": 1.0 + nrm(ks[14], (DEPTH, DN_DV), 0.02),
        "w_out_a": nrm(ks[15], (DEPTH, SWA_OUT, D_MODEL), SWA_OUT ** -0.5),
        "w_out_b": nrm(ks[16], (DEPTH, DN_HEADS * DN_DV, D_MODEL), (DN_HEADS * DN_DV) ** -0.5),
        "w_o": nrm(ks[17], (DEPTH, D_MODEL, D_MODEL), D_MODEL ** -0.5),
        "ln2_g": 1.0 + nrm(ks[18], (DEPTH, D_MODEL), 0.02),
        "w_rg": nrm(ks[19], (DEPTH, D_MODEL, N_EXPERT_GROUPS), D_MODEL ** -0.5),
        "b_rg": nrm(ks[20], (DEPTH, N_EXPERT_GROUPS), 0.01),
        "w_re": nrm(ks[21], (DEPTH, D_MODEL, N_EXPERTS), D_MODEL ** -0.5),
        "b_re": nrm(ks[22], (DEPTH, N_EXPERTS), 0.01),
        "w_e_gate": nrm(ks[23], (DEPTH, N_EXPERTS, D_MODEL, D_EXPERT), D_MODEL ** -0.5),
        "w_e_up": nrm(ks[24], (DEPTH, N_EXPERTS, D_MODEL, D_EXPERT), D_MODEL ** -0.5),
        "w_e_down": nrm(ks[25], (DEPTH, N_EXPERTS, D_EXPERT, D_MODEL), D_EXPERT ** -0.5),
    }


def reference(x_prompt, x_sample, cache_swa0_kv, cache_swa1_kv, cache_swa2_kv, state_dn_conv, state_dn_S,
              ln1_g, w_in, q_norm_g, k_norm_g, dn_conv_w, dn_a_log, dn_dt_bias, dn_norm_g, w_out_a, w_out_b,
              w_o, ln2_g, w_rg, b_rg, w_re, b_re, w_e_gate, w_e_up, w_e_down):
    yp, ys = x_prompt, x_sample
    p_sw0, p_sw1, p_sw2, p_cv, p_S = [], [], [], [], []
    s_sw0, s_sw1, s_sw2, s_cv, s_S = [], [], [], [], []
    for l in range(DEPTH):
        lw = (ln1_g[l], w_in[l], q_norm_g[l], k_norm_g[l], dn_conv_w[l], dn_a_log[l], dn_dt_bias[l],
              dn_norm_g[l], w_out_a[l], w_out_b[l], w_o[l], ln2_g[l], w_rg[l], b_rg[l], w_re[l], b_re[l],
              w_e_gate[l], w_e_up[l], w_e_down[l])
        yp, pw, pc, pS = _layer(yp, None, None, None, *lw)
        ys, sw, sc, sS = _layer(ys, (cache_swa0_kv[l], cache_swa1_kv[l], cache_swa2_kv[l]),
                                state_dn_conv[l], state_dn_S[l], *lw)
        p_sw0.append(pw[0]); p_sw1.append(pw[1]); p_sw2.append(pw[2]); p_cv.append(pc); p_S.append(pS)
        s_sw0.append(sw[0]); s_sw1.append(sw[1]); s_sw2.append(sw[2]); s_cv.append(sc); s_S.append(sS)
    return (yp, ys,
            jnp.stack(p_sw0), jnp.stack(p_sw1), jnp.stack(p_sw2), jnp.stack(p_cv), jnp.stack(p_S),
            jnp.stack(s_sw0), jnp.stack(s_sw1), jnp.stack(s_sw2), jnp.stack(s_cv), jnp.stack(s_S))
```

```python
import functools

import jax
import jax.numpy as jnp
from jax import lax
from jax.experimental import pallas as pl
from jax.experimental.pallas import tpu as pltpu

F32 = jnp.float32
BF16 = jnp.bfloat16
HI = lax.Precision.HIGHEST
EPS = 1e-6

SWA_CONFIGS = ((128, 1), (512, 4), (2048, 16))
SWA_HEADS = 8
SWA_DIM = 64
SWA_GW = SWA_HEADS * SWA_DIM
SWA_SPAN = 128
DN_HEADS = 8
DN_DK = 128
DN_CONV = 4
DN_CHUNK = 64
N_GROUPS = 4
PER_GROUP = 8
N_EXPERTS = N_GROUPS * PER_GROUP
TOP_K = 2

VMEM_LIMIT_BYTES = 56 * 1024 * 1024
LANES = 128
MXU = 256
MOE_ROWS = 256


def _cparams(n_axes):
    return pltpu.CompilerParams(
        dimension_semantics=("arbitrary",) * n_axes, vmem_limit_bytes=VMEM_LIMIT_BYTES
    )


def _rms(x, g):
    return x * lax.rsqrt(jnp.mean(x * x, axis=-1, keepdims=True) + EPS) * g


def _bdot(a, b):
    return jnp.dot(a.astype(BF16), b.astype(BF16), preferred_element_type=F32)


def _hdot(a, b):
    return jnp.dot(a, b, preferred_element_type=F32, precision=HI)


def _sigmoid(x):
    return 1.0 / (1.0 + jnp.exp(-x))


def _silu(x):
    return x * _sigmoid(x)


def _softplus(x):
    return jnp.maximum(x, 0.0) + jnp.log1p(jnp.exp(-jnp.abs(x)))


def _proj_attn_kernel(x_ref, lng_ref, w_ref, ng_ref, bd_ref, p0, p1, p2, t0, t1, t2, h_scr, z_scr,
                      *, tm, dils, tail_rows):
    j = pl.program_id(2)

    @pl.when(j == 0)
    def _():
        h_scr[...] = _rms(x_ref[0], lng_ref[...]).astype(BF16)

    z = jnp.dot(h_scr[...], w_ref[...], preferred_element_type=F32)
    n_cb = z_scr.shape[0]
    for c in range(n_cb):
        z_scr[c] = z[:, c * LANES:(c + 1) * LANES]

    @pl.when(j < 2)
    def _():
        for c in range(0, n_cb, 2):
            zc = jnp.concatenate([z_scr[c], z_scr[c + 1]], axis=-1)
            ss = jnp.dot((zc * zc).astype(BF16), bd_ref[...], preferred_element_type=F32)
            zn = zc * lax.rsqrt(ss * (1.0 / SWA_DIM) + EPS) * ng_ref[0, :, c * LANES:(c + 2) * LANES]
            z_scr[c] = zn[:, :LANES]
            z_scr[c + 1] = zn[:, LANES:]

    outs = (p0, p1, p2)
    tails = (t0, t1, t2)
    per_g = SWA_GW // LANES
    for sec in range(3):

        @pl.when(j == sec)
        def _(sec=sec):
            for gi, d in enumerate(dils):
                for cb in range(per_g):
                    c = gi * per_g + cb
                    col = sec * SWA_GW + cb * LANES
                    for r in range(d):
                        src = z_scr[c] if d == 1 else z_scr[c, pl.ds(r, tm // d, stride=d), :]
                        outs[gi][0, r, :, col:col + LANES] = src.astype(BF16)
                    if sec >= 1:
                        rows = tail_rows[gi]
                        tails[gi][0, :, col - SWA_GW:col - SWA_GW + LANES] = z_scr[c, tm - rows:tm, :]


def _proj_attn(x, ln_g, w_att, ng, bd, *, tm):
    B, L, D = x.shape
    nt = L // tm
    dils = tuple(d for _, d in SWA_CONFIGS)
    keeps = tuple(min(w, L) for w, _ in SWA_CONFIGS)
    tail_rows = tuple(min(k, tm) for k in keeps)
    for k, r in zip(keeps, tail_rows):
        assert k % r == 0 and L % tm == 0
    W3 = 3 * SWA_GW

    def tail_spec(keep, rows):
        nblk = keep // rows
        return pl.BlockSpec((1, rows, 2 * SWA_GW), lambda b, i, j: (b, jnp.maximum(i - (nt - nblk), 0), 0))

    out_shape = [jax.ShapeDtypeStruct((B, d, L // d, W3), BF16) for d in dils]
    out_shape += [jax.ShapeDtypeStruct((B, k, 2 * SWA_GW), F32) for k in keeps]
    out_specs = [pl.BlockSpec((1, d, tm // d, W3), lambda b, i, j: (b, 0, i, 0)) for d in dils]
    out_specs += [tail_spec(k, r) for k, r in zip(keeps, tail_rows)]
    return pl.pallas_call(
        functools.partial(_proj_attn_kernel, tm=tm, dils=dils, tail_rows=tail_rows),
        grid=(B, nt, 3),
        in_specs=[
            pl.BlockSpec((1, tm, D), lambda b, i, j: (b, i, 0)),
            pl.BlockSpec((1, D), lambda b, i, j: (0, 0)),
            pl.BlockSpec((D, W3), lambda b, i, j: (0, j)),
            pl.BlockSpec((1, 1, W3), lambda b, i, j: (jnp.minimum(j, 1), 0, 0)),
            pl.BlockSpec((MXU, MXU), lambda b, i, j: (0, 0)),
        ],
        out_specs=out_specs,
        out_shape=out_shape,
        scratch_shapes=[pltpu.VMEM((tm, D), BF16), pltpu.VMEM((W3 // LANES, tm, LANES), F32)],
        compiler_params=_cparams(3),
        name="proj_attn",
    )(x, ln_g, w_att, ng, bd)


def _proj_plain_kernel(x_ref, lng_ref, w_ref, o_ref, h_scr):
    @pl.when(pl.program_id(1) == 0)
    def _():
        h_scr[...] = _rms(x_ref[...], lng_ref[...]).astype(BF16)

    o_ref[...] = jnp.dot(h_scr[...], w_ref[...], preferred_element_type=F32).astype(o_ref.dtype)


def _proj_plain(x2d, ln_g, w, *, tm, tn, out_dtype, name="proj_plain"):
    N, D = x2d.shape
    C = w.shape[1]
    assert N % tm == 0 and C % tn == 0
    return pl.pallas_call(
        _proj_plain_kernel,
        grid=(N // tm, C // tn),
        in_specs=[
            pl.BlockSpec((tm, D), lambda i, j: (i, 0)),
            pl.BlockSpec((1, D), lambda i, j: (0, 0)),
            pl.BlockSpec((D, tn), lambda i, j: (0, j)),
        ],
        out_specs=pl.BlockSpec((tm, tn), lambda i, j: (i, j)),
        out_shape=jax.ShapeDtypeStruct((N, C), out_dtype),
        scratch_shapes=[pltpu.VMEM((tm, D), BF16)],
        compiler_params=_cparams(2),
        name=name,
    )(x2d, ln_g, w)


def _attn_kernel(q_ref, kc_ref, vc_ref, kp_ref, vp_ref, o_ref, lse_ref, kk_scr, vv_scr, *, tq):
    i = pl.program_id(1)
    blk = SWA_SPAN
    kk_scr[0:blk, :] = kp_ref[0]
    kk_scr[blk:blk + tq, :] = kc_ref[0]
    vv_scr[0:blk, :] = vp_ref[0]
    vv_scr[blk:blk + tq, :] = vc_ref[0]
    qi = lax.broadcasted_iota(jnp.int32, (blk, 2 * blk), 0)
    ki = lax.broadcasted_iota(jnp.int32, (blk, 2 * blk), 1)
    dist = blk + qi - ki
    band = (dist >= 0) & (dist <= SWA_SPAN)
    band_first = band & ((ki >= blk) | (i > 0))
    lo = lax.broadcasted_iota(jnp.int32, (blk, LANES), 1) < SWA_DIM
    zero = jnp.zeros((blk, LANES), BF16)
    for jb in range(tq // blk):
        mask = band_first if jb == 0 else band
        rows = slice(jb * blk, (jb + 1) * blk)
        for hp in range(SWA_GW // LANES):
            cs = slice(hp * LANES, (hp + 1) * LANES)
            qb = q_ref[0, rows, cs]
            kk = kk_scr[jb * blk:(jb + 2) * blk, cs]
            vv = vv_scr[jb * blk:(jb + 2) * blk, cs]
            res_o, res_l = [], []
            for hh in range(2):
                qm = jnp.where(lo if hh == 0 else jnp.logical_not(lo), qb, zero)
                s = lax.dot_general(qm, kk, (((1,), (1,)), ((), ())), preferred_element_type=F32)
                s = jnp.where(mask, s * (SWA_DIM ** -0.5), -jnp.inf)
                m = jnp.max(s, axis=-1, keepdims=True)
                p = jnp.exp(s - m)
                den = jnp.sum(p, axis=-1, keepdims=True)
                pv = jnp.dot(p.astype(BF16), vv, preferred_element_type=F32)
                res_o.append(pv / den)
                res_l.append(jnp.broadcast_to(m + jnp.log(den), (blk, LANES)))
            o_ref[0, rows, cs] = jnp.where(lo, res_o[0], res_o[1]).astype(BF16)
            lse_ref[0, rows, cs] = jnp.where(lo, res_l[0], res_l[1])


def _attn(p, *, tq):
    S, M, _ = p.shape
    assert M % tq == 0 and tq % SWA_SPAN == 0
    nb = tq // SWA_SPAN
    return pl.pallas_call(
        functools.partial(_attn_kernel, tq=tq),
        grid=(S, M // tq),
        in_specs=[
            pl.BlockSpec((1, tq, SWA_GW), lambda s, i: (s, i, 0)),
            pl.BlockSpec((1, tq, SWA_GW), lambda s, i: (s, i, 1)),
            pl.BlockSpec((1, tq, SWA_GW), lambda s, i: (s, i, 2)),
            pl.BlockSpec((1, SWA_SPAN, SWA_GW), lambda s, i: (s, jnp.maximum(i * nb - 1, 0), 1)),
            pl.BlockSpec((1, SWA_SPAN, SWA_GW), lambda s, i: (s, jnp.maximum(i * nb - 1, 0), 2)),
        ],
        out_specs=[
            pl.BlockSpec((1, tq, SWA_GW), lambda s, i: (s, i, 0)),
            pl.BlockSpec((1, tq, SWA_GW), lambda s, i: (s, i, 0)),
        ],
        out_shape=[
            jax.ShapeDtypeStruct((S, M, SWA_GW), BF16),
            jax.ShapeDtypeStruct((S, M, SWA_GW), F32),
        ],
        scratch_shapes=[
            pltpu.VMEM((SWA_SPAN + tq, SWA_GW), BF16),
            pltpu.VMEM((SWA_SPAN + tq, SWA_GW), BF16),
        ],
        compiler_params=_cparams(2),
        name="swa_attn",
    )(p, p, p, p, p)


def _conv_kernel(x_ref, xp_ref, cp_ref, cw_ref, ba_ref, par_ref, q_ref, k_ref, v_ref, g_ref, xs_scr,
                 *, tl):
    i = pl.program_id(1)
    xs_scr[0:8, :] = jnp.where(i == 0, cp_ref[0], xp_ref[0])
    xs_scr[8:8 + tl, :] = x_ref[0]
    nh = DN_HEADS
    outs = (q_ref, k_ref, v_ref)
    for cb in range(3 * nh):
        cs = slice(cb * LANES, (cb + 1) * LANES)
        acc = cw_ref[0:1, cs] * xs_scr[5:5 + tl, cs]
        for t in range(1, DN_CONV):
            acc = acc + cw_ref[t:t + 1, cs] * xs_scr[5 + t:5 + t + tl, cs]
        act = _silu(acc)
        part, h = divmod(cb, nh)
        if part < 2:
            act = act * lax.rsqrt(jnp.sum(act * act, axis=-1, keepdims=True) + EPS)
        if part == 0:
            act = act * (DN_DK ** -0.5)
        outs[part][0, :, h * LANES:(h + 1) * LANES] = act.astype(BF16)
    ba = ba_ref[0]
    lane = lax.broadcasted_iota(jnp.int32, (tl, LANES), 1)
    g = par_ref[0:1, :] * _softplus(ba + par_ref[1:2, :])
    ri = lax.broadcasted_iota(jnp.int32, (tl, tl), 0)
    ci = lax.broadcasted_iota(jnp.int32, (tl, tl), 1)
    tri = jnp.where((ri // DN_CHUNK == ci // DN_CHUNK) & (ci <= ri), 1.0, 0.0).astype(F32)
    gc = _hdot(tri, g)
    g_ref[0] = jnp.where(lane < nh, _sigmoid(ba), gc)


def _dn_conv(raw, conv_prev8, conv_w, ba, par, *, tl):
    B, L, C = raw.shape
    assert L % tl == 0 and tl % DN_CHUNK == 0
    width = DN_HEADS * DN_DK
    return pl.pallas_call(
        functools.partial(_conv_kernel, tl=tl),
        grid=(B, L // tl),
        in_specs=[
            pl.BlockSpec((1, tl, C), lambda b, i: (b, i, 0)),
            pl.BlockSpec((1, 8, C), lambda b, i: (b, jnp.maximum(i * (tl // 8) - 1, 0), 0)),
            pl.BlockSpec((1, 8, C), lambda b, i: (b, 0, 0)),
            pl.BlockSpec((DN_CONV, C), lambda b, i: (0, 0)),
            pl.BlockSpec((1, tl, LANES), lambda b, i: (b, i, 0)),
            pl.BlockSpec((2, LANES), lambda b, i: (0, 0)),
        ],
        out_specs=[pl.BlockSpec((1, tl, width), lambda b, i: (b, i, 0))] * 3
        + [pl.BlockSpec((1, tl, LANES), lambda b, i: (b, i, 0))],
        out_shape=[jax.ShapeDtypeStruct((B, L, width), BF16)] * 3
        + [jax.ShapeDtypeStruct((B, L, LANES), F32)],
        scratch_shapes=[pltpu.VMEM((8 + tl, C), F32)],
        compiler_params=_cparams(2),
        name="dn_conv",
    )(raw, raw, conv_prev8, conv_w, ba, par)


def _intra_kernel(q_ref, k_ref, v_ref, g_ref, u_ref, w_ref, qd_ref, kd_ref, a_ref, gt_ref, *, tl):
    h = pl.program_id(1)
    C = DN_CHUNK
    lane = lax.broadcasted_iota(jnp.int32, (C, LANES), 1)
    ri = lax.broadcasted_iota(jnp.int32, (C, C), 0)
    ci = lax.broadcasted_iota(jnp.int32, (C, C), 1)
    eye = jnp.where(ri == ci, 1.0, 0.0).astype(F32)
    for c in range(tl // C):
        rows = slice(c * C, (c + 1) * C)
        gv = g_ref[0, rows, :]
        beta = jnp.sum(jnp.where(lane == h, gv, 0.0), axis=-1, keepdims=True)
        gc = jnp.sum(jnp.where(lane == h + DN_HEADS, gv, 0.0), axis=-1, keepdims=True)
        q = q_ref[0, rows, :].astype(F32)
        k = k_ref[0, rows, :].astype(F32)
        v = v_ref[0, rows, :].astype(F32)
        lhs = jnp.where(lane == 0, gc, jnp.where(lane == 1, 1.0, 0.0))
        rhs = jnp.where(lane == 0, 1.0, jnp.where(lane == 1, -gc, 0.0))
        diff = lax.dot_general(lhs, rhs, (((1,), (1,)), ((), ())), preferred_element_type=F32, precision=HI)
        decay = jnp.exp(jnp.where(ri >= ci, diff, -jnp.inf))
        kb = k * beta
        kk = lax.dot_general(kb.astype(BF16), k.astype(BF16), (((1,), (1,)), ((), ())), preferred_element_type=F32)
        low = jnp.where(ri > ci, kk * decay, 0.0)
        x = -low
        t = eye + x
        for _ in range(5):
            x = _bdot(x, x)
            t = t + _bdot(t, x)
        eg = jnp.exp(gc)
        glast = gc[C - 1:C, :]
        u_ref[0, 0, rows, :] = _bdot(t, v * beta)
        w_ref[0, 0, rows, :] = _bdot(t, kb * eg).astype(BF16)
        qk = lax.dot_general(q.astype(BF16), k.astype(BF16), (((1,), (1,)), ((), ())), preferred_element_type=F32)
        a_ref[0, 0, rows, :] = (qk * decay).astype(BF16)
        qd_ref[0, 0, rows, :] = (q * eg).astype(BF16)
        kd_ref[0, 0, rows, :] = (k * jnp.exp(glast - gc)).astype(BF16)
        gt_ref[0, 0, c:c + 1, :] = jnp.broadcast_to(jnp.exp(glast), (1, LANES))


def _dn_intra(q, k, v, g, *, tl):
    B, L, _ = q.shape
    H, C = DN_HEADS, DN_CHUNK
    assert L % tl == 0 and (tl // C) % 8 == 0
    qkv_spec = pl.BlockSpec((1, tl, LANES), lambda b, h, i: (b, i, h))
    hl = lambda w: pl.BlockSpec((1, 1, tl, w), lambda b, h, i: (b, h, i, 0))
    return pl.pallas_call(
        functools.partial(_intra_kernel, tl=tl),
        grid=(B, H, L // tl),
        in_specs=[qkv_spec, qkv_spec, qkv_spec, pl.BlockSpec((1, tl, LANES), lambda b, h, i: (b, i, 0))],
        out_specs=[hl(LANES), hl(LANES), hl(LANES), hl(LANES), hl(C),
                   pl.BlockSpec((1, 1, tl // C, LANES), lambda b, h, i: (b, h, i, 0))],
        out_shape=[
            jax.ShapeDtypeStruct((B, H, L, LANES), F32),
            jax.ShapeDtypeStruct((B, H, L, LANES), BF16),
            jax.ShapeDtypeStruct((B, H, L, LANES), BF16),
            jax.ShapeDtypeStruct((B, H, L, LANES), BF16),
            jax.ShapeDtypeStruct((B, H, L, C), BF16),
            jax.ShapeDtypeStruct((B, H, L // C, LANES), F32),
        ],
        compiler_params=_cparams(3),
        name="dn_intra",
    )(q, k, v, g)


def _scan_kernel(u_ref, w_ref, qd_ref, kd_ref, a_ref, gt_ref, s0_ref, o_ref, s_ref, *, n_chunks, hb):
    C = DN_CHUNK

    def body(c, states):
        rows = pl.ds(pl.multiple_of(c * C, C), C)
        new = []
        for hh in range(hb):
            S = states[hh]
            Sb = S.astype(BF16)
            v_new = u_ref[0, hh, rows, :] - jnp.dot(w_ref[0, hh, rows, :], Sb, preferred_element_type=F32)
            vb = v_new.astype(BF16)
            o = jnp.dot(qd_ref[0, hh, rows, :], Sb, preferred_element_type=F32)
            o = o + jnp.dot(a_ref[0, hh, rows, :], vb, preferred_element_type=F32)
            o_ref[0, rows, hh * LANES:(hh + 1) * LANES] = o
            upd = lax.dot_general(kd_ref[0, hh, rows, :], vb, (((0,), (0,)), ((), ())), preferred_element_type=F32)
            new.append(S * gt_ref[0, hh, pl.ds(c, 1), :] + upd)
        return tuple(new)

    final = lax.fori_loop(0, n_chunks, body, tuple(s0_ref[0, hh] for hh in range(hb)))
    for hh in range(hb):
        s_ref[0, hh] = final[hh]


def _dn_scan(u, w, qd, kd, a, gt, s0, *, hb=2):
    B, H, L, _ = u.shape
    C = DN_CHUNK
    hs = lambda wd: pl.BlockSpec((1, hb, L, wd), lambda b, h: (b, h, 0, 0))
    return pl.pallas_call(
        functools.partial(_scan_kernel, n_chunks=L // C, hb=hb),
        grid=(B, H // hb),
        in_specs=[hs(LANES), hs(LANES), hs(LANES), hs(LANES), hs(C),
                  pl.BlockSpec((1, hb, L // C, LANES), lambda b, h: (b, h, 0, 0)),
                  pl.BlockSpec((1, hb, DN_DK, LANES), lambda b, h: (b, h, 0, 0))],
        out_specs=[pl.BlockSpec((1, L, hb * LANES), lambda b, h: (b, 0, h)),
                   pl.BlockSpec((1, hb, DN_DK, LANES), lambda b, h: (b, h, 0, 0))],
        out_shape=[jax.ShapeDtypeStruct((B, L, H * LANES), F32),
                   jax.ShapeDtypeStruct((B, H, DN_DK, LANES), F32)],
        compiler_params=_cparams(2),
        name="dn_scan",
    )(u, w, qd, kd, a, gt, s0)


def _gated_mix(o_a, od, gates, dng, wa, wb, wo, x, dot):
    width = DN_HEADS * DN_DK
    parts = []
    for h in range(DN_HEADS):
        blk = od[:, h * LANES:(h + 1) * LANES]
        parts.append(blk * lax.rsqrt(jnp.mean(blk * blk, axis=-1, keepdims=True) + EPS) * dng)
    odn = jnp.concatenate(parts, axis=-1) * _silu(gates[:, 0:width].astype(F32))
    ya = dot(o_a, wa)
    yb = dot(odn, wb)
    mix = _sigmoid(gates[:, width:2 * width].astype(F32)) * ya + _sigmoid(gates[:, 2 * width:].astype(F32)) * yb
    return x + dot(mix, wo)


def _out_kernel(x_ref, o0, o1, o2, l0, l1, l2, od_ref, gates_ref, dng_ref, wa_ref, wb_ref, wo_ref, y_ref,
                so0, so1, so2, sl0, sl1, sl2, *, tm, dils):
    o_refs, l_refs = (o0, o1, o2), (l0, l1, l2)
    so, sl = (so0, so1, so2), (sl0, sl1, sl2)
    parts = []
    for cb in range(SWA_GW // LANES):
        cs = slice(cb * LANES, (cb + 1) * LANES)
        for gi, d in enumerate(dils):
            for r in range(d):
                dst = slice(None) if d == 1 else pl.ds(r, tm // d, stride=d)
                so[gi][cb, dst, :] = o_refs[gi][0, r, :, cs].astype(F32)
                sl[gi][cb, dst, :] = l_refs[gi][0, r, :, cs]
        ls = [s[cb] for s in sl]
        m = jnp.maximum(jnp.maximum(ls[0], ls[1]), ls[2])
        es = [jnp.exp(l - m) for l in ls]
        parts.append((es[0] * so[0][cb] + es[1] * so[1][cb] + es[2] * so[2][cb]) / (es[0] + es[1] + es[2]))
    o_a = jnp.concatenate(parts, axis=-1)
    y_ref[...] = _gated_mix(o_a, od_ref[...], gates_ref[...], dng_ref[...], wa_ref[...], wb_ref[...],
                            wo_ref[...], x_ref[...], _bdot)


def _out_proj(x2d, os_, ls_, od2d, gates, dng, wa, wb, wo, *, B, L, tm):
    N, D = x2d.shape
    nt = L // tm
    dils = tuple(d for _, d in SWA_CONFIGS)
    grp = lambda d: pl.BlockSpec((1, d, tm // d, SWA_GW), lambda i: (i // nt, 0, i % nt, 0))
    row = lambda w: pl.BlockSpec((tm, w), lambda i: (i, 0))
    full = lambda a: pl.BlockSpec(a.shape, lambda i: (0, 0))
    return pl.pallas_call(
        functools.partial(_out_kernel, tm=tm, dils=dils),
        grid=(N // tm,),
        in_specs=[row(D)] + [grp(d) for d in dils] * 2 + [row(od2d.shape[1]), row(gates.shape[1]),
                                                          full(dng), full(wa), full(wb), full(wo)],
        out_specs=row(D),
        out_shape=jax.ShapeDtypeStruct((N, D), F32),
        scratch_shapes=[pltpu.VMEM((SWA_GW // LANES, tm, LANES), F32)] * 6,
        compiler_params=_cparams(1),
        name="out_proj",
    )(x2d, *os_, *ls_, od2d, gates, dng, wa, wb, wo)


def _router_kernel(x_ref, lng_ref, wr_ref, br_ref, info_ref, cnt_ref, base_scr, *, tm):
    i = pl.program_id(0)

    @pl.when(i == 0)
    def _():
        base_scr[...] = jnp.zeros_like(base_scr)

    h = _rms(x_ref[...], lng_ref[...])
    lg = _bdot(h, wr_ref[...]) + br_ref[...]
    lane = lax.broadcasted_iota(jnp.int32, (tm, LANES), 1)
    big = jnp.int32(1 << 20)
    ninf = -jnp.inf

    def argmax_lane(vals):
        mx = jnp.max(vals, axis=-1, keepdims=True)
        idx = jnp.min(jnp.where(vals == mx, lane, big), axis=-1, keepdims=True)
        return mx, idx

    lgm = jnp.where(lane < N_GROUPS, lg, ninf)
    mg, gsel = argmax_lane(lgm)
    pg = 1.0 / jnp.sum(jnp.exp(lgm - mg), axis=-1, keepdims=True)
    start = N_GROUPS + gsel * PER_GROUP
    le = jnp.where((lane >= start) & (lane < start + PER_GROUP), lg, ninf)
    m1, i1 = argmax_lane(le)
    m2, i2 = argmax_lane(jnp.where(lane == i1, ninf, le))
    e21 = jnp.exp(m2 - m1)
    w1 = pg / (1.0 + e21)
    w2 = pg * e21 / (1.0 + e21)
    oh = jnp.where(lane == i1, 1.0, 0.0) + jnp.where(lane == i2, 1.0, 0.0)
    ri = lax.broadcasted_iota(jnp.int32, (tm, tm), 0)
    ci = lax.broadcasted_iota(jnp.int32, (tm, tm), 1)
    strict = jnp.where(ci < ri, 1.0, 0.0).astype(BF16)
    pref = jnp.dot(strict, oh.astype(BF16), preferred_element_type=F32) + base_scr[...]
    r1 = jnp.sum(jnp.where(lane == i1, pref, 0.0), axis=-1, keepdims=True)
    r2 = jnp.sum(jnp.where(lane == i2, pref, 0.0), axis=-1, keepdims=True)
    base_scr[...] = base_scr[...] + jnp.sum(oh, axis=0, keepdims=True)
    cnt_ref[...] = base_scr[...]
    off = jnp.float32(N_GROUPS)
    info = jnp.where(lane == 0, i1.astype(F32) - off, 0.0)
    info = jnp.where(lane == 1, i2.astype(F32) - off, info)
    info = jnp.where(lane == 2, w1, info)
    info = jnp.where(lane == 3, w2, info)
    info = jnp.where(lane == 4, r1, info)
    info = jnp.where(lane == 5, r2, info)
    info_ref[...] = info


def _router(x2d, ln_g, wr, br, *, tm):
    N, D = x2d.shape
    assert N % tm == 0
    return pl.pallas_call(
        functools.partial(_router_kernel, tm=tm),
        grid=(N // tm,),
        in_specs=[
            pl.BlockSpec((tm, D), lambda i: (i, 0)),
            pl.BlockSpec((1, D), lambda i: (0, 0)),
            pl.BlockSpec((D, LANES), lambda i: (0, 0)),
            pl.BlockSpec((1, LANES), lambda i: (0, 0)),
        ],
        out_specs=[pl.BlockSpec((tm, LANES), lambda i: (i, 0)), pl.BlockSpec((1, LANES), lambda i: (0, 0))],
        out_shape=[jax.ShapeDtypeStruct((N, LANES), F32), jax.ShapeDtypeStruct((1, LANES), F32)],
        scratch_shapes=[pltpu.VMEM((1, LANES), F32)],
        compiler_params=_cparams(1),
        name="router",
    )(x2d, ln_g, wr, br)


def _dispatch_kernel(dest_ref, zb_ref, x_ref, xs_ref, zero_scr, sem, *, tm, tb, n_zb):
    i = pl.program_id(0)

    @pl.when(i == 0)
    def _():
        zero_scr[...] = jnp.zeros_like(zero_scr)

        def zero_copy(n):
            return pltpu.make_async_copy(zero_scr, xs_ref.at[pl.ds(zb_ref[n] * tb, tb), :], sem.at[1])

        def zero_issue(n, carry):
            @pl.when(zb_ref[n] >= 0)
            def _():
                zero_copy(n).start()

            return carry

        def zero_wait(n, carry):
            @pl.when(zb_ref[n] >= 0)
            def _():
                zero_copy(n).wait()

            return carry

        lax.fori_loop(0, n_zb, zero_issue, 0)
        lax.fori_loop(0, n_zb, zero_wait, 0)

    def row_copy(t, slot):
        return pltpu.make_async_copy(
            x_ref.at[pl.ds(t, 1), :], xs_ref.at[pl.ds(dest_ref[(i * tm + t) * TOP_K + slot], 1), :], sem.at[0])

    def issue(t, carry):
        for slot in range(TOP_K):
            row_copy(t, slot).start()
        return carry

    def drain(t, carry):
        for slot in range(TOP_K):
            row_copy(t, slot).wait()
        return carry

    lax.fori_loop(0, tm, issue, 0)
    lax.fori_loop(0, tm, drain, 0)


def _dispatch(dest, zero_blocks, x2d, *, tm, tb, n_rows):
    N, D = x2d.shape
    return pl.pallas_call(
        functools.partial(_dispatch_kernel, tm=tm, tb=tb, n_zb=zero_blocks.shape[0]),
        grid_spec=pltpu.PrefetchScalarGridSpec(
            num_scalar_prefetch=2,
            grid=(N // tm,),
            in_specs=[pl.BlockSpec((tm, D), lambda i, d, z: (i, 0))],
            out_specs=pl.BlockSpec(memory_space=pl.ANY),
            scratch_shapes=[pltpu.VMEM((tb, D), F32), pltpu.SemaphoreType.DMA((2,))],
        ),
        out_shape=jax.ShapeDtypeStruct((n_rows, D), F32),
        compiler_params=_cparams(1),
        name="moe_dispatch",
    )(dest, zero_blocks, x2d)


def _ffn_kernel(be_ref, nb_ref, xs_ref, lng_ref, wg_ref, wu_ref, wd_ref, y_ref):
    used = pl.program_id(0) < nb_ref[0]

    @pl.when(used)
    def _():
        h = _rms(xs_ref[...], lng_ref[...]).astype(BF16)
        g = jnp.dot(h, wg_ref[0], preferred_element_type=F32)
        u = jnp.dot(h, wu_ref[0], preferred_element_type=F32)
        y_ref[...] = jnp.dot((_silu(g) * u).astype(BF16), wd_ref[0], preferred_element_type=F32)

    @pl.when(jnp.logical_not(used))
    def _():
        y_ref[...] = jnp.zeros_like(y_ref)


def _ffn(blk_e, nb_used, xs, ln_g, wg, wu, wd, *, tb):
    P, D = xs.shape
    nb = P // tb
    DE = wg.shape[2]
    rowi = lambda i, be, nbu: (jnp.minimum(i, nbu[0] - 1), 0)
    return pl.pallas_call(
        _ffn_kernel,
        grid_spec=pltpu.PrefetchScalarGridSpec(
            num_scalar_prefetch=2,
            grid=(nb,),
            in_specs=[
                pl.BlockSpec((tb, D), rowi),
                pl.BlockSpec((1, D), lambda i, be, nbu: (0, 0)),
                pl.BlockSpec((1, D, DE), lambda i, be, nbu: (be[i], 0, 0)),
                pl.BlockSpec((1, D, DE), lambda i, be, nbu: (be[i], 0, 0)),
                pl.BlockSpec((1, DE, D), lambda i, be, nbu: (be[i], 0, 0)),
            ],
            out_specs=pl.BlockSpec((tb, D), lambda i, be, nbu: (i, 0)),
        ),
        out_shape=jax.ShapeDtypeStruct((P, D), F32),
        compiler_params=_cparams(1),
        name="moe_ffn",
    )(blk_e, nb_used, xs, ln_g, wg, wu, wd)


def _combine_kernel(dest_ref, x_ref, info_ref, yb_ref, y_ref, g_scr, sem, *, tm):
    i = pl.program_id(0)

    def row_copy(t, slot):
        return pltpu.make_async_copy(
            yb_ref.at[pl.ds(dest_ref[(i * tm + t) * TOP_K + slot], 1), :], g_scr.at[slot, pl.ds(t, 1), :], sem.at[0])

    def issue(t, carry):
        for slot in range(TOP_K):
            row_copy(t, slot).start()
        return carry

    def drain(t, carry):
        for slot in range(TOP_K):
            row_copy(t, slot).wait()
        return carry

    lax.fori_loop(0, tm, issue, 0)
    lax.fori_loop(0, tm, drain, 0)
    info = info_ref[...]
    lane = lax.broadcasted_iota(jnp.int32, info.shape, 1)
    w1 = jnp.sum(jnp.where(lane == 2, info, 0.0), axis=-1, keepdims=True)
    w2 = jnp.sum(jnp.where(lane == 3, info, 0.0), axis=-1, keepdims=True)
    y_ref[...] = x_ref[...] + (w1 * g_scr[0] + w2 * g_scr[1])


def _combine(dest, x2d, info, yb, *, tm):
    N, D = x2d.shape
    return pl.pallas_call(
        functools.partial(_combine_kernel, tm=tm),
        grid_spec=pltpu.PrefetchScalarGridSpec(
            num_scalar_prefetch=1,
            grid=(N // tm,),
            in_specs=[
                pl.BlockSpec((tm, D), lambda i, d: (i, 0)),
                pl.BlockSpec((tm, LANES), lambda i, d: (i, 0)),
                pl.BlockSpec(memory_space=pl.ANY),
            ],
            out_specs=pl.BlockSpec((tm, D), lambda i, d: (i, 0)),
            scratch_shapes=[pltpu.VMEM((TOP_K, tm, D), F32), pltpu.SemaphoreType.DMA((1,))],
        ),
        out_shape=jax.ShapeDtypeStruct((N, D), F32),
        compiler_params=_cparams(1),
        name="moe_combine",
    )(dest, x2d, info, yb)


def _moe(x2d, ln2_g, wr, br, wg, wu, wd, *, tm):
    N, D = x2d.shape
    tb = MOE_ROWS
    info, counts = _router(x2d, ln2_g, wr, br, tm=tm)
    counts = counts[0, N_GROUPS:N_GROUPS + N_EXPERTS].astype(jnp.int32)
    pcounts = (counts + tb - 1) // tb * tb
    pend = jnp.cumsum(pcounts)
    pstart = pend - pcounts
    e = info[:, 0:TOP_K].astype(jnp.int32)
    rank = info[:, 4:4 + TOP_K].astype(jnp.int32)
    dest = (pstart[e] + rank).reshape(-1)
    nb = -(-(N * TOP_K) // tb) + N_EXPERTS
    P = nb * tb
    blk_e = jnp.minimum(jnp.searchsorted(pend, jnp.arange(nb, dtype=jnp.int32) * tb, side="right"),
                        N_EXPERTS - 1).astype(jnp.int32)
    nb_used = (pend[-1] // tb).astype(jnp.int32).reshape(1)
    blocks = jnp.arange(nb, dtype=jnp.int32)
    zero_blocks = jnp.concatenate([jnp.where(counts % tb != 0, pend // tb - 1, -1),
                                   jnp.where(blocks >= nb_used[0], blocks, -1)]).astype(jnp.int32)
    xs = _dispatch(dest, zero_blocks, x2d, tm=tm, tb=tb, n_rows=P)
    yb = _ffn(blk_e, nb_used, xs, ln2_g, wg, wu, wd, tb=tb)
    return _combine(dest, x2d, info, yb, tm=tm)


def _rows8(x):
    return jnp.broadcast_to(x, (8, x.shape[1]))


def _row_hdot(x, m):
    return _hdot(_rows8(x), m)[0:1]


def _bf_round(x):
    return x.astype(BF16).astype(F32)


def _sample_attn_kernel(z_ref, c0, c1, c2, qg_ref, kg_ref, e_ref, et_ref, oa_ref, kv_ref):
    E, ET = e_ref[...], et_ref[...]
    z = z_ref[0]
    W = SWA_GW
    scale = SWA_DIM ** -0.5

    def headnorm(zz, g):
        ss = _row_hdot(zz * zz, E)
        return zz * _row_hdot(lax.rsqrt(ss * (1.0 / SWA_DIM) + EPS), ET) * g

    outs, lses = [], []
    for gi, c_ref in enumerate((c0, c1, c2)):
        q = headnorm(z[:, gi * W:(gi + 1) * W], qg_ref[gi:gi + 1, :])
        k = headnorm(z[:, 3 * W + gi * W:3 * W + (gi + 1) * W], kg_ref[gi:gi + 1, :])
        v = z[:, 6 * W + gi * W:6 * W + (gi + 1) * W]
        kv_ref[0, :, 2 * gi * W:(2 * gi + 1) * W] = k
        kv_ref[0, :, (2 * gi + 1) * W:(2 * gi + 2) * W] = v
        kc = c_ref[0, :, 0:W]
        vc = c_ref[0, :, W:2 * W]
        qr = _bf_round(q)
        s_c = _hdot(_bf_round(kc) * qr, E) * scale
        s_n = _row_hdot(_bf_round(k) * qr, E) * scale
        m = jnp.maximum(jnp.max(s_c, axis=0, keepdims=True), s_n)
        p_c = jnp.exp(s_c - m)
        p_n = jnp.exp(s_n - m)
        den = jnp.sum(p_c, axis=0, keepdims=True) + p_n
        num = jnp.sum(_hdot(_bf_round(p_c), ET) * _bf_round(vc), axis=0, keepdims=True)
        num = num + _row_hdot(_bf_round(p_n), ET) * _bf_round(v)
        outs.append(num / _row_hdot(den, ET))
        lses.append(m + jnp.log(den))
    mm = jnp.maximum(jnp.maximum(lses[0], lses[1]), lses[2])
    es = [jnp.exp(l - mm) for l in lses]
    tot = es[0] + es[1] + es[2]
    oa_ref[0] = sum(_row_hdot(_bf_round(e / tot), ET) * _bf_round(o) for e, o in zip(es, outs))


def _sample_attn(z3, caches, qg, kg, e_mat, et_mat):
    Bs = z3.shape[0]
    W = SWA_GW
    cviews = []
    for (win, dil), c in zip(SWA_CONFIGS, caches):
        assert c.shape[1] == win
        cviews.append(c.reshape(Bs, win // dil, dil * 2 * W))
    full = lambda a: pl.BlockSpec(a.shape, lambda b: (0,) * a.ndim)
    return pl.pallas_call(
        _sample_attn_kernel,
        grid=(Bs,),
        in_specs=[pl.BlockSpec((1, 1, 9 * W), lambda b: (b, 0, 0))]
        + [pl.BlockSpec((1, SWA_SPAN, 2 * W), lambda b: (b, 0, 0))] * 3
        + [full(qg), full(kg), full(e_mat), full(et_mat)],
        out_specs=[pl.BlockSpec((1, 1, W), lambda b: (b, 0, 0)), pl.BlockSpec((1, 1, 6 * W), lambda b: (b, 0, 0))],
        out_shape=[jax.ShapeDtypeStruct((Bs, 1, W), F32), jax.ShapeDtypeStruct((Bs, 1, 6 * W), F32)],
        compiler_params=_cparams(1),
        name="sample_attn",
    )(z3, *cviews, qg, kg, e_mat, et_mat)


def _sample_dn_kernel(raw_ref, cs_ref, cw_ref, ba_ref, par_ref, s_ref, e_ref, etb_ref, etg_ref, o_ref, so_ref):
    E, ETB, ETG = e_ref[...], etb_ref[...], etg_ref[...]
    width = DN_HEADS * DN_DK
    conv = cw_ref[DN_CONV - 1:DN_CONV, :] * raw_ref[0]
    for t in range(DN_CONV - 1):
        conv = conv + cw_ref[t:t + 1, :] * cs_ref[0, t:t + 1, :]
    act = _silu(conv)

    def l2(zz):
        return zz * _row_hdot(lax.rsqrt(_row_hdot(zz * zz, E) + EPS), ETB)

    qn = l2(act[:, 0:width]) * (DN_DK ** -0.5)
    kn = l2(act[:, width:2 * width])
    vn = act[:, 2 * width:3 * width]
    ba = ba_ref[0]
    beta = _row_hdot(_sigmoid(ba), ETB)
    eg = jnp.exp(_row_hdot(par_ref[0:1, :] * _softplus(ba + par_ref[1:2, :]), ETG))
    row0 = lax.broadcasted_iota(jnp.int32, (8, LANES), 0) == 0
    for h in range(DN_HEADS):
        sl = slice(h * LANES, (h + 1) * LANES)
        S = s_ref[0, h]
        q, k, v, b, e = qn[:, sl], kn[:, sl], vn[:, sl], beta[:, sl], eg[:, sl]
        Sr = _bf_round(S)
        v_new = v * b - _row_hdot(_bf_round(k * b * e), Sr)
        a = jnp.sum(q * k, axis=-1, keepdims=True)
        o_ref[0, :, sl] = _row_hdot(_bf_round(q * e), Sr) + a * v_new
        k8 = jnp.where(row0, _rows8(k), 0.0)
        upd = lax.dot_general(k8, _rows8(v_new), (((0,), (0,)), ((), ())), preferred_element_type=F32, precision=HI)
        so_ref[0, h] = S * e + upd


def _sample_dn(raw3, conv_state, conv_w, ba3, par, s0, e_mat, etb, etg):
    Bs, _, C = raw3.shape
    H = DN_HEADS
    full = lambda a: pl.BlockSpec(a.shape, lambda b: (0,) * a.ndim)
    s_spec = pl.BlockSpec((1, H, DN_DK, LANES), lambda b: (b, 0, 0, 0))
    return pl.pallas_call(
        _sample_dn_kernel,
        grid=(Bs,),
        in_specs=[pl.BlockSpec((1, 1, C), lambda b: (b, 0, 0)),
                  pl.BlockSpec((1, DN_CONV - 1, C), lambda b: (b, 0, 0)),
                  full(conv_w),
                  pl.BlockSpec((1, 1, LANES), lambda b: (b, 0, 0)),
                  full(par), s_spec, full(e_mat), full(etb), full(etg)],
        out_specs=[pl.BlockSpec((1, 1, H * LANES), lambda b: (b, 0, 0)), s_spec],
        out_shape=[jax.ShapeDtypeStruct((Bs, 1, H * LANES), F32), jax.ShapeDtypeStruct(s0.shape, F32)],
        compiler_params=_cparams(1),
        name="sample_dn",
    )(raw3, conv_state, conv_w, ba3, par, s0, e_mat, etb, etg)


def _sample_out_kernel(x_ref, oa_ref, od_ref, gates_ref, dng_ref, wa_ref, wb_ref, wo_ref, y_ref):
    y_ref[...] = _gated_mix(oa_ref[...], od_ref[...], gates_ref[...], dng_ref[...], wa_ref[...], wb_ref[...],
                            wo_ref[...], x_ref[...], _bdot)


def _sample_out(x2d, oa, od, gates, dng, wa, wb, wo):
    args = (x2d, oa, od, gates, dng, wa, wb, wo)
    return pl.pallas_call(
        _sample_out_kernel,
        grid=(1,),
        in_specs=[pl.BlockSpec(a.shape, lambda i: (0, 0)) for a in args],
        out_specs=pl.BlockSpec(x2d.shape, lambda i: (0, 0)),
        out_shape=jax.ShapeDtypeStruct(x2d.shape, F32),
        compiler_params=_cparams(1),
        name="sample_out",
    )(*args)


def _head_indicator(width, head):
    c = jnp.arange(width)[:, None] // head
    return (c == jnp.arange(LANES)[None, :]).astype(F32)


def _prep_layer(l, ln1_g, w_in, q_norm_g, k_norm_g, dn_conv_w, dn_a_log, dn_dt_bias, dn_norm_g, w_out_a, w_out_b,
                w_o, ln2_g, w_rg, b_rg, w_re, b_re, w_e_gate, w_e_up, w_e_down):
    D = w_in.shape[1]
    a_w = 3 * 3 * SWA_GW
    dn_w = DN_HEADS * 3 * DN_DK
    hv = DN_HEADS * DN_DK
    w = w_in[l]
    splits = dict(att=w[:, :a_w], dn=w[:, a_w:a_w + dn_w],
                  ba=jnp.pad(w[:, a_w + dn_w:a_w + dn_w + 2 * DN_HEADS], ((0, 0), (0, LANES - 2 * DN_HEADS))),
                  gate=w[:, a_w + dn_w + 2 * DN_HEADS:])
    assert splits["gate"].shape[1] == hv + 2 * D
    tile_heads = lambda g: jnp.broadcast_to(g[:, None, :], (len(SWA_CONFIGS), SWA_HEADS, SWA_DIM)).reshape(len(SWA_CONFIGS), SWA_GW)
    qg, kg = tile_heads(q_norm_g[l]), tile_heads(k_norm_g[l])
    idx = jnp.arange(MXU) // SWA_DIM
    par = jnp.zeros((2, LANES), F32)
    par = par.at[0, DN_HEADS:2 * DN_HEADS].set(-jnp.exp(dn_a_log[l].astype(F32)))
    par = par.at[1, DN_HEADS:2 * DN_HEADS].set(dn_dt_bias[l].astype(F32))
    wr = jnp.pad(jnp.concatenate([w_rg[l], w_re[l]], axis=1), ((0, 0), (0, LANES - N_GROUPS - N_EXPERTS)))
    br = jnp.pad(jnp.concatenate([b_rg[l], b_re[l]]), (0, LANES - N_GROUPS - N_EXPERTS)).reshape(1, LANES)
    e8 = _head_indicator(hv, DN_DK)
    return dict(
        bf16={k: v.astype(BF16) for k, v in splits.items()},
        ln1=ln1_g[l].reshape(1, D), ln2=ln2_g[l].reshape(1, D),
        ng=jnp.stack([qg.reshape(1, -1), kg.reshape(1, -1)]), qg=qg, kg=kg,
        bd=(idx[:, None] == idx[None, :]).astype(BF16),
        conv_w=dn_conv_w[l], par=par, dng=dn_norm_g[l].reshape(1, DN_DK),
        wa=w_out_a[l].astype(BF16), wb=w_out_b[l].astype(BF16), wo=w_o[l].astype(BF16), wr=wr.astype(BF16), br=br,
        wg=w_e_gate[l].astype(BF16), wu=w_e_up[l].astype(BF16), wd=w_e_down[l].astype(BF16),
        e_att=_head_indicator(SWA_GW, SWA_DIM), e_dn=e8, etb=e8.T, etg=jnp.roll(e8, DN_HEADS, axis=1).T,
    )


def _layer_prompt(x, p):
    B, L, D = x.shape
    N = B * L
    x2d = x.reshape(N, D)
    bw = p["bf16"]
    pk0, pk1, pk2, t0, t1, t2 = _proj_attn(x, p["ln1"], bw["att"], p["ng"], p["bd"], tm=min(512, L))
    tmp = min(1024, N)
    raw = _proj_plain(x2d, p["ln1"], bw["dn"], tm=tmp, tn=1536, out_dtype=F32, name="proj_dn")
    gates = _proj_plain(x2d, p["ln1"], bw["gate"], tm=tmp, tn=1536, out_dtype=BF16, name="proj_gate")
    ba = _proj_plain(x2d, p["ln1"], bw["ba"], tm=tmp, tn=LANES, out_dtype=F32, name="proj_ba")
    os_, ls_ = [], []
    for pk in (pk0, pk1, pk2):
        d, M = pk.shape[1], pk.shape[2]
        o, lse = _attn(pk.reshape(B * d, M, pk.shape[3]), tq=min(256, M))
        os_.append(o.reshape(B, d, M, SWA_GW))
        ls_.append(lse.reshape(B, d, M, SWA_GW))
    raw3 = raw.reshape(B, L, -1)
    qd, kd, vd, gb = _dn_conv(raw3, jnp.zeros((B, 8, raw3.shape[2]), F32), p["conv_w"], ba.reshape(B, L, LANES),
                              p["par"], tl=min(256, L))
    u, w, qdec, kdec, a, gt = _dn_intra(qd, kd, vd, gb, tl=min(512, L))
    od, s_new = _dn_scan(u, w, qdec, kdec, a, gt, jnp.zeros((B, DN_HEADS, DN_DK, LANES), F32))
    x2 = _out_proj(x2d, os_, ls_, od.reshape(N, -1), gates, p["dng"], p["wa"], p["wb"], p["wo"], B=B, L=L,
                   tm=min(512, L))
    y = _moe(x2, p["ln2"], p["wr"], p["br"], p["wg"], p["wu"], p["wd"], tm=256)
    kv = [t.reshape(B, t.shape[1], 2, SWA_HEADS, SWA_DIM) for t in (t0, t1, t2)]
    return y.reshape(B, L, D), kv, raw3[:, L - (DN_CONV - 1):], s_new


def _layer_sample(x, caches, conv_state, s0, p):
    Bs, T, D = x.shape
    assert T == 1
    x2d = x.reshape(Bs, D)
    bw = p["bf16"]
    proj = functools.partial(_proj_plain, x2d, p["ln1"], tm=Bs, out_dtype=F32)
    z_att = proj(bw["att"], tn=1536, name="sproj_att")
    raw = proj(bw["dn"], tn=1536, name="sproj_dn")
    gates = proj(bw["gate"], tn=1536, name="sproj_gate")
    ba = proj(bw["ba"], tn=LANES, name="sproj_ba")
    oa, kv = _sample_attn(z_att.reshape(Bs, 1, -1), caches, p["qg"], p["kg"], p["e_att"], p["e_att"].T)
    raw3 = raw.reshape(Bs, 1, -1)
    od, s_new = _sample_dn(raw3, conv_state, p["conv_w"], ba.reshape(Bs, 1, LANES), p["par"], s0,
                           p["e_dn"], p["etb"], p["etg"])
    x2 = _sample_out(x2d, oa.reshape(Bs, -1), od.reshape(Bs, -1), gates, p["dng"], p["wa"], p["wb"], p["wo"])
    y = _moe(x2, p["ln2"], p["wr"], p["br"], p["wg"], p["wu"], p["wd"], tm=Bs)
    W2 = 2 * SWA_GW
    kvs = [kv[:, :, g * W2:(g + 1) * W2].reshape(Bs, 1, 2, SWA_HEADS, SWA_DIM) for g in range(len(SWA_CONFIGS))]
    new_conv = jnp.concatenate([conv_state[:, 1:], raw3], axis=1)
    return y.reshape(Bs, 1, D), kvs, new_conv, s_new


def kernel(x_prompt, x_sample, cache_swa0_kv, cache_swa1_kv, cache_swa2_kv, state_dn_conv, state_dn_S, ln1_g, w_in,
           q_norm_g, k_norm_g, dn_conv_w, dn_a_log, dn_dt_bias, dn_norm_g, w_out_a, w_out_b, w_o, ln2_g, w_rg, b_rg,
           w_re, b_re, w_e_gate, w_e_up, w_e_down):
    yp, ys = x_prompt, x_sample
    outs = [[] for _ in range(10)]
    for l in range(w_in.shape[0]):
        p = _prep_layer(l, ln1_g, w_in, q_norm_g, k_norm_g, dn_conv_w, dn_a_log, dn_dt_bias, dn_norm_g, w_out_a,
                        w_out_b, w_o, ln2_g, w_rg, b_rg, w_re, b_re, w_e_gate, w_e_up, w_e_down)
        yp, pkv, pconv, ps = _layer_prompt(yp, p)
        ys, skv, sconv, ss = _layer_sample(ys, (cache_swa0_kv[l], cache_swa1_kv[l], cache_swa2_kv[l]),
                                           state_dn_conv[l], state_dn_S[l], p)
        for lst, val in zip(outs, (*pkv, pconv, ps, *skv, sconv, ss)):
            lst.append(val)
    return (yp, ys, *(jnp.stack(o) for o in outs))
```

```python
import functools

import jax
import jax.numpy as jnp
from jax import lax
from jax.experimental import pallas as pl
from jax.experimental.pallas import tpu as pltpu

F32 = jnp.float32
BF16 = jnp.bfloat16
HI = lax.Precision.HIGHEST
EPS = 1e-6

SWA_CONFIGS = ((128, 1), (512, 4), (2048, 16))
SWA_HEADS = 8
SWA_DIM = 64
SWA_GW = SWA_HEADS * SWA_DIM
SWA_SPAN = 128
DN_HEADS = 8
DN_DK = 128
DN_CONV = 4
DN_CHUNK = 64
N_GROUPS = 4
PER_GROUP = 8
N_EXPERTS = N_GROUPS * PER_GROUP
TOP_K = 2

VMEM_LIMIT_BYTES = 56 * 1024 * 1024
LANES = 128
MXU = 256
MOE_ROWS = 256


def _cparams(n_axes):
    return pltpu.CompilerParams(
        dimension_semantics=("arbitrary",) * n_axes, vmem_limit_bytes=VMEM_LIMIT_BYTES
    )


def _rms(x, g):
    return x * lax.rsqrt(jnp.mean(x * x, axis=-1, keepdims=True) + EPS) * g


def _bdot(a, b):
    return jnp.dot(a.astype(BF16), b.astype(BF16), preferred_element_type=F32)


def _hdot(a, b):
    return jnp.dot(a, b, preferred_element_type=F32, precision=HI)


def _sigmoid(x):
    return 1.0 / (1.0 + jnp.exp(-x))


def _silu(x):
    return x * _sigmoid(x)


def _softplus(x):
    return jnp.maximum(x, 0.0) + jnp.log1p(jnp.exp(-jnp.abs(x)))


def _proj_attn_kernel(x_ref, lng_ref, w_ref, ng_ref, bd_ref, p0, p1, p2, t0, t1, t2, h_scr, z_scr,
                      *, tm, dils, tail_rows):
    j = pl.program_id(2)

    @pl.when(j == 0)
    def _():
        h_scr[...] = _rms(x_ref[0], lng_ref[...]).astype(BF16)

    z = jnp.dot(h_scr[...], w_ref[...], preferred_element_type=F32)
    n_cb = z_scr.shape[0]
    for c in range(n_cb):
        z_scr[c] = z[:, c * LANES:(c + 1) * LANES]

    @pl.when(j < 2)
    def _():
        for c in range(0, n_cb, 2):
            zc = jnp.concatenate([z_scr[c], z_scr[c + 1]], axis=-1)
            ss = jnp.dot((zc * zc).astype(BF16), bd_ref[...], preferred_element_type=F32)
            zn = zc * lax.rsqrt(ss * (1.0 / SWA_DIM) + EPS) * ng_ref[0, :, c * LANES:(c + 2) * LANES]
            z_scr[c] = zn[:, :LANES]
            z_scr[c + 1] = zn[:, LANES:]

    outs = (p0, p1, p2)
    tails = (t0, t1, t2)
    per_g = SWA_GW // LANES
    for sec in range(3):

        @pl.when(j == sec)
        def _(sec=sec):
            for gi, d in enumerate(dils):
                for cb in range(per_g):
                    c = gi * per_g + cb
                    col = sec * SWA_GW + cb * LANES
                    for r in range(d):
                        src = z_scr[c] if d == 1 else z_scr[c, pl.ds(r, tm // d, stride=d), :]
                        outs[gi][0, r, :, col:col + LANES] = src.astype(BF16)
                    if sec >= 1:
                        rows = tail_rows[gi]
                        tails[gi][0, :, col - SWA_GW:col - SWA_GW + LANES] = z_scr[c, tm - rows:tm, :]


def _proj_attn(x, ln_g, w_att, ng, bd, *, tm):
    B, L, D = x.shape
    nt = L // tm
    dils = tuple(d for _, d in SWA_CONFIGS)
    keeps = tuple(min(w, L) for w, _ in SWA_CONFIGS)
    tail_rows = tuple(min(k, tm) for k in keeps)
    for k, r in zip(keeps, tail_rows):
        assert k % r == 0 and L % tm == 0
    W3 = 3 * SWA_GW

    def tail_spec(keep, rows):
        nblk = keep // rows
        return pl.BlockSpec((1, rows, 2 * SWA_GW), lambda b, i, j: (b, jnp.maximum(i - (nt - nblk), 0), 0))

    out_shape = [jax.ShapeDtypeStruct((B, d, L // d, W3), BF16) for d in dils]
    out_shape += [jax.ShapeDtypeStruct((B, k, 2 * SWA_GW), F32) for k in keeps]
    out_specs = [pl.BlockSpec((1, d, tm // d, W3), lambda b, i, j: (b, 0, i, 0)) for d in dils]
    out_specs += [tail_spec(k, r) for k, r in zip(keeps, tail_rows)]
    return pl.pallas_call(
        functools.partial(_proj_attn_kernel, tm=tm, dils=dils, tail_rows=tail_rows),
        grid=(B, nt, 3),
        in_specs=[
            pl.BlockSpec((1, tm, D), lambda b, i, j: (b, i, 0)),
            pl.BlockSpec((1, D), lambda b, i, j: (0, 0)),
            pl.BlockSpec((D, W3), lambda b, i, j: (0, j)),
            pl.BlockSpec((1, 1, W3), lambda b, i, j: (jnp.minimum(j, 1), 0, 0)),
            pl.BlockSpec((MXU, MXU), lambda b, i, j: (0, 0)),
        ],
        out_specs=out_specs,
        out_shape=out_shape,
        scratch_shapes=[pltpu.VMEM((tm, D), BF16), pltpu.VMEM((W3 // LANES, tm, LANES), F32)],
        compiler_params=_cparams(3),
        name="proj_attn",
    )(x, ln_g, w_att, ng, bd)


def _proj_plain_kernel(x_ref, lng_ref, w_ref, o_ref, h_scr):
    @pl.when(pl.program_id(1) == 0)
    def _():
        h_scr[...] = _rms(x_ref[...], lng_ref[...]).astype(BF16)

    o_ref[...] = jnp.dot(h_scr[...], w_ref[...], preferred_element_type=F32).astype(o_ref.dtype)


def _proj_plain(x2d, ln_g, w, *, tm, tn, out_dtype, name="proj_plain"):
    N, D = x2d.shape
    C = w.shape[1]
    assert N % tm == 0 and C % tn == 0
    return pl.pallas_call(
        _proj_plain_kernel,
        grid=(N // tm, C // tn),
        in_specs=[
            pl.BlockSpec((tm, D), lambda i, j: (i, 0)),
            pl.BlockSpec((1, D), lambda i, j: (0, 0)),
            pl.BlockSpec((D, tn), lambda i, j: (0, j)),
        ],
        out_specs=pl.BlockSpec((tm, tn), lambda i, j: (i, j)),
        out_shape=jax.ShapeDtypeStruct((N, C), out_dtype),
        scratch_shapes=[pltpu.VMEM((tm, D), BF16)],
        compiler_params=_cparams(2),
        name=name,
    )(x2d, ln_g, w)


def _attn_kernel(q_ref, kc_ref, vc_ref, kp_ref, vp_ref, o_ref, lse_ref, kk_scr, vv_scr, *, tq):
    i = pl.program_id(1)
    blk = SWA_SPAN
    kk_scr[0:blk, :] = kp_ref[0]
    kk_scr[blk:blk + tq, :] = kc_ref[0]
    vv_scr[0:blk, :] = vp_ref[0]
    vv_scr[blk:blk + tq, :] = vc_ref[0]
    qi = lax.broadcasted_iota(jnp.int32, (blk, 2 * blk), 0)
    ki = lax.broadcasted_iota(jnp.int32, (blk, 2 * blk), 1)
    dist = blk + qi - ki
    band = (dist >= 0) & (dist <= SWA_SPAN)
    band_first = band & ((ki >= blk) | (i > 0))
    lo = lax.broadcasted_iota(jnp.int32, (blk, LANES), 1) < SWA_DIM
    zero = jnp.zeros((blk, LANES), BF16)
    for jb in range(tq // blk):
        mask = band_first if jb == 0 else band
        rows = slice(jb * blk, (jb + 1) * blk)
        for hp in range(SWA_GW // LANES):
            cs = slice(hp * LANES, (hp + 1) * LANES)
            qb = q_ref[0, rows, cs]
            kk = kk_scr[jb * blk:(jb + 2) * blk, cs]
            vv = vv_scr[jb * blk:(jb + 2) * blk, cs]
            res_o, res_l = [], []
            for hh in range(2):
                qm = jnp.where(lo if hh == 0 else jnp.logical_not(lo), qb, zero)
                s = lax.dot_general(qm, kk, (((1,), (1,)), ((), ())), preferred_element_type=F32)
                s = jnp.where(mask, s * (SWA_DIM ** -0.5), -jnp.inf)
                m = jnp.max(s, axis=-1, keepdims=True)
                p = jnp.exp(s - m)
                den = jnp.sum(p, axis=-1, keepdims=True)
                pv = jnp.dot(p.astype(BF16), vv, preferred_element_type=F32)
                res_o.append(pv / den)
                res_l.append(jnp.broadcast_to(m + jnp.log(den), (blk, LANES)))
            o_ref[0, rows, cs] = jnp.where(lo, res_o[0], res_o[1]).astype(BF16)
            lse_ref[0, rows, cs] = jnp.where(lo, res_l[0], res_l[1])


def _attn(p, *, tq):
    S, M, _ = p.shape
    assert M % tq == 0 and tq % SWA_SPAN == 0
    nb = tq // SWA_SPAN
    return pl.pallas_call(
        functools.partial(_attn_kernel, tq=tq),
        grid=(S, M // tq),
        in_specs=[
            pl.BlockSpec((1, tq, SWA_GW), lambda s, i: (s, i, 0)),
            pl.BlockSpec((1, tq, SWA_GW), lambda s, i: (s, i, 1)),
            pl.BlockSpec((1, tq, SWA_GW), lambda s, i: (s, i, 2)),
            pl.BlockSpec((1, SWA_SPAN, SWA_GW), lambda s, i: (s, jnp.maximum(i * nb - 1, 0), 1)),
            pl.BlockSpec((1, SWA_SPAN, SWA_GW), lambda s, i: (s, jnp.maximum(i * nb - 1, 0), 2)),
        ],
        out_specs=[
            pl.BlockSpec((1, tq, SWA_GW), lambda s, i: (s, i, 0)),
            pl.BlockSpec((1, tq, SWA_GW), lambda s, i: (s, i, 0)),
        ],
        out_shape=[
            jax.ShapeDtypeStruct((S, M, SWA_GW), BF16),
            jax.ShapeDtypeStruct((S, M, SWA_GW), F32),
        ],
        scratch_shapes=[
            pltpu.VMEM((SWA_SPAN + tq, SWA_GW), BF16),
            pltpu.VMEM((SWA_SPAN + tq, SWA_GW), BF16),
        ],
        compiler_params=_cparams(2),
        name="swa_attn",
    )(p, p, p, p, p)


def _conv_kernel(x_ref, xp_ref, cp_ref, cw_ref, ba_ref, par_ref, q_ref, k_ref, v_ref, g_ref, xs_scr,
                 *, tl):
    i = pl.program_id(1)
    xs_scr[0:8, :] = jnp.where(i == 0, cp_ref[0], xp_ref[0])
    xs_scr[8:8 + tl, :] = x_ref[0]
    nh = DN_HEADS
    outs = (q_ref, k_ref, v_ref)
    for cb in range(3 * nh):
        cs = slice(cb * LANES, (cb + 1) * LANES)
        acc = cw_ref[0:1, cs] * xs_scr[5:5 + tl, cs]
        for t in range(1, DN_CONV):
            acc = acc + cw_ref[t:t + 1, cs] * xs_scr[5 + t:5 + t + tl, cs]
        act = _silu(acc)
        part, h = divmod(cb, nh)
        if part < 2:
            act = act * lax.rsqrt(jnp.sum(act * act, axis=-1, keepdims=True) + EPS)
        if part == 0:
            act = act * (DN_DK ** -0.5)
        outs[part][0, :, h * LANES:(h + 1) * LANES] = act.astype(BF16)
    ba = ba_ref[0]
    lane = lax.broadcasted_iota(jnp.int32, (tl, LANES), 1)
    g = par_ref[0:1, :] * _softplus(ba + par_ref[1:2, :])
    ri = lax.broadcasted_iota(jnp.int32, (tl, tl), 0)
    ci = lax.broadcasted_iota(jnp.int32, (tl, tl), 1)
    tri = jnp.where((ri // DN_CHUNK == ci // DN_CHUNK) & (ci <= ri), 1.0, 0.0).astype(F32)
    gc = _hdot(tri, g)
    g_ref[0] = jnp.where(lane < nh, _sigmoid(ba), gc)


def _dn_conv(raw, conv_prev8, conv_w, ba, par, *, tl):
    B, L, C = raw.shape
    assert L % tl == 0 and tl % DN_CHUNK == 0
    width = DN_HEADS * DN_DK
    return pl.pallas_call(
        functools.partial(_conv_kernel, tl=tl),
        grid=(B, L // tl),
        in_specs=[
            pl.BlockSpec((1, tl, C), lambda b, i: (b, i, 0)),
            pl.BlockSpec((1, 8, C), lambda b, i: (b, jnp.maximum(i * (tl // 8) - 1, 0), 0)),
            pl.BlockSpec((1, 8, C), lambda b, i: (b, 0, 0)),
            pl.BlockSpec((DN_CONV, C), lambda b, i: (0, 0)),
            pl.BlockSpec((1, tl, LANES), lambda b, i: (b, i, 0)),
            pl.BlockSpec((2, LANES), lambda b, i: (0, 0)),
        ],
        out_specs=[pl.BlockSpec((1, tl, width), lambda b, i: (b, i, 0))] * 3
        + [pl.BlockSpec((1, tl, LANES), lambda b, i: (b, i, 0))],
        out_shape=[jax.ShapeDtypeStruct((B, L, width), BF16)] * 3
        + [jax.ShapeDtypeStruct((B, L, LANES), F32)],
        scratch_shapes=[pltpu.VMEM((8 + tl, C), F32)],
        compiler_params=_cparams(2),
        name="dn_conv",
    )(raw, raw, conv_prev8, conv_w, ba, par)


def _intra_kernel(q_ref, k_ref, v_ref, g_ref, u_ref, w_ref, qd_ref, kd_ref, a_ref, gt_ref, *, tl):
    h = pl.program_id(1)
    C = DN_CHUNK
    lane = lax.broadcasted_iota(jnp.int32, (C, LANES), 1)
    ri = lax.broadcasted_iota(jnp.int32, (C, C), 0)
    ci = lax.broadcasted_iota(jnp.int32, (C, C), 1)
    eye = jnp.where(ri == ci, 1.0, 0.0).astype(F32)
    nt_dot = lambda a, b: lax.dot_general(a.astype(BF16), b.astype(BF16), (((1,), (1,)), ((), ())),
                                          preferred_element_type=F32)
    rows = [slice(c * C, (c + 1) * C) for c in range(tl // C)]
    gv = [g_ref[0, r, :] for r in rows]
    q = [q_ref[0, r, :].astype(F32) for r in rows]
    k = [k_ref[0, r, :].astype(F32) for r in rows]
    v = [v_ref[0, r, :].astype(F32) for r in rows]
    beta = [jnp.sum(jnp.where(lane == h, x, 0.0), axis=-1, keepdims=True) for x in gv]
    gc = [jnp.sum(jnp.where(lane == h + DN_HEADS, x, 0.0), axis=-1, keepdims=True) for x in gv]
    lhs = [jnp.where(lane == 0, x, jnp.where(lane == 1, 1.0, 0.0)) for x in gc]
    rhs = [jnp.where(lane == 0, 1.0, jnp.where(lane == 1, -x, 0.0)) for x in gc]
    diff = [lax.dot_general(a, b, (((1,), (1,)), ((), ())), preferred_element_type=F32, precision=HI)
            for a, b in zip(lhs, rhs)]
    decay = [jnp.exp(jnp.where(ri >= ci, x, -jnp.inf)) for x in diff]
    kb = [a * b for a, b in zip(k, beta)]
    x = [-jnp.where(ri > ci, nt_dot(a, b) * d, 0.0) for a, b, d in zip(kb, k, decay)]
    t = [eye + a for a in x]
    for _ in range(5):
        x = [_bdot(a, a) for a in x]
        t = [a + _bdot(a, b) for a, b in zip(t, x)]
    eg = [jnp.exp(a) for a in gc]
    glast = [a[C - 1:C, :] for a in gc]
    u = [_bdot(a, b * c) for a, b, c in zip(t, v, beta)]
    w = [_bdot(a, b * c) for a, b, c in zip(t, kb, eg)]
    qk = [nt_dot(a, b) for a, b in zip(q, k)]
    for c, r in enumerate(rows):
        u_ref[0, 0, r, :] = u[c]
        w_ref[0, 0, r, :] = w[c].astype(BF16)
        a_ref[0, 0, r, :] = (qk[c] * decay[c]).astype(BF16)
        qd_ref[0, 0, r, :] = (q[c] * eg[c]).astype(BF16)
        kd_ref[0, 0, r, :] = (k[c] * jnp.exp(glast[c] - gc[c])).astype(BF16)
        gt_ref[0, 0, c:c + 1, :] = jnp.broadcast_to(jnp.exp(glast[c]), (1, LANES))


def _dn_intra(q, k, v, g, *, tl):
    B, L, _ = q.shape
    H, C = DN_HEADS, DN_CHUNK
    assert L % tl == 0 and (tl // C) % 8 == 0
    qkv_spec = pl.BlockSpec((1, tl, LANES), lambda b, h, i: (b, i, h))
    hl = lambda w: pl.BlockSpec((1, 1, tl, w), lambda b, h, i: (b, h, i, 0))
    return pl.pallas_call(
        functools.partial(_intra_kernel, tl=tl),
        grid=(B, H, L // tl),
        in_specs=[qkv_spec, qkv_spec, qkv_spec, pl.BlockSpec((1, tl, LANES), lambda b, h, i: (b, i, 0))],
        out_specs=[hl(LANES), hl(LANES), hl(LANES), hl(LANES), hl(C),
                   pl.BlockSpec((1, 1, tl // C, LANES), lambda b, h, i: (b, h, i, 0))],
        out_shape=[
            jax.ShapeDtypeStruct((B, H, L, LANES), F32),
            jax.ShapeDtypeStruct((B, H, L, LANES), BF16),
            jax.ShapeDtypeStruct((B, H, L, LANES), BF16),
            jax.ShapeDtypeStruct((B, H, L, LANES), BF16),
            jax.ShapeDtypeStruct((B, H, L, C), BF16),
            jax.ShapeDtypeStruct((B, H, L // C, LANES), F32),
        ],
        compiler_params=_cparams(3),
        name="dn_intra",
    )(q, k, v, g)


def _scan_kernel(u_ref, w_ref, qd_ref, kd_ref, a_ref, gt_ref, s0_ref, o_ref, s_ref, *, n_chunks):
    C = DN_CHUNK
    H = s_ref.shape[1]

    @pl.when(pl.program_id(1) == 0)
    def _():
        s_ref[...] = s0_ref[...]

    def body(c, carry):
        rows = pl.ds(pl.multiple_of(c * C, C), C)
        S = [s_ref[0, h] for h in range(H)]
        Sb = [x.astype(BF16) for x in S]
        v_new = [u_ref[0, h, rows, :] - jnp.dot(w_ref[0, h, rows, :], Sb[h], preferred_element_type=F32)
                 for h in range(H)]
        vb = [x.astype(BF16) for x in v_new]
        o = [jnp.dot(qd_ref[0, h, rows, :], Sb[h], preferred_element_type=F32)
             + jnp.dot(a_ref[0, h, rows, :], vb[h], preferred_element_type=F32) for h in range(H)]
        upd = [lax.dot_general(kd_ref[0, h, rows, :], vb[h], (((0,), (0,)), ((), ())), preferred_element_type=F32)
               for h in range(H)]
        for h in range(H):
            o_ref[0, rows, h * LANES:(h + 1) * LANES] = o[h]
            s_ref[0, h] = S[h] * gt_ref[0, h, pl.ds(c, 1), :] + upd[h]
        return carry

    lax.fori_loop(0, n_chunks, body, 0)


def _dn_scan(u, w, qd, kd, a, gt, s0, *, tl):
    B, H, L, _ = u.shape
    C = DN_CHUNK
    assert L % tl == 0 and (tl // C) % 8 == 0
    hs = lambda wd: pl.BlockSpec((1, H, tl, wd), lambda b, i: (b, 0, i, 0))
    s_spec = pl.BlockSpec((1, H, DN_DK, LANES), lambda b, i: (b, 0, 0, 0))
    return pl.pallas_call(
        functools.partial(_scan_kernel, n_chunks=tl // C),
        grid=(B, L // tl),
        in_specs=[hs(LANES), hs(LANES), hs(LANES), hs(LANES), hs(C),
                  pl.BlockSpec((1, H, tl // C, LANES), lambda b, i: (b, 0, i, 0)), s_spec],
        out_specs=[pl.BlockSpec((1, tl, H * LANES), lambda b, i: (b, i, 0)), s_spec],
        out_shape=[jax.ShapeDtypeStruct((B, L, H * LANES), F32),
                   jax.ShapeDtypeStruct((B, H, DN_DK, LANES), F32)],
        compiler_params=_cparams(2),
        name="dn_scan",
    )(u, w, qd, kd, a, gt, s0)


def _gated_mix(o_a, od, gates, dng, wa, wb, wo, x, dot):
    width = DN_HEADS * DN_DK
    parts = []
    for h in range(DN_HEADS):
        blk = od[:, h * LANES:(h + 1) * LANES]
        parts.append(blk * lax.rsqrt(jnp.mean(blk * blk, axis=-1, keepdims=True) + EPS) * dng)
    odn = jnp.concatenate(parts, axis=-1) * _silu(gates[:, 0:width].astype(F32))
    ya = dot(o_a, wa)
    yb = dot(odn, wb)
    mix = _sigmoid(gates[:, width:2 * width].astype(F32)) * ya + _sigmoid(gates[:, 2 * width:].astype(F32)) * yb
    return x + dot(mix, wo)


def _out_kernel(x_ref, o0, o1, o2, l0, l1, l2, od_ref, gates_ref, dng_ref, wa_ref, wb_ref, wo_ref, y_ref,
                so0, so1, so2, sl0, sl1, sl2, *, tm, dils):
    o_refs, l_refs = (o0, o1, o2), (l0, l1, l2)
    so, sl = (so0, so1, so2), (sl0, sl1, sl2)
    parts = []
    for cb in range(SWA_GW // LANES):
        cs = slice(cb * LANES, (cb + 1) * LANES)
        for gi, d in enumerate(dils):
            for r in range(d):
                dst = slice(None) if d == 1 else pl.ds(r, tm // d, stride=d)
                so[gi][cb, dst, :] = o_refs[gi][0, r, :, cs].astype(F32)
                sl[gi][cb, dst, :] = l_refs[gi][0, r, :, cs]
        ls = [s[cb] for s in sl]
        m = jnp.maximum(jnp.maximum(ls[0], ls[1]), ls[2])
        es = [jnp.exp(l - m) for l in ls]
        parts.append((es[0] * so[0][cb] + es[1] * so[1][cb] + es[2] * so[2][cb]) / (es[0] + es[1] + es[2]))
    o_a = jnp.concatenate(parts, axis=-1)
    y_ref[...] = _gated_mix(o_a, od_ref[...], gates_ref[...], dng_ref[...], wa_ref[...], wb_ref[...],
                            wo_ref[...], x_ref[...], _bdot)


def _out_proj(x2d, os_, ls_, od2d, gates, dng, wa, wb, wo, *, B, L, tm):
    N, D = x2d.shape
    nt = L // tm
    dils = tuple(d for _, d in SWA_CONFIGS)
    grp = lambda d: pl.BlockSpec((1, d, tm // d, SWA_GW), lambda i: (i // nt, 0, i % nt, 0))
    row = lambda w: pl.BlockSpec((tm, w), lambda i: (i, 0))
    full = lambda a: pl.BlockSpec(a.shape, lambda i: (0, 0))
    return pl.pallas_call(
        functools.partial(_out_kernel, tm=tm, dils=dils),
        grid=(N // tm,),
        in_specs=[row(D)] + [grp(d) for d in dils] * 2 + [row(od2d.shape[1]), row(gates.shape[1]),
                                                          full(dng), full(wa), full(wb), full(wo)],
        out_specs=row(D),
        out_shape=jax.ShapeDtypeStruct((N, D), F32),
        scratch_shapes=[pltpu.VMEM((SWA_GW // LANES, tm, LANES), F32)] * 6,
        compiler_params=_cparams(1),
        name="out_proj",
    )(x2d, *os_, *ls_, od2d, gates, dng, wa, wb, wo)


def _router_kernel(x_ref, lng_ref, wr_ref, br_ref, info_ref, cnt_ref, base_scr, *, tm):
    i = pl.program_id(0)

    @pl.when(i == 0)
    def _():
        base_scr[...] = jnp.zeros_like(base_scr)

    h = _rms(x_ref[...], lng_ref[...])
    lg = _bdot(h, wr_ref[...]) + br_ref[...]
    lane = lax.broadcasted_iota(jnp.int32, (tm, LANES), 1)
    big = jnp.int32(1 << 20)
    ninf = -jnp.inf

    def argmax_lane(vals):
        mx = jnp.max(vals, axis=-1, keepdims=True)
        idx = jnp.min(jnp.where(vals == mx, lane, big), axis=-1, keepdims=True)
        return mx, idx

    lgm = jnp.where(lane < N_GROUPS, lg, ninf)
    mg, gsel = argmax_lane(lgm)
    pg = 1.0 / jnp.sum(jnp.exp(lgm - mg), axis=-1, keepdims=True)
    start = N_GROUPS + gsel * PER_GROUP
    le = jnp.where((lane >= start) & (lane < start + PER_GROUP), lg, ninf)
    m1, i1 = argmax_lane(le)
    m2, i2 = argmax_lane(jnp.where(lane == i1, ninf, le))
    e21 = jnp.exp(m2 - m1)
    w1 = pg / (1.0 + e21)
    w2 = pg * e21 / (1.0 + e21)
    oh = jnp.where(lane == i1, 1.0, 0.0) + jnp.where(lane == i2, 1.0, 0.0)
    ri = lax.broadcasted_iota(jnp.int32, (tm, tm), 0)
    ci = lax.broadcasted_iota(jnp.int32, (tm, tm), 1)
    strict = jnp.where(ci < ri, 1.0, 0.0).astype(BF16)
    pref = jnp.dot(strict, oh.astype(BF16), preferred_element_type=F32) + base_scr[...]
    r1 = jnp.sum(jnp.where(lane == i1, pref, 0.0), axis=-1, keepdims=True)
    r2 = jnp.sum(jnp.where(lane == i2, pref, 0.0), axis=-1, keepdims=True)
    base_scr[...] = base_scr[...] + jnp.sum(oh, axis=0, keepdims=True)
    cnt_ref[...] = base_scr[...]
    off = jnp.float32(N_GROUPS)
    info = jnp.where(lane == 0, i1.astype(F32) - off, 0.0)
    info = jnp.where(lane == 1, i2.astype(F32) - off, info)
    info = jnp.where(lane == 2, w1, info)
    info = jnp.where(lane == 3, w2, info)
    info = jnp.where(lane == 4, r1, info)
    info = jnp.where(lane == 5, r2, info)
    info_ref[...] = info


def _router(x2d, ln_g, wr, br, *, tm):
    N, D = x2d.shape
    assert N % tm == 0
    return pl.pallas_call(
        functools.partial(_router_kernel, tm=tm),
        grid=(N // tm,),
        in_specs=[
            pl.BlockSpec((tm, D), lambda i: (i, 0)),
            pl.BlockSpec((1, D), lambda i: (0, 0)),
            pl.BlockSpec((D, LANES), lambda i: (0, 0)),
            pl.BlockSpec((1, LANES), lambda i: (0, 0)),
        ],
        out_specs=[pl.BlockSpec((tm, LANES), lambda i: (i, 0)), pl.BlockSpec((1, LANES), lambda i: (0, 0))],
        out_shape=[jax.ShapeDtypeStruct((N, LANES), F32), jax.ShapeDtypeStruct((1, LANES), F32)],
        scratch_shapes=[pltpu.VMEM((1, LANES), F32)],
        compiler_params=_cparams(1),
        name="router",
    )(x2d, ln_g, wr, br)


def _dispatch_kernel(dest_ref, zb_ref, x_ref, xs_ref, zero_scr, sem, *, tm, tb, n_zb, n_tiles):
    i = pl.program_id(0)

    @pl.when(i == 0)
    def _():
        zero_scr[...] = jnp.zeros_like(zero_scr)

        def zero_copy(n):
            return pltpu.make_async_copy(zero_scr, xs_ref.at[pl.ds(zb_ref[n] * tb, tb), :], sem.at[1])

        def zero_issue(n, carry):
            @pl.when(zb_ref[n] >= 0)
            def _():
                zero_copy(n).start()

            return carry

        def zero_wait(n, carry):
            @pl.when(zb_ref[n] >= 0)
            def _():
                zero_copy(n).wait()

            return carry

        lax.fori_loop(0, n_zb, zero_issue, 0)
        lax.fori_loop(0, n_zb, zero_wait, 0)

    def row_copy(tile, t, slot):
        a = tile * tm + t
        return pltpu.make_async_copy(
            x_ref.at[pl.ds(a, 1), :], xs_ref.at[pl.ds(dest_ref[a * TOP_K + slot], 1), :], sem.at[0])

    def issue(t, carry):
        for slot in range(TOP_K):
            row_copy(i, t, slot).start()
        return carry

    def drain(tile):
        def body(t, carry):
            for slot in range(TOP_K):
                row_copy(tile, t, slot).wait()
            return carry

        lax.fori_loop(0, tm, body, 0)

    lax.fori_loop(0, tm, issue, 0)

    @pl.when(i > 0)
    def _():
        drain(i - 1)

    @pl.when(i == n_tiles - 1)
    def _():
        drain(i)


def _dispatch(dest, zero_blocks, x2d, *, tm, tb, n_rows):
    N, D = x2d.shape
    return pl.pallas_call(
        functools.partial(_dispatch_kernel, tm=tm, tb=tb, n_zb=zero_blocks.shape[0], n_tiles=N // tm),
        grid_spec=pltpu.PrefetchScalarGridSpec(
            num_scalar_prefetch=2,
            grid=(N // tm,),
            in_specs=[pl.BlockSpec(memory_space=pl.ANY)],
            out_specs=pl.BlockSpec(memory_space=pl.ANY),
            scratch_shapes=[pltpu.VMEM((tb, D), F32), pltpu.SemaphoreType.DMA((2,))],
        ),
        out_shape=jax.ShapeDtypeStruct((n_rows, D), F32),
        compiler_params=_cparams(1),
        name="moe_dispatch",
    )(dest, zero_blocks, x2d)


def _ffn_kernel(be_ref, nb_ref, xs_ref, lng_ref, wg_ref, wu_ref, wd_ref, y_ref):
    used = pl.program_id(0) < nb_ref[0]

    @pl.when(used)
    def _():
        h = _rms(xs_ref[...], lng_ref[...]).astype(BF16)
        g = jnp.dot(h, wg_ref[0], preferred_element_type=F32)
        u = jnp.dot(h, wu_ref[0], preferred_element_type=F32)
        y_ref[...] = jnp.dot((_silu(g) * u).astype(BF16), wd_ref[0], preferred_element_type=F32)

    @pl.when(jnp.logical_not(used))
    def _():
        y_ref[...] = jnp.zeros_like(y_ref)


def _ffn(blk_e, nb_used, xs, ln_g, wg, wu, wd, *, tb):
    P, D = xs.shape
    nb = P // tb
    DE = wg.shape[2]
    rowi = lambda i, be, nbu: (jnp.minimum(i, nbu[0] - 1), 0)
    return pl.pallas_call(
        _ffn_kernel,
        grid_spec=pltpu.PrefetchScalarGridSpec(
            num_scalar_prefetch=2,
            grid=(nb,),
            in_specs=[
                pl.BlockSpec((tb, D), rowi),
                pl.BlockSpec((1, D), lambda i, be, nbu: (0, 0)),
                pl.BlockSpec((1, D, DE), lambda i, be, nbu: (be[i], 0, 0)),
                pl.BlockSpec((1, D, DE), lambda i, be, nbu: (be[i], 0, 0)),
                pl.BlockSpec((1, DE, D), lambda i, be, nbu: (be[i], 0, 0)),
            ],
            out_specs=pl.BlockSpec((tb, D), lambda i, be, nbu: (i, 0)),
        ),
        out_shape=jax.ShapeDtypeStruct((P, D), F32),
        compiler_params=_cparams(1),
        name="moe_ffn",
    )(blk_e, nb_used, xs, ln_g, wg, wu, wd)


def _combine_kernel(dest_ref, x_ref, info_ref, yb_ref, y_ref, g_scr, sem, *, tm, n_tiles):
    i = pl.program_id(0)

    def row_copy(tile, t, slot):
        buf = lax.rem(tile, 2)
        return pltpu.make_async_copy(
            yb_ref.at[pl.ds(dest_ref[(tile * tm + t) * TOP_K + slot], 1), :],
            g_scr.at[buf, slot, pl.ds(t, 1), :], sem.at[buf])

    def for_rows(tile, fn):
        def body(t, carry):
            for slot in range(TOP_K):
                fn(row_copy(tile, t, slot))
            return carry

        lax.fori_loop(0, tm, body, 0)

    @pl.when(i == 0)
    def _():
        for_rows(i, lambda c: c.start())

    @pl.when(i + 1 < n_tiles)
    def _():
        for_rows(i + 1, lambda c: c.start())

    for_rows(i, lambda c: c.wait())
    buf = lax.rem(i, 2)
    info = info_ref[...]
    lane = lax.broadcasted_iota(jnp.int32, info.shape, 1)
    w1 = jnp.sum(jnp.where(lane == 2, info, 0.0), axis=-1, keepdims=True)
    w2 = jnp.sum(jnp.where(lane == 3, info, 0.0), axis=-1, keepdims=True)
    y_ref[...] = x_ref[...] + (w1 * g_scr[buf, 0] + w2 * g_scr[buf, 1])


def _combine(dest, x2d, info, yb, *, tm):
    N, D = x2d.shape
    return pl.pallas_call(
        functools.partial(_combine_kernel, tm=tm, n_tiles=N // tm),
        grid_spec=pltpu.PrefetchScalarGridSpec(
            num_scalar_prefetch=1,
            grid=(N // tm,),
            in_specs=[
                pl.BlockSpec((tm, D), lambda i, d: (i, 0)),
                pl.BlockSpec((tm, LANES), lambda i, d: (i, 0)),
                pl.BlockSpec(memory_space=pl.ANY),
            ],
            out_specs=pl.BlockSpec((tm, D), lambda i, d: (i, 0)),
            scratch_shapes=[pltpu.VMEM((2, TOP_K, tm, D), F32), pltpu.SemaphoreType.DMA((2,))],
        ),
        out_shape=jax.ShapeDtypeStruct((N, D), F32),
        compiler_params=_cparams(1),
        name="moe_combine",
    )(dest, x2d, info, yb)


def _moe(x2d, ln2_g, wr, br, wg, wu, wd, *, tm):
    N, D = x2d.shape
    tb = MOE_ROWS
    info, counts = _router(x2d, ln2_g, wr, br, tm=tm)
    counts = counts[0, N_GROUPS:N_GROUPS + N_EXPERTS].astype(jnp.int32)
    pcounts = (counts + tb - 1) // tb * tb
    pend = jnp.cumsum(pcounts)
    pstart = pend - pcounts
    e = info[:, 0:TOP_K].astype(jnp.int32)
    rank = info[:, 4:4 + TOP_K].astype(jnp.int32)
    dest = (pstart[e] + rank).reshape(-1)
    nb = -(-(N * TOP_K) // tb) + N_EXPERTS
    P = nb * tb
    blocks = jnp.arange(nb, dtype=jnp.int32)
    blk_e = jnp.minimum(jnp.sum((pend[None, :] <= blocks[:, None] * tb).astype(jnp.int32), axis=1), N_EXPERTS - 1)
    nb_used = (pend[-1] // tb).astype(jnp.int32).reshape(1)
    zero_blocks = jnp.concatenate([jnp.where(counts % tb != 0, pend // tb - 1, -1),
                                   jnp.where(blocks >= nb_used[0], blocks, -1)]).astype(jnp.int32)
    xs = _dispatch(dest, zero_blocks, x2d, tm=tm, tb=tb, n_rows=P)
    yb = _ffn(blk_e, nb_used, xs, ln2_g, wg, wu, wd, tb=tb)
    return _combine(dest, x2d, info, yb, tm=tm)


def _rows8(x):
    return jnp.broadcast_to(x, (8, x.shape[1]))


def _row_hdot(x, m):
    return _hdot(_rows8(x), m)[0:1]


def _bf_round(x):
    return x.astype(BF16).astype(F32)


def _sample_attn_kernel(z_ref, c0, c1, c2, qg_ref, kg_ref, oa_ref, kv_ref):
    scale = SWA_DIM ** -0.5
    ng = len(SWA_CONFIGS)

    def headnorm(zz, g):
        return zz * lax.rsqrt(jnp.mean(zz * zz, axis=-1, keepdims=True) + EPS) * g

    outs, lses = [], []
    for gi, c_ref in enumerate((c0, c1, c2)):
        q = headnorm(z_ref[0, gi], qg_ref[gi:gi + 1, :])
        k = headnorm(z_ref[0, ng + gi], kg_ref[gi:gi + 1, :])
        v = z_ref[0, 2 * ng + gi]
        kv_ref[0, gi, 0] = k
        kv_ref[0, gi, 1] = v
        qr = _bf_round(q)
        s_c = jnp.sum(_bf_round(c_ref[:, 0]) * qr[None], axis=-1, keepdims=True) * scale
        s_n = jnp.sum(_bf_round(k) * qr, axis=-1, keepdims=True) * scale
        m = jnp.maximum(jnp.max(s_c, axis=0), s_n)
        p_c = jnp.exp(s_c - m[None])
        p_n = jnp.exp(s_n - m)
        den = jnp.sum(p_c, axis=0) + p_n
        num = jnp.sum(_bf_round(p_c) * _bf_round(c_ref[:, 1]), axis=0) + _bf_round(p_n) * _bf_round(v)
        outs.append(num / den)
        lses.append(m + jnp.log(den))
    mm = jnp.maximum(jnp.maximum(lses[0], lses[1]), lses[2])
    es = [jnp.exp(l - mm) for l in lses]
    tot = es[0] + es[1] + es[2]
    oa_ref[0] = sum(_bf_round(e / tot) * _bf_round(o) for e, o in zip(es, outs))


def _sample_attn(z4, caches, layer, qg, kg):
    Bs = z4.shape[0]
    ng = len(SWA_CONFIGS)
    cviews, cspecs = [], []
    for (win, dil), c in zip(SWA_CONFIGS, caches):
        assert c.shape[2] == win
        cviews.append(c.reshape(c.shape[0], Bs, win // dil, dil, 2, SWA_HEADS, SWA_DIM))
        cspecs.append(pl.BlockSpec((None, None, SWA_SPAN, None, 2, SWA_HEADS, SWA_DIM),
                                   lambda b: (layer, b, 0, 0, 0, 0, 0)))
    full = lambda a: pl.BlockSpec(a.shape, lambda b: (0,) * a.ndim)
    return pl.pallas_call(
        _sample_attn_kernel,
        grid=(Bs,),
        in_specs=[pl.BlockSpec((1, 3 * ng, SWA_HEADS, SWA_DIM), lambda b: (b, 0, 0, 0))] + cspecs
        + [full(qg), full(kg)],
        out_specs=[pl.BlockSpec((1, SWA_HEADS, SWA_DIM), lambda b: (b, 0, 0)),
                   pl.BlockSpec((1, ng, 2, SWA_HEADS, SWA_DIM), lambda b: (b, 0, 0, 0, 0))],
        out_shape=[jax.ShapeDtypeStruct((Bs, SWA_HEADS, SWA_DIM), F32),
                   jax.ShapeDtypeStruct((Bs, ng, 2, SWA_HEADS, SWA_DIM), F32)],
        compiler_params=_cparams(1),
        name="sample_attn",
    )(z4, *cviews, qg, kg)


def _sample_dn_kernel(raw_ref, cs_ref, cw_ref, ba_ref, par_ref, s_ref, e_ref, etb_ref, etg_ref, o_ref, so_ref):
    E, ETB, ETG = e_ref[...], etb_ref[...], etg_ref[...]
    width = DN_HEADS * DN_DK
    conv = cw_ref[DN_CONV - 1:DN_CONV, :] * raw_ref[0]
    for t in range(DN_CONV - 1):
        conv = conv + cw_ref[t:t + 1, :] * cs_ref[0, t:t + 1, :]
    act = _silu(conv)

    def l2(zz):
        return zz * _row_hdot(lax.rsqrt(_row_hdot(zz * zz, E) + EPS), ETB)

    qn = l2(act[:, 0:width]) * (DN_DK ** -0.5)
    kn = l2(act[:, width:2 * width])
    vn = act[:, 2 * width:3 * width]
    ba = ba_ref[0]
    beta = _row_hdot(_sigmoid(ba), ETB)
    eg = jnp.exp(_row_hdot(par_ref[0:1, :] * _softplus(ba + par_ref[1:2, :]), ETG))
    row0 = lax.broadcasted_iota(jnp.int32, (8, LANES), 0) == 0
    for h in range(DN_HEADS):
        sl = slice(h * LANES, (h + 1) * LANES)
        S = s_ref[0, h]
        q, k, v, b, e = qn[:, sl], kn[:, sl], vn[:, sl], beta[:, sl], eg[:, sl]
        Sr = _bf_round(S)
        v_new = v * b - _row_hdot(_bf_round(k * b * e), Sr)
        a = jnp.sum(q * k, axis=-1, keepdims=True)
        o_ref[0, :, sl] = _row_hdot(_bf_round(q * e), Sr) + a * v_new
        k8 = jnp.where(row0, _rows8(k), 0.0)
        upd = lax.dot_general(k8, _rows8(v_new), (((0,), (0,)), ((), ())), preferred_element_type=F32, precision=HI)
        so_ref[0, h] = S * e + upd


def _sample_dn(raw3, conv_state, s0, layer, conv_w, ba3, par, e_mat, etb, etg):
    Bs, _, C = raw3.shape
    H = DN_HEADS
    full = lambda a: pl.BlockSpec(a.shape, lambda b: (0,) * a.ndim)
    return pl.pallas_call(
        _sample_dn_kernel,
        grid=(Bs,),
        in_specs=[pl.BlockSpec((1, 1, C), lambda b: (b, 0, 0)),
                  pl.BlockSpec((None, 1, DN_CONV - 1, C), lambda b: (layer, b, 0, 0)),
                  full(conv_w),
                  pl.BlockSpec((1, 1, LANES), lambda b: (b, 0, 0)),
                  full(par),
                  pl.BlockSpec((None, 1, H, DN_DK, LANES), lambda b: (layer, b, 0, 0, 0)),
                  full(e_mat), full(etb), full(etg)],
        out_specs=[pl.BlockSpec((1, 1, H * LANES), lambda b: (b, 0, 0)),
                   pl.BlockSpec((1, H, DN_DK, LANES), lambda b: (b, 0, 0, 0))],
        out_shape=[jax.ShapeDtypeStruct((Bs, 1, H * LANES), F32), jax.ShapeDtypeStruct(s0.shape[1:], F32)],
        compiler_params=_cparams(1),
        name="sample_dn",
    )(raw3, conv_state, conv_w, ba3, par, s0, e_mat, etb, etg)


def _sample_out_kernel(x_ref, oa_ref, od_ref, gates_ref, dng_ref, wa_ref, wb_ref, wo_ref, y_ref):
    y_ref[...] = _gated_mix(oa_ref[...], od_ref[...], gates_ref[...], dng_ref[...], wa_ref[...], wb_ref[...],
                            wo_ref[...], x_ref[...], _bdot)


def _sample_out(x2d, oa, od, gates, dng, wa, wb, wo):
    args = (x2d, oa, od, gates, dng, wa, wb, wo)
    return pl.pallas_call(
        _sample_out_kernel,
        grid=(1,),
        in_specs=[pl.BlockSpec(a.shape, lambda i: (0, 0)) for a in args],
        out_specs=pl.BlockSpec(x2d.shape, lambda i: (0, 0)),
        out_shape=jax.ShapeDtypeStruct(x2d.shape, F32),
        compiler_params=_cparams(1),
        name="sample_out",
    )(*args)


def _head_indicator(width, head):
    c = jnp.arange(width)[:, None] // head
    return (c == jnp.arange(LANES)[None, :]).astype(F32)


def _prep_layer(l, ln1_g, w_in, q_norm_g, k_norm_g, dn_conv_w, dn_a_log, dn_dt_bias, dn_norm_g, w_out_a, w_out_b,
                w_o, ln2_g, w_rg, b_rg, w_re, b_re, w_e_gate, w_e_up, w_e_down):
    D = w_in.shape[1]
    a_w = 3 * 3 * SWA_GW
    dn_w = DN_HEADS * 3 * DN_DK
    hv = DN_HEADS * DN_DK
    w = w_in[l]
    splits = dict(att=w[:, :a_w], dn=w[:, a_w:a_w + dn_w],
                  ba=jnp.pad(w[:, a_w + dn_w:a_w + dn_w + 2 * DN_HEADS], ((0, 0), (0, LANES - 2 * DN_HEADS))),
                  gate=w[:, a_w + dn_w + 2 * DN_HEADS:])
    assert splits["gate"].shape[1] == hv + 2 * D
    tile_heads = lambda g: jnp.broadcast_to(g[:, None, :], (len(SWA_CONFIGS), SWA_HEADS, SWA_DIM)).reshape(len(SWA_CONFIGS), SWA_GW)
    qg, kg = tile_heads(q_norm_g[l]), tile_heads(k_norm_g[l])
    idx = jnp.arange(MXU) // SWA_DIM
    par = jnp.zeros((2, LANES), F32)
    par = par.at[0, DN_HEADS:2 * DN_HEADS].set(-jnp.exp(dn_a_log[l].astype(F32)))
    par = par.at[1, DN_HEADS:2 * DN_HEADS].set(dn_dt_bias[l].astype(F32))
    wr = jnp.pad(jnp.concatenate([w_rg[l], w_re[l]], axis=1), ((0, 0), (0, LANES - N_GROUPS - N_EXPERTS)))
    br = jnp.pad(jnp.concatenate([b_rg[l], b_re[l]]), (0, LANES - N_GROUPS - N_EXPERTS)).reshape(1, LANES)
    e8 = _head_indicator(hv, DN_DK)
    return dict(
        bf16={k: v.astype(BF16) for k, v in splits.items()},
        ln1=ln1_g[l].reshape(1, D), ln2=ln2_g[l].reshape(1, D),
        ng=jnp.stack([qg.reshape(1, -1), kg.reshape(1, -1)]), qn=q_norm_g[l], kn=k_norm_g[l],
        bd=(idx[:, None] == idx[None, :]).astype(BF16),
        conv_w=dn_conv_w[l], par=par, dng=dn_norm_g[l].reshape(1, DN_DK),
        wa=w_out_a[l].astype(BF16), wb=w_out_b[l].astype(BF16), wo=w_o[l].astype(BF16), wr=wr.astype(BF16), br=br,
        wg=w_e_gate[l].astype(BF16), wu=w_e_up[l].astype(BF16), wd=w_e_down[l].astype(BF16),
        e_dn=e8, etb=e8.T, etg=jnp.roll(e8, DN_HEADS, axis=1).T,
    )


def _layer_prompt(x, p):
    B, L, D = x.shape
    N = B * L
    x2d = x.reshape(N, D)
    bw = p["bf16"]
    pk0, pk1, pk2, t0, t1, t2 = _proj_attn(x, p["ln1"], bw["att"], p["ng"], p["bd"], tm=min(512, L))
    tmp = min(1024, N)
    raw = _proj_plain(x2d, p["ln1"], bw["dn"], tm=tmp, tn=1536, out_dtype=F32, name="proj_dn")
    gates = _proj_plain(x2d, p["ln1"], bw["gate"], tm=tmp, tn=1536, out_dtype=BF16, name="proj_gate")
    ba = _proj_plain(x2d, p["ln1"], bw["ba"], tm=tmp, tn=LANES, out_dtype=F32, name="proj_ba")
    os_, ls_ = [], []
    for pk in (pk0, pk1, pk2):
        d, M = pk.shape[1], pk.shape[2]
        o, lse = _attn(pk.reshape(B * d, M, pk.shape[3]), tq=min(256, M))
        os_.append(o.reshape(B, d, M, SWA_GW))
        ls_.append(lse.reshape(B, d, M, SWA_GW))
    raw3 = raw.reshape(B, L, -1)
    qd, kd, vd, gb = _dn_conv(raw3, jnp.zeros((B, 8, raw3.shape[2]), F32), p["conv_w"], ba.reshape(B, L, LANES),
                              p["par"], tl=min(256, L))
    u, w, qdec, kdec, a, gt = _dn_intra(qd, kd, vd, gb, tl=min(2048, L))
    od, s_new = _dn_scan(u, w, qdec, kdec, a, gt, jnp.zeros((B, DN_HEADS, DN_DK, LANES), F32), tl=min(1024, L))
    x2 = _out_proj(x2d, os_, ls_, od.reshape(N, -1), gates, p["dng"], p["wa"], p["wb"], p["wo"], B=B, L=L,
                   tm=min(512, L))
    y = _moe(x2, p["ln2"], p["wr"], p["br"], p["wg"], p["wu"], p["wd"], tm=256)
    kv = [t.reshape(B, t.shape[1], 2, SWA_HEADS, SWA_DIM) for t in (t0, t1, t2)]
    return y.reshape(B, L, D), kv, raw3[:, L - (DN_CONV - 1):], s_new


def _layer_sample(x, caches, conv_state, s0, layer, p):
    Bs, T, D = x.shape
    assert T == 1
    x2d = x.reshape(Bs, D)
    bw = p["bf16"]
    proj = functools.partial(_proj_plain, x2d, p["ln1"], tm=Bs, out_dtype=F32)
    z_att = proj(bw["att"], tn=1536, name="sproj_att")
    raw = proj(bw["dn"], tn=1536, name="sproj_dn")
    gates = proj(bw["gate"], tn=1536, name="sproj_gate")
    ba = proj(bw["ba"], tn=LANES, name="sproj_ba")
    oa, kv = _sample_attn(z_att.reshape(Bs, -1, SWA_HEADS, SWA_DIM), caches, layer, p["qn"], p["kn"])
    raw3 = raw.reshape(Bs, 1, -1)
    od, s_new = _sample_dn(raw3, conv_state, s0, layer, p["conv_w"], ba.reshape(Bs, 1, LANES), p["par"],
                           p["e_dn"], p["etb"], p["etg"])
    x2 = _sample_out(x2d, oa.reshape(Bs, -1), od.reshape(Bs, -1), gates, p["dng"], p["wa"], p["wb"], p["wo"])
    y = _moe(x2, p["ln2"], p["wr"], p["br"], p["wg"], p["wu"], p["wd"], tm=Bs)
    kvs = [kv[:, g][:, None] for g in range(len(SWA_CONFIGS))]
    new_conv = jnp.concatenate([conv_state[layer][:, 1:], raw3], axis=1)
    return y.reshape(Bs, 1, D), kvs, new_conv, s_new


def kernel(x_prompt, x_sample, cache_swa0_kv, cache_swa1_kv, cache_swa2_kv, state_dn_conv, state_dn_S, ln1_g, w_in,
           q_norm_g, k_norm_g, dn_conv_w, dn_a_log, dn_dt_bias, dn_norm_g, w_out_a, w_out_b, w_o, ln2_g, w_rg, b_rg,
           w_re, b_re, w_e_gate, w_e_up, w_e_down):
    yp, ys = x_prompt, x_sample
    outs = [[] for _ in range(10)]
    for l in range(w_in.shape[0]):
        p = _prep_layer(l, ln1_g, w_in, q_norm_g, k_norm_g, dn_conv_w, dn_a_log, dn_dt_bias, dn_norm_g, w_out_a,
                        w_out_b, w_o, ln2_g, w_rg, b_rg, w_re, b_re, w_e_gate, w_e_up, w_e_down)
        yp, pkv, pconv, ps = _layer_prompt(yp, p)
        ys, skv, sconv, ss = _layer_sample(ys, (cache_swa0_kv, cache_swa1_kv, cache_swa2_kv), state_dn_conv,
                                           state_dn_S, l, p)
        for lst, val in zip(outs, (*pkv, pconv, ps, *skv, sconv, ss)):
            lst.append(val)
    return (yp, ys, *(jnp.stack(o) for o in outs))
```

```python
import functools

import jax
import jax.numpy as jnp
from jax import lax
from jax.experimental import pallas as pl
from jax.experimental.pallas import tpu as pltpu

F32 = jnp.float32
BF16 = jnp.bfloat16
HI = lax.Precision.HIGHEST
EPS = 1e-6

SWA_CONFIGS = ((128, 1), (512, 4), (2048, 16))
SWA_HEADS = 8
SWA_DIM = 64
SWA_GW = SWA_HEADS * SWA_DIM
SWA_SPAN = 128
DN_HEADS = 8
DN_DK = 128
DN_CONV = 4
DN_CHUNK = 64
N_GROUPS = 4
PER_GROUP = 8
N_EXPERTS = N_GROUPS * PER_GROUP
TOP_K = 2

VMEM_LIMIT_BYTES = 56 * 1024 * 1024
LANES = 128
MXU = 256
MOE_ROWS = 256


def _cparams(n_axes):
    return pltpu.CompilerParams(
        dimension_semantics=("arbitrary",) * n_axes, vmem_limit_bytes=VMEM_LIMIT_BYTES
    )


def _rms(x, g):
    return x * lax.rsqrt(jnp.mean(x * x, axis=-1, keepdims=True) + EPS) * g


def _bdot(a, b):
    return jnp.dot(a.astype(BF16), b.astype(BF16), preferred_element_type=F32)


def _hdot(a, b):
    return jnp.dot(a, b, preferred_element_type=F32, precision=HI)


def _sigmoid(x):
    return 1.0 / (1.0 + jnp.exp(-x))


def _silu(x):
    return x * _sigmoid(x)


def _softplus(x):
    return jnp.maximum(x, 0.0) + jnp.log1p(jnp.exp(-jnp.abs(x)))


def _proj_attn_kernel(x_ref, lng_ref, w_ref, ng_ref, bd_ref, p0, p1, p2, t0, t1, t2, h_scr, z_scr,
                      *, tm, dils, tail_rows, tail_first):
    j = pl.program_id(2)

    @pl.when(j == 0)
    def _():
        h_scr[...] = _rms(x_ref[0], lng_ref[...]).astype(BF16)

    z = jnp.dot(h_scr[...], w_ref[...], preferred_element_type=F32)
    n_cb = z_scr.shape[0]
    for c in range(n_cb):
        z_scr[c] = z[:, c * LANES:(c + 1) * LANES]

    @pl.when(j < 2)
    def _():
        for c in range(0, n_cb, 2):
            zc = jnp.concatenate([z_scr[c], z_scr[c + 1]], axis=-1)
            ss = jnp.dot((zc * zc).astype(BF16), bd_ref[...], preferred_element_type=F32)
            zn = zc * lax.rsqrt(ss * (1.0 / SWA_DIM) + EPS) * ng_ref[0, :, c * LANES:(c + 2) * LANES]
            z_scr[c] = zn[:, :LANES]
            z_scr[c + 1] = zn[:, LANES:]

    outs = (p0, p1, p2)
    tails = (t0, t1, t2)
    per_g = SWA_GW // LANES
    for sec in range(3):

        @pl.when(j == sec)
        def _(sec=sec):
            for gi, d in enumerate(dils):
                for cb in range(per_g):
                    c = gi * per_g + cb
                    col = sec * SWA_GW + cb * LANES
                    for r in range(d):
                        src = z_scr[c] if d == 1 else z_scr[c, pl.ds(r, tm // d, stride=d), :]
                        outs[gi][0, r, :, col:col + LANES] = src.astype(BF16)
                if sec >= 1:
                    @pl.when(pl.program_id(1) >= tail_first[gi])
                    def _(gi=gi):
                        rows = tail_rows[gi]
                        zr = jnp.concatenate([z_scr[gi * per_g + cb, tm - rows:tm, :] for cb in range(per_g)], axis=-1)
                        tails[gi][0, :, sec - 1] = zr.reshape(rows, SWA_HEADS, SWA_DIM)


def _proj_attn(x, ln_g, w_att, ng, bd, *, tm):
    B, L, D = x.shape
    nt = L // tm
    dils = tuple(d for _, d in SWA_CONFIGS)
    keeps = tuple(min(w, L) for w, _ in SWA_CONFIGS)
    tail_rows = tuple(min(k, tm) for k in keeps)
    for k, r in zip(keeps, tail_rows):
        assert k % r == 0 and L % tm == 0
    W3 = 3 * SWA_GW

    tail_first = tuple(nt - k // r for k, r in zip(keeps, tail_rows))

    def tail_spec(rows, first):
        return pl.BlockSpec((1, rows, 2, SWA_HEADS, SWA_DIM),
                            lambda b, i, j: (b, jnp.maximum(i - first, 0), 0, 0, 0))

    out_shape = [jax.ShapeDtypeStruct((B, d, L // d, W3), BF16) for d in dils]
    out_shape += [jax.ShapeDtypeStruct((B, k, 2, SWA_HEADS, SWA_DIM), F32) for k in keeps]
    out_specs = [pl.BlockSpec((1, d, tm // d, W3), lambda b, i, j: (b, 0, i, 0)) for d in dils]
    out_specs += [tail_spec(r, f) for r, f in zip(tail_rows, tail_first)]
    return pl.pallas_call(
        functools.partial(_proj_attn_kernel, tm=tm, dils=dils, tail_rows=tail_rows, tail_first=tail_first),
        grid=(B, nt, 3),
        in_specs=[
            pl.BlockSpec((1, tm, D), lambda b, i, j: (b, i, 0)),
            pl.BlockSpec((1, D), lambda b, i, j: (0, 0)),
            pl.BlockSpec((D, W3), lambda b, i, j: (0, j)),
            pl.BlockSpec((1, 1, W3), lambda b, i, j: (jnp.minimum(j, 1), 0, 0)),
            pl.BlockSpec((MXU, MXU), lambda b, i, j: (0, 0)),
        ],
        out_specs=out_specs,
        out_shape=out_shape,
        scratch_shapes=[pltpu.VMEM((tm, D), BF16), pltpu.VMEM((W3 // LANES, tm, LANES), F32)],
        compiler_params=_cparams(3),
        name="proj_attn",
    )(x, ln_g, w_att, ng, bd)


def _proj_plain_kernel(x_ref, lng_ref, w_ref, o_ref, h_scr):
    @pl.when(pl.program_id(1) == 0)
    def _():
        h_scr[...] = _rms(x_ref[...], lng_ref[...]).astype(BF16)

    o_ref[...] = jnp.dot(h_scr[...], w_ref[...], preferred_element_type=F32).astype(o_ref.dtype)


def _proj_plain(x2d, ln_g, w, *, tm, tn, out_dtype, name="proj_plain"):
    N, D = x2d.shape
    C = w.shape[1]
    assert N % tm == 0 and C % tn == 0
    return pl.pallas_call(
        _proj_plain_kernel,
        grid=(N // tm, C // tn),
        in_specs=[
            pl.BlockSpec((tm, D), lambda i, j: (i, 0)),
            pl.BlockSpec((1, D), lambda i, j: (0, 0)),
            pl.BlockSpec((D, tn), lambda i, j: (0, j)),
        ],
        out_specs=pl.BlockSpec((tm, tn), lambda i, j: (i, j)),
        out_shape=jax.ShapeDtypeStruct((N, C), out_dtype),
        scratch_shapes=[pltpu.VMEM((tm, D), BF16)],
        compiler_params=_cparams(2),
        name=name,
    )(x2d, ln_g, w)


def _attn_kernel(q_ref, kc_ref, vc_ref, kp_ref, vp_ref, o_ref, lse_ref, kk_scr, vv_scr, *, tq):
    i = pl.program_id(1)
    blk = SWA_SPAN
    kk_scr[0:blk, :] = kp_ref[0]
    kk_scr[blk:blk + tq, :] = kc_ref[0]
    vv_scr[0:blk, :] = vp_ref[0]
    vv_scr[blk:blk + tq, :] = vc_ref[0]
    qi = lax.broadcasted_iota(jnp.int32, (blk, 2 * blk), 0)
    ki = lax.broadcasted_iota(jnp.int32, (blk, 2 * blk), 1)
    dist = blk + qi - ki
    band = (dist >= 0) & (dist <= SWA_SPAN)
    band_first = band & ((ki >= blk) | (i > 0))
    lo = lax.broadcasted_iota(jnp.int32, (blk, LANES), 1) < SWA_DIM
    zero = jnp.zeros((blk, LANES), BF16)
    for jb in range(tq // blk):
        mask = band_first if jb == 0 else band
        rows = slice(jb * blk, (jb + 1) * blk)
        for hp in range(SWA_GW // LANES):
            cs = slice(hp * LANES, (hp + 1) * LANES)
            qb = q_ref[0, rows, cs]
            kk = kk_scr[jb * blk:(jb + 2) * blk, cs]
            vv = vv_scr[jb * blk:(jb + 2) * blk, cs]
            res_o, res_l = [], []
            for hh in range(2):
                qm = jnp.where(lo if hh == 0 else jnp.logical_not(lo), qb, zero)
                s = lax.dot_general(qm, kk, (((1,), (1,)), ((), ())), preferred_element_type=F32)
                s = jnp.where(mask, s * (SWA_DIM ** -0.5), -jnp.inf)
                m = jnp.max(s, axis=-1, keepdims=True)
                p = jnp.exp(s - m)
                den = jnp.sum(p, axis=-1, keepdims=True)
                pv = jnp.dot(p.astype(BF16), vv, preferred_element_type=F32)
                res_o.append(pv / den)
                res_l.append(jnp.broadcast_to(m + jnp.log(den), (blk, LANES)))
            o_ref[0, rows, cs] = jnp.where(lo, res_o[0], res_o[1]).astype(BF16)
            lse_ref[0, rows, cs] = jnp.where(lo, res_l[0], res_l[1])


def _attn(p, *, tq):
    S, M, _ = p.shape
    assert M % tq == 0 and tq % SWA_SPAN == 0
    nb = tq // SWA_SPAN
    return pl.pallas_call(
        functools.partial(_attn_kernel, tq=tq),
        grid=(S, M // tq),
        in_specs=[
            pl.BlockSpec((1, tq, SWA_GW), lambda s, i: (s, i, 0)),
            pl.BlockSpec((1, tq, SWA_GW), lambda s, i: (s, i, 1)),
            pl.BlockSpec((1, tq, SWA_GW), lambda s, i: (s, i, 2)),
            pl.BlockSpec((1, SWA_SPAN, SWA_GW), lambda s, i: (s, jnp.maximum(i * nb - 1, 0), 1)),
            pl.BlockSpec((1, SWA_SPAN, SWA_GW), lambda s, i: (s, jnp.maximum(i * nb - 1, 0), 2)),
        ],
        out_specs=[
            pl.BlockSpec((1, tq, SWA_GW), lambda s, i: (s, i, 0)),
            pl.BlockSpec((1, tq, SWA_GW), lambda s, i: (s, i, 0)),
        ],
        out_shape=[
            jax.ShapeDtypeStruct((S, M, SWA_GW), BF16),
            jax.ShapeDtypeStruct((S, M, SWA_GW), F32),
        ],
        scratch_shapes=[
            pltpu.VMEM((SWA_SPAN + tq, SWA_GW), BF16),
            pltpu.VMEM((SWA_SPAN + tq, SWA_GW), BF16),
        ],
        compiler_params=_cparams(2),
        name="swa_attn",
    )(p, p, p, p, p)


def _conv_kernel(x_ref, xp_ref, cp_ref, cw_ref, ba_ref, par_ref, q_ref, k_ref, v_ref, g_ref, xs_scr,
                 *, tl):
    i = pl.program_id(1)
    xs_scr[0:8, :] = jnp.where(i == 0, cp_ref[0], xp_ref[0])
    xs_scr[8:8 + tl, :] = x_ref[0]
    nh = DN_HEADS
    outs = (q_ref, k_ref, v_ref)
    for cb in range(3 * nh):
        cs = slice(cb * LANES, (cb + 1) * LANES)
        acc = cw_ref[0:1, cs] * xs_scr[5:5 + tl, cs]
        for t in range(1, DN_CONV):
            acc = acc + cw_ref[t:t + 1, cs] * xs_scr[5 + t:5 + t + tl, cs]
        act = _silu(acc)
        part, h = divmod(cb, nh)
        if part < 2:
            act = act * lax.rsqrt(jnp.sum(act * act, axis=-1, keepdims=True) + EPS)
        if part == 0:
            act = act * (DN_DK ** -0.5)
        outs[part][0, :, h * LANES:(h + 1) * LANES] = act.astype(BF16)
    ba = ba_ref[0]
    lane = lax.broadcasted_iota(jnp.int32, (tl, LANES), 1)
    g = par_ref[0:1, :] * _softplus(ba + par_ref[1:2, :])
    ri = lax.broadcasted_iota(jnp.int32, (tl, tl), 0)
    ci = lax.broadcasted_iota(jnp.int32, (tl, tl), 1)
    tri = jnp.where((ri // DN_CHUNK == ci // DN_CHUNK) & (ci <= ri), 1.0, 0.0).astype(F32)
    gc = _hdot(tri, g)
    g_ref[0] = jnp.where(lane < nh, _sigmoid(ba), gc)


def _dn_conv(raw, conv_prev8, conv_w, ba, par, *, tl):
    B, L, C = raw.shape
    assert L % tl == 0 and tl % DN_CHUNK == 0
    width = DN_HEADS * DN_DK
    return pl.pallas_call(
        functools.partial(_conv_kernel, tl=tl),
        grid=(B, L // tl),
        in_specs=[
            pl.BlockSpec((1, tl, C), lambda b, i: (b, i, 0)),
            pl.BlockSpec((1, 8, C), lambda b, i: (b, jnp.maximum(i * (tl // 8) - 1, 0), 0)),
            pl.BlockSpec((1, 8, C), lambda b, i: (b, 0, 0)),
            pl.BlockSpec((DN_CONV, C), lambda b, i: (0, 0)),
            pl.BlockSpec((1, tl, LANES), lambda b, i: (b, i, 0)),
            pl.BlockSpec((2, LANES), lambda b, i: (0, 0)),
        ],
        out_specs=[pl.BlockSpec((1, tl, width), lambda b, i: (b, i, 0))] * 3
        + [pl.BlockSpec((1, tl, LANES), lambda b, i: (b, i, 0))],
        out_shape=[jax.ShapeDtypeStruct((B, L, width), BF16)] * 3
        + [jax.ShapeDtypeStruct((B, L, LANES), F32)],
        scratch_shapes=[pltpu.VMEM((8 + tl, C), F32)],
        compiler_params=_cparams(2),
        name="dn_conv",
    )(raw, raw, conv_prev8, conv_w, ba, par)


def _intra_kernel(q_ref, k_ref, v_ref, g_ref, u_ref, w_ref, qd_ref, kd_ref, a_ref, gt_ref, *, tl):
    h = pl.program_id(1)
    C = DN_CHUNK
    lane = lax.broadcasted_iota(jnp.int32, (C, LANES), 1)
    ri = lax.broadcasted_iota(jnp.int32, (C, C), 0)
    ci = lax.broadcasted_iota(jnp.int32, (C, C), 1)
    eye = jnp.where(ri == ci, 1.0, 0.0).astype(F32)
    nt_dot = lambda a, b: lax.dot_general(a.astype(BF16), b.astype(BF16), (((1,), (1,)), ((), ())),
                                          preferred_element_type=F32)
    rows = [slice(c * C, (c + 1) * C) for c in range(tl // C)]
    gv = [g_ref[0, r, :] for r in rows]
    q = [q_ref[0, r, :].astype(F32) for r in rows]
    k = [k_ref[0, r, :].astype(F32) for r in rows]
    v = [v_ref[0, r, :].astype(F32) for r in rows]
    beta = [jnp.sum(jnp.where(lane == h, x, 0.0), axis=-1, keepdims=True) for x in gv]
    gc = [jnp.sum(jnp.where(lane == h + DN_HEADS, x, 0.0), axis=-1, keepdims=True) for x in gv]
    lhs = [jnp.where(lane == 0, x, jnp.where(lane == 1, 1.0, 0.0)) for x in gc]
    rhs = [jnp.where(lane == 0, 1.0, jnp.where(lane == 1, -x, 0.0)) for x in gc]
    diff = [lax.dot_general(a, b, (((1,), (1,)), ((), ())), preferred_element_type=F32, precision=HI)
            for a, b in zip(lhs, rhs)]
    decay = [jnp.exp(jnp.where(ri >= ci, x, -jnp.inf)) for x in diff]
    kb = [a * b for a, b in zip(k, beta)]
    x = [-jnp.where(ri > ci, nt_dot(a, b) * d, 0.0) for a, b, d in zip(kb, k, decay)]
    t = [eye + a for a in x]
    for _ in range(5):
        x = [_bdot(a, a) for a in x]
        t = [a + _bdot(a, b) for a, b in zip(t, x)]
    eg = [jnp.exp(a) for a in gc]
    glast = [a[C - 1:C, :] for a in gc]
    u = [_bdot(a, b * c) for a, b, c in zip(t, v, beta)]
    w = [_bdot(a, b * c) for a, b, c in zip(t, kb, eg)]
    qk = [nt_dot(a, b) for a, b in zip(q, k)]
    for c, r in enumerate(rows):
        u_ref[0, 0, r, :] = u[c]
        w_ref[0, 0, r, :] = w[c].astype(BF16)
        a_ref[0, 0, r, :] = (qk[c] * decay[c]).astype(BF16)
        qd_ref[0, 0, r, :] = (q[c] * eg[c]).astype(BF16)
        kd_ref[0, 0, r, :] = (k[c] * jnp.exp(glast[c] - gc[c])).astype(BF16)
        gt_ref[0, 0, c:c + 1, :] = jnp.broadcast_to(jnp.exp(glast[c]), (1, LANES))


def _dn_intra(q, k, v, g, *, tl):
    B, L, _ = q.shape
    H, C = DN_HEADS, DN_CHUNK
    assert L % tl == 0 and (tl // C) % 8 == 0
    qkv_spec = pl.BlockSpec((1, tl, LANES), lambda b, h, i: (b, i, h))
    hl = lambda w: pl.BlockSpec((1, 1, tl, w), lambda b, h, i: (b, h, i, 0))
    return pl.pallas_call(
        functools.partial(_intra_kernel, tl=tl),
        grid=(B, H, L // tl),
        in_specs=[qkv_spec, qkv_spec, qkv_spec, pl.BlockSpec((1, tl, LANES), lambda b, h, i: (b, i, 0))],
        out_specs=[hl(LANES), hl(LANES), hl(LANES), hl(LANES), hl(C),
                   pl.BlockSpec((1, 1, tl // C, LANES), lambda b, h, i: (b, h, i, 0))],
        out_shape=[
            jax.ShapeDtypeStruct((B, H, L, LANES), F32),
            jax.ShapeDtypeStruct((B, H, L, LANES), BF16),
            jax.ShapeDtypeStruct((B, H, L, LANES), BF16),
            jax.ShapeDtypeStruct((B, H, L, LANES), BF16),
            jax.ShapeDtypeStruct((B, H, L, C), BF16),
            jax.ShapeDtypeStruct((B, H, L // C, LANES), F32),
        ],
        compiler_params=_cparams(3),
        name="dn_intra",
    )(q, k, v, g)


def _scan_kernel(u_ref, w_ref, qd_ref, kd_ref, a_ref, gt_ref, s0_ref, o_ref, s_ref, *, n_chunks):
    C = DN_CHUNK
    H = s_ref.shape[1]

    @pl.when(pl.program_id(1) == 0)
    def _():
        s_ref[...] = s0_ref[...]

    def body(c, carry):
        rows = pl.ds(pl.multiple_of(c * C, C), C)
        S = [s_ref[0, h] for h in range(H)]
        Sb = [x.astype(BF16) for x in S]
        v_new = [u_ref[0, h, rows, :] - jnp.dot(w_ref[0, h, rows, :], Sb[h], preferred_element_type=F32)
                 for h in range(H)]
        vb = [x.astype(BF16) for x in v_new]
        o = [jnp.dot(qd_ref[0, h, rows, :], Sb[h], preferred_element_type=F32)
             + jnp.dot(a_ref[0, h, rows, :], vb[h], preferred_element_type=F32) for h in range(H)]
        upd = [lax.dot_general(kd_ref[0, h, rows, :], vb[h], (((0,), (0,)), ((), ())), preferred_element_type=F32)
               for h in range(H)]
        for h in range(H):
            o_ref[0, rows, h * LANES:(h + 1) * LANES] = o[h]
            s_ref[0, h] = S[h] * gt_ref[0, h, pl.ds(c, 1), :] + upd[h]
        return carry

    lax.fori_loop(0, n_chunks, body, 0)


def _dn_scan(u, w, qd, kd, a, gt, s0, *, tl):
    B, H, L, _ = u.shape
    C = DN_CHUNK
    assert L % tl == 0 and (tl // C) % 8 == 0
    hs = lambda wd: pl.BlockSpec((1, H, tl, wd), lambda b, i: (b, 0, i, 0))
    s_spec = pl.BlockSpec((1, H, DN_DK, LANES), lambda b, i: (b, 0, 0, 0))
    return pl.pallas_call(
        functools.partial(_scan_kernel, n_chunks=tl // C),
        grid=(B, L // tl),
        in_specs=[hs(LANES), hs(LANES), hs(LANES), hs(LANES), hs(C),
                  pl.BlockSpec((1, H, tl // C, LANES), lambda b, i: (b, 0, i, 0)), s_spec],
        out_specs=[pl.BlockSpec((1, tl, H * LANES), lambda b, i: (b, i, 0)), s_spec],
        out_shape=[jax.ShapeDtypeStruct((B, L, H * LANES), F32),
                   jax.ShapeDtypeStruct((B, H, DN_DK, LANES), F32)],
        compiler_params=_cparams(2),
        name="dn_scan",
    )(u, w, qd, kd, a, gt, s0)


def _gated_mix(o_a, od, gates, dng, wa, wb, wo, x, dot):
    width = DN_HEADS * DN_DK
    parts = []
    for h in range(DN_HEADS):
        blk = od[:, h * LANES:(h + 1) * LANES]
        parts.append(blk * lax.rsqrt(jnp.mean(blk * blk, axis=-1, keepdims=True) + EPS) * dng)
    odn = jnp.concatenate(parts, axis=-1) * _silu(gates[:, 0:width].astype(F32))
    ya = dot(o_a, wa)
    yb = dot(odn, wb)
    mix = _sigmoid(gates[:, width:2 * width].astype(F32)) * ya + _sigmoid(gates[:, 2 * width:].astype(F32)) * yb
    return x + dot(mix, wo)


def _out_kernel(x_ref, o0, o1, o2, l0, l1, l2, od_ref, gates_ref, dng_ref, wa_ref, wb_ref, wo_ref, y_ref,
                so0, so1, so2, sl0, sl1, sl2, *, tm, dils):
    o_refs, l_refs = (o0, o1, o2), (l0, l1, l2)
    so, sl = (so0, so1, so2), (sl0, sl1, sl2)
    parts = []
    for cb in range(SWA_GW // LANES):
        cs = slice(cb * LANES, (cb + 1) * LANES)
        for gi, d in enumerate(dils):
            for r in range(d):
                dst = slice(None) if d == 1 else pl.ds(r, tm // d, stride=d)
                so[gi][cb, dst, :] = o_refs[gi][0, r, :, cs].astype(F32)
                sl[gi][cb, dst, :] = l_refs[gi][0, r, :, cs]
        ls = [s[cb] for s in sl]
        m = jnp.maximum(jnp.maximum(ls[0], ls[1]), ls[2])
        es = [jnp.exp(l - m) for l in ls]
        parts.append((es[0] * so[0][cb] + es[1] * so[1][cb] + es[2] * so[2][cb]) / (es[0] + es[1] + es[2]))
    o_a = jnp.concatenate(parts, axis=-1)
    y_ref[...] = _gated_mix(o_a, od_ref[...], gates_ref[...], dng_ref[...], wa_ref[...], wb_ref[...],
                            wo_ref[...], x_ref[...], _bdot)


def _out_proj(x2d, os_, ls_, od2d, gates, dng, wa, wb, wo, *, B, L, tm):
    N, D = x2d.shape
    nt = L // tm
    dils = tuple(d for _, d in SWA_CONFIGS)
    grp = lambda d: pl.BlockSpec((1, d, tm // d, SWA_GW), lambda i: (i // nt, 0, i % nt, 0))
    row = lambda w: pl.BlockSpec((tm, w), lambda i: (i, 0))
    full = lambda a: pl.BlockSpec(a.shape, lambda i: (0, 0))
    return pl.pallas_call(
        functools.partial(_out_kernel, tm=tm, dils=dils),
        grid=(N // tm,),
        in_specs=[row(D)] + [grp(d) for d in dils] * 2 + [row(od2d.shape[1]), row(gates.shape[1]),
                                                          full(dng), full(wa), full(wb), full(wo)],
        out_specs=row(D),
        out_shape=jax.ShapeDtypeStruct((N, D), F32),
        scratch_shapes=[pltpu.VMEM((SWA_GW // LANES, tm, LANES), F32)] * 6,
        compiler_params=_cparams(1),
        name="out_proj",
    )(x2d, *os_, *ls_, od2d, gates, dng, wa, wb, wo)


def _router_kernel(x_ref, lng_ref, wr_ref, br_ref, info_ref, cnt_ref, base_scr, *, tm):
    i = pl.program_id(0)

    @pl.when(i == 0)
    def _():
        base_scr[...] = jnp.zeros_like(base_scr)

    h = _rms(x_ref[...], lng_ref[...])
    lg = _bdot(h, wr_ref[...]) + br_ref[...]
    lane = lax.broadcasted_iota(jnp.int32, (tm, LANES), 1)
    big = jnp.int32(1 << 20)
    ninf = -jnp.inf

    def argmax_lane(vals):
        mx = jnp.max(vals, axis=-1, keepdims=True)
        idx = jnp.min(jnp.where(vals == mx, lane, big), axis=-1, keepdims=True)
        return mx, idx

    lgm = jnp.where(lane < N_GROUPS, lg, ninf)
    mg, gsel = argmax_lane(lgm)
    pg = 1.0 / jnp.sum(jnp.exp(lgm - mg), axis=-1, keepdims=True)
    start = N_GROUPS + gsel * PER_GROUP
    le = jnp.where((lane >= start) & (lane < start + PER_GROUP), lg, ninf)
    m1, i1 = argmax_lane(le)
    m2, i2 = argmax_lane(jnp.where(lane == i1, ninf, le))
    e21 = jnp.exp(m2 - m1)
    w1 = pg / (1.0 + e21)
    w2 = pg * e21 / (1.0 + e21)
    oh = jnp.where(lane == i1, 1.0, 0.0) + jnp.where(lane == i2, 1.0, 0.0)
    ri = lax.broadcasted_iota(jnp.int32, (tm, tm), 0)
    ci = lax.broadcasted_iota(jnp.int32, (tm, tm), 1)
    strict = jnp.where(ci < ri, 1.0, 0.0).astype(BF16)
    pref = jnp.dot(strict, oh.astype(BF16), preferred_element_type=F32) + base_scr[...]
    r1 = jnp.sum(jnp.where(lane == i1, pref, 0.0), axis=-1, keepdims=True)
    r2 = jnp.sum(jnp.where(lane == i2, pref, 0.0), axis=-1, keepdims=True)
    base_scr[...] = base_scr[...] + jnp.sum(oh, axis=0, keepdims=True)
    cnt_ref[...] = base_scr[...]
    off = jnp.float32(N_GROUPS)
    info = jnp.where(lane == 0, i1.astype(F32) - off, 0.0)
    info = jnp.where(lane == 1, i2.astype(F32) - off, info)
    info = jnp.where(lane == 2, w1, info)
    info = jnp.where(lane == 3, w2, info)
    info = jnp.where(lane == 4, r1, info)
    info = jnp.where(lane == 5, r2, info)
    info_ref[...] = info


def _router(x2d, ln_g, wr, br, *, tm):
    N, D = x2d.shape
    assert N % tm == 0
    return pl.pallas_call(
        functools.partial(_router_kernel, tm=tm),
        grid=(N // tm,),
        in_specs=[
            pl.BlockSpec((tm, D), lambda i: (i, 0)),
            pl.BlockSpec((1, D), lambda i: (0, 0)),
            pl.BlockSpec((D, LANES), lambda i: (0, 0)),
            pl.BlockSpec((1, LANES), lambda i: (0, 0)),
        ],
        out_specs=[pl.BlockSpec((tm, LANES), lambda i: (i, 0)), pl.BlockSpec((1, LANES), lambda i: (0, 0))],
        out_shape=[jax.ShapeDtypeStruct((N, LANES), F32), jax.ShapeDtypeStruct((1, LANES), F32)],
        scratch_shapes=[pltpu.VMEM((1, LANES), F32)],
        compiler_params=_cparams(1),
        name="router",
    )(x2d, ln_g, wr, br)


def _dispatch_kernel(dest_ref, zb_ref, x_ref, xs_ref, zero_scr, rows_scr, sem, *, tm, tb, n_zb, n_tiles):
    i = pl.program_id(0)

    @pl.when(i == 0)
    def _():
        zero_scr[...] = jnp.zeros_like(zero_scr)

        def zero_copy(n):
            return pltpu.make_async_copy(zero_scr, xs_ref.at[pl.ds(zb_ref[n] * tb, tb)], sem.at[2])

        def zero_issue(n, carry):
            @pl.when(zb_ref[n] >= 0)
            def _():
                zero_copy(n).start()

            return carry

        def zero_wait(n, carry):
            @pl.when(zb_ref[n] >= 0)
            def _():
                zero_copy(n).wait()

            return carry

        lax.fori_loop(0, n_zb, zero_issue, 0)
        lax.fori_loop(0, n_zb, zero_wait, 0)

    buf_now = lax.rem(i, 2)
    rows_scr[buf_now] = x_ref[...].reshape(rows_scr.shape[1:])

    def row_copy(tile, t, slot):
        buf = lax.rem(tile, 2)
        return pltpu.make_async_copy(
            rows_scr.at[buf, pl.ds(t, 1)],
            xs_ref.at[pl.ds(dest_ref[(tile * tm + t) * TOP_K + slot], 1)], sem.at[buf])

    def issue(t, carry):
        for slot in range(TOP_K):
            row_copy(i, t, slot).start()
        return carry

    def drain(tile):
        def body(t, carry):
            for slot in range(TOP_K):
                row_copy(tile, t, slot).wait()
            return carry

        lax.fori_loop(0, tm, body, 0)

    lax.fori_loop(0, tm, issue, 0)

    @pl.when(i > 0)
    def _():
        drain(i - 1)

    @pl.when(i == n_tiles - 1)
    def _():
        drain(i)


def _dispatch(dest, zero_blocks, x2d, *, tm, tb, n_rows):
    N, D = x2d.shape
    return pl.pallas_call(
        functools.partial(_dispatch_kernel, tm=tm, tb=tb, n_zb=zero_blocks.shape[0], n_tiles=N // tm),
        grid_spec=pltpu.PrefetchScalarGridSpec(
            num_scalar_prefetch=2,
            grid=(N // tm,),
            in_specs=[pl.BlockSpec((tm, D), lambda i, d, z: (i, 0))],
            out_specs=pl.BlockSpec(memory_space=pl.ANY),
            scratch_shapes=[pltpu.VMEM((tb, D // LANES, LANES), F32), pltpu.VMEM((2, tm, D // LANES, LANES), F32),
                            pltpu.SemaphoreType.DMA((3,))],
        ),
        out_shape=jax.ShapeDtypeStruct((n_rows, D // LANES, LANES), F32),
        compiler_params=_cparams(1),
        name="moe_dispatch",
    )(dest, zero_blocks, x2d)


def _ffn_kernel(be_ref, nb_ref, xs_ref, lng_ref, wg_ref, wu_ref, wd_ref, y_ref):
    used = pl.program_id(0) < nb_ref[0]
    tb = xs_ref.shape[0]

    @pl.when(used)
    def _():
        h = _rms(xs_ref[...].reshape(tb, -1), lng_ref[...]).astype(BF16)
        g = jnp.dot(h, wg_ref[0], preferred_element_type=F32)
        u = jnp.dot(h, wu_ref[0], preferred_element_type=F32)
        y = jnp.dot((_silu(g) * u).astype(BF16), wd_ref[0], preferred_element_type=F32)
        y_ref[...] = y.reshape(y_ref.shape)

    @pl.when(jnp.logical_not(used))
    def _():
        y_ref[...] = jnp.zeros_like(y_ref)


def _ffn(blk_e, nb_used, xs, ln_g, wg, wu, wd, *, tb):
    P, S, _ = xs.shape
    D = S * LANES
    nb = P // tb
    DE = wg.shape[2]
    return pl.pallas_call(
        _ffn_kernel,
        grid_spec=pltpu.PrefetchScalarGridSpec(
            num_scalar_prefetch=2,
            grid=(nb,),
            in_specs=[
                pl.BlockSpec((tb, S, LANES), lambda i, be, nbu: (jnp.minimum(i, nbu[0] - 1), 0, 0)),
                pl.BlockSpec((1, D), lambda i, be, nbu: (0, 0)),
                pl.BlockSpec((1, D, DE), lambda i, be, nbu: (be[i], 0, 0)),
                pl.BlockSpec((1, D, DE), lambda i, be, nbu: (be[i], 0, 0)),
                pl.BlockSpec((1, DE, D), lambda i, be, nbu: (be[i], 0, 0)),
            ],
            out_specs=pl.BlockSpec((tb, S, LANES), lambda i, be, nbu: (i, 0, 0)),
        ),
        out_shape=jax.ShapeDtypeStruct((P, S, LANES), F32),
        compiler_params=_cparams(1),
        name="moe_ffn",
    )(blk_e, nb_used, xs, ln_g, wg, wu, wd)


def _combine_kernel(dest_ref, x_ref, info_ref, yb_ref, y_ref, g_scr, sem, *, tm, n_tiles):
    i = pl.program_id(0)

    def row_copy(tile, t, slot):
        buf = lax.rem(tile, 2)
        return pltpu.make_async_copy(
            yb_ref.at[pl.ds(dest_ref[(tile * tm + t) * TOP_K + slot], 1)],
            g_scr.at[buf, slot, pl.ds(t, 1)], sem.at[buf])

    def for_rows(tile, fn):
        def body(t, carry):
            for slot in range(TOP_K):
                fn(row_copy(tile, t, slot))
            return carry

        lax.fori_loop(0, tm, body, 0)

    @pl.when(i == 0)
    def _():
        for_rows(i, lambda c: c.start())

    @pl.when(i + 1 < n_tiles)
    def _():
        for_rows(i + 1, lambda c: c.start())

    for_rows(i, lambda c: c.wait())
    buf = lax.rem(i, 2)
    info = info_ref[...]
    lane = lax.broadcasted_iota(jnp.int32, info.shape, 1)
    w1 = jnp.sum(jnp.where(lane == 2, info, 0.0), axis=-1, keepdims=True)
    w2 = jnp.sum(jnp.where(lane == 3, info, 0.0), axis=-1, keepdims=True)
    g1 = g_scr[buf, 0].reshape(x_ref.shape)
    g2 = g_scr[buf, 1].reshape(x_ref.shape)
    y_ref[...] = x_ref[...] + (w1 * g1 + w2 * g2)


def _combine(dest, x2d, info, yb, *, tm):
    N, D = x2d.shape
    return pl.pallas_call(
        functools.partial(_combine_kernel, tm=tm, n_tiles=N // tm),
        grid_spec=pltpu.PrefetchScalarGridSpec(
            num_scalar_prefetch=1,
            grid=(N // tm,),
            in_specs=[
                pl.BlockSpec((tm, D), lambda i, d: (i, 0)),
                pl.BlockSpec((tm, LANES), lambda i, d: (i, 0)),
                pl.BlockSpec(memory_space=pl.ANY),
            ],
            out_specs=pl.BlockSpec((tm, D), lambda i, d: (i, 0)),
            scratch_shapes=[pltpu.VMEM((2, TOP_K, tm, D // LANES, LANES), F32), pltpu.SemaphoreType.DMA((2,))],
        ),
        out_shape=jax.ShapeDtypeStruct((N, D), F32),
        compiler_params=_cparams(1),
        name="moe_combine",
    )(dest, x2d, info, yb)


def _moe(x2d, ln2_g, wr, br, wg, wu, wd, *, tm):
    N, D = x2d.shape
    tb = MOE_ROWS
    info, counts = _router(x2d, ln2_g, wr, br, tm=tm)
    counts = counts[0, N_GROUPS:N_GROUPS + N_EXPERTS].astype(jnp.int32)
    pcounts = (counts + tb - 1) // tb * tb
    pend = jnp.cumsum(pcounts)
    pstart = pend - pcounts
    e = info[:, 0:TOP_K].astype(jnp.int32)
    rank = info[:, 4:4 + TOP_K].astype(jnp.int32)
    dest = (pstart[e] + rank).reshape(-1)
    nb = -(-(N * TOP_K) // tb) + N_EXPERTS
    P = nb * tb
    blocks = jnp.arange(nb, dtype=jnp.int32)
    blk_e = jnp.minimum(jnp.sum((pend[None, :] <= blocks[:, None] * tb).astype(jnp.int32), axis=1), N_EXPERTS - 1)
    nb_used = (pend[-1] // tb).astype(jnp.int32).reshape(1)
    zero_blocks = jnp.concatenate([jnp.where(counts % tb != 0, pend // tb - 1, -1),
                                   jnp.where(blocks >= nb_used[0], blocks, -1)]).astype(jnp.int32)
    xs = _dispatch(dest, zero_blocks, x2d, tm=tm, tb=tb, n_rows=P)
    yb = _ffn(blk_e, nb_used, xs, ln2_g, wg, wu, wd, tb=tb)
    return _combine(dest, x2d, info, yb, tm=tm)


def _rows8(x):
    return jnp.broadcast_to(x, (8, x.shape[1]))


def _row_hdot(x, m):
    return _hdot(_rows8(x), m)[0:1]


def _bf_round(x):
    return x.astype(BF16).astype(F32)


def _sample_attn_kernel(z_ref, c0, c1, c2, qg_ref, kg_ref, oa_ref, kv_ref):
    scale = SWA_DIM ** -0.5
    ng = len(SWA_CONFIGS)

    def headnorm(zz, g):
        return zz * lax.rsqrt(jnp.mean(zz * zz, axis=-1, keepdims=True) + EPS) * g

    outs, lses = [], []
    for gi, c_ref in enumerate((c0, c1, c2)):
        q = headnorm(z_ref[0, gi], qg_ref[gi:gi + 1, :])
        k = headnorm(z_ref[0, ng + gi], kg_ref[gi:gi + 1, :])
        v = z_ref[0, 2 * ng + gi]
        kv_ref[0, gi, 0] = k
        kv_ref[0, gi, 1] = v
        qr = _bf_round(q)
        s_c = jnp.sum(_bf_round(c_ref[:, 0]) * qr[None], axis=-1, keepdims=True) * scale
        s_n = jnp.sum(_bf_round(k) * qr, axis=-1, keepdims=True) * scale
        m = jnp.maximum(jnp.max(s_c, axis=0), s_n)
        p_c = jnp.exp(s_c - m[None])
        p_n = jnp.exp(s_n - m)
        den = jnp.sum(p_c, axis=0) + p_n
        num = jnp.sum(_bf_round(p_c) * _bf_round(c_ref[:, 1]), axis=0) + _bf_round(p_n) * _bf_round(v)
        outs.append(num / den)
        lses.append(m + jnp.log(den))
    mm = jnp.maximum(jnp.maximum(lses[0], lses[1]), lses[2])
    es = [jnp.exp(l - mm) for l in lses]
    tot = es[0] + es[1] + es[2]
    oa_ref[0] = sum(_bf_round(e / tot) * _bf_round(o) for e, o in zip(es, outs))


def _sample_attn(z4, caches, layer, qg, kg):
    Bs = z4.shape[0]
    ng = len(SWA_CONFIGS)
    cviews, cspecs = [], []
    for (win, dil), c in zip(SWA_CONFIGS, caches):
        assert c.shape[2] == win
        cviews.append(c.reshape(c.shape[0], Bs, win // dil, dil, 2, SWA_HEADS, SWA_DIM))
        cspecs.append(pl.BlockSpec((None, None, SWA_SPAN, None, 2, SWA_HEADS, SWA_DIM),
                                   lambda b: (layer, b, 0, 0, 0, 0, 0)))
    full = lambda a: pl.BlockSpec(a.shape, lambda b: (0,) * a.ndim)
    return pl.pallas_call(
        _sample_attn_kernel,
        grid=(Bs,),
        in_specs=[pl.BlockSpec((1, 3 * ng, SWA_HEADS, SWA_DIM), lambda b: (b, 0, 0, 0))] + cspecs
        + [full(qg), full(kg)],
        out_specs=[pl.BlockSpec((1, SWA_HEADS, SWA_DIM), lambda b: (b, 0, 0)),
                   pl.BlockSpec((1, ng, 2, SWA_HEADS, SWA_DIM), lambda b: (b, 0, 0, 0, 0))],
        out_shape=[jax.ShapeDtypeStruct((Bs, SWA_HEADS, SWA_DIM), F32),
                   jax.ShapeDtypeStruct((Bs, ng, 2, SWA_HEADS, SWA_DIM), F32)],
        compiler_params=_cparams(1),
        name="sample_attn",
    )(z4, *cviews, qg, kg)


def _sample_dn_kernel(raw_ref, cs_ref, cw_ref, ba_ref, par_ref, s_ref, e_ref, etb_ref, etg_ref, o_ref, so_ref):
    E, ETB, ETG = e_ref[...], etb_ref[...], etg_ref[...]
    width = DN_HEADS * DN_DK
    conv = cw_ref[DN_CONV - 1:DN_CONV, :] * raw_ref[0]
    for t in range(DN_CONV - 1):
        conv = conv + cw_ref[t:t + 1, :] * cs_ref[0, t:t + 1, :]
    act = _silu(conv)

    def l2(zz):
        return zz * _row_hdot(lax.rsqrt(_row_hdot(zz * zz, E) + EPS), ETB)

    qn = l2(act[:, 0:width]) * (DN_DK ** -0.5)
    kn = l2(act[:, width:2 * width])
    vn = act[:, 2 * width:3 * width]
    ba = ba_ref[0]
    beta = _row_hdot(_sigmoid(ba), ETB)
    eg = jnp.exp(_row_hdot(par_ref[0:1, :] * _softplus(ba + par_ref[1:2, :]), ETG))
    row0 = lax.broadcasted_iota(jnp.int32, (8, LANES), 0) == 0
    for h in range(DN_HEADS):
        sl = slice(h * LANES, (h + 1) * LANES)
        S = s_ref[0, h]
        q, k, v, b, e = qn[:, sl], kn[:, sl], vn[:, sl], beta[:, sl], eg[:, sl]
        Sr = _bf_round(S)
        v_new = v * b - _row_hdot(_bf_round(k * b * e), Sr)
        a = jnp.sum(q * k, axis=-1, keepdims=True)
        o_ref[0, :, sl] = _row_hdot(_bf_round(q * e), Sr) + a * v_new
        k8 = jnp.where(row0, _rows8(k), 0.0)
        upd = lax.dot_general(k8, _rows8(v_new), (((0,), (0,)), ((), ())), preferred_element_type=F32, precision=HI)
        so_ref[0, h] = S * e + upd


def _sample_dn(raw3, conv_state, s0, layer, conv_w, ba3, par, e_mat, etb, etg):
    Bs, _, C = raw3.shape
    H = DN_HEADS
    full = lambda a: pl.BlockSpec(a.shape, lambda b: (0,) * a.ndim)
    return pl.pallas_call(
        _sample_dn_kernel,
        grid=(Bs,),
        in_specs=[pl.BlockSpec((1, 1, C), lambda b: (b, 0, 0)),
                  pl.BlockSpec((None, 1, DN_CONV - 1, C), lambda b: (layer, b, 0, 0)),
                  full(conv_w),
                  pl.BlockSpec((1, 1, LANES), lambda b: (b, 0, 0)),
                  full(par),
                  pl.BlockSpec((None, 1, H, DN_DK, LANES), lambda b: (layer, b, 0, 0, 0)),
                  full(e_mat), full(etb), full(etg)],
        out_specs=[pl.BlockSpec((1, 1, H * LANES), lambda b: (b, 0, 0)),
                   pl.BlockSpec((1, H, DN_DK, LANES), lambda b: (b, 0, 0, 0))],
        out_shape=[jax.ShapeDtypeStruct((Bs, 1, H * LANES), F32), jax.ShapeDtypeStruct(s0.shape[1:], F32)],
        compiler_params=_cparams(1),
        name="sample_dn",
    )(raw3, conv_state, conv_w, ba3, par, s0, e_mat, etb, etg)


def _sample_out_kernel(x_ref, oa_ref, od_ref, gates_ref, dng_ref, wa_ref, wb_ref, wo_ref, y_ref):
    y_ref[...] = _gated_mix(oa_ref[...], od_ref[...], gates_ref[...], dng_ref[...], wa_ref[...], wb_ref[...],
                            wo_ref[...], x_ref[...], _bdot)


def _sample_out(x2d, oa, od, gates, dng, wa, wb, wo):
    args = (x2d, oa, od, gates, dng, wa, wb, wo)
    return pl.pallas_call(
        _sample_out_kernel,
        grid=(1,),
        in_specs=[pl.BlockSpec(a.shape, lambda i: (0, 0)) for a in args],
        out_specs=pl.BlockSpec(x2d.shape, lambda i: (0, 0)),
        out_shape=jax.ShapeDtypeStruct(x2d.shape, F32),
        compiler_params=_cparams(1),
        name="sample_out",
    )(*args)


def _head_indicator(width, head):
    c = jnp.arange(width)[:, None] // head
    return (c == jnp.arange(LANES)[None, :]).astype(F32)


def _prep_layer(l, ln1_g, w_in, q_norm_g, k_norm_g, dn_conv_w, dn_a_log, dn_dt_bias, dn_norm_g, w_out_a, w_out_b,
                w_o, ln2_g, w_rg, b_rg, w_re, b_re, w_e_gate, w_e_up, w_e_down):
    D = w_in.shape[1]
    a_w = 3 * 3 * SWA_GW
    dn_w = DN_HEADS * 3 * DN_DK
    hv = DN_HEADS * DN_DK
    w = w_in[l]
    splits = dict(att=w[:, :a_w], dn=w[:, a_w:a_w + dn_w],
                  ba=jnp.pad(w[:, a_w + dn_w:a_w + dn_w + 2 * DN_HEADS], ((0, 0), (0, LANES - 2 * DN_HEADS))),
                  gate=w[:, a_w + dn_w + 2 * DN_HEADS:])
    assert splits["gate"].shape[1] == hv + 2 * D
    tile_heads = lambda g: jnp.broadcast_to(g[:, None, :], (len(SWA_CONFIGS), SWA_HEADS, SWA_DIM)).reshape(len(SWA_CONFIGS), SWA_GW)
    qg, kg = tile_heads(q_norm_g[l]), tile_heads(k_norm_g[l])
    idx = jnp.arange(MXU) // SWA_DIM
    par = jnp.zeros((2, LANES), F32)
    par = par.at[0, DN_HEADS:2 * DN_HEADS].set(-jnp.exp(dn_a_log[l].astype(F32)))
    par = par.at[1, DN_HEADS:2 * DN_HEADS].set(dn_dt_bias[l].astype(F32))
    wr = jnp.pad(jnp.concatenate([w_rg[l], w_re[l]], axis=1), ((0, 0), (0, LANES - N_GROUPS - N_EXPERTS)))
    br = jnp.pad(jnp.concatenate([b_rg[l], b_re[l]]), (0, LANES - N_GROUPS - N_EXPERTS)).reshape(1, LANES)
    e8 = _head_indicator(hv, DN_DK)
    return dict(
        bf16={k: v.astype(BF16) for k, v in splits.items()},
        ln1=ln1_g[l].reshape(1, D), ln2=ln2_g[l].reshape(1, D),
        ng=jnp.stack([qg.reshape(1, -1), kg.reshape(1, -1)]), qn=q_norm_g[l], kn=k_norm_g[l],
        bd=(idx[:, None] == idx[None, :]).astype(BF16),
        conv_w=dn_conv_w[l], par=par, dng=dn_norm_g[l].reshape(1, DN_DK),
        wa=w_out_a[l].astype(BF16), wb=w_out_b[l].astype(BF16), wo=w_o[l].astype(BF16), wr=wr.astype(BF16), br=br,
        wg=w_e_gate[l].astype(BF16), wu=w_e_up[l].astype(BF16), wd=w_e_down[l].astype(BF16),
        e_dn=e8, etb=e8.T, etg=jnp.roll(e8, DN_HEADS, axis=1).T,
    )


def _layer_prompt(x, p):
    B, L, D = x.shape
    N = B * L
    x2d = x.reshape(N, D)
    bw = p["bf16"]
    pk0, pk1, pk2, t0, t1, t2 = _proj_attn(x, p["ln1"], bw["att"], p["ng"], p["bd"], tm=min(512, L))
    tmp = min(1024, N)
    raw = _proj_plain(x2d, p["ln1"], bw["dn"], tm=tmp, tn=1536, out_dtype=F32, name="proj_dn")
    gates = _proj_plain(x2d, p["ln1"], bw["gate"], tm=tmp, tn=1536, out_dtype=BF16, name="proj_gate")
    ba = _proj_plain(x2d, p["ln1"], bw["ba"], tm=tmp, tn=LANES, out_dtype=F32, name="proj_ba")
    os_, ls_ = [], []
    for pk in (pk0, pk1, pk2):
        d, M = pk.shape[1], pk.shape[2]
        o, lse = _attn(pk.reshape(B * d, M, pk.shape[3]), tq=min(256, M))
        os_.append(o.reshape(B, d, M, SWA_GW))
        ls_.append(lse.reshape(B, d, M, SWA_GW))
    raw3 = raw.reshape(B, L, -1)
    qd, kd, vd, gb = _dn_conv(raw3, jnp.zeros((B, 8, raw3.shape[2]), F32), p["conv_w"], ba.reshape(B, L, LANES),
                              p["par"], tl=min(256, L))
    u, w, qdec, kdec, a, gt = _dn_intra(qd, kd, vd, gb, tl=min(2048, L))
    od, s_new = _dn_scan(u, w, qdec, kdec, a, gt, jnp.zeros((B, DN_HEADS, DN_DK, LANES), F32), tl=min(1024, L))
    x2 = _out_proj(x2d, os_, ls_, od.reshape(N, -1), gates, p["dng"], p["wa"], p["wb"], p["wo"], B=B, L=L,
                   tm=min(512, L))
    y = _moe(x2, p["ln2"], p["wr"], p["br"], p["wg"], p["wu"], p["wd"], tm=256)
    return y.reshape(B, L, D), [t0, t1, t2], raw3[:, L - (DN_CONV - 1):], s_new


def _layer_sample(x, caches, conv_state, s0, layer, p):
    Bs, T, D = x.shape
    assert T == 1
    x2d = x.reshape(Bs, D)
    bw = p["bf16"]
    proj = functools.partial(_proj_plain, x2d, p["ln1"], tm=Bs, out_dtype=F32)
    z_att = proj(bw["att"], tn=1536, name="sproj_att")
    raw = proj(bw["dn"], tn=1536, name="sproj_dn")
    gates = proj(bw["gate"], tn=1536, name="sproj_gate")
    ba = proj(bw["ba"], tn=LANES, name="sproj_ba")
    oa, kv = _sample_attn(z_att.reshape(Bs, -1, SWA_HEADS, SWA_DIM), caches, layer, p["qn"], p["kn"])
    raw3 = raw.reshape(Bs, 1, -1)
    od, s_new = _sample_dn(raw3, conv_state, s0, layer, p["conv_w"], ba.reshape(Bs, 1, LANES), p["par"],
                           p["e_dn"], p["etb"], p["etg"])
    x2 = _sample_out(x2d, oa.reshape(Bs, -1), od.reshape(Bs, -1), gates, p["dng"], p["wa"], p["wb"], p["wo"])
    y = _moe(x2, p["ln2"], p["wr"], p["br"], p["wg"], p["wu"], p["wd"], tm=Bs)
    kvs = [kv[:, g][:, None] for g in range(len(SWA_CONFIGS))]
    new_conv = jnp.concatenate([conv_state[layer][:, 1:], raw3], axis=1)
    return y.reshape(Bs, 1, D), kvs, new_conv, s_new


def kernel(x_prompt, x_sample, cache_swa0_kv, cache_swa1_kv, cache_swa2_kv, state_dn_conv, state_dn_S, ln1_g, w_in,
           q_norm_g, k_norm_g, dn_conv_w, dn_a_log, dn_dt_bias, dn_norm_g, w_out_a, w_out_b, w_o, ln2_g, w_rg, b_rg,
           w_re, b_re, w_e_gate, w_e_up, w_e_down):
    yp, ys = x_prompt, x_sample
    outs = [[] for _ in range(10)]
    for l in range(w_in.shape[0]):
        p = _prep_layer(l, ln1_g, w_in, q_norm_g, k_norm_g, dn_conv_w, dn_a_log, dn_dt_bias, dn_norm_g, w_out_a,
                        w_out_b, w_o, ln2_g, w_rg, b_rg, w_re, b_re, w_e_gate, w_e_up, w_e_down)
        yp, pkv, pconv, ps = _layer_prompt(yp, p)
        ys, skv, sconv, ss = _layer_sample(ys, (cache_swa0_kv, cache_swa1_kv, cache_swa2_kv), state_dn_conv,
                                           state_dn_S, l, p)
        for lst, val in zip(outs, (*pkv, pconv, ps, *skv, sconv, ss)):
            lst.append(val)
    return (yp, ys, *(jnp.stack(o) for o in outs))
```

```python
import functools

import jax
import jax.numpy as jnp
from jax import lax
from jax.experimental import pallas as pl
from jax.experimental.pallas import tpu as pltpu

F32 = jnp.float32
BF16 = jnp.bfloat16
HI = lax.Precision.HIGHEST
EPS = 1e-6

SWA_CONFIGS = ((128, 1), (512, 4), (2048, 16))
SWA_HEADS = 8
SWA_DIM = 64
SWA_GW = SWA_HEADS * SWA_DIM
SWA_SPAN = 128
DN_HEADS = 8
DN_DK = 128
DN_CONV = 4
DN_CHUNK = 64
N_GROUPS = 4
PER_GROUP = 8
N_EXPERTS = N_GROUPS * PER_GROUP
TOP_K = 2

VMEM_LIMIT_BYTES = 56 * 1024 * 1024
LANES = 128
MXU = 256
MOE_ROWS = 256
ROW_UNROLL = 4


def _cparams(n_axes):
    return pltpu.CompilerParams(
        dimension_semantics=("arbitrary",) * n_axes, vmem_limit_bytes=VMEM_LIMIT_BYTES
    )


def _rms(x, g):
    return x * lax.rsqrt(jnp.mean(x * x, axis=-1, keepdims=True) + EPS) * g


def _bdot(a, b):
    return jnp.dot(a.astype(BF16), b.astype(BF16), preferred_element_type=F32)


def _hdot(a, b):
    return jnp.dot(a, b, preferred_element_type=F32, precision=HI)


def _sigmoid(x):
    return 1.0 / (1.0 + jnp.exp(-x))


def _silu(x):
    return x * _sigmoid(x)


def _softplus(x):
    return jnp.maximum(x, 0.0) + jnp.log1p(jnp.exp(-jnp.abs(x)))


def _proj_attn_kernel(x_ref, lng_ref, w_ref, ng_ref, bd_ref, p0, p1, p2, t0, t1, t2, h_scr, z_scr,
                      *, tm, dils, tail_rows, tail_first):
    j = pl.program_id(2)

    @pl.when(j == 0)
    def _():
        h_scr[...] = _rms(x_ref[0], lng_ref[...]).astype(BF16)

    z = jnp.dot(h_scr[...], w_ref[...], preferred_element_type=F32)
    n_cb = z_scr.shape[0]
    for c in range(n_cb):
        z_scr[c] = z[:, c * LANES:(c + 1) * LANES]

    @pl.when(j < 2)
    def _():
        for c in range(0, n_cb, 2):
            zc = jnp.concatenate([z_scr[c], z_scr[c + 1]], axis=-1)
            ss = jnp.dot((zc * zc).astype(BF16), bd_ref[...], preferred_element_type=F32)
            zn = zc * lax.rsqrt(ss * (1.0 / SWA_DIM) + EPS) * ng_ref[0, :, c * LANES:(c + 2) * LANES]
            z_scr[c] = zn[:, :LANES]
            z_scr[c + 1] = zn[:, LANES:]

    outs = (p0, p1, p2)
    tails = (t0, t1, t2)
    per_g = SWA_GW // LANES
    for sec in range(3):

        @pl.when(j == sec)
        def _(sec=sec):
            for gi, d in enumerate(dils):
                for cb in range(per_g):
                    c = gi * per_g + cb
                    col = sec * SWA_GW + cb * LANES
                    for r in range(d):
                        src = z_scr[c] if d == 1 else z_scr[c, pl.ds(r, tm // d, stride=d), :]
                        outs[gi][0, r, :, col:col + LANES] = src.astype(BF16)
                if sec >= 1:
                    @pl.when(pl.program_id(1) >= tail_first[gi])
                    def _(gi=gi):
                        rows = tail_rows[gi]
                        zr = jnp.concatenate([z_scr[gi * per_g + cb, tm - rows:tm, :] for cb in range(per_g)], axis=-1)
                        tails[gi][0, :, sec - 1] = zr.reshape(rows, SWA_HEADS, SWA_DIM)


def _proj_attn(x, ln_g, w_att, ng, bd, *, tm):
    B, L, D = x.shape
    nt = L // tm
    dils = tuple(d for _, d in SWA_CONFIGS)
    keeps = tuple(min(w, L) for w, _ in SWA_CONFIGS)
    tail_rows = tuple(min(k, tm) for k in keeps)
    for k, r in zip(keeps, tail_rows):
        assert k % r == 0 and L % tm == 0
    W3 = 3 * SWA_GW

    tail_first = tuple(nt - k // r for k, r in zip(keeps, tail_rows))

    def tail_spec(rows, first):
        return pl.BlockSpec((1, rows, 2, SWA_HEADS, SWA_DIM),
                            lambda b, i, j: (b, jnp.maximum(i - first, 0), 0, 0, 0))

    out_shape = [jax.ShapeDtypeStruct((B, d, L // d, W3), BF16) for d in dils]
    out_shape += [jax.ShapeDtypeStruct((B, k, 2, SWA_HEADS, SWA_DIM), F32) for k in keeps]
    out_specs = [pl.BlockSpec((1, d, tm // d, W3), lambda b, i, j: (b, 0, i, 0)) for d in dils]
    out_specs += [tail_spec(r, f) for r, f in zip(tail_rows, tail_first)]
    return pl.pallas_call(
        functools.partial(_proj_attn_kernel, tm=tm, dils=dils, tail_rows=tail_rows, tail_first=tail_first),
        grid=(B, nt, 3),
        in_specs=[
            pl.BlockSpec((1, tm, D), lambda b, i, j: (b, i, 0)),
            pl.BlockSpec((1, D), lambda b, i, j: (0, 0)),
            pl.BlockSpec((D, W3), lambda b, i, j: (0, j)),
            pl.BlockSpec((1, 1, W3), lambda b, i, j: (jnp.minimum(j, 1), 0, 0)),
            pl.BlockSpec((MXU, MXU), lambda b, i, j: (0, 0)),
        ],
        out_specs=out_specs,
        out_shape=out_shape,
        scratch_shapes=[pltpu.VMEM((tm, D), BF16), pltpu.VMEM((W3 // LANES, tm, LANES), F32)],
        compiler_params=_cparams(3),
        name="proj_attn",
    )(x, ln_g, w_att, ng, bd)


def _proj_plain_kernel(x_ref, lng_ref, w_ref, o_ref, h_scr):
    @pl.when(pl.program_id(1) == 0)
    def _():
        h_scr[...] = _rms(x_ref[...], lng_ref[...]).astype(BF16)

    o_ref[...] = jnp.dot(h_scr[...], w_ref[...], preferred_element_type=F32).astype(o_ref.dtype)


def _proj_plain(x2d, ln_g, w, *, tm, tn, out_dtype, name="proj_plain"):
    N, D = x2d.shape
    C = w.shape[1]
    assert N % tm == 0 and C % tn == 0
    return pl.pallas_call(
        _proj_plain_kernel,
        grid=(N // tm, C // tn),
        in_specs=[
            pl.BlockSpec((tm, D), lambda i, j: (i, 0)),
            pl.BlockSpec((1, D), lambda i, j: (0, 0)),
            pl.BlockSpec((D, tn), lambda i, j: (0, j)),
        ],
        out_specs=pl.BlockSpec((tm, tn), lambda i, j: (i, j)),
        out_shape=jax.ShapeDtypeStruct((N, C), out_dtype),
        scratch_shapes=[pltpu.VMEM((tm, D), BF16)],
        compiler_params=_cparams(2),
        name=name,
    )(x2d, ln_g, w)


def _attn_kernel(q_ref, kc_ref, vc_ref, kp_ref, vp_ref, o_ref, lse_ref, kk_scr, vv_scr, *, tq):
    i = pl.program_id(1)
    blk = SWA_SPAN
    kk_scr[0:blk, :] = kp_ref[0]
    kk_scr[blk:blk + tq, :] = kc_ref[0]
    vv_scr[0:blk, :] = vp_ref[0]
    vv_scr[blk:blk + tq, :] = vc_ref[0]
    qi = lax.broadcasted_iota(jnp.int32, (blk, 2 * blk), 0)
    ki = lax.broadcasted_iota(jnp.int32, (blk, 2 * blk), 1)
    dist = blk + qi - ki
    band = (dist >= 0) & (dist <= SWA_SPAN)
    band_first = band & ((ki >= blk) | (i > 0))
    lo = lax.broadcasted_iota(jnp.int32, (blk, LANES), 1) < SWA_DIM
    zero = jnp.zeros((blk, LANES), BF16)
    for jb in range(tq // blk):
        mask = band_first if jb == 0 else band
        rows = slice(jb * blk, (jb + 1) * blk)
        for hp in range(SWA_GW // LANES):
            cs = slice(hp * LANES, (hp + 1) * LANES)
            qb = q_ref[0, rows, cs]
            kk = kk_scr[jb * blk:(jb + 2) * blk, cs]
            vv = vv_scr[jb * blk:(jb + 2) * blk, cs]
            res_o, res_l = [], []
            for hh in range(2):
                qm = jnp.where(lo if hh == 0 else jnp.logical_not(lo), qb, zero)
                s = lax.dot_general(qm, kk, (((1,), (1,)), ((), ())), preferred_element_type=F32)
                s = jnp.where(mask, s * (SWA_DIM ** -0.5), -jnp.inf)
                m = jnp.max(s, axis=-1, keepdims=True)
                p = jnp.exp(s - m)
                den = jnp.sum(p, axis=-1, keepdims=True)
                pv = jnp.dot(p.astype(BF16), vv, preferred_element_type=F32)
                res_o.append(pv / den)
                res_l.append(jnp.broadcast_to(m + jnp.log(den), (blk, LANES)))
            o_ref[0, rows, cs] = jnp.where(lo, res_o[0], res_o[1]).astype(BF16)
            lse_ref[0, rows, cs] = jnp.where(lo, res_l[0], res_l[1])


def _attn(p, *, tq):
    S, M, _ = p.shape
    assert M % tq == 0 and tq % SWA_SPAN == 0
    nb = tq // SWA_SPAN
    return pl.pallas_call(
        functools.partial(_attn_kernel, tq=tq),
        grid=(S, M // tq),
        in_specs=[
            pl.BlockSpec((1, tq, SWA_GW), lambda s, i: (s, i, 0)),
            pl.BlockSpec((1, tq, SWA_GW), lambda s, i: (s, i, 1)),
            pl.BlockSpec((1, tq, SWA_GW), lambda s, i: (s, i, 2)),
            pl.BlockSpec((1, SWA_SPAN, SWA_GW), lambda s, i: (s, jnp.maximum(i * nb - 1, 0), 1)),
            pl.BlockSpec((1, SWA_SPAN, SWA_GW), lambda s, i: (s, jnp.maximum(i * nb - 1, 0), 2)),
        ],
        out_specs=[
            pl.BlockSpec((1, tq, SWA_GW), lambda s, i: (s, i, 0)),
            pl.BlockSpec((1, tq, SWA_GW), lambda s, i: (s, i, 0)),
        ],
        out_shape=[
            jax.ShapeDtypeStruct((S, M, SWA_GW), BF16),
            jax.ShapeDtypeStruct((S, M, SWA_GW), F32),
        ],
        scratch_shapes=[
            pltpu.VMEM((SWA_SPAN + tq, SWA_GW), BF16),
            pltpu.VMEM((SWA_SPAN + tq, SWA_GW), BF16),
        ],
        compiler_params=_cparams(2),
        name="swa_attn",
    )(p, p, p, p, p)


def _conv_kernel(x_ref, xp_ref, cp_ref, cw_ref, ba_ref, par_ref, q_ref, k_ref, v_ref, g_ref, xs_scr,
                 *, tl):
    i = pl.program_id(1)
    xs_scr[0:8, :] = jnp.where(i == 0, cp_ref[0], xp_ref[0])
    xs_scr[8:8 + tl, :] = x_ref[0]
    nh = DN_HEADS
    outs = (q_ref, k_ref, v_ref)
    for cb in range(3 * nh):
        cs = slice(cb * LANES, (cb + 1) * LANES)
        acc = cw_ref[0:1, cs] * xs_scr[5:5 + tl, cs]
        for t in range(1, DN_CONV):
            acc = acc + cw_ref[t:t + 1, cs] * xs_scr[5 + t:5 + t + tl, cs]
        act = _silu(acc)
        part, h = divmod(cb, nh)
        if part < 2:
            act = act * lax.rsqrt(jnp.sum(act * act, axis=-1, keepdims=True) + EPS)
        if part == 0:
            act = act * (DN_DK ** -0.5)
        outs[part][0, :, h * LANES:(h + 1) * LANES] = act.astype(BF16)
    ba = ba_ref[0]
    lane = lax.broadcasted_iota(jnp.int32, (tl, LANES), 1)
    g = par_ref[0:1, :] * _softplus(ba + par_ref[1:2, :])
    ri = lax.broadcasted_iota(jnp.int32, (tl, tl), 0)
    ci = lax.broadcasted_iota(jnp.int32, (tl, tl), 1)
    tri = jnp.where((ri // DN_CHUNK == ci // DN_CHUNK) & (ci <= ri), 1.0, 0.0).astype(F32)
    gc = _hdot(tri, g)
    g_ref[0] = jnp.where(lane < nh, _sigmoid(ba), gc)


def _dn_conv(raw, conv_prev8, conv_w, ba, par, *, tl):
    B, L, C = raw.shape
    assert L % tl == 0 and tl % DN_CHUNK == 0
    width = DN_HEADS * DN_DK
    return pl.pallas_call(
        functools.partial(_conv_kernel, tl=tl),
        grid=(B, L // tl),
        in_specs=[
            pl.BlockSpec((1, tl, C), lambda b, i: (b, i, 0)),
            pl.BlockSpec((1, 8, C), lambda b, i: (b, jnp.maximum(i * (tl // 8) - 1, 0), 0)),
            pl.BlockSpec((1, 8, C), lambda b, i: (b, 0, 0)),
            pl.BlockSpec((DN_CONV, C), lambda b, i: (0, 0)),
            pl.BlockSpec((1, tl, LANES), lambda b, i: (b, i, 0)),
            pl.BlockSpec((2, LANES), lambda b, i: (0, 0)),
        ],
        out_specs=[pl.BlockSpec((1, tl, width), lambda b, i: (b, i, 0))] * 3
        + [pl.BlockSpec((1, tl, LANES), lambda b, i: (b, i, 0))],
        out_shape=[jax.ShapeDtypeStruct((B, L, width), BF16)] * 3
        + [jax.ShapeDtypeStruct((B, L, LANES), F32)],
        scratch_shapes=[pltpu.VMEM((8 + tl, C), F32)],
        compiler_params=_cparams(2),
        name="dn_conv",
    )(raw, raw, conv_prev8, conv_w, ba, par)


def _intra_kernel(q_ref, k_ref, v_ref, g_ref, u_ref, w_ref, qd_ref, kd_ref, a_ref, gt_ref, *, tl):
    h = pl.program_id(1)
    C = DN_CHUNK
    lane = lax.broadcasted_iota(jnp.int32, (C, LANES), 1)
    ri = lax.broadcasted_iota(jnp.int32, (C, C), 0)
    ci = lax.broadcasted_iota(jnp.int32, (C, C), 1)
    eye = jnp.where(ri == ci, 1.0, 0.0).astype(F32)
    nt_dot = lambda a, b: lax.dot_general(a.astype(BF16), b.astype(BF16), (((1,), (1,)), ((), ())),
                                          preferred_element_type=F32)
    rows = [slice(c * C, (c + 1) * C) for c in range(tl // C)]
    gv = [g_ref[0, r, :] for r in rows]
    q = [q_ref[0, r, :].astype(F32) for r in rows]
    k = [k_ref[0, r, :].astype(F32) for r in rows]
    v = [v_ref[0, r, :].astype(F32) for r in rows]
    beta = [jnp.sum(jnp.where(lane == h, x, 0.0), axis=-1, keepdims=True) for x in gv]
    gc = [jnp.sum(jnp.where(lane == h + DN_HEADS, x, 0.0), axis=-1, keepdims=True) for x in gv]
    lhs = [jnp.where(lane == 0, x, jnp.where(lane == 1, 1.0, 0.0)) for x in gc]
    rhs = [jnp.where(lane == 0, 1.0, jnp.where(lane == 1, -x, 0.0)) for x in gc]
    diff = [lax.dot_general(a, b, (((1,), (1,)), ((), ())), preferred_element_type=F32, precision=HI)
            for a, b in zip(lhs, rhs)]
    decay = [jnp.exp(jnp.where(ri >= ci, x, -jnp.inf)) for x in diff]
    kb = [a * b for a, b in zip(k, beta)]
    x = [-jnp.where(ri > ci, nt_dot(a, b) * d, 0.0) for a, b, d in zip(kb, k, decay)]
    t = [eye + a for a in x]
    for _ in range(5):
        x = [_bdot(a, a) for a in x]
        t = [a + _bdot(a, b) for a, b in zip(t, x)]
    eg = [jnp.exp(a) for a in gc]
    glast = [a[C - 1:C, :] for a in gc]
    u = [_bdot(a, b * c) for a, b, c in zip(t, v, beta)]
    w = [_bdot(a, b * c) for a, b, c in zip(t, kb, eg)]
    qk = [nt_dot(a, b) for a, b in zip(q, k)]
    for c, r in enumerate(rows):
        u_ref[0, 0, r, :] = u[c]
        w_ref[0, 0, r, :] = w[c].astype(BF16)
        a_ref[0, 0, r, :] = (qk[c] * decay[c]).astype(BF16)
        qd_ref[0, 0, r, :] = (q[c] * eg[c]).astype(BF16)
        kd_ref[0, 0, r, :] = (k[c] * jnp.exp(glast[c] - gc[c])).astype(BF16)
        gt_ref[0, 0, c:c + 1, :] = jnp.broadcast_to(jnp.exp(glast[c]), (1, LANES))


def _dn_intra(q, k, v, g, *, tl):
    B, L, _ = q.shape
    H, C = DN_HEADS, DN_CHUNK
    assert L % tl == 0 and (tl // C) % 8 == 0
    qkv_spec = pl.BlockSpec((1, tl, LANES), lambda b, h, i: (b, i, h))
    hl = lambda w: pl.BlockSpec((1, 1, tl, w), lambda b, h, i: (b, h, i, 0))
    return pl.pallas_call(
        functools.partial(_intra_kernel, tl=tl),
        grid=(B, H, L // tl),
        in_specs=[qkv_spec, qkv_spec, qkv_spec, pl.BlockSpec((1, tl, LANES), lambda b, h, i: (b, i, 0))],
        out_specs=[hl(LANES), hl(LANES), hl(LANES), hl(LANES), hl(C),
                   pl.BlockSpec((1, 1, tl // C, LANES), lambda b, h, i: (b, h, i, 0))],
        out_shape=[
            jax.ShapeDtypeStruct((B, H, L, LANES), F32),
            jax.ShapeDtypeStruct((B, H, L, LANES), BF16),
            jax.ShapeDtypeStruct((B, H, L, LANES), BF16),
            jax.ShapeDtypeStruct((B, H, L, LANES), BF16),
            jax.ShapeDtypeStruct((B, H, L, C), BF16),
            jax.ShapeDtypeStruct((B, H, L // C, LANES), F32),
        ],
        compiler_params=_cparams(3),
        name="dn_intra",
    )(q, k, v, g)


def _scan_kernel(u_ref, w_ref, qd_ref, kd_ref, a_ref, gt_ref, s0_ref, o_ref, s_ref, *, n_chunks):
    C = DN_CHUNK
    H = s_ref.shape[1]

    @pl.when(pl.program_id(1) == 0)
    def _():
        s_ref[...] = s0_ref[...]

    def body(c, carry):
        rows = pl.ds(pl.multiple_of(c * C, C), C)
        S = [s_ref[0, h] for h in range(H)]
        Sb = [x.astype(BF16) for x in S]
        v_new = [u_ref[0, h, rows, :] - jnp.dot(w_ref[0, h, rows, :], Sb[h], preferred_element_type=F32)
                 for h in range(H)]
        vb = [x.astype(BF16) for x in v_new]
        o = [jnp.dot(qd_ref[0, h, rows, :], Sb[h], preferred_element_type=F32)
             + jnp.dot(a_ref[0, h, rows, :], vb[h], preferred_element_type=F32) for h in range(H)]
        upd = [lax.dot_general(kd_ref[0, h, rows, :], vb[h], (((0,), (0,)), ((), ())), preferred_element_type=F32)
               for h in range(H)]
        for h in range(H):
            o_ref[0, rows, h * LANES:(h + 1) * LANES] = o[h]
            s_ref[0, h] = S[h] * gt_ref[0, h, pl.ds(c, 1), :] + upd[h]
        return carry

    lax.fori_loop(0, n_chunks, body, 0)


def _dn_scan(u, w, qd, kd, a, gt, s0, *, tl):
    B, H, L, _ = u.shape
    C = DN_CHUNK
    assert L % tl == 0 and (tl // C) % 8 == 0
    hs = lambda wd: pl.BlockSpec((1, H, tl, wd), lambda b, i: (b, 0, i, 0))
    s_spec = pl.BlockSpec((1, H, DN_DK, LANES), lambda b, i: (b, 0, 0, 0))
    return pl.pallas_call(
        functools.partial(_scan_kernel, n_chunks=tl // C),
        grid=(B, L // tl),
        in_specs=[hs(LANES), hs(LANES), hs(LANES), hs(LANES), hs(C),
                  pl.BlockSpec((1, H, tl // C, LANES), lambda b, i: (b, 0, i, 0)), s_spec],
        out_specs=[pl.BlockSpec((1, tl, H * LANES), lambda b, i: (b, i, 0)), s_spec],
        out_shape=[jax.ShapeDtypeStruct((B, L, H * LANES), F32),
                   jax.ShapeDtypeStruct((B, H, DN_DK, LANES), F32)],
        compiler_params=_cparams(2),
        name="dn_scan",
    )(u, w, qd, kd, a, gt, s0)


def _gated_mix(o_a, od, gates, dng, wa, wb, wo, x, dot):
    width = DN_HEADS * DN_DK
    parts = []
    for h in range(DN_HEADS):
        blk = od[:, h * LANES:(h + 1) * LANES]
        parts.append(blk * lax.rsqrt(jnp.mean(blk * blk, axis=-1, keepdims=True) + EPS) * dng)
    odn = jnp.concatenate(parts, axis=-1) * _silu(gates[:, 0:width].astype(F32))
    ya = dot(o_a, wa)
    yb = dot(odn, wb)
    mix = _sigmoid(gates[:, width:2 * width].astype(F32)) * ya + _sigmoid(gates[:, 2 * width:].astype(F32)) * yb
    return x + dot(mix, wo)


def _out_kernel(x_ref, o0, o1, o2, l0, l1, l2, od_ref, gates_ref, dng_ref, wa_ref, wb_ref, wo_ref, y_ref,
                so0, so1, so2, sl0, sl1, sl2, *, tm, dils):
    o_refs, l_refs = (o0, o1, o2), (l0, l1, l2)
    so, sl = (so0, so1, so2), (sl0, sl1, sl2)
    parts = []
    for cb in range(SWA_GW // LANES):
        cs = slice(cb * LANES, (cb + 1) * LANES)
        for gi, d in enumerate(dils):
            for r in range(d):
                dst = slice(None) if d == 1 else pl.ds(r, tm // d, stride=d)
                so[gi][cb, dst, :] = o_refs[gi][0, r, :, cs].astype(F32)
                sl[gi][cb, dst, :] = l_refs[gi][0, r, :, cs]
        ls = [s[cb] for s in sl]
        m = jnp.maximum(jnp.maximum(ls[0], ls[1]), ls[2])
        es = [jnp.exp(l - m) for l in ls]
        parts.append((es[0] * so[0][cb] + es[1] * so[1][cb] + es[2] * so[2][cb]) / (es[0] + es[1] + es[2]))
    o_a = jnp.concatenate(parts, axis=-1)
    y_ref[...] = _gated_mix(o_a, od_ref[...], gates_ref[...], dng_ref[...], wa_ref[...], wb_ref[...],
                            wo_ref[...], x_ref[...], _bdot)


def _out_proj(x2d, os_, ls_, od2d, gates, dng, wa, wb, wo, *, B, L, tm):
    N, D = x2d.shape
    nt = L // tm
    dils = tuple(d for _, d in SWA_CONFIGS)
    grp = lambda d: pl.BlockSpec((1, d, tm // d, SWA_GW), lambda i: (i // nt, 0, i % nt, 0))
    row = lambda w: pl.BlockSpec((tm, w), lambda i: (i, 0))
    full = lambda a: pl.BlockSpec(a.shape, lambda i: (0, 0))
    return pl.pallas_call(
        functools.partial(_out_kernel, tm=tm, dils=dils),
        grid=(N // tm,),
        in_specs=[row(D)] + [grp(d) for d in dils] * 2 + [row(od2d.shape[1]), row(gates.shape[1]),
                                                          full(dng), full(wa), full(wb), full(wo)],
        out_specs=row(D),
        out_shape=jax.ShapeDtypeStruct((N, D), F32),
        scratch_shapes=[pltpu.VMEM((SWA_GW // LANES, tm, LANES), F32)] * 6,
        compiler_params=_cparams(1),
        name="out_proj",
    )(x2d, *os_, *ls_, od2d, gates, dng, wa, wb, wo)


def _router_kernel(x_ref, lng_ref, wr_ref, br_ref, info_ref, cnt_ref, base_scr, *, tm):
    i = pl.program_id(0)

    @pl.when(i == 0)
    def _():
        base_scr[...] = jnp.zeros_like(base_scr)

    h = _rms(x_ref[...], lng_ref[...])
    lg = _bdot(h, wr_ref[...]) + br_ref[...]
    lane = lax.broadcasted_iota(jnp.int32, (tm, LANES), 1)
    big = jnp.int32(1 << 20)
    ninf = -jnp.inf

    def argmax_lane(vals):
        mx = jnp.max(vals, axis=-1, keepdims=True)
        idx = jnp.min(jnp.where(vals == mx, lane, big), axis=-1, keepdims=True)
        return mx, idx

    lgm = jnp.where(lane < N_GROUPS, lg, ninf)
    mg, gsel = argmax_lane(lgm)
    pg = 1.0 / jnp.sum(jnp.exp(lgm - mg), axis=-1, keepdims=True)
    start = N_GROUPS + gsel * PER_GROUP
    le = jnp.where((lane >= start) & (lane < start + PER_GROUP), lg, ninf)
    m1, i1 = argmax_lane(le)
    m2, i2 = argmax_lane(jnp.where(lane == i1, ninf, le))
    e21 = jnp.exp(m2 - m1)
    w1 = pg / (1.0 + e21)
    w2 = pg * e21 / (1.0 + e21)
    oh = jnp.where(lane == i1, 1.0, 0.0) + jnp.where(lane == i2, 1.0, 0.0)
    ri = lax.broadcasted_iota(jnp.int32, (tm, tm), 0)
    ci = lax.broadcasted_iota(jnp.int32, (tm, tm), 1)
    strict = jnp.where(ci < ri, 1.0, 0.0).astype(BF16)
    pref = jnp.dot(strict, oh.astype(BF16), preferred_element_type=F32) + base_scr[...]
    r1 = jnp.sum(jnp.where(lane == i1, pref, 0.0), axis=-1, keepdims=True)
    r2 = jnp.sum(jnp.where(lane == i2, pref, 0.0), axis=-1, keepdims=True)
    base_scr[...] = base_scr[...] + jnp.sum(oh, axis=0, keepdims=True)
    cnt_ref[...] = base_scr[...]
    off = jnp.float32(N_GROUPS)
    info = jnp.where(lane == 0, i1.astype(F32) - off, 0.0)
    info = jnp.where(lane == 1, i2.astype(F32) - off, info)
    info = jnp.where(lane == 2, w1, info)
    info = jnp.where(lane == 3, w2, info)
    info = jnp.where(lane == 4, r1, info)
    info = jnp.where(lane == 5, r2, info)
    info_ref[...] = info


def _router(x2d, ln_g, wr, br, *, tm):
    N, D = x2d.shape
    assert N % tm == 0
    return pl.pallas_call(
        functools.partial(_router_kernel, tm=tm),
        grid=(N // tm,),
        in_specs=[
            pl.BlockSpec((tm, D), lambda i: (i, 0)),
            pl.BlockSpec((1, D), lambda i: (0, 0)),
            pl.BlockSpec((D, LANES), lambda i: (0, 0)),
            pl.BlockSpec((1, LANES), lambda i: (0, 0)),
        ],
        out_specs=[pl.BlockSpec((tm, LANES), lambda i: (i, 0)), pl.BlockSpec((1, LANES), lambda i: (0, 0))],
        out_shape=[jax.ShapeDtypeStruct((N, LANES), F32), jax.ShapeDtypeStruct((1, LANES), F32)],
        scratch_shapes=[pltpu.VMEM((1, LANES), F32)],
        compiler_params=_cparams(1),
        name="router",
    )(x2d, ln_g, wr, br)


def _dispatch_kernel(dest_ref, zb_ref, x_ref, xs_ref, zero_scr, rows_scr, sem, *, tm, tb, n_zb, n_tiles):
    i = pl.program_id(0)

    @pl.when(i == 0)
    def _():
        zero_scr[...] = jnp.zeros_like(zero_scr)

        def zero_copy(n):
            return pltpu.make_async_copy(zero_scr, xs_ref.at[pl.ds(zb_ref[n] * tb, tb)], sem.at[2])

        def zero_issue(n, carry):
            @pl.when(zb_ref[n] >= 0)
            def _():
                zero_copy(n).start()

            return carry

        def zero_wait(n, carry):
            @pl.when(zb_ref[n] >= 0)
            def _():
                zero_copy(n).wait()

            return carry

        lax.fori_loop(0, n_zb, zero_issue, 0)
        lax.fori_loop(0, n_zb, zero_wait, 0)

    buf_now = lax.rem(i, 2)
    rows_scr[buf_now] = x_ref[...].reshape(rows_scr.shape[1:])

    def row_copy(tile, t, slot):
        buf = lax.rem(tile, 2)
        return pltpu.make_async_copy(
            rows_scr.at[buf, pl.ds(t, 1)],
            xs_ref.at[pl.ds(dest_ref[(tile * tm + t) * TOP_K + slot], 1)], sem.at[buf])

    def issue(tt, carry):
        for r in range(ROW_UNROLL):
            for slot in range(TOP_K):
                row_copy(i, tt * ROW_UNROLL + r, slot).start(priority=slot)
        return carry

    def drain(tile):
        buf = lax.rem(tile, 2)
        for _ in range(TOP_K):
            pltpu.make_async_copy(rows_scr.at[buf], rows_scr.at[buf], sem.at[buf]).wait()

    lax.fori_loop(0, tm // ROW_UNROLL, issue, 0)

    @pl.when(i > 0)
    def _():
        drain(i - 1)

    @pl.when(i == n_tiles - 1)
    def _():
        drain(i)


def _dispatch(dest, zero_blocks, x2d, *, tm, tb, n_rows):
    N, D = x2d.shape
    return pl.pallas_call(
        functools.partial(_dispatch_kernel, tm=tm, tb=tb, n_zb=zero_blocks.shape[0], n_tiles=N // tm),
        grid_spec=pltpu.PrefetchScalarGridSpec(
            num_scalar_prefetch=2,
            grid=(N // tm,),
            in_specs=[pl.BlockSpec((tm, D), lambda i, d, z: (i, 0))],
            out_specs=pl.BlockSpec(memory_space=pl.ANY),
            scratch_shapes=[pltpu.VMEM((tb, D // LANES, LANES), F32), pltpu.VMEM((2, tm, D // LANES, LANES), F32),
                            pltpu.SemaphoreType.DMA((3,))],
        ),
        out_shape=jax.ShapeDtypeStruct((n_rows, D // LANES, LANES), F32),
        compiler_params=_cparams(1),
        name="moe_dispatch",
    )(dest, zero_blocks, x2d)


def _ffn_kernel(be_ref, nb_ref, xs_ref, lng_ref, wg_ref, wu_ref, wd_ref, y_ref):
    used = pl.program_id(0) < nb_ref[0]
    tb = xs_ref.shape[0]

    @pl.when(used)
    def _():
        h = _rms(xs_ref[...].reshape(tb, -1), lng_ref[...]).astype(BF16)
        g = jnp.dot(h, wg_ref[0], preferred_element_type=F32)
        u = jnp.dot(h, wu_ref[0], preferred_element_type=F32)
        y = jnp.dot((_silu(g) * u).astype(BF16), wd_ref[0], preferred_element_type=F32)
        y_ref[...] = y.reshape(y_ref.shape)

    @pl.when(jnp.logical_not(used))
    def _():
        y_ref[...] = jnp.zeros_like(y_ref)


def _ffn(blk_e, nb_used, xs, ln_g, wg, wu, wd, *, tb):
    P, S, _ = xs.shape
    D = S * LANES
    nb = P // tb
    DE = wg.shape[2]
    return pl.pallas_call(
        _ffn_kernel,
        grid_spec=pltpu.PrefetchScalarGridSpec(
            num_scalar_prefetch=2,
            grid=(nb,),
            in_specs=[
                pl.BlockSpec((tb, S, LANES), lambda i, be, nbu: (jnp.minimum(i, nbu[0] - 1), 0, 0)),
                pl.BlockSpec((1, D), lambda i, be, nbu: (0, 0)),
                pl.BlockSpec((1, D, DE), lambda i, be, nbu: (be[i], 0, 0)),
                pl.BlockSpec((1, D, DE), lambda i, be, nbu: (be[i], 0, 0)),
                pl.BlockSpec((1, DE, D), lambda i, be, nbu: (be[i], 0, 0)),
            ],
            out_specs=pl.BlockSpec((tb, S, LANES), lambda i, be, nbu: (i, 0, 0)),
        ),
        out_shape=jax.ShapeDtypeStruct((P, S, LANES), F32),
        compiler_params=_cparams(1),
        name="moe_ffn",
    )(blk_e, nb_used, xs, ln_g, wg, wu, wd)


def _combine_kernel(dest_ref, x_ref, info_ref, yb_ref, y_ref, g_scr, sem, *, tm, n_tiles):
    i = pl.program_id(0)

    def row_copy(tile, t, slot):
        buf = lax.rem(tile, 2)
        return pltpu.make_async_copy(
            yb_ref.at[pl.ds(dest_ref[(tile * tm + t) * TOP_K + slot], 1)],
            g_scr.at[buf, slot, pl.ds(t, 1)], sem.at[buf])

    def issue_tile(tile):
        def body(tt, carry):
            for r in range(ROW_UNROLL):
                for slot in range(TOP_K):
                    row_copy(tile, tt * ROW_UNROLL + r, slot).start(priority=slot)
            return carry

        lax.fori_loop(0, tm // ROW_UNROLL, body, 0)

    @pl.when(i == 0)
    def _():
        issue_tile(i)

    @pl.when(i + 1 < n_tiles)
    def _():
        issue_tile(i + 1)

    buf = lax.rem(i, 2)
    pltpu.make_async_copy(g_scr.at[buf], g_scr.at[buf], sem.at[buf]).wait()
    info = info_ref[...]
    lane = lax.broadcasted_iota(jnp.int32, info.shape, 1)
    w1 = jnp.sum(jnp.where(lane == 2, info, 0.0), axis=-1, keepdims=True)
    w2 = jnp.sum(jnp.where(lane == 3, info, 0.0), axis=-1, keepdims=True)
    g1 = g_scr[buf, 0].reshape(x_ref.shape)
    g2 = g_scr[buf, 1].reshape(x_ref.shape)
    y_ref[...] = x_ref[...] + (w1 * g1 + w2 * g2)


def _combine(dest, x2d, info, yb, *, tm):
    N, D = x2d.shape
    return pl.pallas_call(
        functools.partial(_combine_kernel, tm=tm, n_tiles=N // tm),
        grid_spec=pltpu.PrefetchScalarGridSpec(
            num_scalar_prefetch=1,
            grid=(N // tm,),
            in_specs=[
                pl.BlockSpec((tm, D), lambda i, d: (i, 0)),
                pl.BlockSpec((tm, LANES), lambda i, d: (i, 0)),
                pl.BlockSpec(memory_space=pl.ANY),
            ],
            out_specs=pl.BlockSpec((tm, D), lambda i, d: (i, 0)),
            scratch_shapes=[pltpu.VMEM((2, TOP_K, tm, D // LANES, LANES), F32), pltpu.SemaphoreType.DMA((2,))],
        ),
        out_shape=jax.ShapeDtypeStruct((N, D), F32),
        compiler_params=_cparams(1),
        name="moe_combine",
    )(dest, x2d, info, yb)


def _moe(x2d, ln2_g, wr, br, wg, wu, wd, *, tm):
    N, D = x2d.shape
    tb = MOE_ROWS
    info, counts = _router(x2d, ln2_g, wr, br, tm=tm)
    counts = counts[0, N_GROUPS:N_GROUPS + N_EXPERTS].astype(jnp.int32)
    pcounts = (counts + tb - 1) // tb * tb
    pend = jnp.cumsum(pcounts)
    pstart = pend - pcounts
    e = info[:, 0:TOP_K].astype(jnp.int32)
    rank = info[:, 4:4 + TOP_K].astype(jnp.int32)
    dest = (pstart[e] + rank).reshape(-1)
    nb = -(-(N * TOP_K) // tb) + N_EXPERTS
    P = nb * tb
    blocks = jnp.arange(nb, dtype=jnp.int32)
    blk_e = jnp.minimum(jnp.sum((pend[None, :] <= blocks[:, None] * tb).astype(jnp.int32), axis=1), N_EXPERTS - 1)
    nb_used = (pend[-1] // tb).astype(jnp.int32).reshape(1)
    zero_blocks = jnp.concatenate([jnp.where(counts % tb != 0, pend // tb - 1, -1),
                                   jnp.where(blocks >= nb_used[0], blocks, -1)]).astype(jnp.int32)
    xs = _dispatch(dest, zero_blocks, x2d, tm=tm, tb=tb, n_rows=P)
    yb = _ffn(blk_e, nb_used, xs, ln2_g, wg, wu, wd, tb=tb)
    return _combine(dest, x2d, info, yb, tm=tm)


def _rows8(x):
    return jnp.broadcast_to(x, (8, x.shape[1]))


def _row_hdot(x, m):
    return _hdot(_rows8(x), m)[0:1]


def _bf_round(x):
    return x.astype(BF16).astype(F32)


def _sample_attn_kernel(z_ref, c0, c1, c2, qg_ref, kg_ref, oa_ref, kv_ref):
    W = SWA_GW
    scale = SWA_DIM ** -0.5
    z = z_ref[0]
    sub = lax.broadcasted_iota(jnp.int32, (SWA_HEADS, W), 0)
    lane = lax.broadcasted_iota(jnp.int32, (SWA_HEADS, W), 1)
    own = lane // SWA_DIM == sub

    def heads(row):
        return jnp.where(own, jnp.broadcast_to(row, (SWA_HEADS, W)), 0.0)

    def head_sum(row):
        return jnp.sum(heads(row), axis=-1, keepdims=True)

    def spread(col):
        return jnp.sum(jnp.where(own, col, 0.0), axis=0, keepdims=True)

    def headnorm(zz, g):
        return zz * spread(lax.rsqrt(head_sum(zz * zz) * (1.0 / SWA_DIM) + EPS)) * g

    outs, lses = [], []
    for gi, (c_ref, (win, dil)) in enumerate(zip((c0, c1, c2), SWA_CONFIGS)):
        q = headnorm(z[:, gi * W:(gi + 1) * W], qg_ref[gi:gi + 1, :])
        k = headnorm(z[:, 3 * W + gi * W:3 * W + (gi + 1) * W], kg_ref[gi:gi + 1, :])
        v = z[:, 6 * W + gi * W:6 * W + (gi + 1) * W]
        kv_ref[0, :, 2 * gi * W:(2 * gi + 1) * W] = k
        kv_ref[0, :, (2 * gi + 1) * W:(2 * gi + 2) * W] = v
        kc = c_ref[0].reshape(W, win).astype(BF16)
        vc = c_ref[1].reshape(W, win).astype(BF16)
        s_c = jnp.dot(heads(q).astype(BF16), kc, preferred_element_type=F32) * scale
        row = lax.broadcasted_iota(jnp.int32, s_c.shape, 1)
        s_c = jnp.where(row % dil == 0, s_c, -jnp.inf)
        s_n = head_sum(_bf_round(k) * _bf_round(q)) * scale
        m = jnp.maximum(jnp.max(s_c, axis=-1, keepdims=True), s_n)
        p_c = jnp.exp(s_c - m)
        p_n = jnp.exp(s_n - m)
        den = jnp.sum(p_c, axis=-1, keepdims=True) + p_n
        pv = lax.dot_general(p_c.astype(BF16), vc, (((1,), (1,)), ((), ())), preferred_element_type=F32)
        num = jnp.sum(jnp.where(own, pv, 0.0), axis=0, keepdims=True) + spread(_bf_round(p_n)) * _bf_round(v)
        outs.append(num / spread(den))
        lses.append(m + jnp.log(den))
    mm = jnp.maximum(jnp.maximum(lses[0], lses[1]), lses[2])
    es = [jnp.exp(l - mm) for l in lses]
    tot = es[0] + es[1] + es[2]
    oa_ref[0] = sum(spread(_bf_round(e / tot)) * _bf_round(o) for e, o in zip(es, outs))


def _sample_attn(z3, caches, layer, qg, kg):
    Bs = z3.shape[0]
    W = SWA_GW
    cviews, cspecs = [], []
    for (win, dil), c in zip(SWA_CONFIGS, caches):
        assert c.shape[2] == win
        cviews.append(jnp.transpose(c, (0, 1, 3, 4, 5, 2)))
        cspecs.append(pl.BlockSpec((None, None, 2, SWA_HEADS, SWA_DIM, win), lambda b: (layer, b, 0, 0, 0, 0)))
    full = lambda a: pl.BlockSpec(a.shape, lambda b: (0,) * a.ndim)
    return pl.pallas_call(
        _sample_attn_kernel,
        grid=(Bs,),
        in_specs=[pl.BlockSpec((1, 1, 9 * W), lambda b: (b, 0, 0))] + cspecs + [full(qg), full(kg)],
        out_specs=[pl.BlockSpec((1, 1, W), lambda b: (b, 0, 0)), pl.BlockSpec((1, 1, 6 * W), lambda b: (b, 0, 0))],
        out_shape=[jax.ShapeDtypeStruct((Bs, 1, W), F32), jax.ShapeDtypeStruct((Bs, 1, 6 * W), F32)],
        compiler_params=_cparams(1),
        name="sample_attn",
    )(z3, *cviews, qg, kg)


def _sample_dn_kernel(raw_ref, cs_ref, cw_ref, ba_ref, par_ref, s_ref, e_ref, etb_ref, etg_ref, o_ref, so_ref):
    E, ETB, ETG = e_ref[...], etb_ref[...], etg_ref[...]
    width = DN_HEADS * DN_DK
    conv = cw_ref[DN_CONV - 1:DN_CONV, :] * raw_ref[0]
    for t in range(DN_CONV - 1):
        conv = conv + cw_ref[t:t + 1, :] * cs_ref[0, t:t + 1, :]
    act = _silu(conv)

    def l2(zz):
        return zz * _row_hdot(lax.rsqrt(_row_hdot(zz * zz, E) + EPS), ETB)

    qn = l2(act[:, 0:width]) * (DN_DK ** -0.5)
    kn = l2(act[:, width:2 * width])
    vn = act[:, 2 * width:3 * width]
    ba = ba_ref[0]
    beta = _row_hdot(_sigmoid(ba), ETB)
    eg = jnp.exp(_row_hdot(par_ref[0:1, :] * _softplus(ba + par_ref[1:2, :]), ETG))
    row0 = lax.broadcasted_iota(jnp.int32, (8, LANES), 0) == 0
    for h in range(DN_HEADS):
        sl = slice(h * LANES, (h + 1) * LANES)
        S = s_ref[0, h]
        q, k, v, b, e = qn[:, sl], kn[:, sl], vn[:, sl], beta[:, sl], eg[:, sl]
        Sr = _bf_round(S)
        v_new = v * b - _row_hdot(_bf_round(k * b * e), Sr)
        a = jnp.sum(q * k, axis=-1, keepdims=True)
        o_ref[0, :, sl] = _row_hdot(_bf_round(q * e), Sr) + a * v_new
        k8 = jnp.where(row0, _rows8(k), 0.0)
        upd = lax.dot_general(k8, _rows8(v_new), (((0,), (0,)), ((), ())), preferred_element_type=F32, precision=HI)
        so_ref[0, h] = S * e + upd


def _sample_dn(raw3, conv_state, s0, layer, conv_w, ba3, par, e_mat, etb, etg):
    Bs, _, C = raw3.shape
    H = DN_HEADS
    full = lambda a: pl.BlockSpec(a.shape, lambda b: (0,) * a.ndim)
    return pl.pallas_call(
        _sample_dn_kernel,
        grid=(Bs,),
        in_specs=[pl.BlockSpec((1, 1, C), lambda b: (b, 0, 0)),
                  pl.BlockSpec((None, 1, DN_CONV - 1, C), lambda b: (layer, b, 0, 0)),
                  full(conv_w),
                  pl.BlockSpec((1, 1, LANES), lambda b: (b, 0, 0)),
                  full(par),
                  pl.BlockSpec((None, 1, H, DN_DK, LANES), lambda b: (layer, b, 0, 0, 0)),
                  full(e_mat), full(etb), full(etg)],
        out_specs=[pl.BlockSpec((1, 1, H * LANES), lambda b: (b, 0, 0)),
                   pl.BlockSpec((1, H, DN_DK, LANES), lambda b: (b, 0, 0, 0))],
        out_shape=[jax.ShapeDtypeStruct((Bs, 1, H * LANES), F32), jax.ShapeDtypeStruct(s0.shape[1:], F32)],
        compiler_params=_cparams(1),
        name="sample_dn",
    )(raw3, conv_state, conv_w, ba3, par, s0, e_mat, etb, etg)


def _sample_out_kernel(x_ref, oa_ref, od_ref, gates_ref, dng_ref, wa_ref, wb_ref, wo_ref, y_ref):
    y_ref[...] = _gated_mix(oa_ref[...], od_ref[...], gates_ref[...], dng_ref[...], wa_ref[...], wb_ref[...],
                            wo_ref[...], x_ref[...], _bdot)


def _sample_out(x2d, oa, od, gates, dng, wa, wb, wo):
    args = (x2d, oa, od, gates, dng, wa, wb, wo)
    return pl.pallas_call(
        _sample_out_kernel,
        grid=(1,),
        in_specs=[pl.BlockSpec(a.shape, lambda i: (0, 0)) for a in args],
        out_specs=pl.BlockSpec(x2d.shape, lambda i: (0, 0)),
        out_shape=jax.ShapeDtypeStruct(x2d.shape, F32),
        compiler_params=_cparams(1),
        name="sample_out",
    )(*args)


def _head_indicator(width, head):
    c = jnp.arange(width)[:, None] // head
    return (c == jnp.arange(LANES)[None, :]).astype(F32)


def _prep_layer(l, ln1_g, w_in, q_norm_g, k_norm_g, dn_conv_w, dn_a_log, dn_dt_bias, dn_norm_g, w_out_a, w_out_b,
                w_o, ln2_g, w_rg, b_rg, w_re, b_re, w_e_gate, w_e_up, w_e_down):
    D = w_in.shape[1]
    a_w = 3 * 3 * SWA_GW
    dn_w = DN_HEADS * 3 * DN_DK
    hv = DN_HEADS * DN_DK
    w = w_in[l]
    splits = dict(att=w[:, :a_w], dn=w[:, a_w:a_w + dn_w],
                  ba=jnp.pad(w[:, a_w + dn_w:a_w + dn_w + 2 * DN_HEADS], ((0, 0), (0, LANES - 2 * DN_HEADS))),
                  gate=w[:, a_w + dn_w + 2 * DN_HEADS:])
    assert splits["gate"].shape[1] == hv + 2 * D
    tile_heads = lambda g: jnp.broadcast_to(g[:, None, :], (len(SWA_CONFIGS), SWA_HEADS, SWA_DIM)).reshape(len(SWA_CONFIGS), SWA_GW)
    qg, kg = tile_heads(q_norm_g[l]), tile_heads(k_norm_g[l])
    idx = jnp.arange(MXU) // SWA_DIM
    par = jnp.zeros((2, LANES), F32)
    par = par.at[0, DN_HEADS:2 * DN_HEADS].set(-jnp.exp(dn_a_log[l].astype(F32)))
    par = par.at[1, DN_HEADS:2 * DN_HEADS].set(dn_dt_bias[l].astype(F32))
    wr = jnp.pad(jnp.concatenate([w_rg[l], w_re[l]], axis=1), ((0, 0), (0, LANES - N_GROUPS - N_EXPERTS)))
    br = jnp.pad(jnp.concatenate([b_rg[l], b_re[l]]), (0, LANES - N_GROUPS - N_EXPERTS)).reshape(1, LANES)
    e8 = _head_indicator(hv, DN_DK)
    return dict(
        bf16={k: v.astype(BF16) for k, v in splits.items()},
        ln1=ln1_g[l].reshape(1, D), ln2=ln2_g[l].reshape(1, D),
        ng=jnp.stack([qg.reshape(1, -1), kg.reshape(1, -1)]), qg=qg, kg=kg,
        bd=(idx[:, None] == idx[None, :]).astype(BF16),
        conv_w=dn_conv_w[l], par=par, dng=dn_norm_g[l].reshape(1, DN_DK),
        wa=w_out_a[l].astype(BF16), wb=w_out_b[l].astype(BF16), wo=w_o[l].astype(BF16), wr=wr.astype(BF16), br=br,
        wg=w_e_gate[l].astype(BF16), wu=w_e_up[l].astype(BF16), wd=w_e_down[l].astype(BF16),
        e_dn=e8, etb=e8.T, etg=jnp.roll(e8, DN_HEADS, axis=1).T,
    )


def _layer_prompt(x, p):
    B, L, D = x.shape
    N = B * L
    x2d = x.reshape(N, D)
    bw = p["bf16"]
    pk0, pk1, pk2, t0, t1, t2 = _proj_attn(x, p["ln1"], bw["att"], p["ng"], p["bd"], tm=min(512, L))
    tmp = min(1024, N)
    raw = _proj_plain(x2d, p["ln1"], bw["dn"], tm=tmp, tn=1536, out_dtype=F32, name="proj_dn")
    gates = _proj_plain(x2d, p["ln1"], bw["gate"], tm=tmp, tn=1536, out_dtype=BF16, name="proj_gate")
    ba = _proj_plain(x2d, p["ln1"], bw["ba"], tm=tmp, tn=LANES, out_dtype=F32, name="proj_ba")
    os_, ls_ = [], []
    for pk in (pk0, pk1, pk2):
        d, M = pk.shape[1], pk.shape[2]
        o, lse = _attn(pk.reshape(B * d, M, pk.shape[3]), tq=min(256, M))
        os_.append(o.reshape(B, d, M, SWA_GW))
        ls_.append(lse.reshape(B, d, M, SWA_GW))
    raw3 = raw.reshape(B, L, -1)
    qd, kd, vd, gb = _dn_conv(raw3, jnp.zeros((B, 8, raw3.shape[2]), F32), p["conv_w"], ba.reshape(B, L, LANES),
                              p["par"], tl=min(256, L))
    u, w, qdec, kdec, a, gt = _dn_intra(qd, kd, vd, gb, tl=min(2048, L))
    od, s_new = _dn_scan(u, w, qdec, kdec, a, gt, jnp.zeros((B, DN_HEADS, DN_DK, LANES), F32), tl=min(1024, L))
    x2 = _out_proj(x2d, os_, ls_, od.reshape(N, -1), gates, p["dng"], p["wa"], p["wb"], p["wo"], B=B, L=L,
                   tm=min(512, L))
    y = _moe(x2, p["ln2"], p["wr"], p["br"], p["wg"], p["wu"], p["wd"], tm=256)
    return y.reshape(B, L, D), [t0, t1, t2], raw3[:, L - (DN_CONV - 1):], s_new


def _layer_sample(x, caches, conv_state, s0, layer, p):
    Bs, T, D = x.shape
    assert T == 1
    x2d = x.reshape(Bs, D)
    bw = p["bf16"]
    proj = functools.partial(_proj_plain, x2d, p["ln1"], tm=Bs, out_dtype=F32)
    z_att = proj(bw["att"], tn=1536, name="sproj_att")
    raw = proj(bw["dn"], tn=1536, name="sproj_dn")
    gates = proj(bw["gate"], tn=1536, name="sproj_gate")
    ba = proj(bw["ba"], tn=LANES, name="sproj_ba")
    oa, kv = _sample_attn(z_att.reshape(Bs, 1, -1), caches, layer, p["qg"], p["kg"])
    raw3 = raw.reshape(Bs, 1, -1)
    od, s_new = _sample_dn(raw3, conv_state, s0, layer, p["conv_w"], ba.reshape(Bs, 1, LANES), p["par"],
                           p["e_dn"], p["etb"], p["etg"])
    x2 = _sample_out(x2d, oa.reshape(Bs, -1), od.reshape(Bs, -1), gates, p["dng"], p["wa"], p["wb"], p["wo"])
    y = _moe(x2, p["ln2"], p["wr"], p["br"], p["wg"], p["wu"], p["wd"], tm=Bs)
    W2 = 2 * SWA_GW
    kvs = [kv[:, :, g * W2:(g + 1) * W2].reshape(Bs, 1, 2, SWA_HEADS, SWA_DIM) for g in range(len(SWA_CONFIGS))]
    new_conv = jnp.concatenate([conv_state[layer][:, 1:], raw3], axis=1)
    return y.reshape(Bs, 1, D), kvs, new_conv, s_new


def kernel(x_prompt, x_sample, cache_swa0_kv, cache_swa1_kv, cache_swa2_kv, state_dn_conv, state_dn_S, ln1_g, w_in,
           q_norm_g, k_norm_g, dn_conv_w, dn_a_log, dn_dt_bias, dn_norm_g, w_out_a, w_out_b, w_o, ln2_g, w_rg, b_rg,
           w_re, b_re, w_e_gate, w_e_up, w_e_down):
    yp, ys = x_prompt, x_sample
    outs = [[] for _ in range(10)]
    for l in range(w_in.shape[0]):
        p = _prep_layer(l, ln1_g, w_in, q_norm_g, k_norm_g, dn_conv_w, dn_a_log, dn_dt_bias, dn_norm_g, w_out_a,
                        w_out_b, w_o, ln2_g, w_rg, b_rg, w_re, b_re, w_e_gate, w_e_up, w_e_down)
        yp, pkv, pconv, ps = _layer_prompt(yp, p)
        ys, skv, sconv, ss = _layer_sample(ys, (cache_swa0_kv, cache_swa1_kv, cache_swa2_kv), state_dn_conv,
                                           state_dn_S, l, p)
        for lst, val in zip(outs, (*pkv, pconv, ps, *skv, sconv, ss)):
            lst.append(val)
    return (yp, ys, *(jnp.stack(o) for o in outs))
```

```python
import functools

import jax
import jax.numpy as jnp
from jax import lax
from jax.experimental import pallas as pl
from jax.experimental.pallas import tpu as pltpu

F32 = jnp.float32
BF16 = jnp.bfloat16
HI = lax.Precision.HIGHEST
EPS = 1e-6

SWA_CONFIGS = ((128, 1), (512, 4), (2048, 16))
SWA_HEADS = 8
SWA_DIM = 64
SWA_GW = SWA_HEADS * SWA_DIM
SWA_SPAN = 128
DN_HEADS = 8
DN_DK = 128
DN_CONV = 4
DN_CHUNK = 64
N_GROUPS = 4
PER_GROUP = 8
N_EXPERTS = N_GROUPS * PER_GROUP
TOP_K = 2

VMEM_LIMIT_BYTES = 56 * 1024 * 1024
LANES = 128
MXU = 256
MOE_ROWS = 256
ROW_UNROLL = 4


def _cparams(n_axes):
    return pltpu.CompilerParams(
        dimension_semantics=("arbitrary",) * n_axes, vmem_limit_bytes=VMEM_LIMIT_BYTES
    )


def _rms(x, g):
    return x * lax.rsqrt(jnp.mean(x * x, axis=-1, keepdims=True) + EPS) * g


def _bdot(a, b):
    return jnp.dot(a.astype(BF16), b.astype(BF16), preferred_element_type=F32)


def _hdot(a, b):
    return jnp.dot(a, b, preferred_element_type=F32, precision=HI)


def _sigmoid(x):
    return 1.0 / (1.0 + jnp.exp(-x))


def _silu(x):
    return x * _sigmoid(x)


def _softplus(x):
    return jnp.maximum(x, 0.0) + jnp.log1p(jnp.exp(-jnp.abs(x)))


def _proj_attn_kernel(x_ref, lng_ref, w_ref, ng_ref, bd_ref, p0, p1, p2, t0, t1, t2, h_scr, z_scr,
                      *, tm, dils, tail_rows, tail_first):
    j = pl.program_id(2)

    @pl.when(j == 0)
    def _():
        h_scr[...] = _rms(x_ref[0], lng_ref[...]).astype(BF16)

    z = jnp.dot(h_scr[...], w_ref[...], preferred_element_type=F32)
    n_cb = z_scr.shape[0]
    for c in range(n_cb):
        z_scr[c] = z[:, c * LANES:(c + 1) * LANES]

    @pl.when(j < 2)
    def _():
        for c in range(0, n_cb, 2):
            zc = jnp.concatenate([z_scr[c], z_scr[c + 1]], axis=-1)
            ss = jnp.dot((zc * zc).astype(BF16), bd_ref[...], preferred_element_type=F32)
            zn = zc * lax.rsqrt(ss * (1.0 / SWA_DIM) + EPS) * ng_ref[0, :, c * LANES:(c + 2) * LANES]
            z_scr[c] = zn[:, :LANES]
            z_scr[c + 1] = zn[:, LANES:]

    outs = (p0, p1, p2)
    tails = (t0, t1, t2)
    per_g = SWA_GW // LANES
    for sec in range(3):

        @pl.when(j == sec)
        def _(sec=sec):
            for gi, d in enumerate(dils):
                for cb in range(per_g):
                    c = gi * per_g + cb
                    col = sec * SWA_GW + cb * LANES
                    for r in range(d):
                        src = z_scr[c] if d == 1 else z_scr[c, pl.ds(r, tm // d, stride=d), :]
                        outs[gi][0, r, :, col:col + LANES] = src.astype(BF16)
                if sec >= 1:
                    @pl.when(pl.program_id(1) >= tail_first[gi])
                    def _(gi=gi):
                        rows = tail_rows[gi]
                        zr = jnp.concatenate([z_scr[gi * per_g + cb, tm - rows:tm, :] for cb in range(per_g)], axis=-1)
                        tails[gi][0, :, sec - 1] = zr.reshape(rows, SWA_HEADS, SWA_DIM)


def _proj_attn(x, ln_g, w_att, ng, bd, *, tm):
    B, L, D = x.shape
    nt = L // tm
    dils = tuple(d for _, d in SWA_CONFIGS)
    keeps = tuple(min(w, L) for w, _ in SWA_CONFIGS)
    tail_rows = tuple(min(k, tm) for k in keeps)
    for k, r in zip(keeps, tail_rows):
        assert k % r == 0 and L % tm == 0
    W3 = 3 * SWA_GW

    tail_first = tuple(nt - k // r for k, r in zip(keeps, tail_rows))

    def tail_spec(rows, first):
        return pl.BlockSpec((1, rows, 2, SWA_HEADS, SWA_DIM),
                            lambda b, i, j: (b, jnp.maximum(i - first, 0), 0, 0, 0))

    out_shape = [jax.ShapeDtypeStruct((B, d, L // d, W3), BF16) for d in dils]
    out_shape += [jax.ShapeDtypeStruct((B, k, 2, SWA_HEADS, SWA_DIM), F32) for k in keeps]
    out_specs = [pl.BlockSpec((1, d, tm // d, W3), lambda b, i, j: (b, 0, i, 0)) for d in dils]
    out_specs += [tail_spec(r, f) for r, f in zip(tail_rows, tail_first)]
    return pl.pallas_call(
        functools.partial(_proj_attn_kernel, tm=tm, dils=dils, tail_rows=tail_rows, tail_first=tail_first),
        grid=(B, nt, 3),
        in_specs=[
            pl.BlockSpec((1, tm, D), lambda b, i, j: (b, i, 0)),
            pl.BlockSpec((1, D), lambda b, i, j: (0, 0)),
            pl.BlockSpec((D, W3), lambda b, i, j: (0, j)),
            pl.BlockSpec((1, 1, W3), lambda b, i, j: (jnp.minimum(j, 1), 0, 0)),
            pl.BlockSpec((MXU, MXU), lambda b, i, j: (0, 0)),
        ],
        out_specs=out_specs,
        out_shape=out_shape,
        scratch_shapes=[pltpu.VMEM((tm, D), BF16), pltpu.VMEM((W3 // LANES, tm, LANES), F32)],
        compiler_params=_cparams(3),
        name="proj_attn",
    )(x, ln_g, w_att, ng, bd)


def _proj_plain_kernel(x_ref, lng_ref, w_ref, o_ref, h_scr):
    @pl.when(pl.program_id(1) == 0)
    def _():
        h_scr[...] = _rms(x_ref[...], lng_ref[...]).astype(BF16)

    o_ref[...] = jnp.dot(h_scr[...], w_ref[...], preferred_element_type=F32).astype(o_ref.dtype)


def _proj_plain(x2d, ln_g, w, *, tm, tn, out_dtype, name="proj_plain"):
    N, D = x2d.shape
    C = w.shape[1]
    assert N % tm == 0 and C % tn == 0
    return pl.pallas_call(
        _proj_plain_kernel,
        grid=(N // tm, C // tn),
        in_specs=[
            pl.BlockSpec((tm, D), lambda i, j: (i, 0)),
            pl.BlockSpec((1, D), lambda i, j: (0, 0)),
            pl.BlockSpec((D, tn), lambda i, j: (0, j)),
        ],
        out_specs=pl.BlockSpec((tm, tn), lambda i, j: (i, j)),
        out_shape=jax.ShapeDtypeStruct((N, C), out_dtype),
        scratch_shapes=[pltpu.VMEM((tm, D), BF16)],
        compiler_params=_cparams(2),
        name=name,
    )(x2d, ln_g, w)


def _attn_kernel(q_ref, kc_ref, vc_ref, kp_ref, vp_ref, o_ref, lse_ref, kk_scr, vv_scr, *, tq):
    i = pl.program_id(1)
    blk = SWA_SPAN
    kk_scr[0:blk, :] = kp_ref[0]
    kk_scr[blk:blk + tq, :] = kc_ref[0]
    vv_scr[0:blk, :] = vp_ref[0]
    vv_scr[blk:blk + tq, :] = vc_ref[0]
    qi = lax.broadcasted_iota(jnp.int32, (blk, 2 * blk), 0)
    ki = lax.broadcasted_iota(jnp.int32, (blk, 2 * blk), 1)
    dist = blk + qi - ki
    band = (dist >= 0) & (dist <= SWA_SPAN)
    band_first = band & ((ki >= blk) | (i > 0))
    lo = lax.broadcasted_iota(jnp.int32, (blk, LANES), 1) < SWA_DIM
    zero = jnp.zeros((blk, LANES), BF16)
    for jb in range(tq // blk):
        mask = band_first if jb == 0 else band
        rows = slice(jb * blk, (jb + 1) * blk)
        for hp in range(SWA_GW // LANES):
            cs = slice(hp * LANES, (hp + 1) * LANES)
            qb = q_ref[0, rows, cs]
            kk = kk_scr[jb * blk:(jb + 2) * blk, cs]
            vv = vv_scr[jb * blk:(jb + 2) * blk, cs]
            res_o, res_l = [], []
            for hh in range(2):
                qm = jnp.where(lo if hh == 0 else jnp.logical_not(lo), qb, zero)
                s = lax.dot_general(qm, kk, (((1,), (1,)), ((), ())), preferred_element_type=F32)
                s = jnp.where(mask, s * (SWA_DIM ** -0.5), -jnp.inf)
                m = jnp.max(s, axis=-1, keepdims=True)
                p = jnp.exp(s - m)
                den = jnp.sum(p, axis=-1, keepdims=True)
                pv = jnp.dot(p.astype(BF16), vv, preferred_element_type=F32)
                res_o.append(pv / den)
                res_l.append(jnp.broadcast_to(m + jnp.log(den), (blk, LANES)))
            o_ref[0, rows, cs] = jnp.where(lo, res_o[0], res_o[1]).astype(BF16)
            lse_ref[0, rows, cs] = jnp.where(lo, res_l[0], res_l[1])


def _attn(p, *, tq):
    S, M, _ = p.shape
    assert M % tq == 0 and tq % SWA_SPAN == 0
    nb = tq // SWA_SPAN
    return pl.pallas_call(
        functools.partial(_attn_kernel, tq=tq),
        grid=(S, M // tq),
        in_specs=[
            pl.BlockSpec((1, tq, SWA_GW), lambda s, i: (s, i, 0)),
            pl.BlockSpec((1, tq, SWA_GW), lambda s, i: (s, i, 1)),
            pl.BlockSpec((1, tq, SWA_GW), lambda s, i: (s, i, 2)),
            pl.BlockSpec((1, SWA_SPAN, SWA_GW), lambda s, i: (s, jnp.maximum(i * nb - 1, 0), 1)),
            pl.BlockSpec((1, SWA_SPAN, SWA_GW), lambda s, i: (s, jnp.maximum(i * nb - 1, 0), 2)),
        ],
        out_specs=[
            pl.BlockSpec((1, tq, SWA_GW), lambda s, i: (s, i, 0)),
            pl.BlockSpec((1, tq, SWA_GW), lambda s, i: (s, i, 0)),
        ],
        out_shape=[
            jax.ShapeDtypeStruct((S, M, SWA_GW), BF16),
            jax.ShapeDtypeStruct((S, M, SWA_GW), F32),
        ],
        scratch_shapes=[
            pltpu.VMEM((SWA_SPAN + tq, SWA_GW), BF16),
            pltpu.VMEM((SWA_SPAN + tq, SWA_GW), BF16),
        ],
        compiler_params=_cparams(2),
        name="swa_attn",
    )(p, p, p, p, p)


def _proj_dn_kernel(x_ref, lng_ref, w_ref, cw_ref, q_ref, k_ref, v_ref, tail_ref, h_scr, z_scr, carry_scr,
                    *, tm, n_ct):
    i = pl.program_id(1)
    j = pl.program_id(2)
    nh = DN_HEADS
    ncb = z_scr.shape[1] // LANES

    @pl.when(j == 0)
    def _():
        h_scr[...] = _rms(x_ref[0], lng_ref[...]).astype(BF16)

    z_scr[0:8, :] = jnp.where(i == 0, 0.0, carry_scr[j])
    z_scr[8:8 + tm, :] = jnp.dot(h_scr[...], w_ref[...], preferred_element_type=F32)
    last = z_scr[tm:tm + 8, :]
    carry_scr[j] = last
    tn = z_scr.shape[1]
    outs = (q_ref, k_ref, v_ref)
    for jj in range(n_ct):

        @pl.when(j == jj)
        def _(jj=jj):
            tail_ref[0, :, jj * tn:(jj + 1) * tn] = last
            for cbl in range(ncb):
                cs = slice(cbl * LANES, (cbl + 1) * LANES)
                part, h = divmod(jj * ncb + cbl, nh)
                xe = z_scr[:, cs]
                acc = cw_ref[0:1, cs] * xe
                for t in range(1, DN_CONV):
                    acc = cw_ref[t:t + 1, cs] * xe + pltpu.roll(acc, 1, axis=0)
                act = _silu(acc[8:])
                if part < 2:
                    act = act * lax.rsqrt(jnp.sum(act * act, axis=-1, keepdims=True) + EPS)
                if part == 0:
                    act = act * (DN_DK ** -0.5)
                outs[part][0, :, h * LANES:(h + 1) * LANES] = act.astype(BF16)


def _proj_dn(x, ln_g, w_dn, conv_w, *, tm):
    B, L, D = x.shape
    C = w_dn.shape[1]
    width = DN_HEADS * DN_DK
    n_ct = 2
    tn = C // n_ct
    assert L % tm == 0 and C == 3 * width and tn % LANES == 0
    qkv = pl.BlockSpec((1, tm, width), lambda b, i, j: (b, i, 0))
    return pl.pallas_call(
        functools.partial(_proj_dn_kernel, tm=tm, n_ct=n_ct),
        grid=(B, L // tm, n_ct),
        in_specs=[
            pl.BlockSpec((1, tm, D), lambda b, i, j: (b, i, 0)),
            pl.BlockSpec((1, D), lambda b, i, j: (0, 0)),
            pl.BlockSpec((D, tn), lambda b, i, j: (0, j)),
            pl.BlockSpec((DN_CONV, tn), lambda b, i, j: (0, j)),
        ],
        out_specs=[qkv, qkv, qkv, pl.BlockSpec((1, 8, C), lambda b, i, j: (b, 0, 0))],
        out_shape=[jax.ShapeDtypeStruct((B, L, width), BF16)] * 3 + [jax.ShapeDtypeStruct((B, 8, C), F32)],
        scratch_shapes=[pltpu.VMEM((tm, D), BF16), pltpu.VMEM((8 + tm, tn), F32), pltpu.VMEM((n_ct, 8, tn), F32)],
        compiler_params=_cparams(3),
        name="proj_dn",
    )(x, ln_g, w_dn, conv_w)


def _gates_kernel(ba_ref, par_ref, g_ref, gt_ref, *, tl):
    nh = DN_HEADS
    ba = ba_ref[0]
    lane = lax.broadcasted_iota(jnp.int32, (tl, LANES), 1)
    g = par_ref[0:1, :] * _softplus(ba + par_ref[1:2, :])
    ri = lax.broadcasted_iota(jnp.int32, (tl, tl), 0)
    ci = lax.broadcasted_iota(jnp.int32, (tl, tl), 1)
    tri = jnp.where((ri // DN_CHUNK == ci // DN_CHUNK) & (ci <= ri), 1.0, 0.0).astype(F32)
    gc = _hdot(tri, g)
    g_ref[0] = jnp.where(lane < nh, _sigmoid(ba), gc)
    gct = lax.dot_general(g, tri, (((0,), (1,)), ((), ())), preferred_element_type=F32, precision=HI)
    gt_ref[0] = gct[nh:2 * nh, :]


def _dn_gates(ba, par, *, tl):
    B, L, _ = ba.shape
    assert L % tl == 0 and tl % DN_CHUNK == 0
    return pl.pallas_call(
        functools.partial(_gates_kernel, tl=tl),
        grid=(B, L // tl),
        in_specs=[pl.BlockSpec((1, tl, LANES), lambda b, i: (b, i, 0)), pl.BlockSpec((2, LANES), lambda b, i: (0, 0))],
        out_specs=[pl.BlockSpec((1, tl, LANES), lambda b, i: (b, i, 0)),
                   pl.BlockSpec((1, DN_HEADS, tl), lambda b, i: (b, 0, i))],
        out_shape=[jax.ShapeDtypeStruct((B, L, LANES), F32), jax.ShapeDtypeStruct((B, DN_HEADS, L), F32)],
        compiler_params=_cparams(2),
        name="dn_gates",
    )(ba, par)


def _intra_kernel(q_ref, k_ref, v_ref, g_ref, gt_in_ref, u_ref, w_ref, qd_ref, kd_ref, a_ref, gt_ref, *, tl):
    h = pl.program_id(1)
    C = DN_CHUNK
    lane = lax.broadcasted_iota(jnp.int32, (C, LANES), 1)
    ri = lax.broadcasted_iota(jnp.int32, (C, C), 0)
    ci = lax.broadcasted_iota(jnp.int32, (C, C), 1)
    eye = jnp.where(ri == ci, 1.0, 0.0).astype(F32)
    nt_dot = lambda a, b: lax.dot_general(a.astype(BF16), b.astype(BF16), (((1,), (1,)), ((), ())),
                                          preferred_element_type=F32)
    rows = [slice(c * C, (c + 1) * C) for c in range(tl // C)]
    gv = [g_ref[0, r, :] for r in rows]
    q = [q_ref[0, r, :].astype(F32) for r in rows]
    k = [k_ref[0, r, :].astype(F32) for r in rows]
    v = [v_ref[0, r, :].astype(F32) for r in rows]
    beta = [jnp.sum(jnp.where(lane == h, x, 0.0), axis=-1, keepdims=True) for x in gv]
    gc = [jnp.sum(jnp.where(lane == h + DN_HEADS, x, 0.0), axis=-1, keepdims=True) for x in gv]
    gc_row = gt_in_ref[0, pl.ds(h, 1), :]
    decay = [jnp.exp(jnp.where(ri >= ci, a - gc_row[:, r], -jnp.inf)) for a, r in zip(gc, rows)]
    kb = [a * b for a, b in zip(k, beta)]
    x = [-jnp.where(ri > ci, nt_dot(a, b) * d, 0.0) for a, b, d in zip(kb, k, decay)]
    t = [eye + a for a in x]
    for _ in range(5):
        x = [_bdot(a, a) for a in x]
        t = [a + _bdot(a, b) for a, b in zip(t, x)]
    eg = [jnp.exp(a) for a in gc]
    glast = [a[C - 1:C, :] for a in gc]
    u = [_bdot(a, b * c) for a, b, c in zip(t, v, beta)]
    w = [_bdot(a, b * c) for a, b, c in zip(t, kb, eg)]
    qk = [nt_dot(a, b) for a, b in zip(q, k)]
    for c, r in enumerate(rows):
        u_ref[0, 0, r, :] = u[c]
        w_ref[0, 0, r, :] = w[c].astype(BF16)
        a_ref[0, 0, r, :] = (qk[c] * decay[c]).astype(BF16)
        qd_ref[0, 0, r, :] = (q[c] * eg[c]).astype(BF16)
        kd_ref[0, 0, r, :] = (k[c] * jnp.exp(glast[c] - gc[c])).astype(BF16)
        gt_ref[0, 0, c:c + 1, :] = jnp.broadcast_to(jnp.exp(glast[c]), (1, LANES))


def _dn_intra(q, k, v, g, gt_rows, *, tl):
    B, L, _ = q.shape
    H, C = DN_HEADS, DN_CHUNK
    assert L % tl == 0 and (tl // C) % 8 == 0
    qkv_spec = pl.BlockSpec((1, tl, LANES), lambda b, h, i: (b, i, h))
    hl = lambda w: pl.BlockSpec((1, 1, tl, w), lambda b, h, i: (b, h, i, 0))
    return pl.pallas_call(
        functools.partial(_intra_kernel, tl=tl),
        grid=(B, H, L // tl),
        in_specs=[qkv_spec, qkv_spec, qkv_spec, pl.BlockSpec((1, tl, LANES), lambda b, h, i: (b, i, 0)),
                  pl.BlockSpec((1, H, tl), lambda b, h, i: (b, 0, i))],
        out_specs=[hl(LANES), hl(LANES), hl(LANES), hl(LANES), hl(C),
                   pl.BlockSpec((1, 1, tl // C, LANES), lambda b, h, i: (b, h, i, 0))],
        out_shape=[
            jax.ShapeDtypeStruct((B, H, L, LANES), F32),
            jax.ShapeDtypeStruct((B, H, L, LANES), BF16),
            jax.ShapeDtypeStruct((B, H, L, LANES), BF16),
            jax.ShapeDtypeStruct((B, H, L, LANES), BF16),
            jax.ShapeDtypeStruct((B, H, L, C), BF16),
            jax.ShapeDtypeStruct((B, H, L // C, LANES), F32),
        ],
        compiler_params=_cparams(3),
        name="dn_intra",
    )(q, k, v, g, gt_rows)


def _scan_kernel(u_ref, w_ref, qd_ref, kd_ref, a_ref, gt_ref, s0_ref, o_ref, s_ref, *, n_chunks):
    C = DN_CHUNK
    H = s_ref.shape[1]

    @pl.when(pl.program_id(1) == 0)
    def _():
        s_ref[...] = s0_ref[...]

    def body(c, carry):
        rows = pl.ds(pl.multiple_of(c * C, C), C)
        S = [s_ref[0, h] for h in range(H)]
        Sb = [x.astype(BF16) for x in S]
        v_new = [u_ref[0, h, rows, :] - jnp.dot(w_ref[0, h, rows, :], Sb[h], preferred_element_type=F32)
                 for h in range(H)]
        vb = [x.astype(BF16) for x in v_new]
        o = [jnp.dot(qd_ref[0, h, rows, :], Sb[h], preferred_element_type=F32)
             + jnp.dot(a_ref[0, h, rows, :], vb[h], preferred_element_type=F32) for h in range(H)]
        upd = [lax.dot_general(kd_ref[0, h, rows, :], vb[h], (((0,), (0,)), ((), ())), preferred_element_type=F32)
               for h in range(H)]
        for h in range(H):
            o_ref[0, rows, h * LANES:(h + 1) * LANES] = o[h]
            s_ref[0, h] = S[h] * gt_ref[0, h, pl.ds(c, 1), :] + upd[h]
        return carry

    lax.fori_loop(0, n_chunks, body, 0)


def _dn_scan(u, w, qd, kd, a, gt, s0, *, tl):
    B, H, L, _ = u.shape
    C = DN_CHUNK
    assert L % tl == 0 and (tl // C) % 8 == 0
    hs = lambda wd: pl.BlockSpec((1, H, tl, wd), lambda b, i: (b, 0, i, 0))
    s_spec = pl.BlockSpec((1, H, DN_DK, LANES), lambda b, i: (b, 0, 0, 0))
    return pl.pallas_call(
        functools.partial(_scan_kernel, n_chunks=tl // C),
        grid=(B, L // tl),
        in_specs=[hs(LANES), hs(LANES), hs(LANES), hs(LANES), hs(C),
                  pl.BlockSpec((1, H, tl // C, LANES), lambda b, i: (b, 0, i, 0)), s_spec],
        out_specs=[pl.BlockSpec((1, tl, H * LANES), lambda b, i: (b, i, 0)), s_spec],
        out_shape=[jax.ShapeDtypeStruct((B, L, H * LANES), F32),
                   jax.ShapeDtypeStruct((B, H, DN_DK, LANES), F32)],
        compiler_params=_cparams(2),
        name="dn_scan",
    )(u, w, qd, kd, a, gt, s0)


def _gated_mix(o_a, od, gates, dng, wa, wb, wo, x, dot):
    width = DN_HEADS * DN_DK
    parts = []
    for h in range(DN_HEADS):
        blk = od[:, h * LANES:(h + 1) * LANES]
        parts.append(blk * lax.rsqrt(jnp.mean(blk * blk, axis=-1, keepdims=True) + EPS) * dng)
    odn = jnp.concatenate(parts, axis=-1) * _silu(gates[:, 0:width].astype(F32))
    ya = dot(o_a, wa)
    yb = dot(odn, wb)
    mix = _sigmoid(gates[:, width:2 * width].astype(F32)) * ya + _sigmoid(gates[:, 2 * width:].astype(F32)) * yb
    return x + dot(mix, wo)


def _out_kernel(x_ref, o0, o1, o2, l0, l1, l2, od_ref, gates_ref, dng_ref, wa_ref, wb_ref, wo_ref, y_ref,
                so0, so1, so2, sl0, sl1, sl2, *, tm, dils):
    o_refs, l_refs = (o0, o1, o2), (l0, l1, l2)
    so, sl = (so0, so1, so2), (sl0, sl1, sl2)
    parts = []
    for cb in range(SWA_GW // LANES):
        cs = slice(cb * LANES, (cb + 1) * LANES)
        for gi, d in enumerate(dils):
            for r in range(d):
                dst = slice(None) if d == 1 else pl.ds(r, tm // d, stride=d)
                so[gi][cb, dst, :] = o_refs[gi][0, r, :, cs].astype(F32)
                sl[gi][cb, dst, :] = l_refs[gi][0, r, :, cs]
        ls = [s[cb] for s in sl]
        m = jnp.maximum(jnp.maximum(ls[0], ls[1]), ls[2])
        es = [jnp.exp(l - m) for l in ls]
        parts.append((es[0] * so[0][cb] + es[1] * so[1][cb] + es[2] * so[2][cb]) / (es[0] + es[1] + es[2]))
    o_a = jnp.concatenate(parts, axis=-1)
    y_ref[...] = _gated_mix(o_a, od_ref[...], gates_ref[...], dng_ref[...], wa_ref[...], wb_ref[...],
                            wo_ref[...], x_ref[...], _bdot)


def _out_proj(x2d, os_, ls_, od2d, gates, dng, wa, wb, wo, *, B, L, tm):
    N, D = x2d.shape
    nt = L // tm
    dils = tuple(d for _, d in SWA_CONFIGS)
    grp = lambda d: pl.BlockSpec((1, d, tm // d, SWA_GW), lambda i: (i // nt, 0, i % nt, 0))
    row = lambda w: pl.BlockSpec((tm, w), lambda i: (i, 0))
    full = lambda a: pl.BlockSpec(a.shape, lambda i: (0, 0))
    return pl.pallas_call(
        functools.partial(_out_kernel, tm=tm, dils=dils),
        grid=(N // tm,),
        in_specs=[row(D)] + [grp(d) for d in dils] * 2 + [row(od2d.shape[1]), row(gates.shape[1]),
                                                          full(dng), full(wa), full(wb), full(wo)],
        out_specs=row(D),
        out_shape=jax.ShapeDtypeStruct((N, D), F32),
        scratch_shapes=[pltpu.VMEM((SWA_GW // LANES, tm, LANES), F32)] * 6,
        compiler_params=_cparams(1),
        name="out_proj",
    )(x2d, *os_, *ls_, od2d, gates, dng, wa, wb, wo)


def _router_kernel(x_ref, lng_ref, wr_ref, br_ref, info_ref, cnt_ref, base_scr, *, tm):
    i = pl.program_id(0)

    @pl.when(i == 0)
    def _():
        base_scr[...] = jnp.zeros_like(base_scr)

    h = _rms(x_ref[...], lng_ref[...])
    lg = _bdot(h, wr_ref[...]) + br_ref[...]
    lane = lax.broadcasted_iota(jnp.int32, (tm, LANES), 1)
    big = jnp.int32(1 << 20)
    ninf = -jnp.inf

    def argmax_lane(vals):
        mx = jnp.max(vals, axis=-1, keepdims=True)
        idx = jnp.min(jnp.where(vals == mx, lane, big), axis=-1, keepdims=True)
        return mx, idx

    lgm = jnp.where(lane < N_GROUPS, lg, ninf)
    mg, gsel = argmax_lane(lgm)
    pg = 1.0 / jnp.sum(jnp.exp(lgm - mg), axis=-1, keepdims=True)
    start = N_GROUPS + gsel * PER_GROUP
    le = jnp.where((lane >= start) & (lane < start + PER_GROUP), lg, ninf)
    m1, i1 = argmax_lane(le)
    m2, i2 = argmax_lane(jnp.where(lane == i1, ninf, le))
    e21 = jnp.exp(m2 - m1)
    w1 = pg / (1.0 + e21)
    w2 = pg * e21 / (1.0 + e21)
    oh = jnp.where(lane == i1, 1.0, 0.0) + jnp.where(lane == i2, 1.0, 0.0)
    ri = lax.broadcasted_iota(jnp.int32, (tm, tm), 0)
    ci = lax.broadcasted_iota(jnp.int32, (tm, tm), 1)
    strict = jnp.where(ci < ri, 1.0, 0.0).astype(BF16)
    pref = jnp.dot(strict, oh.astype(BF16), preferred_element_type=F32) + base_scr[...]
    r1 = jnp.sum(jnp.where(lane == i1, pref, 0.0), axis=-1, keepdims=True)
    r2 = jnp.sum(jnp.where(lane == i2, pref, 0.0), axis=-1, keepdims=True)
    base_scr[...] = base_scr[...] + jnp.sum(oh, axis=0, keepdims=True)
    cnt_ref[...] = base_scr[...]
    off = jnp.float32(N_GROUPS)
    info = jnp.where(lane == 0, i1.astype(F32) - off, 0.0)
    info = jnp.where(lane == 1, i2.astype(F32) - off, info)
    info = jnp.where(lane == 2, w1, info)
    info = jnp.where(lane == 3, w2, info)
    info = jnp.where(lane == 4, r1, info)
    info = jnp.where(lane == 5, r2, info)
    info_ref[...] = info


def _router(x2d, ln_g, wr, br, *, tm):
    N, D = x2d.shape
    assert N % tm == 0
    return pl.pallas_call(
        functools.partial(_router_kernel, tm=tm),
        grid=(N // tm,),
        in_specs=[
            pl.BlockSpec((tm, D), lambda i: (i, 0)),
            pl.BlockSpec((1, D), lambda i: (0, 0)),
            pl.BlockSpec((D, LANES), lambda i: (0, 0)),
            pl.BlockSpec((1, LANES), lambda i: (0, 0)),
        ],
        out_specs=[pl.BlockSpec((tm, LANES), lambda i: (i, 0)), pl.BlockSpec((1, LANES), lambda i: (0, 0))],
        out_shape=[jax.ShapeDtypeStruct((N, LANES), F32), jax.ShapeDtypeStruct((1, LANES), F32)],
        scratch_shapes=[pltpu.VMEM((1, LANES), F32)],
        compiler_params=_cparams(1),
        name="router",
    )(x2d, ln_g, wr, br)


def _dispatch_kernel(dest_ref, zb_ref, x_ref, xs_ref, zero_scr, rows_scr, sem, *, tm, tb, n_zb, n_tiles):
    i = pl.program_id(0)

    @pl.when(i == 0)
    def _():
        zero_scr[...] = jnp.zeros_like(zero_scr)

        def zero_copy(n):
            return pltpu.make_async_copy(zero_scr, xs_ref.at[pl.ds(zb_ref[n] * tb, tb)], sem.at[2])

        def zero_issue(n, carry):
            @pl.when(zb_ref[n] >= 0)
            def _():
                zero_copy(n).start()

            return carry

        def zero_wait(n, carry):
            @pl.when(zb_ref[n] >= 0)
            def _():
                zero_copy(n).wait()

            return carry

        lax.fori_loop(0, n_zb, zero_issue, 0)
        lax.fori_loop(0, n_zb, zero_wait, 0)

    buf_now = lax.rem(i, 2)
    rows_scr[buf_now] = x_ref[...].reshape(rows_scr.shape[1:])

    def row_copy(tile, t, slot):
        buf = lax.rem(tile, 2)
        return pltpu.make_async_copy(
            rows_scr.at[buf, pl.ds(t, 1)],
            xs_ref.at[pl.ds(dest_ref[(tile * tm + t) * TOP_K + slot], 1)], sem.at[buf])

    def issue(tt, carry):
        for r in range(ROW_UNROLL):
            for slot in range(TOP_K):
                row_copy(i, tt * ROW_UNROLL + r, slot).start(priority=slot)
        return carry

    def drain(tile):
        buf = lax.rem(tile, 2)
        for _ in range(TOP_K):
            pltpu.make_async_copy(rows_scr.at[buf], rows_scr.at[buf], sem.at[buf]).wait()

    lax.fori_loop(0, tm // ROW_UNROLL, issue, 0)

    @pl.when(i > 0)
    def _():
        drain(i - 1)

    @pl.when(i == n_tiles - 1)
    def _():
        drain(i)


def _dispatch(dest, zero_blocks, x2d, *, tm, tb, n_rows):
    N, D = x2d.shape
    return pl.pallas_call(
        functools.partial(_dispatch_kernel, tm=tm, tb=tb, n_zb=zero_blocks.shape[0], n_tiles=N // tm),
        grid_spec=pltpu.PrefetchScalarGridSpec(
            num_scalar_prefetch=2,
            grid=(N // tm,),
            in_specs=[pl.BlockSpec((tm, D), lambda i, d, z: (i, 0))],
            out_specs=pl.BlockSpec(memory_space=pl.ANY),
            scratch_shapes=[pltpu.VMEM((tb, D // LANES, LANES), F32), pltpu.VMEM((2, tm, D // LANES, LANES), F32),
                            pltpu.SemaphoreType.DMA((3,))],
        ),
        out_shape=jax.ShapeDtypeStruct((n_rows, D // LANES, LANES), F32),
        compiler_params=_cparams(1),
        name="moe_dispatch",
    )(dest, zero_blocks, x2d)


def _ffn_kernel(be_ref, nb_ref, xs_ref, lng_ref, wg_ref, wu_ref, wd_ref, y_ref):
    used = pl.program_id(0) < nb_ref[0]
    tb = xs_ref.shape[0]

    @pl.when(used)
    def _():
        h = _rms(xs_ref[...].reshape(tb, -1), lng_ref[...]).astype(BF16)
        g = jnp.dot(h, wg_ref[0], preferred_element_type=F32)
        u = jnp.dot(h, wu_ref[0], preferred_element_type=F32)
        y = jnp.dot((_silu(g) * u).astype(BF16), wd_ref[0], preferred_element_type=F32)
        y_ref[...] = y.reshape(y_ref.shape)

    @pl.when(jnp.logical_not(used))
    def _():
        y_ref[...] = jnp.zeros_like(y_ref)


def _ffn(blk_e, nb_used, xs, ln_g, wg, wu, wd, *, tb):
    P, S, _ = xs.shape
    D = S * LANES
    nb = P // tb
    DE = wg.shape[2]
    return pl.pallas_call(
        _ffn_kernel,
        grid_spec=pltpu.PrefetchScalarGridSpec(
            num_scalar_prefetch=2,
            grid=(nb,),
            in_specs=[
                pl.BlockSpec((tb, S, LANES), lambda i, be, nbu: (jnp.minimum(i, nbu[0] - 1), 0, 0)),
                pl.BlockSpec((1, D), lambda i, be, nbu: (0, 0)),
                pl.BlockSpec((1, D, DE), lambda i, be, nbu: (be[i], 0, 0)),
                pl.BlockSpec((1, D, DE), lambda i, be, nbu: (be[i], 0, 0)),
                pl.BlockSpec((1, DE, D), lambda i, be, nbu: (be[i], 0, 0)),
            ],
            out_specs=pl.BlockSpec((tb, S, LANES), lambda i, be, nbu: (i, 0, 0)),
        ),
        out_shape=jax.ShapeDtypeStruct((P, S, LANES), F32),
        compiler_params=_cparams(1),
        name="moe_ffn",
    )(blk_e, nb_used, xs, ln_g, wg, wu, wd)


def _combine_kernel(dest_ref, x_ref, info_ref, yb_ref, y_ref, g_scr, sem, *, tm, n_tiles):
    i = pl.program_id(0)

    def row_copy(tile, t, slot):
        buf = lax.rem(tile, 2)
        return pltpu.make_async_copy(
            yb_ref.at[pl.ds(dest_ref[(tile * tm + t) * TOP_K + slot], 1)],
            g_scr.at[buf, slot, pl.ds(t, 1)], sem.at[buf])

    def issue_tile(tile):
        def body(tt, carry):
            for r in range(ROW_UNROLL):
                for slot in range(TOP_K):
                    row_copy(tile, tt * ROW_UNROLL + r, slot).start(priority=slot)
            return carry

        lax.fori_loop(0, tm // ROW_UNROLL, body, 0)

    @pl.when(i == 0)
    def _():
        issue_tile(i)

    @pl.when(i + 1 < n_tiles)
    def _():
        issue_tile(i + 1)

    buf = lax.rem(i, 2)
    pltpu.make_async_copy(g_scr.at[buf], g_scr.at[buf], sem.at[buf]).wait()
    info = info_ref[...]
    lane = lax.broadcasted_iota(jnp.int32, info.shape, 1)
    w1 = jnp.sum(jnp.where(lane == 2, info, 0.0), axis=-1, keepdims=True)
    w2 = jnp.sum(jnp.where(lane == 3, info, 0.0), axis=-1, keepdims=True)
    g1 = g_scr[buf, 0].reshape(x_ref.shape)
    g2 = g_scr[buf, 1].reshape(x_ref.shape)
    y_ref[...] = x_ref[...] + (w1 * g1 + w2 * g2)


def _combine(dest, x2d, info, yb, *, tm):
    N, D = x2d.shape
    return pl.pallas_call(
        functools.partial(_combine_kernel, tm=tm, n_tiles=N // tm),
        grid_spec=pltpu.PrefetchScalarGridSpec(
            num_scalar_prefetch=1,
            grid=(N // tm,),
            in_specs=[
                pl.BlockSpec((tm, D), lambda i, d: (i, 0)),
                pl.BlockSpec((tm, LANES), lambda i, d: (i, 0)),
                pl.BlockSpec(memory_space=pl.ANY),
            ],
            out_specs=pl.BlockSpec((tm, D), lambda i, d: (i, 0)),
            scratch_shapes=[pltpu.VMEM((2, TOP_K, tm, D // LANES, LANES), F32), pltpu.SemaphoreType.DMA((2,))],
        ),
        out_shape=jax.ShapeDtypeStruct((N, D), F32),
        compiler_params=_cparams(1),
        name="moe_combine",
    )(dest, x2d, info, yb)


def _moe(x2d, ln2_g, wr, br, wg, wu, wd, *, tm):
    N, D = x2d.shape
    tb = MOE_ROWS
    info, counts = _router(x2d, ln2_g, wr, br, tm=tm)
    counts = counts[0, N_GROUPS:N_GROUPS + N_EXPERTS].astype(jnp.int32)
    pcounts = (counts + tb - 1) // tb * tb
    pend = jnp.cumsum(pcounts)
    pstart = pend - pcounts
    e = info[:, 0:TOP_K].astype(jnp.int32)
    rank = info[:, 4:4 + TOP_K].astype(jnp.int32)
    experts = jnp.arange(N_EXPERTS, dtype=jnp.int32)
    dest = (jnp.sum(jnp.where(e[..., None] == experts, pstart, 0), axis=-1) + rank).reshape(-1)
    nb = -(-(N * TOP_K) // tb) + N_EXPERTS
    P = nb * tb
    blocks = jnp.arange(nb, dtype=jnp.int32)
    blk_e = jnp.minimum(jnp.sum((pend[None, :] <= blocks[:, None] * tb).astype(jnp.int32), axis=1), N_EXPERTS - 1)
    nb_used = (pend[-1] // tb).astype(jnp.int32).reshape(1)
    zero_blocks = jnp.concatenate([jnp.where(counts % tb != 0, pend // tb - 1, -1),
                                   jnp.where(blocks >= nb_used[0], blocks, -1)]).astype(jnp.int32)
    xs = _dispatch(dest, zero_blocks, x2d, tm=tm, tb=tb, n_rows=P)
    yb = _ffn(blk_e, nb_used, xs, ln2_g, wg, wu, wd, tb=tb)
    return _combine(dest, x2d, info, yb, tm=tm)


def _rows8(x):
    return jnp.broadcast_to(x, (8, x.shape[1]))


def _row_hdot(x, m):
    return _hdot(_rows8(x), m)[0:1]


def _bf_round(x):
    return x.astype(BF16).astype(F32)


def _sample_attn_kernel(z_ref, c0, c1, c2, qg_ref, kg_ref, oa_ref, kv_ref):
    W = SWA_GW
    scale = SWA_DIM ** -0.5
    z = z_ref[0]
    sub = lax.broadcasted_iota(jnp.int32, (SWA_HEADS, W), 0)
    lane = lax.broadcasted_iota(jnp.int32, (SWA_HEADS, W), 1)
    own = lane // SWA_DIM == sub

    def heads(row):
        return jnp.where(own, jnp.broadcast_to(row, (SWA_HEADS, W)), 0.0)

    def head_sum(row):
        return jnp.sum(heads(row), axis=-1, keepdims=True)

    def spread(col):
        return jnp.sum(jnp.where(own, col, 0.0), axis=0, keepdims=True)

    def headnorm(zz, g):
        return zz * spread(lax.rsqrt(head_sum(zz * zz) * (1.0 / SWA_DIM) + EPS)) * g

    outs, lses = [], []
    for gi, (c_ref, (win, dil)) in enumerate(zip((c0, c1, c2), SWA_CONFIGS)):
        q = headnorm(z[:, gi * W:(gi + 1) * W], qg_ref[gi:gi + 1, :])
        k = headnorm(z[:, 3 * W + gi * W:3 * W + (gi + 1) * W], kg_ref[gi:gi + 1, :])
        v = z[:, 6 * W + gi * W:6 * W + (gi + 1) * W]
        kv_ref[0, :, 2 * gi * W:(2 * gi + 1) * W] = k
        kv_ref[0, :, (2 * gi + 1) * W:(2 * gi + 2) * W] = v
        kc = c_ref[0].reshape(W, win).astype(BF16)
        vc = c_ref[1].reshape(W, win).astype(BF16)
        s_c = jnp.dot(heads(q).astype(BF16), kc, preferred_element_type=F32) * scale
        row = lax.broadcasted_iota(jnp.int32, s_c.shape, 1)
        s_c = jnp.where(row % dil == 0, s_c, -jnp.inf)
        s_n = head_sum(_bf_round(k) * _bf_round(q)) * scale
        m = jnp.maximum(jnp.max(s_c, axis=-1, keepdims=True), s_n)
        p_c = jnp.exp(s_c - m)
        p_n = jnp.exp(s_n - m)
        den = jnp.sum(p_c, axis=-1, keepdims=True) + p_n
        pv = lax.dot_general(p_c.astype(BF16), vc, (((1,), (1,)), ((), ())), preferred_element_type=F32)
        num = jnp.sum(jnp.where(own, pv, 0.0), axis=0, keepdims=True) + spread(_bf_round(p_n)) * _bf_round(v)
        outs.append(num / spread(den))
        lses.append(m + jnp.log(den))
    mm = jnp.maximum(jnp.maximum(lses[0], lses[1]), lses[2])
    es = [jnp.exp(l - mm) for l in lses]
    tot = es[0] + es[1] + es[2]
    oa_ref[0] = sum(spread(_bf_round(e / tot)) * _bf_round(o) for e, o in zip(es, outs))


def _sample_attn(z3, caches, layer, qg, kg):
    Bs = z3.shape[0]
    W = SWA_GW
    cviews, cspecs = [], []
    for (win, dil), c in zip(SWA_CONFIGS, caches):
        assert c.shape[2] == win
        cviews.append(jnp.transpose(c, (0, 1, 3, 4, 5, 2)))
        cspecs.append(pl.BlockSpec((None, None, 2, SWA_HEADS, SWA_DIM, win), lambda b: (layer, b, 0, 0, 0, 0)))
    full = lambda a: pl.BlockSpec(a.shape, lambda b: (0,) * a.ndim)
    return pl.pallas_call(
        _sample_attn_kernel,
        grid=(Bs,),
        in_specs=[pl.BlockSpec((1, 1, 9 * W), lambda b: (b, 0, 0))] + cspecs + [full(qg), full(kg)],
        out_specs=[pl.BlockSpec((1, 1, W), lambda b: (b, 0, 0)), pl.BlockSpec((1, 1, 6 * W), lambda b: (b, 0, 0))],
        out_shape=[jax.ShapeDtypeStruct((Bs, 1, W), F32), jax.ShapeDtypeStruct((Bs, 1, 6 * W), F32)],
        compiler_params=_cparams(1),
        name="sample_attn",
    )(z3, *cviews, qg, kg)


def _sample_dn_kernel(raw_ref, cs_ref, cw_ref, ba_ref, par_ref, s_ref, e_ref, etb_ref, etg_ref, o_ref, so_ref):
    E, ETB, ETG = e_ref[...], etb_ref[...], etg_ref[...]
    width = DN_HEADS * DN_DK
    conv = cw_ref[DN_CONV - 1:DN_CONV, :] * raw_ref[0]
    for t in range(DN_CONV - 1):
        conv = conv + cw_ref[t:t + 1, :] * cs_ref[0, t:t + 1, :]
    act = _silu(conv)

    def l2(zz):
        return zz * _row_hdot(lax.rsqrt(_row_hdot(zz * zz, E) + EPS), ETB)

    qn = l2(act[:, 0:width]) * (DN_DK ** -0.5)
    kn = l2(act[:, width:2 * width])
    vn = act[:, 2 * width:3 * width]
    ba = ba_ref[0]
    beta = _row_hdot(_sigmoid(ba), ETB)
    eg = jnp.exp(_row_hdot(par_ref[0:1, :] * _softplus(ba + par_ref[1:2, :]), ETG))
    row0 = lax.broadcasted_iota(jnp.int32, (8, LANES), 0) == 0
    for h in range(DN_HEADS):
        sl = slice(h * LANES, (h + 1) * LANES)
        S = s_ref[0, h]
        q, k, v, b, e = qn[:, sl], kn[:, sl], vn[:, sl], beta[:, sl], eg[:, sl]
        Sr = _bf_round(S)
        v_new = v * b - _row_hdot(_bf_round(k * b * e), Sr)
        a = jnp.sum(q * k, axis=-1, keepdims=True)
        o_ref[0, :, sl] = _row_hdot(_bf_round(q * e), Sr) + a * v_new
        k8 = jnp.where(row0, _rows8(k), 0.0)
        upd = lax.dot_general(k8, _rows8(v_new), (((0,), (0,)), ((), ())), preferred_element_type=F32, precision=HI)
        so_ref[0, h] = S * e + upd


def _sample_dn(raw3, conv_state, s0, layer, conv_w, ba3, par, e_mat, etb, etg):
    Bs, _, C = raw3.shape
    H = DN_HEADS
    full = lambda a: pl.BlockSpec(a.shape, lambda b: (0,) * a.ndim)
    return pl.pallas_call(
        _sample_dn_kernel,
        grid=(Bs,),
        in_specs=[pl.BlockSpec((1, 1, C), lambda b: (b, 0, 0)),
                  pl.BlockSpec((None, 1, DN_CONV - 1, C), lambda b: (layer, b, 0, 0)),
                  full(conv_w),
                  pl.BlockSpec((1, 1, LANES), lambda b: (b, 0, 0)),
                  full(par),
                  pl.BlockSpec((None, 1, H, DN_DK, LANES), lambda b: (layer, b, 0, 0, 0)),
                  full(e_mat), full(etb), full(etg)],
        out_specs=[pl.BlockSpec((1, 1, H * LANES), lambda b: (b, 0, 0)),
                   pl.BlockSpec((1, H, DN_DK, LANES), lambda b: (b, 0, 0, 0))],
        out_shape=[jax.ShapeDtypeStruct((Bs, 1, H * LANES), F32), jax.ShapeDtypeStruct(s0.shape[1:], F32)],
        compiler_params=_cparams(1),
        name="sample_dn",
    )(raw3, conv_state, conv_w, ba3, par, s0, e_mat, etb, etg)


def _sample_out_kernel(x_ref, oa_ref, od_ref, gates_ref, dng_ref, wa_ref, wb_ref, wo_ref, y_ref):
    y_ref[...] = _gated_mix(oa_ref[...], od_ref[...], gates_ref[...], dng_ref[...], wa_ref[...], wb_ref[...],
                            wo_ref[...], x_ref[...], _bdot)


def _sample_out(x2d, oa, od, gates, dng, wa, wb, wo):
    args = (x2d, oa, od, gates, dng, wa, wb, wo)
    return pl.pallas_call(
        _sample_out_kernel,
        grid=(1,),
        in_specs=[pl.BlockSpec(a.shape, lambda i: (0, 0)) for a in args],
        out_specs=pl.BlockSpec(x2d.shape, lambda i: (0, 0)),
        out_shape=jax.ShapeDtypeStruct(x2d.shape, F32),
        compiler_params=_cparams(1),
        name="sample_out",
    )(*args)


def _head_indicator(width, head):
    c = jnp.arange(width)[:, None] // head
    return (c == jnp.arange(LANES)[None, :]).astype(F32)


def _prep_layer(l, ln1_g, w_in, q_norm_g, k_norm_g, dn_conv_w, dn_a_log, dn_dt_bias, dn_norm_g, w_out_a, w_out_b,
                w_o, ln2_g, w_rg, b_rg, w_re, b_re, w_e_gate, w_e_up, w_e_down):
    D = w_in.shape[1]
    a_w = 3 * 3 * SWA_GW
    dn_w = DN_HEADS * 3 * DN_DK
    hv = DN_HEADS * DN_DK
    w = w_in[l]
    splits = dict(att=w[:, :a_w], dn=w[:, a_w:a_w + dn_w],
                  ba=jnp.pad(w[:, a_w + dn_w:a_w + dn_w + 2 * DN_HEADS], ((0, 0), (0, LANES - 2 * DN_HEADS))),
                  gate=w[:, a_w + dn_w + 2 * DN_HEADS:])
    assert splits["gate"].shape[1] == hv + 2 * D
    tile_heads = lambda g: jnp.broadcast_to(g[:, None, :], (len(SWA_CONFIGS), SWA_HEADS, SWA_DIM)).reshape(len(SWA_CONFIGS), SWA_GW)
    qg, kg = tile_heads(q_norm_g[l]), tile_heads(k_norm_g[l])
    idx = jnp.arange(MXU) // SWA_DIM
    par = jnp.zeros((2, LANES), F32)
    par = par.at[0, DN_HEADS:2 * DN_HEADS].set(-jnp.exp(dn_a_log[l].astype(F32)))
    par = par.at[1, DN_HEADS:2 * DN_HEADS].set(dn_dt_bias[l].astype(F32))
    wr = jnp.pad(jnp.concatenate([w_rg[l], w_re[l]], axis=1), ((0, 0), (0, LANES - N_GROUPS - N_EXPERTS)))
    br = jnp.pad(jnp.concatenate([b_rg[l], b_re[l]]), (0, LANES - N_GROUPS - N_EXPERTS)).reshape(1, LANES)
    e8 = _head_indicator(hv, DN_DK)
    return dict(
        bf16={k: v.astype(BF16) for k, v in splits.items()},
        ln1=ln1_g[l].reshape(1, D), ln2=ln2_g[l].reshape(1, D),
        ng=jnp.stack([qg.reshape(1, -1), kg.reshape(1, -1)]), qg=qg, kg=kg,
        bd=(idx[:, None] == idx[None, :]).astype(BF16),
        conv_w=dn_conv_w[l], par=par, dng=dn_norm_g[l].reshape(1, DN_DK),
        wa=w_out_a[l].astype(BF16), wb=w_out_b[l].astype(BF16), wo=w_o[l].astype(BF16), wr=wr.astype(BF16), br=br,
        wg=w_e_gate[l].astype(BF16), wu=w_e_up[l].astype(BF16), wd=w_e_down[l].astype(BF16),
        e_dn=e8, etb=e8.T, etg=jnp.roll(e8, DN_HEADS, axis=1).T,
    )


def _layer_prompt(x, p):
    B, L, D = x.shape
    N = B * L
    x2d = x.reshape(N, D)
    bw = p["bf16"]
    pk0, pk1, pk2, t0, t1, t2 = _proj_attn(x, p["ln1"], bw["att"], p["ng"], p["bd"], tm=min(512, L))
    tmp = min(1024, N)
    qd, kd, vd, raw_tail = _proj_dn(x, p["ln1"], bw["dn"], p["conv_w"], tm=min(512, L))
    gates = _proj_plain(x2d, p["ln1"], bw["gate"], tm=tmp, tn=1536, out_dtype=BF16, name="proj_gate")
    ba = _proj_plain(x2d, p["ln1"], bw["ba"], tm=tmp, tn=LANES, out_dtype=F32, name="proj_ba")
    os_, ls_ = [], []
    for pk in (pk0, pk1, pk2):
        d, M = pk.shape[1], pk.shape[2]
        o, lse = _attn(pk.reshape(B * d, M, pk.shape[3]), tq=min(256, M))
        os_.append(o.reshape(B, d, M, SWA_GW))
        ls_.append(lse.reshape(B, d, M, SWA_GW))
    gb, gt_rows = _dn_gates(ba.reshape(B, L, LANES), p["par"], tl=min(256, L))
    u, w, qdec, kdec, a, gt = _dn_intra(qd, kd, vd, gb, gt_rows, tl=min(2048, L))
    od, s_new = _dn_scan(u, w, qdec, kdec, a, gt, jnp.zeros((B, DN_HEADS, DN_DK, LANES), F32), tl=min(1024, L))
    x2 = _out_proj(x2d, os_, ls_, od.reshape(N, -1), gates, p["dng"], p["wa"], p["wb"], p["wo"], B=B, L=L,
                   tm=min(512, L))
    y = _moe(x2, p["ln2"], p["wr"], p["br"], p["wg"], p["wu"], p["wd"], tm=256)
    return y.reshape(B, L, D), [t0, t1, t2], raw_tail[:, 8 - (DN_CONV - 1):], s_new


def _layer_sample(x, caches, conv_state, s0, layer, p):
    Bs, T, D = x.shape
    assert T == 1
    x2d = x.reshape(Bs, D)
    bw = p["bf16"]
    proj = functools.partial(_proj_plain, x2d, p["ln1"], tm=Bs, out_dtype=F32)
    z_att = proj(bw["att"], tn=1536, name="sproj_att")
    raw = proj(bw["dn"], tn=1536, name="sproj_dn")
    gates = proj(bw["gate"], tn=1536, name="sproj_gate")
    ba = proj(bw["ba"], tn=LANES, name="sproj_ba")
    oa, kv = _sample_attn(z_att.reshape(Bs, 1, -1), caches, layer, p["qg"], p["kg"])
    raw3 = raw.reshape(Bs, 1, -1)
    od, s_new = _sample_dn(raw3, conv_state, s0, layer, p["conv_w"], ba.reshape(Bs, 1, LANES), p["par"],
                           p["e_dn"], p["etb"], p["etg"])
    x2 = _sample_out(x2d, oa.reshape(Bs, -1), od.reshape(Bs, -1), gates, p["dng"], p["wa"], p["wb"], p["wo"])
    y = _moe(x2, p["ln2"], p["wr"], p["br"], p["wg"], p["wu"], p["wd"], tm=Bs)
    W2 = 2 * SWA_GW
    kvs = [kv[:, :, g * W2:(g + 1) * W2].reshape(Bs, 1, 2, SWA_HEADS, SWA_DIM) for g in range(len(SWA_CONFIGS))]
    new_conv = jnp.concatenate([conv_state[layer][:, 1:], raw3], axis=1)
    return y.reshape(Bs, 1, D), kvs, new_conv, s_new


def kernel(x_prompt, x_sample, cache_swa0_kv, cache_swa1_kv, cache_swa2_kv, state_dn_conv, state_dn_S, ln1_g, w_in,
           q_norm_g, k_norm_g, dn_conv_w, dn_a_log, dn_dt_bias, dn_norm_g, w_out_a, w_out_b, w_o, ln2_g, w_rg, b_rg,
           w_re, b_re, w_e_gate, w_e_up, w_e_down):
    yp, ys = x_prompt, x_sample
    outs = [[] for _ in range(10)]
    for l in range(w_in.shape[0]):
        p = _prep_layer(l, ln1_g, w_in, q_norm_g, k_norm_g, dn_conv_w, dn_a_log, dn_dt_bias, dn_norm_g, w_out_a,
                        w_out_b, w_o, ln2_g, w_rg, b_rg, w_re, b_re, w_e_gate, w_e_up, w_e_down)
        yp, pkv, pconv, ps = _layer_prompt(yp, p)
        ys, skv, sconv, ss = _layer_sample(ys, (cache_swa0_kv, cache_swa1_kv, cache_swa2_kv), state_dn_conv,
                                           state_dn_S, l, p)
        for lst, val in zip(outs, (*pkv, pconv, ps, *skv, sconv, ss)):
            lst.append(val)
    return (yp, ys, *(jnp.stack(o) for o in outs))
```

```python
import functools

import jax
import jax.numpy as jnp
from jax import lax
from jax.experimental import pallas as pl
from jax.experimental.pallas import tpu as pltpu

F32 = jnp.float32
BF16 = jnp.bfloat16
HI = lax.Precision.HIGHEST
EPS = 1e-6

SWA_CONFIGS = ((128, 1), (512, 4), (2048, 16))
SWA_HEADS = 8
SWA_DIM = 64
SWA_GW = SWA_HEADS * SWA_DIM
SWA_SPAN = 128
DN_HEADS = 8
DN_DK = 128
DN_CONV = 4
DN_CHUNK = 64
N_GROUPS = 4
PER_GROUP = 8
N_EXPERTS = N_GROUPS * PER_GROUP
TOP_K = 2

VMEM_LIMIT_BYTES = 56 * 1024 * 1024
LANES = 128
MXU = 256
MOE_ROWS = 256
ROW_UNROLL = 4


def _cparams(n_axes):
    return pltpu.CompilerParams(
        dimension_semantics=("arbitrary",) * n_axes, vmem_limit_bytes=VMEM_LIMIT_BYTES
    )


def _rms(x, g):
    return x * lax.rsqrt(jnp.mean(x * x, axis=-1, keepdims=True) + EPS) * g


def _bdot(a, b):
    return jnp.dot(a.astype(BF16), b.astype(BF16), preferred_element_type=F32)


def _hdot(a, b):
    return jnp.dot(a, b, preferred_element_type=F32, precision=HI)


def _sigmoid(x):
    return 1.0 / (1.0 + jnp.exp(-x))


def _silu(x):
    return x * _sigmoid(x)


def _softplus(x):
    return jnp.maximum(x, 0.0) + jnp.log1p(jnp.exp(-jnp.abs(x)))


def _proj_attn_kernel(x_ref, lng_ref, w_ref, ng_ref, bd_ref, p0, p1, p2, t0, t1, t2, h_scr, z_scr,
                      *, tm, dils, tail_rows, tail_first):
    j = pl.program_id(2)

    @pl.when(j == 0)
    def _():
        h_scr[...] = _rms(x_ref[0], lng_ref[...]).astype(BF16)

    outs = (p0, p1, p2)
    tails = (t0, t1, t2)
    per_g = SWA_GW // LANES
    for sec in range(3):

        @pl.when(j == sec)
        def _(sec=sec):
            tail_parts = {}
            for c2 in range(3 * SWA_GW // MXU):
                zc = jnp.dot(h_scr[...], w_ref[:, c2 * MXU:(c2 + 1) * MXU], preferred_element_type=F32)
                if sec < 2:
                    ms = jnp.dot((zc * zc).astype(BF16), bd_ref[...], preferred_element_type=F32)
                    zc = zc * lax.rsqrt(ms + EPS) * ng_ref[0, :, c2 * MXU:(c2 + 1) * MXU]
                gi, half = divmod(c2 * MXU, SWA_GW)
                if dils[gi] == 1:
                    col = sec * SWA_GW + half
                    outs[gi][0, 0, :, col:col + MXU] = zc.astype(BF16)
                    tail_parts.setdefault(gi, []).append(zc[tm - tail_rows[gi]:tm, :])
                else:
                    z_scr[2 * c2] = zc[:, :LANES]
                    z_scr[2 * c2 + 1] = zc[:, LANES:]
            for gi, d in enumerate(dils):
                if d > 1:
                    for cb in range(per_g):
                        c = gi * per_g + cb
                        col = sec * SWA_GW + cb * LANES
                        for r in range(d):
                            outs[gi][0, r, :, col:col + LANES] = z_scr[c, pl.ds(r, tm // d, stride=d), :].astype(BF16)
                if sec >= 1:
                    @pl.when(pl.program_id(1) >= tail_first[gi])
                    def _(gi=gi, d=d):
                        rows = tail_rows[gi]
                        if d == 1:
                            pieces = tail_parts[gi]
                        else:
                            pieces = [z_scr[gi * per_g + cb, tm - rows:tm, :] for cb in range(per_g)]
                        zr = jnp.concatenate(pieces, axis=-1)
                        tails[gi][0, :, sec - 1] = zr.reshape(rows, SWA_HEADS, SWA_DIM)


def _proj_attn(x, ln_g, w_att, ng, bd, *, tm):
    B, L, D = x.shape
    nt = L // tm
    dils = tuple(d for _, d in SWA_CONFIGS)
    keeps = tuple(min(w, L) for w, _ in SWA_CONFIGS)
    tail_rows = tuple(min(k, tm) for k in keeps)
    for k, r in zip(keeps, tail_rows):
        assert k % r == 0 and L % tm == 0
    W3 = 3 * SWA_GW

    tail_first = tuple(nt - k // r for k, r in zip(keeps, tail_rows))

    def tail_spec(rows, first):
        return pl.BlockSpec((1, rows, 2, SWA_HEADS, SWA_DIM),
                            lambda b, i, j: (b, jnp.maximum(i - first, 0), 0, 0, 0))

    out_shape = [jax.ShapeDtypeStruct((B, d, L // d, W3), BF16) for d in dils]
    out_shape += [jax.ShapeDtypeStruct((B, k, 2, SWA_HEADS, SWA_DIM), F32) for k in keeps]
    out_specs = [pl.BlockSpec((1, d, tm // d, W3), lambda b, i, j: (b, 0, i, 0)) for d in dils]
    out_specs += [tail_spec(r, f) for r, f in zip(tail_rows, tail_first)]
    return pl.pallas_call(
        functools.partial(_proj_attn_kernel, tm=tm, dils=dils, tail_rows=tail_rows, tail_first=tail_first),
        grid=(B, nt, 3),
        in_specs=[
            pl.BlockSpec((1, tm, D), lambda b, i, j: (b, i, 0)),
            pl.BlockSpec((1, D), lambda b, i, j: (0, 0)),
            pl.BlockSpec((D, W3), lambda b, i, j: (0, j)),
            pl.BlockSpec((1, 1, W3), lambda b, i, j: (jnp.minimum(j, 1), 0, 0)),
            pl.BlockSpec((MXU, MXU), lambda b, i, j: (0, 0)),
        ],
        out_specs=out_specs,
        out_shape=out_shape,
        scratch_shapes=[pltpu.VMEM((tm, D), BF16), pltpu.VMEM((W3 // LANES, tm, LANES), F32)],
        compiler_params=_cparams(3),
        name="proj_attn",
    )(x, ln_g, w_att, ng, bd)


def _proj_plain_kernel(x_ref, lng_ref, w_ref, o_ref, h_scr):
    @pl.when(pl.program_id(1) == 0)
    def _():
        h_scr[...] = _rms(x_ref[...], lng_ref[...]).astype(BF16)

    o_ref[...] = jnp.dot(h_scr[...], w_ref[...], preferred_element_type=F32).astype(o_ref.dtype)


def _proj_plain(x2d, ln_g, w, *, tm, tn, out_dtype, name="proj_plain"):
    N, D = x2d.shape
    C = w.shape[1]
    assert N % tm == 0 and C % tn == 0
    return pl.pallas_call(
        _proj_plain_kernel,
        grid=(N // tm, C // tn),
        in_specs=[
            pl.BlockSpec((tm, D), lambda i, j: (i, 0)),
            pl.BlockSpec((1, D), lambda i, j: (0, 0)),
            pl.BlockSpec((D, tn), lambda i, j: (0, j)),
        ],
        out_specs=pl.BlockSpec((tm, tn), lambda i, j: (i, j)),
        out_shape=jax.ShapeDtypeStruct((N, C), out_dtype),
        scratch_shapes=[pltpu.VMEM((tm, D), BF16)],
        compiler_params=_cparams(2),
        name=name,
    )(x2d, ln_g, w)


def _attn_kernel(q_ref, kc_ref, vc_ref, kp_ref, vp_ref, o_ref, lse_ref, kk_scr, vv_scr, *, tq):
    i = pl.program_id(1)
    blk = SWA_SPAN
    kk_scr[0:blk, :] = kp_ref[0]
    kk_scr[blk:blk + tq, :] = kc_ref[0]
    vv_scr[0:blk, :] = vp_ref[0]
    vv_scr[blk:blk + tq, :] = vc_ref[0]
    qi = lax.broadcasted_iota(jnp.int32, (blk, 2 * blk), 0)
    ki = lax.broadcasted_iota(jnp.int32, (blk, 2 * blk), 1)
    dist = blk + qi - ki
    band = (dist >= 0) & (dist <= SWA_SPAN)
    band_first = band & ((ki >= blk) | (i > 0))
    lo = lax.broadcasted_iota(jnp.int32, (blk, LANES), 1) < SWA_DIM
    zero = jnp.zeros((blk, LANES), BF16)
    lane = lax.broadcasted_iota(jnp.int32, (blk, LANES), 1)
    for jb in range(tq // blk):
        mask = band_first if jb == 0 else band
        rows = slice(jb * blk, (jb + 1) * blk)
        lse_all = jnp.zeros((blk, LANES), F32)
        for hp in range(SWA_GW // LANES):
            cs = slice(hp * LANES, (hp + 1) * LANES)
            qb = q_ref[0, rows, cs]
            kk = kk_scr[jb * blk:(jb + 2) * blk, cs]
            vv = vv_scr[jb * blk:(jb + 2) * blk, cs]
            res_o = []
            for hh in range(2):
                qm = jnp.where(lo if hh == 0 else jnp.logical_not(lo), qb, zero)
                s = lax.dot_general(qm, kk, (((1,), (1,)), ((), ())), preferred_element_type=F32)
                s = jnp.where(mask, s * (SWA_DIM ** -0.5), -jnp.inf)
                m = jnp.max(s, axis=-1, keepdims=True)
                p = jnp.exp(s - m)
                den = jnp.sum(p, axis=-1, keepdims=True)
                pv = jnp.dot(p.astype(BF16), vv, preferred_element_type=F32)
                res_o.append(pv / den)
                lse_all = jnp.where(lane == 2 * hp + hh, m + jnp.log(den), lse_all)
            o_ref[0, rows, cs] = jnp.where(lo, res_o[0], res_o[1]).astype(BF16)
        lse_ref[0, rows, :] = lse_all


def _attn(p, *, tq):
    S, M, _ = p.shape
    assert M % tq == 0 and tq % SWA_SPAN == 0
    nb = tq // SWA_SPAN
    return pl.pallas_call(
        functools.partial(_attn_kernel, tq=tq),
        grid=(S, M // tq),
        in_specs=[
            pl.BlockSpec((1, tq, SWA_GW), lambda s, i: (s, i, 0)),
            pl.BlockSpec((1, tq, SWA_GW), lambda s, i: (s, i, 1)),
            pl.BlockSpec((1, tq, SWA_GW), lambda s, i: (s, i, 2)),
            pl.BlockSpec((1, SWA_SPAN, SWA_GW), lambda s, i: (s, jnp.maximum(i * nb - 1, 0), 1)),
            pl.BlockSpec((1, SWA_SPAN, SWA_GW), lambda s, i: (s, jnp.maximum(i * nb - 1, 0), 2)),
        ],
        out_specs=[
            pl.BlockSpec((1, tq, SWA_GW), lambda s, i: (s, i, 0)),
            pl.BlockSpec((1, tq, LANES), lambda s, i: (s, i, 0)),
        ],
        out_shape=[
            jax.ShapeDtypeStruct((S, M, SWA_GW), BF16),
            jax.ShapeDtypeStruct((S, M, LANES), F32),
        ],
        scratch_shapes=[
            pltpu.VMEM((SWA_SPAN + tq, SWA_GW), BF16),
            pltpu.VMEM((SWA_SPAN + tq, SWA_GW), BF16),
        ],
        compiler_params=_cparams(2),
        name="swa_attn",
    )(p, p, p, p, p)


def _proj_dn_kernel(x_ref, lng_ref, w_ref, cw_ref, q_ref, k_ref, v_ref, tail_ref, h_scr, z_scr, carry_scr,
                    *, tm, n_ct):
    i = pl.program_id(1)
    j = pl.program_id(2)
    nh = DN_HEADS
    ncb = z_scr.shape[1] // LANES

    @pl.when(j == 0)
    def _():
        h_scr[...] = _rms(x_ref[0], lng_ref[...]).astype(BF16)

    z_scr[0:8, :] = jnp.where(i == 0, 0.0, carry_scr[j])
    z_scr[8:8 + tm, :] = jnp.dot(h_scr[...], w_ref[...], preferred_element_type=F32)
    last = z_scr[tm:tm + 8, :]
    carry_scr[j] = last
    tn = z_scr.shape[1]
    outs = (q_ref, k_ref, v_ref)
    for jj in range(n_ct):

        @pl.when(j == jj)
        def _(jj=jj):
            tail_ref[0, :, jj * tn:(jj + 1) * tn] = last
            for cbl in range(ncb):
                cs = slice(cbl * LANES, (cbl + 1) * LANES)
                part, h = divmod(jj * ncb + cbl, nh)
                xe = z_scr[:, cs]
                acc = cw_ref[0:1, cs] * xe
                for t in range(1, DN_CONV):
                    acc = cw_ref[t:t + 1, cs] * xe + pltpu.roll(acc, 1, axis=0)
                act = _silu(acc[8:])
                if part < 2:
                    act = act * lax.rsqrt(jnp.sum(act * act, axis=-1, keepdims=True) + EPS)
                if part == 0:
                    act = act * (DN_DK ** -0.5)
                outs[part][0, :, h * LANES:(h + 1) * LANES] = act.astype(BF16)


def _proj_dn(x, ln_g, w_dn, conv_w, *, tm):
    B, L, D = x.shape
    C = w_dn.shape[1]
    width = DN_HEADS * DN_DK
    n_ct = 2
    tn = C // n_ct
    assert L % tm == 0 and C == 3 * width and tn % LANES == 0
    qkv = pl.BlockSpec((1, tm, width), lambda b, i, j: (b, i, 0))
    return pl.pallas_call(
        functools.partial(_proj_dn_kernel, tm=tm, n_ct=n_ct),
        grid=(B, L // tm, n_ct),
        in_specs=[
            pl.BlockSpec((1, tm, D), lambda b, i, j: (b, i, 0)),
            pl.BlockSpec((1, D), lambda b, i, j: (0, 0)),
            pl.BlockSpec((D, tn), lambda b, i, j: (0, j)),
            pl.BlockSpec((DN_CONV, tn), lambda b, i, j: (0, j)),
        ],
        out_specs=[qkv, qkv, qkv, pl.BlockSpec((1, 8, C), lambda b, i, j: (b, 0, 0))],
        out_shape=[jax.ShapeDtypeStruct((B, L, width), BF16)] * 3 + [jax.ShapeDtypeStruct((B, 8, C), F32)],
        scratch_shapes=[pltpu.VMEM((tm, D), BF16), pltpu.VMEM((8 + tm, tn), F32), pltpu.VMEM((n_ct, 8, tn), F32)],
        compiler_params=_cparams(3),
        name="proj_dn",
    )(x, ln_g, w_dn, conv_w)


def _gates_kernel(ba_ref, par_ref, g_ref, gt_ref, *, tl):
    nh = DN_HEADS
    ba = ba_ref[0]
    lane = lax.broadcasted_iota(jnp.int32, (tl, LANES), 1)
    g = par_ref[0:1, :] * _softplus(ba + par_ref[1:2, :])
    ri = lax.broadcasted_iota(jnp.int32, (tl, tl), 0)
    ci = lax.broadcasted_iota(jnp.int32, (tl, tl), 1)
    tri = jnp.where((ri // DN_CHUNK == ci // DN_CHUNK) & (ci <= ri), 1.0, 0.0).astype(BF16)
    g_hi = g.astype(BF16)
    r1 = g - g_hi.astype(F32)
    g_mid = r1.astype(BF16)
    g_lo = (r1 - g_mid.astype(F32)).astype(BF16)
    gc = sum(jnp.dot(tri, piece, preferred_element_type=F32) for piece in (g_hi, g_mid, g_lo))
    g_ref[0] = jnp.where(lane < nh, _sigmoid(ba), gc)
    gt_ref[0] = jnp.transpose(gc)[nh:2 * nh, :]


def _dn_gates(ba, par, *, tl):
    B, L, _ = ba.shape
    assert L % tl == 0 and tl % DN_CHUNK == 0
    return pl.pallas_call(
        functools.partial(_gates_kernel, tl=tl),
        grid=(B, L // tl),
        in_specs=[pl.BlockSpec((1, tl, LANES), lambda b, i: (b, i, 0)), pl.BlockSpec((2, LANES), lambda b, i: (0, 0))],
        out_specs=[pl.BlockSpec((1, tl, LANES), lambda b, i: (b, i, 0)),
                   pl.BlockSpec((1, DN_HEADS, tl), lambda b, i: (b, 0, i))],
        out_shape=[jax.ShapeDtypeStruct((B, L, LANES), F32), jax.ShapeDtypeStruct((B, DN_HEADS, L), F32)],
        compiler_params=_cparams(2),
        name="dn_gates",
    )(ba, par)


def _intra_kernel(q_ref, k_ref, v_ref, g_ref, gt_in_ref, u_ref, w_ref, qd_ref, kd_ref, a_ref, gt_ref, *, tl):
    h = pl.program_id(1)
    C = DN_CHUNK
    lane = lax.broadcasted_iota(jnp.int32, (C, LANES), 1)
    ri = lax.broadcasted_iota(jnp.int32, (C, C), 0)
    ci = lax.broadcasted_iota(jnp.int32, (C, C), 1)
    eye = jnp.where(ri == ci, 1.0, 0.0).astype(F32)
    nt_dot = lambda a, b: lax.dot_general(a.astype(BF16), b.astype(BF16), (((1,), (1,)), ((), ())),
                                          preferred_element_type=F32)
    rows = [slice(c * C, (c + 1) * C) for c in range(tl // C)]
    gv = [g_ref[0, r, :] for r in rows]
    q = [q_ref[0, r, :].astype(F32) for r in rows]
    k = [k_ref[0, r, :].astype(F32) for r in rows]
    v = [v_ref[0, r, :].astype(F32) for r in rows]
    beta = [jnp.sum(jnp.where(lane == h, x, 0.0), axis=-1, keepdims=True) for x in gv]
    gc = [jnp.sum(jnp.where(lane == h + DN_HEADS, x, 0.0), axis=-1, keepdims=True) for x in gv]
    gc_row = gt_in_ref[0, pl.ds(h, 1), :]
    decay = [jnp.exp(jnp.where(ri >= ci, a - gc_row[:, r], -jnp.inf)) for a, r in zip(gc, rows)]
    kb = [a * b for a, b in zip(k, beta)]
    x = [-jnp.where(ri > ci, nt_dot(a, b) * d, 0.0) for a, b, d in zip(kb, k, decay)]
    t = [eye + a for a in x]
    for _ in range(5):
        x = [_bdot(a, a) for a in x]
        t = [a + _bdot(a, b) for a, b in zip(t, x)]
    eg = [jnp.exp(a) for a in gc]
    glast = [a[C - 1:C, :] for a in gc]
    u = [_bdot(a, b * c) for a, b, c in zip(t, v, beta)]
    w = [_bdot(a, b * c) for a, b, c in zip(t, kb, eg)]
    qk = [nt_dot(a, b) for a, b in zip(q, k)]
    for c, r in enumerate(rows):
        u_ref[0, 0, r, :] = u[c]
        w_ref[0, 0, r, :] = w[c].astype(BF16)
        a_ref[0, 0, r, :] = (qk[c] * decay[c]).astype(BF16)
        qd_ref[0, 0, r, :] = (q[c] * eg[c]).astype(BF16)
        kd_ref[0, 0, r, :] = (k[c] * jnp.exp(glast[c] - gc[c])).astype(BF16)
        gt_ref[0, 0, c:c + 1, :] = jnp.broadcast_to(jnp.exp(glast[c]), (1, LANES))


def _dn_intra(q, k, v, g, gt_rows, *, tl):
    B, L, _ = q.shape
    H, C = DN_HEADS, DN_CHUNK
    assert L % tl == 0 and (tl // C) % 8 == 0
    qkv_spec = pl.BlockSpec((1, tl, LANES), lambda b, h, i: (b, i, h))
    hl = lambda w: pl.BlockSpec((1, 1, tl, w), lambda b, h, i: (b, h, i, 0))
    return pl.pallas_call(
        functools.partial(_intra_kernel, tl=tl),
        grid=(B, H, L // tl),
        in_specs=[qkv_spec, qkv_spec, qkv_spec, pl.BlockSpec((1, tl, LANES), lambda b, h, i: (b, i, 0)),
                  pl.BlockSpec((1, H, tl), lambda b, h, i: (b, 0, i))],
        out_specs=[hl(LANES), hl(LANES), hl(LANES), hl(LANES), hl(C),
                   pl.BlockSpec((1, 1, tl // C, LANES), lambda b, h, i: (b, h, i, 0))],
        out_shape=[
            jax.ShapeDtypeStruct((B, H, L, LANES), F32),
            jax.ShapeDtypeStruct((B, H, L, LANES), BF16),
            jax.ShapeDtypeStruct((B, H, L, LANES), BF16),
            jax.ShapeDtypeStruct((B, H, L, LANES), BF16),
            jax.ShapeDtypeStruct((B, H, L, C), BF16),
            jax.ShapeDtypeStruct((B, H, L // C, LANES), F32),
        ],
        compiler_params=_cparams(3),
        name="dn_intra",
    )(q, k, v, g, gt_rows)


def _scan_kernel(u_ref, w_ref, qd_ref, kd_ref, a_ref, gt_ref, s0_ref, o_ref, s_ref, *, n_chunks):
    C = DN_CHUNK
    H = s_ref.shape[1]

    @pl.when(pl.program_id(1) == 0)
    def _():
        s_ref[...] = s0_ref[...]

    def body(c, carry):
        rows = pl.ds(pl.multiple_of(c * C, C), C)
        S = [s_ref[0, h] for h in range(H)]
        Sb = [x.astype(BF16) for x in S]
        v_new = [u_ref[0, h, rows, :] - jnp.dot(w_ref[0, h, rows, :], Sb[h], preferred_element_type=F32)
                 for h in range(H)]
        vb = [x.astype(BF16) for x in v_new]
        o = [jnp.dot(qd_ref[0, h, rows, :], Sb[h], preferred_element_type=F32)
             + jnp.dot(a_ref[0, h, rows, :], vb[h], preferred_element_type=F32) for h in range(H)]
        upd = [lax.dot_general(kd_ref[0, h, rows, :], vb[h], (((0,), (0,)), ((), ())), preferred_element_type=F32)
               for h in range(H)]
        for h in range(H):
            o_ref[0, rows, h * LANES:(h + 1) * LANES] = o[h]
            s_ref[0, h] = S[h] * gt_ref[0, h, pl.ds(c, 1), :] + upd[h]
        return carry

    lax.fori_loop(0, n_chunks, body, 0)


def _dn_scan(u, w, qd, kd, a, gt, s0, *, tl):
    B, H, L, _ = u.shape
    C = DN_CHUNK
    assert L % tl == 0 and (tl // C) % 8 == 0
    hs = lambda wd: pl.BlockSpec((1, H, tl, wd), lambda b, i: (b, 0, i, 0))
    s_spec = pl.BlockSpec((1, H, DN_DK, LANES), lambda b, i: (b, 0, 0, 0))
    return pl.pallas_call(
        functools.partial(_scan_kernel, n_chunks=tl // C),
        grid=(B, L // tl),
        in_specs=[hs(LANES), hs(LANES), hs(LANES), hs(LANES), hs(C),
                  pl.BlockSpec((1, H, tl // C, LANES), lambda b, i: (b, 0, i, 0)), s_spec],
        out_specs=[pl.BlockSpec((1, tl, H * LANES), lambda b, i: (b, i, 0)), s_spec],
        out_shape=[jax.ShapeDtypeStruct((B, L, H * LANES), F32),
                   jax.ShapeDtypeStruct((B, H, DN_DK, LANES), F32)],
        compiler_params=_cparams(2),
        name="dn_scan",
    )(u, w, qd, kd, a, gt, s0)


def _gated_mix(o_a, od, gates, dng, wa, wb, wo, x, dot):
    width = DN_HEADS * DN_DK
    parts = []
    for h in range(DN_HEADS):
        blk = od[:, h * LANES:(h + 1) * LANES]
        parts.append(blk * lax.rsqrt(jnp.mean(blk * blk, axis=-1, keepdims=True) + EPS) * dng)
    odn = jnp.concatenate(parts, axis=-1) * _silu(gates[:, 0:width].astype(F32))
    ya = dot(o_a, wa)
    yb = dot(odn, wb)
    mix = _sigmoid(gates[:, width:2 * width].astype(F32)) * ya + _sigmoid(gates[:, 2 * width:].astype(F32)) * yb
    return x + dot(mix, wo)


def _out_kernel(x_ref, o0, o1, o2, l0, l1, l2, od_ref, gates_ref, dng_ref, wa_ref, wb_ref, wo_ref, e_ref, y_ref,
                so0, so1, so2, sl0, sl1, sl2, *, tm, dils):
    o_refs, l_refs = (o0, o1, o2), (l0, l1, l2)
    so, sl = (so0, so1, so2), (sl0, sl1, sl2)
    for gi, d in enumerate(dils):
        for r in range(d):
            dst = slice(None) if d == 1 else pl.ds(r, tm // d, stride=d)
            sl[gi][dst, :] = l_refs[gi][0, r]
            for cb in range(SWA_GW // LANES):
                so[gi][cb, dst, :] = o_refs[gi][0, r, :, cb * LANES:(cb + 1) * LANES].astype(F32)
    ls = [s[...] for s in sl]
    m = jnp.maximum(jnp.maximum(ls[0], ls[1]), ls[2])
    es = [jnp.exp(l - m) for l in ls]
    tot = es[0] + es[1] + es[2]
    alphas = [jnp.dot((e / tot).astype(BF16), e_ref[...], preferred_element_type=F32) for e in es]
    parts = []
    for cb in range(SWA_GW // LANES):
        cs = slice(cb * LANES, (cb + 1) * LANES)
        parts.append(alphas[0][:, cs] * so[0][cb] + alphas[1][:, cs] * so[1][cb] + alphas[2][:, cs] * so[2][cb])
    o_a = jnp.concatenate(parts, axis=-1)
    y_ref[...] = _gated_mix(o_a, od_ref[...], gates_ref[...], dng_ref[...], wa_ref[...], wb_ref[...],
                            wo_ref[...], x_ref[...], _bdot)


def _out_proj(x2d, os_, ls_, od2d, gates, dng, wa, wb, wo, e_att, *, B, L, tm):
    N, D = x2d.shape
    nt = L // tm
    dils = tuple(d for _, d in SWA_CONFIGS)
    grp = lambda d, w: pl.BlockSpec((1, d, tm // d, w), lambda i: (i // nt, 0, i % nt, 0))
    row = lambda w: pl.BlockSpec((tm, w), lambda i: (i, 0))
    full = lambda a: pl.BlockSpec(a.shape, lambda i: (0, 0))
    return pl.pallas_call(
        functools.partial(_out_kernel, tm=tm, dils=dils),
        grid=(N // tm,),
        in_specs=[row(D)] + [grp(d, SWA_GW) for d in dils] + [grp(d, LANES) for d in dils]
        + [row(od2d.shape[1]), row(gates.shape[1]), full(dng), full(wa), full(wb), full(wo), full(e_att)],
        out_specs=row(D),
        out_shape=jax.ShapeDtypeStruct((N, D), F32),
        scratch_shapes=[pltpu.VMEM((SWA_GW // LANES, tm, LANES), F32)] * 3 + [pltpu.VMEM((tm, LANES), F32)] * 3,
        compiler_params=_cparams(1),
        name="out_proj",
    )(x2d, *os_, *ls_, od2d, gates, dng, wa, wb, wo, e_att)


def _router_kernel(x_ref, lng_ref, wr_ref, br_ref, info_ref, cnt_ref, base_scr, *, tm):
    i = pl.program_id(0)

    @pl.when(i == 0)
    def _():
        base_scr[...] = jnp.zeros_like(base_scr)

    h = _rms(x_ref[...], lng_ref[...])
    lg = _bdot(h, wr_ref[...]) + br_ref[...]
    lane = lax.broadcasted_iota(jnp.int32, (tm, LANES), 1)
    big = jnp.int32(1 << 20)
    ninf = -jnp.inf

    def argmax_lane(vals):
        mx = jnp.max(vals, axis=-1, keepdims=True)
        idx = jnp.min(jnp.where(vals == mx, lane, big), axis=-1, keepdims=True)
        return mx, idx

    lgm = jnp.where(lane < N_GROUPS, lg, ninf)
    mg, gsel = argmax_lane(lgm)
    pg = 1.0 / jnp.sum(jnp.exp(lgm - mg), axis=-1, keepdims=True)
    start = N_GROUPS + gsel * PER_GROUP
    le = jnp.where((lane >= start) & (lane < start + PER_GROUP), lg, ninf)
    m1, i1 = argmax_lane(le)
    m2, i2 = argmax_lane(jnp.where(lane == i1, ninf, le))
    e21 = jnp.exp(m2 - m1)
    w1 = pg / (1.0 + e21)
    w2 = pg * e21 / (1.0 + e21)
    oh = jnp.where(lane == i1, 1.0, 0.0) + jnp.where(lane == i2, 1.0, 0.0)
    ri = lax.broadcasted_iota(jnp.int32, (tm, tm), 0)
    ci = lax.broadcasted_iota(jnp.int32, (tm, tm), 1)
    strict = jnp.where(ci < ri, 1.0, 0.0).astype(BF16)
    pref = jnp.dot(strict, oh.astype(BF16), preferred_element_type=F32) + base_scr[...]
    r1 = jnp.sum(jnp.where(lane == i1, pref, 0.0), axis=-1, keepdims=True)
    r2 = jnp.sum(jnp.where(lane == i2, pref, 0.0), axis=-1, keepdims=True)
    base_scr[...] = base_scr[...] + jnp.sum(oh, axis=0, keepdims=True)
    cnt_ref[...] = base_scr[...]
    off = jnp.float32(N_GROUPS)
    info = jnp.where(lane == 0, i1.astype(F32) - off, 0.0)
    info = jnp.where(lane == 1, i2.astype(F32) - off, info)
    info = jnp.where(lane == 2, w1, info)
    info = jnp.where(lane == 3, w2, info)
    info = jnp.where(lane == 4, r1, info)
    info = jnp.where(lane == 5, r2, info)
    info_ref[...] = info


def _router(x2d, ln_g, wr, br, *, tm):
    N, D = x2d.shape
    assert N % tm == 0
    return pl.pallas_call(
        functools.partial(_router_kernel, tm=tm),
        grid=(N // tm,),
        in_specs=[
            pl.BlockSpec((tm, D), lambda i: (i, 0)),
            pl.BlockSpec((1, D), lambda i: (0, 0)),
            pl.BlockSpec((D, LANES), lambda i: (0, 0)),
            pl.BlockSpec((1, LANES), lambda i: (0, 0)),
        ],
        out_specs=[pl.BlockSpec((tm, LANES), lambda i: (i, 0)), pl.BlockSpec((1, LANES), lambda i: (0, 0))],
        out_shape=[jax.ShapeDtypeStruct((N, LANES), F32), jax.ShapeDtypeStruct((1, LANES), F32)],
        scratch_shapes=[pltpu.VMEM((1, LANES), F32)],
        compiler_params=_cparams(1),
        name="router",
    )(x2d, ln_g, wr, br)


def _dispatch_kernel(dest_ref, zb_ref, x_ref, xs_ref, zero_scr, rows_scr, sem, *, tm, tb, n_zb, n_tiles):
    i = pl.program_id(0)

    @pl.when(i == 0)
    def _():
        zero_scr[...] = jnp.zeros_like(zero_scr)

        def zero_copy(n):
            return pltpu.make_async_copy(zero_scr, xs_ref.at[pl.ds(zb_ref[n] * tb, tb)], sem.at[2])

        def zero_issue(n, carry):
            @pl.when(zb_ref[n] >= 0)
            def _():
                zero_copy(n).start()

            return carry

        def zero_wait(n, carry):
            @pl.when(zb_ref[n] >= 0)
            def _():
                zero_copy(n).wait()

            return carry

        lax.fori_loop(0, n_zb, zero_issue, 0)
        lax.fori_loop(0, n_zb, zero_wait, 0)

    buf_now = lax.rem(i, 2)
    rows_scr[buf_now] = x_ref[...].reshape(rows_scr.shape[1:])

    def row_copy(tile, t, slot):
        buf = lax.rem(tile, 2)
        return pltpu.make_async_copy(
            rows_scr.at[buf, pl.ds(t, 1)],
            xs_ref.at[pl.ds(dest_ref[(tile * tm + t) * TOP_K + slot], 1)], sem.at[buf])

    def issue(tt, carry):
        for r in range(ROW_UNROLL):
            for slot in range(TOP_K):
                row_copy(i, tt * ROW_UNROLL + r, slot).start(priority=slot)
        return carry

    def drain(tile):
        buf = lax.rem(tile, 2)
        for _ in range(TOP_K):
            pltpu.make_async_copy(rows_scr.at[buf], rows_scr.at[buf], sem.at[buf]).wait()

    lax.fori_loop(0, tm // ROW_UNROLL, issue, 0)

    @pl.when(i > 0)
    def _():
        drain(i - 1)

    @pl.when(i == n_tiles - 1)
    def _():
        drain(i)


def _dispatch(dest, zero_blocks, x2d, *, tm, tb, n_rows):
    N, D = x2d.shape
    return pl.pallas_call(
        functools.partial(_dispatch_kernel, tm=tm, tb=tb, n_zb=zero_blocks.shape[0], n_tiles=N // tm),
        grid_spec=pltpu.PrefetchScalarGridSpec(
            num_scalar_prefetch=2,
            grid=(N // tm,),
            in_specs=[pl.BlockSpec((tm, D), lambda i, d, z: (i, 0))],
            out_specs=pl.BlockSpec(memory_space=pl.ANY),
            scratch_shapes=[pltpu.VMEM((tb, D // LANES, LANES), F32), pltpu.VMEM((2, tm, D // LANES, LANES), F32),
                            pltpu.SemaphoreType.DMA((3,))],
        ),
        out_shape=jax.ShapeDtypeStruct((n_rows, D // LANES, LANES), F32),
        compiler_params=_cparams(1),
        name="moe_dispatch",
    )(dest, zero_blocks, x2d)


def _ffn_kernel(be_ref, nb_ref, xs_ref, lng_ref, wg_ref, wu_ref, wd_ref, y_ref, wg_scr, wu_scr, wd_scr):
    i = pl.program_id(0)
    used = i < nb_ref[0]
    tb = xs_ref.shape[0]

    @pl.when(jnp.logical_or(i == 0, be_ref[i] != be_ref[jnp.maximum(i - 1, 0)]))
    def _():
        wg_scr[...] = wg_ref[0].astype(BF16)
        wu_scr[...] = wu_ref[0].astype(BF16)
        wd_scr[...] = wd_ref[0].astype(BF16)

    @pl.when(used)
    def _():
        h = _rms(xs_ref[...].reshape(tb, -1), lng_ref[...]).astype(BF16)
        g = jnp.dot(h, wg_scr[...], preferred_element_type=F32)
        u = jnp.dot(h, wu_scr[...], preferred_element_type=F32)
        y = jnp.dot((_silu(g) * u).astype(BF16), wd_scr[...], preferred_element_type=F32)
        y_ref[...] = y.reshape(y_ref.shape)

    @pl.when(jnp.logical_not(used))
    def _():
        y_ref[...] = jnp.zeros_like(y_ref)


def _ffn(blk_e, nb_used, xs, ln_g, wg, wu, wd, layer, *, tb):
    P, S, _ = xs.shape
    D = S * LANES
    nb = P // tb
    DE = wg.shape[3]
    return pl.pallas_call(
        _ffn_kernel,
        grid_spec=pltpu.PrefetchScalarGridSpec(
            num_scalar_prefetch=2,
            grid=(nb,),
            in_specs=[
                pl.BlockSpec((tb, S, LANES), lambda i, be, nbu: (jnp.minimum(i, nbu[0] - 1), 0, 0)),
                pl.BlockSpec((1, D), lambda i, be, nbu: (0, 0)),
                pl.BlockSpec((None, 1, D, DE), lambda i, be, nbu: (layer, be[i], 0, 0)),
                pl.BlockSpec((None, 1, D, DE), lambda i, be, nbu: (layer, be[i], 0, 0)),
                pl.BlockSpec((None, 1, DE, D), lambda i, be, nbu: (layer, be[i], 0, 0)),
            ],
            out_specs=pl.BlockSpec((tb, S, LANES), lambda i, be, nbu: (i, 0, 0)),
            scratch_shapes=[pltpu.VMEM((D, DE), BF16), pltpu.VMEM((D, DE), BF16), pltpu.VMEM((DE, D), BF16)],
        ),
        out_shape=jax.ShapeDtypeStruct((P, S, LANES), F32),
        compiler_params=_cparams(1),
        name="moe_ffn",
    )(blk_e, nb_used, xs, ln_g, wg, wu, wd)


def _combine_kernel(dest_ref, x_ref, info_ref, yb_ref, y_ref, g_scr, sem, *, tm, n_tiles):
    i = pl.program_id(0)

    def row_copy(tile, t, slot):
        buf = lax.rem(tile, 2)
        return pltpu.make_async_copy(
            yb_ref.at[pl.ds(dest_ref[(tile * tm + t) * TOP_K + slot], 1)],
            g_scr.at[buf, slot, pl.ds(t, 1)], sem.at[buf])

    def issue_tile(tile):
        def body(tt, carry):
            for r in range(ROW_UNROLL):
                for slot in range(TOP_K):
                    row_copy(tile, tt * ROW_UNROLL + r, slot).start(priority=slot)
            return carry

        lax.fori_loop(0, tm // ROW_UNROLL, body, 0)

    @pl.when(i == 0)
    def _():
        issue_tile(i)

    @pl.when(i + 1 < n_tiles)
    def _():
        issue_tile(i + 1)

    buf = lax.rem(i, 2)
    pltpu.make_async_copy(g_scr.at[buf], g_scr.at[buf], sem.at[buf]).wait()
    info = info_ref[...]
    lane = lax.broadcasted_iota(jnp.int32, info.shape, 1)
    w1 = jnp.sum(jnp.where(lane == 2, info, 0.0), axis=-1, keepdims=True)
    w2 = jnp.sum(jnp.where(lane == 3, info, 0.0), axis=-1, keepdims=True)
    g1 = g_scr[buf, 0].reshape(x_ref.shape)
    g2 = g_scr[buf, 1].reshape(x_ref.shape)
    y_ref[...] = x_ref[...] + (w1 * g1 + w2 * g2)


def _combine(dest, x2d, info, yb, *, tm):
    N, D = x2d.shape
    return pl.pallas_call(
        functools.partial(_combine_kernel, tm=tm, n_tiles=N // tm),
        grid_spec=pltpu.PrefetchScalarGridSpec(
            num_scalar_prefetch=1,
            grid=(N // tm,),
            in_specs=[
                pl.BlockSpec((tm, D), lambda i, d: (i, 0)),
                pl.BlockSpec((tm, LANES), lambda i, d: (i, 0)),
                pl.BlockSpec(memory_space=pl.ANY),
            ],
            out_specs=pl.BlockSpec((tm, D), lambda i, d: (i, 0)),
            scratch_shapes=[pltpu.VMEM((2, TOP_K, tm, D // LANES, LANES), F32), pltpu.SemaphoreType.DMA((2,))],
        ),
        out_shape=jax.ShapeDtypeStruct((N, D), F32),
        compiler_params=_cparams(1),
        name="moe_combine",
    )(dest, x2d, info, yb)


def _moe(x2d, ln2_g, wr, br, wg, wu, wd, layer, *, tm):
    N, D = x2d.shape
    tb = MOE_ROWS
    info, counts = _router(x2d, ln2_g, wr, br, tm=tm)
    counts = counts[0, N_GROUPS:N_GROUPS + N_EXPERTS].astype(jnp.int32)
    pcounts = (counts + tb - 1) // tb * tb
    pend = jnp.cumsum(pcounts)
    pstart = pend - pcounts
    e = info[:, 0:TOP_K].astype(jnp.int32)
    rank = info[:, 4:4 + TOP_K].astype(jnp.int32)
    experts = jnp.arange(N_EXPERTS, dtype=jnp.int32)
    dest = (jnp.sum(jnp.where(e[..., None] == experts, pstart, 0), axis=-1) + rank).reshape(-1)
    nb = -(-(N * TOP_K) // tb) + N_EXPERTS
    P = nb * tb
    blocks = jnp.arange(nb, dtype=jnp.int32)
    blk_e = jnp.minimum(jnp.sum((pend[None, :] <= blocks[:, None] * tb).astype(jnp.int32), axis=1), N_EXPERTS - 1)
    nb_used = (pend[-1] // tb).astype(jnp.int32).reshape(1)
    zero_blocks = jnp.concatenate([jnp.where(counts % tb != 0, pend // tb - 1, -1),
                                   jnp.where(blocks >= nb_used[0], blocks, -1)]).astype(jnp.int32)
    xs = _dispatch(dest, zero_blocks, x2d, tm=tm, tb=tb, n_rows=P)
    yb = _ffn(blk_e, nb_used, xs, ln2_g, wg, wu, wd, layer, tb=tb)
    return _combine(dest, x2d, info, yb, tm=tm)


def _rows8(x):
    return jnp.broadcast_to(x, (8, x.shape[1]))


def _row_hdot(x, m):
    return _hdot(_rows8(x), m)[0:1]


def _bf_round(x):
    return x.astype(BF16).astype(F32)


def _sample_attn_kernel(z_ref, c0, c1, c2, qg_ref, kg_ref, oa_ref, kv_ref):
    W = SWA_GW
    scale = SWA_DIM ** -0.5
    z = z_ref[0]
    sub = lax.broadcasted_iota(jnp.int32, (SWA_HEADS, W), 0)
    lane = lax.broadcasted_iota(jnp.int32, (SWA_HEADS, W), 1)
    own = lane // SWA_DIM == sub

    def heads(row):
        return jnp.where(own, jnp.broadcast_to(row, (SWA_HEADS, W)), 0.0)

    def head_sum(row):
        return jnp.sum(heads(row), axis=-1, keepdims=True)

    def spread(col):
        return jnp.sum(jnp.where(own, col, 0.0), axis=0, keepdims=True)

    def headnorm(zz, g):
        return zz * spread(lax.rsqrt(head_sum(zz * zz) * (1.0 / SWA_DIM) + EPS)) * g

    outs, lses = [], []
    for gi, (c_ref, (win, dil)) in enumerate(zip((c0, c1, c2), SWA_CONFIGS)):
        q = headnorm(z[:, gi * W:(gi + 1) * W], qg_ref[gi:gi + 1, :])
        k = headnorm(z[:, 3 * W + gi * W:3 * W + (gi + 1) * W], kg_ref[gi:gi + 1, :])
        v = z[:, 6 * W + gi * W:6 * W + (gi + 1) * W]
        kv_ref[0, :, 2 * gi * W:(2 * gi + 1) * W] = k
        kv_ref[0, :, (2 * gi + 1) * W:(2 * gi + 2) * W] = v
        kc = c_ref[0].reshape(W, win).astype(BF16)
        vc = c_ref[1].reshape(W, win).astype(BF16)
        s_c = jnp.dot(heads(q).astype(BF16), kc, preferred_element_type=F32) * scale
        row = lax.broadcasted_iota(jnp.int32, s_c.shape, 1)
        s_c = jnp.where(row % dil == 0, s_c, -jnp.inf)
        s_n = head_sum(_bf_round(k) * _bf_round(q)) * scale
        m = jnp.maximum(jnp.max(s_c, axis=-1, keepdims=True), s_n)
        p_c = jnp.exp(s_c - m)
        p_n = jnp.exp(s_n - m)
        den = jnp.sum(p_c, axis=-1, keepdims=True) + p_n
        pv = lax.dot_general(p_c.astype(BF16), vc, (((1,), (1,)), ((), ())), preferred_element_type=F32)
        num = jnp.sum(jnp.where(own, pv, 0.0), axis=0, keepdims=True) + spread(_bf_round(p_n)) * _bf_round(v)
        outs.append(num / spread(den))
        lses.append(m + jnp.log(den))
    mm = jnp.maximum(jnp.maximum(lses[0], lses[1]), lses[2])
    es = [jnp.exp(l - mm) for l in lses]
    tot = es[0] + es[1] + es[2]
    oa_ref[0] = sum(spread(_bf_round(e / tot)) * _bf_round(o) for e, o in zip(es, outs))


def _sample_attn(z3, caches, layer, qg, kg):
    Bs = z3.shape[0]
    W = SWA_GW
    cviews, cspecs = [], []
    for (win, dil), c in zip(SWA_CONFIGS, caches):
        assert c.shape[2] == win
        cviews.append(jnp.transpose(c, (0, 1, 3, 4, 5, 2)))
        cspecs.append(pl.BlockSpec((None, None, 2, SWA_HEADS, SWA_DIM, win), lambda b: (layer, b, 0, 0, 0, 0)))
    full = lambda a: pl.BlockSpec(a.shape, lambda b: (0,) * a.ndim)
    return pl.pallas_call(
        _sample_attn_kernel,
        grid=(Bs,),
        in_specs=[pl.BlockSpec((1, 1, 9 * W), lambda b: (b, 0, 0))] + cspecs + [full(qg), full(kg)],
        out_specs=[pl.BlockSpec((1, 1, W), lambda b: (b, 0, 0)), pl.BlockSpec((1, 1, 6 * W), lambda b: (b, 0, 0))],
        out_shape=[jax.ShapeDtypeStruct((Bs, 1, W), F32), jax.ShapeDtypeStruct((Bs, 1, 6 * W), F32)],
        compiler_params=_cparams(1),
        name="sample_attn",
    )(z3, *cviews, qg, kg)


def _sample_dn_kernel(raw_ref, cs_ref, cw_ref, ba_ref, par_ref, s_ref, e_ref, etb_ref, etg_ref, o_ref, so_ref):
    E, ETB, ETG = e_ref[...], etb_ref[...], etg_ref[...]
    width = DN_HEADS * DN_DK
    conv = cw_ref[DN_CONV - 1:DN_CONV, :] * raw_ref[0]
    for t in range(DN_CONV - 1):
        conv = conv + cw_ref[t:t + 1, :] * cs_ref[0, t:t + 1, :]
    act = _silu(conv)

    def l2(zz):
        return zz * _row_hdot(lax.rsqrt(_row_hdot(zz * zz, E) + EPS), ETB)

    qn = l2(act[:, 0:width]) * (DN_DK ** -0.5)
    kn = l2(act[:, width:2 * width])
    vn = act[:, 2 * width:3 * width]
    ba = ba_ref[0]
    beta = _row_hdot(_sigmoid(ba), ETB)
    eg = jnp.exp(_row_hdot(par_ref[0:1, :] * _softplus(ba + par_ref[1:2, :]), ETG))
    row0 = lax.broadcasted_iota(jnp.int32, (8, LANES), 0) == 0
    for h in range(DN_HEADS):
        sl = slice(h * LANES, (h + 1) * LANES)
        S = s_ref[0, h]
        q, k, v, b, e = qn[:, sl], kn[:, sl], vn[:, sl], beta[:, sl], eg[:, sl]
        Sr = _bf_round(S)
        v_new = v * b - _row_hdot(_bf_round(k * b * e), Sr)
        a = jnp.sum(q * k, axis=-1, keepdims=True)
        o_ref[0, :, sl] = _row_hdot(_bf_round(q * e), Sr) + a * v_new
        k8 = jnp.where(row0, _rows8(k), 0.0)
        upd = lax.dot_general(k8, _rows8(v_new), (((0,), (0,)), ((), ())), preferred_element_type=F32, precision=HI)
        so_ref[0, h] = S * e + upd


def _sample_dn(raw3, conv_state, s0, layer, conv_w, ba3, par, e_mat, etb, etg):
    Bs, _, C = raw3.shape
    H = DN_HEADS
    full = lambda a: pl.BlockSpec(a.shape, lambda b: (0,) * a.ndim)
    return pl.pallas_call(
        _sample_dn_kernel,
        grid=(Bs,),
        in_specs=[pl.BlockSpec((1, 1, C), lambda b: (b, 0, 0)),
                  pl.BlockSpec((None, 1, DN_CONV - 1, C), lambda b: (layer, b, 0, 0)),
                  full(conv_w),
                  pl.BlockSpec((1, 1, LANES), lambda b: (b, 0, 0)),
                  full(par),
                  pl.BlockSpec((None, 1, H, DN_DK, LANES), lambda b: (layer, b, 0, 0, 0)),
                  full(e_mat), full(etb), full(etg)],
        out_specs=[pl.BlockSpec((1, 1, H * LANES), lambda b: (b, 0, 0)),
                   pl.BlockSpec((1, H, DN_DK, LANES), lambda b: (b, 0, 0, 0))],
        out_shape=[jax.ShapeDtypeStruct((Bs, 1, H * LANES), F32), jax.ShapeDtypeStruct(s0.shape[1:], F32)],
        compiler_params=_cparams(1),
        name="sample_dn",
    )(raw3, conv_state, conv_w, ba3, par, s0, e_mat, etb, etg)


def _sample_out_kernel(x_ref, oa_ref, od_ref, gates_ref, dng_ref, wa_ref, wb_ref, wo_ref, y_ref):
    y_ref[...] = _gated_mix(oa_ref[...], od_ref[...], gates_ref[...], dng_ref[...], wa_ref[...], wb_ref[...],
                            wo_ref[...], x_ref[...], _bdot)


def _sample_out(x2d, oa, od, gates, dng, wa, wb, wo):
    args = (x2d, oa, od, gates, dng, wa, wb, wo)
    return pl.pallas_call(
        _sample_out_kernel,
        grid=(1,),
        in_specs=[pl.BlockSpec(a.shape, lambda i: (0, 0)) for a in args],
        out_specs=pl.BlockSpec(x2d.shape, lambda i: (0, 0)),
        out_shape=jax.ShapeDtypeStruct(x2d.shape, F32),
        compiler_params=_cparams(1),
        name="sample_out",
    )(*args)


def _head_indicator(width, head):
    c = jnp.arange(width)[:, None] // head
    return (c == jnp.arange(LANES)[None, :]).astype(F32)


def _prep_layer(l, ln1_g, w_in, q_norm_g, k_norm_g, dn_conv_w, dn_a_log, dn_dt_bias, dn_norm_g, w_out_a, w_out_b,
                w_o, ln2_g, w_rg, b_rg, w_re, b_re, w_e_gate, w_e_up, w_e_down):
    D = w_in.shape[1]
    a_w = 3 * 3 * SWA_GW
    dn_w = DN_HEADS * 3 * DN_DK
    hv = DN_HEADS * DN_DK
    w = w_in[l]
    splits = dict(att=w[:, :a_w], dn=w[:, a_w:a_w + dn_w],
                  ba=jnp.pad(w[:, a_w + dn_w:a_w + dn_w + 2 * DN_HEADS], ((0, 0), (0, LANES - 2 * DN_HEADS))),
                  gate=w[:, a_w + dn_w + 2 * DN_HEADS:])
    assert splits["gate"].shape[1] == hv + 2 * D
    tile_heads = lambda g: jnp.broadcast_to(g[:, None, :], (len(SWA_CONFIGS), SWA_HEADS, SWA_DIM)).reshape(len(SWA_CONFIGS), SWA_GW)
    qg, kg = tile_heads(q_norm_g[l]), tile_heads(k_norm_g[l])
    idx = jnp.arange(MXU) // SWA_DIM
    par = jnp.zeros((2, LANES), F32)
    par = par.at[0, DN_HEADS:2 * DN_HEADS].set(-jnp.exp(dn_a_log[l].astype(F32)))
    par = par.at[1, DN_HEADS:2 * DN_HEADS].set(dn_dt_bias[l].astype(F32))
    wr = jnp.pad(jnp.concatenate([w_rg[l], w_re[l]], axis=1), ((0, 0), (0, LANES - N_GROUPS - N_EXPERTS)))
    br = jnp.pad(jnp.concatenate([b_rg[l], b_re[l]]), (0, LANES - N_GROUPS - N_EXPERTS)).reshape(1, LANES)
    e8 = _head_indicator(hv, DN_DK)
    return dict(
        bf16={k: v.astype(BF16) for k, v in splits.items()},
        ln1=ln1_g[l].reshape(1, D), ln2=ln2_g[l].reshape(1, D),
        ng=jnp.stack([qg.reshape(1, -1), kg.reshape(1, -1)]), qg=qg, kg=kg,
        bd=((idx[:, None] == idx[None, :]).astype(F32) / SWA_DIM).astype(BF16),
        conv_w=dn_conv_w[l], par=par, dng=dn_norm_g[l].reshape(1, DN_DK),
        wa=w_out_a[l].astype(BF16), wb=w_out_b[l].astype(BF16), wo=w_o[l].astype(BF16), wr=wr.astype(BF16), br=br,
        wg=w_e_gate, wu=w_e_up, wd=w_e_down, layer=l,
        e_dn=e8, etb=e8.T, etg=jnp.roll(e8, DN_HEADS, axis=1).T,
        e_att=_head_indicator(SWA_GW, SWA_DIM).T.astype(BF16),
    )


def _layer_prompt(x, p):
    B, L, D = x.shape
    N = B * L
    x2d = x.reshape(N, D)
    bw = p["bf16"]
    pk0, pk1, pk2, t0, t1, t2 = _proj_attn(x, p["ln1"], bw["att"], p["ng"], p["bd"], tm=min(512, L))
    tmp = min(1024, N)
    qd, kd, vd, raw_tail = _proj_dn(x, p["ln1"], bw["dn"], p["conv_w"], tm=min(512, L))
    gates = _proj_plain(x2d, p["ln1"], bw["gate"], tm=tmp, tn=1536, out_dtype=BF16, name="proj_gate")
    ba = _proj_plain(x2d, p["ln1"], bw["ba"], tm=tmp, tn=LANES, out_dtype=F32, name="proj_ba")
    os_, ls_ = [], []
    for pk in (pk0, pk1, pk2):
        d, M = pk.shape[1], pk.shape[2]
        o, lse = _attn(pk.reshape(B * d, M, pk.shape[3]), tq=min(256, M))
        os_.append(o.reshape(B, d, M, SWA_GW))
        ls_.append(lse.reshape(B, d, M, LANES))
    gb, gt_rows = _dn_gates(ba.reshape(B, L, LANES), p["par"], tl=min(256, L))
    u, w, qdec, kdec, a, gt = _dn_intra(qd, kd, vd, gb, gt_rows, tl=min(2048, L))
    od, s_new = _dn_scan(u, w, qdec, kdec, a, gt, jnp.zeros((B, DN_HEADS, DN_DK, LANES), F32), tl=min(1024, L))
    x2 = _out_proj(x2d, os_, ls_, od.reshape(N, -1), gates, p["dng"], p["wa"], p["wb"], p["wo"], p["e_att"],
                   B=B, L=L, tm=min(512, L))
    y = _moe(x2, p["ln2"], p["wr"], p["br"], p["wg"], p["wu"], p["wd"], p["layer"], tm=256)
    return y.reshape(B, L, D), [t0, t1, t2], raw_tail[:, 8 - (DN_CONV - 1):], s_new


def _layer_sample(x, caches, conv_state, s0, layer, p):
    Bs, T, D = x.shape
    assert T == 1
    x2d = x.reshape(Bs, D)
    bw = p["bf16"]
    proj = functools.partial(_proj_plain, x2d, p["ln1"], tm=Bs, out_dtype=F32)
    z_att = proj(bw["att"], tn=1536, name="sproj_att")
    raw = proj(bw["dn"], tn=1536, name="sproj_dn")
    gates = proj(bw["gate"], tn=1536, name="sproj_gate")
    ba = proj(bw["ba"], tn=LANES, name="sproj_ba")
    oa, kv = _sample_attn(z_att.reshape(Bs, 1, -1), caches, layer, p["qg"], p["kg"])
    raw3 = raw.reshape(Bs, 1, -1)
    od, s_new = _sample_dn(raw3, conv_state, s0, layer, p["conv_w"], ba.reshape(Bs, 1, LANES), p["par"],
                           p["e_dn"], p["etb"], p["etg"])
    x2 = _sample_out(x2d, oa.reshape(Bs, -1), od.reshape(Bs, -1), gates, p["dng"], p["wa"], p["wb"], p["wo"])
    y = _moe(x2, p["ln2"], p["wr"], p["br"], p["wg"], p["wu"], p["wd"], p["layer"], tm=Bs)
    W2 = 2 * SWA_GW
    kvs = [kv[:, :, g * W2:(g + 1) * W2].reshape(Bs, 1, 2, SWA_HEADS, SWA_DIM) for g in range(len(SWA_CONFIGS))]
    new_conv = jnp.concatenate([conv_state[layer][:, 1:], raw3], axis=1)
    return y.reshape(Bs, 1, D), kvs, new_conv, s_new


def kernel(x_prompt, x_sample, cache_swa0_kv, cache_swa1_kv, cache_swa2_kv, state_dn_conv, state_dn_S, ln1_g, w_in,
           q_norm_g, k_norm_g, dn_conv_w, dn_a_log, dn_dt_bias, dn_norm_g, w_out_a, w_out_b, w_o, ln2_g, w_rg, b_rg,
           w_re, b_re, w_e_gate, w_e_up, w_e_down):
    yp, ys = x_prompt, x_sample
    outs = [[] for _ in range(10)]
    for l in range(w_in.shape[0]):
        p = _prep_layer(l, ln1_g, w_in, q_norm_g, k_norm_g, dn_conv_w, dn_a_log, dn_dt_bias, dn_norm_g, w_out_a,
                        w_out_b, w_o, ln2_g, w_rg, b_rg, w_re, b_re, w_e_gate, w_e_up, w_e_down)
        yp, pkv, pconv, ps = _layer_prompt(yp, p)
        ys, skv, sconv, ss = _layer_sample(ys, (cache_swa0_kv, cache_swa1_kv, cache_swa2_kv), state_dn_conv,
                                           state_dn_S, l, p)
        for lst, val in zip(outs, (*pkv, pconv, ps, *skv, sconv, ss)):
            lst.append(val)
    return (yp, ys, *(jnp.stack(o) for o in outs))
```

```python
import functools

import jax
import jax.numpy as jnp
from jax import lax
from jax.experimental import pallas as pl
from jax.experimental.pallas import tpu as pltpu

F32 = jnp.float32
BF16 = jnp.bfloat16
HI = lax.Precision.HIGHEST
EPS = 1e-6

SWA_CONFIGS = ((128, 1), (512, 4), (2048, 16))
SWA_HEADS = 8
SWA_DIM = 64
SWA_GW = SWA_HEADS * SWA_DIM
SWA_SPAN = 128
DN_HEADS = 8
DN_DK = 128
DN_CONV = 4
DN_CHUNK = 64
N_GROUPS = 4
PER_GROUP = 8
N_EXPERTS = N_GROUPS * PER_GROUP
TOP_K = 2

VMEM_LIMIT_BYTES = 56 * 1024 * 1024
LANES = 128
MXU = 256
MOE_ROWS = 512
MOE_ROWS_SMALL = 128
ROW_UNROLL = 8


def _cparams(n_axes):
    return pltpu.CompilerParams(
        dimension_semantics=("arbitrary",) * n_axes, vmem_limit_bytes=VMEM_LIMIT_BYTES
    )


def _rms(x, g):
    return x * lax.rsqrt(jnp.mean(x * x, axis=-1, keepdims=True) + EPS) * g


def _bdot(a, b):
    return jnp.dot(a.astype(BF16), b.astype(BF16), preferred_element_type=F32)


def _hdot(a, b):
    return jnp.dot(a, b, preferred_element_type=F32, precision=HI)


def _sigmoid(x):
    return 1.0 / (1.0 + jnp.exp(-x))


def _silu(x):
    return x * _sigmoid(x)


def _softplus(x):
    return jnp.maximum(x, 0.0) + jnp.log1p(jnp.exp(-jnp.abs(x)))


def _proj_attn_kernel(x_ref, lng_ref, w_ref, ng_ref, bd_ref, p_ref, t_ref, *, n_tiles):
    rows = x_ref.shape[1]
    h = _rms(x_ref[0], lng_ref[...]).astype(BF16)
    z = jnp.dot(h, w_ref[...], preferred_element_type=F32)
    kv = []
    for c in range(0, 3 * SWA_GW, MXU):
        zc = z[:, c:c + MXU]
        if c < 2 * SWA_GW:
            ms = jnp.dot((zc * zc).astype(BF16), bd_ref[...], preferred_element_type=F32)
            zc = zc * lax.rsqrt(ms + EPS) * ng_ref[0, :, c:c + MXU]
        p_ref[0, 0, :, c:c + MXU] = zc.astype(BF16)
        if c >= SWA_GW:
            kv.append(zc[rows - SWA_SPAN:rows, :])

    @pl.when(pl.program_id(2) == n_tiles - 1)
    def _():
        per = SWA_GW // MXU
        for s in range(2):
            zr = jnp.concatenate(kv[s * per:(s + 1) * per], axis=-1)
            t_ref[0, :, s] = zr.reshape(SWA_SPAN, SWA_HEADS, SWA_DIM)


def _proj_attn(x, ln_g, w_g, ng_g, bd, *, dil, tmr):
    B, L, D = x.shape
    M = L // dil
    assert L % dil == 0 and M % tmr == 0 and tmr >= SWA_SPAN
    nt = M // tmr
    W3 = 3 * SWA_GW
    keep = SWA_SPAN * dil
    p, t = pl.pallas_call(
        functools.partial(_proj_attn_kernel, n_tiles=nt),
        grid=(B, dil, nt),
        in_specs=[
            pl.BlockSpec((1, tmr, D), lambda b, r, i: (b, i, r)),
            pl.BlockSpec((1, D), lambda b, r, i: (0, 0)),
            pl.BlockSpec((D, W3), lambda b, r, i: (0, 0)),
            pl.BlockSpec((1, 1, 2 * SWA_GW), lambda b, r, i: (0, 0, 0)),
            pl.BlockSpec((MXU, MXU), lambda b, r, i: (0, 0)),
        ],
        out_specs=[
            pl.BlockSpec((1, 1, tmr, W3), lambda b, r, i: (b, r, i, 0)),
            pl.BlockSpec((1, SWA_SPAN, None, 2, SWA_HEADS, SWA_DIM), lambda b, r, i: (b, 0, r, 0, 0, 0)),
        ],
        out_shape=[
            jax.ShapeDtypeStruct((B, dil, M, W3), BF16),
            jax.ShapeDtypeStruct((B, SWA_SPAN, dil, 2, SWA_HEADS, SWA_DIM), F32),
        ],
        compiler_params=_cparams(3),
        name="proj_attn",
    )(x.reshape(B, M, dil * D), ln_g, w_g, ng_g, bd)
    return p, t.reshape(B, keep, 2, SWA_HEADS, SWA_DIM)


def _proj_plain_kernel(x_ref, lng_ref, w_ref, o_ref, h_scr):
    @pl.when(pl.program_id(1) == 0)
    def _():
        h_scr[...] = _rms(x_ref[...], lng_ref[...]).astype(BF16)

    o_ref[...] = jnp.dot(h_scr[...], w_ref[...], preferred_element_type=F32).astype(o_ref.dtype)


def _proj_plain(x2d, ln_g, w, *, tm, tn, out_dtype, name="proj_plain"):
    N, D = x2d.shape
    C = w.shape[1]
    assert N % tm == 0 and C % tn == 0
    return pl.pallas_call(
        _proj_plain_kernel,
        grid=(N // tm, C // tn),
        in_specs=[
            pl.BlockSpec((tm, D), lambda i, j: (i, 0)),
            pl.BlockSpec((1, D), lambda i, j: (0, 0)),
            pl.BlockSpec((D, tn), lambda i, j: (0, j)),
        ],
        out_specs=pl.BlockSpec((tm, tn), lambda i, j: (i, j)),
        out_shape=jax.ShapeDtypeStruct((N, C), out_dtype),
        scratch_shapes=[pltpu.VMEM((tm, D), BF16)],
        compiler_params=_cparams(2),
        name=name,
    )(x2d, ln_g, w)


def _attn_kernel(q_ref, kc_ref, vc_ref, kp_ref, vp_ref, o_ref, lse_ref, kk_scr, vv_scr, *, tq):
    i = pl.program_id(1)
    blk = SWA_SPAN
    kk_scr[0:blk, :] = kp_ref[0]
    kk_scr[blk:blk + tq, :] = kc_ref[0]
    vv_scr[0:blk, :] = vp_ref[0]
    vv_scr[blk:blk + tq, :] = vc_ref[0]
    qi = lax.broadcasted_iota(jnp.int32, (blk, 2 * blk), 0)
    ki = lax.broadcasted_iota(jnp.int32, (blk, 2 * blk), 1)
    dist = blk + qi - ki
    band = (dist >= 0) & (dist <= SWA_SPAN)
    band_first = band & ((ki >= blk) | (i > 0))
    lo = lax.broadcasted_iota(jnp.int32, (blk, LANES), 1) < SWA_DIM
    zero = jnp.zeros((blk, LANES), BF16)
    lane = lax.broadcasted_iota(jnp.int32, (blk, LANES), 1)
    for jb in range(tq // blk):
        mask = band_first if jb == 0 else band
        rows = slice(jb * blk, (jb + 1) * blk)
        lse_all = jnp.zeros((blk, LANES), F32)
        for hp in range(SWA_GW // LANES):
            cs = slice(hp * LANES, (hp + 1) * LANES)
            qb = q_ref[0, rows, cs]
            kk = kk_scr[jb * blk:(jb + 2) * blk, cs]
            vv = vv_scr[jb * blk:(jb + 2) * blk, cs]
            res_o = []
            for hh in range(2):
                qm = jnp.where(lo if hh == 0 else jnp.logical_not(lo), qb, zero)
                s = lax.dot_general(qm, kk, (((1,), (1,)), ((), ())), preferred_element_type=F32)
                s = jnp.where(mask, s * (SWA_DIM ** -0.5), -jnp.inf)
                m = jnp.max(s, axis=-1, keepdims=True)
                p = jnp.exp(s - m)
                den = jnp.sum(p, axis=-1, keepdims=True)
                pv = jnp.dot(p.astype(BF16), vv, preferred_element_type=F32)
                res_o.append(pv / den)
                lse_all = jnp.where(lane == 2 * hp + hh, m + jnp.log(den), lse_all)
            o_ref[0, rows, cs] = jnp.where(lo, res_o[0], res_o[1]).astype(BF16)
        lse_ref[0, rows, :] = lse_all


def _attn(p, *, tq):
    S, M, _ = p.shape
    assert M % tq == 0 and tq % SWA_SPAN == 0
    nb = tq // SWA_SPAN
    return pl.pallas_call(
        functools.partial(_attn_kernel, tq=tq),
        grid=(S, M // tq),
        in_specs=[
            pl.BlockSpec((1, tq, SWA_GW), lambda s, i: (s, i, 0)),
            pl.BlockSpec((1, tq, SWA_GW), lambda s, i: (s, i, 1)),
            pl.BlockSpec((1, tq, SWA_GW), lambda s, i: (s, i, 2)),
            pl.BlockSpec((1, SWA_SPAN, SWA_GW), lambda s, i: (s, jnp.maximum(i * nb - 1, 0), 1)),
            pl.BlockSpec((1, SWA_SPAN, SWA_GW), lambda s, i: (s, jnp.maximum(i * nb - 1, 0), 2)),
        ],
        out_specs=[
            pl.BlockSpec((1, tq, SWA_GW), lambda s, i: (s, i, 0)),
            pl.BlockSpec((1, tq, LANES), lambda s, i: (s, i, 0)),
        ],
        out_shape=[
            jax.ShapeDtypeStruct((S, M, SWA_GW), BF16),
            jax.ShapeDtypeStruct((S, M, LANES), F32),
        ],
        scratch_shapes=[
            pltpu.VMEM((SWA_SPAN + tq, SWA_GW), BF16),
            pltpu.VMEM((SWA_SPAN + tq, SWA_GW), BF16),
        ],
        compiler_params=_cparams(2),
        name="swa_attn",
    )(p, p, p, p, p)


def _proj_dn_kernel(x_ref, lng_ref, w_ref, cw_ref, q_ref, k_ref, v_ref, tail_ref, h_scr, z_scr, carry_scr,
                    *, tm, n_ct):
    i = pl.program_id(1)
    j = pl.program_id(2)
    nh = DN_HEADS
    ncb = z_scr.shape[1] // LANES

    @pl.when(j == 0)
    def _():
        h_scr[...] = _rms(x_ref[0], lng_ref[...]).astype(BF16)

    z_scr[0:8, :] = jnp.where(i == 0, 0.0, carry_scr[j])
    z_scr[8:8 + tm, :] = jnp.dot(h_scr[...], w_ref[...], preferred_element_type=F32)
    last = z_scr[tm:tm + 8, :]
    carry_scr[j] = last
    tn = z_scr.shape[1]
    outs = (q_ref, k_ref, v_ref)
    for jj in range(n_ct):

        @pl.when(j == jj)
        def _(jj=jj):
            tail_ref[0, :, jj * tn:(jj + 1) * tn] = last
            for cbl in range(ncb):
                cs = slice(cbl * LANES, (cbl + 1) * LANES)
                part, h = divmod(jj * ncb + cbl, nh)
                xe = z_scr[:, cs]
                acc = cw_ref[0:1, cs] * xe
                for t in range(1, DN_CONV):
                    acc = cw_ref[t:t + 1, cs] * xe + pltpu.roll(acc, 1, axis=0)
                act = _silu(acc[8:])
                if part < 2:
                    act = act * lax.rsqrt(jnp.sum(act * act, axis=-1, keepdims=True) + EPS)
                if part == 0:
                    act = act * (DN_DK ** -0.5)
                outs[part][0, :, h * LANES:(h + 1) * LANES] = act.astype(BF16)


def _proj_dn(x, ln_g, w_dn, conv_w, *, tm):
    B, L, D = x.shape
    C = w_dn.shape[1]
    width = DN_HEADS * DN_DK
    n_ct = 2
    tn = C // n_ct
    assert L % tm == 0 and C == 3 * width and tn % LANES == 0
    qkv = pl.BlockSpec((1, tm, width), lambda b, i, j: (b, i, 0))
    return pl.pallas_call(
        functools.partial(_proj_dn_kernel, tm=tm, n_ct=n_ct),
        grid=(B, L // tm, n_ct),
        in_specs=[
            pl.BlockSpec((1, tm, D), lambda b, i, j: (b, i, 0)),
            pl.BlockSpec((1, D), lambda b, i, j: (0, 0)),
            pl.BlockSpec((D, tn), lambda b, i, j: (0, j)),
            pl.BlockSpec((DN_CONV, tn), lambda b, i, j: (0, j)),
        ],
        out_specs=[qkv, qkv, qkv, pl.BlockSpec((1, 8, C), lambda b, i, j: (b, 0, 0))],
        out_shape=[jax.ShapeDtypeStruct((B, L, width), BF16)] * 3 + [jax.ShapeDtypeStruct((B, 8, C), F32)],
        scratch_shapes=[pltpu.VMEM((tm, D), BF16), pltpu.VMEM((8 + tm, tn), F32), pltpu.VMEM((n_ct, 8, tn), F32)],
        compiler_params=_cparams(3),
        name="proj_dn",
    )(x, ln_g, w_dn, conv_w)


def _gates_kernel(ba_ref, par_ref, g_ref, gt_ref, *, tl):
    nh = DN_HEADS
    ba = ba_ref[0]
    lane = lax.broadcasted_iota(jnp.int32, (tl, LANES), 1)
    g = par_ref[0:1, :] * _softplus(ba + par_ref[1:2, :])
    ri = lax.broadcasted_iota(jnp.int32, (tl, tl), 0)
    ci = lax.broadcasted_iota(jnp.int32, (tl, tl), 1)
    tri = jnp.where((ri // DN_CHUNK == ci // DN_CHUNK) & (ci <= ri), 1.0, 0.0).astype(BF16)
    g_hi = g.astype(BF16)
    r1 = g - g_hi.astype(F32)
    g_mid = r1.astype(BF16)
    g_lo = (r1 - g_mid.astype(F32)).astype(BF16)
    gc = sum(jnp.dot(tri, piece, preferred_element_type=F32) for piece in (g_hi, g_mid, g_lo))
    g_ref[0] = jnp.where(lane < nh, _sigmoid(ba), gc)
    gt_ref[0] = jnp.transpose(gc)[nh:2 * nh, :]


def _dn_gates(ba, par, *, tl):
    B, L, _ = ba.shape
    assert L % tl == 0 and tl % DN_CHUNK == 0
    return pl.pallas_call(
        functools.partial(_gates_kernel, tl=tl),
        grid=(B, L // tl),
        in_specs=[pl.BlockSpec((1, tl, LANES), lambda b, i: (b, i, 0)), pl.BlockSpec((2, LANES), lambda b, i: (0, 0))],
        out_specs=[pl.BlockSpec((1, tl, LANES), lambda b, i: (b, i, 0)),
                   pl.BlockSpec((1, DN_HEADS, tl), lambda b, i: (b, 0, i))],
        out_shape=[jax.ShapeDtypeStruct((B, L, LANES), F32), jax.ShapeDtypeStruct((B, DN_HEADS, L), F32)],
        compiler_params=_cparams(2),
        name="dn_gates",
    )(ba, par)


def _intra_kernel(q_ref, k_ref, v_ref, g_ref, gt_in_ref, u_ref, w_ref, qd_ref, kd_ref, a_ref, gt_ref, *, tl):
    h = pl.program_id(1)
    C = DN_CHUNK
    lane = lax.broadcasted_iota(jnp.int32, (C, LANES), 1)
    ri = lax.broadcasted_iota(jnp.int32, (C, C), 0)
    ci = lax.broadcasted_iota(jnp.int32, (C, C), 1)
    eye = jnp.where(ri == ci, 1.0, 0.0).astype(F32)
    nt_dot = lambda a, b: lax.dot_general(a.astype(BF16), b.astype(BF16), (((1,), (1,)), ((), ())),
                                          preferred_element_type=F32)
    rows = [slice(c * C, (c + 1) * C) for c in range(tl // C)]
    gv = [g_ref[0, r, :] for r in rows]
    q = [q_ref[0, r, :].astype(F32) for r in rows]
    k = [k_ref[0, r, :].astype(F32) for r in rows]
    v = [v_ref[0, r, :].astype(F32) for r in rows]
    beta = [jnp.sum(jnp.where(lane == h, x, 0.0), axis=-1, keepdims=True) for x in gv]
    gc = [jnp.sum(jnp.where(lane == h + DN_HEADS, x, 0.0), axis=-1, keepdims=True) for x in gv]
    gc_row = gt_in_ref[0, pl.ds(h, 1), :]
    decay = [jnp.exp(jnp.where(ri >= ci, a - gc_row[:, r], -jnp.inf)) for a, r in zip(gc, rows)]
    kb = [a * b for a, b in zip(k, beta)]
    x = [-jnp.where(ri > ci, nt_dot(a, b) * d, 0.0) for a, b, d in zip(kb, k, decay)]
    t = [eye + a for a in x]
    for _ in range(5):
        x = [_bdot(a, a) for a in x]
        t = [a + _bdot(a, b) for a, b in zip(t, x)]
    eg = [jnp.exp(a) for a in gc]
    glast = [a[C - 1:C, :] for a in gc]
    u = [_bdot(a, b * c) for a, b, c in zip(t, v, beta)]
    w = [_bdot(a, b * c) for a, b, c in zip(t, kb, eg)]
    qk = [nt_dot(a, b) for a, b in zip(q, k)]
    for c, r in enumerate(rows):
        u_ref[0, 0, r, :] = u[c]
        w_ref[0, 0, r, :] = w[c].astype(BF16)
        a_ref[0, 0, r, :] = (qk[c] * decay[c]).astype(BF16)
        qd_ref[0, 0, r, :] = (q[c] * eg[c]).astype(BF16)
        kd_ref[0, 0, r, :] = (k[c] * jnp.exp(glast[c] - gc[c])).astype(BF16)
        gt_ref[0, 0, c:c + 1, :] = jnp.broadcast_to(jnp.exp(glast[c]), (1, LANES))


def _dn_intra(q, k, v, g, gt_rows, *, tl):
    B, L, _ = q.shape
    H, C = DN_HEADS, DN_CHUNK
    assert L % tl == 0 and (tl // C) % 8 == 0
    qkv_spec = pl.BlockSpec((1, tl, LANES), lambda b, h, i: (b, i, h))
    hl = lambda w: pl.BlockSpec((1, 1, tl, w), lambda b, h, i: (b, h, i, 0))
    return pl.pallas_call(
        functools.partial(_intra_kernel, tl=tl),
        grid=(B, H, L // tl),
        in_specs=[qkv_spec, qkv_spec, qkv_spec, pl.BlockSpec((1, tl, LANES), lambda b, h, i: (b, i, 0)),
                  pl.BlockSpec((1, H, tl), lambda b, h, i: (b, 0, i))],
        out_specs=[hl(LANES), hl(LANES), hl(LANES), hl(LANES), hl(C),
                   pl.BlockSpec((1, 1, tl // C, LANES), lambda b, h, i: (b, h, i, 0))],
        out_shape=[
            jax.ShapeDtypeStruct((B, H, L, LANES), F32),
            jax.ShapeDtypeStruct((B, H, L, LANES), BF16),
            jax.ShapeDtypeStruct((B, H, L, LANES), BF16),
            jax.ShapeDtypeStruct((B, H, L, LANES), BF16),
            jax.ShapeDtypeStruct((B, H, L, C), BF16),
            jax.ShapeDtypeStruct((B, H, L // C, LANES), F32),
        ],
        compiler_params=_cparams(3),
        name="dn_intra",
    )(q, k, v, g, gt_rows)


def _scan_kernel(u_ref, w_ref, qd_ref, kd_ref, a_ref, gt_ref, s0_ref, o_ref, s_ref, *, n_chunks):
    C = DN_CHUNK
    H = s_ref.shape[1]

    @pl.when(pl.program_id(1) == 0)
    def _():
        s_ref[...] = s0_ref[...]

    def body(c, carry):
        rows = pl.ds(pl.multiple_of(c * C, C), C)
        S = [s_ref[0, h] for h in range(H)]
        Sb = [x.astype(BF16) for x in S]
        v_new = [u_ref[0, h, rows, :] - jnp.dot(w_ref[0, h, rows, :], Sb[h], preferred_element_type=F32)
                 for h in range(H)]
        vb = [x.astype(BF16) for x in v_new]
        o = [jnp.dot(qd_ref[0, h, rows, :], Sb[h], preferred_element_type=F32)
             + jnp.dot(a_ref[0, h, rows, :], vb[h], preferred_element_type=F32) for h in range(H)]
        upd = [lax.dot_general(kd_ref[0, h, rows, :], vb[h], (((0,), (0,)), ((), ())), preferred_element_type=F32)
               for h in range(H)]
        for h in range(H):
            o_ref[0, rows, h * LANES:(h + 1) * LANES] = o[h]
            s_ref[0, h] = S[h] * gt_ref[0, h, pl.ds(c, 1), :] + upd[h]
        return carry

    lax.fori_loop(0, n_chunks, body, 0)


def _dn_scan(u, w, qd, kd, a, gt, s0, *, tl):
    B, H, L, _ = u.shape
    C = DN_CHUNK
    assert L % tl == 0 and (tl // C) % 8 == 0
    hs = lambda wd: pl.BlockSpec((1, H, tl, wd), lambda b, i: (b, 0, i, 0))
    s_spec = pl.BlockSpec((1, H, DN_DK, LANES), lambda b, i: (b, 0, 0, 0))
    return pl.pallas_call(
        functools.partial(_scan_kernel, n_chunks=tl // C),
        grid=(B, L // tl),
        in_specs=[hs(LANES), hs(LANES), hs(LANES), hs(LANES), hs(C),
                  pl.BlockSpec((1, H, tl // C, LANES), lambda b, i: (b, 0, i, 0)), s_spec],
        out_specs=[pl.BlockSpec((1, tl, H * LANES), lambda b, i: (b, i, 0)), s_spec],
        out_shape=[jax.ShapeDtypeStruct((B, L, H * LANES), F32),
                   jax.ShapeDtypeStruct((B, H, DN_DK, LANES), F32)],
        compiler_params=_cparams(2),
        name="dn_scan",
    )(u, w, qd, kd, a, gt, s0)


def _gated_mix(o_a, od, gates, dng, wa, wb, wo, x, dot):
    width = DN_HEADS * DN_DK
    parts = []
    for h in range(DN_HEADS):
        blk = od[:, h * LANES:(h + 1) * LANES]
        parts.append(blk * lax.rsqrt(jnp.mean(blk * blk, axis=-1, keepdims=True) + EPS) * dng)
    odn = jnp.concatenate(parts, axis=-1) * _silu(gates[:, 0:width].astype(F32))
    ya = dot(o_a, wa)
    yb = dot(odn, wb)
    mix = _sigmoid(gates[:, width:2 * width].astype(F32)) * ya + _sigmoid(gates[:, 2 * width:].astype(F32)) * yb
    return x + dot(mix, wo)


def _out_kernel(x_ref, o0, o1, o2, l0, l1, l2, od_ref, gates_ref, dng_ref, wa_ref, wb_ref, wo_ref, e_ref, y_ref,
                so0, so1, so2, sl0, sl1, sl2, *, tm, dils):
    o_refs, l_refs = (o0, o1, o2), (l0, l1, l2)
    so, sl = (so0, so1, so2), (sl0, sl1, sl2)
    for gi, d in enumerate(dils):
        for r in range(d):
            dst = slice(None) if d == 1 else pl.ds(r, tm // d, stride=d)
            sl[gi][dst, :] = l_refs[gi][0, r]
            for cb in range(SWA_GW // LANES):
                so[gi][cb, dst, :] = o_refs[gi][0, r, :, cb * LANES:(cb + 1) * LANES].astype(F32)
    ls = [s[...] for s in sl]
    m = jnp.maximum(jnp.maximum(ls[0], ls[1]), ls[2])
    es = [jnp.exp(l - m) for l in ls]
    tot = es[0] + es[1] + es[2]
    alphas = [jnp.dot((e / tot).astype(BF16), e_ref[...], preferred_element_type=F32) for e in es]
    parts = []
    for cb in range(SWA_GW // LANES):
        cs = slice(cb * LANES, (cb + 1) * LANES)
        parts.append(alphas[0][:, cs] * so[0][cb] + alphas[1][:, cs] * so[1][cb] + alphas[2][:, cs] * so[2][cb])
    o_a = jnp.concatenate(parts, axis=-1)
    y_ref[...] = _gated_mix(o_a, od_ref[...], gates_ref[...], dng_ref[...], wa_ref[...], wb_ref[...],
                            wo_ref[...], x_ref[...], _bdot)


def _out_proj(x2d, os_, ls_, od2d, gates, dng, wa, wb, wo, e_att, *, B, L, tm):
    N, D = x2d.shape
    nt = L // tm
    dils = tuple(d for _, d in SWA_CONFIGS)
    grp = lambda d, w: pl.BlockSpec((1, d, tm // d, w), lambda i: (i // nt, 0, i % nt, 0))
    row = lambda w: pl.BlockSpec((tm, w), lambda i: (i, 0))
    full = lambda a: pl.BlockSpec(a.shape, lambda i: (0, 0))
    return pl.pallas_call(
        functools.partial(_out_kernel, tm=tm, dils=dils),
        grid=(N // tm,),
        in_specs=[row(D)] + [grp(d, SWA_GW) for d in dils] + [grp(d, LANES) for d in dils]
        + [row(od2d.shape[1]), row(gates.shape[1]), full(dng), full(wa), full(wb), full(wo), full(e_att)],
        out_specs=row(D),
        out_shape=jax.ShapeDtypeStruct((N, D), F32),
        scratch_shapes=[pltpu.VMEM((SWA_GW // LANES, tm, LANES), F32)] * 3 + [pltpu.VMEM((tm, LANES), F32)] * 3,
        compiler_params=_cparams(1),
        name="out_proj",
    )(x2d, *os_, *ls_, od2d, gates, dng, wa, wb, wo, e_att)


def _router_kernel(x_ref, lng_ref, wr_ref, br_ref, info_ref, cnt_ref, base_scr, *, tm):
    i = pl.program_id(0)

    @pl.when(i == 0)
    def _():
        base_scr[...] = jnp.zeros_like(base_scr)

    h = _rms(x_ref[...], lng_ref[...])
    lg = _bdot(h, wr_ref[...]) + br_ref[...]
    lane = lax.broadcasted_iota(jnp.int32, (tm, LANES), 1)
    big = jnp.int32(1 << 20)
    ninf = -jnp.inf

    def argmax_lane(vals):
        mx = jnp.max(vals, axis=-1, keepdims=True)
        idx = jnp.min(jnp.where(vals == mx, lane, big), axis=-1, keepdims=True)
        return mx, idx

    lgm = jnp.where(lane < N_GROUPS, lg, ninf)
    mg, gsel = argmax_lane(lgm)
    pg = 1.0 / jnp.sum(jnp.exp(lgm - mg), axis=-1, keepdims=True)
    start = N_GROUPS + gsel * PER_GROUP
    le = jnp.where((lane >= start) & (lane < start + PER_GROUP), lg, ninf)
    m1, i1 = argmax_lane(le)
    m2, i2 = argmax_lane(jnp.where(lane == i1, ninf, le))
    e21 = jnp.exp(m2 - m1)
    w1 = pg / (1.0 + e21)
    w2 = pg * e21 / (1.0 + e21)
    oh = jnp.where(lane == i1, 1.0, 0.0) + jnp.where(lane == i2, 1.0, 0.0)
    ri = lax.broadcasted_iota(jnp.int32, (tm, tm), 0)
    ci = lax.broadcasted_iota(jnp.int32, (tm, tm), 1)
    strict = jnp.where(ci < ri, 1.0, 0.0).astype(BF16)
    pref = jnp.dot(strict, oh.astype(BF16), preferred_element_type=F32) + base_scr[...]
    r1 = jnp.sum(jnp.where(lane == i1, pref, 0.0), axis=-1, keepdims=True)
    r2 = jnp.sum(jnp.where(lane == i2, pref, 0.0), axis=-1, keepdims=True)
    base_scr[...] = base_scr[...] + jnp.sum(oh, axis=0, keepdims=True)
    cnt_ref[...] = base_scr[...]
    off = jnp.float32(N_GROUPS)
    info = jnp.where(lane == 0, i1.astype(F32) - off, 0.0)
    info = jnp.where(lane == 1, i2.astype(F32) - off, info)
    info = jnp.where(lane == 2, w1, info)
    info = jnp.where(lane == 3, w2, info)
    info = jnp.where(lane == 4, r1, info)
    info = jnp.where(lane == 5, r2, info)
    info_ref[...] = info


def _router(x2d, ln_g, wr, br, *, tm):
    N, D = x2d.shape
    assert N % tm == 0
    return pl.pallas_call(
        functools.partial(_router_kernel, tm=tm),
        grid=(N // tm,),
        in_specs=[
            pl.BlockSpec((tm, D), lambda i: (i, 0)),
            pl.BlockSpec((1, D), lambda i: (0, 0)),
            pl.BlockSpec((D, LANES), lambda i: (0, 0)),
            pl.BlockSpec((1, LANES), lambda i: (0, 0)),
        ],
        out_specs=[pl.BlockSpec((tm, LANES), lambda i: (i, 0)), pl.BlockSpec((1, LANES), lambda i: (0, 0))],
        out_shape=[jax.ShapeDtypeStruct((N, LANES), F32), jax.ShapeDtypeStruct((1, LANES), F32)],
        scratch_shapes=[pltpu.VMEM((1, LANES), F32)],
        compiler_params=_cparams(1),
        name="router",
    )(x2d, ln_g, wr, br)


def _dispatch_kernel(dest_ref, zb_ref, x_ref, xs_ref, zero_scr, rows_scr, sem, *, tm, tb, n_zb, n_tiles):
    i = pl.program_id(0)

    @pl.when(i == 0)
    def _():
        zero_scr[...] = jnp.zeros_like(zero_scr)

        def zero_copy(n):
            return pltpu.make_async_copy(zero_scr, xs_ref.at[pl.ds(zb_ref[n] * tb, tb)], sem.at[2])

        def zero_issue(n, carry):
            @pl.when(zb_ref[n] >= 0)
            def _():
                zero_copy(n).start()

            return carry

        def zero_wait(n, carry):
            @pl.when(zb_ref[n] >= 0)
            def _():
                zero_copy(n).wait()

            return carry

        lax.fori_loop(0, n_zb, zero_issue, 0)
        lax.fori_loop(0, n_zb, zero_wait, 0)

    buf_now = lax.rem(i, 2)
    rows_scr[buf_now] = x_ref[...].reshape(rows_scr.shape[1:])

    def row_copy(tile, t, slot):
        buf = lax.rem(tile, 2)
        return pltpu.make_async_copy(
            rows_scr.at[buf, pl.ds(t, 1)],
            xs_ref.at[pl.ds(dest_ref[(tile * tm + t) * TOP_K + slot], 1)], sem.at[buf])

    def issue(tt, carry):
        for r in range(ROW_UNROLL):
            for slot in range(TOP_K):
                row_copy(i, tt * ROW_UNROLL + r, slot).start(priority=slot)
        return carry

    def drain(tile):
        buf = lax.rem(tile, 2)
        for _ in range(TOP_K):
            pltpu.make_async_copy(rows_scr.at[buf], rows_scr.at[buf], sem.at[buf]).wait()

    lax.fori_loop(0, tm // ROW_UNROLL, issue, 0)

    @pl.when(i > 0)
    def _():
        drain(i - 1)

    @pl.when(i == n_tiles - 1)
    def _():
        drain(i)


def _dispatch(dest, zero_blocks, x2d, *, tm, tb, n_rows):
    N, D = x2d.shape
    return pl.pallas_call(
        functools.partial(_dispatch_kernel, tm=tm, tb=tb, n_zb=zero_blocks.shape[0], n_tiles=N // tm),
        grid_spec=pltpu.PrefetchScalarGridSpec(
            num_scalar_prefetch=2,
            grid=(N // tm,),
            in_specs=[pl.BlockSpec((tm, D), lambda i, d, z: (i, 0))],
            out_specs=pl.BlockSpec(memory_space=pl.ANY),
            scratch_shapes=[pltpu.VMEM((tb, D // LANES, LANES), F32), pltpu.VMEM((2, tm, D // LANES, LANES), F32),
                            pltpu.SemaphoreType.DMA((3,))],
        ),
        out_shape=jax.ShapeDtypeStruct((n_rows, D // LANES, LANES), F32),
        compiler_params=_cparams(1),
        name="moe_dispatch",
    )(dest, zero_blocks, x2d)


def _ffn_kernel(be_ref, nb_ref, xs_ref, lng_ref, wg_ref, wu_ref, wd_ref, y_ref, wg_scr, wu_scr, wd_scr):
    i = pl.program_id(0)
    used = i < nb_ref[0]
    tb = xs_ref.shape[0]

    @pl.when(jnp.logical_or(i == 0, be_ref[i] != be_ref[jnp.maximum(i - 1, 0)]))
    def _():
        wg_scr[...] = wg_ref[0].astype(BF16)
        wu_scr[...] = wu_ref[0].astype(BF16)
        wd_scr[...] = wd_ref[0].astype(BF16)

    @pl.when(used)
    def _():
        h = _rms(xs_ref[...].reshape(tb, -1), lng_ref[...]).astype(BF16)
        g = jnp.dot(h, wg_scr[...], preferred_element_type=F32)
        u = jnp.dot(h, wu_scr[...], preferred_element_type=F32)
        y = jnp.dot((_silu(g) * u).astype(BF16), wd_scr[...], preferred_element_type=F32)
        y_ref[...] = y.reshape(y_ref.shape)

    @pl.when(jnp.logical_not(used))
    def _():
        y_ref[...] = jnp.zeros_like(y_ref)


def _ffn(blk_e, nb_used, xs, ln_g, wg, wu, wd, layer, *, tb):
    P, S, _ = xs.shape
    D = S * LANES
    nb = P // tb
    DE = wg.shape[3]
    return pl.pallas_call(
        _ffn_kernel,
        grid_spec=pltpu.PrefetchScalarGridSpec(
            num_scalar_prefetch=2,
            grid=(nb,),
            in_specs=[
                pl.BlockSpec((tb, S, LANES), lambda i, be, nbu: (jnp.minimum(i, nbu[0] - 1), 0, 0)),
                pl.BlockSpec((1, D), lambda i, be, nbu: (0, 0)),
                pl.BlockSpec((None, 1, D, DE), lambda i, be, nbu: (layer, be[i], 0, 0)),
                pl.BlockSpec((None, 1, D, DE), lambda i, be, nbu: (layer, be[i], 0, 0)),
                pl.BlockSpec((None, 1, DE, D), lambda i, be, nbu: (layer, be[i], 0, 0)),
            ],
            out_specs=pl.BlockSpec((tb, S, LANES), lambda i, be, nbu: (i, 0, 0)),
            scratch_shapes=[pltpu.VMEM((D, DE), BF16), pltpu.VMEM((D, DE), BF16), pltpu.VMEM((DE, D), BF16)],
        ),
        out_shape=jax.ShapeDtypeStruct((P, S, LANES), F32),
        compiler_params=_cparams(1),
        name="moe_ffn",
    )(blk_e, nb_used, xs, ln_g, wg, wu, wd)


def _combine_kernel(dest_ref, x_ref, info_ref, yb_ref, y_ref, g_scr, sem, *, tm, n_tiles):
    i = pl.program_id(0)

    def row_copy(tile, t, slot):
        buf = lax.rem(tile, 2)
        return pltpu.make_async_copy(
            yb_ref.at[pl.ds(dest_ref[(tile * tm + t) * TOP_K + slot], 1)],
            g_scr.at[buf, slot, pl.ds(t, 1)], sem.at[buf])

    def issue_tile(tile):
        def body(tt, carry):
            for r in range(ROW_UNROLL):
                for slot in range(TOP_K):
                    row_copy(tile, tt * ROW_UNROLL + r, slot).start(priority=slot)
            return carry

        lax.fori_loop(0, tm // ROW_UNROLL, body, 0)

    @pl.when(i == 0)
    def _():
        issue_tile(i)

    @pl.when(i + 1 < n_tiles)
    def _():
        issue_tile(i + 1)

    buf = lax.rem(i, 2)
    pltpu.make_async_copy(g_scr.at[buf], g_scr.at[buf], sem.at[buf]).wait()
    info = info_ref[...]
    lane = lax.broadcasted_iota(jnp.int32, info.shape, 1)
    w1 = jnp.sum(jnp.where(lane == 2, info, 0.0), axis=-1, keepdims=True)
    w2 = jnp.sum(jnp.where(lane == 3, info, 0.0), axis=-1, keepdims=True)
    g1 = g_scr[buf, 0].reshape(x_ref.shape)
    g2 = g_scr[buf, 1].reshape(x_ref.shape)
    y_ref[...] = x_ref[...] + (w1 * g1 + w2 * g2)


def _combine(dest, x2d, info, yb, *, tm):
    N, D = x2d.shape
    return pl.pallas_call(
        functools.partial(_combine_kernel, tm=tm, n_tiles=N // tm),
        grid_spec=pltpu.PrefetchScalarGridSpec(
            num_scalar_prefetch=1,
            grid=(N // tm,),
            in_specs=[
                pl.BlockSpec((tm, D), lambda i, d: (i, 0)),
                pl.BlockSpec((tm, LANES), lambda i, d: (i, 0)),
                pl.BlockSpec(memory_space=pl.ANY),
            ],
            out_specs=pl.BlockSpec((tm, D), lambda i, d: (i, 0)),
            scratch_shapes=[pltpu.VMEM((2, TOP_K, tm, D // LANES, LANES), F32), pltpu.SemaphoreType.DMA((2,))],
        ),
        out_shape=jax.ShapeDtypeStruct((N, D), F32),
        compiler_params=_cparams(1),
        name="moe_combine",
    )(dest, x2d, info, yb)


def _moe(x2d, ln2_g, wr, br, wg, wu, wd, layer, *, tm):
    N, D = x2d.shape
    tb = MOE_ROWS if N * TOP_K >= N_EXPERTS * MOE_ROWS else MOE_ROWS_SMALL
    info, counts = _router(x2d, ln2_g, wr, br, tm=tm)
    counts = counts[0, N_GROUPS:N_GROUPS + N_EXPERTS].astype(jnp.int32)
    pcounts = (counts + tb - 1) // tb * tb
    pend = jnp.cumsum(pcounts)
    pstart = pend - pcounts
    e = info[:, 0:TOP_K].astype(jnp.int32)
    rank = info[:, 4:4 + TOP_K].astype(jnp.int32)
    experts = jnp.arange(N_EXPERTS, dtype=jnp.int32)
    dest = (jnp.sum(jnp.where(e[..., None] == experts, pstart, 0), axis=-1) + rank).reshape(-1)
    nb = -(-(N * TOP_K) // tb) + N_EXPERTS
    P = nb * tb
    blocks = jnp.arange(nb, dtype=jnp.int32)
    blk_e = jnp.minimum(jnp.sum((pend[None, :] <= blocks[:, None] * tb).astype(jnp.int32), axis=1), N_EXPERTS - 1)
    nb_used = (pend[-1] // tb).astype(jnp.int32).reshape(1)
    zero_blocks = jnp.concatenate([jnp.where(counts % tb != 0, pend // tb - 1, -1),
                                   jnp.where(blocks >= nb_used[0], blocks, -1)]).astype(jnp.int32)
    xs = _dispatch(dest, zero_blocks, x2d, tm=tm, tb=tb, n_rows=P)
    yb = _ffn(blk_e, nb_used, xs, ln2_g, wg, wu, wd, layer, tb=tb)
    return _combine(dest, x2d, info, yb, tm=tm)


def _rows8(x):
    return jnp.broadcast_to(x, (8, x.shape[1]))


def _row_hdot(x, m):
    return _hdot(_rows8(x), m)[0:1]


def _bf_round(x):
    return x.astype(BF16).astype(F32)


def _sample_attn_kernel(z_ref, c0, c1, c2, qg_ref, kg_ref, oa_ref, kv_ref):
    W = SWA_GW
    scale = SWA_DIM ** -0.5
    z = z_ref[0]
    sub = lax.broadcasted_iota(jnp.int32, (SWA_HEADS, W), 0)
    lane = lax.broadcasted_iota(jnp.int32, (SWA_HEADS, W), 1)
    own = lane // SWA_DIM == sub

    def heads(row):
        return jnp.where(own, jnp.broadcast_to(row, (SWA_HEADS, W)), 0.0)

    def head_sum(row):
        return jnp.sum(heads(row), axis=-1, keepdims=True)

    def spread(col):
        return jnp.sum(jnp.where(own, col, 0.0), axis=0, keepdims=True)

    def headnorm(zz, g):
        return zz * spread(lax.rsqrt(head_sum(zz * zz) * (1.0 / SWA_DIM) + EPS)) * g

    outs, lses = [], []
    for gi, (c_ref, (win, dil)) in enumerate(zip((c0, c1, c2), SWA_CONFIGS)):
        q = headnorm(z[:, gi * W:(gi + 1) * W], qg_ref[gi:gi + 1, :])
        k = headnorm(z[:, 3 * W + gi * W:3 * W + (gi + 1) * W], kg_ref[gi:gi + 1, :])
        v = z[:, 6 * W + gi * W:6 * W + (gi + 1) * W]
        kv_ref[0, :, 2 * gi * W:(2 * gi + 1) * W] = k
        kv_ref[0, :, (2 * gi + 1) * W:(2 * gi + 2) * W] = v
        kc = c_ref[0].reshape(W, win).astype(BF16)
        vc = c_ref[1].reshape(W, win).astype(BF16)
        s_c = jnp.dot(heads(q).astype(BF16), kc, preferred_element_type=F32) * scale
        row = lax.broadcasted_iota(jnp.int32, s_c.shape, 1)
        s_c = jnp.where(row % dil == 0, s_c, -jnp.inf)
        s_n = head_sum(_bf_round(k) * _bf_round(q)) * scale
        m = jnp.maximum(jnp.max(s_c, axis=-1, keepdims=True), s_n)
        p_c = jnp.exp(s_c - m)
        p_n = jnp.exp(s_n - m)
        den = jnp.sum(p_c, axis=-1, keepdims=True) + p_n
        pv = lax.dot_general(p_c.astype(BF16), vc, (((1,), (1,)), ((), ())), preferred_element_type=F32)
        num = jnp.sum(jnp.where(own, pv, 0.0), axis=0, keepdims=True) + spread(_bf_round(p_n)) * _bf_round(v)
        outs.append(num / spread(den))
        lses.append(m + jnp.log(den))
    mm = jnp.maximum(jnp.maximum(lses[0], lses[1]), lses[2])
    es = [jnp.exp(l - mm) for l in lses]
    tot = es[0] + es[1] + es[2]
    oa_ref[0] = sum(spread(_bf_round(e / tot)) * _bf_round(o) for e, o in zip(es, outs))


def _sample_attn(z3, caches, layer, qg, kg):
    Bs = z3.shape[0]
    W = SWA_GW
    cviews, cspecs = [], []
    for (win, dil), c in zip(SWA_CONFIGS, caches):
        assert c.shape[2] == win
        cviews.append(jnp.transpose(c, (0, 1, 3, 4, 5, 2)))
        cspecs.append(pl.BlockSpec((None, None, 2, SWA_HEADS, SWA_DIM, win), lambda b: (layer, b, 0, 0, 0, 0)))
    full = lambda a: pl.BlockSpec(a.shape, lambda b: (0,) * a.ndim)
    return pl.pallas_call(
        _sample_attn_kernel,
        grid=(Bs,),
        in_specs=[pl.BlockSpec((1, 1, 9 * W), lambda b: (b, 0, 0))] + cspecs + [full(qg), full(kg)],
        out_specs=[pl.BlockSpec((1, 1, W), lambda b: (b, 0, 0)), pl.BlockSpec((1, 1, 6 * W), lambda b: (b, 0, 0))],
        out_shape=[jax.ShapeDtypeStruct((Bs, 1, W), F32), jax.ShapeDtypeStruct((Bs, 1, 6 * W), F32)],
        compiler_params=_cparams(1),
        name="sample_attn",
    )(z3, *cviews, qg, kg)


def _sample_dn_kernel(raw_ref, cs_ref, cw_ref, ba_ref, par_ref, s_ref, e_ref, etb_ref, etg_ref, o_ref, so_ref):
    E, ETB, ETG = e_ref[...], etb_ref[...], etg_ref[...]
    width = DN_HEADS * DN_DK
    conv = cw_ref[DN_CONV - 1:DN_CONV, :] * raw_ref[0]
    for t in range(DN_CONV - 1):
        conv = conv + cw_ref[t:t + 1, :] * cs_ref[0, t:t + 1, :]
    act = _silu(conv)

    def l2(zz):
        return zz * _row_hdot(lax.rsqrt(_row_hdot(zz * zz, E) + EPS), ETB)

    qn = l2(act[:, 0:width]) * (DN_DK ** -0.5)
    kn = l2(act[:, width:2 * width])
    vn = act[:, 2 * width:3 * width]
    ba = ba_ref[0]
    beta = _row_hdot(_sigmoid(ba), ETB)
    eg = jnp.exp(_row_hdot(par_ref[0:1, :] * _softplus(ba + par_ref[1:2, :]), ETG))
    row0 = lax.broadcasted_iota(jnp.int32, (8, LANES), 0) == 0
    for h in range(DN_HEADS):
        sl = slice(h * LANES, (h + 1) * LANES)
        S = s_ref[0, h]
        q, k, v, b, e = qn[:, sl], kn[:, sl], vn[:, sl], beta[:, sl], eg[:, sl]
        Sr = _bf_round(S)
        v_new = v * b - _row_hdot(_bf_round(k * b * e), Sr)
        a = jnp.sum(q * k, axis=-1, keepdims=True)
        o_ref[0, :, sl] = _row_hdot(_bf_round(q * e), Sr) + a * v_new
        k8 = jnp.where(row0, _rows8(k), 0.0)
        upd = lax.dot_general(k8, _rows8(v_new), (((0,), (0,)), ((), ())), preferred_element_type=F32, precision=HI)
        so_ref[0, h] = S * e + upd


def _sample_dn(raw3, conv_state, s0, layer, conv_w, ba3, par, e_mat, etb, etg):
    Bs, _, C = raw3.shape
    H = DN_HEADS
    full = lambda a: pl.BlockSpec(a.shape, lambda b: (0,) * a.ndim)
    return pl.pallas_call(
        _sample_dn_kernel,
        grid=(Bs,),
        in_specs=[pl.BlockSpec((1, 1, C), lambda b: (b, 0, 0)),
                  pl.BlockSpec((None, 1, DN_CONV - 1, C), lambda b: (layer, b, 0, 0)),
                  full(conv_w),
                  pl.BlockSpec((1, 1, LANES), lambda b: (b, 0, 0)),
                  full(par),
                  pl.BlockSpec((None, 1, H, DN_DK, LANES), lambda b: (layer, b, 0, 0, 0)),
                  full(e_mat), full(etb), full(etg)],
        out_specs=[pl.BlockSpec((1, 1, H * LANES), lambda b: (b, 0, 0)),
                   pl.BlockSpec((1, H, DN_DK, LANES), lambda b: (b, 0, 0, 0))],
        out_shape=[jax.ShapeDtypeStruct((Bs, 1, H * LANES), F32), jax.ShapeDtypeStruct(s0.shape[1:], F32)],
        compiler_params=_cparams(1),
        name="sample_dn",
    )(raw3, conv_state, conv_w, ba3, par, s0, e_mat, etb, etg)


def _sample_out_kernel(x_ref, oa_ref, od_ref, gates_ref, dng_ref, wa_ref, wb_ref, wo_ref, y_ref):
    y_ref[...] = _gated_mix(oa_ref[...], od_ref[...], gates_ref[...], dng_ref[...], wa_ref[...], wb_ref[...],
                            wo_ref[...], x_ref[...], _bdot)


def _sample_out(x2d, oa, od, gates, dng, wa, wb, wo):
    args = (x2d, oa, od, gates, dng, wa, wb, wo)
    return pl.pallas_call(
        _sample_out_kernel,
        grid=(1,),
        in_specs=[pl.BlockSpec(a.shape, lambda i: (0, 0)) for a in args],
        out_specs=pl.BlockSpec(x2d.shape, lambda i: (0, 0)),
        out_shape=jax.ShapeDtypeStruct(x2d.shape, F32),
        compiler_params=_cparams(1),
        name="sample_out",
    )(*args)


def _head_indicator(width, head):
    c = jnp.arange(width)[:, None] // head
    return (c == jnp.arange(LANES)[None, :]).astype(F32)


def _prep_layer(l, ln1_g, w_in, q_norm_g, k_norm_g, dn_conv_w, dn_a_log, dn_dt_bias, dn_norm_g, w_out_a, w_out_b,
                w_o, ln2_g, w_rg, b_rg, w_re, b_re, w_e_gate, w_e_up, w_e_down):
    D = w_in.shape[1]
    a_w = 3 * 3 * SWA_GW
    dn_w = DN_HEADS * 3 * DN_DK
    hv = DN_HEADS * DN_DK
    w = w_in[l]
    splits = dict(att=w[:, :a_w], dn=w[:, a_w:a_w + dn_w],
                  ba=jnp.pad(w[:, a_w + dn_w:a_w + dn_w + 2 * DN_HEADS], ((0, 0), (0, LANES - 2 * DN_HEADS))),
                  gate=w[:, a_w + dn_w + 2 * DN_HEADS:])
    assert splits["gate"].shape[1] == hv + 2 * D
    tile_heads = lambda g: jnp.broadcast_to(g[:, None, :], (len(SWA_CONFIGS), SWA_HEADS, SWA_DIM)).reshape(len(SWA_CONFIGS), SWA_GW)
    qg, kg = tile_heads(q_norm_g[l]), tile_heads(k_norm_g[l])
    idx = jnp.arange(MXU) // SWA_DIM
    n_g = len(SWA_CONFIGS)
    par = jnp.zeros((2, LANES), F32)
    par = par.at[0, DN_HEADS:2 * DN_HEADS].set(-jnp.exp(dn_a_log[l].astype(F32)))
    par = par.at[1, DN_HEADS:2 * DN_HEADS].set(dn_dt_bias[l].astype(F32))
    wr = jnp.pad(jnp.concatenate([w_rg[l], w_re[l]], axis=1), ((0, 0), (0, LANES - N_GROUPS - N_EXPERTS)))
    br = jnp.pad(jnp.concatenate([b_rg[l], b_re[l]]), (0, LANES - N_GROUPS - N_EXPERTS)).reshape(1, LANES)
    e8 = _head_indicator(hv, DN_DK)
    return dict(
        bf16={k: v.astype(BF16) for k, v in splits.items()},
        ln1=ln1_g[l].reshape(1, D), ln2=ln2_g[l].reshape(1, D),
        qg=qg, kg=kg,
        w_grp=[jnp.concatenate([w[:, s * n_g * SWA_GW + g * SWA_GW:s * n_g * SWA_GW + (g + 1) * SWA_GW]
                                for s in range(3)], axis=1).astype(BF16) for g in range(n_g)],
        ng_grp=[jnp.concatenate([qg[g], kg[g]]).reshape(1, 1, 2 * SWA_GW) for g in range(n_g)],
        bd=((idx[:, None] == idx[None, :]).astype(F32) / SWA_DIM).astype(BF16),
        conv_w=dn_conv_w[l], par=par, dng=dn_norm_g[l].reshape(1, DN_DK),
        wa=w_out_a[l].astype(BF16), wb=w_out_b[l].astype(BF16), wo=w_o[l].astype(BF16), wr=wr.astype(BF16), br=br,
        wg=w_e_gate, wu=w_e_up, wd=w_e_down, layer=l,
        e_dn=e8, etb=e8.T, etg=jnp.roll(e8, DN_HEADS, axis=1).T,
        e_att=_head_indicator(SWA_GW, SWA_DIM).T.astype(BF16),
    )


def _layer_prompt(x, p):
    B, L, D = x.shape
    N = B * L
    x2d = x.reshape(N, D)
    bw = p["bf16"]
    pks, tails = [], []
    for g, (win, dil) in enumerate(SWA_CONFIGS):
        assert L >= win
        pk, tail = _proj_attn(x, p["ln1"], p["w_grp"][g], p["ng_grp"][g], p["bd"], dil=dil, tmr=min(512, L // dil))
        pks.append(pk)
        tails.append(tail)
    tmp = min(1024, N)
    qd, kd, vd, raw_tail = _proj_dn(x, p["ln1"], bw["dn"], p["conv_w"], tm=min(512, L))
    gates = _proj_plain(x2d, p["ln1"], bw["gate"], tm=tmp, tn=1536, out_dtype=BF16, name="proj_gate")
    ba = _proj_plain(x2d, p["ln1"], bw["ba"], tm=tmp, tn=LANES, out_dtype=F32, name="proj_ba")
    os_, ls_ = [], []
    for pk in pks:
        d, M = pk.shape[1], pk.shape[2]
        o, lse = _attn(pk.reshape(B * d, M, pk.shape[3]), tq=min(256, M))
        os_.append(o.reshape(B, d, M, SWA_GW))
        ls_.append(lse.reshape(B, d, M, LANES))
    gb, gt_rows = _dn_gates(ba.reshape(B, L, LANES), p["par"], tl=min(256, L))
    u, w, qdec, kdec, a, gt = _dn_intra(qd, kd, vd, gb, gt_rows, tl=min(2048, L))
    od, s_new = _dn_scan(u, w, qdec, kdec, a, gt, jnp.zeros((B, DN_HEADS, DN_DK, LANES), F32), tl=min(1024, L))
    x2 = _out_proj(x2d, os_, ls_, od.reshape(N, -1), gates, p["dng"], p["wa"], p["wb"], p["wo"], p["e_att"],
                   B=B, L=L, tm=min(512, L))
    y = _moe(x2, p["ln2"], p["wr"], p["br"], p["wg"], p["wu"], p["wd"], p["layer"], tm=256)
    return y.reshape(B, L, D), tails, raw_tail[:, 8 - (DN_CONV - 1):], s_new


def _layer_sample(x, caches, conv_state, s0, layer, p):
    Bs, T, D = x.shape
    assert T == 1
    x2d = x.reshape(Bs, D)
    bw = p["bf16"]
    proj = functools.partial(_proj_plain, x2d, p["ln1"], tm=Bs, out_dtype=F32)
    z_att = proj(bw["att"], tn=1536, name="sproj_att")
    raw = proj(bw["dn"], tn=1536, name="sproj_dn")
    gates = proj(bw["gate"], tn=1536, name="sproj_gate")
    ba = proj(bw["ba"], tn=LANES, name="sproj_ba")
    oa, kv = _sample_attn(z_att.reshape(Bs, 1, -1), caches, layer, p["qg"], p["kg"])
    raw3 = raw.reshape(Bs, 1, -1)
    od, s_new = _sample_dn(raw3, conv_state, s0, layer, p["conv_w"], ba.reshape(Bs, 1, LANES), p["par"],
                           p["e_dn"], p["etb"], p["etg"])
    x2 = _sample_out(x2d, oa.reshape(Bs, -1), od.reshape(Bs, -1), gates, p["dng"], p["wa"], p["wb"], p["wo"])
    y = _moe(x2, p["ln2"], p["wr"], p["br"], p["wg"], p["wu"], p["wd"], p["layer"], tm=Bs)
    W2 = 2 * SWA_GW
    kvs = [kv[:, :, g * W2:(g + 1) * W2].reshape(Bs, 1, 2, SWA_HEADS, SWA_DIM) for g in range(len(SWA_CONFIGS))]
    new_conv = jnp.concatenate([conv_state[layer][:, 1:], raw3], axis=1)
    return y.reshape(Bs, 1, D), kvs, new_conv, s_new


def kernel(x_prompt, x_sample, cache_swa0_kv, cache_swa1_kv, cache_swa2_kv, state_dn_conv, state_dn_S, ln1_g, w_in,
           q_norm_g, k_norm_g, dn_conv_w, dn_a_log, dn_dt_bias, dn_norm_g, w_out_a, w_out_b, w_o, ln2_g, w_rg, b_rg,
           w_re, b_re, w_e_gate, w_e_up, w_e_down):
    yp, ys = x_prompt, x_sample
    outs = [[] for _ in range(10)]
    for l in range(w_in.shape[0]):
        p = _prep_layer(l, ln1_g, w_in, q_norm_g, k_norm_g, dn_conv_w, dn_a_log, dn_dt_bias, dn_norm_g, w_out_a,
                        w_out_b, w_o, ln2_g, w_rg, b_rg, w_re, b_re, w_e_gate, w_e_up, w_e_down)
        yp, pkv, pconv, ps = _layer_prompt(yp, p)
        ys, skv, sconv, ss = _layer_sample(ys, (cache_swa0_kv, cache_swa1_kv, cache_swa2_kv), state_dn_conv,
                                           state_dn_S, l, p)
        for lst, val in zip(outs, (*pkv, pconv, ps, *skv, sconv, ss)):
            lst.append(val)
    return (yp, ys, *(jnp.stack(o) for o in outs))
```

```python
import functools

import jax
import jax.numpy as jnp
from jax import lax
from jax.experimental import pallas as pl
from jax.experimental.pallas import tpu as pltpu

F32 = jnp.float32
BF16 = jnp.bfloat16
HI = lax.Precision.HIGHEST
EPS = 1e-6

SWA_CONFIGS = ((128, 1), (512, 4), (2048, 16))
SWA_HEADS = 8
SWA_DIM = 64
SWA_GW = SWA_HEADS * SWA_DIM
SWA_SPAN = 128
DN_HEADS = 8
DN_DK = 128
DN_CONV = 4
DN_CHUNK = 64
N_GROUPS = 4
PER_GROUP = 8
N_EXPERTS = N_GROUPS * PER_GROUP
TOP_K = 2

VMEM_LIMIT_BYTES = 56 * 1024 * 1024
LANES = 128
MXU = 256
MOE_ROWS = 512
MOE_ROWS_SMALL = 128
ROW_UNROLL = 8


def _cparams(n_axes):
    return pltpu.CompilerParams(
        dimension_semantics=("arbitrary",) * n_axes, vmem_limit_bytes=VMEM_LIMIT_BYTES
    )


def _rms(x, g):
    return x * lax.rsqrt(jnp.mean(x * x, axis=-1, keepdims=True) + EPS) * g


def _bdot(a, b):
    return jnp.dot(a.astype(BF16), b.astype(BF16), preferred_element_type=F32)


def _hdot(a, b):
    return jnp.dot(a, b, preferred_element_type=F32, precision=HI)


def _sigmoid(x):
    return 1.0 / (1.0 + jnp.exp(-x))


def _silu(x):
    return x * _sigmoid(x)


def _softplus(x):
    return jnp.maximum(x, 0.0) + jnp.log1p(jnp.exp(-jnp.abs(x)))


def _norm_permute_kernel(x_ref, lng_ref, *refs, tm, dils):
    outs, h_scr = refs[:-1], refs[-1]
    h = _rms(x_ref[0], lng_ref[...])
    n_cb = h_scr.shape[0]
    for cb in range(n_cb):
        h_scr[cb] = h[:, cb * LANES:(cb + 1) * LANES]
    for o_ref, d in zip(outs, dils):
        for cb in range(n_cb):
            for r in range(d):
                src = h_scr[cb] if d == 1 else h_scr[cb, pl.ds(r, tm // d, stride=d), :]
                o_ref[0, r, :, cb * LANES:(cb + 1) * LANES] = src.astype(BF16)


def _norm_permute(x, ln_g, dils, *, tm):
    B, L, D = x.shape
    assert L % tm == 0 and all(tm % (16 * d) == 0 for d in dils)
    return pl.pallas_call(
        functools.partial(_norm_permute_kernel, tm=tm, dils=dils),
        grid=(B, L // tm),
        in_specs=[pl.BlockSpec((1, tm, D), lambda b, i: (b, i, 0)), pl.BlockSpec((1, D), lambda b, i: (0, 0))],
        out_specs=[pl.BlockSpec((1, d, tm // d, D), lambda b, i: (b, 0, i, 0)) for d in dils],
        out_shape=[jax.ShapeDtypeStruct((B, d, L // d, D), BF16) for d in dils],
        scratch_shapes=[pltpu.VMEM((D // LANES, tm, LANES), F32)],
        compiler_params=_cparams(2),
        name="norm_permute",
    )(x, ln_g)


def _proj_attn_kernel(h_ref, w_ref, ng_ref, bd_ref, p_ref, t_ref, *, n_tiles):
    rows = h_ref.shape[2]
    z = jnp.dot(h_ref[0, 0], w_ref[...], preferred_element_type=F32)
    kv = []
    for c in range(0, 3 * SWA_GW, MXU):
        zc = z[:, c:c + MXU]
        if c < 2 * SWA_GW:
            ms = jnp.dot((zc * zc).astype(BF16), bd_ref[...], preferred_element_type=F32)
            zc = zc * lax.rsqrt(ms + EPS) * ng_ref[0, :, c:c + MXU]
        p_ref[0, 0, :, c:c + MXU] = zc.astype(BF16)
        if c >= SWA_GW:
            kv.append(zc[rows - SWA_SPAN:rows, :])

    @pl.when(pl.program_id(2) == n_tiles - 1)
    def _():
        per = SWA_GW // MXU
        for s in range(2):
            zr = jnp.concatenate(kv[s * per:(s + 1) * per], axis=-1)
            t_ref[0, :, s] = zr.reshape(SWA_SPAN, SWA_HEADS, SWA_DIM)


def _proj_attn(hg, w_g, ng_g, bd, *, tmr):
    B, dil, M, D = hg.shape
    assert M % tmr == 0 and tmr >= SWA_SPAN
    nt = M // tmr
    W3 = 3 * SWA_GW
    keep = SWA_SPAN * dil
    p, t = pl.pallas_call(
        functools.partial(_proj_attn_kernel, n_tiles=nt),
        grid=(B, dil, nt),
        in_specs=[
            pl.BlockSpec((1, 1, tmr, D), lambda b, r, i: (b, r, i, 0)),
            pl.BlockSpec((D, W3), lambda b, r, i: (0, 0)),
            pl.BlockSpec((1, 1, 2 * SWA_GW), lambda b, r, i: (0, 0, 0)),
            pl.BlockSpec((MXU, MXU), lambda b, r, i: (0, 0)),
        ],
        out_specs=[
            pl.BlockSpec((1, 1, tmr, W3), lambda b, r, i: (b, r, i, 0)),
            pl.BlockSpec((1, SWA_SPAN, None, 2, SWA_HEADS, SWA_DIM), lambda b, r, i: (b, 0, r, 0, 0, 0)),
        ],
        out_shape=[
            jax.ShapeDtypeStruct((B, dil, M, W3), BF16),
            jax.ShapeDtypeStruct((B, SWA_SPAN, dil, 2, SWA_HEADS, SWA_DIM), F32),
        ],
        compiler_params=_cparams(3),
        name="proj_attn",
    )(hg, w_g, ng_g, bd)
    return p, t.reshape(B, keep, 2, SWA_HEADS, SWA_DIM)


def _proj_plain_kernel(x_ref, lng_ref, w_ref, o_ref, h_scr):
    @pl.when(pl.program_id(1) == 0)
    def _():
        h_scr[...] = _rms(x_ref[...], lng_ref[...]).astype(BF16)

    o_ref[...] = jnp.dot(h_scr[...], w_ref[...], preferred_element_type=F32).astype(o_ref.dtype)


def _proj_plain(x2d, ln_g, w, *, tm, tn, out_dtype, name="proj_plain"):
    N, D = x2d.shape
    C = w.shape[1]
    assert N % tm == 0 and C % tn == 0
    return pl.pallas_call(
        _proj_plain_kernel,
        grid=(N // tm, C // tn),
        in_specs=[
            pl.BlockSpec((tm, D), lambda i, j: (i, 0)),
            pl.BlockSpec((1, D), lambda i, j: (0, 0)),
            pl.BlockSpec((D, tn), lambda i, j: (0, j)),
        ],
        out_specs=pl.BlockSpec((tm, tn), lambda i, j: (i, j)),
        out_shape=jax.ShapeDtypeStruct((N, C), out_dtype),
        scratch_shapes=[pltpu.VMEM((tm, D), BF16)],
        compiler_params=_cparams(2),
        name=name,
    )(x2d, ln_g, w)


def _matmul_kernel(h_ref, w_ref, o_ref):
    o_ref[...] = jnp.dot(h_ref[...], w_ref[...], preferred_element_type=F32).astype(o_ref.dtype)


def _matmul(h2d, w, *, tm, tn, out_dtype, name):
    N, D = h2d.shape
    C = w.shape[1]
    assert N % tm == 0 and C % tn == 0
    return pl.pallas_call(
        _matmul_kernel,
        grid=(N // tm, C // tn),
        in_specs=[pl.BlockSpec((tm, D), lambda i, j: (i, 0)), pl.BlockSpec((D, tn), lambda i, j: (0, j))],
        out_specs=pl.BlockSpec((tm, tn), lambda i, j: (i, j)),
        out_shape=jax.ShapeDtypeStruct((N, C), out_dtype),
        compiler_params=_cparams(2),
        name=name,
    )(h2d, w)


def _attn_kernel(q_ref, kc_ref, vc_ref, kp_ref, vp_ref, o_ref, lse_ref, kk_scr, vv_scr, *, tq):
    i = pl.program_id(1)
    blk = SWA_SPAN
    kk_scr[0:blk, :] = kp_ref[0]
    kk_scr[blk:blk + tq, :] = kc_ref[0]
    vv_scr[0:blk, :] = vp_ref[0]
    vv_scr[blk:blk + tq, :] = vc_ref[0]
    qi = lax.broadcasted_iota(jnp.int32, (blk, 2 * blk), 0)
    ki = lax.broadcasted_iota(jnp.int32, (blk, 2 * blk), 1)
    dist = blk + qi - ki
    band = (dist >= 0) & (dist <= SWA_SPAN)
    band_first = band & ((ki >= blk) | (i > 0))
    lo = lax.broadcasted_iota(jnp.int32, (blk, LANES), 1) < SWA_DIM
    zero = jnp.zeros((blk, LANES), BF16)
    lane = lax.broadcasted_iota(jnp.int32, (blk, LANES), 1)
    for jb in range(tq // blk):
        mask = band_first if jb == 0 else band
        rows = slice(jb * blk, (jb + 1) * blk)
        lse_all = jnp.zeros((blk, LANES), F32)
        for hp in range(SWA_GW // LANES):
            cs = slice(hp * LANES, (hp + 1) * LANES)
            qb = q_ref[0, rows, cs]
            kk = kk_scr[jb * blk:(jb + 2) * blk, cs]
            vv = vv_scr[jb * blk:(jb + 2) * blk, cs]
            res_o = []
            for hh in range(2):
                qm = jnp.where(lo if hh == 0 else jnp.logical_not(lo), qb, zero)
                s = lax.dot_general(qm, kk, (((1,), (1,)), ((), ())), preferred_element_type=F32)
                s = jnp.where(mask, s * (SWA_DIM ** -0.5), -jnp.inf)
                m = jnp.max(s, axis=-1, keepdims=True)
                p = jnp.exp(s - m)
                den = jnp.sum(p, axis=-1, keepdims=True)
                pv = jnp.dot(p.astype(BF16), vv, preferred_element_type=F32)
                res_o.append(pv / den)
                lse_all = jnp.where(lane == 2 * hp + hh, m + jnp.log(den), lse_all)
            o_ref[0, rows, cs] = jnp.where(lo, res_o[0], res_o[1]).astype(BF16)
        lse_ref[0, rows, :] = lse_all


def _attn(p, *, tq):
    S, M, _ = p.shape
    assert M % tq == 0 and tq % SWA_SPAN == 0
    nb = tq // SWA_SPAN
    return pl.pallas_call(
        functools.partial(_attn_kernel, tq=tq),
        grid=(S, M // tq),
        in_specs=[
            pl.BlockSpec((1, tq, SWA_GW), lambda s, i: (s, i, 0)),
            pl.BlockSpec((1, tq, SWA_GW), lambda s, i: (s, i, 1)),
            pl.BlockSpec((1, tq, SWA_GW), lambda s, i: (s, i, 2)),
            pl.BlockSpec((1, SWA_SPAN, SWA_GW), lambda s, i: (s, jnp.maximum(i * nb - 1, 0), 1)),
            pl.BlockSpec((1, SWA_SPAN, SWA_GW), lambda s, i: (s, jnp.maximum(i * nb - 1, 0), 2)),
        ],
        out_specs=[
            pl.BlockSpec((1, tq, SWA_GW), lambda s, i: (s, i, 0)),
            pl.BlockSpec((1, tq, LANES), lambda s, i: (s, i, 0)),
        ],
        out_shape=[
            jax.ShapeDtypeStruct((S, M, SWA_GW), BF16),
            jax.ShapeDtypeStruct((S, M, LANES), F32),
        ],
        scratch_shapes=[
            pltpu.VMEM((SWA_SPAN + tq, SWA_GW), BF16),
            pltpu.VMEM((SWA_SPAN + tq, SWA_GW), BF16),
        ],
        compiler_params=_cparams(2),
        name="swa_attn",
    )(p, p, p, p, p)


def _proj_dn_kernel(h_ref, w_ref, cw_ref, q_ref, k_ref, v_ref, tail_ref, z_scr, carry_scr, *, tm, n_ct):
    i = pl.program_id(1)
    j = pl.program_id(2)
    nh = DN_HEADS
    ncb = z_scr.shape[1] // LANES

    z_scr[0:8, :] = jnp.where(i == 0, 0.0, carry_scr[j])
    z_scr[8:8 + tm, :] = jnp.dot(h_ref[0], w_ref[...], preferred_element_type=F32)
    last = z_scr[tm:tm + 8, :]
    carry_scr[j] = last
    tn = z_scr.shape[1]
    outs = (q_ref, k_ref, v_ref)
    for jj in range(n_ct):

        @pl.when(j == jj)
        def _(jj=jj):
            tail_ref[0, :, jj * tn:(jj + 1) * tn] = last
            for cbl in range(ncb):
                cs = slice(cbl * LANES, (cbl + 1) * LANES)
                part, h = divmod(jj * ncb + cbl, nh)
                xe = z_scr[:, cs]
                acc = cw_ref[0:1, cs] * xe
                for t in range(1, DN_CONV):
                    acc = cw_ref[t:t + 1, cs] * xe + pltpu.roll(acc, 1, axis=0)
                act = _silu(acc[8:])
                if part < 2:
                    act = act * lax.rsqrt(jnp.sum(act * act, axis=-1, keepdims=True) + EPS)
                if part == 0:
                    act = act * (DN_DK ** -0.5)
                outs[part][0, :, h * LANES:(h + 1) * LANES] = act.astype(BF16)


def _proj_dn(h, w_dn, conv_w, *, tm):
    B, L, D = h.shape
    C = w_dn.shape[1]
    width = DN_HEADS * DN_DK
    n_ct = 2
    tn = C // n_ct
    assert L % tm == 0 and C == 3 * width and tn % LANES == 0
    qkv = pl.BlockSpec((1, tm, width), lambda b, i, j: (b, i, 0))
    return pl.pallas_call(
        functools.partial(_proj_dn_kernel, tm=tm, n_ct=n_ct),
        grid=(B, L // tm, n_ct),
        in_specs=[
            pl.BlockSpec((1, tm, D), lambda b, i, j: (b, i, 0)),
            pl.BlockSpec((D, tn), lambda b, i, j: (0, j)),
            pl.BlockSpec((DN_CONV, tn), lambda b, i, j: (0, j)),
        ],
        out_specs=[qkv, qkv, qkv, pl.BlockSpec((1, 8, C), lambda b, i, j: (b, 0, 0))],
        out_shape=[jax.ShapeDtypeStruct((B, L, width), BF16)] * 3 + [jax.ShapeDtypeStruct((B, 8, C), F32)],
        scratch_shapes=[pltpu.VMEM((8 + tm, tn), F32), pltpu.VMEM((n_ct, 8, tn), F32)],
        compiler_params=_cparams(3),
        name="proj_dn",
    )(h, w_dn, conv_w)


def _gates_kernel(ba_ref, par_ref, g_ref, gt_ref, *, tl):
    nh = DN_HEADS
    ba = ba_ref[0]
    lane = lax.broadcasted_iota(jnp.int32, (tl, LANES), 1)
    g = par_ref[0:1, :] * _softplus(ba + par_ref[1:2, :])
    ri = lax.broadcasted_iota(jnp.int32, (tl, tl), 0)
    ci = lax.broadcasted_iota(jnp.int32, (tl, tl), 1)
    tri = jnp.where((ri // DN_CHUNK == ci // DN_CHUNK) & (ci <= ri), 1.0, 0.0).astype(BF16)
    g_hi = g.astype(BF16)
    r1 = g - g_hi.astype(F32)
    g_mid = r1.astype(BF16)
    g_lo = (r1 - g_mid.astype(F32)).astype(BF16)
    gc = sum(jnp.dot(tri, piece, preferred_element_type=F32) for piece in (g_hi, g_mid, g_lo))
    g_ref[0] = jnp.where(lane < nh, _sigmoid(ba), gc)
    gt_ref[0] = jnp.transpose(gc)[nh:2 * nh, :]


def _dn_gates(ba, par, *, tl):
    B, L, _ = ba.shape
    assert L % tl == 0 and tl % DN_CHUNK == 0
    return pl.pallas_call(
        functools.partial(_gates_kernel, tl=tl),
        grid=(B, L // tl),
        in_specs=[pl.BlockSpec((1, tl, LANES), lambda b, i: (b, i, 0)), pl.BlockSpec((2, LANES), lambda b, i: (0, 0))],
        out_specs=[pl.BlockSpec((1, tl, LANES), lambda b, i: (b, i, 0)),
                   pl.BlockSpec((1, DN_HEADS, tl), lambda b, i: (b, 0, i))],
        out_shape=[jax.ShapeDtypeStruct((B, L, LANES), F32), jax.ShapeDtypeStruct((B, DN_HEADS, L), F32)],
        compiler_params=_cparams(2),
        name="dn_gates",
    )(ba, par)


def _intra_kernel(q_ref, k_ref, v_ref, g_ref, gt_in_ref, u_ref, w_ref, qd_ref, kd_ref, a_ref, gt_ref, *, tl):
    h = pl.program_id(1)
    C = DN_CHUNK
    lane = lax.broadcasted_iota(jnp.int32, (C, LANES), 1)
    ri = lax.broadcasted_iota(jnp.int32, (C, C), 0)
    ci = lax.broadcasted_iota(jnp.int32, (C, C), 1)
    eye = jnp.where(ri == ci, 1.0, 0.0).astype(F32)
    nt_dot = lambda a, b: lax.dot_general(a.astype(BF16), b.astype(BF16), (((1,), (1,)), ((), ())),
                                          preferred_element_type=F32)
    rows = [slice(c * C, (c + 1) * C) for c in range(tl // C)]
    gv = [g_ref[0, r, :] for r in rows]
    q = [q_ref[0, r, :].astype(F32) for r in rows]
    k = [k_ref[0, r, :].astype(F32) for r in rows]
    v = [v_ref[0, r, :].astype(F32) for r in rows]
    beta = [jnp.sum(jnp.where(lane == h, x, 0.0), axis=-1, keepdims=True) for x in gv]
    gc = [jnp.sum(jnp.where(lane == h + DN_HEADS, x, 0.0), axis=-1, keepdims=True) for x in gv]
    gc_row = gt_in_ref[0, pl.ds(h, 1), :]
    decay = [jnp.exp(jnp.where(ri >= ci, a - gc_row[:, r], -jnp.inf)) for a, r in zip(gc, rows)]
    kb = [a * b for a, b in zip(k, beta)]
    x = [-jnp.where(ri > ci, nt_dot(a, b) * d, 0.0) for a, b, d in zip(kb, k, decay)]
    t = [eye + a for a in x]
    for _ in range(5):
        x = [_bdot(a, a) for a in x]
        t = [a + _bdot(a, b) for a, b in zip(t, x)]
    eg = [jnp.exp(a) for a in gc]
    glast = [a[C - 1:C, :] for a in gc]
    u = [_bdot(a, b * c) for a, b, c in zip(t, v, beta)]
    w = [_bdot(a, b * c) for a, b, c in zip(t, kb, eg)]
    qk = [nt_dot(a, b) for a, b in zip(q, k)]
    for c, r in enumerate(rows):
        u_ref[0, 0, r, :] = u[c]
        w_ref[0, 0, r, :] = w[c].astype(BF16)
        a_ref[0, 0, r, :] = (qk[c] * decay[c]).astype(BF16)
        qd_ref[0, 0, r, :] = (q[c] * eg[c]).astype(BF16)
        kd_ref[0, 0, r, :] = (k[c] * jnp.exp(glast[c] - gc[c])).astype(BF16)
        gt_ref[0, 0, c:c + 1, :] = jnp.broadcast_to(jnp.exp(glast[c]), (1, LANES))


def _dn_intra(q, k, v, g, gt_rows, *, tl):
    B, L, _ = q.shape
    H, C = DN_HEADS, DN_CHUNK
    assert L % tl == 0 and (tl // C) % 8 == 0
    qkv_spec = pl.BlockSpec((1, tl, LANES), lambda b, h, i: (b, i, h))
    hl = lambda w: pl.BlockSpec((1, 1, tl, w), lambda b, h, i: (b, h, i, 0))
    return pl.pallas_call(
        functools.partial(_intra_kernel, tl=tl),
        grid=(B, H, L // tl),
        in_specs=[qkv_spec, qkv_spec, qkv_spec, pl.BlockSpec((1, tl, LANES), lambda b, h, i: (b, i, 0)),
                  pl.BlockSpec((1, H, tl), lambda b, h, i: (b, 0, i))],
        out_specs=[hl(LANES), hl(LANES), hl(LANES), hl(LANES), hl(C),
                   pl.BlockSpec((1, 1, tl // C, LANES), lambda b, h, i: (b, h, i, 0))],
        out_shape=[
            jax.ShapeDtypeStruct((B, H, L, LANES), F32),
            jax.ShapeDtypeStruct((B, H, L, LANES), BF16),
            jax.ShapeDtypeStruct((B, H, L, LANES), BF16),
            jax.ShapeDtypeStruct((B, H, L, LANES), BF16),
            jax.ShapeDtypeStruct((B, H, L, C), BF16),
            jax.ShapeDtypeStruct((B, H, L // C, LANES), F32),
        ],
        compiler_params=_cparams(3),
        name="dn_intra",
    )(q, k, v, g, gt_rows)


def _scan_kernel(u_ref, w_ref, qd_ref, kd_ref, a_ref, gt_ref, s0_ref, o_ref, s_ref, *, n_chunks):
    C = DN_CHUNK
    H = s_ref.shape[1]

    @pl.when(pl.program_id(1) == 0)
    def _():
        s_ref[...] = s0_ref[...]

    def body(c, carry):
        rows = pl.ds(pl.multiple_of(c * C, C), C)
        S = [s_ref[0, h] for h in range(H)]
        Sb = [x.astype(BF16) for x in S]
        v_new = [u_ref[0, h, rows, :] - jnp.dot(w_ref[0, h, rows, :], Sb[h], preferred_element_type=F32)
                 for h in range(H)]
        vb = [x.astype(BF16) for x in v_new]
        o = [jnp.dot(qd_ref[0, h, rows, :], Sb[h], preferred_element_type=F32)
             + jnp.dot(a_ref[0, h, rows, :], vb[h], preferred_element_type=F32) for h in range(H)]
        upd = [lax.dot_general(kd_ref[0, h, rows, :], vb[h], (((0,), (0,)), ((), ())), preferred_element_type=F32)
               for h in range(H)]
        for h in range(H):
            o_ref[0, rows, h * LANES:(h + 1) * LANES] = o[h]
            s_ref[0, h] = S[h] * gt_ref[0, h, pl.ds(c, 1), :] + upd[h]
        return carry

    lax.fori_loop(0, n_chunks, body, 0)


def _dn_scan(u, w, qd, kd, a, gt, s0, *, tl):
    B, H, L, _ = u.shape
    C = DN_CHUNK
    assert L % tl == 0 and (tl // C) % 8 == 0
    hs = lambda wd: pl.BlockSpec((1, H, tl, wd), lambda b, i: (b, 0, i, 0))
    s_spec = pl.BlockSpec((1, H, DN_DK, LANES), lambda b, i: (b, 0, 0, 0))
    return pl.pallas_call(
        functools.partial(_scan_kernel, n_chunks=tl // C),
        grid=(B, L // tl),
        in_specs=[hs(LANES), hs(LANES), hs(LANES), hs(LANES), hs(C),
                  pl.BlockSpec((1, H, tl // C, LANES), lambda b, i: (b, 0, i, 0)), s_spec],
        out_specs=[pl.BlockSpec((1, tl, H * LANES), lambda b, i: (b, i, 0)), s_spec],
        out_shape=[jax.ShapeDtypeStruct((B, L, H * LANES), F32),
                   jax.ShapeDtypeStruct((B, H, DN_DK, LANES), F32)],
        compiler_params=_cparams(2),
        name="dn_scan",
    )(u, w, qd, kd, a, gt, s0)


def _gated_mix(o_a, od, gates, dng, wa, wb, wo, x, dot):
    width = DN_HEADS * DN_DK
    parts = []
    for h in range(DN_HEADS):
        blk = od[:, h * LANES:(h + 1) * LANES]
        parts.append(blk * lax.rsqrt(jnp.mean(blk * blk, axis=-1, keepdims=True) + EPS) * dng)
    odn = jnp.concatenate(parts, axis=-1) * _silu(gates[:, 0:width].astype(F32))
    ya = dot(o_a, wa)
    yb = dot(odn, wb)
    mix = _sigmoid(gates[:, width:2 * width].astype(F32)) * ya + _sigmoid(gates[:, 2 * width:].astype(F32)) * yb
    return x + dot(mix, wo)


def _out_kernel(x_ref, o0, o1, o2, l0, l1, l2, od_ref, gates_ref, dng_ref, wa_ref, wb_ref, wo_ref, e_ref, y_ref,
                so0, so1, so2, sl0, sl1, sl2, *, tm, dils):
    o_refs, l_refs = (o0, o1, o2), (l0, l1, l2)
    so, sl = (so0, so1, so2), (sl0, sl1, sl2)
    for gi, d in enumerate(dils):
        for r in range(d):
            dst = slice(None) if d == 1 else pl.ds(r, tm // d, stride=d)
            sl[gi][dst, :] = l_refs[gi][0, r]
            for cb in range(SWA_GW // LANES):
                so[gi][cb, dst, :] = o_refs[gi][0, r, :, cb * LANES:(cb + 1) * LANES].astype(F32)
    ls = [s[...] for s in sl]
    m = jnp.maximum(jnp.maximum(ls[0], ls[1]), ls[2])
    es = [jnp.exp(l - m) for l in ls]
    tot = es[0] + es[1] + es[2]
    alphas = [jnp.dot((e / tot).astype(BF16), e_ref[...], preferred_element_type=F32) for e in es]
    parts = []
    for cb in range(SWA_GW // LANES):
        cs = slice(cb * LANES, (cb + 1) * LANES)
        parts.append(alphas[0][:, cs] * so[0][cb] + alphas[1][:, cs] * so[1][cb] + alphas[2][:, cs] * so[2][cb])
    o_a = jnp.concatenate(parts, axis=-1)
    y_ref[...] = _gated_mix(o_a, od_ref[...], gates_ref[...], dng_ref[...], wa_ref[...], wb_ref[...],
                            wo_ref[...], x_ref[...], _bdot)


def _out_proj(x2d, os_, ls_, od2d, gates, dng, wa, wb, wo, e_att, *, B, L, tm):
    N, D = x2d.shape
    nt = L // tm
    dils = tuple(d for _, d in SWA_CONFIGS)
    grp = lambda d, w: pl.BlockSpec((1, d, tm // d, w), lambda i: (i // nt, 0, i % nt, 0))
    row = lambda w: pl.BlockSpec((tm, w), lambda i: (i, 0))
    full = lambda a: pl.BlockSpec(a.shape, lambda i: (0, 0))
    return pl.pallas_call(
        functools.partial(_out_kernel, tm=tm, dils=dils),
        grid=(N // tm,),
        in_specs=[row(D)] + [grp(d, SWA_GW) for d in dils] + [grp(d, LANES) for d in dils]
        + [row(od2d.shape[1]), row(gates.shape[1]), full(dng), full(wa), full(wb), full(wo), full(e_att)],
        out_specs=row(D),
        out_shape=jax.ShapeDtypeStruct((N, D), F32),
        scratch_shapes=[pltpu.VMEM((SWA_GW // LANES, tm, LANES), F32)] * 3 + [pltpu.VMEM((tm, LANES), F32)] * 3,
        compiler_params=_cparams(1),
        name="out_proj",
    )(x2d, *os_, *ls_, od2d, gates, dng, wa, wb, wo, e_att)


def _router_kernel(x_ref, lng_ref, wr_ref, br_ref, info_ref, cnt_ref, base_scr, *, tm):
    i = pl.program_id(0)

    @pl.when(i == 0)
    def _():
        base_scr[...] = jnp.zeros_like(base_scr)

    h = _rms(x_ref[...], lng_ref[...])
    lg = _bdot(h, wr_ref[...]) + br_ref[...]
    lane = lax.broadcasted_iota(jnp.int32, (tm, LANES), 1)
    big = jnp.int32(1 << 20)
    ninf = -jnp.inf

    def argmax_lane(vals):
        mx = jnp.max(vals, axis=-1, keepdims=True)
        idx = jnp.min(jnp.where(vals == mx, lane, big), axis=-1, keepdims=True)
        return mx, idx

    lgm = jnp.where(lane < N_GROUPS, lg, ninf)
    mg, gsel = argmax_lane(lgm)
    pg = 1.0 / jnp.sum(jnp.exp(lgm - mg), axis=-1, keepdims=True)
    start = N_GROUPS + gsel * PER_GROUP
    le = jnp.where((lane >= start) & (lane < start + PER_GROUP), lg, ninf)
    m1, i1 = argmax_lane(le)
    m2, i2 = argmax_lane(jnp.where(lane == i1, ninf, le))
    e21 = jnp.exp(m2 - m1)
    w1 = pg / (1.0 + e21)
    w2 = pg * e21 / (1.0 + e21)
    oh = jnp.where(lane == i1, 1.0, 0.0) + jnp.where(lane == i2, 1.0, 0.0)
    ri = lax.broadcasted_iota(jnp.int32, (tm, tm), 0)
    ci = lax.broadcasted_iota(jnp.int32, (tm, tm), 1)
    strict = jnp.where(ci < ri, 1.0, 0.0).astype(BF16)
    pref = jnp.dot(strict, oh.astype(BF16), preferred_element_type=F32) + base_scr[...]
    r1 = jnp.sum(jnp.where(lane == i1, pref, 0.0), axis=-1, keepdims=True)
    r2 = jnp.sum(jnp.where(lane == i2, pref, 0.0), axis=-1, keepdims=True)
    base_scr[...] = base_scr[...] + jnp.sum(oh, axis=0, keepdims=True)
    cnt_ref[...] = base_scr[...]
    off = jnp.float32(N_GROUPS)
    info = jnp.where(lane == 0, i1.astype(F32) - off, 0.0)
    info = jnp.where(lane == 1, i2.astype(F32) - off, info)
    info = jnp.where(lane == 2, w1, info)
    info = jnp.where(lane == 3, w2, info)
    info = jnp.where(lane == 4, r1, info)
    info = jnp.where(lane == 5, r2, info)
    info_ref[...] = info


def _router(x2d, ln_g, wr, br, *, tm):
    N, D = x2d.shape
    assert N % tm == 0
    return pl.pallas_call(
        functools.partial(_router_kernel, tm=tm),
        grid=(N // tm,),
        in_specs=[
            pl.BlockSpec((tm, D), lambda i: (i, 0)),
            pl.BlockSpec((1, D), lambda i: (0, 0)),
            pl.BlockSpec((D, LANES), lambda i: (0, 0)),
            pl.BlockSpec((1, LANES), lambda i: (0, 0)),
        ],
        out_specs=[pl.BlockSpec((tm, LANES), lambda i: (i, 0)), pl.BlockSpec((1, LANES), lambda i: (0, 0))],
        out_shape=[jax.ShapeDtypeStruct((N, LANES), F32), jax.ShapeDtypeStruct((1, LANES), F32)],
        scratch_shapes=[pltpu.VMEM((1, LANES), F32)],
        compiler_params=_cparams(1),
        name="router",
    )(x2d, ln_g, wr, br)


def _dispatch_kernel(dest_ref, zb_ref, x_ref, xs_ref, zero_scr, rows_scr, sem, *, tm, tb, n_zb, n_tiles):
    i = pl.program_id(0)

    @pl.when(i == 0)
    def _():
        zero_scr[...] = jnp.zeros_like(zero_scr)

        def zero_copy(n):
            return pltpu.make_async_copy(zero_scr, xs_ref.at[pl.ds(zb_ref[n] * tb, tb)], sem.at[2])

        def zero_issue(n, carry):
            @pl.when(zb_ref[n] >= 0)
            def _():
                zero_copy(n).start()

            return carry

        def zero_wait(n, carry):
            @pl.when(zb_ref[n] >= 0)
            def _():
                zero_copy(n).wait()

            return carry

        lax.fori_loop(0, n_zb, zero_issue, 0)
        lax.fori_loop(0, n_zb, zero_wait, 0)

    buf_now = lax.rem(i, 2)
    rows_scr[buf_now] = x_ref[...].reshape(rows_scr.shape[1:])

    def row_copy(tile, t, slot):
        buf = lax.rem(tile, 2)
        return pltpu.make_async_copy(
            rows_scr.at[buf, pl.ds(t, 1)],
            xs_ref.at[pl.ds(dest_ref[(tile * tm + t) * TOP_K + slot], 1)], sem.at[buf])

    def issue(tt, carry):
        for r in range(ROW_UNROLL):
            for slot in range(TOP_K):
                row_copy(i, tt * ROW_UNROLL + r, slot).start(priority=slot)
        return carry

    def drain(tile):
        buf = lax.rem(tile, 2)
        for _ in range(TOP_K):
            pltpu.make_async_copy(rows_scr.at[buf], rows_scr.at[buf], sem.at[buf]).wait()

    lax.fori_loop(0, tm // ROW_UNROLL, issue, 0)

    @pl.when(i > 0)
    def _():
        drain(i - 1)

    @pl.when(i == n_tiles - 1)
    def _():
        drain(i)


def _dispatch(dest, zero_blocks, x2d, *, tm, tb, n_rows):
    N, D = x2d.shape
    return pl.pallas_call(
        functools.partial(_dispatch_kernel, tm=tm, tb=tb, n_zb=zero_blocks.shape[0], n_tiles=N // tm),
        grid_spec=pltpu.PrefetchScalarGridSpec(
            num_scalar_prefetch=2,
            grid=(N // tm,),
            in_specs=[pl.BlockSpec((tm, D), lambda i, d, z: (i, 0))],
            out_specs=pl.BlockSpec(memory_space=pl.ANY),
            scratch_shapes=[pltpu.VMEM((tb, D // LANES, LANES), F32), pltpu.VMEM((2, tm, D // LANES, LANES), F32),
                            pltpu.SemaphoreType.DMA((3,))],
        ),
        out_shape=jax.ShapeDtypeStruct((n_rows, D // LANES, LANES), F32),
        compiler_params=_cparams(1),
        name="moe_dispatch",
    )(dest, zero_blocks, x2d)


def _ffn_kernel(be_ref, nb_ref, xs_ref, lng_ref, wg_ref, wu_ref, wd_ref, y_ref, wg_scr, wu_scr, wd_scr):
    i = pl.program_id(0)
    used = i < nb_ref[0]
    tb = xs_ref.shape[0]

    @pl.when(jnp.logical_or(i == 0, be_ref[i] != be_ref[jnp.maximum(i - 1, 0)]))
    def _():
        wg_scr[...] = wg_ref[0].astype(BF16)
        wu_scr[...] = wu_ref[0].astype(BF16)
        wd_scr[...] = wd_ref[0].astype(BF16)

    @pl.when(used)
    def _():
        h = _rms(xs_ref[...].reshape(tb, -1), lng_ref[...]).astype(BF16)
        g = jnp.dot(h, wg_scr[...], preferred_element_type=F32)
        u = jnp.dot(h, wu_scr[...], preferred_element_type=F32)
        y = jnp.dot((_silu(g) * u).astype(BF16), wd_scr[...], preferred_element_type=F32)
        y_ref[...] = y.reshape(y_ref.shape)

    @pl.when(jnp.logical_not(used))
    def _():
        y_ref[...] = jnp.zeros_like(y_ref)


def _ffn(blk_e, nb_used, xs, ln_g, wg, wu, wd, layer, *, tb):
    P, S, _ = xs.shape
    D = S * LANES
    nb = P // tb
    DE = wg.shape[3]
    return pl.pallas_call(
        _ffn_kernel,
        grid_spec=pltpu.PrefetchScalarGridSpec(
            num_scalar_prefetch=2,
            grid=(nb,),
            in_specs=[
                pl.BlockSpec((tb, S, LANES), lambda i, be, nbu: (jnp.minimum(i, nbu[0] - 1), 0, 0)),
                pl.BlockSpec((1, D), lambda i, be, nbu: (0, 0)),
                pl.BlockSpec((None, 1, D, DE), lambda i, be, nbu: (layer, be[i], 0, 0)),
                pl.BlockSpec((None, 1, D, DE), lambda i, be, nbu: (layer, be[i], 0, 0)),
                pl.BlockSpec((None, 1, DE, D), lambda i, be, nbu: (layer, be[i], 0, 0)),
            ],
            out_specs=pl.BlockSpec((tb, S, LANES), lambda i, be, nbu: (i, 0, 0)),
            scratch_shapes=[pltpu.VMEM((D, DE), BF16), pltpu.VMEM((D, DE), BF16), pltpu.VMEM((DE, D), BF16)],
        ),
        out_shape=jax.ShapeDtypeStruct((P, S, LANES), F32),
        compiler_params=_cparams(1),
        name="moe_ffn",
    )(blk_e, nb_used, xs, ln_g, wg, wu, wd)


def _combine_kernel(dest_ref, x_ref, info_ref, yb_ref, y_ref, g_scr, sem, *, tm, n_tiles):
    i = pl.program_id(0)

    def row_copy(tile, t, slot):
        buf = lax.rem(tile, 2)
        return pltpu.make_async_copy(
            yb_ref.at[pl.ds(dest_ref[(tile * tm + t) * TOP_K + slot], 1)],
            g_scr.at[buf, slot, pl.ds(t, 1)], sem.at[buf])

    def issue_tile(tile):
        def body(tt, carry):
            for r in range(ROW_UNROLL):
                for slot in range(TOP_K):
                    row_copy(tile, tt * ROW_UNROLL + r, slot).start(priority=slot)
            return carry

        lax.fori_loop(0, tm // ROW_UNROLL, body, 0)

    @pl.when(i == 0)
    def _():
        issue_tile(i)

    @pl.when(i + 1 < n_tiles)
    def _():
        issue_tile(i + 1)

    buf = lax.rem(i, 2)
    pltpu.make_async_copy(g_scr.at[buf], g_scr.at[buf], sem.at[buf]).wait()
    info = info_ref[...]
    lane = lax.broadcasted_iota(jnp.int32, info.shape, 1)
    w1 = jnp.sum(jnp.where(lane == 2, info, 0.0), axis=-1, keepdims=True)
    w2 = jnp.sum(jnp.where(lane == 3, info, 0.0), axis=-1, keepdims=True)
    g1 = g_scr[buf, 0].reshape(x_ref.shape)
    g2 = g_scr[buf, 1].reshape(x_ref.shape)
    y_ref[...] = x_ref[...] + (w1 * g1 + w2 * g2)


def _combine(dest, x2d, info, yb, *, tm):
    N, D = x2d.shape
    return pl.pallas_call(
        functools.partial(_combine_kernel, tm=tm, n_tiles=N // tm),
        grid_spec=pltpu.PrefetchScalarGridSpec(
            num_scalar_prefetch=1,
            grid=(N // tm,),
            in_specs=[
                pl.BlockSpec((tm, D), lambda i, d: (i, 0)),
                pl.BlockSpec((tm, LANES), lambda i, d: (i, 0)),
                pl.BlockSpec(memory_space=pl.ANY),
            ],
            out_specs=pl.BlockSpec((tm, D), lambda i, d: (i, 0)),
            scratch_shapes=[pltpu.VMEM((2, TOP_K, tm, D // LANES, LANES), F32), pltpu.SemaphoreType.DMA((2,))],
        ),
        out_shape=jax.ShapeDtypeStruct((N, D), F32),
        compiler_params=_cparams(1),
        name="moe_combine",
    )(dest, x2d, info, yb)


def _moe(x2d, ln2_g, wr, br, wg, wu, wd, layer, *, tm):
    N, D = x2d.shape
    tb = MOE_ROWS if N * TOP_K >= N_EXPERTS * MOE_ROWS else MOE_ROWS_SMALL
    info, counts = _router(x2d, ln2_g, wr, br, tm=tm)
    counts = counts[0, N_GROUPS:N_GROUPS + N_EXPERTS].astype(jnp.int32)
    pcounts = (counts + tb - 1) // tb * tb
    pend = jnp.cumsum(pcounts)
    pstart = pend - pcounts
    e = info[:, 0:TOP_K].astype(jnp.int32)
    rank = info[:, 4:4 + TOP_K].astype(jnp.int32)
    experts = jnp.arange(N_EXPERTS, dtype=jnp.int32)
    dest = (jnp.sum(jnp.where(e[..., None] == experts, pstart, 0), axis=-1) + rank).reshape(-1)
    nb = -(-(N * TOP_K) // tb) + N_EXPERTS
    P = nb * tb
    blocks = jnp.arange(nb, dtype=jnp.int32)
    blk_e = jnp.minimum(jnp.sum((pend[None, :] <= blocks[:, None] * tb).astype(jnp.int32), axis=1), N_EXPERTS - 1)
    nb_used = (pend[-1] // tb).astype(jnp.int32).reshape(1)
    zero_blocks = jnp.concatenate([jnp.where(counts % tb != 0, pend // tb - 1, -1),
                                   jnp.where(blocks >= nb_used[0], blocks, -1)]).astype(jnp.int32)
    xs = _dispatch(dest, zero_blocks, x2d, tm=tm, tb=tb, n_rows=P)
    yb = _ffn(blk_e, nb_used, xs, ln2_g, wg, wu, wd, layer, tb=tb)
    return _combine(dest, x2d, info, yb, tm=tm)


def _rows8(x):
    return jnp.broadcast_to(x, (8, x.shape[1]))


def _row_hdot(x, m):
    return _hdot(_rows8(x), m)[0:1]


def _bf_round(x):
    return x.astype(BF16).astype(F32)


def _sample_attn_kernel(z_ref, c0, c1, c2, qg_ref, kg_ref, oa_ref, kv_ref):
    W = SWA_GW
    scale = SWA_DIM ** -0.5
    z = z_ref[0]
    sub = lax.broadcasted_iota(jnp.int32, (SWA_HEADS, W), 0)
    lane = lax.broadcasted_iota(jnp.int32, (SWA_HEADS, W), 1)
    own = lane // SWA_DIM == sub

    def heads(row):
        return jnp.where(own, jnp.broadcast_to(row, (SWA_HEADS, W)), 0.0)

    def head_sum(row):
        return jnp.sum(heads(row), axis=-1, keepdims=True)

    def spread(col):
        return jnp.sum(jnp.where(own, col, 0.0), axis=0, keepdims=True)

    def headnorm(zz, g):
        return zz * spread(lax.rsqrt(head_sum(zz * zz) * (1.0 / SWA_DIM) + EPS)) * g

    outs, lses = [], []
    for gi, (c_ref, (win, dil)) in enumerate(zip((c0, c1, c2), SWA_CONFIGS)):
        q = headnorm(z[:, gi * W:(gi + 1) * W], qg_ref[gi:gi + 1, :])
        k = headnorm(z[:, 3 * W + gi * W:3 * W + (gi + 1) * W], kg_ref[gi:gi + 1, :])
        v = z[:, 6 * W + gi * W:6 * W + (gi + 1) * W]
        kv_ref[0, :, 2 * gi * W:(2 * gi + 1) * W] = k
        kv_ref[0, :, (2 * gi + 1) * W:(2 * gi + 2) * W] = v
        kc = c_ref[0].reshape(W, win).astype(BF16)
        vc = c_ref[1].reshape(W, win).astype(BF16)
        s_c = jnp.dot(heads(q).astype(BF16), kc, preferred_element_type=F32) * scale
        row = lax.broadcasted_iota(jnp.int32, s_c.shape, 1)
        s_c = jnp.where(row % dil == 0, s_c, -jnp.inf)
        s_n = head_sum(_bf_round(k) * _bf_round(q)) * scale
        m = jnp.maximum(jnp.max(s_c, axis=-1, keepdims=True), s_n)
        p_c = jnp.exp(s_c - m)
        p_n = jnp.exp(s_n - m)
        den = jnp.sum(p_c, axis=-1, keepdims=True) + p_n
        pv = lax.dot_general(p_c.astype(BF16), vc, (((1,), (1,)), ((), ())), preferred_element_type=F32)
        num = jnp.sum(jnp.where(own, pv, 0.0), axis=0, keepdims=True) + spread(_bf_round(p_n)) * _bf_round(v)
        outs.append(num / spread(den))
        lses.append(m + jnp.log(den))
    mm = jnp.maximum(jnp.maximum(lses[0], lses[1]), lses[2])
    es = [jnp.exp(l - mm) for l in lses]
    tot = es[0] + es[1] + es[2]
    oa_ref[0] = sum(spread(_bf_round(e / tot)) * _bf_round(o) for e, o in zip(es, outs))


def _sample_attn(z3, caches, layer, qg, kg):
    Bs = z3.shape[0]
    W = SWA_GW
    cviews, cspecs = [], []
    for (win, dil), c in zip(SWA_CONFIGS, caches):
        assert c.shape[2] == win
        cviews.append(jnp.transpose(c, (0, 1, 3, 4, 5, 2)))
        cspecs.append(pl.BlockSpec((None, None, 2, SWA_HEADS, SWA_DIM, win), lambda b: (layer, b, 0, 0, 0, 0)))
    full = lambda a: pl.BlockSpec(a.shape, lambda b: (0,) * a.ndim)
    return pl.pallas_call(
        _sample_attn_kernel,
        grid=(Bs,),
        in_specs=[pl.BlockSpec((1, 1, 9 * W), lambda b: (b, 0, 0))] + cspecs + [full(qg), full(kg)],
        out_specs=[pl.BlockSpec((1, 1, W), lambda b: (b, 0, 0)), pl.BlockSpec((1, 1, 6 * W), lambda b: (b, 0, 0))],
        out_shape=[jax.ShapeDtypeStruct((Bs, 1, W), F32), jax.ShapeDtypeStruct((Bs, 1, 6 * W), F32)],
        compiler_params=_cparams(1),
        name="sample_attn",
    )(z3, *cviews, qg, kg)


def _sample_dn_kernel(raw_ref, cs_ref, cw_ref, ba_ref, par_ref, s_ref, e_ref, etb_ref, etg_ref, o_ref, so_ref):
    E, ETB, ETG = e_ref[...], etb_ref[...], etg_ref[...]
    width = DN_HEADS * DN_DK
    conv = cw_ref[DN_CONV - 1:DN_CONV, :] * raw_ref[0]
    for t in range(DN_CONV - 1):
        conv = conv + cw_ref[t:t + 1, :] * cs_ref[0, t:t + 1, :]
    act = _silu(conv)

    def l2(zz):
        return zz * _row_hdot(lax.rsqrt(_row_hdot(zz * zz, E) + EPS), ETB)

    qn = l2(act[:, 0:width]) * (DN_DK ** -0.5)
    kn = l2(act[:, width:2 * width])
    vn = act[:, 2 * width:3 * width]
    ba = ba_ref[0]
    beta = _row_hdot(_sigmoid(ba), ETB)
    eg = jnp.exp(_row_hdot(par_ref[0:1, :] * _softplus(ba + par_ref[1:2, :]), ETG))
    row0 = lax.broadcasted_iota(jnp.int32, (8, LANES), 0) == 0
    for h in range(DN_HEADS):
        sl = slice(h * LANES, (h + 1) * LANES)
        S = s_ref[0, h]
        q, k, v, b, e = qn[:, sl], kn[:, sl], vn[:, sl], beta[:, sl], eg[:, sl]
        Sr = _bf_round(S)
        v_new = v * b - _row_hdot(_bf_round(k * b * e), Sr)
        a = jnp.sum(q * k, axis=-1, keepdims=True)
        o_ref[0, :, sl] = _row_hdot(_bf_round(q * e), Sr) + a * v_new
        k8 = jnp.where(row0, _rows8(k), 0.0)
        upd = lax.dot_general(k8, _rows8(v_new), (((0,), (0,)), ((), ())), preferred_element_type=F32, precision=HI)
        so_ref[0, h] = S * e + upd


def _sample_dn(raw3, conv_state, s0, layer, conv_w, ba3, par, e_mat, etb, etg):
    Bs, _, C = raw3.shape
    H = DN_HEADS
    full = lambda a: pl.BlockSpec(a.shape, lambda b: (0,) * a.ndim)
    return pl.pallas_call(
        _sample_dn_kernel,
        grid=(Bs,),
        in_specs=[pl.BlockSpec((1, 1, C), lambda b: (b, 0, 0)),
                  pl.BlockSpec((None, 1, DN_CONV - 1, C), lambda b: (layer, b, 0, 0)),
                  full(conv_w),
                  pl.BlockSpec((1, 1, LANES), lambda b: (b, 0, 0)),
                  full(par),
                  pl.BlockSpec((None, 1, H, DN_DK, LANES), lambda b: (layer, b, 0, 0, 0)),
                  full(e_mat), full(etb), full(etg)],
        out_specs=[pl.BlockSpec((1, 1, H * LANES), lambda b: (b, 0, 0)),
                   pl.BlockSpec((1, H, DN_DK, LANES), lambda b: (b, 0, 0, 0))],
        out_shape=[jax.ShapeDtypeStruct((Bs, 1, H * LANES), F32), jax.ShapeDtypeStruct(s0.shape[1:], F32)],
        compiler_params=_cparams(1),
        name="sample_dn",
    )(raw3, conv_state, conv_w, ba3, par, s0, e_mat, etb, etg)


def _sample_out_kernel(x_ref, oa_ref, od_ref, gates_ref, dng_ref, wa_ref, wb_ref, wo_ref, y_ref):
    y_ref[...] = _gated_mix(oa_ref[...], od_ref[...], gates_ref[...], dng_ref[...], wa_ref[...], wb_ref[...],
                            wo_ref[...], x_ref[...], _bdot)


def _sample_out(x2d, oa, od, gates, dng, wa, wb, wo):
    args = (x2d, oa, od, gates, dng, wa, wb, wo)
    return pl.pallas_call(
        _sample_out_kernel,
        grid=(1,),
        in_specs=[pl.BlockSpec(a.shape, lambda i: (0, 0)) for a in args],
        out_specs=pl.BlockSpec(x2d.shape, lambda i: (0, 0)),
        out_shape=jax.ShapeDtypeStruct(x2d.shape, F32),
        compiler_params=_cparams(1),
        name="sample_out",
    )(*args)


def _head_indicator(width, head):
    c = jnp.arange(width)[:, None] // head
    return (c == jnp.arange(LANES)[None, :]).astype(F32)


def _prep_layer(l, ln1_g, w_in, q_norm_g, k_norm_g, dn_conv_w, dn_a_log, dn_dt_bias, dn_norm_g, w_out_a, w_out_b,
                w_o, ln2_g, w_rg, b_rg, w_re, b_re, w_e_gate, w_e_up, w_e_down):
    D = w_in.shape[1]
    a_w = 3 * 3 * SWA_GW
    dn_w = DN_HEADS * 3 * DN_DK
    hv = DN_HEADS * DN_DK
    w = w_in[l]
    splits = dict(att=w[:, :a_w], dn=w[:, a_w:a_w + dn_w],
                  ba=jnp.pad(w[:, a_w + dn_w:a_w + dn_w + 2 * DN_HEADS], ((0, 0), (0, LANES - 2 * DN_HEADS))),
                  gate=w[:, a_w + dn_w + 2 * DN_HEADS:])
    assert splits["gate"].shape[1] == hv + 2 * D
    tile_heads = lambda g: jnp.broadcast_to(g[:, None, :], (len(SWA_CONFIGS), SWA_HEADS, SWA_DIM)).reshape(len(SWA_CONFIGS), SWA_GW)
    qg, kg = tile_heads(q_norm_g[l]), tile_heads(k_norm_g[l])
    idx = jnp.arange(MXU) // SWA_DIM
    n_g = len(SWA_CONFIGS)
    par = jnp.zeros((2, LANES), F32)
    par = par.at[0, DN_HEADS:2 * DN_HEADS].set(-jnp.exp(dn_a_log[l].astype(F32)))
    par = par.at[1, DN_HEADS:2 * DN_HEADS].set(dn_dt_bias[l].astype(F32))
    wr = jnp.pad(jnp.concatenate([w_rg[l], w_re[l]], axis=1), ((0, 0), (0, LANES - N_GROUPS - N_EXPERTS)))
    br = jnp.pad(jnp.concatenate([b_rg[l], b_re[l]]), (0, LANES - N_GROUPS - N_EXPERTS)).reshape(1, LANES)
    e8 = _head_indicator(hv, DN_DK)
    return dict(
        bf16={k: v.astype(BF16) for k, v in splits.items()},
        ln1=ln1_g[l].reshape(1, D), ln2=ln2_g[l].reshape(1, D),
        qg=qg, kg=kg,
        w_grp=[jnp.concatenate([w[:, s * n_g * SWA_GW + g * SWA_GW:s * n_g * SWA_GW + (g + 1) * SWA_GW]
                                for s in range(3)], axis=1).astype(BF16) for g in range(n_g)],
        ng_grp=[jnp.concatenate([qg[g], kg[g]]).reshape(1, 1, 2 * SWA_GW) for g in range(n_g)],
        bd=((idx[:, None] == idx[None, :]).astype(F32) / SWA_DIM).astype(BF16),
        conv_w=dn_conv_w[l], par=par, dng=dn_norm_g[l].reshape(1, DN_DK),
        wa=w_out_a[l].astype(BF16), wb=w_out_b[l].astype(BF16), wo=w_o[l].astype(BF16), wr=wr.astype(BF16), br=br,
        wg=w_e_gate, wu=w_e_up, wd=w_e_down, layer=l,
        e_dn=e8, etb=e8.T, etg=jnp.roll(e8, DN_HEADS, axis=1).T,
        e_att=_head_indicator(SWA_GW, SWA_DIM).T.astype(BF16),
    )


def _layer_prompt(x, p):
    B, L, D = x.shape
    N = B * L
    x2d = x.reshape(N, D)
    bw = p["bf16"]
    pks, tails = [], []
    hgs = _norm_permute(x, p["ln1"], tuple(d for _, d in SWA_CONFIGS), tm=min(512, L))
    for g, (win, dil) in enumerate(SWA_CONFIGS):
        assert L >= win
        pk, tail = _proj_attn(hgs[g], p["w_grp"][g], p["ng_grp"][g], p["bd"], tmr=min(512, L // dil))
        pks.append(pk)
        tails.append(tail)
    tmp = min(1024, N)
    h0 = hgs[0].reshape(B, L, D)
    qd, kd, vd, raw_tail = _proj_dn(h0, bw["dn"], p["conv_w"], tm=min(512, L))
    gates = _matmul(h0.reshape(N, D), bw["gate"], tm=tmp, tn=1536, out_dtype=BF16, name="proj_gate")
    ba = _matmul(h0.reshape(N, D), bw["ba"], tm=tmp, tn=LANES, out_dtype=F32, name="proj_ba")
    os_, ls_ = [], []
    for pk in pks:
        d, M = pk.shape[1], pk.shape[2]
        o, lse = _attn(pk.reshape(B * d, M, pk.shape[3]), tq=min(256, M))
        os_.append(o.reshape(B, d, M, SWA_GW))
        ls_.append(lse.reshape(B, d, M, LANES))
    gb, gt_rows = _dn_gates(ba.reshape(B, L, LANES), p["par"], tl=min(256, L))
    u, w, qdec, kdec, a, gt = _dn_intra(qd, kd, vd, gb, gt_rows, tl=min(2048, L))
    od, s_new = _dn_scan(u, w, qdec, kdec, a, gt, jnp.zeros((B, DN_HEADS, DN_DK, LANES), F32), tl=min(1024, L))
    x2 = _out_proj(x2d, os_, ls_, od.reshape(N, -1), gates, p["dng"], p["wa"], p["wb"], p["wo"], p["e_att"],
                   B=B, L=L, tm=min(512, L))
    y = _moe(x2, p["ln2"], p["wr"], p["br"], p["wg"], p["wu"], p["wd"], p["layer"], tm=256)
    return y.reshape(B, L, D), tails, raw_tail[:, 8 - (DN_CONV - 1):], s_new


def _layer_sample(x, caches, conv_state, s0, layer, p):
    Bs, T, D = x.shape
    assert T == 1
    x2d = x.reshape(Bs, D)
    bw = p["bf16"]
    proj = functools.partial(_proj_plain, x2d, p["ln1"], tm=Bs, out_dtype=F32)
    z_att = proj(bw["att"], tn=1536, name="sproj_att")
    raw = proj(bw["dn"], tn=1536, name="sproj_dn")
    gates = proj(bw["gate"], tn=1536, name="sproj_gate")
    ba = proj(bw["ba"], tn=LANES, name="sproj_ba")
    oa, kv = _sample_attn(z_att.reshape(Bs, 1, -1), caches, layer, p["qg"], p["kg"])
    raw3 = raw.reshape(Bs, 1, -1)
    od, s_new = _sample_dn(raw3, conv_state, s0, layer, p["conv_w"], ba.reshape(Bs, 1, LANES), p["par"],
                           p["e_dn"], p["etb"], p["etg"])
    x2 = _sample_out(x2d, oa.reshape(Bs, -1), od.reshape(Bs, -1), gates, p["dng"], p["wa"], p["wb"], p["wo"])
    y = _moe(x2, p["ln2"], p["wr"], p["br"], p["wg"], p["wu"], p["wd"], p["layer"], tm=Bs)
    W2 = 2 * SWA_GW
    kvs = [kv[:, :, g * W2:(g + 1) * W2].reshape(Bs, 1, 2, SWA_HEADS, SWA_DIM) for g in range(len(SWA_CONFIGS))]
    new_conv = jnp.concatenate([conv_state[layer][:, 1:], raw3], axis=1)
    return y.reshape(Bs, 1, D), kvs, new_conv, s_new


def kernel(x_prompt, x_sample, cache_swa0_kv, cache_swa1_kv, cache_swa2_kv, state_dn_conv, state_dn_S, ln1_g, w_in,
           q_norm_g, k_norm_g, dn_conv_w, dn_a_log, dn_dt_bias, dn_norm_g, w_out_a, w_out_b, w_o, ln2_g, w_rg, b_rg,
           w_re, b_re, w_e_gate, w_e_up, w_e_down):
    yp, ys = x_prompt, x_sample
    outs = [[] for _ in range(10)]
    for l in range(w_in.shape[0]):
        p = _prep_layer(l, ln1_g, w_in, q_norm_g, k_norm_g, dn_conv_w, dn_a_log, dn_dt_bias, dn_norm_g, w_out_a,
                        w_out_b, w_o, ln2_g, w_rg, b_rg, w_re, b_re, w_e_gate, w_e_up, w_e_down)
        yp, pkv, pconv, ps = _layer_prompt(yp, p)
        ys, skv, sconv, ss = _layer_sample(ys, (cache_swa0_kv, cache_swa1_kv, cache_swa2_kv), state_dn_conv,
                                           state_dn_S, l, p)
        for lst, val in zip(outs, (*pkv, pconv, ps, *skv, sconv, ss)):
            lst.append(val)
    return (yp, ys, *(jnp.stack(o) for o in outs))
```

```python
import functools

import jax
import jax.numpy as jnp
from jax import lax
from jax.experimental import pallas as pl
from jax.experimental.pallas import tpu as pltpu

F32 = jnp.float32
BF16 = jnp.bfloat16
HI = lax.Precision.HIGHEST
EPS = 1e-6

SWA_CONFIGS = ((128, 1), (512, 4), (2048, 16))
SWA_HEADS = 8
SWA_DIM = 64
SWA_GW = SWA_HEADS * SWA_DIM
SWA_SPAN = 128
DN_HEADS = 8
DN_DK = 128
DN_CONV = 4
DN_CHUNK = 64
N_GROUPS = 4
PER_GROUP = 8
N_EXPERTS = N_GROUPS * PER_GROUP
TOP_K = 2

VMEM_LIMIT_BYTES = 56 * 1024 * 1024
LANES = 128
MXU = 256
MOE_ROWS = 512
MOE_ROWS_SMALL = 128
ROW_UNROLL = 8


def _cparams(n_axes):
    return pltpu.CompilerParams(
        dimension_semantics=("arbitrary",) * n_axes, vmem_limit_bytes=VMEM_LIMIT_BYTES
    )


def _rms(x, g):
    return x * lax.rsqrt(jnp.mean(x * x, axis=-1, keepdims=True) + EPS) * g


def _bdot(a, b):
    return jnp.dot(a.astype(BF16), b.astype(BF16), preferred_element_type=F32)


def _hdot(a, b):
    return jnp.dot(a, b, preferred_element_type=F32, precision=HI)


def _sigmoid(x):
    return 0.5 * jnp.tanh(0.5 * x) + 0.5


def _silu(x):
    half = 0.5 * x
    return half * jnp.tanh(half) + half


def _softplus(x):
    return jnp.maximum(x, 0.0) + jnp.log1p(jnp.exp(-jnp.abs(x)))


def _norm_permute_kernel(x_ref, lng_ref, *refs, tm, dils):
    outs, h_scr = refs[:-1], refs[-1]
    h = _rms(x_ref[0], lng_ref[...])
    n_cb = h_scr.shape[0]
    for cb in range(n_cb):
        h_scr[cb] = h[:, cb * LANES:(cb + 1) * LANES]
    for o_ref, d in zip(outs, dils):
        for cb in range(n_cb):
            for r in range(d):
                src = h_scr[cb] if d == 1 else h_scr[cb, pl.ds(r, tm // d, stride=d), :]
                o_ref[0, r, :, cb * LANES:(cb + 1) * LANES] = src.astype(BF16)


def _norm_permute(x, ln_g, dils, *, tm):
    B, L, D = x.shape
    assert L % tm == 0 and all(tm % (16 * d) == 0 for d in dils)
    return pl.pallas_call(
        functools.partial(_norm_permute_kernel, tm=tm, dils=dils),
        grid=(B, L // tm),
        in_specs=[pl.BlockSpec((1, tm, D), lambda b, i: (b, i, 0)), pl.BlockSpec((1, D), lambda b, i: (0, 0))],
        out_specs=[pl.BlockSpec((1, d, tm // d, D), lambda b, i: (b, 0, i, 0)) for d in dils],
        out_shape=[jax.ShapeDtypeStruct((B, d, L // d, D), BF16) for d in dils],
        scratch_shapes=[pltpu.VMEM((D // LANES, tm, LANES), F32)],
        compiler_params=_cparams(2),
        name="norm_permute",
    )(x, ln_g)


def _proj_attn_kernel(h_ref, w_ref, ng_ref, bd_ref, p_ref, t_ref, *, n_tiles):
    rows = h_ref.shape[2]
    z = jnp.dot(h_ref[0, 0], w_ref[...], preferred_element_type=F32)
    kv = []
    for c in range(0, 3 * SWA_GW, MXU):
        zc = z[:, c:c + MXU]
        if c < 2 * SWA_GW:
            ms = jnp.dot((zc * zc).astype(BF16), bd_ref[...], preferred_element_type=F32)
            zc = zc * lax.rsqrt(ms + EPS) * ng_ref[0, :, c:c + MXU]
        p_ref[0, 0, :, c:c + MXU] = zc.astype(BF16)
        if c >= SWA_GW:
            kv.append(zc[rows - SWA_SPAN:rows, :])

    @pl.when(pl.program_id(2) == n_tiles - 1)
    def _():
        per = SWA_GW // MXU
        for s in range(2):
            zr = jnp.concatenate(kv[s * per:(s + 1) * per], axis=-1)
            t_ref[0, :, s] = zr.reshape(SWA_SPAN, SWA_HEADS, SWA_DIM)


def _proj_attn(hg, w_g, ng_g, bd, *, tmr):
    B, dil, M, D = hg.shape
    assert M % tmr == 0 and tmr >= SWA_SPAN
    nt = M // tmr
    W3 = 3 * SWA_GW
    keep = SWA_SPAN * dil
    p, t = pl.pallas_call(
        functools.partial(_proj_attn_kernel, n_tiles=nt),
        grid=(B, dil, nt),
        in_specs=[
            pl.BlockSpec((1, 1, tmr, D), lambda b, r, i: (b, r, i, 0)),
            pl.BlockSpec((D, W3), lambda b, r, i: (0, 0)),
            pl.BlockSpec((1, 1, 2 * SWA_GW), lambda b, r, i: (0, 0, 0)),
            pl.BlockSpec((MXU, MXU), lambda b, r, i: (0, 0)),
        ],
        out_specs=[
            pl.BlockSpec((1, 1, tmr, W3), lambda b, r, i: (b, r, i, 0)),
            pl.BlockSpec((1, SWA_SPAN, None, 2, SWA_HEADS, SWA_DIM), lambda b, r, i: (b, 0, r, 0, 0, 0)),
        ],
        out_shape=[
            jax.ShapeDtypeStruct((B, dil, M, W3), BF16),
            jax.ShapeDtypeStruct((B, SWA_SPAN, dil, 2, SWA_HEADS, SWA_DIM), F32),
        ],
        compiler_params=_cparams(3),
        name="proj_attn",
    )(hg, w_g, ng_g, bd)
    return p, t.reshape(B, keep, 2, SWA_HEADS, SWA_DIM)


def _proj_plain_kernel(x_ref, lng_ref, w_ref, o_ref, h_scr):
    @pl.when(pl.program_id(1) == 0)
    def _():
        h_scr[...] = _rms(x_ref[...], lng_ref[...]).astype(BF16)

    o_ref[...] = jnp.dot(h_scr[...], w_ref[...], preferred_element_type=F32).astype(o_ref.dtype)


def _proj_plain(x2d, ln_g, w, *, tm, tn, out_dtype, name="proj_plain"):
    N, D = x2d.shape
    C = w.shape[1]
    assert N % tm == 0 and C % tn == 0
    return pl.pallas_call(
        _proj_plain_kernel,
        grid=(N // tm, C // tn),
        in_specs=[
            pl.BlockSpec((tm, D), lambda i, j: (i, 0)),
            pl.BlockSpec((1, D), lambda i, j: (0, 0)),
            pl.BlockSpec((D, tn), lambda i, j: (0, j)),
        ],
        out_specs=pl.BlockSpec((tm, tn), lambda i, j: (i, j)),
        out_shape=jax.ShapeDtypeStruct((N, C), out_dtype),
        scratch_shapes=[pltpu.VMEM((tm, D), BF16)],
        compiler_params=_cparams(2),
        name=name,
    )(x2d, ln_g, w)


def _matmul_kernel(h_ref, w_ref, o_ref):
    o_ref[...] = jnp.dot(h_ref[...], w_ref[...], preferred_element_type=F32).astype(o_ref.dtype)


def _matmul(h2d, w, *, tm, tn, out_dtype, name):
    N, D = h2d.shape
    C = w.shape[1]
    assert N % tm == 0 and C % tn == 0
    return pl.pallas_call(
        _matmul_kernel,
        grid=(N // tm, C // tn),
        in_specs=[pl.BlockSpec((tm, D), lambda i, j: (i, 0)), pl.BlockSpec((D, tn), lambda i, j: (0, j))],
        out_specs=pl.BlockSpec((tm, tn), lambda i, j: (i, j)),
        out_shape=jax.ShapeDtypeStruct((N, C), out_dtype),
        compiler_params=_cparams(2),
        name=name,
    )(h2d, w)


def _attn_kernel(q_ref, kc_ref, vc_ref, kp_ref, vp_ref, o_ref, lse_ref, kk_scr, vv_scr, *, tq):
    i = pl.program_id(1)
    blk = SWA_SPAN
    kk_scr[0:blk, :] = kp_ref[0]
    kk_scr[blk:blk + tq, :] = kc_ref[0]
    vv_scr[0:blk, :] = vp_ref[0]
    vv_scr[blk:blk + tq, :] = vc_ref[0]
    qi = lax.broadcasted_iota(jnp.int32, (blk, 2 * blk), 0)
    ki = lax.broadcasted_iota(jnp.int32, (blk, 2 * blk), 1)
    dist = blk + qi - ki
    band = (dist >= 0) & (dist <= SWA_SPAN)
    band_first = band & ((ki >= blk) | (i > 0))
    lo = lax.broadcasted_iota(jnp.int32, (blk, LANES), 1) < SWA_DIM
    zero = jnp.zeros((blk, LANES), BF16)
    lane = lax.broadcasted_iota(jnp.int32, (blk, LANES), 1)
    for jb in range(tq // blk):
        mask = band_first if jb == 0 else band
        rows = slice(jb * blk, (jb + 1) * blk)
        lse_all = jnp.zeros((blk, LANES), F32)
        for hp in range(SWA_GW // LANES):
            cs = slice(hp * LANES, (hp + 1) * LANES)
            qb = q_ref[0, rows, cs]
            kk = kk_scr[jb * blk:(jb + 2) * blk, cs]
            vv = vv_scr[jb * blk:(jb + 2) * blk, cs]
            res_o = []
            for hh in range(2):
                qm = jnp.where(lo if hh == 0 else jnp.logical_not(lo), qb, zero)
                s = lax.dot_general(qm, kk, (((1,), (1,)), ((), ())), preferred_element_type=F32)
                s = jnp.where(mask, s * (SWA_DIM ** -0.5), -jnp.inf)
                m = jnp.max(s, axis=-1, keepdims=True)
                p = jnp.exp(s - m)
                den = jnp.sum(p, axis=-1, keepdims=True)
                pv = jnp.dot(p.astype(BF16), vv, preferred_element_type=F32)
                res_o.append(pv / den)
                lse_all = jnp.where(lane == 2 * hp + hh, m + jnp.log(den), lse_all)
            o_ref[0, rows, cs] = jnp.where(lo, res_o[0], res_o[1]).astype(BF16)
        lse_ref[0, rows, :] = lse_all


def _attn(p, *, tq):
    S, M, _ = p.shape
    assert M % tq == 0 and tq % SWA_SPAN == 0
    nb = tq // SWA_SPAN
    return pl.pallas_call(
        functools.partial(_attn_kernel, tq=tq),
        grid=(S, M // tq),
        in_specs=[
            pl.BlockSpec((1, tq, SWA_GW), lambda s, i: (s, i, 0)),
            pl.BlockSpec((1, tq, SWA_GW), lambda s, i: (s, i, 1)),
            pl.BlockSpec((1, tq, SWA_GW), lambda s, i: (s, i, 2)),
            pl.BlockSpec((1, SWA_SPAN, SWA_GW), lambda s, i: (s, jnp.maximum(i * nb - 1, 0), 1)),
            pl.BlockSpec((1, SWA_SPAN, SWA_GW), lambda s, i: (s, jnp.maximum(i * nb - 1, 0), 2)),
        ],
        out_specs=[
            pl.BlockSpec((1, tq, SWA_GW), lambda s, i: (s, i, 0)),
            pl.BlockSpec((1, tq, LANES), lambda s, i: (s, i, 0)),
        ],
        out_shape=[
            jax.ShapeDtypeStruct((S, M, SWA_GW), BF16),
            jax.ShapeDtypeStruct((S, M, LANES), F32),
        ],
        scratch_shapes=[
            pltpu.VMEM((SWA_SPAN + tq, SWA_GW), BF16),
            pltpu.VMEM((SWA_SPAN + tq, SWA_GW), BF16),
        ],
        compiler_params=_cparams(2),
        name="swa_attn",
    )(p, p, p, p, p)


def _proj_dn_kernel(h_ref, w_ref, cw_ref, q_ref, k_ref, v_ref, tail_ref, z_scr, carry_scr, *, tm, n_ct):
    i = pl.program_id(1)
    j = pl.program_id(2)
    nh = DN_HEADS
    ncb = z_scr.shape[1] // LANES

    z_scr[0:8, :] = jnp.where(i == 0, 0.0, carry_scr[j])
    z_scr[8:8 + tm, :] = jnp.dot(h_ref[0], w_ref[...], preferred_element_type=F32)
    last = z_scr[tm:tm + 8, :]
    carry_scr[j] = last
    tn = z_scr.shape[1]
    outs = (q_ref, k_ref, v_ref)
    for jj in range(n_ct):

        @pl.when(j == jj)
        def _(jj=jj):
            tail_ref[0, :, jj * tn:(jj + 1) * tn] = last
            for cbl in range(ncb):
                cs = slice(cbl * LANES, (cbl + 1) * LANES)
                part, h = divmod(jj * ncb + cbl, nh)
                xe = z_scr[:, cs]
                acc = (0.5 * cw_ref[0:1, cs]) * xe
                for t in range(1, DN_CONV):
                    acc = (0.5 * cw_ref[t:t + 1, cs]) * xe + pltpu.roll(acc, 1, axis=0)
                half = acc[8:]
                act = half * jnp.tanh(half) + half
                if part < 2:
                    inv = lax.rsqrt(jnp.sum(act * act, axis=-1, keepdims=True) + EPS)
                    act = act * (inv * (DN_DK ** -0.5) if part == 0 else inv)
                outs[part][0, :, h * LANES:(h + 1) * LANES] = act.astype(BF16)


def _proj_dn(h, w_dn, conv_w, *, tm):
    B, L, D = h.shape
    C = w_dn.shape[1]
    width = DN_HEADS * DN_DK
    n_ct = 2
    tn = C // n_ct
    assert L % tm == 0 and C == 3 * width and tn % LANES == 0
    qkv = pl.BlockSpec((1, tm, width), lambda b, i, j: (b, i, 0))
    return pl.pallas_call(
        functools.partial(_proj_dn_kernel, tm=tm, n_ct=n_ct),
        grid=(B, L // tm, n_ct),
        in_specs=[
            pl.BlockSpec((1, tm, D), lambda b, i, j: (b, i, 0)),
            pl.BlockSpec((D, tn), lambda b, i, j: (0, j)),
            pl.BlockSpec((DN_CONV, tn), lambda b, i, j: (0, j)),
        ],
        out_specs=[qkv, qkv, qkv, pl.BlockSpec((1, 8, C), lambda b, i, j: (b, 0, 0))],
        out_shape=[jax.ShapeDtypeStruct((B, L, width), BF16)] * 3 + [jax.ShapeDtypeStruct((B, 8, C), F32)],
        scratch_shapes=[pltpu.VMEM((8 + tm, tn), F32), pltpu.VMEM((n_ct, 8, tn), F32)],
        compiler_params=_cparams(3),
        name="proj_dn",
    )(h, w_dn, conv_w)


def _gates_kernel(ba_ref, par_ref, g_ref, gt_ref, *, tl):
    nh = DN_HEADS
    ba = ba_ref[0]
    lane = lax.broadcasted_iota(jnp.int32, (tl, LANES), 1)
    g = par_ref[0:1, :] * _softplus(ba + par_ref[1:2, :])
    ri = lax.broadcasted_iota(jnp.int32, (tl, tl), 0)
    ci = lax.broadcasted_iota(jnp.int32, (tl, tl), 1)
    tri = jnp.where((ri // DN_CHUNK == ci // DN_CHUNK) & (ci <= ri), 1.0, 0.0).astype(BF16)
    g_hi = g.astype(BF16)
    r1 = g - g_hi.astype(F32)
    g_mid = r1.astype(BF16)
    g_lo = (r1 - g_mid.astype(F32)).astype(BF16)
    gc = sum(jnp.dot(tri, piece, preferred_element_type=F32) for piece in (g_hi, g_mid, g_lo))
    g_ref[0] = jnp.where(lane < nh, _sigmoid(ba), gc)
    gt_ref[0] = jnp.transpose(gc)[nh:2 * nh, :]


def _dn_gates(ba, par, *, tl):
    B, L, _ = ba.shape
    assert L % tl == 0 and tl % DN_CHUNK == 0
    return pl.pallas_call(
        functools.partial(_gates_kernel, tl=tl),
        grid=(B, L // tl),
        in_specs=[pl.BlockSpec((1, tl, LANES), lambda b, i: (b, i, 0)), pl.BlockSpec((2, LANES), lambda b, i: (0, 0))],
        out_specs=[pl.BlockSpec((1, tl, LANES), lambda b, i: (b, i, 0)),
                   pl.BlockSpec((1, DN_HEADS, tl), lambda b, i: (b, 0, i))],
        out_shape=[jax.ShapeDtypeStruct((B, L, LANES), F32), jax.ShapeDtypeStruct((B, DN_HEADS, L), F32)],
        compiler_params=_cparams(2),
        name="dn_gates",
    )(ba, par)


def _intra_kernel(q_ref, k_ref, v_ref, g_ref, gt_in_ref, u_ref, w_ref, qd_ref, kd_ref, a_ref, gt_ref, *, tl):
    h = pl.program_id(1)
    C = DN_CHUNK
    lane = lax.broadcasted_iota(jnp.int32, (C, LANES), 1)
    ri = lax.broadcasted_iota(jnp.int32, (C, C), 0)
    ci = lax.broadcasted_iota(jnp.int32, (C, C), 1)
    eye = jnp.where(ri == ci, 1.0, 0.0).astype(F32)
    nt_dot = lambda a, b: lax.dot_general(a.astype(BF16), b.astype(BF16), (((1,), (1,)), ((), ())),
                                          preferred_element_type=F32)
    rows = [slice(c * C, (c + 1) * C) for c in range(tl // C)]
    gv = [g_ref[0, r, :] for r in rows]
    q = [q_ref[0, r, :].astype(F32) for r in rows]
    k = [k_ref[0, r, :].astype(F32) for r in rows]
    v = [v_ref[0, r, :].astype(F32) for r in rows]
    beta = [jnp.sum(jnp.where(lane == h, x, 0.0), axis=-1, keepdims=True) for x in gv]
    gc = [jnp.sum(jnp.where(lane == h + DN_HEADS, x, 0.0), axis=-1, keepdims=True) for x in gv]
    gc_row = gt_in_ref[0, pl.ds(h, 1), :]
    decay = [jnp.exp(jnp.where(ri >= ci, a - gc_row[:, r], -jnp.inf)) for a, r in zip(gc, rows)]
    kb = [a * b for a, b in zip(k, beta)]
    kq = [nt_dot(jnp.concatenate([a, b], axis=0), c) for a, b, c in zip(kb, q, k)]
    x = [-jnp.where(ri > ci, m[:C] * d, 0.0) for m, d in zip(kq, decay)]
    t = [eye + a for a in x]
    x = [_bdot(a, a) for a in x]
    for _ in range(4):
        both = [_bdot(jnp.concatenate([a, b], axis=0), a) for a, b in zip(x, t)]
        t = [b + m[C:] for b, m in zip(t, both)]
        x = [m[:C] for m in both]
    t = [b + _bdot(b, a) for a, b in zip(x, t)]
    eg = [jnp.exp(a) for a in gc]
    glast = [a[C - 1:C, :] for a in gc]
    uw = [_bdot(a, jnp.concatenate([b * c, d * e], axis=1)) for a, b, c, d, e in zip(t, v, beta, kb, eg)]
    for c, r in enumerate(rows):
        u_ref[0, 0, r, :] = uw[c][:, :LANES]
        w_ref[0, 0, r, :] = uw[c][:, LANES:].astype(BF16)
        a_ref[0, 0, r, :] = (kq[c][C:] * decay[c]).astype(BF16)
        qd_ref[0, 0, r, :] = (q[c] * eg[c]).astype(BF16)
        kd_ref[0, 0, r, :] = (k[c] * jnp.exp(glast[c] - gc[c])).astype(BF16)
        gt_ref[0, 0, c:c + 1, :] = jnp.broadcast_to(jnp.exp(glast[c]), (1, LANES))


def _dn_intra(q, k, v, g, gt_rows, *, tl):
    B, L, _ = q.shape
    H, C = DN_HEADS, DN_CHUNK
    assert L % tl == 0 and (tl // C) % 8 == 0
    qkv_spec = pl.BlockSpec((1, tl, LANES), lambda b, h, i: (b, i, h))
    hl = lambda w: pl.BlockSpec((1, 1, tl, w), lambda b, h, i: (b, h, i, 0))
    return pl.pallas_call(
        functools.partial(_intra_kernel, tl=tl),
        grid=(B, H, L // tl),
        in_specs=[qkv_spec, qkv_spec, qkv_spec, pl.BlockSpec((1, tl, LANES), lambda b, h, i: (b, i, 0)),
                  pl.BlockSpec((1, H, tl), lambda b, h, i: (b, 0, i))],
        out_specs=[hl(LANES), hl(LANES), hl(LANES), hl(LANES), hl(C),
                   pl.BlockSpec((1, 1, tl // C, LANES), lambda b, h, i: (b, h, i, 0))],
        out_shape=[
            jax.ShapeDtypeStruct((B, H, L, LANES), F32),
            jax.ShapeDtypeStruct((B, H, L, LANES), BF16),
            jax.ShapeDtypeStruct((B, H, L, LANES), BF16),
            jax.ShapeDtypeStruct((B, H, L, LANES), BF16),
            jax.ShapeDtypeStruct((B, H, L, C), BF16),
            jax.ShapeDtypeStruct((B, H, L // C, LANES), F32),
        ],
        compiler_params=_cparams(3),
        name="dn_intra",
    )(q, k, v, g, gt_rows)


def _scan_kernel(u_ref, w_ref, qd_ref, kd_ref, a_ref, gt_ref, s0_ref, o_ref, s_ref, *, n_chunks):
    C = DN_CHUNK
    H = s_ref.shape[1]

    @pl.when(pl.program_id(1) == 0)
    def _():
        s_ref[...] = s0_ref[...]

    def body(c, carry):
        rows = pl.ds(pl.multiple_of(c * C, C), C)
        S = [s_ref[0, h] for h in range(H)]
        Sb = [x.astype(BF16) for x in S]
        v_new = [u_ref[0, h, rows, :] - jnp.dot(w_ref[0, h, rows, :], Sb[h], preferred_element_type=F32)
                 for h in range(H)]
        vb = [x.astype(BF16) for x in v_new]
        o = [jnp.dot(qd_ref[0, h, rows, :], Sb[h], preferred_element_type=F32)
             + jnp.dot(a_ref[0, h, rows, :], vb[h], preferred_element_type=F32) for h in range(H)]
        upd = [lax.dot_general(kd_ref[0, h, rows, :], vb[h], (((0,), (0,)), ((), ())), preferred_element_type=F32)
               for h in range(H)]
        for h in range(H):
            o_ref[0, rows, h * LANES:(h + 1) * LANES] = o[h]
            s_ref[0, h] = S[h] * gt_ref[0, h, pl.ds(c, 1), :] + upd[h]
        return carry

    lax.fori_loop(0, n_chunks, body, 0)


def _dn_scan(u, w, qd, kd, a, gt, s0, *, tl):
    B, H, L, _ = u.shape
    C = DN_CHUNK
    assert L % tl == 0 and (tl // C) % 8 == 0
    hs = lambda wd: pl.BlockSpec((1, H, tl, wd), lambda b, i: (b, 0, i, 0))
    s_spec = pl.BlockSpec((1, H, DN_DK, LANES), lambda b, i: (b, 0, 0, 0))
    return pl.pallas_call(
        functools.partial(_scan_kernel, n_chunks=tl // C),
        grid=(B, L // tl),
        in_specs=[hs(LANES), hs(LANES), hs(LANES), hs(LANES), hs(C),
                  pl.BlockSpec((1, H, tl // C, LANES), lambda b, i: (b, 0, i, 0)), s_spec],
        out_specs=[pl.BlockSpec((1, tl, H * LANES), lambda b, i: (b, i, 0)), s_spec],
        out_shape=[jax.ShapeDtypeStruct((B, L, H * LANES), F32),
                   jax.ShapeDtypeStruct((B, H, DN_DK, LANES), F32)],
        compiler_params=_cparams(2),
        name="dn_scan",
    )(u, w, qd, kd, a, gt, s0)


def _gated_mix(o_a, od, gates, dng, wa, wb, wo, x, dot):
    width = DN_HEADS * DN_DK
    parts = []
    for h in range(DN_HEADS):
        blk = od[:, h * LANES:(h + 1) * LANES]
        parts.append(blk * lax.rsqrt(jnp.mean(blk * blk, axis=-1, keepdims=True) + EPS) * dng)
    odn = jnp.concatenate(parts, axis=-1) * _silu(gates[:, 0:width].astype(F32))
    ya = dot(o_a, wa)
    yb = dot(odn, wb)
    mix = _sigmoid(gates[:, width:2 * width].astype(F32)) * ya + _sigmoid(gates[:, 2 * width:].astype(F32)) * yb
    return x + dot(mix, wo)


def _out_kernel(x_ref, o0, o1, o2, l0, l1, l2, od_ref, gates_ref, dng_ref, wa_ref, wb_ref, wo_ref, e_ref, y_ref,
                so0, so1, so2, sl0, sl1, sl2, *, tm, dils):
    o_refs, l_refs = (o0, o1, o2), (l0, l1, l2)
    so, sl = (so0, so1, so2), (sl0, sl1, sl2)
    for gi, d in enumerate(dils):
        for r in range(d):
            dst = slice(None) if d == 1 else pl.ds(r, tm // d, stride=d)
            sl[gi][dst, :] = l_refs[gi][0, r]
            for cb in range(SWA_GW // LANES):
                so[gi][cb, dst, :] = o_refs[gi][0, r, :, cb * LANES:(cb + 1) * LANES].astype(F32)
    ls = [s[...] for s in sl]
    m = jnp.maximum(jnp.maximum(ls[0], ls[1]), ls[2])
    es = [jnp.exp(l - m) for l in ls]
    tot = es[0] + es[1] + es[2]
    alphas = [jnp.dot((e / tot).astype(BF16), e_ref[...], preferred_element_type=F32) for e in es]
    parts = []
    for cb in range(SWA_GW // LANES):
        cs = slice(cb * LANES, (cb + 1) * LANES)
        parts.append(alphas[0][:, cs] * so[0][cb] + alphas[1][:, cs] * so[1][cb] + alphas[2][:, cs] * so[2][cb])
    o_a = jnp.concatenate(parts, axis=-1)
    y_ref[...] = _gated_mix(o_a, od_ref[...], gates_ref[...], dng_ref[...], wa_ref[...], wb_ref[...],
                            wo_ref[...], x_ref[...], _bdot)


def _out_proj(x2d, os_, ls_, od2d, gates, dng, wa, wb, wo, e_att, *, B, L, tm):
    N, D = x2d.shape
    nt = L // tm
    dils = tuple(d for _, d in SWA_CONFIGS)
    grp = lambda d, w: pl.BlockSpec((1, d, tm // d, w), lambda i: (i // nt, 0, i % nt, 0))
    row = lambda w: pl.BlockSpec((tm, w), lambda i: (i, 0))
    full = lambda a: pl.BlockSpec(a.shape, lambda i: (0, 0))
    return pl.pallas_call(
        functools.partial(_out_kernel, tm=tm, dils=dils),
        grid=(N // tm,),
        in_specs=[row(D)] + [grp(d, SWA_GW) for d in dils] + [grp(d, LANES) for d in dils]
        + [row(od2d.shape[1]), row(gates.shape[1]), full(dng), full(wa), full(wb), full(wo), full(e_att)],
        out_specs=row(D),
        out_shape=jax.ShapeDtypeStruct((N, D), F32),
        scratch_shapes=[pltpu.VMEM((SWA_GW // LANES, tm, LANES), F32)] * 3 + [pltpu.VMEM((tm, LANES), F32)] * 3,
        compiler_params=_cparams(1),
        name="out_proj",
    )(x2d, *os_, *ls_, od2d, gates, dng, wa, wb, wo, e_att)


def _router_kernel(x_ref, lng_ref, wr_ref, br_ref, info_ref, cnt_ref, base_scr, *, tm):
    i = pl.program_id(0)

    @pl.when(i == 0)
    def _():
        base_scr[...] = jnp.zeros_like(base_scr)

    h = _rms(x_ref[...], lng_ref[...])
    lg = _bdot(h, wr_ref[...]) + br_ref[...]
    lane = lax.broadcasted_iota(jnp.int32, (tm, LANES), 1)
    big = jnp.int32(1 << 20)
    ninf = -jnp.inf

    def argmax_lane(vals):
        mx = jnp.max(vals, axis=-1, keepdims=True)
        idx = jnp.min(jnp.where(vals == mx, lane, big), axis=-1, keepdims=True)
        return mx, idx

    lgm = jnp.where(lane < N_GROUPS, lg, ninf)
    mg, gsel = argmax_lane(lgm)
    pg = 1.0 / jnp.sum(jnp.exp(lgm - mg), axis=-1, keepdims=True)
    start = N_GROUPS + gsel * PER_GROUP
    le = jnp.where((lane >= start) & (lane < start + PER_GROUP), lg, ninf)
    m1, i1 = argmax_lane(le)
    m2, i2 = argmax_lane(jnp.where(lane == i1, ninf, le))
    e21 = jnp.exp(m2 - m1)
    w1 = pg / (1.0 + e21)
    w2 = pg * e21 / (1.0 + e21)
    oh = jnp.where(lane == i1, 1.0, 0.0) + jnp.where(lane == i2, 1.0, 0.0)
    ri = lax.broadcasted_iota(jnp.int32, (tm, tm), 0)
    ci = lax.broadcasted_iota(jnp.int32, (tm, tm), 1)
    strict = jnp.where(ci < ri, 1.0, 0.0).astype(BF16)
    pref = jnp.dot(strict, oh.astype(BF16), preferred_element_type=F32) + base_scr[...]
    r1 = jnp.sum(jnp.where(lane == i1, pref, 0.0), axis=-1, keepdims=True)
    r2 = jnp.sum(jnp.where(lane == i2, pref, 0.0), axis=-1, keepdims=True)
    base_scr[...] = base_scr[...] + jnp.sum(oh, axis=0, keepdims=True)
    cnt_ref[...] = base_scr[...]
    off = jnp.float32(N_GROUPS)
    info = jnp.where(lane == 0, i1.astype(F32) - off, 0.0)
    info = jnp.where(lane == 1, i2.astype(F32) - off, info)
    info = jnp.where(lane == 2, w1, info)
    info = jnp.where(lane == 3, w2, info)
    info = jnp.where(lane == 4, r1, info)
    info = jnp.where(lane == 5, r2, info)
    info_ref[...] = info


def _router(x2d, ln_g, wr, br, *, tm):
    N, D = x2d.shape
    assert N % tm == 0
    return pl.pallas_call(
        functools.partial(_router_kernel, tm=tm),
        grid=(N // tm,),
        in_specs=[
            pl.BlockSpec((tm, D), lambda i: (i, 0)),
            pl.BlockSpec((1, D), lambda i: (0, 0)),
            pl.BlockSpec((D, LANES), lambda i: (0, 0)),
            pl.BlockSpec((1, LANES), lambda i: (0, 0)),
        ],
        out_specs=[pl.BlockSpec((tm, LANES), lambda i: (i, 0)), pl.BlockSpec((1, LANES), lambda i: (0, 0))],
        out_shape=[jax.ShapeDtypeStruct((N, LANES), F32), jax.ShapeDtypeStruct((1, LANES), F32)],
        scratch_shapes=[pltpu.VMEM((1, LANES), F32)],
        compiler_params=_cparams(1),
        name="router",
    )(x2d, ln_g, wr, br)


def _dispatch_kernel(dest_ref, zb_ref, x_ref, xs_ref, zero_scr, rows_scr, sem, *, tm, tb, n_zb, n_tiles):
    i = pl.program_id(0)

    @pl.when(i == 0)
    def _():
        zero_scr[...] = jnp.zeros_like(zero_scr)

        def zero_copy(n):
            return pltpu.make_async_copy(zero_scr, xs_ref.at[pl.ds(zb_ref[n] * tb, tb)], sem.at[2])

        def zero_issue(n, carry):
            @pl.when(zb_ref[n] >= 0)
            def _():
                zero_copy(n).start()

            return carry

        def zero_wait(n, carry):
            @pl.when(zb_ref[n] >= 0)
            def _():
                zero_copy(n).wait()

            return carry

        lax.fori_loop(0, n_zb, zero_issue, 0)
        lax.fori_loop(0, n_zb, zero_wait, 0)

    buf_now = lax.rem(i, 2)
    rows_scr[buf_now] = x_ref[...].reshape(rows_scr.shape[1:])

    def row_copy(tile, t, slot):
        buf = lax.rem(tile, 2)
        return pltpu.make_async_copy(
            rows_scr.at[buf, pl.ds(t, 1)],
            xs_ref.at[pl.ds(dest_ref[(tile * tm + t) * TOP_K + slot], 1)], sem.at[buf])

    def issue(tt, carry):
        for r in range(ROW_UNROLL):
            for slot in range(TOP_K):
                row_copy(i, tt * ROW_UNROLL + r, slot).start(priority=slot)
        return carry

    def drain(tile):
        buf = lax.rem(tile, 2)
        for _ in range(TOP_K):
            pltpu.make_async_copy(rows_scr.at[buf], rows_scr.at[buf], sem.at[buf]).wait()

    lax.fori_loop(0, tm // ROW_UNROLL, issue, 0)

    @pl.when(i > 0)
    def _():
        drain(i - 1)

    @pl.when(i == n_tiles - 1)
    def _():
        drain(i)


def _dispatch(dest, zero_blocks, x2d, *, tm, tb, n_rows):
    N, D = x2d.shape
    return pl.pallas_call(
        functools.partial(_dispatch_kernel, tm=tm, tb=tb, n_zb=zero_blocks.shape[0], n_tiles=N // tm),
        grid_spec=pltpu.PrefetchScalarGridSpec(
            num_scalar_prefetch=2,
            grid=(N // tm,),
            in_specs=[pl.BlockSpec((tm, D), lambda i, d, z: (i, 0))],
            out_specs=pl.BlockSpec(memory_space=pl.ANY),
            scratch_shapes=[pltpu.VMEM((tb, D // LANES, LANES), F32), pltpu.VMEM((2, tm, D // LANES, LANES), F32),
                            pltpu.SemaphoreType.DMA((3,))],
        ),
        out_shape=jax.ShapeDtypeStruct((n_rows, D // LANES, LANES), F32),
        compiler_params=_cparams(1),
        name="moe_dispatch",
    )(dest, zero_blocks, x2d)


def _ffn_kernel(be_ref, nb_ref, xs_ref, lng_ref, wg_ref, wu_ref, wd_ref, y_ref, wg_scr, wu_scr, wd_scr):
    i = pl.program_id(0)
    used = i < nb_ref[0]
    tb = xs_ref.shape[0]

    @pl.when(jnp.logical_or(i == 0, be_ref[i] != be_ref[jnp.maximum(i - 1, 0)]))
    def _():
        wg_scr[...] = wg_ref[0].astype(BF16)
        wu_scr[...] = wu_ref[0].astype(BF16)
        wd_scr[...] = wd_ref[0].astype(BF16)

    @pl.when(used)
    def _():
        h = _rms(xs_ref[...].reshape(tb, -1), lng_ref[...]).astype(BF16)
        g = jnp.dot(h, wg_scr[...], preferred_element_type=F32)
        u = jnp.dot(h, wu_scr[...], preferred_element_type=F32)
        y = jnp.dot((_silu(g) * u).astype(BF16), wd_scr[...], preferred_element_type=F32)
        y_ref[...] = y.reshape(y_ref.shape)

    @pl.when(jnp.logical_not(used))
    def _():
        y_ref[...] = jnp.zeros_like(y_ref)


def _ffn(blk_e, nb_used, xs, ln_g, wg, wu, wd, layer, *, tb):
    P, S, _ = xs.shape
    D = S * LANES
    nb = P // tb
    DE = wg.shape[3]
    return pl.pallas_call(
        _ffn_kernel,
        grid_spec=pltpu.PrefetchScalarGridSpec(
            num_scalar_prefetch=2,
            grid=(nb,),
            in_specs=[
                pl.BlockSpec((tb, S, LANES), lambda i, be, nbu: (jnp.minimum(i, nbu[0] - 1), 0, 0)),
                pl.BlockSpec((1, D), lambda i, be, nbu: (0, 0)),
                pl.BlockSpec((None, 1, D, DE), lambda i, be, nbu: (layer, be[i], 0, 0)),
                pl.BlockSpec((None, 1, D, DE), lambda i, be, nbu: (layer, be[i], 0, 0)),
                pl.BlockSpec((None, 1, DE, D), lambda i, be, nbu: (layer, be[i], 0, 0)),
            ],
            out_specs=pl.BlockSpec((tb, S, LANES), lambda i, be, nbu: (i, 0, 0)),
            scratch_shapes=[pltpu.VMEM((D, DE), BF16), pltpu.VMEM((D, DE), BF16), pltpu.VMEM((DE, D), BF16)],
        ),
        out_shape=jax.ShapeDtypeStruct((P, S, LANES), F32),
        compiler_params=_cparams(1),
        name="moe_ffn",
    )(blk_e, nb_used, xs, ln_g, wg, wu, wd)


def _combine_kernel(dest_ref, x_ref, info_ref, yb_ref, y_ref, g_scr, sem, *, tm, n_tiles):
    i = pl.program_id(0)

    def row_copy(tile, t, slot):
        buf = lax.rem(tile, 2)
        return pltpu.make_async_copy(
            yb_ref.at[pl.ds(dest_ref[(tile * tm + t) * TOP_K + slot], 1)],
            g_scr.at[buf, slot, pl.ds(t, 1)], sem.at[buf])

    def issue_tile(tile):
        def body(tt, carry):
            for r in range(ROW_UNROLL):
                for slot in range(TOP_K):
                    row_copy(tile, tt * ROW_UNROLL + r, slot).start(priority=slot)
            return carry

        lax.fori_loop(0, tm // ROW_UNROLL, body, 0)

    @pl.when(i == 0)
    def _():
        issue_tile(i)

    @pl.when(i + 1 < n_tiles)
    def _():
        issue_tile(i + 1)

    buf = lax.rem(i, 2)
    pltpu.make_async_copy(g_scr.at[buf], g_scr.at[buf], sem.at[buf]).wait()
    info = info_ref[...]
    lane = lax.broadcasted_iota(jnp.int32, info.shape, 1)
    w1 = jnp.sum(jnp.where(lane == 2, info, 0.0), axis=-1, keepdims=True)
    w2 = jnp.sum(jnp.where(lane == 3, info, 0.0), axis=-1, keepdims=True)
    g1 = g_scr[buf, 0].reshape(x_ref.shape)
    g2 = g_scr[buf, 1].reshape(x_ref.shape)
    y_ref[...] = x_ref[...] + (w1 * g1 + w2 * g2)


def _combine(dest, x2d, info, yb, *, tm):
    N, D = x2d.shape
    return pl.pallas_call(
        functools.partial(_combine_kernel, tm=tm, n_tiles=N // tm),
        grid_spec=pltpu.PrefetchScalarGridSpec(
            num_scalar_prefetch=1,
            grid=(N // tm,),
            in_specs=[
                pl.BlockSpec((tm, D), lambda i, d: (i, 0)),
                pl.BlockSpec((tm, LANES), lambda i, d: (i, 0)),
                pl.BlockSpec(memory_space=pl.ANY),
            ],
            out_specs=pl.BlockSpec((tm, D), lambda i, d: (i, 0)),
            scratch_shapes=[pltpu.VMEM((2, TOP_K, tm, D // LANES, LANES), F32), pltpu.SemaphoreType.DMA((2,))],
        ),
        out_shape=jax.ShapeDtypeStruct((N, D), F32),
        compiler_params=_cparams(1),
        name="moe_combine",
    )(dest, x2d, info, yb)


def _moe(x2d, ln2_g, wr, br, wg, wu, wd, layer, *, tm):
    N, D = x2d.shape
    tb = MOE_ROWS if N * TOP_K >= N_EXPERTS * MOE_ROWS else MOE_ROWS_SMALL
    info, counts = _router(x2d, ln2_g, wr, br, tm=tm)
    counts = counts[0, N_GROUPS:N_GROUPS + N_EXPERTS].astype(jnp.int32)
    pcounts = (counts + tb - 1) // tb * tb
    pend = jnp.cumsum(pcounts)
    pstart = pend - pcounts
    e = info[:, 0:TOP_K].astype(jnp.int32)
    rank = info[:, 4:4 + TOP_K].astype(jnp.int32)
    experts = jnp.arange(N_EXPERTS, dtype=jnp.int32)
    dest = (jnp.sum(jnp.where(e[..., None] == experts, pstart, 0), axis=-1) + rank).reshape(-1)
    nb = -(-(N * TOP_K) // tb) + N_EXPERTS
    P = nb * tb
    blocks = jnp.arange(nb, dtype=jnp.int32)
    blk_e = jnp.minimum(jnp.sum((pend[None, :] <= blocks[:, None] * tb).astype(jnp.int32), axis=1), N_EXPERTS - 1)
    nb_used = (pend[-1] // tb).astype(jnp.int32).reshape(1)
    zero_blocks = jnp.concatenate([jnp.where(counts % tb != 0, pend // tb - 1, -1),
                                   jnp.where(blocks >= nb_used[0], blocks, -1)]).astype(jnp.int32)
    xs = _dispatch(dest, zero_blocks, x2d, tm=tm, tb=tb, n_rows=P)
    yb = _ffn(blk_e, nb_used, xs, ln2_g, wg, wu, wd, layer, tb=tb)
    return _combine(dest, x2d, info, yb, tm=tm)


def _rows8(x):
    return jnp.broadcast_to(x, (8, x.shape[1]))


def _row_hdot(x, m):
    return _hdot(_rows8(x), m)[0:1]


def _bf_round(x):
    return x.astype(BF16).astype(F32)


def _sample_attn_kernel(z_ref, c0, c1, c2, qg_ref, kg_ref, oa_ref, kv_ref):
    W = SWA_GW
    scale = SWA_DIM ** -0.5
    z = z_ref[0]
    sub = lax.broadcasted_iota(jnp.int32, (SWA_HEADS, W), 0)
    lane = lax.broadcasted_iota(jnp.int32, (SWA_HEADS, W), 1)
    own = lane // SWA_DIM == sub

    def heads(row):
        return jnp.where(own, jnp.broadcast_to(row, (SWA_HEADS, W)), 0.0)

    def head_sum(row):
        return jnp.sum(heads(row), axis=-1, keepdims=True)

    def spread(col):
        return jnp.sum(jnp.where(own, col, 0.0), axis=0, keepdims=True)

    def headnorm(zz, g):
        return zz * spread(lax.rsqrt(head_sum(zz * zz) * (1.0 / SWA_DIM) + EPS)) * g

    outs, lses = [], []
    for gi, (c_ref, (win, dil)) in enumerate(zip((c0, c1, c2), SWA_CONFIGS)):
        q = headnorm(z[:, gi * W:(gi + 1) * W], qg_ref[gi:gi + 1, :])
        k = headnorm(z[:, 3 * W + gi * W:3 * W + (gi + 1) * W], kg_ref[gi:gi + 1, :])
        v = z[:, 6 * W + gi * W:6 * W + (gi + 1) * W]
        kv_ref[0, :, 2 * gi * W:(2 * gi + 1) * W] = k
        kv_ref[0, :, (2 * gi + 1) * W:(2 * gi + 2) * W] = v
        kc = c_ref[0].reshape(W, win).astype(BF16)
        vc = c_ref[1].reshape(W, win).astype(BF16)
        s_c = jnp.dot(heads(q).astype(BF16), kc, preferred_element_type=F32) * scale
        row = lax.broadcasted_iota(jnp.int32, s_c.shape, 1)
        s_c = jnp.where(row % dil == 0, s_c, -jnp.inf)
        s_n = head_sum(_bf_round(k) * _bf_round(q)) * scale
        m = jnp.maximum(jnp.max(s_c, axis=-1, keepdims=True), s_n)
        p_c = jnp.exp(s_c - m)
        p_n = jnp.exp(s_n - m)
        den = jnp.sum(p_c, axis=-1, keepdims=True) + p_n
        pv = lax.dot_general(p_c.astype(BF16), vc, (((1,), (1,)), ((), ())), preferred_element_type=F32)
        num = jnp.sum(jnp.where(own, pv, 0.0), axis=0, keepdims=True) + spread(_bf_round(p_n)) * _bf_round(v)
        outs.append(num / spread(den))
        lses.append(m + jnp.log(den))
    mm = jnp.maximum(jnp.maximum(lses[0], lses[1]), lses[2])
    es = [jnp.exp(l - mm) for l in lses]
    tot = es[0] + es[1] + es[2]
    oa_ref[0] = sum(spread(_bf_round(e / tot)) * _bf_round(o) for e, o in zip(es, outs))


def _sample_attn(z3, caches, layer, qg, kg):
    Bs = z3.shape[0]
    W = SWA_GW
    cviews, cspecs = [], []
    for (win, dil), c in zip(SWA_CONFIGS, caches):
        assert c.shape[2] == win
        cviews.append(jnp.transpose(c, (0, 1, 3, 4, 5, 2)))
        cspecs.append(pl.BlockSpec((None, None, 2, SWA_HEADS, SWA_DIM, win), lambda b: (layer, b, 0, 0, 0, 0)))
    full = lambda a: pl.BlockSpec(a.shape, lambda b: (0,) * a.ndim)
    return pl.pallas_call(
        _sample_attn_kernel,
        grid=(Bs,),
        in_specs=[pl.BlockSpec((1, 1, 9 * W), lambda b: (b, 0, 0))] + cspecs + [full(qg), full(kg)],
        out_specs=[pl.BlockSpec((1, 1, W), lambda b: (b, 0, 0)), pl.BlockSpec((1, 1, 6 * W), lambda b: (b, 0, 0))],
        out_shape=[jax.ShapeDtypeStruct((Bs, 1, W), F32), jax.ShapeDtypeStruct((Bs, 1, 6 * W), F32)],
        compiler_params=_cparams(1),
        name="sample_attn",
    )(z3, *cviews, qg, kg)


def _sample_dn_kernel(raw_ref, cs_ref, cw_ref, ba_ref, par_ref, s_ref, e_ref, etb_ref, etg_ref, o_ref, so_ref):
    E, ETB, ETG = e_ref[...], etb_ref[...], etg_ref[...]
    width = DN_HEADS * DN_DK
    conv = cw_ref[DN_CONV - 1:DN_CONV, :] * raw_ref[0]
    for t in range(DN_CONV - 1):
        conv = conv + cw_ref[t:t + 1, :] * cs_ref[0, t:t + 1, :]
    act = _silu(conv)

    def l2(zz):
        return zz * _row_hdot(lax.rsqrt(_row_hdot(zz * zz, E) + EPS), ETB)

    qn = l2(act[:, 0:width]) * (DN_DK ** -0.5)
    kn = l2(act[:, width:2 * width])
    vn = act[:, 2 * width:3 * width]
    ba = ba_ref[0]
    beta = _row_hdot(_sigmoid(ba), ETB)
    eg = jnp.exp(_row_hdot(par_ref[0:1, :] * _softplus(ba + par_ref[1:2, :]), ETG))
    row0 = lax.broadcasted_iota(jnp.int32, (8, LANES), 0) == 0
    for h in range(DN_HEADS):
        sl = slice(h * LANES, (h + 1) * LANES)
        S = s_ref[0, h]
        q, k, v, b, e = qn[:, sl], kn[:, sl], vn[:, sl], beta[:, sl], eg[:, sl]
        Sb = S.astype(BF16)
        wq = jnp.concatenate([k * b * e, q * e, jnp.zeros((6, LANES), F32)], axis=0)
        both = jnp.dot(wq.astype(BF16), Sb, preferred_element_type=F32)
        v_new = v * b - both[0:1]
        a = jnp.sum(q * k, axis=-1, keepdims=True)
        o_ref[0, :, sl] = both[1:2] + a * v_new
        k8 = jnp.where(row0, _rows8(k), 0.0)
        upd = lax.dot_general(k8, _rows8(v_new), (((0,), (0,)), ((), ())), preferred_element_type=F32, precision=HI)
        so_ref[0, h] = S * e + upd


def _sample_dn(raw3, conv_state, s0, layer, conv_w, ba3, par, e_mat, etb, etg):
    Bs, _, C = raw3.shape
    H = DN_HEADS
    full = lambda a: pl.BlockSpec(a.shape, lambda b: (0,) * a.ndim)
    return pl.pallas_call(
        _sample_dn_kernel,
        grid=(Bs,),
        in_specs=[pl.BlockSpec((1, 1, C), lambda b: (b, 0, 0)),
                  pl.BlockSpec((None, 1, DN_CONV - 1, C), lambda b: (layer, b, 0, 0)),
                  full(conv_w),
                  pl.BlockSpec((1, 1, LANES), lambda b: (b, 0, 0)),
                  full(par),
                  pl.BlockSpec((None, 1, H, DN_DK, LANES), lambda b: (layer, b, 0, 0, 0)),
                  full(e_mat), full(etb), full(etg)],
        out_specs=[pl.BlockSpec((1, 1, H * LANES), lambda b: (b, 0, 0)),
                   pl.BlockSpec((1, H, DN_DK, LANES), lambda b: (b, 0, 0, 0))],
        out_shape=[jax.ShapeDtypeStruct((Bs, 1, H * LANES), F32), jax.ShapeDtypeStruct(s0.shape[1:], F32)],
        compiler_params=_cparams(1),
        name="sample_dn",
    )(raw3, conv_state, conv_w, ba3, par, s0, e_mat, etb, etg)


def _sample_out_kernel(x_ref, oa_ref, od_ref, gates_ref, dng_ref, wa_ref, wb_ref, wo_ref, y_ref):
    y_ref[...] = _gated_mix(oa_ref[...], od_ref[...], gates_ref[...], dng_ref[...], wa_ref[...], wb_ref[...],
                            wo_ref[...], x_ref[...], _bdot)


def _sample_out(x2d, oa, od, gates, dng, wa, wb, wo):
    args = (x2d, oa, od, gates, dng, wa, wb, wo)
    return pl.pallas_call(
        _sample_out_kernel,
        grid=(1,),
        in_specs=[pl.BlockSpec(a.shape, lambda i: (0, 0)) for a in args],
        out_specs=pl.BlockSpec(x2d.shape, lambda i: (0, 0)),
        out_shape=jax.ShapeDtypeStruct(x2d.shape, F32),
        compiler_params=_cparams(1),
        name="sample_out",
    )(*args)


def _head_indicator(width, head):
    c = jnp.arange(width)[:, None] // head
    return (c == jnp.arange(LANES)[None, :]).astype(F32)


def _prep_layer(l, ln1_g, w_in, q_norm_g, k_norm_g, dn_conv_w, dn_a_log, dn_dt_bias, dn_norm_g, w_out_a, w_out_b,
                w_o, ln2_g, w_rg, b_rg, w_re, b_re, w_e_gate, w_e_up, w_e_down):
    D = w_in.shape[1]
    a_w = 3 * 3 * SWA_GW
    dn_w = DN_HEADS * 3 * DN_DK
    hv = DN_HEADS * DN_DK
    w = w_in[l]
    splits = dict(att=w[:, :a_w], dn=w[:, a_w:a_w + dn_w],
                  ba=jnp.pad(w[:, a_w + dn_w:a_w + dn_w + 2 * DN_HEADS], ((0, 0), (0, LANES - 2 * DN_HEADS))),
                  gate=w[:, a_w + dn_w + 2 * DN_HEADS:])
    assert splits["gate"].shape[1] == hv + 2 * D
    tile_heads = lambda g: jnp.broadcast_to(g[:, None, :], (len(SWA_CONFIGS), SWA_HEADS, SWA_DIM)).reshape(len(SWA_CONFIGS), SWA_GW)
    qg, kg = tile_heads(q_norm_g[l]), tile_heads(k_norm_g[l])
    idx = jnp.arange(MXU) // SWA_DIM
    n_g = len(SWA_CONFIGS)
    par = jnp.zeros((2, LANES), F32)
    par = par.at[0, DN_HEADS:2 * DN_HEADS].set(-jnp.exp(dn_a_log[l].astype(F32)))
    par = par.at[1, DN_HEADS:2 * DN_HEADS].set(dn_dt_bias[l].astype(F32))
    wr = jnp.pad(jnp.concatenate([w_rg[l], w_re[l]], axis=1), ((0, 0), (0, LANES - N_GROUPS - N_EXPERTS)))
    br = jnp.pad(jnp.concatenate([b_rg[l], b_re[l]]), (0, LANES - N_GROUPS - N_EXPERTS)).reshape(1, LANES)
    e8 = _head_indicator(hv, DN_DK)
    return dict(
        bf16={k: v.astype(BF16) for k, v in splits.items()},
        ln1=ln1_g[l].reshape(1, D), ln2=ln2_g[l].reshape(1, D),
        qg=qg, kg=kg,
        w_grp=[jnp.concatenate([w[:, s * n_g * SWA_GW + g * SWA_GW:s * n_g * SWA_GW + (g + 1) * SWA_GW]
                                for s in range(3)], axis=1).astype(BF16) for g in range(n_g)],
        ng_grp=[jnp.concatenate([qg[g], kg[g]]).reshape(1, 1, 2 * SWA_GW) for g in range(n_g)],
        bd=((idx[:, None] == idx[None, :]).astype(F32) / SWA_DIM).astype(BF16),
        conv_w=dn_conv_w[l], par=par, dng=dn_norm_g[l].reshape(1, DN_DK),
        wa=w_out_a[l].astype(BF16), wb=w_out_b[l].astype(BF16), wo=w_o[l].astype(BF16), wr=wr.astype(BF16), br=br,
        wg=w_e_gate, wu=w_e_up, wd=w_e_down, layer=l,
        e_dn=e8, etb=e8.T, etg=jnp.roll(e8, DN_HEADS, axis=1).T,
        e_att=_head_indicator(SWA_GW, SWA_DIM).T.astype(BF16),
    )


def _layer_prompt(x, p):
    B, L, D = x.shape
    N = B * L
    x2d = x.reshape(N, D)
    bw = p["bf16"]
    pks, tails = [], []
    hgs = _norm_permute(x, p["ln1"], tuple(d for _, d in SWA_CONFIGS), tm=min(512, L))
    for g, (win, dil) in enumerate(SWA_CONFIGS):
        assert L >= win
        pk, tail = _proj_attn(hgs[g], p["w_grp"][g], p["ng_grp"][g], p["bd"], tmr=min(512, L // dil))
        pks.append(pk)
        tails.append(tail)
    tmp = min(1024, N)
    h0 = hgs[0].reshape(B, L, D)
    qd, kd, vd, raw_tail = _proj_dn(h0, bw["dn"], p["conv_w"], tm=min(512, L))
    gates = _matmul(h0.reshape(N, D), bw["gate"], tm=tmp, tn=1536, out_dtype=BF16, name="proj_gate")
    ba = _matmul(h0.reshape(N, D), bw["ba"], tm=tmp, tn=LANES, out_dtype=F32, name="proj_ba")
    os_, ls_ = [], []
    for pk in pks:
        d, M = pk.shape[1], pk.shape[2]
        o, lse = _attn(pk.reshape(B * d, M, pk.shape[3]), tq=min(256, M))
        os_.append(o.reshape(B, d, M, SWA_GW))
        ls_.append(lse.reshape(B, d, M, LANES))
    gb, gt_rows = _dn_gates(ba.reshape(B, L, LANES), p["par"], tl=min(256, L))
    u, w, qdec, kdec, a, gt = _dn_intra(qd, kd, vd, gb, gt_rows, tl=min(2048, L))
    od, s_new = _dn_scan(u, w, qdec, kdec, a, gt, jnp.zeros((B, DN_HEADS, DN_DK, LANES), F32), tl=min(1024, L))
    x2 = _out_proj(x2d, os_, ls_, od.reshape(N, -1), gates, p["dng"], p["wa"], p["wb"], p["wo"], p["e_att"],
                   B=B, L=L, tm=min(512, L))
    y = _moe(x2, p["ln2"], p["wr"], p["br"], p["wg"], p["wu"], p["wd"], p["layer"], tm=256)
    return y.reshape(B, L, D), tails, raw_tail[:, 8 - (DN_CONV - 1):], s_new


def _layer_sample(x, caches, conv_state, s0, layer, p):
    Bs, T, D = x.shape
    assert T == 1
    x2d = x.reshape(Bs, D)
    bw = p["bf16"]
    proj = functools.partial(_proj_plain, x2d, p["ln1"], tm=Bs, out_dtype=F32)
    z_att = proj(bw["att"], tn=1536, name="sproj_att")
    raw = proj(bw["dn"], tn=1536, name="sproj_dn")
    gates = proj(bw["gate"], tn=1536, name="sproj_gate")
    ba = proj(bw["ba"], tn=LANES, name="sproj_ba")
    oa, kv = _sample_attn(z_att.reshape(Bs, 1, -1), caches, layer, p["qg"], p["kg"])
    raw3 = raw.reshape(Bs, 1, -1)
    od, s_new = _sample_dn(raw3, conv_state, s0, layer, p["conv_w"], ba.reshape(Bs, 1, LANES), p["par"],
                           p["e_dn"], p["etb"], p["etg"])
    x2 = _sample_out(x2d, oa.reshape(Bs, -1), od.reshape(Bs, -1), gates, p["dng"], p["wa"], p["wb"], p["wo"])
    y = _moe(x2, p["ln2"], p["wr"], p["br"], p["wg"], p["wu"], p["wd"], p["layer"], tm=Bs)
    W2 = 2 * SWA_GW
    kvs = [kv[:, :, g * W2:(g + 1) * W2].reshape(Bs, 1, 2, SWA_HEADS, SWA_DIM) for g in range(len(SWA_CONFIGS))]
    new_conv = jnp.concatenate([conv_state[layer][:, 1:], raw3], axis=1)
    return y.reshape(Bs, 1, D), kvs, new_conv, s_new


def kernel(x_prompt, x_sample, cache_swa0_kv, cache_swa1_kv, cache_swa2_kv, state_dn_conv, state_dn_S, ln1_g, w_in,
           q_norm_g, k_norm_g, dn_conv_w, dn_a_log, dn_dt_bias, dn_norm_g, w_out_a, w_out_b, w_o, ln2_g, w_rg, b_rg,
           w_re, b_re, w_e_gate, w_e_up, w_e_down):
    yp, ys = x_prompt, x_sample
    outs = [[] for _ in range(10)]
    for l in range(w_in.shape[0]):
        p = _prep_layer(l, ln1_g, w_in, q_norm_g, k_norm_g, dn_conv_w, dn_a_log, dn_dt_bias, dn_norm_g, w_out_a,
                        w_out_b, w_o, ln2_g, w_rg, b_rg, w_re, b_re, w_e_gate, w_e_up, w_e_down)
        yp, pkv, pconv, ps = _layer_prompt(yp, p)
        ys, skv, sconv, ss = _layer_sample(ys, (cache_swa0_kv, cache_swa1_kv, cache_swa2_kv), state_dn_conv,
                                           state_dn_S, l, p)
        for lst, val in zip(outs, (*pkv, pconv, ps, *skv, sconv, ss)):
            lst.append(val)
    return (yp, ys, *(jnp.stack(o) for o in outs))
```

```python
import functools

import jax
import jax.numpy as jnp
from jax import lax
from jax.experimental import pallas as pl
from jax.experimental.pallas import tpu as pltpu

F32 = jnp.float32
BF16 = jnp.bfloat16
HI = lax.Precision.HIGHEST
EPS = 1e-6

SWA_CONFIGS = ((128, 1), (512, 4), (2048, 16))
SWA_HEADS = 8
SWA_DIM = 64
SWA_GW = SWA_HEADS * SWA_DIM
SWA_SPAN = 128
DN_HEADS = 8
DN_DK = 128
DN_CONV = 4
DN_CHUNK = 64
N_GROUPS = 4
PER_GROUP = 8
N_EXPERTS = N_GROUPS * PER_GROUP
TOP_K = 2

VMEM_LIMIT_BYTES = 56 * 1024 * 1024
LANES = 128
MXU = 256
MOE_ROWS = 512
MOE_ROWS_SMALL = 128
ROW_UNROLL = 8
ROUTER_ROWS = 512


def _cparams(n_axes):
    return pltpu.CompilerParams(
        dimension_semantics=("arbitrary",) * n_axes, vmem_limit_bytes=VMEM_LIMIT_BYTES
    )


def _rms(x, g):
    return x * lax.rsqrt(jnp.mean(x * x, axis=-1, keepdims=True) + EPS) * g


def _bdot(a, b):
    return jnp.dot(a.astype(BF16), b.astype(BF16), preferred_element_type=F32)


def _hdot(a, b):
    return jnp.dot(a, b, preferred_element_type=F32, precision=HI)


def _sigmoid(x):
    return 0.5 * jnp.tanh(0.5 * x) + 0.5


def _silu(x):
    half = 0.5 * x
    return half * jnp.tanh(half) + half


def _softplus(x):
    return jnp.maximum(x, 0.0) + jnp.log1p(jnp.exp(-jnp.abs(x)))


def _norm_permute_kernel(x_ref, lng_ref, *refs, tm, dils):
    outs, h_scr = refs[:-1], refs[-1]
    h = _rms(x_ref[0], lng_ref[...])
    n_cb = h_scr.shape[0]
    for cb in range(n_cb):
        h_scr[cb] = h[:, cb * LANES:(cb + 1) * LANES]
    for o_ref, d in zip(outs, dils):
        for cb in range(n_cb):
            for r in range(d):
                src = h_scr[cb] if d == 1 else h_scr[cb, pl.ds(r, tm // d, stride=d), :]
                o_ref[0, r, :, cb * LANES:(cb + 1) * LANES] = src.astype(BF16)


def _norm_permute(x, ln_g, dils, *, tm):
    B, L, D = x.shape
    assert L % tm == 0 and all(tm % (16 * d) == 0 for d in dils)
    return pl.pallas_call(
        functools.partial(_norm_permute_kernel, tm=tm, dils=dils),
        grid=(B, L // tm),
        in_specs=[pl.BlockSpec((1, tm, D), lambda b, i: (b, i, 0)), pl.BlockSpec((1, D), lambda b, i: (0, 0))],
        out_specs=[pl.BlockSpec((1, d, tm // d, D), lambda b, i: (b, 0, i, 0)) for d in dils],
        out_shape=[jax.ShapeDtypeStruct((B, d, L // d, D), BF16) for d in dils],
        scratch_shapes=[pltpu.VMEM((D // LANES, tm, LANES), F32)],
        compiler_params=_cparams(2),
        name="norm_permute",
    )(x, ln_g)


def _proj_attn_kernel(h_ref, w_ref, ng_ref, bd_ref, p_ref, t_ref, *, n_tiles):
    rows = h_ref.shape[2]
    z = jnp.dot(h_ref[0, 0], w_ref[...], preferred_element_type=F32)
    kv = []
    for c in range(0, 3 * SWA_GW, MXU):
        zc = z[:, c:c + MXU]
        if c < 2 * SWA_GW:
            ms = jnp.dot((zc * zc).astype(BF16), bd_ref[...], preferred_element_type=F32)
            zc = zc * lax.rsqrt(ms + EPS) * ng_ref[0, :, c:c + MXU]
        p_ref[0, 0, :, c:c + MXU] = zc.astype(BF16)
        if c >= SWA_GW:
            kv.append(zc[rows - SWA_SPAN:rows, :])

    @pl.when(pl.program_id(2) == n_tiles - 1)
    def _():
        per = SWA_GW // MXU
        for s in range(2):
            zr = jnp.concatenate(kv[s * per:(s + 1) * per], axis=-1)
            t_ref[0, :, s] = zr.reshape(SWA_SPAN, SWA_HEADS, SWA_DIM)


def _proj_attn(hg, w_g, ng_g, bd, *, tmr):
    B, dil, M, D = hg.shape
    assert M % tmr == 0 and tmr >= SWA_SPAN
    nt = M // tmr
    W3 = 3 * SWA_GW
    keep = SWA_SPAN * dil
    p, t = pl.pallas_call(
        functools.partial(_proj_attn_kernel, n_tiles=nt),
        grid=(B, dil, nt),
        in_specs=[
            pl.BlockSpec((1, 1, tmr, D), lambda b, r, i: (b, r, i, 0)),
            pl.BlockSpec((D, W3), lambda b, r, i: (0, 0)),
            pl.BlockSpec((1, 1, 2 * SWA_GW), lambda b, r, i: (0, 0, 0)),
            pl.BlockSpec((MXU, MXU), lambda b, r, i: (0, 0)),
        ],
        out_specs=[
            pl.BlockSpec((1, 1, tmr, W3), lambda b, r, i: (b, r, i, 0)),
            pl.BlockSpec((1, SWA_SPAN, None, 2, SWA_HEADS, SWA_DIM), lambda b, r, i: (b, 0, r, 0, 0, 0)),
        ],
        out_shape=[
            jax.ShapeDtypeStruct((B, dil, M, W3), BF16),
            jax.ShapeDtypeStruct((B, SWA_SPAN, dil, 2, SWA_HEADS, SWA_DIM), F32),
        ],
        compiler_params=_cparams(3),
        name="proj_attn",
    )(hg, w_g, ng_g, bd)
    return p, t.reshape(B, keep, 2, SWA_HEADS, SWA_DIM)


def _proj_plain_kernel(x_ref, lng_ref, w_ref, o_ref, h_scr):
    @pl.when(pl.program_id(1) == 0)
    def _():
        h_scr[...] = _rms(x_ref[...], lng_ref[...]).astype(BF16)

    o_ref[...] = jnp.dot(h_scr[...], w_ref[...], preferred_element_type=F32).astype(o_ref.dtype)


def _proj_plain(x2d, ln_g, w, *, tm, tn, out_dtype, name="proj_plain"):
    N, D = x2d.shape
    C = w.shape[1]
    assert N % tm == 0 and C % tn == 0
    return pl.pallas_call(
        _proj_plain_kernel,
        grid=(N // tm, C // tn),
        in_specs=[
            pl.BlockSpec((tm, D), lambda i, j: (i, 0)),
            pl.BlockSpec((1, D), lambda i, j: (0, 0)),
            pl.BlockSpec((D, tn), lambda i, j: (0, j)),
        ],
        out_specs=pl.BlockSpec((tm, tn), lambda i, j: (i, j)),
        out_shape=jax.ShapeDtypeStruct((N, C), out_dtype),
        scratch_shapes=[pltpu.VMEM((tm, D), BF16)],
        compiler_params=_cparams(2),
        name=name,
    )(x2d, ln_g, w)


def _matmul_kernel(h_ref, w_ref, o_ref):
    o_ref[...] = jnp.dot(h_ref[...], w_ref[...], preferred_element_type=F32).astype(o_ref.dtype)


def _matmul(h2d, w, *, tm, tn, out_dtype, name):
    N, D = h2d.shape
    C = w.shape[1]
    assert N % tm == 0 and C % tn == 0
    return pl.pallas_call(
        _matmul_kernel,
        grid=(N // tm, C // tn),
        in_specs=[pl.BlockSpec((tm, D), lambda i, j: (i, 0)), pl.BlockSpec((D, tn), lambda i, j: (0, j))],
        out_specs=pl.BlockSpec((tm, tn), lambda i, j: (i, j)),
        out_shape=jax.ShapeDtypeStruct((N, C), out_dtype),
        compiler_params=_cparams(2),
        name=name,
    )(h2d, w)


def _attn_kernel(q_ref, kc_ref, vc_ref, kp_ref, vp_ref, o_ref, lse_ref, kk_scr, vv_scr, *, tq):
    i = pl.program_id(1)
    blk = SWA_SPAN
    kk_scr[0:blk, :] = kp_ref[0]
    kk_scr[blk:blk + tq, :] = kc_ref[0]
    vv_scr[0:blk, :] = vp_ref[0]
    vv_scr[blk:blk + tq, :] = vc_ref[0]
    qi = lax.broadcasted_iota(jnp.int32, (blk, 2 * blk), 0)
    ki = lax.broadcasted_iota(jnp.int32, (blk, 2 * blk), 1)
    dist = blk + qi - ki
    band = (dist >= 0) & (dist <= SWA_SPAN)
    band_first = band & ((ki >= blk) | (i > 0))
    lo = lax.broadcasted_iota(jnp.int32, (blk, LANES), 1) < SWA_DIM
    zero = jnp.zeros((blk, LANES), BF16)
    lane = lax.broadcasted_iota(jnp.int32, (blk, LANES), 1)
    for jb in range(tq // blk):
        mask = band_first if jb == 0 else band
        rows = slice(jb * blk, (jb + 1) * blk)
        lse_all = jnp.zeros((blk, LANES), F32)
        for hp in range(SWA_GW // LANES):
            cs = slice(hp * LANES, (hp + 1) * LANES)
            qb = q_ref[0, rows, cs]
            kk = kk_scr[jb * blk:(jb + 2) * blk, cs]
            vv = vv_scr[jb * blk:(jb + 2) * blk, cs]
            res_o = []
            for hh in range(2):
                qm = jnp.where(lo if hh == 0 else jnp.logical_not(lo), qb, zero)
                s = lax.dot_general(qm, kk, (((1,), (1,)), ((), ())), preferred_element_type=F32)
                s = jnp.where(mask, s, -jnp.inf)
                m = jnp.max(s, axis=-1, keepdims=True)
                p = jnp.exp(s - m)
                den = jnp.sum(p, axis=-1, keepdims=True)
                pv = jnp.dot(p.astype(BF16), vv, preferred_element_type=F32)
                res_o.append(pv / den)
                lse_all = jnp.where(lane == 2 * hp + hh, m + jnp.log(den), lse_all)
            o_ref[0, rows, cs] = jnp.where(lo, res_o[0], res_o[1]).astype(BF16)
        lse_ref[0, rows, :] = lse_all


def _attn(p, *, tq):
    S, M, _ = p.shape
    assert M % tq == 0 and tq % SWA_SPAN == 0
    nb = tq // SWA_SPAN
    return pl.pallas_call(
        functools.partial(_attn_kernel, tq=tq),
        grid=(S, M // tq),
        in_specs=[
            pl.BlockSpec((1, tq, SWA_GW), lambda s, i: (s, i, 0)),
            pl.BlockSpec((1, tq, SWA_GW), lambda s, i: (s, i, 1)),
            pl.BlockSpec((1, tq, SWA_GW), lambda s, i: (s, i, 2)),
            pl.BlockSpec((1, SWA_SPAN, SWA_GW), lambda s, i: (s, jnp.maximum(i * nb - 1, 0), 1)),
            pl.BlockSpec((1, SWA_SPAN, SWA_GW), lambda s, i: (s, jnp.maximum(i * nb - 1, 0), 2)),
        ],
        out_specs=[
            pl.BlockSpec((1, tq, SWA_GW), lambda s, i: (s, i, 0)),
            pl.BlockSpec((1, tq, LANES), lambda s, i: (s, i, 0)),
        ],
        out_shape=[
            jax.ShapeDtypeStruct((S, M, SWA_GW), BF16),
            jax.ShapeDtypeStruct((S, M, LANES), F32),
        ],
        scratch_shapes=[
            pltpu.VMEM((SWA_SPAN + tq, SWA_GW), BF16),
            pltpu.VMEM((SWA_SPAN + tq, SWA_GW), BF16),
        ],
        compiler_params=_cparams(2),
        name="swa_attn",
    )(p, p, p, p, p)


def _proj_dn_kernel(h_ref, w_ref, cw_ref, q_ref, k_ref, v_ref, tail_ref, z_scr, carry_scr, *, tm, n_ct):
    i = pl.program_id(1)
    j = pl.program_id(2)
    nh = DN_HEADS
    ncb = z_scr.shape[1] // LANES

    z_scr[0:8, :] = jnp.where(i == 0, 0.0, carry_scr[j])
    z_scr[8:8 + tm, :] = jnp.dot(h_ref[0], w_ref[...], preferred_element_type=F32)
    last = z_scr[tm:tm + 8, :]
    carry_scr[j] = last
    tn = z_scr.shape[1]
    outs = (q_ref, k_ref, v_ref)
    for jj in range(n_ct):

        @pl.when(j == jj)
        def _(jj=jj):
            tail_ref[0, :, jj * tn:(jj + 1) * tn] = last
            for cbl in range(ncb):
                cs = slice(cbl * LANES, (cbl + 1) * LANES)
                part, h = divmod(jj * ncb + cbl, nh)
                xe = z_scr[:, cs]
                acc = (0.5 * cw_ref[0:1, cs]) * xe
                for t in range(1, DN_CONV):
                    acc = (0.5 * cw_ref[t:t + 1, cs]) * xe + pltpu.roll(acc, 1, axis=0)
                half = acc[8:]
                act = half * jnp.tanh(half) + half
                if part < 2:
                    inv = lax.rsqrt(jnp.sum(act * act, axis=-1, keepdims=True) + EPS)
                    act = act * (inv * (DN_DK ** -0.5) if part == 0 else inv)
                outs[part][0, :, h * LANES:(h + 1) * LANES] = act.astype(BF16)


def _proj_dn(h, w_dn, conv_w, *, tm):
    B, L, D = h.shape
    C = w_dn.shape[1]
    width = DN_HEADS * DN_DK
    n_ct = 2
    tn = C // n_ct
    assert L % tm == 0 and C == 3 * width and tn % LANES == 0
    qkv = pl.BlockSpec((1, tm, width), lambda b, i, j: (b, i, 0))
    return pl.pallas_call(
        functools.partial(_proj_dn_kernel, tm=tm, n_ct=n_ct),
        grid=(B, L // tm, n_ct),
        in_specs=[
            pl.BlockSpec((1, tm, D), lambda b, i, j: (b, i, 0)),
            pl.BlockSpec((D, tn), lambda b, i, j: (0, j)),
            pl.BlockSpec((DN_CONV, tn), lambda b, i, j: (0, j)),
        ],
        out_specs=[qkv, qkv, qkv, pl.BlockSpec((1, 8, C), lambda b, i, j: (b, 0, 0))],
        out_shape=[jax.ShapeDtypeStruct((B, L, width), BF16)] * 3 + [jax.ShapeDtypeStruct((B, 8, C), F32)],
        scratch_shapes=[pltpu.VMEM((8 + tm, tn), F32), pltpu.VMEM((n_ct, 8, tn), F32)],
        compiler_params=_cparams(3),
        name="proj_dn",
    )(h, w_dn, conv_w)


def _gates_kernel(ba_ref, par_ref, g_ref, gt_ref, *, tl):
    nh = DN_HEADS
    ba = ba_ref[0]
    lane = lax.broadcasted_iota(jnp.int32, (tl, LANES), 1)
    g = par_ref[0:1, :] * _softplus(ba + par_ref[1:2, :])
    ri = lax.broadcasted_iota(jnp.int32, (tl, tl), 0)
    ci = lax.broadcasted_iota(jnp.int32, (tl, tl), 1)
    tri = jnp.where((ri // DN_CHUNK == ci // DN_CHUNK) & (ci <= ri), 1.0, 0.0).astype(BF16)
    g_hi = g.astype(BF16)
    r1 = g - g_hi.astype(F32)
    g_mid = r1.astype(BF16)
    g_lo = (r1 - g_mid.astype(F32)).astype(BF16)
    gc = sum(jnp.dot(tri, piece, preferred_element_type=F32) for piece in (g_hi, g_mid, g_lo))
    g_ref[0] = jnp.where(lane < nh, _sigmoid(ba), gc)
    gt_ref[0] = jnp.transpose(gc)[nh:2 * nh, :]


def _dn_gates(ba, par, *, tl):
    B, L, _ = ba.shape
    assert L % tl == 0 and tl % DN_CHUNK == 0
    return pl.pallas_call(
        functools.partial(_gates_kernel, tl=tl),
        grid=(B, L // tl),
        in_specs=[pl.BlockSpec((1, tl, LANES), lambda b, i: (b, i, 0)), pl.BlockSpec((2, LANES), lambda b, i: (0, 0))],
        out_specs=[pl.BlockSpec((1, tl, LANES), lambda b, i: (b, i, 0)),
                   pl.BlockSpec((1, DN_HEADS, tl), lambda b, i: (b, 0, i))],
        out_shape=[jax.ShapeDtypeStruct((B, L, LANES), F32), jax.ShapeDtypeStruct((B, DN_HEADS, L), F32)],
        compiler_params=_cparams(2),
        name="dn_gates",
    )(ba, par)


def _intra_kernel(q_ref, k_ref, v_ref, g_ref, gt_in_ref, u_ref, w_ref, qd_ref, kd_ref, a_ref, gt_ref, *, tl):
    h = pl.program_id(1)
    C = DN_CHUNK
    lane = lax.broadcasted_iota(jnp.int32, (C, LANES), 1)
    ri = lax.broadcasted_iota(jnp.int32, (C, C), 0)
    ci = lax.broadcasted_iota(jnp.int32, (C, C), 1)
    eye = jnp.where(ri == ci, 1.0, 0.0).astype(F32)
    nt_dot = lambda a, b: lax.dot_general(a.astype(BF16), b.astype(BF16), (((1,), (1,)), ((), ())),
                                          preferred_element_type=F32)
    rows = [slice(c * C, (c + 1) * C) for c in range(tl // C)]
    gv = [g_ref[0, r, :] for r in rows]
    q = [q_ref[0, r, :].astype(F32) for r in rows]
    k = [k_ref[0, r, :].astype(F32) for r in rows]
    v = [v_ref[0, r, :].astype(F32) for r in rows]
    beta = [jnp.sum(jnp.where(lane == h, x, 0.0), axis=-1, keepdims=True) for x in gv]
    gc = [jnp.sum(jnp.where(lane == h + DN_HEADS, x, 0.0), axis=-1, keepdims=True) for x in gv]
    gc_row = gt_in_ref[0, pl.ds(h, 1), :]
    decay = [jnp.exp(jnp.where(ri >= ci, a - gc_row[:, r], -jnp.inf)) for a, r in zip(gc, rows)]
    kb = [a * b for a, b in zip(k, beta)]
    kq = [nt_dot(jnp.concatenate([a, b], axis=0), c) for a, b, c in zip(kb, q, k)]
    x = [-jnp.where(ri > ci, m[:C] * d, 0.0) for m, d in zip(kq, decay)]
    t = [eye + a for a in x]
    x = [_bdot(a, a) for a in x]
    for _ in range(4):
        both = [_bdot(jnp.concatenate([a, b], axis=0), a) for a, b in zip(x, t)]
        t = [b + m[C:] for b, m in zip(t, both)]
        x = [m[:C] for m in both]
    t = [b + _bdot(b, a) for a, b in zip(x, t)]
    eg = [jnp.exp(a) for a in gc]
    glast = [a[C - 1:C, :] for a in gc]
    uw = [_bdot(a, jnp.concatenate([b * c, d * e], axis=1)) for a, b, c, d, e in zip(t, v, beta, kb, eg)]
    for c, r in enumerate(rows):
        u_ref[0, 0, r, :] = uw[c][:, :LANES]
        w_ref[0, 0, r, :] = uw[c][:, LANES:].astype(BF16)
        a_ref[0, 0, r, :] = (kq[c][C:] * decay[c]).astype(BF16)
        qd_ref[0, 0, r, :] = (q[c] * eg[c]).astype(BF16)
        kd_ref[0, 0, r, :] = (k[c] * jnp.exp(glast[c] - gc[c])).astype(BF16)
        gt_ref[0, 0, c:c + 1, :] = jnp.broadcast_to(jnp.exp(glast[c]), (1, LANES))


def _dn_intra(q, k, v, g, gt_rows, *, tl):
    B, L, _ = q.shape
    H, C = DN_HEADS, DN_CHUNK
    assert L % tl == 0 and (tl // C) % 8 == 0
    qkv_spec = pl.BlockSpec((1, tl, LANES), lambda b, h, i: (b, i, h))
    hl = lambda w: pl.BlockSpec((1, 1, tl, w), lambda b, h, i: (b, h, i, 0))
    return pl.pallas_call(
        functools.partial(_intra_kernel, tl=tl),
        grid=(B, H, L // tl),
        in_specs=[qkv_spec, qkv_spec, qkv_spec, pl.BlockSpec((1, tl, LANES), lambda b, h, i: (b, i, 0)),
                  pl.BlockSpec((1, H, tl), lambda b, h, i: (b, 0, i))],
        out_specs=[hl(LANES), hl(LANES), hl(LANES), hl(LANES), hl(C),
                   pl.BlockSpec((1, 1, tl // C, LANES), lambda b, h, i: (b, h, i, 0))],
        out_shape=[
            jax.ShapeDtypeStruct((B, H, L, LANES), F32),
            jax.ShapeDtypeStruct((B, H, L, LANES), BF16),
            jax.ShapeDtypeStruct((B, H, L, LANES), BF16),
            jax.ShapeDtypeStruct((B, H, L, LANES), BF16),
            jax.ShapeDtypeStruct((B, H, L, C), BF16),
            jax.ShapeDtypeStruct((B, H, L // C, LANES), F32),
        ],
        compiler_params=_cparams(3),
        name="dn_intra",
    )(q, k, v, g, gt_rows)


def _scan_kernel(u_ref, w_ref, qd_ref, kd_ref, a_ref, gt_ref, s0_ref, o_ref, s_ref, *, n_chunks):
    C = DN_CHUNK
    bb, H = s_ref.shape[0], s_ref.shape[1]
    seqs = [(b, h) for b in range(bb) for h in range(H)]

    @pl.when(pl.program_id(1) == 0)
    def _():
        s_ref[...] = s0_ref[...]

    def body(c, carry):
        rows = pl.ds(pl.multiple_of(c * C, C), C)
        S = [s_ref[b, h] for b, h in seqs]
        Sb = [x.astype(BF16) for x in S]
        v_new = [u_ref[b, h, rows, :] - jnp.dot(w_ref[b, h, rows, :], sb, preferred_element_type=F32)
                 for (b, h), sb in zip(seqs, Sb)]
        vb = [x.astype(BF16) for x in v_new]
        o = [jnp.dot(qd_ref[b, h, rows, :], sb, preferred_element_type=F32)
             + jnp.dot(a_ref[b, h, rows, :], v, preferred_element_type=F32) for (b, h), sb, v in zip(seqs, Sb, vb)]
        upd = [lax.dot_general(kd_ref[b, h, rows, :], v, (((0,), (0,)), ((), ())), preferred_element_type=F32)
               for (b, h), v in zip(seqs, vb)]
        for n, (b, h) in enumerate(seqs):
            o_ref[b, rows, h * LANES:(h + 1) * LANES] = o[n]
            s_ref[b, h] = S[n] * gt_ref[b, h, pl.ds(c, 1), :] + upd[n]
        return carry

    lax.fori_loop(0, n_chunks, body, 0)


def _dn_scan(u, w, qd, kd, a, gt, s0, *, tl, bb):
    B, H, L, _ = u.shape
    C = DN_CHUNK
    assert L % tl == 0 and (tl // C) % 8 == 0 and B % bb == 0
    hs = lambda wd: pl.BlockSpec((bb, H, tl, wd), lambda b, i: (b, 0, i, 0))
    s_spec = pl.BlockSpec((bb, H, DN_DK, LANES), lambda b, i: (b, 0, 0, 0))
    return pl.pallas_call(
        functools.partial(_scan_kernel, n_chunks=tl // C),
        grid=(B // bb, L // tl),
        in_specs=[hs(LANES), hs(LANES), hs(LANES), hs(LANES), hs(C),
                  pl.BlockSpec((bb, H, tl // C, LANES), lambda b, i: (b, 0, i, 0)), s_spec],
        out_specs=[pl.BlockSpec((bb, tl, H * LANES), lambda b, i: (b, i, 0)), s_spec],
        out_shape=[jax.ShapeDtypeStruct((B, L, H * LANES), F32),
                   jax.ShapeDtypeStruct((B, H, DN_DK, LANES), F32)],
        compiler_params=_cparams(2),
        name="dn_scan",
    )(u, w, qd, kd, a, gt, s0)


def _gated_mix(o_a, od, gates, dng, wa, wb, wo, x, dot):
    width = DN_HEADS * DN_DK
    parts = []
    for h in range(DN_HEADS):
        blk = od[:, h * LANES:(h + 1) * LANES]
        parts.append(blk * lax.rsqrt(jnp.mean(blk * blk, axis=-1, keepdims=True) + EPS) * dng)
    odn = jnp.concatenate(parts, axis=-1) * _silu(gates[:, 0:width].astype(F32))
    ya = dot(o_a, wa)
    yb = dot(odn, wb)
    mix = _sigmoid(gates[:, width:2 * width].astype(F32)) * ya + _sigmoid(gates[:, 2 * width:].astype(F32)) * yb
    return x + dot(mix, wo)


def _out_kernel(x_ref, o0, o1, o2, l0, l1, l2, od_ref, gates_ref, dng_ref, wa_ref, wb_ref, wo_ref, e_ref, y_ref,
                so0, so1, so2, sl0, sl1, sl2, *, tm, dils):
    o_refs, l_refs = (o0, o1, o2), (l0, l1, l2)
    so, sl = (so0, so1, so2), (sl0, sl1, sl2)
    for gi, d in enumerate(dils):
        for r in range(d):
            dst = slice(None) if d == 1 else pl.ds(r, tm // d, stride=d)
            sl[gi][dst, :] = l_refs[gi][0, r]
            for cb in range(SWA_GW // LANES):
                so[gi][cb, dst, :] = o_refs[gi][0, r, :, cb * LANES:(cb + 1) * LANES].astype(F32)
    ls = [s[...] for s in sl]
    m = jnp.maximum(jnp.maximum(ls[0], ls[1]), ls[2])
    es = [jnp.exp(l - m) for l in ls]
    tot = es[0] + es[1] + es[2]
    alphas = [jnp.dot((e / tot).astype(BF16), e_ref[...], preferred_element_type=F32) for e in es]
    parts = []
    for cb in range(SWA_GW // LANES):
        cs = slice(cb * LANES, (cb + 1) * LANES)
        parts.append(alphas[0][:, cs] * so[0][cb] + alphas[1][:, cs] * so[1][cb] + alphas[2][:, cs] * so[2][cb])
    o_a = jnp.concatenate(parts, axis=-1)
    y_ref[...] = _gated_mix(o_a, od_ref[...], gates_ref[...], dng_ref[...], wa_ref[...], wb_ref[...],
                            wo_ref[...], x_ref[...], _bdot)


def _out_proj(x2d, os_, ls_, od2d, gates, dng, wa, wb, wo, e_att, *, B, L, tm):
    N, D = x2d.shape
    nt = L // tm
    dils = tuple(d for _, d in SWA_CONFIGS)
    grp = lambda d, w: pl.BlockSpec((1, d, tm // d, w), lambda i: (i // nt, 0, i % nt, 0))
    row = lambda w: pl.BlockSpec((tm, w), lambda i: (i, 0))
    full = lambda a: pl.BlockSpec(a.shape, lambda i: (0, 0))
    return pl.pallas_call(
        functools.partial(_out_kernel, tm=tm, dils=dils),
        grid=(N // tm,),
        in_specs=[row(D)] + [grp(d, SWA_GW) for d in dils] + [grp(d, LANES) for d in dils]
        + [row(od2d.shape[1]), row(gates.shape[1]), full(dng), full(wa), full(wb), full(wo), full(e_att)],
        out_specs=row(D),
        out_shape=jax.ShapeDtypeStruct((N, D), F32),
        scratch_shapes=[pltpu.VMEM((SWA_GW // LANES, tm, LANES), F32)] * 3 + [pltpu.VMEM((tm, LANES), F32)] * 3,
        compiler_params=_cparams(1),
        name="out_proj",
    )(x2d, *os_, *ls_, od2d, gates, dng, wa, wb, wo, e_att)


def _router_kernel(x_ref, lng_ref, wr_ref, br_ref, info_ref, cnt_ref, base_scr, *, tm):
    i = pl.program_id(0)

    @pl.when(i == 0)
    def _():
        base_scr[...] = jnp.zeros_like(base_scr)

    h = _rms(x_ref[...], lng_ref[...])
    lg = _bdot(h, wr_ref[...]) + br_ref[...]
    lane = lax.broadcasted_iota(jnp.int32, (tm, LANES), 1)
    big = jnp.int32(1 << 20)
    ninf = -jnp.inf

    def argmax_lane(vals):
        mx = jnp.max(vals, axis=-1, keepdims=True)
        idx = jnp.min(jnp.where(vals == mx, lane, big), axis=-1, keepdims=True)
        return mx, idx

    lgm = jnp.where(lane < N_GROUPS, lg, ninf)
    mg, gsel = argmax_lane(lgm)
    pg = 1.0 / jnp.sum(jnp.exp(lgm - mg), axis=-1, keepdims=True)
    start = N_GROUPS + gsel * PER_GROUP
    le = jnp.where((lane >= start) & (lane < start + PER_GROUP), lg, ninf)
    m1, i1 = argmax_lane(le)
    m2, i2 = argmax_lane(jnp.where(lane == i1, ninf, le))
    e21 = jnp.exp(m2 - m1)
    w1 = pg / (1.0 + e21)
    w2 = pg * e21 / (1.0 + e21)
    oh = jnp.where(lane == i1, 1.0, 0.0) + jnp.where(lane == i2, 1.0, 0.0)
    ri = lax.broadcasted_iota(jnp.int32, (tm, tm), 0)
    ci = lax.broadcasted_iota(jnp.int32, (tm, tm), 1)
    strict = jnp.where(ci < ri, 1.0, 0.0).astype(BF16)
    pref = jnp.dot(strict, oh.astype(BF16), preferred_element_type=F32) + base_scr[...]
    r1 = jnp.sum(jnp.where(lane == i1, pref, 0.0), axis=-1, keepdims=True)
    r2 = jnp.sum(jnp.where(lane == i2, pref, 0.0), axis=-1, keepdims=True)
    base_scr[...] = base_scr[...] + jnp.sum(oh, axis=0, keepdims=True)
    cnt_ref[...] = base_scr[...]
    off = jnp.float32(N_GROUPS)
    info = jnp.where(lane == 0, i1.astype(F32) - off, 0.0)
    info = jnp.where(lane == 1, i2.astype(F32) - off, info)
    info = jnp.where(lane == 2, w1, info)
    info = jnp.where(lane == 3, w2, info)
    info = jnp.where(lane == 4, r1, info)
    info = jnp.where(lane == 5, r2, info)
    info_ref[...] = info


def _router(x2d, ln_g, wr, br, *, tm):
    N, D = x2d.shape
    assert N % tm == 0
    return pl.pallas_call(
        functools.partial(_router_kernel, tm=tm),
        grid=(N // tm,),
        in_specs=[
            pl.BlockSpec((tm, D), lambda i: (i, 0)),
            pl.BlockSpec((1, D), lambda i: (0, 0)),
            pl.BlockSpec((D, LANES), lambda i: (0, 0)),
            pl.BlockSpec((1, LANES), lambda i: (0, 0)),
        ],
        out_specs=[pl.BlockSpec((tm, LANES), lambda i: (i, 0)), pl.BlockSpec((1, LANES), lambda i: (0, 0))],
        out_shape=[jax.ShapeDtypeStruct((N, LANES), F32), jax.ShapeDtypeStruct((1, LANES), F32)],
        scratch_shapes=[pltpu.VMEM((1, LANES), F32)],
        compiler_params=_cparams(1),
        name="router",
    )(x2d, ln_g, wr, br)


def _dispatch_kernel(dest_ref, zb_ref, x_ref, xs_ref, zero_scr, rows_scr, sem, *, tm, tb, n_zb, n_tiles):
    i = pl.program_id(0)

    @pl.when(i == 0)
    def _():
        zero_scr[...] = jnp.zeros_like(zero_scr)

        def zero_copy(n):
            return pltpu.make_async_copy(zero_scr, xs_ref.at[pl.ds(zb_ref[n] * tb, tb)], sem.at[2])

        def zero_issue(n, carry):
            @pl.when(zb_ref[n] >= 0)
            def _():
                zero_copy(n).start()

            return carry

        def zero_wait(n, carry):
            @pl.when(zb_ref[n] >= 0)
            def _():
                zero_copy(n).wait()

            return carry

        lax.fori_loop(0, n_zb, zero_issue, 0)
        lax.fori_loop(0, n_zb, zero_wait, 0)

    buf_now = lax.rem(i, 2)
    rows_scr[buf_now] = x_ref[...].reshape(rows_scr.shape[1:])

    def row_copy(tile, t, slot):
        buf = lax.rem(tile, 2)
        return pltpu.make_async_copy(
            rows_scr.at[buf, pl.ds(t, 1)],
            xs_ref.at[pl.ds(dest_ref[(tile * tm + t) * TOP_K + slot], 1)], sem.at[buf])

    def issue(tt, carry):
        for r in range(ROW_UNROLL):
            for slot in range(TOP_K):
                row_copy(i, tt * ROW_UNROLL + r, slot).start(priority=slot)
        return carry

    def drain(tile):
        buf = lax.rem(tile, 2)
        for _ in range(TOP_K):
            pltpu.make_async_copy(rows_scr.at[buf], rows_scr.at[buf], sem.at[buf]).wait()

    lax.fori_loop(0, tm // ROW_UNROLL, issue, 0)

    @pl.when(i > 0)
    def _():
        drain(i - 1)

    @pl.when(i == n_tiles - 1)
    def _():
        drain(i)


def _dispatch(dest, zero_blocks, x2d, *, tm, tb, n_rows):
    N, D = x2d.shape
    return pl.pallas_call(
        functools.partial(_dispatch_kernel, tm=tm, tb=tb, n_zb=zero_blocks.shape[0], n_tiles=N // tm),
        grid_spec=pltpu.PrefetchScalarGridSpec(
            num_scalar_prefetch=2,
            grid=(N // tm,),
            in_specs=[pl.BlockSpec((tm, D), lambda i, d, z: (i, 0))],
            out_specs=pl.BlockSpec(memory_space=pl.ANY),
            scratch_shapes=[pltpu.VMEM((tb, D // LANES, LANES), F32), pltpu.VMEM((2, tm, D // LANES, LANES), F32),
                            pltpu.SemaphoreType.DMA((3,))],
        ),
        out_shape=jax.ShapeDtypeStruct((n_rows, D // LANES, LANES), F32),
        compiler_params=_cparams(1),
        name="moe_dispatch",
    )(dest, zero_blocks, x2d)


def _ffn_kernel(be_ref, nb_ref, xs_ref, lng_ref, wg_ref, wu_ref, wd_ref, y_ref, wg_scr, wu_scr, wd_scr):
    i = pl.program_id(0)
    used = i < nb_ref[0]
    tb = xs_ref.shape[0]

    @pl.when(jnp.logical_or(i == 0, be_ref[i] != be_ref[jnp.maximum(i - 1, 0)]))
    def _():
        wg_scr[...] = wg_ref[0].astype(BF16)
        wu_scr[...] = wu_ref[0].astype(BF16)
        wd_scr[...] = wd_ref[0].astype(BF16)

    @pl.when(used)
    def _():
        h = _rms(xs_ref[...].reshape(tb, -1), lng_ref[...]).astype(BF16)
        g = jnp.dot(h, wg_scr[...], preferred_element_type=F32)
        u = jnp.dot(h, wu_scr[...], preferred_element_type=F32)
        y = jnp.dot((_silu(g) * u).astype(BF16), wd_scr[...], preferred_element_type=F32)
        y_ref[...] = y.reshape(y_ref.shape)

    @pl.when(jnp.logical_not(used))
    def _():
        y_ref[...] = jnp.zeros_like(y_ref)


def _ffn(blk_e, nb_used, xs, ln_g, wg, wu, wd, layer, *, tb):
    P, S, _ = xs.shape
    D = S * LANES
    nb = P // tb
    DE = wg.shape[3]
    return pl.pallas_call(
        _ffn_kernel,
        grid_spec=pltpu.PrefetchScalarGridSpec(
            num_scalar_prefetch=2,
            grid=(nb,),
            in_specs=[
                pl.BlockSpec((tb, S, LANES), lambda i, be, nbu: (jnp.minimum(i, nbu[0] - 1), 0, 0)),
                pl.BlockSpec((1, D), lambda i, be, nbu: (0, 0)),
                pl.BlockSpec((None, 1, D, DE), lambda i, be, nbu: (layer, be[i], 0, 0)),
                pl.BlockSpec((None, 1, D, DE), lambda i, be, nbu: (layer, be[i], 0, 0)),
                pl.BlockSpec((None, 1, DE, D), lambda i, be, nbu: (layer, be[i], 0, 0)),
            ],
            out_specs=pl.BlockSpec((tb, S, LANES), lambda i, be, nbu: (i, 0, 0)),
            scratch_shapes=[pltpu.VMEM((D, DE), BF16), pltpu.VMEM((D, DE), BF16), pltpu.VMEM((DE, D), BF16)],
        ),
        out_shape=jax.ShapeDtypeStruct((P, S, LANES), F32),
        compiler_params=_cparams(1),
        name="moe_ffn",
    )(blk_e, nb_used, xs, ln_g, wg, wu, wd)


def _combine_kernel(dest_ref, x_ref, info_ref, yb_ref, y_ref, g_scr, sem, *, tm, n_tiles):
    i = pl.program_id(0)

    def row_copy(tile, t, slot):
        buf = lax.rem(tile, 2)
        return pltpu.make_async_copy(
            yb_ref.at[pl.ds(dest_ref[(tile * tm + t) * TOP_K + slot], 1)],
            g_scr.at[buf, slot, pl.ds(t, 1)], sem.at[buf])

    def issue_tile(tile):
        def body(tt, carry):
            for r in range(ROW_UNROLL):
                for slot in range(TOP_K):
                    row_copy(tile, tt * ROW_UNROLL + r, slot).start(priority=slot)
            return carry

        lax.fori_loop(0, tm // ROW_UNROLL, body, 0)

    @pl.when(i == 0)
    def _():
        issue_tile(i)

    @pl.when(i + 1 < n_tiles)
    def _():
        issue_tile(i + 1)

    buf = lax.rem(i, 2)
    pltpu.make_async_copy(g_scr.at[buf], g_scr.at[buf], sem.at[buf]).wait()
    info = info_ref[...]
    lane = lax.broadcasted_iota(jnp.int32, info.shape, 1)
    w1 = jnp.sum(jnp.where(lane == 2, info, 0.0), axis=-1, keepdims=True)
    w2 = jnp.sum(jnp.where(lane == 3, info, 0.0), axis=-1, keepdims=True)
    g1 = g_scr[buf, 0].reshape(x_ref.shape)
    g2 = g_scr[buf, 1].reshape(x_ref.shape)
    y_ref[...] = x_ref[...] + (w1 * g1 + w2 * g2)


def _combine(dest, x2d, info, yb, *, tm):
    N, D = x2d.shape
    return pl.pallas_call(
        functools.partial(_combine_kernel, tm=tm, n_tiles=N // tm),
        grid_spec=pltpu.PrefetchScalarGridSpec(
            num_scalar_prefetch=1,
            grid=(N // tm,),
            in_specs=[
                pl.BlockSpec((tm, D), lambda i, d: (i, 0)),
                pl.BlockSpec((tm, LANES), lambda i, d: (i, 0)),
                pl.BlockSpec(memory_space=pl.ANY),
            ],
            out_specs=pl.BlockSpec((tm, D), lambda i, d: (i, 0)),
            scratch_shapes=[pltpu.VMEM((2, TOP_K, tm, D // LANES, LANES), F32), pltpu.SemaphoreType.DMA((2,))],
        ),
        out_shape=jax.ShapeDtypeStruct((N, D), F32),
        compiler_params=_cparams(1),
        name="moe_combine",
    )(dest, x2d, info, yb)


def _moe(x2d, ln2_g, wr, br, wg, wu, wd, layer, *, tm):
    N, D = x2d.shape
    tb = MOE_ROWS if N * TOP_K >= N_EXPERTS * MOE_ROWS else MOE_ROWS_SMALL
    info, counts = _router(x2d, ln2_g, wr, br, tm=ROUTER_ROWS if N % ROUTER_ROWS == 0 else tm)
    counts = counts[0, N_GROUPS:N_GROUPS + N_EXPERTS].astype(jnp.int32)
    pcounts = (counts + tb - 1) // tb * tb
    pend = jnp.cumsum(pcounts)
    pstart = pend - pcounts
    e = info[:, 0:TOP_K].astype(jnp.int32)
    rank = info[:, 4:4 + TOP_K].astype(jnp.int32)
    experts = jnp.arange(N_EXPERTS, dtype=jnp.int32)
    dest = (jnp.sum(jnp.where(e[..., None] == experts, pstart, 0), axis=-1) + rank).reshape(-1)
    nb = -(-(N * TOP_K) // tb) + N_EXPERTS
    P = nb * tb
    blocks = jnp.arange(nb, dtype=jnp.int32)
    blk_e = jnp.minimum(jnp.sum((pend[None, :] <= blocks[:, None] * tb).astype(jnp.int32), axis=1), N_EXPERTS - 1)
    nb_used = (pend[-1] // tb).astype(jnp.int32).reshape(1)
    zero_blocks = jnp.concatenate([jnp.where(counts % tb != 0, pend // tb - 1, -1),
                                   jnp.where(blocks >= nb_used[0], blocks, -1)]).astype(jnp.int32)
    xs = _dispatch(dest, zero_blocks, x2d, tm=tm, tb=tb, n_rows=P)
    yb = _ffn(blk_e, nb_used, xs, ln2_g, wg, wu, wd, layer, tb=tb)
    return _combine(dest, x2d, info, yb, tm=tm)


def _rows8(x):
    return jnp.broadcast_to(x, (8, x.shape[1]))


def _row_hdot(x, m):
    return _hdot(_rows8(x), m)[0:1]


def _bf_round(x):
    return x.astype(BF16).astype(F32)


def _sample_attn_kernel(z_ref, c0, c1, c2, qg_ref, kg_ref, oa_ref, kv_ref):
    W = SWA_GW
    scale = SWA_DIM ** -0.5
    z = z_ref[0]
    sub = lax.broadcasted_iota(jnp.int32, (SWA_HEADS, W), 0)
    lane = lax.broadcasted_iota(jnp.int32, (SWA_HEADS, W), 1)
    own = lane // SWA_DIM == sub

    def heads(row):
        return jnp.where(own, jnp.broadcast_to(row, (SWA_HEADS, W)), 0.0)

    def head_sum(row):
        return jnp.sum(heads(row), axis=-1, keepdims=True)

    def spread(col):
        return jnp.sum(jnp.where(own, col, 0.0), axis=0, keepdims=True)

    def headnorm(zz, g):
        return zz * spread(lax.rsqrt(head_sum(zz * zz) * (1.0 / SWA_DIM) + EPS)) * g

    outs, lses = [], []
    for gi, (c_ref, (win, dil)) in enumerate(zip((c0, c1, c2), SWA_CONFIGS)):
        q = headnorm(z[:, gi * W:(gi + 1) * W], qg_ref[gi:gi + 1, :])
        k = headnorm(z[:, 3 * W + gi * W:3 * W + (gi + 1) * W], kg_ref[gi:gi + 1, :])
        v = z[:, 6 * W + gi * W:6 * W + (gi + 1) * W]
        kv_ref[0, :, 2 * gi * W:(2 * gi + 1) * W] = k
        kv_ref[0, :, (2 * gi + 1) * W:(2 * gi + 2) * W] = v
        kc = c_ref[0].reshape(W, win).astype(BF16)
        vc = c_ref[1].reshape(W, win).astype(BF16)
        s_c = jnp.dot(heads(q).astype(BF16), kc, preferred_element_type=F32) * scale
        row = lax.broadcasted_iota(jnp.int32, s_c.shape, 1)
        s_c = jnp.where(row % dil == 0, s_c, -jnp.inf)
        s_n = head_sum(_bf_round(k) * _bf_round(q)) * scale
        m = jnp.maximum(jnp.max(s_c, axis=-1, keepdims=True), s_n)
        p_c = jnp.exp(s_c - m)
        p_n = jnp.exp(s_n - m)
        den = jnp.sum(p_c, axis=-1, keepdims=True) + p_n
        pv = lax.dot_general(p_c.astype(BF16), vc, (((1,), (1,)), ((), ())), preferred_element_type=F32)
        num = jnp.sum(jnp.where(own, pv, 0.0), axis=0, keepdims=True) + spread(_bf_round(p_n)) * _bf_round(v)
        outs.append(num / spread(den))
        lses.append(m + jnp.log(den))
    mm = jnp.maximum(jnp.maximum(lses[0], lses[1]), lses[2])
    es = [jnp.exp(l - mm) for l in lses]
    tot = es[0] + es[1] + es[2]
    oa_ref[0] = sum(spread(_bf_round(e / tot)) * _bf_round(o) for e, o in zip(es, outs))


def _sample_attn(z3, caches, layer, qg, kg):
    Bs = z3.shape[0]
    W = SWA_GW
    cviews, cspecs = [], []
    for (win, dil), c in zip(SWA_CONFIGS, caches):
        assert c.shape[2] == win
        cviews.append(jnp.transpose(c, (0, 1, 3, 4, 5, 2)))
        cspecs.append(pl.BlockSpec((None, None, 2, SWA_HEADS, SWA_DIM, win), lambda b: (layer, b, 0, 0, 0, 0)))
    full = lambda a: pl.BlockSpec(a.shape, lambda b: (0,) * a.ndim)
    return pl.pallas_call(
        _sample_attn_kernel,
        grid=(Bs,),
        in_specs=[pl.BlockSpec((1, 1, 9 * W), lambda b: (b, 0, 0))] + cspecs + [full(qg), full(kg)],
        out_specs=[pl.BlockSpec((1, 1, W), lambda b: (b, 0, 0)), pl.BlockSpec((1, 1, 6 * W), lambda b: (b, 0, 0))],
        out_shape=[jax.ShapeDtypeStruct((Bs, 1, W), F32), jax.ShapeDtypeStruct((Bs, 1, 6 * W), F32)],
        compiler_params=_cparams(1),
        name="sample_attn",
    )(z3, *cviews, qg, kg)


def _sample_dn_kernel(raw_ref, cs_ref, cw_ref, ba_ref, par_ref, s_ref, e_ref, etb_ref, etg_ref, o_ref, so_ref):
    E, ETB, ETG = e_ref[...], etb_ref[...], etg_ref[...]
    width = DN_HEADS * DN_DK
    conv = cw_ref[DN_CONV - 1:DN_CONV, :] * raw_ref[0]
    for t in range(DN_CONV - 1):
        conv = conv + cw_ref[t:t + 1, :] * cs_ref[0, t:t + 1, :]
    act = _silu(conv)

    def l2(zz):
        return zz * _row_hdot(lax.rsqrt(_row_hdot(zz * zz, E) + EPS), ETB)

    qn = l2(act[:, 0:width]) * (DN_DK ** -0.5)
    kn = l2(act[:, width:2 * width])
    vn = act[:, 2 * width:3 * width]
    ba = ba_ref[0]
    beta = _row_hdot(_sigmoid(ba), ETB)
    eg = jnp.exp(_row_hdot(par_ref[0:1, :] * _softplus(ba + par_ref[1:2, :]), ETG))
    row0 = lax.broadcasted_iota(jnp.int32, (8, LANES), 0) == 0
    for h in range(DN_HEADS):
        sl = slice(h * LANES, (h + 1) * LANES)
        S = s_ref[0, h]
        q, k, v, b, e = qn[:, sl], kn[:, sl], vn[:, sl], beta[:, sl], eg[:, sl]
        Sb = S.astype(BF16)
        wq = jnp.concatenate([k * b * e, q * e, jnp.zeros((6, LANES), F32)], axis=0)
        both = jnp.dot(wq.astype(BF16), Sb, preferred_element_type=F32)
        v_new = v * b - both[0:1]
        a = jnp.sum(q * k, axis=-1, keepdims=True)
        o_ref[0, :, sl] = both[1:2] + a * v_new
        k8 = jnp.where(row0, _rows8(k), 0.0)
        upd = lax.dot_general(k8, _rows8(v_new), (((0,), (0,)), ((), ())), preferred_element_type=F32, precision=HI)
        so_ref[0, h] = S * e + upd


def _sample_dn(raw3, conv_state, s0, layer, conv_w, ba3, par, e_mat, etb, etg):
    Bs, _, C = raw3.shape
    H = DN_HEADS
    full = lambda a: pl.BlockSpec(a.shape, lambda b: (0,) * a.ndim)
    return pl.pallas_call(
        _sample_dn_kernel,
        grid=(Bs,),
        in_specs=[pl.BlockSpec((1, 1, C), lambda b: (b, 0, 0)),
                  pl.BlockSpec((None, 1, DN_CONV - 1, C), lambda b: (layer, b, 0, 0)),
                  full(conv_w),
                  pl.BlockSpec((1, 1, LANES), lambda b: (b, 0, 0)),
                  full(par),
                  pl.BlockSpec((None, 1, H, DN_DK, LANES), lambda b: (layer, b, 0, 0, 0)),
                  full(e_mat), full(etb), full(etg)],
        out_specs=[pl.BlockSpec((1, 1, H * LANES), lambda b: (b, 0, 0)),
                   pl.BlockSpec((1, H, DN_DK, LANES), lambda b: (b, 0, 0, 0))],
        out_shape=[jax.ShapeDtypeStruct((Bs, 1, H * LANES), F32), jax.ShapeDtypeStruct(s0.shape[1:], F32)],
        compiler_params=_cparams(1),
        name="sample_dn",
    )(raw3, conv_state, conv_w, ba3, par, s0, e_mat, etb, etg)


def _sample_out_kernel(x_ref, oa_ref, od_ref, gates_ref, dng_ref, wa_ref, wb_ref, wo_ref, y_ref):
    y_ref[...] = _gated_mix(oa_ref[...], od_ref[...], gates_ref[...], dng_ref[...], wa_ref[...], wb_ref[...],
                            wo_ref[...], x_ref[...], _bdot)


def _sample_out(x2d, oa, od, gates, dng, wa, wb, wo):
    args = (x2d, oa, od, gates, dng, wa, wb, wo)
    return pl.pallas_call(
        _sample_out_kernel,
        grid=(1,),
        in_specs=[pl.BlockSpec(a.shape, lambda i: (0, 0)) for a in args],
        out_specs=pl.BlockSpec(x2d.shape, lambda i: (0, 0)),
        out_shape=jax.ShapeDtypeStruct(x2d.shape, F32),
        compiler_params=_cparams(1),
        name="sample_out",
    )(*args)


def _head_indicator(width, head):
    c = jnp.arange(width)[:, None] // head
    return (c == jnp.arange(LANES)[None, :]).astype(F32)


def _prep_layer(l, ln1_g, w_in, q_norm_g, k_norm_g, dn_conv_w, dn_a_log, dn_dt_bias, dn_norm_g, w_out_a, w_out_b,
                w_o, ln2_g, w_rg, b_rg, w_re, b_re, w_e_gate, w_e_up, w_e_down):
    D = w_in.shape[1]
    a_w = 3 * 3 * SWA_GW
    dn_w = DN_HEADS * 3 * DN_DK
    hv = DN_HEADS * DN_DK
    w = w_in[l]
    splits = dict(att=w[:, :a_w], dn=w[:, a_w:a_w + dn_w],
                  ba=jnp.pad(w[:, a_w + dn_w:a_w + dn_w + 2 * DN_HEADS], ((0, 0), (0, LANES - 2 * DN_HEADS))),
                  gate=w[:, a_w + dn_w + 2 * DN_HEADS:])
    assert splits["gate"].shape[1] == hv + 2 * D
    tile_heads = lambda g: jnp.broadcast_to(g[:, None, :], (len(SWA_CONFIGS), SWA_HEADS, SWA_DIM)).reshape(len(SWA_CONFIGS), SWA_GW)
    qg, kg = tile_heads(q_norm_g[l]), tile_heads(k_norm_g[l])
    idx = jnp.arange(MXU) // SWA_DIM
    n_g = len(SWA_CONFIGS)
    par = jnp.zeros((2, LANES), F32)
    par = par.at[0, DN_HEADS:2 * DN_HEADS].set(-jnp.exp(dn_a_log[l].astype(F32)))
    par = par.at[1, DN_HEADS:2 * DN_HEADS].set(dn_dt_bias[l].astype(F32))
    wr = jnp.pad(jnp.concatenate([w_rg[l], w_re[l]], axis=1), ((0, 0), (0, LANES - N_GROUPS - N_EXPERTS)))
    br = jnp.pad(jnp.concatenate([b_rg[l], b_re[l]]), (0, LANES - N_GROUPS - N_EXPERTS)).reshape(1, LANES)
    e8 = _head_indicator(hv, DN_DK)
    return dict(
        bf16={k: v.astype(BF16) for k, v in splits.items()},
        ln1=ln1_g[l].reshape(1, D), ln2=ln2_g[l].reshape(1, D),
        qg=qg, kg=kg,
        w_grp=[jnp.concatenate([w[:, s * n_g * SWA_GW + g * SWA_GW:s * n_g * SWA_GW + (g + 1) * SWA_GW]
                                for s in range(3)], axis=1).astype(BF16) for g in range(n_g)],
        ng_grp=[jnp.concatenate([qg[g] * SWA_DIM ** -0.5, kg[g]]).reshape(1, 1, 2 * SWA_GW) for g in range(n_g)],
        bd=((idx[:, None] == idx[None, :]).astype(F32) / SWA_DIM).astype(BF16),
        conv_w=dn_conv_w[l], par=par, dng=dn_norm_g[l].reshape(1, DN_DK),
        wa=w_out_a[l].astype(BF16), wb=w_out_b[l].astype(BF16), wo=w_o[l].astype(BF16), wr=wr.astype(BF16), br=br,
        wg=w_e_gate, wu=w_e_up, wd=w_e_down, layer=l,
        e_dn=e8, etb=e8.T, etg=jnp.roll(e8, DN_HEADS, axis=1).T,
        e_att=_head_indicator(SWA_GW, SWA_DIM).T.astype(BF16),
    )


def _layer_prompt(x, p):
    B, L, D = x.shape
    N = B * L
    x2d = x.reshape(N, D)
    bw = p["bf16"]
    pks, tails = [], []
    hgs = _norm_permute(x, p["ln1"], tuple(d for _, d in SWA_CONFIGS), tm=min(512, L))
    for g, (win, dil) in enumerate(SWA_CONFIGS):
        assert L >= win
        pk, tail = _proj_attn(hgs[g], p["w_grp"][g], p["ng_grp"][g], p["bd"], tmr=min(512, L // dil))
        pks.append(pk)
        tails.append(tail)
    tmp = min(1024, N)
    h0 = hgs[0].reshape(B, L, D)
    qd, kd, vd, raw_tail = _proj_dn(h0, bw["dn"], p["conv_w"], tm=min(512, L))
    gates = _matmul(h0.reshape(N, D), bw["gate"], tm=tmp, tn=1536, out_dtype=BF16, name="proj_gate")
    ba = _matmul(h0.reshape(N, D), bw["ba"], tm=tmp, tn=LANES, out_dtype=F32, name="proj_ba")
    os_, ls_ = [], []
    for pk in pks:
        d, M = pk.shape[1], pk.shape[2]
        o, lse = _attn(pk.reshape(B * d, M, pk.shape[3]), tq=min(256, M))
        os_.append(o.reshape(B, d, M, SWA_GW))
        ls_.append(lse.reshape(B, d, M, LANES))
    gb, gt_rows = _dn_gates(ba.reshape(B, L, LANES), p["par"], tl=min(256, L))
    u, w, qdec, kdec, a, gt = _dn_intra(qd, kd, vd, gb, gt_rows, tl=min(2048, L))
    od, s_new = _dn_scan(u, w, qdec, kdec, a, gt, jnp.zeros((B, DN_HEADS, DN_DK, LANES), F32), tl=min(512, L),
                         bb=2 if B % 2 == 0 else 1)
    x2 = _out_proj(x2d, os_, ls_, od.reshape(N, -1), gates, p["dng"], p["wa"], p["wb"], p["wo"], p["e_att"],
                   B=B, L=L, tm=min(512, L))
    y = _moe(x2, p["ln2"], p["wr"], p["br"], p["wg"], p["wu"], p["wd"], p["layer"], tm=256)
    return y.reshape(B, L, D), tails, raw_tail[:, 8 - (DN_CONV - 1):], s_new


def _layer_sample(x, caches, conv_state, s0, layer, p):
    Bs, T, D = x.shape
    assert T == 1
    x2d = x.reshape(Bs, D)
    bw = p["bf16"]
    proj = functools.partial(_proj_plain, x2d, p["ln1"], tm=Bs, out_dtype=F32)
    z_att = proj(bw["att"], tn=1536, name="sproj_att")
    raw = proj(bw["dn"], tn=1536, name="sproj_dn")
    gates = proj(bw["gate"], tn=1536, name="sproj_gate")
    ba = proj(bw["ba"], tn=LANES, name="sproj_ba")
    oa, kv = _sample_attn(z_att.reshape(Bs, 1, -1), caches, layer, p["qg"], p["kg"])
    raw3 = raw.reshape(Bs, 1, -1)
    od, s_new = _sample_dn(raw3, conv_state, s0, layer, p["conv_w"], ba.reshape(Bs, 1, LANES), p["par"],
                           p["e_dn"], p["etb"], p["etg"])
    x2 = _sample_out(x2d, oa.reshape(Bs, -1), od.reshape(Bs, -1), gates, p["dng"], p["wa"], p["wb"], p["wo"])
    y = _moe(x2, p["ln2"], p["wr"], p["br"], p["wg"], p["wu"], p["wd"], p["layer"], tm=Bs)
    W2 = 2 * SWA_GW
    kvs = [kv[:, :, g * W2:(g + 1) * W2].reshape(Bs, 1, 2, SWA_HEADS, SWA_DIM) for g in range(len(SWA_CONFIGS))]
    new_conv = jnp.concatenate([conv_state[layer][:, 1:], raw3], axis=1)
    return y.reshape(Bs, 1, D), kvs, new_conv, s_new


def kernel(x_prompt, x_sample, cache_swa0_kv, cache_swa1_kv, cache_swa2_kv, state_dn_conv, state_dn_S, ln1_g, w_in,
           q_norm_g, k_norm_g, dn_conv_w, dn_a_log, dn_dt_bias, dn_norm_g, w_out_a, w_out_b, w_o, ln2_g, w_rg, b_rg,
           w_re, b_re, w_e_gate, w_e_up, w_e_down):
    yp, ys = x_prompt, x_sample
    outs = [[] for _ in range(10)]
    for l in range(w_in.shape[0]):
        p = _prep_layer(l, ln1_g, w_in, q_norm_g, k_norm_g, dn_conv_w, dn_a_log, dn_dt_bias, dn_norm_g, w_out_a,
                        w_out_b, w_o, ln2_g, w_rg, b_rg, w_re, b_re, w_e_gate, w_e_up, w_e_down)
        yp, pkv, pconv, ps = _layer_prompt(yp, p)
        ys, skv, sconv, ss = _layer_sample(ys, (cache_swa0_kv, cache_swa1_kv, cache_swa2_kv), state_dn_conv,
                                           state_dn_S, l, p)
        for lst, val in zip(outs, (*pkv, pconv, ps, *skv, sconv, ss)):
            lst.append(val)
    return (yp, ys, *(jnp.stack(o) for o in outs))
```

```python
import functools

import jax
import jax.numpy as jnp
from jax import lax
from jax.experimental import pallas as pl
from jax.experimental.pallas import tpu as pltpu

F32 = jnp.float32
BF16 = jnp.bfloat16
HI = lax.Precision.HIGHEST
EPS = 1e-6

SWA_CONFIGS = ((128, 1), (512, 4), (2048, 16))
SWA_HEADS = 8
SWA_DIM = 64
SWA_GW = SWA_HEADS * SWA_DIM
SWA_SPAN = 128
DN_HEADS = 8
DN_DK = 128
DN_CONV = 4
DN_CHUNK = 64
N_GROUPS = 4
PER_GROUP = 8
N_EXPERTS = N_GROUPS * PER_GROUP
TOP_K = 2

VMEM_LIMIT_BYTES = 56 * 1024 * 1024
LANES = 128
MXU = 256
MOE_ROWS = 512
MOE_ROWS_SMALL = 128
ROW_UNROLL = 8
ROUTER_ROWS = 512


def _cparams(n_axes):
    return pltpu.CompilerParams(
        dimension_semantics=("arbitrary",) * n_axes, vmem_limit_bytes=VMEM_LIMIT_BYTES
    )


def _rms(x, g):
    return x * lax.rsqrt(jnp.mean(x * x, axis=-1, keepdims=True) + EPS) * g


def _bdot(a, b):
    return jnp.dot(a.astype(BF16), b.astype(BF16), preferred_element_type=F32)


def _hdot(a, b):
    return jnp.dot(a, b, preferred_element_type=F32, precision=HI)


def _sigmoid(x):
    return 0.5 * jnp.tanh(0.5 * x) + 0.5


def _silu(x):
    half = 0.5 * x
    return half * jnp.tanh(half) + half


def _pack_bf16_pairs(x):
    w = x.shape[1] // 2
    lo = lax.bitcast_convert_type(x[:, :w].astype(BF16).astype(F32), jnp.uint32) >> 16
    hi = lax.bitcast_convert_type(x[:, w:].astype(BF16).astype(F32), jnp.uint32) & jnp.uint32(0xFFFF0000)
    return lo | hi


def _unpack_bf16_pairs(p):
    lo = lax.bitcast_convert_type(p << 16, F32)
    hi = lax.bitcast_convert_type(p & jnp.uint32(0xFFFF0000), F32)
    return jnp.concatenate([lo, hi], axis=-1)


def _softplus(x):
    return jnp.maximum(x, 0.0) + jnp.log1p(jnp.exp(-jnp.abs(x)))


def _norm_permute_kernel(x_ref, lng_ref, *refs, tm, dils):
    outs, h_scr = refs[:-1], refs[-1]
    h = _rms(x_ref[0], lng_ref[...])
    n_cb = h_scr.shape[0]
    for cb in range(n_cb):
        h_scr[cb] = h[:, cb * LANES:(cb + 1) * LANES]
    for o_ref, d in zip(outs, dils):
        for cb in range(n_cb):
            for r in range(d):
                src = h_scr[cb] if d == 1 else h_scr[cb, pl.ds(r, tm // d, stride=d), :]
                o_ref[0, r, :, cb * LANES:(cb + 1) * LANES] = src.astype(BF16)


def _norm_permute(x, ln_g, dils, *, tm):
    B, L, D = x.shape
    assert L % tm == 0 and all(tm % (16 * d) == 0 for d in dils)
    return pl.pallas_call(
        functools.partial(_norm_permute_kernel, tm=tm, dils=dils),
        grid=(B, L // tm),
        in_specs=[pl.BlockSpec((1, tm, D), lambda b, i: (b, i, 0)), pl.BlockSpec((1, D), lambda b, i: (0, 0))],
        out_specs=[pl.BlockSpec((1, d, tm // d, D), lambda b, i: (b, 0, i, 0)) for d in dils],
        out_shape=[jax.ShapeDtypeStruct((B, d, L // d, D), BF16) for d in dils],
        scratch_shapes=[pltpu.VMEM((D // LANES, tm, LANES), F32)],
        compiler_params=_cparams(2),
        name="norm_permute",
    )(x, ln_g)


def _proj_attn_kernel(h_ref, w_ref, ng_ref, bd_ref, p_ref, t_ref, *, n_tiles):
    rows = h_ref.shape[2]
    z = jnp.dot(h_ref[0, 0], w_ref[...], preferred_element_type=F32)
    kv = []
    for c in range(0, 3 * SWA_GW, MXU):
        zc = z[:, c:c + MXU]
        if c < 2 * SWA_GW:
            ms = jnp.dot((zc * zc).astype(BF16), bd_ref[...], preferred_element_type=F32)
            zc = zc * lax.rsqrt(ms + EPS) * ng_ref[0, :, c:c + MXU]
        p_ref[0, 0, :, c:c + MXU] = zc.astype(BF16)
        if c >= SWA_GW:
            kv.append(zc[rows - SWA_SPAN:rows, :])

    @pl.when(pl.program_id(2) == n_tiles - 1)
    def _():
        per = SWA_GW // MXU
        for s in range(2):
            zr = jnp.concatenate(kv[s * per:(s + 1) * per], axis=-1)
            t_ref[0, :, s] = zr.reshape(SWA_SPAN, SWA_HEADS, SWA_DIM)


def _proj_attn(hg, w_g, ng_g, bd, *, tmr):
    B, dil, M, D = hg.shape
    assert M % tmr == 0 and tmr >= SWA_SPAN
    nt = M // tmr
    W3 = 3 * SWA_GW
    keep = SWA_SPAN * dil
    p, t = pl.pallas_call(
        functools.partial(_proj_attn_kernel, n_tiles=nt),
        grid=(B, dil, nt),
        in_specs=[
            pl.BlockSpec((1, 1, tmr, D), lambda b, r, i: (b, r, i, 0)),
            pl.BlockSpec((D, W3), lambda b, r, i: (0, 0)),
            pl.BlockSpec((1, 1, 2 * SWA_GW), lambda b, r, i: (0, 0, 0)),
            pl.BlockSpec((MXU, MXU), lambda b, r, i: (0, 0)),
        ],
        out_specs=[
            pl.BlockSpec((1, 1, tmr, W3), lambda b, r, i: (b, r, i, 0)),
            pl.BlockSpec((1, SWA_SPAN, None, 2, SWA_HEADS, SWA_DIM), lambda b, r, i: (b, 0, r, 0, 0, 0)),
        ],
        out_shape=[
            jax.ShapeDtypeStruct((B, dil, M, W3), BF16),
            jax.ShapeDtypeStruct((B, SWA_SPAN, dil, 2, SWA_HEADS, SWA_DIM), F32),
        ],
        compiler_params=_cparams(3),
        name="proj_attn",
    )(hg, w_g, ng_g, bd)
    return p, t.reshape(B, keep, 2, SWA_HEADS, SWA_DIM)


def _proj_plain_kernel(x_ref, lng_ref, w_ref, o_ref, h_scr):
    @pl.when(pl.program_id(1) == 0)
    def _():
        h_scr[...] = _rms(x_ref[...], lng_ref[...]).astype(BF16)

    o_ref[...] = jnp.dot(h_scr[...], w_ref[...], preferred_element_type=F32).astype(o_ref.dtype)


def _proj_plain(x2d, ln_g, w, *, tm, tn, out_dtype, name="proj_plain"):
    N, D = x2d.shape
    C = w.shape[1]
    assert N % tm == 0 and C % tn == 0
    return pl.pallas_call(
        _proj_plain_kernel,
        grid=(N // tm, C // tn),
        in_specs=[
            pl.BlockSpec((tm, D), lambda i, j: (i, 0)),
            pl.BlockSpec((1, D), lambda i, j: (0, 0)),
            pl.BlockSpec((D, tn), lambda i, j: (0, j)),
        ],
        out_specs=pl.BlockSpec((tm, tn), lambda i, j: (i, j)),
        out_shape=jax.ShapeDtypeStruct((N, C), out_dtype),
        scratch_shapes=[pltpu.VMEM((tm, D), BF16)],
        compiler_params=_cparams(2),
        name=name,
    )(x2d, ln_g, w)


def _matmul_kernel(h_ref, w_ref, o_ref):
    o_ref[...] = jnp.dot(h_ref[...], w_ref[...], preferred_element_type=F32).astype(o_ref.dtype)


def _matmul(h2d, w, *, tm, tn, out_dtype, name):
    N, D = h2d.shape
    C = w.shape[1]
    assert N % tm == 0 and C % tn == 0
    return pl.pallas_call(
        _matmul_kernel,
        grid=(N // tm, C // tn),
        in_specs=[pl.BlockSpec((tm, D), lambda i, j: (i, 0)), pl.BlockSpec((D, tn), lambda i, j: (0, j))],
        out_specs=pl.BlockSpec((tm, tn), lambda i, j: (i, j)),
        out_shape=jax.ShapeDtypeStruct((N, C), out_dtype),
        compiler_params=_cparams(2),
        name=name,
    )(h2d, w)


def _attn_kernel(q_ref, kc_ref, vc_ref, kp_ref, vp_ref, o_ref, lse_ref, kk_scr, vv_scr, *, tq):
    i = pl.program_id(1)
    blk = SWA_SPAN
    kk_scr[0:blk, :] = kp_ref[0]
    kk_scr[blk:blk + tq, :] = kc_ref[0]
    vv_scr[0:blk, :] = vp_ref[0]
    vv_scr[blk:blk + tq, :] = vc_ref[0]
    qi = lax.broadcasted_iota(jnp.int32, (blk, 2 * blk), 0)
    ki = lax.broadcasted_iota(jnp.int32, (blk, 2 * blk), 1)
    dist = blk + qi - ki
    band = (dist >= 0) & (dist <= SWA_SPAN)
    band_first = band & ((ki >= blk) | (i > 0))
    lo = lax.broadcasted_iota(jnp.int32, (blk, LANES), 1) < SWA_DIM
    zero = jnp.zeros((blk, LANES), BF16)
    lane = lax.broadcasted_iota(jnp.int32, (blk, LANES), 1)
    for jb in range(tq // blk):
        mask = band_first if jb == 0 else band
        rows = slice(jb * blk, (jb + 1) * blk)
        lse_all = jnp.zeros((blk, LANES), F32)
        for hp in range(SWA_GW // LANES):
            cs = slice(hp * LANES, (hp + 1) * LANES)
            qb = q_ref[0, rows, cs]
            kk = kk_scr[jb * blk:(jb + 2) * blk, cs]
            vv = vv_scr[jb * blk:(jb + 2) * blk, cs]
            res_o = []
            for hh in range(2):
                qm = jnp.where(lo if hh == 0 else jnp.logical_not(lo), qb, zero)
                s = lax.dot_general(qm, kk, (((1,), (1,)), ((), ())), preferred_element_type=F32)
                s = jnp.where(mask, s, -jnp.inf)
                m = jnp.max(s, axis=-1, keepdims=True)
                p = jnp.exp(s - m)
                den = jnp.sum(p, axis=-1, keepdims=True)
                pv = jnp.dot(p.astype(BF16), vv, preferred_element_type=F32)
                res_o.append(pv / den)
                lse_all = jnp.where(lane == 2 * hp + hh, m + jnp.log(den), lse_all)
            o_ref[0, rows, cs] = jnp.where(lo, res_o[0], res_o[1]).astype(BF16)
        lse_ref[0, rows, :] = lse_all


def _attn(p, *, tq):
    S, M, _ = p.shape
    assert M % tq == 0 and tq % SWA_SPAN == 0
    nb = tq // SWA_SPAN
    return pl.pallas_call(
        functools.partial(_attn_kernel, tq=tq),
        grid=(S, M // tq),
        in_specs=[
            pl.BlockSpec((1, tq, SWA_GW), lambda s, i: (s, i, 0)),
            pl.BlockSpec((1, tq, SWA_GW), lambda s, i: (s, i, 1)),
            pl.BlockSpec((1, tq, SWA_GW), lambda s, i: (s, i, 2)),
            pl.BlockSpec((1, SWA_SPAN, SWA_GW), lambda s, i: (s, jnp.maximum(i * nb - 1, 0), 1)),
            pl.BlockSpec((1, SWA_SPAN, SWA_GW), lambda s, i: (s, jnp.maximum(i * nb - 1, 0), 2)),
        ],
        out_specs=[
            pl.BlockSpec((1, tq, SWA_GW), lambda s, i: (s, i, 0)),
            pl.BlockSpec((1, tq, LANES), lambda s, i: (s, i, 0)),
        ],
        out_shape=[
            jax.ShapeDtypeStruct((S, M, SWA_GW), BF16),
            jax.ShapeDtypeStruct((S, M, LANES), F32),
        ],
        scratch_shapes=[
            pltpu.VMEM((SWA_SPAN + tq, SWA_GW), BF16),
            pltpu.VMEM((SWA_SPAN + tq, SWA_GW), BF16),
        ],
        compiler_params=_cparams(2),
        name="swa_attn",
    )(p, p, p, p, p)


def _proj_dn_kernel(h_ref, w_ref, cw_ref, q_ref, k_ref, v_ref, tail_ref, z_scr, carry_scr, *, tm, n_ct):
    i = pl.program_id(1)
    j = pl.program_id(2)
    nh = DN_HEADS
    ncb = z_scr.shape[1] // LANES

    z_scr[0:8, :] = jnp.where(i == 0, 0.0, carry_scr[j])
    z_scr[8:8 + tm, :] = jnp.dot(h_ref[0], w_ref[...], preferred_element_type=F32)
    last = z_scr[tm:tm + 8, :]
    carry_scr[j] = last
    tn = z_scr.shape[1]
    outs = (q_ref, k_ref, v_ref)
    for jj in range(n_ct):

        @pl.when(j == jj)
        def _(jj=jj):
            tail_ref[0, :, jj * tn:(jj + 1) * tn] = last
            for cbl in range(ncb):
                cs = slice(cbl * LANES, (cbl + 1) * LANES)
                part, h = divmod(jj * ncb + cbl, nh)
                xe = z_scr[:, cs]
                acc = (0.5 * cw_ref[0:1, cs]) * xe
                for t in range(1, DN_CONV):
                    acc = (0.5 * cw_ref[t:t + 1, cs]) * xe + pltpu.roll(acc, 1, axis=0)
                half = acc[8:]
                act = half * jnp.tanh(half) + half
                if part < 2:
                    inv = lax.rsqrt(jnp.sum(act * act, axis=-1, keepdims=True) + EPS)
                    act = act * (inv * (DN_DK ** -0.5) if part == 0 else inv)
                outs[part][0, :, h * LANES:(h + 1) * LANES] = act.astype(BF16)


def _proj_dn(h, w_dn, conv_w, *, tm):
    B, L, D = h.shape
    C = w_dn.shape[1]
    width = DN_HEADS * DN_DK
    n_ct = 2
    tn = C // n_ct
    assert L % tm == 0 and C == 3 * width and tn % LANES == 0
    qkv = pl.BlockSpec((1, tm, width), lambda b, i, j: (b, i, 0))
    return pl.pallas_call(
        functools.partial(_proj_dn_kernel, tm=tm, n_ct=n_ct),
        grid=(B, L // tm, n_ct),
        in_specs=[
            pl.BlockSpec((1, tm, D), lambda b, i, j: (b, i, 0)),
            pl.BlockSpec((D, tn), lambda b, i, j: (0, j)),
            pl.BlockSpec((DN_CONV, tn), lambda b, i, j: (0, j)),
        ],
        out_specs=[qkv, qkv, qkv, pl.BlockSpec((1, 8, C), lambda b, i, j: (b, 0, 0))],
        out_shape=[jax.ShapeDtypeStruct((B, L, width), BF16)] * 3 + [jax.ShapeDtypeStruct((B, 8, C), F32)],
        scratch_shapes=[pltpu.VMEM((8 + tm, tn), F32), pltpu.VMEM((n_ct, 8, tn), F32)],
        compiler_params=_cparams(3),
        name="proj_dn",
    )(h, w_dn, conv_w)


def _gates_kernel(ba_ref, par_ref, g_ref, gt_ref, *, tl):
    nh = DN_HEADS
    ba = ba_ref[0]
    lane = lax.broadcasted_iota(jnp.int32, (tl, LANES), 1)
    g = par_ref[0:1, :] * _softplus(ba + par_ref[1:2, :])
    ri = lax.broadcasted_iota(jnp.int32, (tl, tl), 0)
    ci = lax.broadcasted_iota(jnp.int32, (tl, tl), 1)
    tri = jnp.where((ri // DN_CHUNK == ci // DN_CHUNK) & (ci <= ri), 1.0, 0.0).astype(BF16)
    g_hi = g.astype(BF16)
    r1 = g - g_hi.astype(F32)
    g_mid = r1.astype(BF16)
    g_lo = (r1 - g_mid.astype(F32)).astype(BF16)
    gc = sum(jnp.dot(tri, piece, preferred_element_type=F32) for piece in (g_hi, g_mid, g_lo))
    g_ref[0] = jnp.where(lane < nh, _sigmoid(ba), gc)
    gt_ref[0] = jnp.transpose(gc)[nh:2 * nh, :]


def _dn_gates(ba, par, *, tl):
    B, L, _ = ba.shape
    assert L % tl == 0 and tl % DN_CHUNK == 0
    return pl.pallas_call(
        functools.partial(_gates_kernel, tl=tl),
        grid=(B, L // tl),
        in_specs=[pl.BlockSpec((1, tl, LANES), lambda b, i: (b, i, 0)), pl.BlockSpec((2, LANES), lambda b, i: (0, 0))],
        out_specs=[pl.BlockSpec((1, tl, LANES), lambda b, i: (b, i, 0)),
                   pl.BlockSpec((1, DN_HEADS, tl), lambda b, i: (b, 0, i))],
        out_shape=[jax.ShapeDtypeStruct((B, L, LANES), F32), jax.ShapeDtypeStruct((B, DN_HEADS, L), F32)],
        compiler_params=_cparams(2),
        name="dn_gates",
    )(ba, par)


def _intra_kernel(q_ref, k_ref, v_ref, g_ref, gt_in_ref, u_ref, w_ref, qd_ref, kd_ref, a_ref, gt_ref, *, tl):
    h = pl.program_id(1)
    C = DN_CHUNK
    lane = lax.broadcasted_iota(jnp.int32, (C, LANES), 1)
    ri = lax.broadcasted_iota(jnp.int32, (C, C), 0)
    ci = lax.broadcasted_iota(jnp.int32, (C, C), 1)
    eye = jnp.where(ri == ci, 1.0, 0.0).astype(F32)
    nt_dot = lambda a, b: lax.dot_general(a.astype(BF16), b.astype(BF16), (((1,), (1,)), ((), ())),
                                          preferred_element_type=F32)
    rows = [slice(c * C, (c + 1) * C) for c in range(tl // C)]
    gv = [g_ref[0, r, :] for r in rows]
    q = [q_ref[0, r, :].astype(F32) for r in rows]
    k = [k_ref[0, r, :].astype(F32) for r in rows]
    v = [v_ref[0, r, :].astype(F32) for r in rows]
    beta = [jnp.sum(jnp.where(lane == h, x, 0.0), axis=-1, keepdims=True) for x in gv]
    gc = [jnp.sum(jnp.where(lane == h + DN_HEADS, x, 0.0), axis=-1, keepdims=True) for x in gv]
    gc_row = gt_in_ref[0, pl.ds(h, 1), :]
    decay = [jnp.exp(jnp.where(ri >= ci, a - gc_row[:, r], -jnp.inf)) for a, r in zip(gc, rows)]
    kb = [a * b for a, b in zip(k, beta)]
    kq = [nt_dot(jnp.concatenate([a, b], axis=0), c) for a, b, c in zip(kb, q, k)]
    x = [-jnp.where(ri > ci, m[:C] * d, 0.0) for m, d in zip(kq, decay)]
    t = [eye + a for a in x]
    x = [_bdot(a, a) for a in x]
    for _ in range(4):
        both = [_bdot(jnp.concatenate([a, b], axis=0), a) for a, b in zip(x, t)]
        t = [b + m[C:] for b, m in zip(t, both)]
        x = [m[:C] for m in both]
    t = [b + _bdot(b, a) for a, b in zip(x, t)]
    eg = [jnp.exp(a) for a in gc]
    glast = [a[C - 1:C, :] for a in gc]
    uw = [_bdot(a, jnp.concatenate([b * c, d * e], axis=1)) for a, b, c, d, e in zip(t, v, beta, kb, eg)]
    for c, r in enumerate(rows):
        u_ref[0, 0, r, :] = uw[c][:, :LANES]
        w_ref[0, 0, r, :] = uw[c][:, LANES:].astype(BF16)
        a_ref[0, 0, r, :] = (kq[c][C:] * decay[c]).astype(BF16)
        qd_ref[0, 0, r, :] = (q[c] * eg[c]).astype(BF16)
        kd_ref[0, 0, r, :] = (k[c] * jnp.exp(glast[c] - gc[c])).astype(BF16)
        gt_ref[0, 0, c:c + 1, :] = jnp.broadcast_to(jnp.exp(glast[c]), (1, LANES))


def _dn_intra(q, k, v, g, gt_rows, *, tl):
    B, L, _ = q.shape
    H, C = DN_HEADS, DN_CHUNK
    assert L % tl == 0 and (tl // C) % 8 == 0
    qkv_spec = pl.BlockSpec((1, tl, LANES), lambda b, h, i: (b, i, h))
    hl = lambda w: pl.BlockSpec((1, 1, tl, w), lambda b, h, i: (b, h, i, 0))
    return pl.pallas_call(
        functools.partial(_intra_kernel, tl=tl),
        grid=(B, H, L // tl),
        in_specs=[qkv_spec, qkv_spec, qkv_spec, pl.BlockSpec((1, tl, LANES), lambda b, h, i: (b, i, 0)),
                  pl.BlockSpec((1, H, tl), lambda b, h, i: (b, 0, i))],
        out_specs=[hl(LANES), hl(LANES), hl(LANES), hl(LANES), hl(C),
                   pl.BlockSpec((1, 1, tl // C, LANES), lambda b, h, i: (b, h, i, 0))],
        out_shape=[
            jax.ShapeDtypeStruct((B, H, L, LANES), F32),
            jax.ShapeDtypeStruct((B, H, L, LANES), BF16),
            jax.ShapeDtypeStruct((B, H, L, LANES), BF16),
            jax.ShapeDtypeStruct((B, H, L, LANES), BF16),
            jax.ShapeDtypeStruct((B, H, L, C), BF16),
            jax.ShapeDtypeStruct((B, H, L // C, LANES), F32),
        ],
        compiler_params=_cparams(3),
        name="dn_intra",
    )(q, k, v, g, gt_rows)


def _scan_kernel(u_ref, w_ref, qd_ref, kd_ref, a_ref, gt_ref, s0_ref, o_ref, s_ref, *, n_chunks):
    C = DN_CHUNK
    bb, H = s_ref.shape[0], s_ref.shape[1]
    seqs = [(b, h) for b in range(bb) for h in range(H)]

    @pl.when(pl.program_id(1) == 0)
    def _():
        s_ref[...] = s0_ref[...]

    def body(c, carry):
        rows = pl.ds(pl.multiple_of(c * C, C), C)
        S = [s_ref[b, h] for b, h in seqs]
        Sb = [x.astype(BF16) for x in S]
        v_new = [u_ref[b, h, rows, :] - jnp.dot(w_ref[b, h, rows, :], sb, preferred_element_type=F32)
                 for (b, h), sb in zip(seqs, Sb)]
        vb = [x.astype(BF16) for x in v_new]
        o = [jnp.dot(qd_ref[b, h, rows, :], sb, preferred_element_type=F32)
             + jnp.dot(a_ref[b, h, rows, :], v, preferred_element_type=F32) for (b, h), sb, v in zip(seqs, Sb, vb)]
        upd = [lax.dot_general(kd_ref[b, h, rows, :], v, (((0,), (0,)), ((), ())), preferred_element_type=F32)
               for (b, h), v in zip(seqs, vb)]
        for n, (b, h) in enumerate(seqs):
            o_ref[b, rows, h * LANES:(h + 1) * LANES] = o[n]
            s_ref[b, h] = S[n] * gt_ref[b, h, pl.ds(c, 1), :] + upd[n]
        return carry

    lax.fori_loop(0, n_chunks, body, 0)


def _dn_scan(u, w, qd, kd, a, gt, s0, *, tl, bb):
    B, H, L, _ = u.shape
    C = DN_CHUNK
    assert L % tl == 0 and (tl // C) % 8 == 0 and B % bb == 0
    hs = lambda wd: pl.BlockSpec((bb, H, tl, wd), lambda b, i: (b, 0, i, 0))
    s_spec = pl.BlockSpec((bb, H, DN_DK, LANES), lambda b, i: (b, 0, 0, 0))
    return pl.pallas_call(
        functools.partial(_scan_kernel, n_chunks=tl // C),
        grid=(B // bb, L // tl),
        in_specs=[hs(LANES), hs(LANES), hs(LANES), hs(LANES), hs(C),
                  pl.BlockSpec((bb, H, tl // C, LANES), lambda b, i: (b, 0, i, 0)), s_spec],
        out_specs=[pl.BlockSpec((bb, tl, H * LANES), lambda b, i: (b, i, 0)), s_spec],
        out_shape=[jax.ShapeDtypeStruct((B, L, H * LANES), F32),
                   jax.ShapeDtypeStruct((B, H, DN_DK, LANES), F32)],
        compiler_params=_cparams(2),
        name="dn_scan",
    )(u, w, qd, kd, a, gt, s0)


def _gated_mix(o_a, od, gates, dng, wa, wb, wo, x, dot):
    width = DN_HEADS * DN_DK
    parts = []
    for h in range(DN_HEADS):
        blk = od[:, h * LANES:(h + 1) * LANES]
        parts.append(blk * lax.rsqrt(jnp.mean(blk * blk, axis=-1, keepdims=True) + EPS) * dng)
    odn = jnp.concatenate(parts, axis=-1) * _silu(gates[:, 0:width].astype(F32))
    ya = dot(o_a, wa)
    yb = dot(odn, wb)
    mix = _sigmoid(gates[:, width:2 * width].astype(F32)) * ya + _sigmoid(gates[:, 2 * width:].astype(F32)) * yb
    return x + dot(mix, wo)


def _out_kernel(x_ref, o0, o1, o2, l0, l1, l2, od_ref, gates_ref, dng_ref, wa_ref, wb_ref, wo_ref, e_ref, y_ref,
                so0, so1, so2, sl0, sl1, sl2, *, tm, dils):
    o_refs, l_refs = (o0, o1, o2), (l0, l1, l2)
    so, sl = (so0, so1, so2), (sl0, sl1, sl2)
    for gi, d in enumerate(dils):
        for r in range(d):
            dst = slice(None) if d == 1 else pl.ds(r, tm // d, stride=d)
            sl[gi][dst, :] = l_refs[gi][0, r]
            for cb in range(SWA_GW // LANES):
                so[gi][cb, dst, :] = o_refs[gi][0, r, :, cb * LANES:(cb + 1) * LANES].astype(F32)
    ls = [s[...] for s in sl]
    m = jnp.maximum(jnp.maximum(ls[0], ls[1]), ls[2])
    es = [jnp.exp(l - m) for l in ls]
    tot = es[0] + es[1] + es[2]
    alphas = [jnp.dot((e / tot).astype(BF16), e_ref[...], preferred_element_type=F32) for e in es]
    parts = []
    for cb in range(SWA_GW // LANES):
        cs = slice(cb * LANES, (cb + 1) * LANES)
        parts.append(alphas[0][:, cs] * so[0][cb] + alphas[1][:, cs] * so[1][cb] + alphas[2][:, cs] * so[2][cb])
    o_a = jnp.concatenate(parts, axis=-1)
    y_ref[...] = _gated_mix(o_a, od_ref[...], gates_ref[...], dng_ref[...], wa_ref[...], wb_ref[...],
                            wo_ref[...], x_ref[...], _bdot)


def _out_proj(x2d, os_, ls_, od2d, gates, dng, wa, wb, wo, e_att, *, B, L, tm):
    N, D = x2d.shape
    nt = L // tm
    dils = tuple(d for _, d in SWA_CONFIGS)
    grp = lambda d, w: pl.BlockSpec((1, d, tm // d, w), lambda i: (i // nt, 0, i % nt, 0))
    row = lambda w: pl.BlockSpec((tm, w), lambda i: (i, 0))
    full = lambda a: pl.BlockSpec(a.shape, lambda i: (0, 0))
    return pl.pallas_call(
        functools.partial(_out_kernel, tm=tm, dils=dils),
        grid=(N // tm,),
        in_specs=[row(D)] + [grp(d, SWA_GW) for d in dils] + [grp(d, LANES) for d in dils]
        + [row(od2d.shape[1]), row(gates.shape[1]), full(dng), full(wa), full(wb), full(wo), full(e_att)],
        out_specs=row(D),
        out_shape=jax.ShapeDtypeStruct((N, D), F32),
        scratch_shapes=[pltpu.VMEM((SWA_GW // LANES, tm, LANES), F32)] * 3 + [pltpu.VMEM((tm, LANES), F32)] * 3,
        compiler_params=_cparams(1),
        name="out_proj",
    )(x2d, *os_, *ls_, od2d, gates, dng, wa, wb, wo, e_att)


def _router_kernel(x_ref, lng_ref, wr_ref, br_ref, info_ref, cnt_ref, base_scr, *, tm):
    i = pl.program_id(0)

    @pl.when(i == 0)
    def _():
        base_scr[...] = jnp.zeros_like(base_scr)

    h = _rms(x_ref[...], lng_ref[...])
    lg = _bdot(h, wr_ref[...]) + br_ref[...]
    lane = lax.broadcasted_iota(jnp.int32, (tm, LANES), 1)
    big = jnp.int32(1 << 20)
    ninf = -jnp.inf

    def argmax_lane(vals):
        mx = jnp.max(vals, axis=-1, keepdims=True)
        idx = jnp.min(jnp.where(vals == mx, lane, big), axis=-1, keepdims=True)
        return mx, idx

    lgm = jnp.where(lane < N_GROUPS, lg, ninf)
    mg, gsel = argmax_lane(lgm)
    pg = 1.0 / jnp.sum(jnp.exp(lgm - mg), axis=-1, keepdims=True)
    start = N_GROUPS + gsel * PER_GROUP
    le = jnp.where((lane >= start) & (lane < start + PER_GROUP), lg, ninf)
    m1, i1 = argmax_lane(le)
    m2, i2 = argmax_lane(jnp.where(lane == i1, ninf, le))
    e21 = jnp.exp(m2 - m1)
    w1 = pg / (1.0 + e21)
    w2 = pg * e21 / (1.0 + e21)
    oh = jnp.where(lane == i1, 1.0, 0.0) + jnp.where(lane == i2, 1.0, 0.0)
    ri = lax.broadcasted_iota(jnp.int32, (tm, tm), 0)
    ci = lax.broadcasted_iota(jnp.int32, (tm, tm), 1)
    strict = jnp.where(ci < ri, 1.0, 0.0).astype(BF16)
    pref = jnp.dot(strict, oh.astype(BF16), preferred_element_type=F32) + base_scr[...]
    r1 = jnp.sum(jnp.where(lane == i1, pref, 0.0), axis=-1, keepdims=True)
    r2 = jnp.sum(jnp.where(lane == i2, pref, 0.0), axis=-1, keepdims=True)
    base_scr[...] = base_scr[...] + jnp.sum(oh, axis=0, keepdims=True)
    cnt_ref[...] = base_scr[...]
    off = jnp.float32(N_GROUPS)
    info = jnp.where(lane == 0, i1.astype(F32) - off, 0.0)
    info = jnp.where(lane == 1, i2.astype(F32) - off, info)
    info = jnp.where(lane == 2, w1, info)
    info = jnp.where(lane == 3, w2, info)
    info = jnp.where(lane == 4, r1, info)
    info = jnp.where(lane == 5, r2, info)
    info_ref[...] = info


def _router(x2d, ln_g, wr, br, *, tm):
    N, D = x2d.shape
    assert N % tm == 0
    return pl.pallas_call(
        functools.partial(_router_kernel, tm=tm),
        grid=(N // tm,),
        in_specs=[
            pl.BlockSpec((tm, D), lambda i: (i, 0)),
            pl.BlockSpec((1, D), lambda i: (0, 0)),
            pl.BlockSpec((D, LANES), lambda i: (0, 0)),
            pl.BlockSpec((1, LANES), lambda i: (0, 0)),
        ],
        out_specs=[pl.BlockSpec((tm, LANES), lambda i: (i, 0)), pl.BlockSpec((1, LANES), lambda i: (0, 0))],
        out_shape=[jax.ShapeDtypeStruct((N, LANES), F32), jax.ShapeDtypeStruct((1, LANES), F32)],
        scratch_shapes=[pltpu.VMEM((1, LANES), F32)],
        compiler_params=_cparams(1),
        name="router",
    )(x2d, ln_g, wr, br)


def _dispatch_kernel(dest_ref, zb_ref, x_ref, lng_ref, xs_ref, zero_scr, rows_scr, sem, *, tm, tb, n_zb, n_tiles):
    i = pl.program_id(0)

    @pl.when(i == 0)
    def _():
        zero_scr[...] = jnp.zeros_like(zero_scr)

        def zero_copy(n):
            return pltpu.make_async_copy(zero_scr, xs_ref.at[pl.ds(zb_ref[n] * tb, tb)], sem.at[2])

        def zero_issue(n, carry):
            @pl.when(zb_ref[n] >= 0)
            def _():
                zero_copy(n).start()

            return carry

        def zero_wait(n, carry):
            @pl.when(zb_ref[n] >= 0)
            def _():
                zero_copy(n).wait()

            return carry

        lax.fori_loop(0, n_zb, zero_issue, 0)
        lax.fori_loop(0, n_zb, zero_wait, 0)

    buf_now = lax.rem(i, 2)
    rows_scr[buf_now] = _pack_bf16_pairs(_rms(x_ref[...], lng_ref[...]))

    def row_copy(tile, t, slot):
        buf = lax.rem(tile, 2)
        return pltpu.make_async_copy(
            rows_scr.at[buf, pl.ds(t, 1)],
            xs_ref.at[pl.ds(dest_ref[(tile * tm + t) * TOP_K + slot], 1)], sem.at[buf])

    def issue(tt, carry):
        for r in range(ROW_UNROLL):
            for slot in range(TOP_K):
                row_copy(i, tt * ROW_UNROLL + r, slot).start(priority=slot)
        return carry

    def drain(tile):
        buf = lax.rem(tile, 2)
        for _ in range(TOP_K):
            pltpu.make_async_copy(rows_scr.at[buf], rows_scr.at[buf], sem.at[buf]).wait()

    lax.fori_loop(0, tm // ROW_UNROLL, issue, 0)

    @pl.when(i > 0)
    def _():
        drain(i - 1)

    @pl.when(i == n_tiles - 1)
    def _():
        drain(i)


def _dispatch(dest, zero_blocks, x2d, ln_g, *, tm, tb, n_rows):
    N, D = x2d.shape
    return pl.pallas_call(
        functools.partial(_dispatch_kernel, tm=tm, tb=tb, n_zb=zero_blocks.shape[0], n_tiles=N // tm),
        grid_spec=pltpu.PrefetchScalarGridSpec(
            num_scalar_prefetch=2,
            grid=(N // tm,),
            in_specs=[pl.BlockSpec((tm, D), lambda i, d, z: (i, 0)), pl.BlockSpec((1, D), lambda i, d, z: (0, 0))],
            out_specs=pl.BlockSpec(memory_space=pl.ANY),
            scratch_shapes=[pltpu.VMEM((tb, D // 2), jnp.uint32), pltpu.VMEM((2, tm, D // 2), jnp.uint32),
                            pltpu.SemaphoreType.DMA((3,))],
        ),
        out_shape=jax.ShapeDtypeStruct((n_rows, D // 2), jnp.uint32),
        compiler_params=_cparams(1),
        name="moe_dispatch",
    )(dest, zero_blocks, x2d, ln_g)


def _ffn_kernel(be_ref, nb_ref, xs_ref, wg_ref, wu_ref, wd_ref, y_ref, wg_scr, wu_scr, wd_scr):
    i = pl.program_id(0)
    used = i < nb_ref[0]

    @pl.when(jnp.logical_or(i == 0, be_ref[i] != be_ref[jnp.maximum(i - 1, 0)]))
    def _():
        wg_scr[...] = wg_ref[0].astype(BF16)
        wu_scr[...] = wu_ref[0].astype(BF16)
        wd_scr[...] = wd_ref[0].astype(BF16)

    @pl.when(used)
    def _():
        h = _unpack_bf16_pairs(xs_ref[...]).astype(BF16)
        g = jnp.dot(h, wg_scr[...], preferred_element_type=F32)
        u = jnp.dot(h, wu_scr[...], preferred_element_type=F32)
        y = jnp.dot((_silu(g) * u).astype(BF16), wd_scr[...], preferred_element_type=F32)
        y_ref[...] = _pack_bf16_pairs(y)

    @pl.when(jnp.logical_not(used))
    def _():
        y_ref[...] = jnp.zeros_like(y_ref)


def _ffn(blk_e, nb_used, xs, wg, wu, wd, layer, *, tb):
    P, W = xs.shape
    D = 2 * W
    nb = P // tb
    DE = wg.shape[3]
    return pl.pallas_call(
        _ffn_kernel,
        grid_spec=pltpu.PrefetchScalarGridSpec(
            num_scalar_prefetch=2,
            grid=(nb,),
            in_specs=[
                pl.BlockSpec((tb, W), lambda i, be, nbu: (jnp.minimum(i, nbu[0] - 1), 0)),
                pl.BlockSpec((None, 1, D, DE), lambda i, be, nbu: (layer, be[i], 0, 0)),
                pl.BlockSpec((None, 1, D, DE), lambda i, be, nbu: (layer, be[i], 0, 0)),
                pl.BlockSpec((None, 1, DE, D), lambda i, be, nbu: (layer, be[i], 0, 0)),
            ],
            out_specs=pl.BlockSpec((tb, W), lambda i, be, nbu: (i, 0)),
            scratch_shapes=[pltpu.VMEM((D, DE), BF16), pltpu.VMEM((D, DE), BF16), pltpu.VMEM((DE, D), BF16)],
        ),
        out_shape=jax.ShapeDtypeStruct((P, W), jnp.uint32),
        compiler_params=_cparams(1),
        name="moe_ffn",
    )(blk_e, nb_used, xs, wg, wu, wd)


def _combine_kernel(dest_ref, x_ref, info_ref, yb_ref, y_ref, g_scr, sem, *, tm, n_tiles):
    i = pl.program_id(0)

    def row_copy(tile, t, slot):
        buf = lax.rem(tile, 2)
        return pltpu.make_async_copy(
            yb_ref.at[pl.ds(dest_ref[(tile * tm + t) * TOP_K + slot], 1)],
            g_scr.at[buf, slot, pl.ds(t, 1)], sem.at[buf])

    def issue_tile(tile):
        def body(tt, carry):
            for r in range(ROW_UNROLL):
                for slot in range(TOP_K):
                    row_copy(tile, tt * ROW_UNROLL + r, slot).start(priority=slot)
            return carry

        lax.fori_loop(0, tm // ROW_UNROLL, body, 0)

    @pl.when(i == 0)
    def _():
        issue_tile(i)

    @pl.when(i + 1 < n_tiles)
    def _():
        issue_tile(i + 1)

    buf = lax.rem(i, 2)
    pltpu.make_async_copy(g_scr.at[buf], g_scr.at[buf], sem.at[buf]).wait()
    info = info_ref[...]
    lane = lax.broadcasted_iota(jnp.int32, info.shape, 1)
    w1 = jnp.sum(jnp.where(lane == 2, info, 0.0), axis=-1, keepdims=True)
    w2 = jnp.sum(jnp.where(lane == 3, info, 0.0), axis=-1, keepdims=True)
    g1 = _unpack_bf16_pairs(g_scr[buf, 0])
    g2 = _unpack_bf16_pairs(g_scr[buf, 1])
    y_ref[...] = x_ref[...] + (w1 * g1 + w2 * g2)


def _combine(dest, x2d, info, yb, *, tm):
    N, D = x2d.shape
    return pl.pallas_call(
        functools.partial(_combine_kernel, tm=tm, n_tiles=N // tm),
        grid_spec=pltpu.PrefetchScalarGridSpec(
            num_scalar_prefetch=1,
            grid=(N // tm,),
            in_specs=[
                pl.BlockSpec((tm, D), lambda i, d: (i, 0)),
                pl.BlockSpec((tm, LANES), lambda i, d: (i, 0)),
                pl.BlockSpec(memory_space=pl.ANY),
            ],
            out_specs=pl.BlockSpec((tm, D), lambda i, d: (i, 0)),
            scratch_shapes=[pltpu.VMEM((2, TOP_K, tm, D // 2), jnp.uint32), pltpu.SemaphoreType.DMA((2,))],
        ),
        out_shape=jax.ShapeDtypeStruct((N, D), F32),
        compiler_params=_cparams(1),
        name="moe_combine",
    )(dest, x2d, info, yb)


def _moe(x2d, ln2_g, wr, br, wg, wu, wd, layer, *, tm):
    N, D = x2d.shape
    tb = MOE_ROWS if N * TOP_K >= N_EXPERTS * MOE_ROWS else MOE_ROWS_SMALL
    info, counts = _router(x2d, ln2_g, wr, br, tm=ROUTER_ROWS if N % ROUTER_ROWS == 0 else tm)
    counts = counts[0, N_GROUPS:N_GROUPS + N_EXPERTS].astype(jnp.int32)
    pcounts = (counts + tb - 1) // tb * tb
    pend = jnp.cumsum(pcounts)
    pstart = pend - pcounts
    e = info[:, 0:TOP_K].astype(jnp.int32)
    rank = info[:, 4:4 + TOP_K].astype(jnp.int32)
    experts = jnp.arange(N_EXPERTS, dtype=jnp.int32)
    dest = (jnp.sum(jnp.where(e[..., None] == experts, pstart, 0), axis=-1) + rank).reshape(-1)
    nb = -(-(N * TOP_K) // tb) + N_EXPERTS
    P = nb * tb
    blocks = jnp.arange(nb, dtype=jnp.int32)
    blk_e = jnp.minimum(jnp.sum((pend[None, :] <= blocks[:, None] * tb).astype(jnp.int32), axis=1), N_EXPERTS - 1)
    nb_used = (pend[-1] // tb).astype(jnp.int32).reshape(1)
    zero_blocks = jnp.concatenate([jnp.where(counts % tb != 0, pend // tb - 1, -1),
                                   jnp.where(blocks >= nb_used[0], blocks, -1)]).astype(jnp.int32)
    xs = _dispatch(dest, zero_blocks, x2d, ln2_g, tm=tm, tb=tb, n_rows=P)
    yb = _ffn(blk_e, nb_used, xs, wg, wu, wd, layer, tb=tb)
    return _combine(dest, x2d, info, yb, tm=tm)


def _rows8(x):
    return jnp.broadcast_to(x, (8, x.shape[1]))


def _row_hdot(x, m):
    return _hdot(_rows8(x), m)[0:1]


def _bf_round(x):
    return x.astype(BF16).astype(F32)


def _sample_attn_kernel(z_ref, c0, c1, c2, qg_ref, kg_ref, oa_ref, kv_ref):
    W = SWA_GW
    scale = SWA_DIM ** -0.5
    z = z_ref[0]
    sub = lax.broadcasted_iota(jnp.int32, (SWA_HEADS, W), 0)
    lane = lax.broadcasted_iota(jnp.int32, (SWA_HEADS, W), 1)
    own = lane // SWA_DIM == sub

    def heads(row):
        return jnp.where(own, jnp.broadcast_to(row, (SWA_HEADS, W)), 0.0)

    def head_sum(row):
        return jnp.sum(heads(row), axis=-1, keepdims=True)

    def spread(col):
        return jnp.sum(jnp.where(own, col, 0.0), axis=0, keepdims=True)

    def headnorm(zz, g):
        return zz * spread(lax.rsqrt(head_sum(zz * zz) * (1.0 / SWA_DIM) + EPS)) * g

    outs, lses = [], []
    for gi, (c_ref, (win, dil)) in enumerate(zip((c0, c1, c2), SWA_CONFIGS)):
        q = headnorm(z[:, gi * W:(gi + 1) * W], qg_ref[gi:gi + 1, :])
        k = headnorm(z[:, 3 * W + gi * W:3 * W + (gi + 1) * W], kg_ref[gi:gi + 1, :])
        v = z[:, 6 * W + gi * W:6 * W + (gi + 1) * W]
        kv_ref[0, :, 2 * gi * W:(2 * gi + 1) * W] = k
        kv_ref[0, :, (2 * gi + 1) * W:(2 * gi + 2) * W] = v
        kc = c_ref[0].reshape(W, win).astype(BF16)
        vc = c_ref[1].reshape(W, win).astype(BF16)
        s_c = jnp.dot(heads(q).astype(BF16), kc, preferred_element_type=F32) * scale
        row = lax.broadcasted_iota(jnp.int32, s_c.shape, 1)
        s_c = jnp.where(row % dil == 0, s_c, -jnp.inf)
        s_n = head_sum(_bf_round(k) * _bf_round(q)) * scale
        m = jnp.maximum(jnp.max(s_c, axis=-1, keepdims=True), s_n)
        p_c = jnp.exp(s_c - m)
        p_n = jnp.exp(s_n - m)
        den = jnp.sum(p_c, axis=-1, keepdims=True) + p_n
        pv = lax.dot_general(p_c.astype(BF16), vc, (((1,), (1,)), ((), ())), preferred_element_type=F32)
        num = jnp.sum(jnp.where(own, pv, 0.0), axis=0, keepdims=True) + spread(_bf_round(p_n)) * _bf_round(v)
        outs.append(num / spread(den))
        lses.append(m + jnp.log(den))
    mm = jnp.maximum(jnp.maximum(lses[0], lses[1]), lses[2])
    es = [jnp.exp(l - mm) for l in lses]
    tot = es[0] + es[1] + es[2]
    oa_ref[0] = sum(spread(_bf_round(e / tot)) * _bf_round(o) for e, o in zip(es, outs))


def _sample_attn(z3, caches, layer, qg, kg):
    Bs = z3.shape[0]
    W = SWA_GW
    cviews, cspecs = [], []
    for (win, dil), c in zip(SWA_CONFIGS, caches):
        assert c.shape[2] == win
        cviews.append(jnp.transpose(c, (0, 1, 3, 4, 5, 2)))
        cspecs.append(pl.BlockSpec((None, None, 2, SWA_HEADS, SWA_DIM, win), lambda b: (layer, b, 0, 0, 0, 0)))
    full = lambda a: pl.BlockSpec(a.shape, lambda b: (0,) * a.ndim)
    return pl.pallas_call(
        _sample_attn_kernel,
        grid=(Bs,),
        in_specs=[pl.BlockSpec((1, 1, 9 * W), lambda b: (b, 0, 0))] + cspecs + [full(qg), full(kg)],
        out_specs=[pl.BlockSpec((1, 1, W), lambda b: (b, 0, 0)), pl.BlockSpec((1, 1, 6 * W), lambda b: (b, 0, 0))],
        out_shape=[jax.ShapeDtypeStruct((Bs, 1, W), F32), jax.ShapeDtypeStruct((Bs, 1, 6 * W), F32)],
        compiler_params=_cparams(1),
        name="sample_attn",
    )(z3, *cviews, qg, kg)


def _sample_dn_kernel(raw_ref, cs_ref, cw_ref, ba_ref, par_ref, s_ref, e_ref, etb_ref, etg_ref, o_ref, so_ref):
    E, ETB, ETG = e_ref[...], etb_ref[...], etg_ref[...]
    width = DN_HEADS * DN_DK
    conv = cw_ref[DN_CONV - 1:DN_CONV, :] * raw_ref[0]
    for t in range(DN_CONV - 1):
        conv = conv + cw_ref[t:t + 1, :] * cs_ref[0, t:t + 1, :]
    act = _silu(conv)

    def l2(zz):
        return zz * _row_hdot(lax.rsqrt(_row_hdot(zz * zz, E) + EPS), ETB)

    qn = l2(act[:, 0:width]) * (DN_DK ** -0.5)
    kn = l2(act[:, width:2 * width])
    vn = act[:, 2 * width:3 * width]
    ba = ba_ref[0]
    beta = _row_hdot(_sigmoid(ba), ETB)
    eg = jnp.exp(_row_hdot(par_ref[0:1, :] * _softplus(ba + par_ref[1:2, :]), ETG))
    row0 = lax.broadcasted_iota(jnp.int32, (8, LANES), 0) == 0
    for h in range(DN_HEADS):
        sl = slice(h * LANES, (h + 1) * LANES)
        S = s_ref[0, h]
        q, k, v, b, e = qn[:, sl], kn[:, sl], vn[:, sl], beta[:, sl], eg[:, sl]
        Sb = S.astype(BF16)
        wq = jnp.concatenate([k * b * e, q * e, jnp.zeros((6, LANES), F32)], axis=0)
        both = jnp.dot(wq.astype(BF16), Sb, preferred_element_type=F32)
        v_new = v * b - both[0:1]
        a = jnp.sum(q * k, axis=-1, keepdims=True)
        o_ref[0, :, sl] = both[1:2] + a * v_new
        k8 = jnp.where(row0, _rows8(k), 0.0)
        upd = lax.dot_general(k8, _rows8(v_new), (((0,), (0,)), ((), ())), preferred_element_type=F32, precision=HI)
        so_ref[0, h] = S * e + upd


def _sample_dn(raw3, conv_state, s0, layer, conv_w, ba3, par, e_mat, etb, etg):
    Bs, _, C = raw3.shape
    H = DN_HEADS
    full = lambda a: pl.BlockSpec(a.shape, lambda b: (0,) * a.ndim)
    return pl.pallas_call(
        _sample_dn_kernel,
        grid=(Bs,),
        in_specs=[pl.BlockSpec((1, 1, C), lambda b: (b, 0, 0)),
                  pl.BlockSpec((None, 1, DN_CONV - 1, C), lambda b: (layer, b, 0, 0)),
                  full(conv_w),
                  pl.BlockSpec((1, 1, LANES), lambda b: (b, 0, 0)),
                  full(par),
                  pl.BlockSpec((None, 1, H, DN_DK, LANES), lambda b: (layer, b, 0, 0, 0)),
                  full(e_mat), full(etb), full(etg)],
        out_specs=[pl.BlockSpec((1, 1, H * LANES), lambda b: (b, 0, 0)),
                   pl.BlockSpec((1, H, DN_DK, LANES), lambda b: (b, 0, 0, 0))],
        out_shape=[jax.ShapeDtypeStruct((Bs, 1, H * LANES), F32), jax.ShapeDtypeStruct(s0.shape[1:], F32)],
        compiler_params=_cparams(1),
        name="sample_dn",
    )(raw3, conv_state, conv_w, ba3, par, s0, e_mat, etb, etg)


def _sample_out_kernel(x_ref, oa_ref, od_ref, gates_ref, dng_ref, wa_ref, wb_ref, wo_ref, y_ref):
    y_ref[...] = _gated_mix(oa_ref[...], od_ref[...], gates_ref[...], dng_ref[...], wa_ref[...], wb_ref[...],
                            wo_ref[...], x_ref[...], _bdot)


def _sample_out(x2d, oa, od, gates, dng, wa, wb, wo):
    args = (x2d, oa, od, gates, dng, wa, wb, wo)
    return pl.pallas_call(
        _sample_out_kernel,
        grid=(1,),
        in_specs=[pl.BlockSpec(a.shape, lambda i: (0, 0)) for a in args],
        out_specs=pl.BlockSpec(x2d.shape, lambda i: (0, 0)),
        out_shape=jax.ShapeDtypeStruct(x2d.shape, F32),
        compiler_params=_cparams(1),
        name="sample_out",
    )(*args)


def _head_indicator(width, head):
    c = jnp.arange(width)[:, None] // head
    return (c == jnp.arange(LANES)[None, :]).astype(F32)


def _prep_layer(l, ln1_g, w_in, q_norm_g, k_norm_g, dn_conv_w, dn_a_log, dn_dt_bias, dn_norm_g, w_out_a, w_out_b,
                w_o, ln2_g, w_rg, b_rg, w_re, b_re, w_e_gate, w_e_up, w_e_down):
    D = w_in.shape[1]
    a_w = 3 * 3 * SWA_GW
    dn_w = DN_HEADS * 3 * DN_DK
    hv = DN_HEADS * DN_DK
    w = w_in[l]
    splits = dict(att=w[:, :a_w], dn=w[:, a_w:a_w + dn_w],
                  ba=jnp.pad(w[:, a_w + dn_w:a_w + dn_w + 2 * DN_HEADS], ((0, 0), (0, LANES - 2 * DN_HEADS))),
                  gate=w[:, a_w + dn_w + 2 * DN_HEADS:])
    assert splits["gate"].shape[1] == hv + 2 * D
    tile_heads = lambda g: jnp.broadcast_to(g[:, None, :], (len(SWA_CONFIGS), SWA_HEADS, SWA_DIM)).reshape(len(SWA_CONFIGS), SWA_GW)
    qg, kg = tile_heads(q_norm_g[l]), tile_heads(k_norm_g[l])
    idx = jnp.arange(MXU) // SWA_DIM
    n_g = len(SWA_CONFIGS)
    par = jnp.zeros((2, LANES), F32)
    par = par.at[0, DN_HEADS:2 * DN_HEADS].set(-jnp.exp(dn_a_log[l].astype(F32)))
    par = par.at[1, DN_HEADS:2 * DN_HEADS].set(dn_dt_bias[l].astype(F32))
    wr = jnp.pad(jnp.concatenate([w_rg[l], w_re[l]], axis=1), ((0, 0), (0, LANES - N_GROUPS - N_EXPERTS)))
    br = jnp.pad(jnp.concatenate([b_rg[l], b_re[l]]), (0, LANES - N_GROUPS - N_EXPERTS)).reshape(1, LANES)
    e8 = _head_indicator(hv, DN_DK)
    return dict(
        bf16={k: v.astype(BF16) for k, v in splits.items()},
        ln1=ln1_g[l].reshape(1, D), ln2=ln2_g[l].reshape(1, D),
        qg=qg, kg=kg,
        w_grp=[jnp.concatenate([w[:, s * n_g * SWA_GW + g * SWA_GW:s * n_g * SWA_GW + (g + 1) * SWA_GW]
                                for s in range(3)], axis=1).astype(BF16) for g in range(n_g)],
        ng_grp=[jnp.concatenate([qg[g] * SWA_DIM ** -0.5, kg[g]]).reshape(1, 1, 2 * SWA_GW) for g in range(n_g)],
        bd=((idx[:, None] == idx[None, :]).astype(F32) / SWA_DIM).astype(BF16),
        conv_w=dn_conv_w[l], par=par, dng=dn_norm_g[l].reshape(1, DN_DK),
        wa=w_out_a[l].astype(BF16), wb=w_out_b[l].astype(BF16), wo=w_o[l].astype(BF16), wr=wr.astype(BF16), br=br,
        wg=w_e_gate, wu=w_e_up, wd=w_e_down, layer=l,
        e_dn=e8, etb=e8.T, etg=jnp.roll(e8, DN_HEADS, axis=1).T,
        e_att=_head_indicator(SWA_GW, SWA_DIM).T.astype(BF16),
    )


def _layer_prompt(x, p):
    B, L, D = x.shape
    N = B * L
    x2d = x.reshape(N, D)
    bw = p["bf16"]
    pks, tails = [], []
    hgs = _norm_permute(x, p["ln1"], tuple(d for _, d in SWA_CONFIGS), tm=min(512, L))
    for g, (win, dil) in enumerate(SWA_CONFIGS):
        assert L >= win
        pk, tail = _proj_attn(hgs[g], p["w_grp"][g], p["ng_grp"][g], p["bd"], tmr=min(512, L // dil))
        pks.append(pk)
        tails.append(tail)
    tmp = min(1024, N)
    h0 = hgs[0].reshape(B, L, D)
    qd, kd, vd, raw_tail = _proj_dn(h0, bw["dn"], p["conv_w"], tm=min(512, L))
    gates = _matmul(h0.reshape(N, D), bw["gate"], tm=tmp, tn=1536, out_dtype=BF16, name="proj_gate")
    ba = _matmul(h0.reshape(N, D), bw["ba"], tm=tmp, tn=LANES, out_dtype=F32, name="proj_ba")
    os_, ls_ = [], []
    for pk in pks:
        d, M = pk.shape[1], pk.shape[2]
        o, lse = _attn(pk.reshape(B * d, M, pk.shape[3]), tq=min(256, M))
        os_.append(o.reshape(B, d, M, SWA_GW))
        ls_.append(lse.reshape(B, d, M, LANES))
    gb, gt_rows = _dn_gates(ba.reshape(B, L, LANES), p["par"], tl=min(256, L))
    u, w, qdec, kdec, a, gt = _dn_intra(qd, kd, vd, gb, gt_rows, tl=min(2048, L))
    od, s_new = _dn_scan(u, w, qdec, kdec, a, gt, jnp.zeros((B, DN_HEADS, DN_DK, LANES), F32), tl=min(512, L),
                         bb=2 if B % 2 == 0 else 1)
    x2 = _out_proj(x2d, os_, ls_, od.reshape(N, -1), gates, p["dng"], p["wa"], p["wb"], p["wo"], p["e_att"],
                   B=B, L=L, tm=min(512, L))
    y = _moe(x2, p["ln2"], p["wr"], p["br"], p["wg"], p["wu"], p["wd"], p["layer"], tm=256)
    return y.reshape(B, L, D), tails, raw_tail[:, 8 - (DN_CONV - 1):], s_new


def _layer_sample(x, caches, conv_state, s0, layer, p):
    Bs, T, D = x.shape
    assert T == 1
    x2d = x.reshape(Bs, D)
    bw = p["bf16"]
    proj = functools.partial(_proj_plain, x2d, p["ln1"], tm=Bs, out_dtype=F32)
    z_att = proj(bw["att"], tn=1536, name="sproj_att")
    raw = proj(bw["dn"], tn=1536, name="sproj_dn")
    gates = proj(bw["gate"], tn=1536, name="sproj_gate")
    ba = proj(bw["ba"], tn=LANES, name="sproj_ba")
    oa, kv = _sample_attn(z_att.reshape(Bs, 1, -1), caches, layer, p["qg"], p["kg"])
    raw3 = raw.reshape(Bs, 1, -1)
    od, s_new = _sample_dn(raw3, conv_state, s0, layer, p["conv_w"], ba.reshape(Bs, 1, LANES), p["par"],
                           p["e_dn"], p["etb"], p["etg"])
    x2 = _sample_out(x2d, oa.reshape(Bs, -1), od.reshape(Bs, -1), gates, p["dng"], p["wa"], p["wb"], p["wo"])
    y = _moe(x2, p["ln2"], p["wr"], p["br"], p["wg"], p["wu"], p["wd"], p["layer"], tm=Bs)
    W2 = 2 * SWA_GW
    kvs = [kv[:, :, g * W2:(g + 1) * W2].reshape(Bs, 1, 2, SWA_HEADS, SWA_DIM) for g in range(len(SWA_CONFIGS))]
    new_conv = jnp.concatenate([conv_state[layer][:, 1:], raw3], axis=1)
    return y.reshape(Bs, 1, D), kvs, new_conv, s_new


def kernel(x_prompt, x_sample, cache_swa0_kv, cache_swa1_kv, cache_swa2_kv, state_dn_conv, state_dn_S, ln1_g, w_in,
           q_norm_g, k_norm_g, dn_conv_w, dn_a_log, dn_dt_bias, dn_norm_g, w_out_a, w_out_b, w_o, ln2_g, w_rg, b_rg,
           w_re, b_re, w_e_gate, w_e_up, w_e_down):
    yp, ys = x_prompt, x_sample
    outs = [[] for _ in range(10)]
    for l in range(w_in.shape[0]):
        p = _prep_layer(l, ln1_g, w_in, q_norm_g, k_norm_g, dn_conv_w, dn_a_log, dn_dt_bias, dn_norm_g, w_out_a,
                        w_out_b, w_o, ln2_g, w_rg, b_rg, w_re, b_re, w_e_gate, w_e_up, w_e_down)
        yp, pkv, pconv, ps = _layer_prompt(yp, p)
        ys, skv, sconv, ss = _layer_sample(ys, (cache_swa0_kv, cache_swa1_kv, cache_swa2_kv), state_dn_conv,
                                           state_dn_S, l, p)
        for lst, val in zip(outs, (*pkv, pconv, ps, *skv, sconv, ss)):
            lst.append(val)
    return (yp, ys, *(jnp.stack(o) for o in outs))
```

```python
import functools

import jax
import jax.numpy as jnp
from jax import lax
from jax.experimental import pallas as pl
from jax.experimental.pallas import tpu as pltpu

F32 = jnp.float32
BF16 = jnp.bfloat16
HI = lax.Precision.HIGHEST
EPS = 1e-6

SWA_CONFIGS = ((128, 1), (512, 4), (2048, 16))
SWA_HEADS = 8
SWA_DIM = 64
SWA_GW = SWA_HEADS * SWA_DIM
SWA_SPAN = 128
DN_HEADS = 8
DN_DK = 128
DN_CONV = 4
DN_CHUNK = 64
N_GROUPS = 4
PER_GROUP = 8
N_EXPERTS = N_GROUPS * PER_GROUP
TOP_K = 2

VMEM_LIMIT_BYTES = 56 * 1024 * 1024
LANES = 128
MXU = 256
MOE_ROWS = 512
MOE_ROWS_SMALL = 128
ROW_UNROLL = 8
ROUTER_ROWS = 512


def _cparams(n_axes):
    return pltpu.CompilerParams(
        dimension_semantics=("arbitrary",) * n_axes, vmem_limit_bytes=VMEM_LIMIT_BYTES
    )


def _rms(x, g):
    return x * lax.rsqrt(jnp.mean(x * x, axis=-1, keepdims=True) + EPS) * g


def _bdot(a, b):
    return jnp.dot(a.astype(BF16), b.astype(BF16), preferred_element_type=F32)


def _hdot(a, b):
    return jnp.dot(a, b, preferred_element_type=F32, precision=HI)


def _sigmoid(x):
    return 0.5 * jnp.tanh(0.5 * x) + 0.5


def _silu(x):
    half = 0.5 * x
    return half * jnp.tanh(half) + half


def _pack_bf16_pairs(x):
    w = x.shape[1] // 2
    lo = lax.bitcast_convert_type(x[:, :w].astype(BF16).astype(F32), jnp.uint32) >> 16
    hi = lax.bitcast_convert_type(x[:, w:].astype(BF16).astype(F32), jnp.uint32) & jnp.uint32(0xFFFF0000)
    return lo | hi


def _unpack_bf16_pairs(p):
    lo = lax.bitcast_convert_type(p << 16, F32)
    hi = lax.bitcast_convert_type(p & jnp.uint32(0xFFFF0000), F32)
    return jnp.concatenate([lo, hi], axis=-1)


def _softplus(x):
    return jnp.maximum(x, 0.0) + jnp.log1p(jnp.exp(-jnp.abs(x)))


def _norm_permute_kernel(x_ref, lng_ref, *refs, tm, dils):
    outs, h_scr = refs[:-1], refs[-1]
    h = _rms(x_ref[0], lng_ref[...])
    n_cb = h_scr.shape[0]
    for cb in range(n_cb):
        h_scr[cb] = h[:, cb * LANES:(cb + 1) * LANES]
    for o_ref, d in zip(outs, dils):
        for cb in range(n_cb):
            for r in range(d):
                src = h_scr[cb] if d == 1 else h_scr[cb, pl.ds(r, tm // d, stride=d), :]
                o_ref[0, r, :, cb * LANES:(cb + 1) * LANES] = src.astype(BF16)


def _norm_permute(x, ln_g, dils, *, tm):
    B, L, D = x.shape
    assert L % tm == 0 and all(tm % (16 * d) == 0 for d in dils)
    return pl.pallas_call(
        functools.partial(_norm_permute_kernel, tm=tm, dils=dils),
        grid=(B, L // tm),
        in_specs=[pl.BlockSpec((1, tm, D), lambda b, i: (b, i, 0)), pl.BlockSpec((1, D), lambda b, i: (0, 0))],
        out_specs=[pl.BlockSpec((1, d, tm // d, D), lambda b, i: (b, 0, i, 0)) for d in dils],
        out_shape=[jax.ShapeDtypeStruct((B, d, L // d, D), BF16) for d in dils],
        scratch_shapes=[pltpu.VMEM((D // LANES, tm, LANES), F32)],
        compiler_params=_cparams(2),
        name="norm_permute",
    )(x, ln_g)


def _proj_attn_kernel(h_ref, w_ref, ng_ref, bd_ref, p_ref, t_ref, *, n_tiles):
    rows = h_ref.shape[2]
    z = jnp.dot(h_ref[0, 0], w_ref[...], preferred_element_type=F32)
    kv = []
    for c in range(0, 3 * SWA_GW, MXU):
        zc = z[:, c:c + MXU]
        if c < 2 * SWA_GW:
            ms = jnp.dot((zc * zc).astype(BF16), bd_ref[...], preferred_element_type=F32)
            zc = zc * lax.rsqrt(ms + EPS) * ng_ref[0, :, c:c + MXU]
        p_ref[0, 0, :, c:c + MXU] = zc.astype(BF16)
        if c >= SWA_GW:
            kv.append(zc[rows - SWA_SPAN:rows, :])

    @pl.when(pl.program_id(2) == n_tiles - 1)
    def _():
        per = SWA_GW // MXU
        for s in range(2):
            zr = jnp.concatenate(kv[s * per:(s + 1) * per], axis=-1)
            t_ref[0, :, s] = zr.reshape(SWA_SPAN, SWA_HEADS, SWA_DIM)


def _proj_attn(hg, w_g, ng_g, bd, *, tmr):
    B, dil, M, D = hg.shape
    assert M % tmr == 0 and tmr >= SWA_SPAN
    nt = M // tmr
    W3 = 3 * SWA_GW
    keep = SWA_SPAN * dil
    p, t = pl.pallas_call(
        functools.partial(_proj_attn_kernel, n_tiles=nt),
        grid=(B, dil, nt),
        in_specs=[
            pl.BlockSpec((1, 1, tmr, D), lambda b, r, i: (b, r, i, 0)),
            pl.BlockSpec((D, W3), lambda b, r, i: (0, 0)),
            pl.BlockSpec((1, 1, 2 * SWA_GW), lambda b, r, i: (0, 0, 0)),
            pl.BlockSpec((MXU, MXU), lambda b, r, i: (0, 0)),
        ],
        out_specs=[
            pl.BlockSpec((1, 1, tmr, W3), lambda b, r, i: (b, r, i, 0)),
            pl.BlockSpec((1, SWA_SPAN, None, 2, SWA_HEADS, SWA_DIM), lambda b, r, i: (b, 0, r, 0, 0, 0)),
        ],
        out_shape=[
            jax.ShapeDtypeStruct((B, dil, M, W3), BF16),
            jax.ShapeDtypeStruct((B, SWA_SPAN, dil, 2, SWA_HEADS, SWA_DIM), F32),
        ],
        compiler_params=_cparams(3),
        name="proj_attn",
    )(hg, w_g, ng_g, bd)
    return p, t.reshape(B, keep, 2, SWA_HEADS, SWA_DIM)


def _proj_plain_kernel(x_ref, lng_ref, w_ref, o_ref, h_scr):
    @pl.when(pl.program_id(1) == 0)
    def _():
        h_scr[...] = _rms(x_ref[...], lng_ref[...]).astype(BF16)

    o_ref[...] = jnp.dot(h_scr[...], w_ref[...], preferred_element_type=F32).astype(o_ref.dtype)


def _proj_plain(x2d, ln_g, w, *, tm, tn, out_dtype, name="proj_plain"):
    N, D = x2d.shape
    C = w.shape[1]
    assert N % tm == 0 and C % tn == 0
    return pl.pallas_call(
        _proj_plain_kernel,
        grid=(N // tm, C // tn),
        in_specs=[
            pl.BlockSpec((tm, D), lambda i, j: (i, 0)),
            pl.BlockSpec((1, D), lambda i, j: (0, 0)),
            pl.BlockSpec((D, tn), lambda i, j: (0, j)),
        ],
        out_specs=pl.BlockSpec((tm, tn), lambda i, j: (i, j)),
        out_shape=jax.ShapeDtypeStruct((N, C), out_dtype),
        scratch_shapes=[pltpu.VMEM((tm, D), BF16)],
        compiler_params=_cparams(2),
        name=name,
    )(x2d, ln_g, w)


def _matmul_kernel(h_ref, w_ref, o_ref):
    o_ref[...] = jnp.dot(h_ref[...], w_ref[...], preferred_element_type=F32).astype(o_ref.dtype)


def _matmul(h2d, w, *, tm, tn, out_dtype, name):
    N, D = h2d.shape
    C = w.shape[1]
    assert N % tm == 0 and C % tn == 0
    return pl.pallas_call(
        _matmul_kernel,
        grid=(N // tm, C // tn),
        in_specs=[pl.BlockSpec((tm, D), lambda i, j: (i, 0)), pl.BlockSpec((D, tn), lambda i, j: (0, j))],
        out_specs=pl.BlockSpec((tm, tn), lambda i, j: (i, j)),
        out_shape=jax.ShapeDtypeStruct((N, C), out_dtype),
        compiler_params=_cparams(2),
        name=name,
    )(h2d, w)


def _attn_kernel(q_ref, kc_ref, vc_ref, kp_ref, vp_ref, o_ref, lse_ref, kk_scr, vv_scr, *, tq):
    i = pl.program_id(1)
    blk = SWA_SPAN
    kk_scr[0:blk, :] = kp_ref[0]
    kk_scr[blk:blk + tq, :] = kc_ref[0]
    vv_scr[0:blk, :] = vp_ref[0]
    vv_scr[blk:blk + tq, :] = vc_ref[0]
    qi = lax.broadcasted_iota(jnp.int32, (blk, 2 * blk), 0)
    ki = lax.broadcasted_iota(jnp.int32, (blk, 2 * blk), 1)
    dist = blk + qi - ki
    band = (dist >= 0) & (dist <= SWA_SPAN)
    band_first = band & ((ki >= blk) | (i > 0))
    lo = lax.broadcasted_iota(jnp.int32, (blk, LANES), 1) < SWA_DIM
    zero = jnp.zeros((blk, LANES), BF16)
    lane = lax.broadcasted_iota(jnp.int32, (blk, LANES), 1)
    for jb in range(tq // blk):
        mask = band_first if jb == 0 else band
        rows = slice(jb * blk, (jb + 1) * blk)
        lse_all = jnp.zeros((blk, LANES), F32)
        for hp in range(SWA_GW // LANES):
            cs = slice(hp * LANES, (hp + 1) * LANES)
            qb = q_ref[0, rows, cs]
            kk = kk_scr[jb * blk:(jb + 2) * blk, cs]
            vv = vv_scr[jb * blk:(jb + 2) * blk, cs]
            res_o = []
            for hh in range(2):
                qm = jnp.where(lo if hh == 0 else jnp.logical_not(lo), qb, zero)
                s = lax.dot_general(qm, kk, (((1,), (1,)), ((), ())), preferred_element_type=F32)
                s = jnp.where(mask, s, -jnp.inf)
                m = jnp.max(s, axis=-1, keepdims=True)
                p = jnp.exp(s - m)
                den = jnp.sum(p, axis=-1, keepdims=True)
                pv = jnp.dot(p.astype(BF16), vv, preferred_element_type=F32)
                res_o.append(pv / den)
                lse_all = jnp.where(lane == 2 * hp + hh, m + jnp.log(den), lse_all)
            o_ref[0, rows, cs] = jnp.where(lo, res_o[0], res_o[1]).astype(BF16)
        lse_ref[0, rows, :] = lse_all


def _attn(p, *, tq):
    S, M, _ = p.shape
    assert M % tq == 0 and tq % SWA_SPAN == 0
    nb = tq // SWA_SPAN
    return pl.pallas_call(
        functools.partial(_attn_kernel, tq=tq),
        grid=(S, M // tq),
        in_specs=[
            pl.BlockSpec((1, tq, SWA_GW), lambda s, i: (s, i, 0)),
            pl.BlockSpec((1, tq, SWA_GW), lambda s, i: (s, i, 1)),
            pl.BlockSpec((1, tq, SWA_GW), lambda s, i: (s, i, 2)),
            pl.BlockSpec((1, SWA_SPAN, SWA_GW), lambda s, i: (s, jnp.maximum(i * nb - 1, 0), 1)),
            pl.BlockSpec((1, SWA_SPAN, SWA_GW), lambda s, i: (s, jnp.maximum(i * nb - 1, 0), 2)),
        ],
        out_specs=[
            pl.BlockSpec((1, tq, SWA_GW), lambda s, i: (s, i, 0)),
            pl.BlockSpec((1, tq, LANES), lambda s, i: (s, i, 0)),
        ],
        out_shape=[
            jax.ShapeDtypeStruct((S, M, SWA_GW), BF16),
            jax.ShapeDtypeStruct((S, M, LANES), F32),
        ],
        scratch_shapes=[
            pltpu.VMEM((SWA_SPAN + tq, SWA_GW), BF16),
            pltpu.VMEM((SWA_SPAN + tq, SWA_GW), BF16),
        ],
        compiler_params=_cparams(2),
        name="swa_attn",
    )(p, p, p, p, p)


def _proj_dn_kernel(h_ref, w_ref, cw_ref, q_ref, k_ref, v_ref, tail_ref, z_scr, carry_scr, *, tm, n_ct):
    i = pl.program_id(1)
    j = pl.program_id(2)
    nh = DN_HEADS
    ncb = z_scr.shape[1] // LANES

    z_scr[0:8, :] = jnp.where(i == 0, 0.0, carry_scr[j])
    z_scr[8:8 + tm, :] = jnp.dot(h_ref[0], w_ref[...], preferred_element_type=F32)
    last = z_scr[tm:tm + 8, :]
    carry_scr[j] = last
    tn = z_scr.shape[1]
    outs = (q_ref, k_ref, v_ref)
    for jj in range(n_ct):

        @pl.when(j == jj)
        def _(jj=jj):
            tail_ref[0, :, jj * tn:(jj + 1) * tn] = last
            for cbl in range(ncb):
                cs = slice(cbl * LANES, (cbl + 1) * LANES)
                part, h = divmod(jj * ncb + cbl, nh)
                xe = z_scr[:, cs]
                acc = (0.5 * cw_ref[0:1, cs]) * xe
                for t in range(1, DN_CONV):
                    acc = (0.5 * cw_ref[t:t + 1, cs]) * xe + pltpu.roll(acc, 1, axis=0)
                half = acc[8:]
                act = half * jnp.tanh(half) + half
                if part < 2:
                    inv = lax.rsqrt(jnp.sum(act * act, axis=-1, keepdims=True) + EPS)
                    act = act * (inv * (DN_DK ** -0.5) if part == 0 else inv)
                outs[part][0, :, h * LANES:(h + 1) * LANES] = act.astype(BF16)


def _proj_dn(h, w_dn, conv_w, *, tm):
    B, L, D = h.shape
    C = w_dn.shape[1]
    width = DN_HEADS * DN_DK
    n_ct = 2
    tn = C // n_ct
    assert L % tm == 0 and C == 3 * width and tn % LANES == 0
    qkv = pl.BlockSpec((1, tm, width), lambda b, i, j: (b, i, 0))
    return pl.pallas_call(
        functools.partial(_proj_dn_kernel, tm=tm, n_ct=n_ct),
        grid=(B, L // tm, n_ct),
        in_specs=[
            pl.BlockSpec((1, tm, D), lambda b, i, j: (b, i, 0)),
            pl.BlockSpec((D, tn), lambda b, i, j: (0, j)),
            pl.BlockSpec((DN_CONV, tn), lambda b, i, j: (0, j)),
        ],
        out_specs=[qkv, qkv, qkv, pl.BlockSpec((1, 8, C), lambda b, i, j: (b, 0, 0))],
        out_shape=[jax.ShapeDtypeStruct((B, L, width), BF16)] * 3 + [jax.ShapeDtypeStruct((B, 8, C), F32)],
        scratch_shapes=[pltpu.VMEM((8 + tm, tn), F32), pltpu.VMEM((n_ct, 8, tn), F32)],
        compiler_params=_cparams(3),
        name="proj_dn",
    )(h, w_dn, conv_w)


def _gates_kernel(ba_ref, par_ref, g_ref, gt_ref, *, tl):
    nh = DN_HEADS
    ba = ba_ref[0]
    lane = lax.broadcasted_iota(jnp.int32, (tl, LANES), 1)
    g = par_ref[0:1, :] * _softplus(ba + par_ref[1:2, :])
    ri = lax.broadcasted_iota(jnp.int32, (tl, tl), 0)
    ci = lax.broadcasted_iota(jnp.int32, (tl, tl), 1)
    tri = jnp.where((ri // DN_CHUNK == ci // DN_CHUNK) & (ci <= ri), 1.0, 0.0).astype(BF16)
    g_hi = g.astype(BF16)
    r1 = g - g_hi.astype(F32)
    g_mid = r1.astype(BF16)
    g_lo = (r1 - g_mid.astype(F32)).astype(BF16)
    gc = sum(jnp.dot(tri, piece, preferred_element_type=F32) for piece in (g_hi, g_mid, g_lo))
    g_ref[0] = jnp.where(lane < nh, _sigmoid(ba), gc)
    gt_ref[0] = jnp.transpose(gc)[nh:2 * nh, :]


def _dn_gates(ba, par, *, tl):
    B, L, _ = ba.shape
    assert L % tl == 0 and tl % DN_CHUNK == 0
    return pl.pallas_call(
        functools.partial(_gates_kernel, tl=tl),
        grid=(B, L // tl),
        in_specs=[pl.BlockSpec((1, tl, LANES), lambda b, i: (b, i, 0)), pl.BlockSpec((2, LANES), lambda b, i: (0, 0))],
        out_specs=[pl.BlockSpec((1, tl, LANES), lambda b, i: (b, i, 0)),
                   pl.BlockSpec((1, DN_HEADS, tl), lambda b, i: (b, 0, i))],
        out_shape=[jax.ShapeDtypeStruct((B, L, LANES), F32), jax.ShapeDtypeStruct((B, DN_HEADS, L), F32)],
        compiler_params=_cparams(2),
        name="dn_gates",
    )(ba, par)


def _intra_kernel(q_ref, k_ref, v_ref, g_ref, gt_in_ref, u_ref, w_ref, qd_ref, kd_ref, a_ref, gt_ref, *, tl):
    h = pl.program_id(1)
    C = DN_CHUNK
    lane = lax.broadcasted_iota(jnp.int32, (C, LANES), 1)
    ri = lax.broadcasted_iota(jnp.int32, (C, C), 0)
    ci = lax.broadcasted_iota(jnp.int32, (C, C), 1)
    eye = jnp.where(ri == ci, 1.0, 0.0).astype(F32)
    nt_dot = lambda a, b: lax.dot_general(a.astype(BF16), b.astype(BF16), (((1,), (1,)), ((), ())),
                                          preferred_element_type=F32)
    rows = [slice(c * C, (c + 1) * C) for c in range(tl // C)]
    gv = [g_ref[0, r, :] for r in rows]
    q = [q_ref[0, r, :].astype(F32) for r in rows]
    k = [k_ref[0, r, :].astype(F32) for r in rows]
    v = [v_ref[0, r, :].astype(F32) for r in rows]
    beta = [jnp.sum(jnp.where(lane == h, x, 0.0), axis=-1, keepdims=True) for x in gv]
    gc = [jnp.sum(jnp.where(lane == h + DN_HEADS, x, 0.0), axis=-1, keepdims=True) for x in gv]
    gc_row = gt_in_ref[0, pl.ds(h, 1), :]
    decay = [jnp.exp(jnp.where(ri >= ci, a - gc_row[:, r], -jnp.inf)) for a, r in zip(gc, rows)]
    kb = [a * b for a, b in zip(k, beta)]
    kq = [nt_dot(jnp.concatenate([a, b], axis=0), c) for a, b, c in zip(kb, q, k)]
    x = [-jnp.where(ri > ci, m[:C] * d, 0.0) for m, d in zip(kq, decay)]
    t = [eye + a for a in x]
    x = [_bdot(a, a) for a in x]
    for _ in range(4):
        both = [_bdot(jnp.concatenate([a, b], axis=0), a) for a, b in zip(x, t)]
        t = [b + m[C:] for b, m in zip(t, both)]
        x = [m[:C] for m in both]
    t = [b + _bdot(b, a) for a, b in zip(x, t)]
    eg = [jnp.exp(a) for a in gc]
    glast = [a[C - 1:C, :] for a in gc]
    uw = [_bdot(a, jnp.concatenate([b * c, d * e], axis=1)) for a, b, c, d, e in zip(t, v, beta, kb, eg)]
    for c, r in enumerate(rows):
        u_ref[0, 0, r, :] = uw[c][:, :LANES]
        w_ref[0, 0, r, :] = uw[c][:, LANES:].astype(BF16)
        a_ref[0, 0, r, :] = (kq[c][C:] * decay[c]).astype(BF16)
        qd_ref[0, 0, r, :] = (q[c] * eg[c]).astype(BF16)
        kd_ref[0, 0, r, :] = (k[c] * jnp.exp(glast[c] - gc[c])).astype(BF16)
        gt_ref[0, 0, c:c + 1, :] = jnp.broadcast_to(jnp.exp(glast[c]), (1, LANES))


def _dn_intra(q, k, v, g, gt_rows, *, tl):
    B, L, _ = q.shape
    H, C = DN_HEADS, DN_CHUNK
    assert L % tl == 0 and (tl // C) % 8 == 0
    qkv_spec = pl.BlockSpec((1, tl, LANES), lambda b, h, i: (b, i, h))
    hl = lambda w: pl.BlockSpec((1, 1, tl, w), lambda b, h, i: (b, h, i, 0))
    return pl.pallas_call(
        functools.partial(_intra_kernel, tl=tl),
        grid=(B, H, L // tl),
        in_specs=[qkv_spec, qkv_spec, qkv_spec, pl.BlockSpec((1, tl, LANES), lambda b, h, i: (b, i, 0)),
                  pl.BlockSpec((1, H, tl), lambda b, h, i: (b, 0, i))],
        out_specs=[hl(LANES), hl(LANES), hl(LANES), hl(LANES), hl(C),
                   pl.BlockSpec((1, 1, tl // C, LANES), lambda b, h, i: (b, h, i, 0))],
        out_shape=[
            jax.ShapeDtypeStruct((B, H, L, LANES), F32),
            jax.ShapeDtypeStruct((B, H, L, LANES), BF16),
            jax.ShapeDtypeStruct((B, H, L, LANES), BF16),
            jax.ShapeDtypeStruct((B, H, L, LANES), BF16),
            jax.ShapeDtypeStruct((B, H, L, C), BF16),
            jax.ShapeDtypeStruct((B, H, L // C, LANES), F32),
        ],
        compiler_params=_cparams(3),
        name="dn_intra",
    )(q, k, v, g, gt_rows)


def _scan_kernel(u_ref, w_ref, qd_ref, kd_ref, a_ref, gt_ref, s0_ref, o_ref, s_ref, *, n_chunks):
    C = DN_CHUNK
    bb, H = s_ref.shape[0], s_ref.shape[1]
    seqs = [(b, h) for b in range(bb) for h in range(H)]

    @pl.when(pl.program_id(1) == 0)
    def _():
        s_ref[...] = s0_ref[...]

    def body(c, carry):
        rows = pl.ds(pl.multiple_of(c * C, C), C)
        S = [s_ref[b, h] for b, h in seqs]
        Sb = [x.astype(BF16) for x in S]
        v_new = [u_ref[b, h, rows, :] - jnp.dot(w_ref[b, h, rows, :], sb, preferred_element_type=F32)
                 for (b, h), sb in zip(seqs, Sb)]
        vb = [x.astype(BF16) for x in v_new]
        o = [jnp.dot(qd_ref[b, h, rows, :], sb, preferred_element_type=F32)
             + jnp.dot(a_ref[b, h, rows, :], v, preferred_element_type=F32) for (b, h), sb, v in zip(seqs, Sb, vb)]
        upd = [lax.dot_general(kd_ref[b, h, rows, :], v, (((0,), (0,)), ((), ())), preferred_element_type=F32)
               for (b, h), v in zip(seqs, vb)]
        for n, (b, h) in enumerate(seqs):
            o_ref[b, rows, h * LANES:(h + 1) * LANES] = o[n]
            s_ref[b, h] = S[n] * gt_ref[b, h, pl.ds(c, 1), :] + upd[n]
        return carry

    lax.fori_loop(0, n_chunks, body, 0)


def _dn_scan(u, w, qd, kd, a, gt, s0, *, tl, bb):
    B, H, L, _ = u.shape
    C = DN_CHUNK
    assert L % tl == 0 and (tl // C) % 8 == 0 and B % bb == 0
    hs = lambda wd: pl.BlockSpec((bb, H, tl, wd), lambda b, i: (b, 0, i, 0))
    s_spec = pl.BlockSpec((bb, H, DN_DK, LANES), lambda b, i: (b, 0, 0, 0))
    return pl.pallas_call(
        functools.partial(_scan_kernel, n_chunks=tl // C),
        grid=(B // bb, L // tl),
        in_specs=[hs(LANES), hs(LANES), hs(LANES), hs(LANES), hs(C),
                  pl.BlockSpec((bb, H, tl // C, LANES), lambda b, i: (b, 0, i, 0)), s_spec],
        out_specs=[pl.BlockSpec((bb, tl, H * LANES), lambda b, i: (b, i, 0)), s_spec],
        out_shape=[jax.ShapeDtypeStruct((B, L, H * LANES), F32),
                   jax.ShapeDtypeStruct((B, H, DN_DK, LANES), F32)],
        compiler_params=_cparams(2),
        name="dn_scan",
    )(u, w, qd, kd, a, gt, s0)


def _gated_mix(o_a, od, gates, dng, wa, wb, wo, x, dot):
    width = DN_HEADS * DN_DK
    parts = []
    for h in range(DN_HEADS):
        blk = od[:, h * LANES:(h + 1) * LANES]
        parts.append(blk * lax.rsqrt(jnp.mean(blk * blk, axis=-1, keepdims=True) + EPS) * dng)
    odn = jnp.concatenate(parts, axis=-1) * _silu(gates[:, 0:width].astype(F32))
    ya = dot(o_a, wa)
    yb = dot(odn, wb)
    mix = _sigmoid(gates[:, width:2 * width].astype(F32)) * ya + _sigmoid(gates[:, 2 * width:].astype(F32)) * yb
    return x + dot(mix, wo)


def _out_kernel(x_ref, o0, o1, o2, l0, l1, l2, od_ref, gates_ref, dng_ref, wa_ref, wb_ref, wo_ref, e_ref, y_ref,
                so0, so1, so2, sl0, sl1, sl2, *, tm, dils):
    o_refs, l_refs = (o0, o1, o2), (l0, l1, l2)
    so, sl = (so0, so1, so2), (sl0, sl1, sl2)
    for gi, d in enumerate(dils):
        for r in range(d):
            dst = slice(None) if d == 1 else pl.ds(r, tm // d, stride=d)
            sl[gi][dst, :] = l_refs[gi][0, r]
            for cb in range(SWA_GW // LANES):
                so[gi][cb, dst, :] = o_refs[gi][0, r, :, cb * LANES:(cb + 1) * LANES].astype(F32)
    ls = [s[...] for s in sl]
    m = jnp.maximum(jnp.maximum(ls[0], ls[1]), ls[2])
    es = [jnp.exp(l - m) for l in ls]
    tot = es[0] + es[1] + es[2]
    alphas = [jnp.dot((e / tot).astype(BF16), e_ref[...], preferred_element_type=F32) for e in es]
    parts = []
    for cb in range(SWA_GW // LANES):
        cs = slice(cb * LANES, (cb + 1) * LANES)
        parts.append(alphas[0][:, cs] * so[0][cb] + alphas[1][:, cs] * so[1][cb] + alphas[2][:, cs] * so[2][cb])
    o_a = jnp.concatenate(parts, axis=-1)
    y_ref[...] = _gated_mix(o_a, od_ref[...], gates_ref[...], dng_ref[...], wa_ref[...], wb_ref[...],
                            wo_ref[...], x_ref[...], _bdot)


def _out_proj(x2d, os_, ls_, od2d, gates, dng, wa, wb, wo, e_att, *, B, L, tm):
    N, D = x2d.shape
    nt = L // tm
    dils = tuple(d for _, d in SWA_CONFIGS)
    grp = lambda d, w: pl.BlockSpec((1, d, tm // d, w), lambda i: (i // nt, 0, i % nt, 0))
    row = lambda w: pl.BlockSpec((tm, w), lambda i: (i, 0))
    full = lambda a: pl.BlockSpec(a.shape, lambda i: (0, 0))
    return pl.pallas_call(
        functools.partial(_out_kernel, tm=tm, dils=dils),
        grid=(N // tm,),
        in_specs=[row(D)] + [grp(d, SWA_GW) for d in dils] + [grp(d, LANES) for d in dils]
        + [row(od2d.shape[1]), row(gates.shape[1]), full(dng), full(wa), full(wb), full(wo), full(e_att)],
        out_specs=row(D),
        out_shape=jax.ShapeDtypeStruct((N, D), F32),
        scratch_shapes=[pltpu.VMEM((SWA_GW // LANES, tm, LANES), F32)] * 3 + [pltpu.VMEM((tm, LANES), F32)] * 3,
        compiler_params=_cparams(1),
        name="out_proj",
    )(x2d, *os_, *ls_, od2d, gates, dng, wa, wb, wo, e_att)


def _router_kernel(x_ref, lng_ref, wr_ref, br_ref, info_ref, cnt_ref, base_scr, *, tm):
    i = pl.program_id(0)

    @pl.when(i == 0)
    def _():
        base_scr[...] = jnp.zeros_like(base_scr)

    h = _rms(x_ref[...], lng_ref[...])
    lg = _bdot(h, wr_ref[...]) + br_ref[...]
    lane = lax.broadcasted_iota(jnp.int32, (tm, LANES), 1)
    big = jnp.int32(1 << 20)
    ninf = -jnp.inf

    def argmax_lane(vals):
        mx = jnp.max(vals, axis=-1, keepdims=True)
        idx = jnp.min(jnp.where(vals == mx, lane, big), axis=-1, keepdims=True)
        return mx, idx

    lgm = jnp.where(lane < N_GROUPS, lg, ninf)
    mg, gsel = argmax_lane(lgm)
    pg = 1.0 / jnp.sum(jnp.exp(lgm - mg), axis=-1, keepdims=True)
    start = N_GROUPS + gsel * PER_GROUP
    le = jnp.where((lane >= start) & (lane < start + PER_GROUP), lg, ninf)
    m1, i1 = argmax_lane(le)
    m2, i2 = argmax_lane(jnp.where(lane == i1, ninf, le))
    e21 = jnp.exp(m2 - m1)
    w1 = pg / (1.0 + e21)
    w2 = pg * e21 / (1.0 + e21)
    oh = jnp.where(lane == i1, 1.0, 0.0) + jnp.where(lane == i2, 1.0, 0.0)
    ri = lax.broadcasted_iota(jnp.int32, (tm, tm), 0)
    ci = lax.broadcasted_iota(jnp.int32, (tm, tm), 1)
    strict = jnp.where(ci < ri, 1.0, 0.0).astype(BF16)
    pref = jnp.dot(strict, oh.astype(BF16), preferred_element_type=F32) + base_scr[...]
    r1 = jnp.sum(jnp.where(lane == i1, pref, 0.0), axis=-1, keepdims=True)
    r2 = jnp.sum(jnp.where(lane == i2, pref, 0.0), axis=-1, keepdims=True)
    base_scr[...] = base_scr[...] + jnp.sum(oh, axis=0, keepdims=True)
    cnt_ref[...] = base_scr[...]
    off = jnp.float32(N_GROUPS)
    info = jnp.where(lane == 0, i1.astype(F32) - off, 0.0)
    info = jnp.where(lane == 1, i2.astype(F32) - off, info)
    info = jnp.where(lane == 2, w1, info)
    info = jnp.where(lane == 3, w2, info)
    info = jnp.where(lane == 4, r1, info)
    info = jnp.where(lane == 5, r2, info)
    info_ref[...] = info


def _router(x2d, ln_g, wr, br, *, tm):
    N, D = x2d.shape
    assert N % tm == 0
    return pl.pallas_call(
        functools.partial(_router_kernel, tm=tm),
        grid=(N // tm,),
        in_specs=[
            pl.BlockSpec((tm, D), lambda i: (i, 0)),
            pl.BlockSpec((1, D), lambda i: (0, 0)),
            pl.BlockSpec((D, LANES), lambda i: (0, 0)),
            pl.BlockSpec((1, LANES), lambda i: (0, 0)),
        ],
        out_specs=[pl.BlockSpec((tm, LANES), lambda i: (i, 0)), pl.BlockSpec((1, LANES), lambda i: (0, 0))],
        out_shape=[jax.ShapeDtypeStruct((N, LANES), F32), jax.ShapeDtypeStruct((1, LANES), F32)],
        scratch_shapes=[pltpu.VMEM((1, LANES), F32)],
        compiler_params=_cparams(1),
        name="router",
    )(x2d, ln_g, wr, br)


def _dispatch_kernel(dest_ref, zb_ref, x_ref, lng_ref, xs_ref, zero_scr, rows_scr, sem, *, tm, tb, n_zb, n_tiles):
    i = pl.program_id(0)

    @pl.when(i == 0)
    def _():
        zero_scr[...] = jnp.zeros_like(zero_scr)

        def zero_copy(n):
            return pltpu.make_async_copy(zero_scr, xs_ref.at[pl.ds(zb_ref[n] * tb, tb)], sem.at[2])

        def zero_issue(n, carry):
            @pl.when(zb_ref[n] >= 0)
            def _():
                zero_copy(n).start()

            return carry

        def zero_wait(n, carry):
            @pl.when(zb_ref[n] >= 0)
            def _():
                zero_copy(n).wait()

            return carry

        lax.fori_loop(0, n_zb, zero_issue, 0)
        lax.fori_loop(0, n_zb, zero_wait, 0)

    buf_now = lax.rem(i, 2)
    rows_scr[buf_now] = _pack_bf16_pairs(_rms(x_ref[...], lng_ref[...])).reshape(rows_scr.shape[1:])

    def row_copy(tile, t, slot):
        buf = lax.rem(tile, 2)
        return pltpu.make_async_copy(
            rows_scr.at[buf, pl.ds(t, 1)],
            xs_ref.at[pl.ds(dest_ref[(tile * tm + t) * TOP_K + slot], 1)], sem.at[buf])

    def issue(tt, carry):
        for r in range(ROW_UNROLL):
            for slot in range(TOP_K):
                row_copy(i, tt * ROW_UNROLL + r, slot).start(priority=slot)
        return carry

    def drain(tile):
        buf = lax.rem(tile, 2)
        for _ in range(TOP_K):
            pltpu.make_async_copy(rows_scr.at[buf], rows_scr.at[buf], sem.at[buf]).wait()

    lax.fori_loop(0, tm // ROW_UNROLL, issue, 0)

    @pl.when(i > 0)
    def _():
        drain(i - 1)

    @pl.when(i == n_tiles - 1)
    def _():
        drain(i)


def _dispatch(dest, zero_blocks, x2d, ln_g, *, tm, tb, n_rows):
    N, D = x2d.shape
    row = (D // 2 // LANES, LANES)
    return pl.pallas_call(
        functools.partial(_dispatch_kernel, tm=tm, tb=tb, n_zb=zero_blocks.shape[0], n_tiles=N // tm),
        grid_spec=pltpu.PrefetchScalarGridSpec(
            num_scalar_prefetch=2,
            grid=(N // tm,),
            in_specs=[pl.BlockSpec((tm, D), lambda i, d, z: (i, 0)), pl.BlockSpec((1, D), lambda i, d, z: (0, 0))],
            out_specs=pl.BlockSpec(memory_space=pl.ANY),
            scratch_shapes=[pltpu.VMEM((tb,) + row, jnp.uint32), pltpu.VMEM((2, tm) + row, jnp.uint32),
                            pltpu.SemaphoreType.DMA((3,))],
        ),
        out_shape=jax.ShapeDtypeStruct((n_rows,) + row, jnp.uint32),
        compiler_params=_cparams(1),
        name="moe_dispatch",
    )(dest, zero_blocks, x2d, ln_g)


def _ffn_kernel(be_ref, nb_ref, xs_ref, wg_ref, wu_ref, wd_ref, y_ref, wg_scr, wu_scr, wd_scr):
    i = pl.program_id(0)
    used = i < nb_ref[0]

    @pl.when(jnp.logical_or(i == 0, be_ref[i] != be_ref[jnp.maximum(i - 1, 0)]))
    def _():
        wg_scr[...] = wg_ref[0].astype(BF16)
        wu_scr[...] = wu_ref[0].astype(BF16)
        wd_scr[...] = wd_ref[0].astype(BF16)

    @pl.when(used)
    def _():
        h = _unpack_bf16_pairs(xs_ref[...].reshape(xs_ref.shape[0], -1)).astype(BF16)
        g = jnp.dot(h, wg_scr[...], preferred_element_type=F32)
        u = jnp.dot(h, wu_scr[...], preferred_element_type=F32)
        y = jnp.dot((_silu(g) * u).astype(BF16), wd_scr[...], preferred_element_type=F32)
        y_ref[...] = _pack_bf16_pairs(y).reshape(y_ref.shape)

    @pl.when(jnp.logical_not(used))
    def _():
        y_ref[...] = jnp.zeros_like(y_ref)


def _ffn(blk_e, nb_used, xs, wg, wu, wd, layer, *, tb):
    P, S, _ = xs.shape
    D = 2 * S * LANES
    nb = P // tb
    DE = wg.shape[3]
    return pl.pallas_call(
        _ffn_kernel,
        grid_spec=pltpu.PrefetchScalarGridSpec(
            num_scalar_prefetch=2,
            grid=(nb,),
            in_specs=[
                pl.BlockSpec((tb, S, LANES), lambda i, be, nbu: (jnp.minimum(i, nbu[0] - 1), 0, 0)),
                pl.BlockSpec((None, 1, D, DE), lambda i, be, nbu: (layer, be[i], 0, 0)),
                pl.BlockSpec((None, 1, D, DE), lambda i, be, nbu: (layer, be[i], 0, 0)),
                pl.BlockSpec((None, 1, DE, D), lambda i, be, nbu: (layer, be[i], 0, 0)),
            ],
            out_specs=pl.BlockSpec((tb, S, LANES), lambda i, be, nbu: (i, 0, 0)),
            scratch_shapes=[pltpu.VMEM((D, DE), BF16), pltpu.VMEM((D, DE), BF16), pltpu.VMEM((DE, D), BF16)],
        ),
        out_shape=jax.ShapeDtypeStruct((P, S, LANES), jnp.uint32),
        compiler_params=_cparams(1),
        name="moe_ffn",
    )(blk_e, nb_used, xs, wg, wu, wd)


def _combine_kernel(dest_ref, x_ref, info_ref, yb_ref, y_ref, g_scr, sem, *, tm, n_tiles):
    i = pl.program_id(0)

    def row_copy(tile, t, slot):
        buf = lax.rem(tile, 2)
        return pltpu.make_async_copy(
            yb_ref.at[pl.ds(dest_ref[(tile * tm + t) * TOP_K + slot], 1)],
            g_scr.at[buf, slot, pl.ds(t, 1)], sem.at[buf])

    def issue_tile(tile):
        def body(tt, carry):
            for r in range(ROW_UNROLL):
                for slot in range(TOP_K):
                    row_copy(tile, tt * ROW_UNROLL + r, slot).start(priority=slot)
            return carry

        lax.fori_loop(0, tm // ROW_UNROLL, body, 0)

    @pl.when(i == 0)
    def _():
        issue_tile(i)

    @pl.when(i + 1 < n_tiles)
    def _():
        issue_tile(i + 1)

    buf = lax.rem(i, 2)
    pltpu.make_async_copy(g_scr.at[buf], g_scr.at[buf], sem.at[buf]).wait()
    info = info_ref[...]
    lane = lax.broadcasted_iota(jnp.int32, info.shape, 1)
    w1 = jnp.sum(jnp.where(lane == 2, info, 0.0), axis=-1, keepdims=True)
    w2 = jnp.sum(jnp.where(lane == 3, info, 0.0), axis=-1, keepdims=True)
    g1 = _unpack_bf16_pairs(g_scr[buf, 0].reshape(tm, -1))
    g2 = _unpack_bf16_pairs(g_scr[buf, 1].reshape(tm, -1))
    y_ref[...] = x_ref[...] + (w1 * g1 + w2 * g2)


def _combine(dest, x2d, info, yb, *, tm):
    N, D = x2d.shape
    return pl.pallas_call(
        functools.partial(_combine_kernel, tm=tm, n_tiles=N // tm),
        grid_spec=pltpu.PrefetchScalarGridSpec(
            num_scalar_prefetch=1,
            grid=(N // tm,),
            in_specs=[
                pl.BlockSpec((tm, D), lambda i, d: (i, 0)),
                pl.BlockSpec((tm, LANES), lambda i, d: (i, 0)),
                pl.BlockSpec(memory_space=pl.ANY),
            ],
            out_specs=pl.BlockSpec((tm, D), lambda i, d: (i, 0)),
            scratch_shapes=[pltpu.VMEM((2, TOP_K, tm, D // 2 // LANES, LANES), jnp.uint32),
                            pltpu.SemaphoreType.DMA((2,))],
        ),
        out_shape=jax.ShapeDtypeStruct((N, D), F32),
        compiler_params=_cparams(1),
        name="moe_combine",
    )(dest, x2d, info, yb)


def _moe(x2d, ln2_g, wr, br, wg, wu, wd, layer, *, tm):
    N, D = x2d.shape
    tb = MOE_ROWS if N * TOP_K >= N_EXPERTS * MOE_ROWS else MOE_ROWS_SMALL
    info, counts = _router(x2d, ln2_g, wr, br, tm=ROUTER_ROWS if N % ROUTER_ROWS == 0 else tm)
    counts = counts[0, N_GROUPS:N_GROUPS + N_EXPERTS].astype(jnp.int32)
    pcounts = (counts + tb - 1) // tb * tb
    pend = jnp.cumsum(pcounts)
    pstart = pend - pcounts
    e = info[:, 0:TOP_K].astype(jnp.int32)
    rank = info[:, 4:4 + TOP_K].astype(jnp.int32)
    experts = jnp.arange(N_EXPERTS, dtype=jnp.int32)
    dest = (jnp.sum(jnp.where(e[..., None] == experts, pstart, 0), axis=-1) + rank).reshape(-1)
    nb = -(-(N * TOP_K) // tb) + N_EXPERTS
    P = nb * tb
    blocks = jnp.arange(nb, dtype=jnp.int32)
    blk_e = jnp.minimum(jnp.sum((pend[None, :] <= blocks[:, None] * tb).astype(jnp.int32), axis=1), N_EXPERTS - 1)
    nb_used = (pend[-1] // tb).astype(jnp.int32).reshape(1)
    zero_blocks = jnp.concatenate([jnp.where(counts % tb != 0, pend // tb - 1, -1),
                                   jnp.where(blocks >= nb_used[0], blocks, -1)]).astype(jnp.int32)
    xs = _dispatch(dest, zero_blocks, x2d, ln2_g, tm=tm, tb=tb, n_rows=P)
    yb = _ffn(blk_e, nb_used, xs, wg, wu, wd, layer, tb=tb)
    return _combine(dest, x2d, info, yb, tm=tm)


def _rows8(x):
    return jnp.broadcast_to(x, (8, x.shape[1]))


def _row_hdot(x, m):
    return _hdot(_rows8(x), m)[0:1]


def _bf_round(x):
    return x.astype(BF16).astype(F32)


def _sample_attn_kernel(z_ref, c0, c1, c2, qg_ref, kg_ref, oa_ref, kv_ref):
    W = SWA_GW
    scale = SWA_DIM ** -0.5
    z = z_ref[0]
    sub = lax.broadcasted_iota(jnp.int32, (SWA_HEADS, W), 0)
    lane = lax.broadcasted_iota(jnp.int32, (SWA_HEADS, W), 1)
    own = lane // SWA_DIM == sub

    def heads(row):
        return jnp.where(own, jnp.broadcast_to(row, (SWA_HEADS, W)), 0.0)

    def head_sum(row):
        return jnp.sum(heads(row), axis=-1, keepdims=True)

    def spread(col):
        return jnp.sum(jnp.where(own, col, 0.0), axis=0, keepdims=True)

    def headnorm(zz, g):
        return zz * spread(lax.rsqrt(head_sum(zz * zz) * (1.0 / SWA_DIM) + EPS)) * g

    outs, lses = [], []
    for gi, (c_ref, (win, dil)) in enumerate(zip((c0, c1, c2), SWA_CONFIGS)):
        q = headnorm(z[:, gi * W:(gi + 1) * W], qg_ref[gi:gi + 1, :])
        k = headnorm(z[:, 3 * W + gi * W:3 * W + (gi + 1) * W], kg_ref[gi:gi + 1, :])
        v = z[:, 6 * W + gi * W:6 * W + (gi + 1) * W]
        kv_ref[0, :, 2 * gi * W:(2 * gi + 1) * W] = k
        kv_ref[0, :, (2 * gi + 1) * W:(2 * gi + 2) * W] = v
        kc = c_ref[0].reshape(W, win).astype(BF16)
        vc = c_ref[1].reshape(W, win).astype(BF16)
        s_c = jnp.dot(heads(q).astype(BF16), kc, preferred_element_type=F32) * scale
        row = lax.broadcasted_iota(jnp.int32, s_c.shape, 1)
        s_c = jnp.where(row % dil == 0, s_c, -jnp.inf)
        s_n = head_sum(_bf_round(k) * _bf_round(q)) * scale
        m = jnp.maximum(jnp.max(s_c, axis=-1, keepdims=True), s_n)
        p_c = jnp.exp(s_c - m)
        p_n = jnp.exp(s_n - m)
        den = jnp.sum(p_c, axis=-1, keepdims=True) + p_n
        pv = lax.dot_general(p_c.astype(BF16), vc, (((1,), (1,)), ((), ())), preferred_element_type=F32)
        num = jnp.sum(jnp.where(own, pv, 0.0), axis=0, keepdims=True) + spread(_bf_round(p_n)) * _bf_round(v)
        outs.append(num / spread(den))
        lses.append(m + jnp.log(den))
    mm = jnp.maximum(jnp.maximum(lses[0], lses[1]), lses[2])
    es = [jnp.exp(l - mm) for l in lses]
    tot = es[0] + es[1] + es[2]
    oa_ref[0] = sum(spread(_bf_round(e / tot)) * _bf_round(o) for e, o in zip(es, outs))


def _sample_attn(z3, caches, layer, qg, kg):
    Bs = z3.shape[0]
    W = SWA_GW
    cviews, cspecs = [], []
    for (win, dil), c in zip(SWA_CONFIGS, caches):
        assert c.shape[2] == win
        cviews.append(jnp.transpose(c, (0, 1, 3, 4, 5, 2)))
        cspecs.append(pl.BlockSpec((None, None, 2, SWA_HEADS, SWA_DIM, win), lambda b: (layer, b, 0, 0, 0, 0)))
    full = lambda a: pl.BlockSpec(a.shape, lambda b: (0,) * a.ndim)
    return pl.pallas_call(
        _sample_attn_kernel,
        grid=(Bs,),
        in_specs=[pl.BlockSpec((1, 1, 9 * W), lambda b: (b, 0, 0))] + cspecs + [full(qg), full(kg)],
        out_specs=[pl.BlockSpec((1, 1, W), lambda b: (b, 0, 0)), pl.BlockSpec((1, 1, 6 * W), lambda b: (b, 0, 0))],
        out_shape=[jax.ShapeDtypeStruct((Bs, 1, W), F32), jax.ShapeDtypeStruct((Bs, 1, 6 * W), F32)],
        compiler_params=_cparams(1),
        name="sample_attn",
    )(z3, *cviews, qg, kg)


def _sample_dn_kernel(raw_ref, cs_ref, cw_ref, ba_ref, par_ref, s_ref, e_ref, etb_ref, etg_ref, o_ref, so_ref):
    E, ETB, ETG = e_ref[...], etb_ref[...], etg_ref[...]
    width = DN_HEADS * DN_DK
    conv = cw_ref[DN_CONV - 1:DN_CONV, :] * raw_ref[0]
    for t in range(DN_CONV - 1):
        conv = conv + cw_ref[t:t + 1, :] * cs_ref[0, t:t + 1, :]
    act = _silu(conv)

    def l2(zz):
        return zz * _row_hdot(lax.rsqrt(_row_hdot(zz * zz, E) + EPS), ETB)

    qn = l2(act[:, 0:width]) * (DN_DK ** -0.5)
    kn = l2(act[:, width:2 * width])
    vn = act[:, 2 * width:3 * width]
    ba = ba_ref[0]
    beta = _row_hdot(_sigmoid(ba), ETB)
    eg = jnp.exp(_row_hdot(par_ref[0:1, :] * _softplus(ba + par_ref[1:2, :]), ETG))
    row0 = lax.broadcasted_iota(jnp.int32, (8, LANES), 0) == 0
    for h in range(DN_HEADS):
        sl = slice(h * LANES, (h + 1) * LANES)
        S = s_ref[0, h]
        q, k, v, b, e = qn[:, sl], kn[:, sl], vn[:, sl], beta[:, sl], eg[:, sl]
        Sb = S.astype(BF16)
        wq = jnp.concatenate([k * b * e, q * e, jnp.zeros((6, LANES), F32)], axis=0)
        both = jnp.dot(wq.astype(BF16), Sb, preferred_element_type=F32)
        v_new = v * b - both[0:1]
        a = jnp.sum(q * k, axis=-1, keepdims=True)
        o_ref[0, :, sl] = both[1:2] + a * v_new
        k8 = jnp.where(row0, _rows8(k), 0.0)
        upd = lax.dot_general(k8, _rows8(v_new), (((0,), (0,)), ((), ())), preferred_element_type=F32, precision=HI)
        so_ref[0, h] = S * e + upd


def _sample_dn(raw3, conv_state, s0, layer, conv_w, ba3, par, e_mat, etb, etg):
    Bs, _, C = raw3.shape
    H = DN_HEADS
    full = lambda a: pl.BlockSpec(a.shape, lambda b: (0,) * a.ndim)
    return pl.pallas_call(
        _sample_dn_kernel,
        grid=(Bs,),
        in_specs=[pl.BlockSpec((1, 1, C), lambda b: (b, 0, 0)),
                  pl.BlockSpec((None, 1, DN_CONV - 1, C), lambda b: (layer, b, 0, 0)),
                  full(conv_w),
                  pl.BlockSpec((1, 1, LANES), lambda b: (b, 0, 0)),
                  full(par),
                  pl.BlockSpec((None, 1, H, DN_DK, LANES), lambda b: (layer, b, 0, 0, 0)),
                  full(e_mat), full(etb), full(etg)],
        out_specs=[pl.BlockSpec((1, 1, H * LANES), lambda b: (b, 0, 0)),
                   pl.BlockSpec((1, H, DN_DK, LANES), lambda b: (b, 0, 0, 0))],
        out_shape=[jax.ShapeDtypeStruct((Bs, 1, H * LANES), F32), jax.ShapeDtypeStruct(s0.shape[1:], F32)],
        compiler_params=_cparams(1),
        name="sample_dn",
    )(raw3, conv_state, conv_w, ba3, par, s0, e_mat, etb, etg)


def _sample_out_kernel(x_ref, oa_ref, od_ref, gates_ref, dng_ref, wa_ref, wb_ref, wo_ref, y_ref):
    y_ref[...] = _gated_mix(oa_ref[...], od_ref[...], gates_ref[...], dng_ref[...], wa_ref[...], wb_ref[...],
                            wo_ref[...], x_ref[...], _bdot)


def _sample_out(x2d, oa, od, gates, dng, wa, wb, wo):
    args = (x2d, oa, od, gates, dng, wa, wb, wo)
    return pl.pallas_call(
        _sample_out_kernel,
        grid=(1,),
        in_specs=[pl.BlockSpec(a.shape, lambda i: (0, 0)) for a in args],
        out_specs=pl.BlockSpec(x2d.shape, lambda i: (0, 0)),
        out_shape=jax.ShapeDtypeStruct(x2d.shape, F32),
        compiler_params=_cparams(1),
        name="sample_out",
    )(*args)


def _head_indicator(width, head):
    c = jnp.arange(width)[:, None] // head
    return (c == jnp.arange(LANES)[None, :]).astype(F32)


def _prep_layer(l, ln1_g, w_in, q_norm_g, k_norm_g, dn_conv_w, dn_a_log, dn_dt_bias, dn_norm_g, w_out_a, w_out_b,
                w_o, ln2_g, w_rg, b_rg, w_re, b_re, w_e_gate, w_e_up, w_e_down):
    D = w_in.shape[1]
    a_w = 3 * 3 * SWA_GW
    dn_w = DN_HEADS * 3 * DN_DK
    hv = DN_HEADS * DN_DK
    w = w_in[l]
    splits = dict(att=w[:, :a_w], dn=w[:, a_w:a_w + dn_w],
                  ba=jnp.pad(w[:, a_w + dn_w:a_w + dn_w + 2 * DN_HEADS], ((0, 0), (0, LANES - 2 * DN_HEADS))),
                  gate=w[:, a_w + dn_w + 2 * DN_HEADS:])
    assert splits["gate"].shape[1] == hv + 2 * D
    tile_heads = lambda g: jnp.broadcast_to(g[:, None, :], (len(SWA_CONFIGS), SWA_HEADS, SWA_DIM)).reshape(len(SWA_CONFIGS), SWA_GW)
    qg, kg = tile_heads(q_norm_g[l]), tile_heads(k_norm_g[l])
    idx = jnp.arange(MXU) // SWA_DIM
    n_g = len(SWA_CONFIGS)
    par = jnp.zeros((2, LANES), F32)
    par = par.at[0, DN_HEADS:2 * DN_HEADS].set(-jnp.exp(dn_a_log[l].astype(F32)))
    par = par.at[1, DN_HEADS:2 * DN_HEADS].set(dn_dt_bias[l].astype(F32))
    wr = jnp.pad(jnp.concatenate([w_rg[l], w_re[l]], axis=1), ((0, 0), (0, LANES - N_GROUPS - N_EXPERTS)))
    br = jnp.pad(jnp.concatenate([b_rg[l], b_re[l]]), (0, LANES - N_GROUPS - N_EXPERTS)).reshape(1, LANES)
    e8 = _head_indicator(hv, DN_DK)
    return dict(
        bf16={k: v.astype(BF16) for k, v in splits.items()},
        ln1=ln1_g[l].reshape(1, D), ln2=ln2_g[l].reshape(1, D),
        qg=qg, kg=kg,
        w_grp=[jnp.concatenate([w[:, s * n_g * SWA_GW + g * SWA_GW:s * n_g * SWA_GW + (g + 1) * SWA_GW]
                                for s in range(3)], axis=1).astype(BF16) for g in range(n_g)],
        ng_grp=[jnp.concatenate([qg[g] * SWA_DIM ** -0.5, kg[g]]).reshape(1, 1, 2 * SWA_GW) for g in range(n_g)],
        bd=((idx[:, None] == idx[None, :]).astype(F32) / SWA_DIM).astype(BF16),
        conv_w=dn_conv_w[l], par=par, dng=dn_norm_g[l].reshape(1, DN_DK),
        wa=w_out_a[l].astype(BF16), wb=w_out_b[l].astype(BF16), wo=w_o[l].astype(BF16), wr=wr.astype(BF16), br=br,
        wg=w_e_gate, wu=w_e_up, wd=w_e_down, layer=l,
        e_dn=e8, etb=e8.T, etg=jnp.roll(e8, DN_HEADS, axis=1).T,
        e_att=_head_indicator(SWA_GW, SWA_DIM).T.astype(BF16),
    )


def _layer_prompt(x, p):
    B, L, D = x.shape
    N = B * L
    x2d = x.reshape(N, D)
    bw = p["bf16"]
    pks, tails = [], []
    hgs = _norm_permute(x, p["ln1"], tuple(d for _, d in SWA_CONFIGS), tm=min(512, L))
    for g, (win, dil) in enumerate(SWA_CONFIGS):
        assert L >= win
        pk, tail = _proj_attn(hgs[g], p["w_grp"][g], p["ng_grp"][g], p["bd"], tmr=min(512, L // dil))
        pks.append(pk)
        tails.append(tail)
    tmp = min(1024, N)
    h0 = hgs[0].reshape(B, L, D)
    qd, kd, vd, raw_tail = _proj_dn(h0, bw["dn"], p["conv_w"], tm=min(512, L))
    gates = _matmul(h0.reshape(N, D), bw["gate"], tm=tmp, tn=1536, out_dtype=BF16, name="proj_gate")
    ba = _matmul(h0.reshape(N, D), bw["ba"], tm=tmp, tn=LANES, out_dtype=F32, name="proj_ba")
    os_, ls_ = [], []
    for pk in pks:
        d, M = pk.shape[1], pk.shape[2]
        o, lse = _attn(pk.reshape(B * d, M, pk.shape[3]), tq=min(256, M))
        os_.append(o.reshape(B, d, M, SWA_GW))
        ls_.append(lse.reshape(B, d, M, LANES))
    gb, gt_rows = _dn_gates(ba.reshape(B, L, LANES), p["par"], tl=min(256, L))
    u, w, qdec, kdec, a, gt = _dn_intra(qd, kd, vd, gb, gt_rows, tl=min(2048, L))
    od, s_new = _dn_scan(u, w, qdec, kdec, a, gt, jnp.zeros((B, DN_HEADS, DN_DK, LANES), F32), tl=min(512, L),
                         bb=2 if B % 2 == 0 else 1)
    x2 = _out_proj(x2d, os_, ls_, od.reshape(N, -1), gates, p["dng"], p["wa"], p["wb"], p["wo"], p["e_att"],
                   B=B, L=L, tm=min(512, L))
    y = _moe(x2, p["ln2"], p["wr"], p["br"], p["wg"], p["wu"], p["wd"], p["layer"], tm=256)
    return y.reshape(B, L, D), tails, raw_tail[:, 8 - (DN_CONV - 1):], s_new


def _layer_sample(x, caches, conv_state, s0, layer, p):
    Bs, T, D = x.shape
    assert T == 1
    x2d = x.reshape(Bs, D)
    bw = p["bf16"]
    proj = functools.partial(_proj_plain, x2d, p["ln1"], tm=Bs, out_dtype=F32)
    z_att = proj(bw["att"], tn=1536, name="sproj_att")
    raw = proj(bw["dn"], tn=1536, name="sproj_dn")
    gates = proj(bw["gate"], tn=1536, name="sproj_gate")
    ba = proj(bw["ba"], tn=LANES, name="sproj_ba")
    oa, kv = _sample_attn(z_att.reshape(Bs, 1, -1), caches, layer, p["qg"], p["kg"])
    raw3 = raw.reshape(Bs, 1, -1)
    od, s_new = _sample_dn(raw3, conv_state, s0, layer, p["conv_w"], ba.reshape(Bs, 1, LANES), p["par"],
                           p["e_dn"], p["etb"], p["etg"])
    x2 = _sample_out(x2d, oa.reshape(Bs, -1), od.reshape(Bs, -1), gates, p["dng"], p["wa"], p["wb"], p["wo"])
    y = _moe(x2, p["ln2"], p["wr"], p["br"], p["wg"], p["wu"], p["wd"], p["layer"], tm=Bs)
    W2 = 2 * SWA_GW
    kvs = [kv[:, :, g * W2:(g + 1) * W2].reshape(Bs, 1, 2, SWA_HEADS, SWA_DIM) for g in range(len(SWA_CONFIGS))]
    new_conv = jnp.concatenate([conv_state[layer][:, 1:], raw3], axis=1)
    return y.reshape(Bs, 1, D), kvs, new_conv, s_new


def kernel(x_prompt, x_sample, cache_swa0_kv, cache_swa1_kv, cache_swa2_kv, state_dn_conv, state_dn_S, ln1_g, w_in,
           q_norm_g, k_norm_g, dn_conv_w, dn_a_log, dn_dt_bias, dn_norm_g, w_out_a, w_out_b, w_o, ln2_g, w_rg, b_rg,
           w_re, b_re, w_e_gate, w_e_up, w_e_down):
    yp, ys = x_prompt, x_sample
    outs = [[] for _ in range(10)]
    for l in range(w_in.shape[0]):
        p = _prep_layer(l, ln1_g, w_in, q_norm_g, k_norm_g, dn_conv_w, dn_a_log, dn_dt_bias, dn_norm_g, w_out_a,
                        w_out_b, w_o, ln2_g, w_rg, b_rg, w_re, b_re, w_e_gate, w_e_up, w_e_down)
        yp, pkv, pconv, ps = _layer_prompt(yp, p)
        ys, skv, sconv, ss = _layer_sample(ys, (cache_swa0_kv, cache_swa1_kv, cache_swa2_kv), state_dn_conv,
                                           state_dn_S, l, p)
        for lst, val in zip(outs, (*pkv, pconv, ps, *skv, sconv, ss)):
            lst.append(val)
    return (yp, ys, *(jnp.stack(o) for o in outs))
```

```python
import functools

import jax
import jax.numpy as jnp
from jax import lax
from jax.experimental import pallas as pl
from jax.experimental.pallas import tpu as pltpu

F32 = jnp.float32
BF16 = jnp.bfloat16
HI = lax.Precision.HIGHEST
EPS = 1e-6

SWA_CONFIGS = ((128, 1), (512, 4), (2048, 16))
SWA_HEADS = 8
SWA_DIM = 64
SWA_GW = SWA_HEADS * SWA_DIM
SWA_SPAN = 128
DN_HEADS = 8
DN_DK = 128
DN_CONV = 4
DN_CHUNK = 64
N_GROUPS = 4
PER_GROUP = 8
N_EXPERTS = N_GROUPS * PER_GROUP
TOP_K = 2

VMEM_LIMIT_BYTES = 56 * 1024 * 1024
LANES = 128
MXU = 256
MOE_ROWS = 512
MOE_ROWS_SMALL = 128
ROW_UNROLL = 8
ROUTER_ROWS = 512


def _cparams(n_axes):
    return pltpu.CompilerParams(
        dimension_semantics=("arbitrary",) * n_axes, vmem_limit_bytes=VMEM_LIMIT_BYTES
    )


def _rms(x, g):
    return x * lax.rsqrt(jnp.mean(x * x, axis=-1, keepdims=True) + EPS) * g


def _bdot(a, b):
    return jnp.dot(a.astype(BF16), b.astype(BF16), preferred_element_type=F32)


def _hdot(a, b):
    return jnp.dot(a, b, preferred_element_type=F32, precision=HI)


def _sigmoid(x):
    return 0.5 * jnp.tanh(0.5 * x) + 0.5


def _silu(x):
    half = 0.5 * x
    return half * jnp.tanh(half) + half


def _pack_bf16_pairs(x):
    w = x.shape[1] // 2
    lo = lax.bitcast_convert_type(x[:, :w].astype(BF16).astype(F32), jnp.uint32) >> 16
    hi = lax.bitcast_convert_type(x[:, w:].astype(BF16).astype(F32), jnp.uint32) & jnp.uint32(0xFFFF0000)
    return lo | hi


def _unpack_bf16_pairs(p):
    lo = lax.bitcast_convert_type(p << 16, F32)
    hi = lax.bitcast_convert_type(p & jnp.uint32(0xFFFF0000), F32)
    return jnp.concatenate([lo, hi], axis=-1)


def _softplus(x):
    return jnp.maximum(x, 0.0) + jnp.log1p(jnp.exp(-jnp.abs(x)))


def _norm_permute_kernel(x_ref, lng_ref, *refs, tm, dils):
    outs, h_scr = refs[:-1], refs[-1]
    h = _rms(x_ref[0], lng_ref[...])
    n_cb = h_scr.shape[0]
    for cb in range(n_cb):
        h_scr[cb] = h[:, cb * LANES:(cb + 1) * LANES]
    for o_ref, d in zip(outs, dils):
        for cb in range(n_cb):
            for r in range(d):
                src = h_scr[cb] if d == 1 else h_scr[cb, pl.ds(r, tm // d, stride=d), :]
                o_ref[0, r, :, cb * LANES:(cb + 1) * LANES] = src.astype(BF16)


def _norm_permute(x, ln_g, dils, *, tm):
    B, L, D = x.shape
    assert L % tm == 0 and all(tm % (16 * d) == 0 for d in dils)
    return pl.pallas_call(
        functools.partial(_norm_permute_kernel, tm=tm, dils=dils),
        grid=(B, L // tm),
        in_specs=[pl.BlockSpec((1, tm, D), lambda b, i: (b, i, 0)), pl.BlockSpec((1, D), lambda b, i: (0, 0))],
        out_specs=[pl.BlockSpec((1, d, tm // d, D), lambda b, i: (b, 0, i, 0)) for d in dils],
        out_shape=[jax.ShapeDtypeStruct((B, d, L // d, D), BF16) for d in dils],
        scratch_shapes=[pltpu.VMEM((D // LANES, tm, LANES), F32)],
        compiler_params=_cparams(2),
        name="norm_permute",
    )(x, ln_g)


def _proj_attn_kernel(h_ref, w_ref, ng_ref, bd_ref, p_ref, t_ref, *, n_tiles):
    nr, rows = h_ref.shape[1], h_ref.shape[2]
    z = jnp.dot(h_ref[0].reshape(nr * rows, -1), w_ref[...], preferred_element_type=F32)
    kv = []
    for c in range(0, 3 * SWA_GW, MXU):
        zc = z[:, c:c + MXU]
        if c < 2 * SWA_GW:
            ms = jnp.dot((zc * zc).astype(BF16), bd_ref[...], preferred_element_type=F32)
            zc = zc * lax.rsqrt(ms + EPS) * ng_ref[0, :, c:c + MXU]
        p_ref[0, :, :, c:c + MXU] = zc.astype(BF16).reshape(nr, rows, MXU)
        if c >= SWA_GW:
            kv.append(zc)

    @pl.when(pl.program_id(2) == n_tiles - 1)
    def _():
        per = SWA_GW // MXU
        for rr in range(nr):
            last = slice((rr + 1) * rows - SWA_SPAN, (rr + 1) * rows)
            for s in range(2):
                zr = jnp.concatenate([part[last, :] for part in kv[s * per:(s + 1) * per]], axis=-1)
                t_ref[0, :, rr, s] = zr.reshape(SWA_SPAN, SWA_HEADS, SWA_DIM)


def _proj_attn(hg, w_g, ng_g, bd, *, tmr, nr):
    B, dil, M, D = hg.shape
    assert M % tmr == 0 and tmr >= SWA_SPAN and dil % nr == 0
    nt = M // tmr
    W3 = 3 * SWA_GW
    keep = SWA_SPAN * dil
    p, t = pl.pallas_call(
        functools.partial(_proj_attn_kernel, n_tiles=nt),
        grid=(B, dil // nr, nt),
        in_specs=[
            pl.BlockSpec((1, nr, tmr, D), lambda b, r, i: (b, r, i, 0)),
            pl.BlockSpec((D, W3), lambda b, r, i: (0, 0)),
            pl.BlockSpec((1, 1, 2 * SWA_GW), lambda b, r, i: (0, 0, 0)),
            pl.BlockSpec((MXU, MXU), lambda b, r, i: (0, 0)),
        ],
        out_specs=[
            pl.BlockSpec((1, nr, tmr, W3), lambda b, r, i: (b, r, i, 0)),
            pl.BlockSpec((1, SWA_SPAN, nr, 2, SWA_HEADS, SWA_DIM), lambda b, r, i: (b, 0, r, 0, 0, 0)),
        ],
        out_shape=[
            jax.ShapeDtypeStruct((B, dil, M, W3), BF16),
            jax.ShapeDtypeStruct((B, SWA_SPAN, dil, 2, SWA_HEADS, SWA_DIM), F32),
        ],
        compiler_params=_cparams(3),
        name="proj_attn",
    )(hg, w_g, ng_g, bd)
    return p, t.reshape(B, keep, 2, SWA_HEADS, SWA_DIM)


def _proj_plain_kernel(x_ref, lng_ref, w_ref, o_ref, h_scr):
    @pl.when(pl.program_id(1) == 0)
    def _():
        h_scr[...] = _rms(x_ref[...], lng_ref[...]).astype(BF16)

    o_ref[...] = jnp.dot(h_scr[...], w_ref[...], preferred_element_type=F32).astype(o_ref.dtype)


def _proj_plain(x2d, ln_g, w, *, tm, tn, out_dtype, name="proj_plain"):
    N, D = x2d.shape
    C = w.shape[1]
    assert N % tm == 0 and C % tn == 0
    return pl.pallas_call(
        _proj_plain_kernel,
        grid=(N // tm, C // tn),
        in_specs=[
            pl.BlockSpec((tm, D), lambda i, j: (i, 0)),
            pl.BlockSpec((1, D), lambda i, j: (0, 0)),
            pl.BlockSpec((D, tn), lambda i, j: (0, j)),
        ],
        out_specs=pl.BlockSpec((tm, tn), lambda i, j: (i, j)),
        out_shape=jax.ShapeDtypeStruct((N, C), out_dtype),
        scratch_shapes=[pltpu.VMEM((tm, D), BF16)],
        compiler_params=_cparams(2),
        name=name,
    )(x2d, ln_g, w)


def _matmul_kernel(h_ref, w_ref, o_ref):
    o_ref[...] = jnp.dot(h_ref[...], w_ref[...], preferred_element_type=F32).astype(o_ref.dtype)


def _matmul(h2d, w, *, tm, tn, out_dtype, name):
    N, D = h2d.shape
    C = w.shape[1]
    assert N % tm == 0 and C % tn == 0
    return pl.pallas_call(
        _matmul_kernel,
        grid=(N // tm, C // tn),
        in_specs=[pl.BlockSpec((tm, D), lambda i, j: (i, 0)), pl.BlockSpec((D, tn), lambda i, j: (0, j))],
        out_specs=pl.BlockSpec((tm, tn), lambda i, j: (i, j)),
        out_shape=jax.ShapeDtypeStruct((N, C), out_dtype),
        compiler_params=_cparams(2),
        name=name,
    )(h2d, w)


def _attn_kernel(q_ref, kc_ref, vc_ref, kp_ref, vp_ref, o_ref, lse_ref, kk_scr, vv_scr, *, tq):
    i = pl.program_id(1)
    blk = SWA_SPAN
    kk_scr[0:blk, :] = kp_ref[0]
    kk_scr[blk:blk + tq, :] = kc_ref[0]
    vv_scr[0:blk, :] = vp_ref[0]
    vv_scr[blk:blk + tq, :] = vc_ref[0]
    qi = lax.broadcasted_iota(jnp.int32, (blk, 2 * blk), 0)
    ki = lax.broadcasted_iota(jnp.int32, (blk, 2 * blk), 1)
    dist = blk + qi - ki
    band = (dist >= 0) & (dist <= SWA_SPAN)
    band_first = band & ((ki >= blk) | (i > 0))
    lo = lax.broadcasted_iota(jnp.int32, (blk, LANES), 1) < SWA_DIM
    zero = jnp.zeros((blk, LANES), BF16)
    lane = lax.broadcasted_iota(jnp.int32, (blk, LANES), 1)
    for jb in range(tq // blk):
        mask = band_first if jb == 0 else band
        rows = slice(jb * blk, (jb + 1) * blk)
        lse_all = jnp.zeros((blk, LANES), F32)
        for hp in range(SWA_GW // LANES):
            cs = slice(hp * LANES, (hp + 1) * LANES)
            qb = q_ref[0, rows, cs]
            kk = kk_scr[jb * blk:(jb + 2) * blk, cs]
            vv = vv_scr[jb * blk:(jb + 2) * blk, cs]
            res_o = []
            for hh in range(2):
                qm = jnp.where(lo if hh == 0 else jnp.logical_not(lo), qb, zero)
                s = lax.dot_general(qm, kk, (((1,), (1,)), ((), ())), preferred_element_type=F32)
                s = jnp.where(mask, s, -jnp.inf)
                m = jnp.max(s, axis=-1, keepdims=True)
                p = jnp.exp(s - m)
                den = jnp.sum(p, axis=-1, keepdims=True)
                pv = jnp.dot(p.astype(BF16), vv, preferred_element_type=F32)
                res_o.append(pv / den)
                lse_all = jnp.where(lane == 2 * hp + hh, m + jnp.log(den), lse_all)
            o_ref[0, rows, cs] = jnp.where(lo, res_o[0], res_o[1]).astype(BF16)
        lse_ref[0, rows, :] = lse_all


def _attn(p, *, tq):
    S, M, _ = p.shape
    assert M % tq == 0 and tq % SWA_SPAN == 0
    nb = tq // SWA_SPAN
    return pl.pallas_call(
        functools.partial(_attn_kernel, tq=tq),
        grid=(S, M // tq),
        in_specs=[
            pl.BlockSpec((1, tq, SWA_GW), lambda s, i: (s, i, 0)),
            pl.BlockSpec((1, tq, SWA_GW), lambda s, i: (s, i, 1)),
            pl.BlockSpec((1, tq, SWA_GW), lambda s, i: (s, i, 2)),
            pl.BlockSpec((1, SWA_SPAN, SWA_GW), lambda s, i: (s, jnp.maximum(i * nb - 1, 0), 1)),
            pl.BlockSpec((1, SWA_SPAN, SWA_GW), lambda s, i: (s, jnp.maximum(i * nb - 1, 0), 2)),
        ],
        out_specs=[
            pl.BlockSpec((1, tq, SWA_GW), lambda s, i: (s, i, 0)),
            pl.BlockSpec((1, tq, LANES), lambda s, i: (s, i, 0)),
        ],
        out_shape=[
            jax.ShapeDtypeStruct((S, M, SWA_GW), BF16),
            jax.ShapeDtypeStruct((S, M, LANES), F32),
        ],
        scratch_shapes=[
            pltpu.VMEM((SWA_SPAN + tq, SWA_GW), BF16),
            pltpu.VMEM((SWA_SPAN + tq, SWA_GW), BF16),
        ],
        compiler_params=_cparams(2),
        name="swa_attn",
    )(p, p, p, p, p)


def _proj_dn_kernel(h_ref, w_ref, cw_ref, q_ref, k_ref, v_ref, tail_ref, z_scr, carry_scr, *, tm, n_ct):
    i = pl.program_id(1)
    j = pl.program_id(2)
    nh = DN_HEADS
    ncb = z_scr.shape[1] // LANES

    z_scr[0:8, :] = jnp.where(i == 0, 0.0, carry_scr[j])
    z_scr[8:8 + tm, :] = jnp.dot(h_ref[0], w_ref[...], preferred_element_type=F32)
    last = z_scr[tm:tm + 8, :]
    carry_scr[j] = last
    tn = z_scr.shape[1]
    outs = (q_ref, k_ref, v_ref)
    for jj in range(n_ct):

        @pl.when(j == jj)
        def _(jj=jj):
            tail_ref[0, :, jj * tn:(jj + 1) * tn] = last
            for cbl in range(ncb):
                cs = slice(cbl * LANES, (cbl + 1) * LANES)
                part, h = divmod(jj * ncb + cbl, nh)
                xe = z_scr[:, cs]
                acc = (0.5 * cw_ref[0:1, cs]) * xe
                for t in range(1, DN_CONV):
                    acc = (0.5 * cw_ref[t:t + 1, cs]) * xe + pltpu.roll(acc, 1, axis=0)
                half = acc[8:]
                act = half * jnp.tanh(half) + half
                if part < 2:
                    inv = lax.rsqrt(jnp.sum(act * act, axis=-1, keepdims=True) + EPS)
                    act = act * (inv * (DN_DK ** -0.5) if part == 0 else inv)
                outs[part][0, :, h * LANES:(h + 1) * LANES] = act.astype(BF16)


def _proj_dn(h, w_dn, conv_w, *, tm):
    B, L, D = h.shape
    C = w_dn.shape[1]
    width = DN_HEADS * DN_DK
    n_ct = 2
    tn = C // n_ct
    assert L % tm == 0 and C == 3 * width and tn % LANES == 0
    qkv = pl.BlockSpec((1, tm, width), lambda b, i, j: (b, i, 0))
    return pl.pallas_call(
        functools.partial(_proj_dn_kernel, tm=tm, n_ct=n_ct),
        grid=(B, L // tm, n_ct),
        in_specs=[
            pl.BlockSpec((1, tm, D), lambda b, i, j: (b, i, 0)),
            pl.BlockSpec((D, tn), lambda b, i, j: (0, j)),
            pl.BlockSpec((DN_CONV, tn), lambda b, i, j: (0, j)),
        ],
        out_specs=[qkv, qkv, qkv, pl.BlockSpec((1, 8, C), lambda b, i, j: (b, 0, 0))],
        out_shape=[jax.ShapeDtypeStruct((B, L, width), BF16)] * 3 + [jax.ShapeDtypeStruct((B, 8, C), F32)],
        scratch_shapes=[pltpu.VMEM((8 + tm, tn), F32), pltpu.VMEM((n_ct, 8, tn), F32)],
        compiler_params=_cparams(3),
        name="proj_dn",
    )(h, w_dn, conv_w)


def _gates_kernel(h_ref, w_ref, par_ref, g_ref, gt_ref, *, tl):
    nh = DN_HEADS
    z = jnp.dot(h_ref[0], w_ref[...], preferred_element_type=F32)
    lane = lax.broadcasted_iota(jnp.int32, (tl, LANES), 1)
    ri = lax.broadcasted_iota(jnp.int32, (tl, tl), 0)
    ci = lax.broadcasted_iota(jnp.int32, (tl, tl), 1)
    tri = jnp.where((ri // DN_CHUNK == ci // DN_CHUNK) & (ci <= ri), 1.0, 0.0).astype(BF16)
    for sb in range(z.shape[0] // tl):
        rows = slice(sb * tl, (sb + 1) * tl)
        ba = z[rows]
        g = par_ref[0:1, :] * _softplus(ba + par_ref[1:2, :])
        g_hi = g.astype(BF16)
        r1 = g - g_hi.astype(F32)
        g_mid = r1.astype(BF16)
        g_lo = (r1 - g_mid.astype(F32)).astype(BF16)
        gc = sum(jnp.dot(tri, piece, preferred_element_type=F32) for piece in (g_hi, g_mid, g_lo))
        g_ref[0, rows, :] = jnp.where(lane < nh, _sigmoid(ba), gc)
        gt_ref[0, :, rows] = jnp.transpose(gc)[nh:2 * nh, :]


def _dn_gates(h, w_ba, par, *, tm, tl):
    B, L, D = h.shape
    assert L % tm == 0 and tm % tl == 0 and tl % DN_CHUNK == 0
    return pl.pallas_call(
        functools.partial(_gates_kernel, tl=tl),
        grid=(B, L // tm),
        in_specs=[pl.BlockSpec((1, tm, D), lambda b, i: (b, i, 0)), pl.BlockSpec((D, LANES), lambda b, i: (0, 0)),
                  pl.BlockSpec((2, LANES), lambda b, i: (0, 0))],
        out_specs=[pl.BlockSpec((1, tm, LANES), lambda b, i: (b, i, 0)),
                   pl.BlockSpec((1, DN_HEADS, tm), lambda b, i: (b, 0, i))],
        out_shape=[jax.ShapeDtypeStruct((B, L, LANES), F32), jax.ShapeDtypeStruct((B, DN_HEADS, L), F32)],
        compiler_params=_cparams(2),
        name="dn_gates",
    )(h, w_ba, par)


def _intra_kernel(q_ref, k_ref, v_ref, g_ref, gt_in_ref, u_ref, w_ref, qd_ref, kd_ref, a_ref, gt_ref, *, tl):
    h = pl.program_id(1)
    C = DN_CHUNK
    lane = lax.broadcasted_iota(jnp.int32, (C, LANES), 1)
    ri = lax.broadcasted_iota(jnp.int32, (C, C), 0)
    ci = lax.broadcasted_iota(jnp.int32, (C, C), 1)
    eye = jnp.where(ri == ci, 1.0, 0.0).astype(F32)
    nt_dot = lambda a, b: lax.dot_general(a.astype(BF16), b.astype(BF16), (((1,), (1,)), ((), ())),
                                          preferred_element_type=F32)
    rows = [slice(c * C, (c + 1) * C) for c in range(tl // C)]
    gv = [g_ref[0, r, :] for r in rows]
    q = [q_ref[0, r, :].astype(F32) for r in rows]
    k = [k_ref[0, r, :].astype(F32) for r in rows]
    v = [v_ref[0, r, :].astype(F32) for r in rows]
    beta = [jnp.sum(jnp.where(lane == h, x, 0.0), axis=-1, keepdims=True) for x in gv]
    gc = [jnp.sum(jnp.where(lane == h + DN_HEADS, x, 0.0), axis=-1, keepdims=True) for x in gv]
    gc_row = gt_in_ref[0, pl.ds(h, 1), :]
    decay = [jnp.exp(jnp.where(ri >= ci, a - gc_row[:, r], -jnp.inf)) for a, r in zip(gc, rows)]
    kb = [a * b for a, b in zip(k, beta)]
    kq = [nt_dot(jnp.concatenate([a, b], axis=0), c) for a, b, c in zip(kb, q, k)]
    x = [-jnp.where(ri > ci, m[:C] * d, 0.0) for m, d in zip(kq, decay)]
    t = [eye + a for a in x]
    x = [_bdot(a, a) for a in x]
    for _ in range(4):
        both = [_bdot(jnp.concatenate([a, b], axis=0), a) for a, b in zip(x, t)]
        t = [b + m[C:] for b, m in zip(t, both)]
        x = [m[:C] for m in both]
    t = [b + _bdot(b, a) for a, b in zip(x, t)]
    eg = [jnp.exp(a) for a in gc]
    glast = [a[C - 1:C, :] for a in gc]
    uw = [_bdot(a, jnp.concatenate([b * c, d * e], axis=1)) for a, b, c, d, e in zip(t, v, beta, kb, eg)]
    for c, r in enumerate(rows):
        u_ref[0, 0, r, :] = uw[c][:, :LANES]
        w_ref[0, 0, r, :] = uw[c][:, LANES:].astype(BF16)
        a_ref[0, 0, r, :] = (kq[c][C:] * decay[c]).astype(BF16)
        qd_ref[0, 0, r, :] = (q[c] * eg[c]).astype(BF16)
        kd_ref[0, 0, r, :] = (k[c] * jnp.exp(glast[c] - gc[c])).astype(BF16)
        gt_ref[0, 0, c:c + 1, :] = jnp.broadcast_to(jnp.exp(glast[c]), (1, LANES))


def _dn_intra(q, k, v, g, gt_rows, *, tl):
    B, L, _ = q.shape
    H, C = DN_HEADS, DN_CHUNK
    assert L % tl == 0 and (tl // C) % 8 == 0
    qkv_spec = pl.BlockSpec((1, tl, LANES), lambda b, h, i: (b, i, h))
    hl = lambda w: pl.BlockSpec((1, 1, tl, w), lambda b, h, i: (b, h, i, 0))
    return pl.pallas_call(
        functools.partial(_intra_kernel, tl=tl),
        grid=(B, H, L // tl),
        in_specs=[qkv_spec, qkv_spec, qkv_spec, pl.BlockSpec((1, tl, LANES), lambda b, h, i: (b, i, 0)),
                  pl.BlockSpec((1, H, tl), lambda b, h, i: (b, 0, i))],
        out_specs=[hl(LANES), hl(LANES), hl(LANES), hl(LANES), hl(C),
                   pl.BlockSpec((1, 1, tl // C, LANES), lambda b, h, i: (b, h, i, 0))],
        out_shape=[
            jax.ShapeDtypeStruct((B, H, L, LANES), F32),
            jax.ShapeDtypeStruct((B, H, L, LANES), BF16),
            jax.ShapeDtypeStruct((B, H, L, LANES), BF16),
            jax.ShapeDtypeStruct((B, H, L, LANES), BF16),
            jax.ShapeDtypeStruct((B, H, L, C), BF16),
            jax.ShapeDtypeStruct((B, H, L // C, LANES), F32),
        ],
        compiler_params=_cparams(3),
        name="dn_intra",
    )(q, k, v, g, gt_rows)


def _scan_kernel(u_ref, w_ref, qd_ref, kd_ref, a_ref, gt_ref, s0_ref, o_ref, s_ref, *, n_chunks):
    C = DN_CHUNK
    bb, H = s_ref.shape[0], s_ref.shape[1]
    seqs = [(b, h) for b in range(bb) for h in range(H)]

    @pl.when(pl.program_id(1) == 0)
    def _():
        s_ref[...] = s0_ref[...]

    def body(c, carry):
        rows = pl.ds(pl.multiple_of(c * C, C), C)
        S = [s_ref[b, h] for b, h in seqs]
        Sb = [x.astype(BF16) for x in S]
        v_new = [u_ref[b, h, rows, :] - jnp.dot(w_ref[b, h, rows, :], sb, preferred_element_type=F32)
                 for (b, h), sb in zip(seqs, Sb)]
        vb = [x.astype(BF16) for x in v_new]
        o = [jnp.dot(qd_ref[b, h, rows, :], sb, preferred_element_type=F32)
             + jnp.dot(a_ref[b, h, rows, :], v, preferred_element_type=F32) for (b, h), sb, v in zip(seqs, Sb, vb)]
        upd = [lax.dot_general(kd_ref[b, h, rows, :], v, (((0,), (0,)), ((), ())), preferred_element_type=F32)
               for (b, h), v in zip(seqs, vb)]
        for n, (b, h) in enumerate(seqs):
            o_ref[b, rows, h * LANES:(h + 1) * LANES] = o[n]
            s_ref[b, h] = S[n] * gt_ref[b, h, pl.ds(c, 1), :] + upd[n]
        return carry

    lax.fori_loop(0, n_chunks, body, 0)


def _dn_scan(u, w, qd, kd, a, gt, s0, *, tl, bb):
    B, H, L, _ = u.shape
    C = DN_CHUNK
    assert L % tl == 0 and (tl // C) % 8 == 0 and B % bb == 0
    hs = lambda wd: pl.BlockSpec((bb, H, tl, wd), lambda b, i: (b, 0, i, 0))
    s_spec = pl.BlockSpec((bb, H, DN_DK, LANES), lambda b, i: (b, 0, 0, 0))
    return pl.pallas_call(
        functools.partial(_scan_kernel, n_chunks=tl // C),
        grid=(B // bb, L // tl),
        in_specs=[hs(LANES), hs(LANES), hs(LANES), hs(LANES), hs(C),
                  pl.BlockSpec((bb, H, tl // C, LANES), lambda b, i: (b, 0, i, 0)), s_spec],
        out_specs=[pl.BlockSpec((bb, tl, H * LANES), lambda b, i: (b, i, 0)), s_spec],
        out_shape=[jax.ShapeDtypeStruct((B, L, H * LANES), F32),
                   jax.ShapeDtypeStruct((B, H, DN_DK, LANES), F32)],
        compiler_params=_cparams(2),
        name="dn_scan",
    )(u, w, qd, kd, a, gt, s0)


def _gated_mix(o_a, od, gates, dng, wa, wb, wo, x, dot):
    width = DN_HEADS * DN_DK
    parts = []
    for h in range(DN_HEADS):
        blk = od[:, h * LANES:(h + 1) * LANES]
        parts.append(blk * lax.rsqrt(jnp.mean(blk * blk, axis=-1, keepdims=True) + EPS) * dng)
    odn = jnp.concatenate(parts, axis=-1) * _silu(gates[:, 0:width].astype(F32))
    ya = dot(o_a, wa)
    yb = dot(odn, wb)
    mix = _sigmoid(gates[:, width:2 * width].astype(F32)) * ya + _sigmoid(gates[:, 2 * width:].astype(F32)) * yb
    return x + dot(mix, wo)


def _out_kernel(x_ref, o0, o1, o2, l0, l1, l2, od_ref, gates_ref, dng_ref, wa_ref, wb_ref, wo_ref, e_ref, y_ref,
                so0, so1, so2, sl0, sl1, sl2, *, tm, dils):
    o_refs, l_refs = (o0, o1, o2), (l0, l1, l2)
    so, sl = (so0, so1, so2), (sl0, sl1, sl2)
    for gi, d in enumerate(dils):
        for r in range(d):
            dst = slice(None) if d == 1 else pl.ds(r, tm // d, stride=d)
            sl[gi][dst, :] = l_refs[gi][0, r]
            for cb in range(SWA_GW // LANES):
                so[gi][cb, dst, :] = o_refs[gi][0, r, :, cb * LANES:(cb + 1) * LANES].astype(F32)
    ls = [s[...] for s in sl]
    m = jnp.maximum(jnp.maximum(ls[0], ls[1]), ls[2])
    es = [jnp.exp(l - m) for l in ls]
    tot = es[0] + es[1] + es[2]
    alphas = [jnp.dot((e / tot).astype(BF16), e_ref[...], preferred_element_type=F32) for e in es]
    parts = []
    for cb in range(SWA_GW // LANES):
        cs = slice(cb * LANES, (cb + 1) * LANES)
        parts.append(alphas[0][:, cs] * so[0][cb] + alphas[1][:, cs] * so[1][cb] + alphas[2][:, cs] * so[2][cb])
    o_a = jnp.concatenate(parts, axis=-1)
    y_ref[...] = _gated_mix(o_a, od_ref[...], gates_ref[...], dng_ref[...], wa_ref[...], wb_ref[...],
                            wo_ref[...], x_ref[...], _bdot)


def _out_proj(x2d, os_, ls_, od2d, gates, dng, wa, wb, wo, e_att, *, B, L, tm):
    N, D = x2d.shape
    nt = L // tm
    dils = tuple(d for _, d in SWA_CONFIGS)
    grp = lambda d, w: pl.BlockSpec((1, d, tm // d, w), lambda i: (i // nt, 0, i % nt, 0))
    row = lambda w: pl.BlockSpec((tm, w), lambda i: (i, 0))
    full = lambda a: pl.BlockSpec(a.shape, lambda i: (0, 0))
    return pl.pallas_call(
        functools.partial(_out_kernel, tm=tm, dils=dils),
        grid=(N // tm,),
        in_specs=[row(D)] + [grp(d, SWA_GW) for d in dils] + [grp(d, LANES) for d in dils]
        + [row(od2d.shape[1]), row(gates.shape[1]), full(dng), full(wa), full(wb), full(wo), full(e_att)],
        out_specs=row(D),
        out_shape=jax.ShapeDtypeStruct((N, D), F32),
        scratch_shapes=[pltpu.VMEM((SWA_GW // LANES, tm, LANES), F32)] * 3 + [pltpu.VMEM((tm, LANES), F32)] * 3,
        compiler_params=_cparams(1),
        name="out_proj",
    )(x2d, *os_, *ls_, od2d, gates, dng, wa, wb, wo, e_att)


def _router_kernel(x_ref, lng_ref, wr_ref, br_ref, info_ref, cnt_ref, base_scr, *, tm):
    i = pl.program_id(0)

    @pl.when(i == 0)
    def _():
        base_scr[...] = jnp.zeros_like(base_scr)

    h = _rms(x_ref[...], lng_ref[...])
    lg = _bdot(h, wr_ref[...]) + br_ref[...]
    lane = lax.broadcasted_iota(jnp.int32, (tm, LANES), 1)
    big = jnp.int32(1 << 20)
    ninf = -jnp.inf

    def argmax_lane(vals):
        mx = jnp.max(vals, axis=-1, keepdims=True)
        idx = jnp.min(jnp.where(vals == mx, lane, big), axis=-1, keepdims=True)
        return mx, idx

    lgm = jnp.where(lane < N_GROUPS, lg, ninf)
    mg, gsel = argmax_lane(lgm)
    pg = 1.0 / jnp.sum(jnp.exp(lgm - mg), axis=-1, keepdims=True)
    start = N_GROUPS + gsel * PER_GROUP
    le = jnp.where((lane >= start) & (lane < start + PER_GROUP), lg, ninf)
    m1, i1 = argmax_lane(le)
    m2, i2 = argmax_lane(jnp.where(lane == i1, ninf, le))
    e21 = jnp.exp(m2 - m1)
    w1 = pg / (1.0 + e21)
    w2 = pg * e21 / (1.0 + e21)
    oh = jnp.where(lane == i1, 1.0, 0.0) + jnp.where(lane == i2, 1.0, 0.0)
    ri = lax.broadcasted_iota(jnp.int32, (tm, tm), 0)
    ci = lax.broadcasted_iota(jnp.int32, (tm, tm), 1)
    strict = jnp.where(ci < ri, 1.0, 0.0).astype(BF16)
    pref = jnp.dot(strict, oh.astype(BF16), preferred_element_type=F32) + base_scr[...]
    r1 = jnp.sum(jnp.where(lane == i1, pref, 0.0), axis=-1, keepdims=True)
    r2 = jnp.sum(jnp.where(lane == i2, pref, 0.0), axis=-1, keepdims=True)
    base_scr[...] = base_scr[...] + jnp.sum(oh, axis=0, keepdims=True)
    cnt_ref[...] = base_scr[...]
    off = jnp.float32(N_GROUPS)
    info = jnp.where(lane == 0, i1.astype(F32) - off, 0.0)
    info = jnp.where(lane == 1, i2.astype(F32) - off, info)
    info = jnp.where(lane == 2, w1, info)
    info = jnp.where(lane == 3, w2, info)
    info = jnp.where(lane == 4, r1, info)
    info = jnp.where(lane == 5, r2, info)
    info_ref[...] = info


def _router(x2d, ln_g, wr, br, *, tm):
    N, D = x2d.shape
    assert N % tm == 0
    return pl.pallas_call(
        functools.partial(_router_kernel, tm=tm),
        grid=(N // tm,),
        in_specs=[
            pl.BlockSpec((tm, D), lambda i: (i, 0)),
            pl.BlockSpec((1, D), lambda i: (0, 0)),
            pl.BlockSpec((D, LANES), lambda i: (0, 0)),
            pl.BlockSpec((1, LANES), lambda i: (0, 0)),
        ],
        out_specs=[pl.BlockSpec((tm, LANES), lambda i: (i, 0)), pl.BlockSpec((1, LANES), lambda i: (0, 0))],
        out_shape=[jax.ShapeDtypeStruct((N, LANES), F32), jax.ShapeDtypeStruct((1, LANES), F32)],
        scratch_shapes=[pltpu.VMEM((1, LANES), F32)],
        compiler_params=_cparams(1),
        name="router",
    )(x2d, ln_g, wr, br)


def _dispatch_kernel(dest_ref, zb_ref, x_ref, lng_ref, xs_ref, zero_scr, rows_scr, sem, *, tm, tb, n_zb, n_tiles):
    i = pl.program_id(0)

    @pl.when(i == 0)
    def _():
        zero_scr[...] = jnp.zeros_like(zero_scr)

        def zero_copy(n):
            return pltpu.make_async_copy(zero_scr, xs_ref.at[pl.ds(zb_ref[n] * tb, tb)], sem.at[2])

        def zero_issue(n, carry):
            @pl.when(zb_ref[n] >= 0)
            def _():
                zero_copy(n).start()

            return carry

        def zero_wait(n, carry):
            @pl.when(zb_ref[n] >= 0)
            def _():
                zero_copy(n).wait()

            return carry

        lax.fori_loop(0, n_zb, zero_issue, 0)
        lax.fori_loop(0, n_zb, zero_wait, 0)

    buf_now = lax.rem(i, 2)
    rows_scr[buf_now] = _pack_bf16_pairs(_rms(x_ref[...], lng_ref[...])).reshape(rows_scr.shape[1:])

    def row_copy(tile, t, slot):
        buf = lax.rem(tile, 2)
        return pltpu.make_async_copy(
            rows_scr.at[buf, pl.ds(t, 1)],
            xs_ref.at[pl.ds(dest_ref[(tile * tm + t) * TOP_K + slot], 1)], sem.at[buf])

    def issue(tt, carry):
        for r in range(ROW_UNROLL):
            for slot in range(TOP_K):
                row_copy(i, tt * ROW_UNROLL + r, slot).start(priority=slot)
        return carry

    def drain(tile):
        buf = lax.rem(tile, 2)
        for _ in range(TOP_K):
            pltpu.make_async_copy(rows_scr.at[buf], rows_scr.at[buf], sem.at[buf]).wait()

    lax.fori_loop(0, tm // ROW_UNROLL, issue, 0)

    @pl.when(i > 0)
    def _():
        drain(i - 1)

    @pl.when(i == n_tiles - 1)
    def _():
        drain(i)


def _dispatch(dest, zero_blocks, x2d, ln_g, *, tm, tb, n_rows):
    N, D = x2d.shape
    row = (D // 2 // LANES, LANES)
    return pl.pallas_call(
        functools.partial(_dispatch_kernel, tm=tm, tb=tb, n_zb=zero_blocks.shape[0], n_tiles=N // tm),
        grid_spec=pltpu.PrefetchScalarGridSpec(
            num_scalar_prefetch=2,
            grid=(N // tm,),
            in_specs=[pl.BlockSpec((tm, D), lambda i, d, z: (i, 0)), pl.BlockSpec((1, D), lambda i, d, z: (0, 0))],
            out_specs=pl.BlockSpec(memory_space=pl.ANY),
            scratch_shapes=[pltpu.VMEM((tb,) + row, jnp.uint32), pltpu.VMEM((2, tm) + row, jnp.uint32),
                            pltpu.SemaphoreType.DMA((3,))],
        ),
        out_shape=jax.ShapeDtypeStruct((n_rows,) + row, jnp.uint32),
        compiler_params=_cparams(1),
        name="moe_dispatch",
    )(dest, zero_blocks, x2d, ln_g)


def _ffn_kernel(be_ref, nb_ref, xs_ref, wg_ref, wu_ref, wd_ref, y_ref, wg_scr, wu_scr, wd_scr):
    i = pl.program_id(0)
    used = i < nb_ref[0]

    @pl.when(jnp.logical_or(i == 0, be_ref[i] != be_ref[jnp.maximum(i - 1, 0)]))
    def _():
        wg_scr[...] = wg_ref[0].astype(BF16)
        wu_scr[...] = wu_ref[0].astype(BF16)
        wd_scr[...] = wd_ref[0].astype(BF16)

    @pl.when(used)
    def _():
        h = _unpack_bf16_pairs(xs_ref[...].reshape(xs_ref.shape[0], -1)).astype(BF16)
        g = jnp.dot(h, wg_scr[...], preferred_element_type=F32)
        u = jnp.dot(h, wu_scr[...], preferred_element_type=F32)
        y = jnp.dot((_silu(g) * u).astype(BF16), wd_scr[...], preferred_element_type=F32)
        y_ref[...] = _pack_bf16_pairs(y).reshape(y_ref.shape)

    @pl.when(jnp.logical_not(used))
    def _():
        y_ref[...] = jnp.zeros_like(y_ref)


def _ffn(blk_e, nb_used, xs, wg, wu, wd, layer, *, tb):
    P, S, _ = xs.shape
    D = 2 * S * LANES
    nb = P // tb
    DE = wg.shape[3]
    return pl.pallas_call(
        _ffn_kernel,
        grid_spec=pltpu.PrefetchScalarGridSpec(
            num_scalar_prefetch=2,
            grid=(nb,),
            in_specs=[
                pl.BlockSpec((tb, S, LANES), lambda i, be, nbu: (jnp.minimum(i, nbu[0] - 1), 0, 0)),
                pl.BlockSpec((None, 1, D, DE), lambda i, be, nbu: (layer, be[i], 0, 0)),
                pl.BlockSpec((None, 1, D, DE), lambda i, be, nbu: (layer, be[i], 0, 0)),
                pl.BlockSpec((None, 1, DE, D), lambda i, be, nbu: (layer, be[i], 0, 0)),
            ],
            out_specs=pl.BlockSpec((tb, S, LANES), lambda i, be, nbu: (i, 0, 0)),
            scratch_shapes=[pltpu.VMEM((D, DE), BF16), pltpu.VMEM((D, DE), BF16), pltpu.VMEM((DE, D), BF16)],
        ),
        out_shape=jax.ShapeDtypeStruct((P, S, LANES), jnp.uint32),
        compiler_params=_cparams(1),
        name="moe_ffn",
    )(blk_e, nb_used, xs, wg, wu, wd)


def _combine_kernel(dest_ref, x_ref, info_ref, yb_ref, y_ref, g_scr, sem, *, tm, n_tiles):
    i = pl.program_id(0)

    def row_copy(tile, t, slot):
        buf = lax.rem(tile, 2)
        return pltpu.make_async_copy(
            yb_ref.at[pl.ds(dest_ref[(tile * tm + t) * TOP_K + slot], 1)],
            g_scr.at[buf, slot, pl.ds(t, 1)], sem.at[buf])

    def issue_tile(tile):
        def body(tt, carry):
            for r in range(ROW_UNROLL):
                for slot in range(TOP_K):
                    row_copy(tile, tt * ROW_UNROLL + r, slot).start(priority=slot)
            return carry

        lax.fori_loop(0, tm // ROW_UNROLL, body, 0)

    @pl.when(i == 0)
    def _():
        issue_tile(i)

    @pl.when(i + 1 < n_tiles)
    def _():
        issue_tile(i + 1)

    buf = lax.rem(i, 2)
    pltpu.make_async_copy(g_scr.at[buf], g_scr.at[buf], sem.at[buf]).wait()
    info = info_ref[...]
    lane = lax.broadcasted_iota(jnp.int32, info.shape, 1)
    w1 = jnp.sum(jnp.where(lane == 2, info, 0.0), axis=-1, keepdims=True)
    w2 = jnp.sum(jnp.where(lane == 3, info, 0.0), axis=-1, keepdims=True)
    g1 = _unpack_bf16_pairs(g_scr[buf, 0].reshape(tm, -1))
    g2 = _unpack_bf16_pairs(g_scr[buf, 1].reshape(tm, -1))
    y_ref[...] = x_ref[...] + (w1 * g1 + w2 * g2)


def _combine(dest, x2d, info, yb, *, tm):
    N, D = x2d.shape
    return pl.pallas_call(
        functools.partial(_combine_kernel, tm=tm, n_tiles=N // tm),
        grid_spec=pltpu.PrefetchScalarGridSpec(
            num_scalar_prefetch=1,
            grid=(N // tm,),
            in_specs=[
                pl.BlockSpec((tm, D), lambda i, d: (i, 0)),
                pl.BlockSpec((tm, LANES), lambda i, d: (i, 0)),
                pl.BlockSpec(memory_space=pl.ANY),
            ],
            out_specs=pl.BlockSpec((tm, D), lambda i, d: (i, 0)),
            scratch_shapes=[pltpu.VMEM((2, TOP_K, tm, D // 2 // LANES, LANES), jnp.uint32),
                            pltpu.SemaphoreType.DMA((2,))],
        ),
        out_shape=jax.ShapeDtypeStruct((N, D), F32),
        compiler_params=_cparams(1),
        name="moe_combine",
    )(dest, x2d, info, yb)


def _moe(x2d, ln2_g, wr, br, wg, wu, wd, layer, *, tm):
    N, D = x2d.shape
    tb = MOE_ROWS if N * TOP_K >= N_EXPERTS * MOE_ROWS else MOE_ROWS_SMALL
    info, counts = _router(x2d, ln2_g, wr, br, tm=ROUTER_ROWS if N % ROUTER_ROWS == 0 else tm)
    counts = counts[0, N_GROUPS:N_GROUPS + N_EXPERTS].astype(jnp.int32)
    pcounts = (counts + tb - 1) // tb * tb
    pend = jnp.cumsum(pcounts)
    pstart = pend - pcounts
    e = info[:, 0:TOP_K].astype(jnp.int32)
    rank = info[:, 4:4 + TOP_K].astype(jnp.int32)
    experts = jnp.arange(N_EXPERTS, dtype=jnp.int32)
    dest = (jnp.sum(jnp.where(e[..., None] == experts, pstart, 0), axis=-1) + rank).reshape(-1)
    nb = -(-(N * TOP_K) // tb) + N_EXPERTS
    P = nb * tb
    blocks = jnp.arange(nb, dtype=jnp.int32)
    blk_e = jnp.minimum(jnp.sum((pend[None, :] <= blocks[:, None] * tb).astype(jnp.int32), axis=1), N_EXPERTS - 1)
    nb_used = (pend[-1] // tb).astype(jnp.int32).reshape(1)
    zero_blocks = jnp.concatenate([jnp.where(counts % tb != 0, pend // tb - 1, -1),
                                   jnp.where(blocks >= nb_used[0], blocks, -1)]).astype(jnp.int32)
    xs = _dispatch(dest, zero_blocks, x2d, ln2_g, tm=tm, tb=tb, n_rows=P)
    yb = _ffn(blk_e, nb_used, xs, wg, wu, wd, layer, tb=tb)
    return _combine(dest, x2d, info, yb, tm=tm)


def _rows8(x):
    return jnp.broadcast_to(x, (8, x.shape[1]))


def _row_hdot(x, m):
    return _hdot(_rows8(x), m)[0:1]


def _bf_round(x):
    return x.astype(BF16).astype(F32)


def _sample_attn_kernel(z_ref, c0, c1, c2, qg_ref, kg_ref, oa_ref, kv_ref):
    W = SWA_GW
    scale = SWA_DIM ** -0.5
    z = z_ref[0]
    sub = lax.broadcasted_iota(jnp.int32, (SWA_HEADS, W), 0)
    lane = lax.broadcasted_iota(jnp.int32, (SWA_HEADS, W), 1)
    own = lane // SWA_DIM == sub

    def heads(row):
        return jnp.where(own, jnp.broadcast_to(row, (SWA_HEADS, W)), 0.0)

    def head_sum(row):
        return jnp.sum(heads(row), axis=-1, keepdims=True)

    def spread(col):
        return jnp.sum(jnp.where(own, col, 0.0), axis=0, keepdims=True)

    def headnorm(zz, g):
        return zz * spread(lax.rsqrt(head_sum(zz * zz) * (1.0 / SWA_DIM) + EPS)) * g

    outs, lses = [], []
    for gi, (c_ref, (win, dil)) in enumerate(zip((c0, c1, c2), SWA_CONFIGS)):
        q = headnorm(z[:, gi * W:(gi + 1) * W], qg_ref[gi:gi + 1, :])
        k = headnorm(z[:, 3 * W + gi * W:3 * W + (gi + 1) * W], kg_ref[gi:gi + 1, :])
        v = z[:, 6 * W + gi * W:6 * W + (gi + 1) * W]
        kv_ref[0, :, 2 * gi * W:(2 * gi + 1) * W] = k
        kv_ref[0, :, (2 * gi + 1) * W:(2 * gi + 2) * W] = v
        kc = c_ref[0].reshape(W, win).astype(BF16)
        vc = c_ref[1].reshape(W, win).astype(BF16)
        s_c = jnp.dot(heads(q).astype(BF16), kc, preferred_element_type=F32) * scale
        row = lax.broadcasted_iota(jnp.int32, s_c.shape, 1)
        s_c = jnp.where(row % dil == 0, s_c, -jnp.inf)
        s_n = head_sum(_bf_round(k) * _bf_round(q)) * scale
        m = jnp.maximum(jnp.max(s_c, axis=-1, keepdims=True), s_n)
        p_c = jnp.exp(s_c - m)
        p_n = jnp.exp(s_n - m)
        den = jnp.sum(p_c, axis=-1, keepdims=True) + p_n
        pv = lax.dot_general(p_c.astype(BF16), vc, (((1,), (1,)), ((), ())), preferred_element_type=F32)
        num = jnp.sum(jnp.where(own, pv, 0.0), axis=0, keepdims=True) + spread(_bf_round(p_n)) * _bf_round(v)
        outs.append(num / spread(den))
        lses.append(m + jnp.log(den))
    mm = jnp.maximum(jnp.maximum(lses[0], lses[1]), lses[2])
    es = [jnp.exp(l - mm) for l in lses]
    tot = es[0] + es[1] + es[2]
    oa_ref[0] = sum(spread(_bf_round(e / tot)) * _bf_round(o) for e, o in zip(es, outs))


def _sample_attn(z3, caches, layer, qg, kg):
    Bs = z3.shape[0]
    W = SWA_GW
    cviews, cspecs = [], []
    for (win, dil), c in zip(SWA_CONFIGS, caches):
        assert c.shape[2] == win
        cviews.append(jnp.transpose(c, (0, 1, 3, 4, 5, 2)))
        cspecs.append(pl.BlockSpec((None, None, 2, SWA_HEADS, SWA_DIM, win), lambda b: (layer, b, 0, 0, 0, 0)))
    full = lambda a: pl.BlockSpec(a.shape, lambda b: (0,) * a.ndim)
    return pl.pallas_call(
        _sample_attn_kernel,
        grid=(Bs,),
        in_specs=[pl.BlockSpec((1, 1, 9 * W), lambda b: (b, 0, 0))] + cspecs + [full(qg), full(kg)],
        out_specs=[pl.BlockSpec((1, 1, W), lambda b: (b, 0, 0)), pl.BlockSpec((1, 1, 6 * W), lambda b: (b, 0, 0))],
        out_shape=[jax.ShapeDtypeStruct((Bs, 1, W), F32), jax.ShapeDtypeStruct((Bs, 1, 6 * W), F32)],
        compiler_params=_cparams(1),
        name="sample_attn",
    )(z3, *cviews, qg, kg)


def _sample_dn_kernel(raw_ref, cs_ref, cw_ref, ba_ref, par_ref, s_ref, e_ref, etb_ref, etg_ref, o_ref, so_ref):
    E, ETB, ETG = e_ref[...], etb_ref[...], etg_ref[...]
    width = DN_HEADS * DN_DK
    conv = cw_ref[DN_CONV - 1:DN_CONV, :] * raw_ref[0]
    for t in range(DN_CONV - 1):
        conv = conv + cw_ref[t:t + 1, :] * cs_ref[0, t:t + 1, :]
    act = _silu(conv)

    def l2(zz):
        return zz * _row_hdot(lax.rsqrt(_row_hdot(zz * zz, E) + EPS), ETB)

    qn = l2(act[:, 0:width]) * (DN_DK ** -0.5)
    kn = l2(act[:, width:2 * width])
    vn = act[:, 2 * width:3 * width]
    ba = ba_ref[0]
    beta = _row_hdot(_sigmoid(ba), ETB)
    eg = jnp.exp(_row_hdot(par_ref[0:1, :] * _softplus(ba + par_ref[1:2, :]), ETG))
    row0 = lax.broadcasted_iota(jnp.int32, (8, LANES), 0) == 0
    for h in range(DN_HEADS):
        sl = slice(h * LANES, (h + 1) * LANES)
        S = s_ref[0, h]
        q, k, v, b, e = qn[:, sl], kn[:, sl], vn[:, sl], beta[:, sl], eg[:, sl]
        Sb = S.astype(BF16)
        wq = jnp.concatenate([k * b * e, q * e, jnp.zeros((6, LANES), F32)], axis=0)
        both = jnp.dot(wq.astype(BF16), Sb, preferred_element_type=F32)
        v_new = v * b - both[0:1]
        a = jnp.sum(q * k, axis=-1, keepdims=True)
        o_ref[0, :, sl] = both[1:2] + a * v_new
        k8 = jnp.where(row0, _rows8(k), 0.0)
        upd = lax.dot_general(k8, _rows8(v_new), (((0,), (0,)), ((), ())), preferred_element_type=F32, precision=HI)
        so_ref[0, h] = S * e + upd


def _sample_dn(raw3, conv_state, s0, layer, conv_w, ba3, par, e_mat, etb, etg):
    Bs, _, C = raw3.shape
    H = DN_HEADS
    full = lambda a: pl.BlockSpec(a.shape, lambda b: (0,) * a.ndim)
    return pl.pallas_call(
        _sample_dn_kernel,
        grid=(Bs,),
        in_specs=[pl.BlockSpec((1, 1, C), lambda b: (b, 0, 0)),
                  pl.BlockSpec((None, 1, DN_CONV - 1, C), lambda b: (layer, b, 0, 0)),
                  full(conv_w),
                  pl.BlockSpec((1, 1, LANES), lambda b: (b, 0, 0)),
                  full(par),
                  pl.BlockSpec((None, 1, H, DN_DK, LANES), lambda b: (layer, b, 0, 0, 0)),
                  full(e_mat), full(etb), full(etg)],
        out_specs=[pl.BlockSpec((1, 1, H * LANES), lambda b: (b, 0, 0)),
                   pl.BlockSpec((1, H, DN_DK, LANES), lambda b: (b, 0, 0, 0))],
        out_shape=[jax.ShapeDtypeStruct((Bs, 1, H * LANES), F32), jax.ShapeDtypeStruct(s0.shape[1:], F32)],
        compiler_params=_cparams(1),
        name="sample_dn",
    )(raw3, conv_state, conv_w, ba3, par, s0, e_mat, etb, etg)


def _sample_out_kernel(x_ref, oa_ref, od_ref, gates_ref, dng_ref, wa_ref, wb_ref, wo_ref, y_ref):
    y_ref[...] = _gated_mix(oa_ref[...], od_ref[...], gates_ref[...], dng_ref[...], wa_ref[...], wb_ref[...],
                            wo_ref[...], x_ref[...], _bdot)


def _sample_out(x2d, oa, od, gates, dng, wa, wb, wo):
    args = (x2d, oa, od, gates, dng, wa, wb, wo)
    return pl.pallas_call(
        _sample_out_kernel,
        grid=(1,),
        in_specs=[pl.BlockSpec(a.shape, lambda i: (0, 0)) for a in args],
        out_specs=pl.BlockSpec(x2d.shape, lambda i: (0, 0)),
        out_shape=jax.ShapeDtypeStruct(x2d.shape, F32),
        compiler_params=_cparams(1),
        name="sample_out",
    )(*args)


def _head_indicator(width, head):
    c = jnp.arange(width)[:, None] // head
    return (c == jnp.arange(LANES)[None, :]).astype(F32)


def _prep_layer(l, ln1_g, w_in, q_norm_g, k_norm_g, dn_conv_w, dn_a_log, dn_dt_bias, dn_norm_g, w_out_a, w_out_b,
                w_o, ln2_g, w_rg, b_rg, w_re, b_re, w_e_gate, w_e_up, w_e_down):
    D = w_in.shape[1]
    a_w = 3 * 3 * SWA_GW
    dn_w = DN_HEADS * 3 * DN_DK
    hv = DN_HEADS * DN_DK
    w = w_in[l]
    splits = dict(att=w[:, :a_w], dn=w[:, a_w:a_w + dn_w],
                  ba=jnp.pad(w[:, a_w + dn_w:a_w + dn_w + 2 * DN_HEADS], ((0, 0), (0, LANES - 2 * DN_HEADS))),
                  gate=w[:, a_w + dn_w + 2 * DN_HEADS:])
    assert splits["gate"].shape[1] == hv + 2 * D
    tile_heads = lambda g: jnp.broadcast_to(g[:, None, :], (len(SWA_CONFIGS), SWA_HEADS, SWA_DIM)).reshape(len(SWA_CONFIGS), SWA_GW)
    qg, kg = tile_heads(q_norm_g[l]), tile_heads(k_norm_g[l])
    idx = jnp.arange(MXU) // SWA_DIM
    n_g = len(SWA_CONFIGS)
    par = jnp.zeros((2, LANES), F32)
    par = par.at[0, DN_HEADS:2 * DN_HEADS].set(-jnp.exp(dn_a_log[l].astype(F32)))
    par = par.at[1, DN_HEADS:2 * DN_HEADS].set(dn_dt_bias[l].astype(F32))
    wr = jnp.pad(jnp.concatenate([w_rg[l], w_re[l]], axis=1), ((0, 0), (0, LANES - N_GROUPS - N_EXPERTS)))
    br = jnp.pad(jnp.concatenate([b_rg[l], b_re[l]]), (0, LANES - N_GROUPS - N_EXPERTS)).reshape(1, LANES)
    e8 = _head_indicator(hv, DN_DK)
    return dict(
        bf16={k: v.astype(BF16) for k, v in splits.items()},
        ln1=ln1_g[l].reshape(1, D), ln2=ln2_g[l].reshape(1, D),
        qg=qg, kg=kg,
        w_grp=[jnp.concatenate([w[:, s * n_g * SWA_GW + g * SWA_GW:s * n_g * SWA_GW + (g + 1) * SWA_GW]
                                for s in range(3)], axis=1).astype(BF16) for g in range(n_g)],
        ng_grp=[jnp.concatenate([qg[g] * SWA_DIM ** -0.5, kg[g]]).reshape(1, 1, 2 * SWA_GW) for g in range(n_g)],
        bd=((idx[:, None] == idx[None, :]).astype(F32) / SWA_DIM).astype(BF16),
        conv_w=dn_conv_w[l], par=par, dng=dn_norm_g[l].reshape(1, DN_DK),
        wa=w_out_a[l].astype(BF16), wb=w_out_b[l].astype(BF16), wo=w_o[l].astype(BF16), wr=wr.astype(BF16), br=br,
        wg=w_e_gate, wu=w_e_up, wd=w_e_down, layer=l,
        e_dn=e8, etb=e8.T, etg=jnp.roll(e8, DN_HEADS, axis=1).T,
        e_att=_head_indicator(SWA_GW, SWA_DIM).T.astype(BF16),
    )


def _layer_prompt(x, p):
    B, L, D = x.shape
    N = B * L
    x2d = x.reshape(N, D)
    bw = p["bf16"]
    pks, tails = [], []
    hgs = _norm_permute(x, p["ln1"], tuple(d for _, d in SWA_CONFIGS), tm=min(512, L))
    for g, (win, dil) in enumerate(SWA_CONFIGS):
        assert L >= win
        tmr = min(512, L // dil)
        pk, tail = _proj_attn(hgs[g], p["w_grp"][g], p["ng_grp"][g], p["bd"], tmr=tmr, nr=min(dil, max(1, 512 // tmr)))
        pks.append(pk)
        tails.append(tail)
    tmp = min(1024, N)
    h0 = hgs[0].reshape(B, L, D)
    qd, kd, vd, raw_tail = _proj_dn(h0, bw["dn"], p["conv_w"], tm=min(512, L))
    gates = _matmul(h0.reshape(N, D), bw["gate"], tm=tmp, tn=1536, out_dtype=BF16, name="proj_gate")
    os_, ls_ = [], []
    for pk in pks:
        d, M = pk.shape[1], pk.shape[2]
        o, lse = _attn(pk.reshape(B * d, M, pk.shape[3]), tq=min(256, M))
        os_.append(o.reshape(B, d, M, SWA_GW))
        ls_.append(lse.reshape(B, d, M, LANES))
    gb, gt_rows = _dn_gates(h0, bw["ba"], p["par"], tm=min(1024, L), tl=min(256, L))
    u, w, qdec, kdec, a, gt = _dn_intra(qd, kd, vd, gb, gt_rows, tl=min(2048, L))
    od, s_new = _dn_scan(u, w, qdec, kdec, a, gt, jnp.zeros((B, DN_HEADS, DN_DK, LANES), F32), tl=min(512, L),
                         bb=2 if B % 2 == 0 else 1)
    x2 = _out_proj(x2d, os_, ls_, od.reshape(N, -1), gates, p["dng"], p["wa"], p["wb"], p["wo"], p["e_att"],
                   B=B, L=L, tm=min(512, L))
    y = _moe(x2, p["ln2"], p["wr"], p["br"], p["wg"], p["wu"], p["wd"], p["layer"], tm=256)
    return y.reshape(B, L, D), tails, raw_tail[:, 8 - (DN_CONV - 1):], s_new


def _layer_sample(x, caches, conv_state, s0, layer, p):
    Bs, T, D = x.shape
    assert T == 1
    x2d = x.reshape(Bs, D)
    bw = p["bf16"]
    proj = functools.partial(_proj_plain, x2d, p["ln1"], tm=Bs, out_dtype=F32)
    z_att = proj(bw["att"], tn=1536, name="sproj_att")
    raw = proj(bw["dn"], tn=1536, name="sproj_dn")
    gates = proj(bw["gate"], tn=1536, name="sproj_gate")
    ba = proj(bw["ba"], tn=LANES, name="sproj_ba")
    oa, kv = _sample_attn(z_att.reshape(Bs, 1, -1), caches, layer, p["qg"], p["kg"])
    raw3 = raw.reshape(Bs, 1, -1)
    od, s_new = _sample_dn(raw3, conv_state, s0, layer, p["conv_w"], ba.reshape(Bs, 1, LANES), p["par"],
                           p["e_dn"], p["etb"], p["etg"])
    x2 = _sample_out(x2d, oa.reshape(Bs, -1), od.reshape(Bs, -1), gates, p["dng"], p["wa"], p["wb"], p["wo"])
    y = _moe(x2, p["ln2"], p["wr"], p["br"], p["wg"], p["wu"], p["wd"], p["layer"], tm=Bs)
    W2 = 2 * SWA_GW
    kvs = [kv[:, :, g * W2:(g + 1) * W2].reshape(Bs, 1, 2, SWA_HEADS, SWA_DIM) for g in range(len(SWA_CONFIGS))]
    new_conv = jnp.concatenate([conv_state[layer][:, 1:], raw3], axis=1)
    return y.reshape(Bs, 1, D), kvs, new_conv, s_new


def kernel(x_prompt, x_sample, cache_swa0_kv, cache_swa1_kv, cache_swa2_kv, state_dn_conv, state_dn_S, ln1_g, w_in,
           q_norm_g, k_norm_g, dn_conv_w, dn_a_log, dn_dt_bias, dn_norm_g, w_out_a, w_out_b, w_o, ln2_g, w_rg, b_rg,
           w_re, b_re, w_e_gate, w_e_up, w_e_down):
    yp, ys = x_prompt, x_sample
    outs = [[] for _ in range(10)]
    for l in range(w_in.shape[0]):
        p = _prep_layer(l, ln1_g, w_in, q_norm_g, k_norm_g, dn_conv_w, dn_a_log, dn_dt_bias, dn_norm_g, w_out_a,
                        w_out_b, w_o, ln2_g, w_rg, b_rg, w_re, b_re, w_e_gate, w_e_up, w_e_down)
        yp, pkv, pconv, ps = _layer_prompt(yp, p)
        ys, skv, sconv, ss = _layer_sample(ys, (cache_swa0_kv, cache_swa1_kv, cache_swa2_kv), state_dn_conv,
                                           state_dn_S, l, p)
        for lst, val in zip(outs, (*pkv, pconv, ps, *skv, sconv, ss)):
            lst.append(val)
    return (yp, ys, *(jnp.stack(o) for o in outs))
```

```python
import functools

import jax
import jax.numpy as jnp
from jax import lax
from jax.experimental import pallas as pl
from jax.experimental.pallas import tpu as pltpu

F32 = jnp.float32
BF16 = jnp.bfloat16
HI = lax.Precision.HIGHEST
EPS = 1e-6

SWA_CONFIGS = ((128, 1), (512, 4), (2048, 16))
SWA_HEADS = 8
SWA_DIM = 64
SWA_GW = SWA_HEADS * SWA_DIM
SWA_SPAN = 128
DN_HEADS = 8
DN_DK = 128
DN_CONV = 4
DN_CHUNK = 64
N_GROUPS = 4
PER_GROUP = 8
N_EXPERTS = N_GROUPS * PER_GROUP
TOP_K = 2

VMEM_LIMIT_BYTES = 56 * 1024 * 1024
LANES = 128
MXU = 256
MOE_ROWS = 512
MOE_ROWS_SMALL = 128
ROW_UNROLL = 8
ROUTER_ROWS = 512


def _cparams(n_axes):
    return pltpu.CompilerParams(
        dimension_semantics=("arbitrary",) * n_axes, vmem_limit_bytes=VMEM_LIMIT_BYTES
    )


def _rms(x, g):
    return x * lax.rsqrt(jnp.mean(x * x, axis=-1, keepdims=True) + EPS) * g


def _bdot(a, b):
    return jnp.dot(a.astype(BF16), b.astype(BF16), preferred_element_type=F32)


def _hdot(a, b):
    return jnp.dot(a, b, preferred_element_type=F32, precision=HI)


def _sigmoid(x):
    return 0.5 * jnp.tanh(0.5 * x) + 0.5


def _silu(x):
    half = 0.5 * x
    return half * jnp.tanh(half) + half


def _pack_bf16_pairs(x):
    w = x.shape[1] // 2
    lo = lax.bitcast_convert_type(x[:, :w].astype(BF16).astype(F32), jnp.uint32) >> 16
    hi = lax.bitcast_convert_type(x[:, w:].astype(BF16).astype(F32), jnp.uint32) & jnp.uint32(0xFFFF0000)
    return lo | hi


def _unpack_bf16_pairs(p):
    lo = lax.bitcast_convert_type(p << 16, F32)
    hi = lax.bitcast_convert_type(p & jnp.uint32(0xFFFF0000), F32)
    return jnp.concatenate([lo, hi], axis=-1)


def _softplus(x):
    return jnp.maximum(x, 0.0) + jnp.log1p(jnp.exp(-jnp.abs(x)))


def _norm_permute_kernel(x_ref, lng_ref, *refs, tm, dils):
    outs, h_scr = refs[:-1], refs[-1]
    h = _rms(x_ref[0], lng_ref[...])
    n_cb = h_scr.shape[0]
    for cb in range(n_cb):
        h_scr[cb] = h[:, cb * LANES:(cb + 1) * LANES]
    for o_ref, d in zip(outs, dils):
        for cb in range(n_cb):
            for r in range(d):
                src = h_scr[cb] if d == 1 else h_scr[cb, pl.ds(r, tm // d, stride=d), :]
                o_ref[0, r, :, cb * LANES:(cb + 1) * LANES] = src.astype(BF16)


def _norm_permute(x, ln_g, dils, *, tm):
    B, L, D = x.shape
    assert L % tm == 0 and all(tm % (16 * d) == 0 for d in dils)
    return pl.pallas_call(
        functools.partial(_norm_permute_kernel, tm=tm, dils=dils),
        grid=(B, L // tm),
        in_specs=[pl.BlockSpec((1, tm, D), lambda b, i: (b, i, 0)), pl.BlockSpec((1, D), lambda b, i: (0, 0))],
        out_specs=[pl.BlockSpec((1, d, tm // d, D), lambda b, i: (b, 0, i, 0)) for d in dils],
        out_shape=[jax.ShapeDtypeStruct((B, d, L // d, D), BF16) for d in dils],
        scratch_shapes=[pltpu.VMEM((D // LANES, tm, LANES), F32)],
        compiler_params=_cparams(2),
        name="norm_permute",
    )(x, ln_g)


def _proj_attn_kernel(h_ref, w_ref, ng_ref, bd_ref, p_ref, t_ref, *, n_tiles):
    nr, rows = h_ref.shape[1], h_ref.shape[2]
    z = jnp.dot(h_ref[0].reshape(nr * rows, -1), w_ref[...], preferred_element_type=F32)
    kv = []
    for c in range(0, 3 * SWA_GW, MXU):
        zc = z[:, c:c + MXU]
        if c < 2 * SWA_GW:
            ms = jnp.dot((zc * zc).astype(BF16), bd_ref[...], preferred_element_type=F32)
            zc = zc * lax.rsqrt(ms + EPS) * ng_ref[0, :, c:c + MXU]
        p_ref[0, :, :, c:c + MXU] = zc.astype(BF16).reshape(nr, rows, MXU)
        if c >= SWA_GW:
            kv.append(zc)

    @pl.when(pl.program_id(2) == n_tiles - 1)
    def _():
        per = SWA_GW // MXU
        for rr in range(nr):
            last = slice((rr + 1) * rows - SWA_SPAN, (rr + 1) * rows)
            for s in range(2):
                zr = jnp.concatenate([part[last, :] for part in kv[s * per:(s + 1) * per]], axis=-1)
                t_ref[0, :, rr, s] = zr.reshape(SWA_SPAN, SWA_HEADS, SWA_DIM)


def _proj_attn(hg, w_g, ng_g, bd, *, tmr, nr):
    B, dil, M, D = hg.shape
    assert M % tmr == 0 and tmr >= SWA_SPAN and dil % nr == 0
    nt = M // tmr
    W3 = 3 * SWA_GW
    keep = SWA_SPAN * dil
    p, t = pl.pallas_call(
        functools.partial(_proj_attn_kernel, n_tiles=nt),
        grid=(B, dil // nr, nt),
        in_specs=[
            pl.BlockSpec((1, nr, tmr, D), lambda b, r, i: (b, r, i, 0)),
            pl.BlockSpec((D, W3), lambda b, r, i: (0, 0)),
            pl.BlockSpec((1, 1, 2 * SWA_GW), lambda b, r, i: (0, 0, 0)),
            pl.BlockSpec((MXU, MXU), lambda b, r, i: (0, 0)),
        ],
        out_specs=[
            pl.BlockSpec((1, nr, tmr, W3), lambda b, r, i: (b, r, i, 0)),
            pl.BlockSpec((1, SWA_SPAN, nr, 2, SWA_HEADS, SWA_DIM), lambda b, r, i: (b, 0, r, 0, 0, 0)),
        ],
        out_shape=[
            jax.ShapeDtypeStruct((B, dil, M, W3), BF16),
            jax.ShapeDtypeStruct((B, SWA_SPAN, dil, 2, SWA_HEADS, SWA_DIM), F32),
        ],
        compiler_params=_cparams(3),
        name="proj_attn",
    )(hg, w_g, ng_g, bd)
    return p, t.reshape(B, keep, 2, SWA_HEADS, SWA_DIM)


def _proj_plain_kernel(x_ref, lng_ref, w_ref, o_ref, h_scr):
    @pl.when(pl.program_id(1) == 0)
    def _():
        h_scr[...] = _rms(x_ref[...], lng_ref[...]).astype(BF16)

    o_ref[...] = jnp.dot(h_scr[...], w_ref[...], preferred_element_type=F32).astype(o_ref.dtype)


def _proj_plain(x2d, ln_g, w, *, tm, tn, out_dtype, name="proj_plain"):
    N, D = x2d.shape
    C = w.shape[1]
    assert N % tm == 0 and C % tn == 0
    return pl.pallas_call(
        _proj_plain_kernel,
        grid=(N // tm, C // tn),
        in_specs=[
            pl.BlockSpec((tm, D), lambda i, j: (i, 0)),
            pl.BlockSpec((1, D), lambda i, j: (0, 0)),
            pl.BlockSpec((D, tn), lambda i, j: (0, j)),
        ],
        out_specs=pl.BlockSpec((tm, tn), lambda i, j: (i, j)),
        out_shape=jax.ShapeDtypeStruct((N, C), out_dtype),
        scratch_shapes=[pltpu.VMEM((tm, D), BF16)],
        compiler_params=_cparams(2),
        name=name,
    )(x2d, ln_g, w)


def _matmul_kernel(h_ref, w_ref, o_ref):
    o_ref[...] = jnp.dot(h_ref[...], w_ref[...], preferred_element_type=F32).astype(o_ref.dtype)


def _matmul(h2d, w, *, tm, tn, out_dtype, name):
    N, D = h2d.shape
    C = w.shape[1]
    assert N % tm == 0 and C % tn == 0
    return pl.pallas_call(
        _matmul_kernel,
        grid=(N // tm, C // tn),
        in_specs=[pl.BlockSpec((tm, D), lambda i, j: (i, 0)), pl.BlockSpec((D, tn), lambda i, j: (0, j))],
        out_specs=pl.BlockSpec((tm, tn), lambda i, j: (i, j)),
        out_shape=jax.ShapeDtypeStruct((N, C), out_dtype),
        compiler_params=_cparams(2),
        name=name,
    )(h2d, w)


def _attn_kernel(q_ref, kc_ref, vc_ref, kp_ref, vp_ref, o_ref, lse_ref, kk_scr, vv_scr, *, tq):
    i = pl.program_id(1)
    blk = SWA_SPAN
    kk_scr[0:blk, :] = kp_ref[0]
    kk_scr[blk:blk + tq, :] = kc_ref[0]
    vv_scr[0:blk, :] = vp_ref[0]
    vv_scr[blk:blk + tq, :] = vc_ref[0]
    qi = lax.broadcasted_iota(jnp.int32, (blk, 2 * blk), 0)
    ki = lax.broadcasted_iota(jnp.int32, (blk, 2 * blk), 1)
    dist = blk + qi - ki
    band = (dist >= 0) & (dist <= SWA_SPAN)
    band_first = band & ((ki >= blk) | (i > 0))
    lo = lax.broadcasted_iota(jnp.int32, (blk, LANES), 1) < SWA_DIM
    zero = jnp.zeros((blk, LANES), BF16)
    lane = lax.broadcasted_iota(jnp.int32, (blk, LANES), 1)
    for jb in range(tq // blk):
        mask = band_first if jb == 0 else band
        rows = slice(jb * blk, (jb + 1) * blk)
        lse_all = jnp.zeros((blk, LANES), F32)
        for hp in range(SWA_GW // LANES):
            cs = slice(hp * LANES, (hp + 1) * LANES)
            qb = q_ref[0, rows, cs]
            kk = kk_scr[jb * blk:(jb + 2) * blk, cs]
            vv = vv_scr[jb * blk:(jb + 2) * blk, cs]
            res_o = []
            for hh in range(2):
                qm = jnp.where(lo if hh == 0 else jnp.logical_not(lo), qb, zero)
                s = lax.dot_general(qm, kk, (((1,), (1,)), ((), ())), preferred_element_type=F32)
                s = jnp.where(mask, s, -jnp.inf)
                m = jnp.max(s, axis=-1, keepdims=True)
                p = jnp.exp(s - m)
                den = jnp.sum(p, axis=-1, keepdims=True)
                pv = jnp.dot(p.astype(BF16), vv, preferred_element_type=F32)
                res_o.append(pv / den)
                lse_all = jnp.where(lane == 2 * hp + hh, m + jnp.log(den), lse_all)
            o_ref[0, rows, cs] = jnp.where(lo, res_o[0], res_o[1]).astype(BF16)
        lse_ref[0, rows, :] = lse_all


def _attn(p, *, tq):
    S, M, _ = p.shape
    assert M % tq == 0 and tq % SWA_SPAN == 0
    nb = tq // SWA_SPAN
    return pl.pallas_call(
        functools.partial(_attn_kernel, tq=tq),
        grid=(S, M // tq),
        in_specs=[
            pl.BlockSpec((1, tq, SWA_GW), lambda s, i: (s, i, 0)),
            pl.BlockSpec((1, tq, SWA_GW), lambda s, i: (s, i, 1)),
            pl.BlockSpec((1, tq, SWA_GW), lambda s, i: (s, i, 2)),
            pl.BlockSpec((1, SWA_SPAN, SWA_GW), lambda s, i: (s, jnp.maximum(i * nb - 1, 0), 1)),
            pl.BlockSpec((1, SWA_SPAN, SWA_GW), lambda s, i: (s, jnp.maximum(i * nb - 1, 0), 2)),
        ],
        out_specs=[
            pl.BlockSpec((1, tq, SWA_GW), lambda s, i: (s, i, 0)),
            pl.BlockSpec((1, tq, LANES), lambda s, i: (s, i, 0)),
        ],
        out_shape=[
            jax.ShapeDtypeStruct((S, M, SWA_GW), BF16),
            jax.ShapeDtypeStruct((S, M, LANES), F32),
        ],
        scratch_shapes=[
            pltpu.VMEM((SWA_SPAN + tq, SWA_GW), BF16),
            pltpu.VMEM((SWA_SPAN + tq, SWA_GW), BF16),
        ],
        compiler_params=_cparams(2),
        name="swa_attn",
    )(p, p, p, p, p)


def _proj_dn_kernel(h_ref, w_ref, cw_ref, q_ref, k_ref, v_ref, tail_ref, z_scr, carry_scr, *, tm, n_ct):
    i = pl.program_id(1)
    j = pl.program_id(2)
    nh = DN_HEADS
    ncb = z_scr.shape[1] // LANES

    z_scr[0:8, :] = jnp.where(i == 0, 0.0, carry_scr[j])
    z_scr[8:8 + tm, :] = jnp.dot(h_ref[0], w_ref[...], preferred_element_type=F32)
    last = z_scr[tm:tm + 8, :]
    carry_scr[j] = last
    tn = z_scr.shape[1]
    outs = (q_ref, k_ref, v_ref)
    for jj in range(n_ct):

        @pl.when(j == jj)
        def _(jj=jj):
            tail_ref[0, :, jj * tn:(jj + 1) * tn] = last
            for cbl in range(ncb):
                cs = slice(cbl * LANES, (cbl + 1) * LANES)
                part, h = divmod(jj * ncb + cbl, nh)
                xe = z_scr[:, cs]
                acc = (0.5 * cw_ref[0:1, cs]) * xe
                for t in range(1, DN_CONV):
                    acc = (0.5 * cw_ref[t:t + 1, cs]) * xe + pltpu.roll(acc, 1, axis=0)
                half = acc[8:]
                act = half * jnp.tanh(half) + half
                if part < 2:
                    inv = lax.rsqrt(jnp.sum(act * act, axis=-1, keepdims=True) + EPS)
                    act = act * (inv * (DN_DK ** -0.5) if part == 0 else inv)
                outs[part][0, :, h * LANES:(h + 1) * LANES] = act.astype(BF16)


def _proj_dn(h, w_dn, conv_w, *, tm):
    B, L, D = h.shape
    C = w_dn.shape[1]
    width = DN_HEADS * DN_DK
    n_ct = 2
    tn = C // n_ct
    assert L % tm == 0 and C == 3 * width and tn % LANES == 0
    qkv = pl.BlockSpec((1, tm, width), lambda b, i, j: (b, i, 0))
    return pl.pallas_call(
        functools.partial(_proj_dn_kernel, tm=tm, n_ct=n_ct),
        grid=(B, L // tm, n_ct),
        in_specs=[
            pl.BlockSpec((1, tm, D), lambda b, i, j: (b, i, 0)),
            pl.BlockSpec((D, tn), lambda b, i, j: (0, j)),
            pl.BlockSpec((DN_CONV, tn), lambda b, i, j: (0, j)),
        ],
        out_specs=[qkv, qkv, qkv, pl.BlockSpec((1, 8, C), lambda b, i, j: (b, 0, 0))],
        out_shape=[jax.ShapeDtypeStruct((B, L, width), BF16)] * 3 + [jax.ShapeDtypeStruct((B, 8, C), F32)],
        scratch_shapes=[pltpu.VMEM((8 + tm, tn), F32), pltpu.VMEM((n_ct, 8, tn), F32)],
        compiler_params=_cparams(3),
        name="proj_dn",
    )(h, w_dn, conv_w)


def _gates_kernel(h_ref, w_ref, par_ref, g_ref, gt_ref, *, tl):
    nh = DN_HEADS
    z = jnp.dot(h_ref[0], w_ref[...], preferred_element_type=F32)
    lane = lax.broadcasted_iota(jnp.int32, (tl, LANES), 1)
    ri = lax.broadcasted_iota(jnp.int32, (tl, tl), 0)
    ci = lax.broadcasted_iota(jnp.int32, (tl, tl), 1)
    tri = jnp.where((ri // DN_CHUNK == ci // DN_CHUNK) & (ci <= ri), 1.0, 0.0).astype(BF16)
    for sb in range(z.shape[0] // tl):
        rows = slice(sb * tl, (sb + 1) * tl)
        ba = z[rows]
        g = par_ref[0:1, :] * _softplus(ba + par_ref[1:2, :])
        g_hi = g.astype(BF16)
        r1 = g - g_hi.astype(F32)
        g_mid = r1.astype(BF16)
        g_lo = (r1 - g_mid.astype(F32)).astype(BF16)
        gc = sum(jnp.dot(tri, piece, preferred_element_type=F32) for piece in (g_hi, g_mid, g_lo))
        g_ref[0, rows, :] = jnp.where(lane < nh, _sigmoid(ba), gc)
        gt_ref[0, :, rows] = jnp.transpose(gc)[nh:2 * nh, :]


def _dn_gates(h, w_ba, par, *, tm, tl):
    B, L, D = h.shape
    assert L % tm == 0 and tm % tl == 0 and tl % DN_CHUNK == 0
    return pl.pallas_call(
        functools.partial(_gates_kernel, tl=tl),
        grid=(B, L // tm),
        in_specs=[pl.BlockSpec((1, tm, D), lambda b, i: (b, i, 0)), pl.BlockSpec((D, LANES), lambda b, i: (0, 0)),
                  pl.BlockSpec((2, LANES), lambda b, i: (0, 0))],
        out_specs=[pl.BlockSpec((1, tm, LANES), lambda b, i: (b, i, 0)),
                   pl.BlockSpec((1, DN_HEADS, tm), lambda b, i: (b, 0, i))],
        out_shape=[jax.ShapeDtypeStruct((B, L, LANES), F32), jax.ShapeDtypeStruct((B, DN_HEADS, L), F32)],
        compiler_params=_cparams(2),
        name="dn_gates",
    )(h, w_ba, par)


def _intra_kernel(q_ref, k_ref, v_ref, g_ref, gt_in_ref, u_ref, w_ref, qd_ref, kd_ref, a_ref, gt_ref, *, tl):
    h = pl.program_id(1)
    C = DN_CHUNK
    lane = lax.broadcasted_iota(jnp.int32, (C, LANES), 1)
    ri = lax.broadcasted_iota(jnp.int32, (C, C), 0)
    ci = lax.broadcasted_iota(jnp.int32, (C, C), 1)
    eye = jnp.where(ri == ci, 1.0, 0.0).astype(F32)
    nt_dot = lambda a, b: lax.dot_general(a.astype(BF16), b.astype(BF16), (((1,), (1,)), ((), ())),
                                          preferred_element_type=F32)
    rows = [slice(c * C, (c + 1) * C) for c in range(tl // C)]
    gv = [g_ref[0, r, :] for r in rows]
    q = [q_ref[0, r, :].astype(F32) for r in rows]
    k = [k_ref[0, r, :].astype(F32) for r in rows]
    v = [v_ref[0, r, :].astype(F32) for r in rows]
    beta = [jnp.sum(jnp.where(lane == h, x, 0.0), axis=-1, keepdims=True) for x in gv]
    gc = [jnp.sum(jnp.where(lane == h + DN_HEADS, x, 0.0), axis=-1, keepdims=True) for x in gv]
    gc_row = gt_in_ref[0, pl.ds(h, 1), :]
    decay = [jnp.exp(jnp.where(ri >= ci, a - gc_row[:, r], -jnp.inf)) for a, r in zip(gc, rows)]
    kb = [a * b for a, b in zip(k, beta)]
    kq = [nt_dot(jnp.concatenate([a, b], axis=0), c) for a, b, c in zip(kb, q, k)]
    x = [-jnp.where(ri > ci, m[:C] * d, 0.0) for m, d in zip(kq, decay)]
    t = [eye + a for a in x]
    x = [_bdot(a, a) for a in x]
    for _ in range(4):
        both = [_bdot(jnp.concatenate([a, b], axis=0), a) for a, b in zip(x, t)]
        t = [b + m[C:] for b, m in zip(t, both)]
        x = [m[:C] for m in both]
    t = [b + _bdot(b, a) for a, b in zip(x, t)]
    eg = [jnp.exp(a) for a in gc]
    glast = [a[C - 1:C, :] for a in gc]
    uw = [_bdot(a, jnp.concatenate([b * c, d * e], axis=1)) for a, b, c, d, e in zip(t, v, beta, kb, eg)]
    for c, r in enumerate(rows):
        u_ref[0, 0, r, :] = uw[c][:, :LANES]
        w_ref[0, 0, r, :] = uw[c][:, LANES:].astype(BF16)
        a_ref[0, 0, r, :] = (kq[c][C:] * decay[c]).astype(BF16)
        qd_ref[0, 0, r, :] = (q[c] * eg[c]).astype(BF16)
        kd_ref[0, 0, r, :] = (k[c] * jnp.exp(glast[c] - gc[c])).astype(BF16)
        gt_ref[0, 0, c:c + 1, :] = jnp.broadcast_to(jnp.exp(glast[c]), (1, LANES))


def _dn_intra(q, k, v, g, gt_rows, *, tl):
    B, L, _ = q.shape
    H, C = DN_HEADS, DN_CHUNK
    assert L % tl == 0 and (tl // C) % 8 == 0
    qkv_spec = pl.BlockSpec((1, tl, LANES), lambda b, h, i: (b, i, h))
    hl = lambda w: pl.BlockSpec((1, 1, tl, w), lambda b, h, i: (b, h, i, 0))
    return pl.pallas_call(
        functools.partial(_intra_kernel, tl=tl),
        grid=(B, H, L // tl),
        in_specs=[qkv_spec, qkv_spec, qkv_spec, pl.BlockSpec((1, tl, LANES), lambda b, h, i: (b, i, 0)),
                  pl.BlockSpec((1, H, tl), lambda b, h, i: (b, 0, i))],
        out_specs=[hl(LANES), hl(LANES), hl(LANES), hl(LANES), hl(C),
                   pl.BlockSpec((1, 1, tl // C, LANES), lambda b, h, i: (b, h, i, 0))],
        out_shape=[
            jax.ShapeDtypeStruct((B, H, L, LANES), F32),
            jax.ShapeDtypeStruct((B, H, L, LANES), BF16),
            jax.ShapeDtypeStruct((B, H, L, LANES), BF16),
            jax.ShapeDtypeStruct((B, H, L, LANES), BF16),
            jax.ShapeDtypeStruct((B, H, L, C), BF16),
            jax.ShapeDtypeStruct((B, H, L // C, LANES), F32),
        ],
        compiler_params=_cparams(3),
        name="dn_intra",
    )(q, k, v, g, gt_rows)


def _scan_kernel(u_ref, w_ref, qd_ref, kd_ref, a_ref, gt_ref, s0_ref, o_ref, s_ref, *, n_chunks):
    C = DN_CHUNK
    bb, H = s_ref.shape[0], s_ref.shape[1]
    seqs = [(b, h) for b in range(bb) for h in range(H)]

    @pl.when(pl.program_id(1) == 0)
    def _():
        s_ref[...] = s0_ref[...]

    def body(c, carry):
        rows = pl.ds(pl.multiple_of(c * C, C), C)
        S = [s_ref[b, h] for b, h in seqs]
        Sb = [x.astype(BF16) for x in S]
        v_new = [u_ref[b, h, rows, :] - jnp.dot(w_ref[b, h, rows, :], sb, preferred_element_type=F32)
                 for (b, h), sb in zip(seqs, Sb)]
        vb = [x.astype(BF16) for x in v_new]
        o = [jnp.dot(qd_ref[b, h, rows, :], sb, preferred_element_type=F32)
             + jnp.dot(a_ref[b, h, rows, :], v, preferred_element_type=F32) for (b, h), sb, v in zip(seqs, Sb, vb)]
        upd = [lax.dot_general(kd_ref[b, h, rows, :], v, (((0,), (0,)), ((), ())), preferred_element_type=F32)
               for (b, h), v in zip(seqs, vb)]
        for n, (b, h) in enumerate(seqs):
            o_ref[b, rows, h * LANES:(h + 1) * LANES] = o[n]
            s_ref[b, h] = S[n] * gt_ref[b, h, pl.ds(c, 1), :] + upd[n]
        return carry

    lax.fori_loop(0, n_chunks, body, 0)


def _dn_scan(u, w, qd, kd, a, gt, s0, *, tl, bb):
    B, H, L, _ = u.shape
    C = DN_CHUNK
    assert L % tl == 0 and (tl // C) % 8 == 0 and B % bb == 0
    hs = lambda wd: pl.BlockSpec((bb, H, tl, wd), lambda b, i: (b, 0, i, 0))
    s_spec = pl.BlockSpec((bb, H, DN_DK, LANES), lambda b, i: (b, 0, 0, 0))
    return pl.pallas_call(
        functools.partial(_scan_kernel, n_chunks=tl // C),
        grid=(B // bb, L // tl),
        in_specs=[hs(LANES), hs(LANES), hs(LANES), hs(LANES), hs(C),
                  pl.BlockSpec((bb, H, tl // C, LANES), lambda b, i: (b, 0, i, 0)), s_spec],
        out_specs=[pl.BlockSpec((bb, tl, H * LANES), lambda b, i: (b, i, 0)), s_spec],
        out_shape=[jax.ShapeDtypeStruct((B, L, H * LANES), F32),
                   jax.ShapeDtypeStruct((B, H, DN_DK, LANES), F32)],
        compiler_params=_cparams(2),
        name="dn_scan",
    )(u, w, qd, kd, a, gt, s0)


def _gated_mix(o_a, od, gates, dng, wa, wb, wo, x, dot):
    width = DN_HEADS * DN_DK
    parts = []
    for h in range(DN_HEADS):
        blk = od[:, h * LANES:(h + 1) * LANES]
        parts.append(blk * lax.rsqrt(jnp.mean(blk * blk, axis=-1, keepdims=True) + EPS) * dng)
    odn = jnp.concatenate(parts, axis=-1) * _silu(gates[:, 0:width].astype(F32))
    ya = dot(o_a, wa)
    yb = dot(odn, wb)
    mix = _sigmoid(gates[:, width:2 * width].astype(F32)) * ya + _sigmoid(gates[:, 2 * width:].astype(F32)) * yb
    return x + dot(mix, wo)


def _out_kernel(x_ref, o0, o1, o2, l0, l1, l2, od_ref, gates_ref, dng_ref, wa_ref, wb_ref, wo_ref, e_ref, y_ref,
                so0, so1, so2, sl0, sl1, sl2, *, tm, dils):
    o_refs, l_refs = (o0, o1, o2), (l0, l1, l2)
    so, sl = (so0, so1, so2), (sl0, sl1, sl2)
    for gi, d in enumerate(dils):
        for r in range(d):
            dst = slice(None) if d == 1 else pl.ds(r, tm // d, stride=d)
            sl[gi][dst, :] = l_refs[gi][0, r]
            for cb in range(SWA_GW // LANES):
                so[gi][cb, dst, :] = o_refs[gi][0, r, :, cb * LANES:(cb + 1) * LANES].astype(F32)
    ls = [s[...] for s in sl]
    m = jnp.maximum(jnp.maximum(ls[0], ls[1]), ls[2])
    es = [jnp.exp(l - m) for l in ls]
    tot = es[0] + es[1] + es[2]
    alphas = [jnp.dot((e / tot).astype(BF16), e_ref[...], preferred_element_type=F32) for e in es]
    parts = []
    for cb in range(SWA_GW // LANES):
        cs = slice(cb * LANES, (cb + 1) * LANES)
        parts.append(alphas[0][:, cs] * so[0][cb] + alphas[1][:, cs] * so[1][cb] + alphas[2][:, cs] * so[2][cb])
    o_a = jnp.concatenate(parts, axis=-1)
    y_ref[...] = _gated_mix(o_a, od_ref[...], gates_ref[...], dng_ref[...], wa_ref[...], wb_ref[...],
                            wo_ref[...], x_ref[...], _bdot)


def _out_proj(x2d, os_, ls_, od2d, gates, dng, wa, wb, wo, e_att, *, B, L, tm):
    N, D = x2d.shape
    nt = L // tm
    dils = tuple(d for _, d in SWA_CONFIGS)
    grp = lambda d, w: pl.BlockSpec((1, d, tm // d, w), lambda i: (i // nt, 0, i % nt, 0))
    row = lambda w: pl.BlockSpec((tm, w), lambda i: (i, 0))
    full = lambda a: pl.BlockSpec(a.shape, lambda i: (0, 0))
    return pl.pallas_call(
        functools.partial(_out_kernel, tm=tm, dils=dils),
        grid=(N // tm,),
        in_specs=[row(D)] + [grp(d, SWA_GW) for d in dils] + [grp(d, LANES) for d in dils]
        + [row(od2d.shape[1]), row(gates.shape[1]), full(dng), full(wa), full(wb), full(wo), full(e_att)],
        out_specs=row(D),
        out_shape=jax.ShapeDtypeStruct((N, D), F32),
        scratch_shapes=[pltpu.VMEM((SWA_GW // LANES, tm, LANES), F32)] * 3 + [pltpu.VMEM((tm, LANES), F32)] * 3,
        compiler_params=_cparams(1),
        name="out_proj",
    )(x2d, *os_, *ls_, od2d, gates, dng, wa, wb, wo, e_att)


def _router_kernel(x_ref, lng_ref, wr_ref, br_ref, info_ref, cnt_ref, base_scr, *, tm):
    i = pl.program_id(0)

    @pl.when(i == 0)
    def _():
        base_scr[...] = jnp.zeros_like(base_scr)

    h = _rms(x_ref[...], lng_ref[...])
    lg = _bdot(h, wr_ref[...]) + br_ref[...]
    lane = lax.broadcasted_iota(jnp.int32, (tm, LANES), 1)
    big = jnp.int32(1 << 20)
    ninf = -jnp.inf

    def argmax_lane(vals):
        mx = jnp.max(vals, axis=-1, keepdims=True)
        idx = jnp.min(jnp.where(vals == mx, lane, big), axis=-1, keepdims=True)
        return mx, idx

    lgm = jnp.where(lane < N_GROUPS, lg, ninf)
    mg, gsel = argmax_lane(lgm)
    pg = 1.0 / jnp.sum(jnp.exp(lgm - mg), axis=-1, keepdims=True)
    start = N_GROUPS + gsel * PER_GROUP
    le = jnp.where((lane >= start) & (lane < start + PER_GROUP), lg, ninf)
    m1, i1 = argmax_lane(le)
    m2, i2 = argmax_lane(jnp.where(lane == i1, ninf, le))
    e21 = jnp.exp(m2 - m1)
    w1 = pg / (1.0 + e21)
    w2 = pg * e21 / (1.0 + e21)
    oh = jnp.where(lane == i1, 1.0, 0.0) + jnp.where(lane == i2, 1.0, 0.0)
    ri = lax.broadcasted_iota(jnp.int32, (tm, tm), 0)
    ci = lax.broadcasted_iota(jnp.int32, (tm, tm), 1)
    strict = jnp.where(ci < ri, 1.0, 0.0).astype(BF16)
    pref = jnp.dot(strict, oh.astype(BF16), preferred_element_type=F32) + base_scr[...]
    r1 = jnp.sum(jnp.where(lane == i1, pref, 0.0), axis=-1, keepdims=True)
    r2 = jnp.sum(jnp.where(lane == i2, pref, 0.0), axis=-1, keepdims=True)
    base_scr[...] = base_scr[...] + jnp.sum(oh, axis=0, keepdims=True)
    cnt_ref[...] = base_scr[...]
    off = jnp.float32(N_GROUPS)
    info = jnp.where(lane == 0, i1.astype(F32) - off, 0.0)
    info = jnp.where(lane == 1, i2.astype(F32) - off, info)
    info = jnp.where(lane == 2, w1, info)
    info = jnp.where(lane == 3, w2, info)
    info = jnp.where(lane == 4, r1, info)
    info = jnp.where(lane == 5, r2, info)
    info_ref[...] = info


def _router(x2d, ln_g, wr, br, *, tm):
    N, D = x2d.shape
    assert N % tm == 0
    return pl.pallas_call(
        functools.partial(_router_kernel, tm=tm),
        grid=(N // tm,),
        in_specs=[
            pl.BlockSpec((tm, D), lambda i: (i, 0)),
            pl.BlockSpec((1, D), lambda i: (0, 0)),
            pl.BlockSpec((D, LANES), lambda i: (0, 0)),
            pl.BlockSpec((1, LANES), lambda i: (0, 0)),
        ],
        out_specs=[pl.BlockSpec((tm, LANES), lambda i: (i, 0)), pl.BlockSpec((1, LANES), lambda i: (0, 0))],
        out_shape=[jax.ShapeDtypeStruct((N, LANES), F32), jax.ShapeDtypeStruct((1, LANES), F32)],
        scratch_shapes=[pltpu.VMEM((1, LANES), F32)],
        compiler_params=_cparams(1),
        name="router",
    )(x2d, ln_g, wr, br)


def _dispatch_kernel(dest_ref, zb_ref, x_ref, lng_ref, xs_ref, zero_scr, rows_scr, sem, *, tm, tb, n_zb, n_tiles):
    i = pl.program_id(0)

    @pl.when(i == 0)
    def _():
        zero_scr[...] = jnp.zeros_like(zero_scr)

        def zero_copy(n):
            return pltpu.make_async_copy(zero_scr, xs_ref.at[pl.ds(zb_ref[n] * tb, tb)], sem.at[2])

        def zero_issue(n, carry):
            @pl.when(zb_ref[n] >= 0)
            def _():
                zero_copy(n).start()

            return carry

        def zero_wait(n, carry):
            @pl.when(zb_ref[n] >= 0)
            def _():
                zero_copy(n).wait()

            return carry

        lax.fori_loop(0, n_zb, zero_issue, 0)
        lax.fori_loop(0, n_zb, zero_wait, 0)

    buf_now = lax.rem(i, 2)
    rows_scr[buf_now] = _pack_bf16_pairs(_rms(x_ref[...], lng_ref[...])).reshape(rows_scr.shape[1:])

    def row_copy(tile, t, slot):
        buf = lax.rem(tile, 2)
        return pltpu.make_async_copy(
            rows_scr.at[buf, pl.ds(t, 1)],
            xs_ref.at[pl.ds(dest_ref[(tile * tm + t) * TOP_K + slot], 1)], sem.at[buf])

    def issue(tt, carry):
        for r in range(ROW_UNROLL):
            for slot in range(TOP_K):
                row_copy(i, tt * ROW_UNROLL + r, slot).start(priority=slot)
        return carry

    def drain(tile):
        buf = lax.rem(tile, 2)
        for _ in range(TOP_K):
            pltpu.make_async_copy(rows_scr.at[buf], rows_scr.at[buf], sem.at[buf]).wait()

    lax.fori_loop(0, tm // ROW_UNROLL, issue, 0)

    @pl.when(i > 0)
    def _():
        drain(i - 1)

    @pl.when(i == n_tiles - 1)
    def _():
        drain(i)


def _dispatch(dest, zero_blocks, x2d, ln_g, *, tm, tb, n_rows):
    N, D = x2d.shape
    row = (D // 2 // LANES, LANES)
    return pl.pallas_call(
        functools.partial(_dispatch_kernel, tm=tm, tb=tb, n_zb=zero_blocks.shape[0], n_tiles=N // tm),
        grid_spec=pltpu.PrefetchScalarGridSpec(
            num_scalar_prefetch=2,
            grid=(N // tm,),
            in_specs=[pl.BlockSpec((tm, D), lambda i, d, z: (i, 0)), pl.BlockSpec((1, D), lambda i, d, z: (0, 0))],
            out_specs=pl.BlockSpec(memory_space=pl.ANY),
            scratch_shapes=[pltpu.VMEM((tb,) + row, jnp.uint32), pltpu.VMEM((2, tm) + row, jnp.uint32),
                            pltpu.SemaphoreType.DMA((3,))],
        ),
        out_shape=jax.ShapeDtypeStruct((n_rows,) + row, jnp.uint32),
        compiler_params=_cparams(1),
        name="moe_dispatch",
    )(dest, zero_blocks, x2d, ln_g)


def _ffn_kernel(be_ref, nb_ref, xs_ref, wg_ref, wu_ref, wd_ref, y_ref, wg_scr, wu_scr, wd_scr):
    i = pl.program_id(0)
    used = i < nb_ref[0]

    @pl.when(jnp.logical_or(i == 0, be_ref[i] != be_ref[jnp.maximum(i - 1, 0)]))
    def _():
        wg_scr[...] = wg_ref[0].astype(BF16)
        wu_scr[...] = wu_ref[0].astype(BF16)
        wd_scr[...] = wd_ref[0].astype(BF16)

    @pl.when(used)
    def _():
        h = _unpack_bf16_pairs(xs_ref[...].reshape(xs_ref.shape[0], -1)).astype(BF16)
        g = jnp.dot(h, wg_scr[...], preferred_element_type=F32)
        u = jnp.dot(h, wu_scr[...], preferred_element_type=F32)
        y = jnp.dot((_silu(g) * u).astype(BF16), wd_scr[...], preferred_element_type=F32)
        y_ref[...] = _pack_bf16_pairs(y).reshape(y_ref.shape)

    @pl.when(jnp.logical_not(used))
    def _():
        y_ref[...] = jnp.zeros_like(y_ref)


def _ffn(blk_e, nb_used, xs, wg, wu, wd, layer, *, tb):
    P, S, _ = xs.shape
    D = 2 * S * LANES
    nb = P // tb
    DE = wg.shape[3]
    return pl.pallas_call(
        _ffn_kernel,
        grid_spec=pltpu.PrefetchScalarGridSpec(
            num_scalar_prefetch=2,
            grid=(nb,),
            in_specs=[
                pl.BlockSpec((tb, S, LANES), lambda i, be, nbu: (jnp.minimum(i, nbu[0] - 1), 0, 0)),
                pl.BlockSpec((None, 1, D, DE), lambda i, be, nbu: (layer, be[i], 0, 0)),
                pl.BlockSpec((None, 1, D, DE), lambda i, be, nbu: (layer, be[i], 0, 0)),
                pl.BlockSpec((None, 1, DE, D), lambda i, be, nbu: (layer, be[i], 0, 0)),
            ],
            out_specs=pl.BlockSpec((tb, S, LANES), lambda i, be, nbu: (i, 0, 0)),
            scratch_shapes=[pltpu.VMEM((D, DE), BF16), pltpu.VMEM((D, DE), BF16), pltpu.VMEM((DE, D), BF16)],
        ),
        out_shape=jax.ShapeDtypeStruct((P, S, LANES), jnp.uint32),
        compiler_params=_cparams(1),
        name="moe_ffn",
    )(blk_e, nb_used, xs, wg, wu, wd)


def _combine_kernel(dest_ref, x_ref, info_ref, yb_ref, y_ref, g_scr, sem, *, tm, n_tiles):
    i = pl.program_id(0)

    def row_copy(tile, t, slot):
        buf = lax.rem(tile, 2)
        return pltpu.make_async_copy(
            yb_ref.at[pl.ds(dest_ref[(tile * tm + t) * TOP_K + slot], 1)],
            g_scr.at[buf, slot, pl.ds(t, 1)], sem.at[buf])

    def issue_tile(tile):
        def body(tt, carry):
            for r in range(ROW_UNROLL):
                for slot in range(TOP_K):
                    row_copy(tile, tt * ROW_UNROLL + r, slot).start(priority=slot)
            return carry

        lax.fori_loop(0, tm // ROW_UNROLL, body, 0)

    @pl.when(i == 0)
    def _():
        issue_tile(i)

    @pl.when(i + 1 < n_tiles)
    def _():
        issue_tile(i + 1)

    buf = lax.rem(i, 2)
    pltpu.make_async_copy(g_scr.at[buf], g_scr.at[buf], sem.at[buf]).wait()
    info = info_ref[...]
    lane = lax.broadcasted_iota(jnp.int32, info.shape, 1)
    w1 = jnp.sum(jnp.where(lane == 2, info, 0.0), axis=-1, keepdims=True)
    w2 = jnp.sum(jnp.where(lane == 3, info, 0.0), axis=-1, keepdims=True)
    g1 = _unpack_bf16_pairs(g_scr[buf, 0].reshape(tm, -1))
    g2 = _unpack_bf16_pairs(g_scr[buf, 1].reshape(tm, -1))
    y_ref[...] = x_ref[...] + (w1 * g1 + w2 * g2)


def _combine(dest, x2d, info, yb, *, tm):
    N, D = x2d.shape
    return pl.pallas_call(
        functools.partial(_combine_kernel, tm=tm, n_tiles=N // tm),
        grid_spec=pltpu.PrefetchScalarGridSpec(
            num_scalar_prefetch=1,
            grid=(N // tm,),
            in_specs=[
                pl.BlockSpec((tm, D), lambda i, d: (i, 0)),
                pl.BlockSpec((tm, LANES), lambda i, d: (i, 0)),
                pl.BlockSpec(memory_space=pl.ANY),
            ],
            out_specs=pl.BlockSpec((tm, D), lambda i, d: (i, 0)),
            scratch_shapes=[pltpu.VMEM((2, TOP_K, tm, D // 2 // LANES, LANES), jnp.uint32),
                            pltpu.SemaphoreType.DMA((2,))],
        ),
        out_shape=jax.ShapeDtypeStruct((N, D), F32),
        compiler_params=_cparams(1),
        name="moe_combine",
    )(dest, x2d, info, yb)


def _moe(x2d, ln2_g, wr, br, wg, wu, wd, layer, *, tm):
    N, D = x2d.shape
    tb = MOE_ROWS if N * TOP_K >= N_EXPERTS * MOE_ROWS else MOE_ROWS_SMALL
    info, counts = _router(x2d, ln2_g, wr, br, tm=ROUTER_ROWS if N % ROUTER_ROWS == 0 else tm)
    counts = counts[0, N_GROUPS:N_GROUPS + N_EXPERTS].astype(jnp.int32)
    pcounts = (counts + tb - 1) // tb * tb
    pend = jnp.cumsum(pcounts)
    pstart = pend - pcounts
    e = info[:, 0:TOP_K].astype(jnp.int32)
    rank = info[:, 4:4 + TOP_K].astype(jnp.int32)
    experts = jnp.arange(N_EXPERTS, dtype=jnp.int32)
    dest = (jnp.sum(jnp.where(e[..., None] == experts, pstart, 0), axis=-1) + rank).reshape(-1)
    nb = -(-(N * TOP_K) // tb) + N_EXPERTS
    P = nb * tb
    blocks = jnp.arange(nb, dtype=jnp.int32)
    blk_e = jnp.minimum(jnp.sum((pend[None, :] <= blocks[:, None] * tb).astype(jnp.int32), axis=1), N_EXPERTS - 1)
    nb_used = (pend[-1] // tb).astype(jnp.int32).reshape(1)
    zero_blocks = jnp.concatenate([jnp.where(counts % tb != 0, pend // tb - 1, -1),
                                   jnp.where(blocks >= nb_used[0], blocks, -1)]).astype(jnp.int32)
    xs = _dispatch(dest, zero_blocks, x2d, ln2_g, tm=tm, tb=tb, n_rows=P)
    yb = _ffn(blk_e, nb_used, xs, wg, wu, wd, layer, tb=tb)
    return _combine(dest, x2d, info, yb, tm=tm)


def _rows8(x):
    return jnp.broadcast_to(x, (8, x.shape[1]))


def _row_hdot(x, m):
    return _hdot(_rows8(x), m)[0:1]


def _bf_round(x):
    return x.astype(BF16).astype(F32)


def _sample_attn_kernel(z_ref, c0, c1, c2, qg_ref, kg_ref, oa_ref, kv_ref):
    W = SWA_GW
    scale = SWA_DIM ** -0.5
    z = z_ref[0]
    sub = lax.broadcasted_iota(jnp.int32, (SWA_HEADS, W), 0)
    lane = lax.broadcasted_iota(jnp.int32, (SWA_HEADS, W), 1)
    own = lane // SWA_DIM == sub

    def heads(row):
        return jnp.where(own, jnp.broadcast_to(row, (SWA_HEADS, W)), 0.0)

    def head_sum(row):
        return jnp.sum(heads(row), axis=-1, keepdims=True)

    def spread(col):
        return jnp.sum(jnp.where(own, col, 0.0), axis=0, keepdims=True)

    def headnorm(zz, g):
        return zz * spread(lax.rsqrt(head_sum(zz * zz) * (1.0 / SWA_DIM) + EPS)) * g

    outs, lses = [], []
    for gi, (c_ref, (win, dil)) in enumerate(zip((c0, c1, c2), SWA_CONFIGS)):
        q = headnorm(z[:, gi * W:(gi + 1) * W], qg_ref[gi:gi + 1, :])
        k = headnorm(z[:, 3 * W + gi * W:3 * W + (gi + 1) * W], kg_ref[gi:gi + 1, :])
        v = z[:, 6 * W + gi * W:6 * W + (gi + 1) * W]
        kv_ref[0, :, 2 * gi * W:(2 * gi + 1) * W] = k
        kv_ref[0, :, (2 * gi + 1) * W:(2 * gi + 2) * W] = v
        kc = c_ref[0].reshape(W, win).astype(BF16)
        vc = c_ref[1].reshape(W, win).astype(BF16)
        s_c = jnp.dot(heads(q).astype(BF16), kc, preferred_element_type=F32) * scale
        row = lax.broadcasted_iota(jnp.int32, s_c.shape, 1)
        s_c = jnp.where(row % dil == 0, s_c, -jnp.inf)
        s_n = head_sum(_bf_round(k) * _bf_round(q)) * scale
        m = jnp.maximum(jnp.max(s_c, axis=-1, keepdims=True), s_n)
        p_c = jnp.exp(s_c - m)
        p_n = jnp.exp(s_n - m)
        den = jnp.sum(p_c, axis=-1, keepdims=True) + p_n
        pv = lax.dot_general(p_c.astype(BF16), vc, (((1,), (1,)), ((), ())), preferred_element_type=F32)
        num = jnp.sum(jnp.where(own, pv, 0.0), axis=0, keepdims=True) + spread(_bf_round(p_n)) * _bf_round(v)
        outs.append(num / spread(den))
        lses.append(m + jnp.log(den))
    mm = jnp.maximum(jnp.maximum(lses[0], lses[1]), lses[2])
    es = [jnp.exp(l - mm) for l in lses]
    tot = es[0] + es[1] + es[2]
    oa_ref[0] = sum(spread(_bf_round(e / tot)) * _bf_round(o) for e, o in zip(es, outs))


def _sample_attn(z3, caches, layer, qg, kg):
    Bs = z3.shape[0]
    W = SWA_GW
    cviews, cspecs = [], []
    for (win, dil), c in zip(SWA_CONFIGS, caches):
        assert c.shape[2] == win
        cviews.append(jnp.transpose(c, (0, 1, 3, 4, 5, 2)))
        cspecs.append(pl.BlockSpec((None, None, 2, SWA_HEADS, SWA_DIM, win), lambda b: (layer, b, 0, 0, 0, 0)))
    full = lambda a: pl.BlockSpec(a.shape, lambda b: (0,) * a.ndim)
    return pl.pallas_call(
        _sample_attn_kernel,
        grid=(Bs,),
        in_specs=[pl.BlockSpec((1, 1, 9 * W), lambda b: (b, 0, 0))] + cspecs + [full(qg), full(kg)],
        out_specs=[pl.BlockSpec((1, 1, W), lambda b: (b, 0, 0)), pl.BlockSpec((1, 1, 6 * W), lambda b: (b, 0, 0))],
        out_shape=[jax.ShapeDtypeStruct((Bs, 1, W), F32), jax.ShapeDtypeStruct((Bs, 1, 6 * W), F32)],
        compiler_params=_cparams(1),
        name="sample_attn",
    )(z3, *cviews, qg, kg)


def _sample_dn_kernel(raw_ref, cs_ref, cw_ref, ba_ref, par_ref, s_ref, e_ref, etb_ref, etg_ref, o_ref, so_ref):
    E, ETB, ETG = e_ref[...], etb_ref[...], etg_ref[...]
    width = DN_HEADS * DN_DK
    conv = cw_ref[DN_CONV - 1:DN_CONV, :] * raw_ref[0]
    for t in range(DN_CONV - 1):
        conv = conv + cw_ref[t:t + 1, :] * cs_ref[0, t:t + 1, :]
    act = _silu(conv)

    def l2(zz):
        return zz * _row_hdot(lax.rsqrt(_row_hdot(zz * zz, E) + EPS), ETB)

    qn = l2(act[:, 0:width]) * (DN_DK ** -0.5)
    kn = l2(act[:, width:2 * width])
    vn = act[:, 2 * width:3 * width]
    ba = ba_ref[0]
    beta = _row_hdot(_sigmoid(ba), ETB)
    eg = jnp.exp(_row_hdot(par_ref[0:1, :] * _softplus(ba + par_ref[1:2, :]), ETG))
    row0 = lax.broadcasted_iota(jnp.int32, (8, LANES), 0) == 0
    for h in range(DN_HEADS):
        sl = slice(h * LANES, (h + 1) * LANES)
        S = s_ref[0, h]
        q, k, v, b, e = qn[:, sl], kn[:, sl], vn[:, sl], beta[:, sl], eg[:, sl]
        Sb = S.astype(BF16)
        wq = jnp.concatenate([k * b * e, q * e, jnp.zeros((6, LANES), F32)], axis=0)
        both = jnp.dot(wq.astype(BF16), Sb, preferred_element_type=F32)
        v_new = v * b - both[0:1]
        a = jnp.sum(q * k, axis=-1, keepdims=True)
        o_ref[0, :, sl] = both[1:2] + a * v_new
        k8 = jnp.where(row0, _rows8(k), 0.0)
        upd = lax.dot_general(k8, _rows8(v_new), (((0,), (0,)), ((), ())), preferred_element_type=F32, precision=HI)
        so_ref[0, h] = S * e + upd


def _sample_dn(raw3, conv_state, s0, layer, conv_w, ba3, par, e_mat, etb, etg):
    Bs, _, C = raw3.shape
    H = DN_HEADS
    full = lambda a: pl.BlockSpec(a.shape, lambda b: (0,) * a.ndim)
    return pl.pallas_call(
        _sample_dn_kernel,
        grid=(Bs,),
        in_specs=[pl.BlockSpec((1, 1, C), lambda b: (b, 0, 0)),
                  pl.BlockSpec((None, 1, DN_CONV - 1, C), lambda b: (layer, b, 0, 0)),
                  full(conv_w),
                  pl.BlockSpec((1, 1, LANES), lambda b: (b, 0, 0)),
                  full(par),
                  pl.BlockSpec((None, 1, H, DN_DK, LANES), lambda b: (layer, b, 0, 0, 0)),
                  full(e_mat), full(etb), full(etg)],
        out_specs=[pl.BlockSpec((1, 1, H * LANES), lambda b: (b, 0, 0)),
                   pl.BlockSpec((1, H, DN_DK, LANES), lambda b: (b, 0, 0, 0))],
        out_shape=[jax.ShapeDtypeStruct((Bs, 1, H * LANES), F32), jax.ShapeDtypeStruct(s0.shape[1:], F32)],
        compiler_params=_cparams(1),
        name="sample_dn",
    )(raw3, conv_state, conv_w, ba3, par, s0, e_mat, etb, etg)


def _sample_out_kernel(x_ref, oa_ref, od_ref, gates_ref, dng_ref, wa_ref, wb_ref, wo_ref, y_ref):
    y_ref[...] = _gated_mix(oa_ref[...], od_ref[...], gates_ref[...], dng_ref[...], wa_ref[...], wb_ref[...],
                            wo_ref[...], x_ref[...], _bdot)


def _sample_out(x2d, oa, od, gates, dng, wa, wb, wo):
    args = (x2d, oa, od, gates, dng, wa, wb, wo)
    return pl.pallas_call(
        _sample_out_kernel,
        grid=(1,),
        in_specs=[pl.BlockSpec(a.shape, lambda i: (0, 0)) for a in args],
        out_specs=pl.BlockSpec(x2d.shape, lambda i: (0, 0)),
        out_shape=jax.ShapeDtypeStruct(x2d.shape, F32),
        compiler_params=_cparams(1),
        name="sample_out",
    )(*args)


def _head_indicator(width, head):
    c = jnp.arange(width)[:, None] // head
    return (c == jnp.arange(LANES)[None, :]).astype(F32)


def _prep_layer(l, ln1_g, w_in, q_norm_g, k_norm_g, dn_conv_w, dn_a_log, dn_dt_bias, dn_norm_g, w_out_a, w_out_b,
                w_o, ln2_g, w_rg, b_rg, w_re, b_re, w_e_gate, w_e_up, w_e_down):
    D = w_in.shape[1]
    a_w = 3 * 3 * SWA_GW
    dn_w = DN_HEADS * 3 * DN_DK
    hv = DN_HEADS * DN_DK
    w = w_in[l]
    splits = dict(att=w[:, :a_w], dn=w[:, a_w:a_w + dn_w],
                  ba=jnp.pad(w[:, a_w + dn_w:a_w + dn_w + 2 * DN_HEADS], ((0, 0), (0, LANES - 2 * DN_HEADS))),
                  gate=w[:, a_w + dn_w + 2 * DN_HEADS:])
    assert splits["gate"].shape[1] == hv + 2 * D
    tile_heads = lambda g: jnp.broadcast_to(g[:, None, :], (len(SWA_CONFIGS), SWA_HEADS, SWA_DIM)).reshape(len(SWA_CONFIGS), SWA_GW)
    qg, kg = tile_heads(q_norm_g[l]), tile_heads(k_norm_g[l])
    idx = jnp.arange(MXU) // SWA_DIM
    n_g = len(SWA_CONFIGS)
    par = jnp.zeros((2, LANES), F32)
    par = par.at[0, DN_HEADS:2 * DN_HEADS].set(-jnp.exp(dn_a_log[l].astype(F32)))
    par = par.at[1, DN_HEADS:2 * DN_HEADS].set(dn_dt_bias[l].astype(F32))
    wr = jnp.pad(jnp.concatenate([w_rg[l], w_re[l]], axis=1), ((0, 0), (0, LANES - N_GROUPS - N_EXPERTS)))
    br = jnp.pad(jnp.concatenate([b_rg[l], b_re[l]]), (0, LANES - N_GROUPS - N_EXPERTS)).reshape(1, LANES)
    e8 = _head_indicator(hv, DN_DK)
    return dict(
        bf16={k: v.astype(BF16) for k, v in splits.items()},
        ln1=ln1_g[l].reshape(1, D), ln2=ln2_g[l].reshape(1, D),
        qg=qg, kg=kg,
        w_grp=[jnp.concatenate([w[:, s * n_g * SWA_GW + g * SWA_GW:s * n_g * SWA_GW + (g + 1) * SWA_GW]
                                for s in range(3)], axis=1).astype(BF16) for g in range(n_g)],
        ng_grp=[jnp.concatenate([qg[g] * SWA_DIM ** -0.5, kg[g]]).reshape(1, 1, 2 * SWA_GW) for g in range(n_g)],
        bd=((idx[:, None] == idx[None, :]).astype(F32) / SWA_DIM).astype(BF16),
        conv_w=dn_conv_w[l], par=par, dng=dn_norm_g[l].reshape(1, DN_DK),
        wa=w_out_a[l].astype(BF16), wb=w_out_b[l].astype(BF16), wo=w_o[l].astype(BF16), wr=wr.astype(BF16), br=br,
        wg=w_e_gate, wu=w_e_up, wd=w_e_down, layer=l,
        e_dn=e8, etb=e8.T, etg=jnp.roll(e8, DN_HEADS, axis=1).T,
        e_att=_head_indicator(SWA_GW, SWA_DIM).T.astype(BF16),
    )


def _layer_prompt(x, p):
    B, L, D = x.shape
    N = B * L
    x2d = x.reshape(N, D)
    bw = p["bf16"]
    pks, tails = [], []
    hgs = _norm_permute(x, p["ln1"], tuple(d for _, d in SWA_CONFIGS), tm=min(512, L))
    for g, (win, dil) in enumerate(SWA_CONFIGS):
        assert L >= win
        tmr = min(512, L // dil)
        pk, tail = _proj_attn(hgs[g], p["w_grp"][g], p["ng_grp"][g], p["bd"], tmr=tmr, nr=min(dil, max(1, 512 // tmr)))
        pks.append(pk)
        tails.append(tail)
    tmp = min(1024, N)
    h0 = hgs[0].reshape(B, L, D)
    qd, kd, vd, raw_tail = _proj_dn(h0, bw["dn"], p["conv_w"], tm=min(512, L))
    gates = _matmul(h0.reshape(N, D), bw["gate"], tm=tmp, tn=1536, out_dtype=BF16, name="proj_gate")
    os_, ls_ = [], []
    for pk in pks:
        d, M = pk.shape[1], pk.shape[2]
        o, lse = _attn(pk.reshape(B * d, M, pk.shape[3]), tq=min(512, M))
        os_.append(o.reshape(B, d, M, SWA_GW))
        ls_.append(lse.reshape(B, d, M, LANES))
    gb, gt_rows = _dn_gates(h0, bw["ba"], p["par"], tm=min(1024, L), tl=min(256, L))
    u, w, qdec, kdec, a, gt = _dn_intra(qd, kd, vd, gb, gt_rows, tl=min(2048, L))
    od, s_new = _dn_scan(u, w, qdec, kdec, a, gt, jnp.zeros((B, DN_HEADS, DN_DK, LANES), F32), tl=min(512, L),
                         bb=2 if B % 2 == 0 else 1)
    x2 = _out_proj(x2d, os_, ls_, od.reshape(N, -1), gates, p["dng"], p["wa"], p["wb"], p["wo"], p["e_att"],
                   B=B, L=L, tm=min(512, L))
    y = _moe(x2, p["ln2"], p["wr"], p["br"], p["wg"], p["wu"], p["wd"], p["layer"], tm=512)
    return y.reshape(B, L, D), tails, raw_tail[:, 8 - (DN_CONV - 1):], s_new


def _layer_sample(x, caches, conv_state, s0, layer, p):
    Bs, T, D = x.shape
    assert T == 1
    x2d = x.reshape(Bs, D)
    bw = p["bf16"]
    proj = functools.partial(_proj_plain, x2d, p["ln1"], tm=Bs, out_dtype=F32)
    z_att = proj(bw["att"], tn=1536, name="sproj_att")
    raw = proj(bw["dn"], tn=1536, name="sproj_dn")
    gates = proj(bw["gate"], tn=1536, name="sproj_gate")
    ba = proj(bw["ba"], tn=LANES, name="sproj_ba")
    oa, kv = _sample_attn(z_att.reshape(Bs, 1, -1), caches, layer, p["qg"], p["kg"])
    raw3 = raw.reshape(Bs, 1, -1)
    od, s_new = _sample_dn(raw3, conv_state, s0, layer, p["conv_w"], ba.reshape(Bs, 1, LANES), p["par"],
                           p["e_dn"], p["etb"], p["etg"])
    x2 = _sample_out(x2d, oa.reshape(Bs, -1), od.reshape(Bs, -1), gates, p["dng"], p["wa"], p["wb"], p["wo"])
    y = _moe(x2, p["ln2"], p["wr"], p["br"], p["wg"], p["wu"], p["wd"], p["layer"], tm=Bs)
    W2 = 2 * SWA_GW
    kvs = [kv[:, :, g * W2:(g + 1) * W2].reshape(Bs, 1, 2, SWA_HEADS, SWA_DIM) for g in range(len(SWA_CONFIGS))]
    new_conv = jnp.concatenate([conv_state[layer][:, 1:], raw3], axis=1)
    return y.reshape(Bs, 1, D), kvs, new_conv, s_new


def kernel(x_prompt, x_sample, cache_swa0_kv, cache_swa1_kv, cache_swa2_kv, state_dn_conv, state_dn_S, ln1_g, w_in,
           q_norm_g, k_norm_g, dn_conv_w, dn_a_log, dn_dt_bias, dn_norm_g, w_out_a, w_out_b, w_o, ln2_g, w_rg, b_rg,
           w_re, b_re, w_e_gate, w_e_up, w_e_down):
    yp, ys = x_prompt, x_sample
    outs = [[] for _ in range(10)]
    for l in range(w_in.shape[0]):
        p = _prep_layer(l, ln1_g, w_in, q_norm_g, k_norm_g, dn_conv_w, dn_a_log, dn_dt_bias, dn_norm_g, w_out_a,
                        w_out_b, w_o, ln2_g, w_rg, b_rg, w_re, b_re, w_e_gate, w_e_up, w_e_down)
        yp, pkv, pconv, ps = _layer_prompt(yp, p)
        ys, skv, sconv, ss = _layer_sample(ys, (cache_swa0_kv, cache_swa1_kv, cache_swa2_kv), state_dn_conv,
                                           state_dn_S, l, p)
        for lst, val in zip(outs, (*pkv, pconv, ps, *skv, sconv, ss)):
            lst.append(val)
    return (yp, ys, *(jnp.stack(o) for o in outs))
```

```python
import functools

import jax
import jax.numpy as jnp
from jax import lax
from jax.experimental import pallas as pl
from jax.experimental.pallas import tpu as pltpu

F32 = jnp.float32
BF16 = jnp.bfloat16
HI = lax.Precision.HIGHEST
EPS = 1e-6

SWA_CONFIGS = ((128, 1), (512, 4), (2048, 16))
SWA_HEADS = 8
SWA_DIM = 64
SWA_GW = SWA_HEADS * SWA_DIM
SWA_SPAN = 128
DN_HEADS = 8
DN_DK = 128
DN_CONV = 4
DN_CHUNK = 64
N_GROUPS = 4
PER_GROUP = 8
N_EXPERTS = N_GROUPS * PER_GROUP
TOP_K = 2

VMEM_LIMIT_BYTES = 56 * 1024 * 1024
LANES = 128
MXU = 256
MOE_ROWS = 512
MOE_ROWS_SMALL = 128
ROW_UNROLL = 8
ROUTER_ROWS = 512


def _cparams(n_axes):
    return pltpu.CompilerParams(
        dimension_semantics=("arbitrary",) * n_axes, vmem_limit_bytes=VMEM_LIMIT_BYTES
    )


def _rms(x, g):
    return x * lax.rsqrt(jnp.mean(x * x, axis=-1, keepdims=True) + EPS) * g


def _bdot(a, b):
    return jnp.dot(a.astype(BF16), b.astype(BF16), preferred_element_type=F32)


def _hdot(a, b):
    return jnp.dot(a, b, preferred_element_type=F32, precision=HI)


def _sigmoid(x):
    return 0.5 * jnp.tanh(0.5 * x) + 0.5


def _silu(x):
    half = 0.5 * x
    return half * jnp.tanh(half) + half


def _pack_bf16_pairs(x):
    w = x.shape[1] // 2
    lo = lax.bitcast_convert_type(x[:, :w].astype(BF16).astype(F32), jnp.uint32) >> 16
    hi = lax.bitcast_convert_type(x[:, w:].astype(BF16).astype(F32), jnp.uint32) & jnp.uint32(0xFFFF0000)
    return lo | hi


def _unpack_bf16_pairs(p):
    lo = lax.bitcast_convert_type(p << 16, F32)
    hi = lax.bitcast_convert_type(p & jnp.uint32(0xFFFF0000), F32)
    return jnp.concatenate([lo, hi], axis=-1)


def _softplus(x):
    return jnp.maximum(x, 0.0) + jnp.log1p(jnp.exp(-jnp.abs(x)))


def _norm_permute_kernel(x_ref, lng_ref, *refs, tm, dils):
    outs, h_scr = refs[:-1], refs[-1]
    h = _rms(x_ref[0], lng_ref[...])
    n_cb = h_scr.shape[0]
    for cb in range(n_cb):
        h_scr[cb] = h[:, cb * LANES:(cb + 1) * LANES]
    for o_ref, d in zip(outs, dils):
        for cb in range(n_cb):
            for r in range(d):
                src = h_scr[cb] if d == 1 else h_scr[cb, pl.ds(r, tm // d, stride=d), :]
                o_ref[0, r, :, cb * LANES:(cb + 1) * LANES] = src.astype(BF16)


def _norm_permute(x, ln_g, dils, *, tm):
    B, L, D = x.shape
    assert L % tm == 0 and all(tm % (16 * d) == 0 for d in dils)
    return pl.pallas_call(
        functools.partial(_norm_permute_kernel, tm=tm, dils=dils),
        grid=(B, L // tm),
        in_specs=[pl.BlockSpec((1, tm, D), lambda b, i: (b, i, 0)), pl.BlockSpec((1, D), lambda b, i: (0, 0))],
        out_specs=[pl.BlockSpec((1, d, tm // d, D), lambda b, i: (b, 0, i, 0)) for d in dils],
        out_shape=[jax.ShapeDtypeStruct((B, d, L // d, D), BF16) for d in dils],
        scratch_shapes=[pltpu.VMEM((D // LANES, tm, LANES), F32)],
        compiler_params=_cparams(2),
        name="norm_permute",
    )(x, ln_g)


def _proj_attn_kernel(h_ref, w_ref, ng_ref, bd_ref, p_ref, t_ref, *, n_tiles):
    nr, rows = h_ref.shape[1], h_ref.shape[2]
    z = jnp.dot(h_ref[0].reshape(nr * rows, -1), w_ref[...], preferred_element_type=F32)
    kv = []
    for c in range(0, 3 * SWA_GW, MXU):
        zc = z[:, c:c + MXU]
        if c < 2 * SWA_GW:
            ms = jnp.dot((zc * zc).astype(BF16), bd_ref[...], preferred_element_type=F32)
            zc = zc * lax.rsqrt(ms + EPS) * ng_ref[0, :, c:c + MXU]
        p_ref[0, :, :, c:c + MXU] = zc.astype(BF16).reshape(nr, rows, MXU)
        if c >= SWA_GW:
            kv.append(zc)

    @pl.when(pl.program_id(2) == n_tiles - 1)
    def _():
        per = SWA_GW // MXU
        for rr in range(nr):
            last = slice((rr + 1) * rows - SWA_SPAN, (rr + 1) * rows)
            for s in range(2):
                zr = jnp.concatenate([part[last, :] for part in kv[s * per:(s + 1) * per]], axis=-1)
                t_ref[0, :, rr, s] = zr.reshape(SWA_SPAN, SWA_HEADS, SWA_DIM)


def _proj_attn(hg, w_g, ng_g, bd, *, tmr, nr):
    B, dil, M, D = hg.shape
    assert M % tmr == 0 and tmr >= SWA_SPAN and dil % nr == 0
    nt = M // tmr
    W3 = 3 * SWA_GW
    keep = SWA_SPAN * dil
    p, t = pl.pallas_call(
        functools.partial(_proj_attn_kernel, n_tiles=nt),
        grid=(B, dil // nr, nt),
        in_specs=[
            pl.BlockSpec((1, nr, tmr, D), lambda b, r, i: (b, r, i, 0)),
            pl.BlockSpec((D, W3), lambda b, r, i: (0, 0)),
            pl.BlockSpec((1, 1, 2 * SWA_GW), lambda b, r, i: (0, 0, 0)),
            pl.BlockSpec((MXU, MXU), lambda b, r, i: (0, 0)),
        ],
        out_specs=[
            pl.BlockSpec((1, nr, tmr, W3), lambda b, r, i: (b, r, i, 0)),
            pl.BlockSpec((1, SWA_SPAN, nr, 2, SWA_HEADS, SWA_DIM), lambda b, r, i: (b, 0, r, 0, 0, 0)),
        ],
        out_shape=[
            jax.ShapeDtypeStruct((B, dil, M, W3), BF16),
            jax.ShapeDtypeStruct((B, SWA_SPAN, dil, 2, SWA_HEADS, SWA_DIM), F32),
        ],
        compiler_params=_cparams(3),
        name="proj_attn",
    )(hg, w_g, ng_g, bd)
    return p, t.reshape(B, keep, 2, SWA_HEADS, SWA_DIM)


def _proj_plain_kernel(x_ref, lng_ref, w_ref, o_ref, h_scr):
    @pl.when(pl.program_id(1) == 0)
    def _():
        h_scr[...] = _rms(x_ref[...], lng_ref[...]).astype(BF16)

    o_ref[...] = jnp.dot(h_scr[...], w_ref[...], preferred_element_type=F32).astype(o_ref.dtype)


def _proj_plain(x2d, ln_g, w, *, tm, tn, out_dtype, name="proj_plain"):
    N, D = x2d.shape
    C = w.shape[1]
    assert N % tm == 0 and C % tn == 0
    return pl.pallas_call(
        _proj_plain_kernel,
        grid=(N // tm, C // tn),
        in_specs=[
            pl.BlockSpec((tm, D), lambda i, j: (i, 0)),
            pl.BlockSpec((1, D), lambda i, j: (0, 0)),
            pl.BlockSpec((D, tn), lambda i, j: (0, j)),
        ],
        out_specs=pl.BlockSpec((tm, tn), lambda i, j: (i, j)),
        out_shape=jax.ShapeDtypeStruct((N, C), out_dtype),
        scratch_shapes=[pltpu.VMEM((tm, D), BF16)],
        compiler_params=_cparams(2),
        name=name,
    )(x2d, ln_g, w)


def _matmul_kernel(h_ref, w_ref, o_ref):
    o_ref[...] = jnp.dot(h_ref[...], w_ref[...], preferred_element_type=F32).astype(o_ref.dtype)


def _matmul(h2d, w, *, tm, tn, out_dtype, name):
    N, D = h2d.shape
    C = w.shape[1]
    assert N % tm == 0 and C % tn == 0
    return pl.pallas_call(
        _matmul_kernel,
        grid=(N // tm, C // tn),
        in_specs=[pl.BlockSpec((tm, D), lambda i, j: (i, 0)), pl.BlockSpec((D, tn), lambda i, j: (0, j))],
        out_specs=pl.BlockSpec((tm, tn), lambda i, j: (i, j)),
        out_shape=jax.ShapeDtypeStruct((N, C), out_dtype),
        compiler_params=_cparams(2),
        name=name,
    )(h2d, w)


def _attn_kernel(q_ref, kc_ref, vc_ref, kp_ref, vp_ref, o_ref, lse_ref, kk_scr, vv_scr, *, tq):
    i = pl.program_id(1)
    blk = SWA_SPAN
    kk_scr[0:blk, :] = kp_ref[0]
    kk_scr[blk:blk + tq, :] = kc_ref[0]
    vv_scr[0:blk, :] = vp_ref[0]
    vv_scr[blk:blk + tq, :] = vc_ref[0]
    qi = lax.broadcasted_iota(jnp.int32, (blk, 2 * blk), 0)
    ki = lax.broadcasted_iota(jnp.int32, (blk, 2 * blk), 1)
    dist = blk + qi - ki
    band = (dist >= 0) & (dist <= SWA_SPAN)
    band_first = band & ((ki >= blk) | (i > 0))
    lo = lax.broadcasted_iota(jnp.int32, (blk, LANES), 1) < SWA_DIM
    zero = jnp.zeros((blk, LANES), BF16)
    lane = lax.broadcasted_iota(jnp.int32, (blk, LANES), 1)
    for jb in range(tq // blk):
        mask = band_first if jb == 0 else band
        rows = slice(jb * blk, (jb + 1) * blk)
        lse_all = jnp.zeros((blk, LANES), F32)
        for hp in range(SWA_GW // LANES):
            cs = slice(hp * LANES, (hp + 1) * LANES)
            qb = q_ref[0, rows, cs]
            kk = kk_scr[jb * blk:(jb + 2) * blk, cs]
            vv = vv_scr[jb * blk:(jb + 2) * blk, cs]
            res_o = []
            for hh in range(2):
                qm = jnp.where(lo if hh == 0 else jnp.logical_not(lo), qb, zero)
                s = lax.dot_general(qm, kk, (((1,), (1,)), ((), ())), preferred_element_type=F32)
                s = jnp.where(mask, s, -jnp.inf)
                m = jnp.max(s, axis=-1, keepdims=True)
                p = jnp.exp(s - m)
                den = jnp.sum(p, axis=-1, keepdims=True)
                pv = jnp.dot(p.astype(BF16), vv, preferred_element_type=F32)
                res_o.append(pv / den)
                lse_all = jnp.where(lane == 2 * hp + hh, m + jnp.log(den), lse_all)
            o_ref[0, rows, cs] = jnp.where(lo, res_o[0], res_o[1]).astype(BF16)
        lse_ref[0, rows, :] = lse_all


def _attn(p, *, tq):
    S, M, _ = p.shape
    assert M % tq == 0 and tq % SWA_SPAN == 0
    nb = tq // SWA_SPAN
    return pl.pallas_call(
        functools.partial(_attn_kernel, tq=tq),
        grid=(S, M // tq),
        in_specs=[
            pl.BlockSpec((1, tq, SWA_GW), lambda s, i: (s, i, 0)),
            pl.BlockSpec((1, tq, SWA_GW), lambda s, i: (s, i, 1)),
            pl.BlockSpec((1, tq, SWA_GW), lambda s, i: (s, i, 2)),
            pl.BlockSpec((1, SWA_SPAN, SWA_GW), lambda s, i: (s, jnp.maximum(i * nb - 1, 0), 1)),
            pl.BlockSpec((1, SWA_SPAN, SWA_GW), lambda s, i: (s, jnp.maximum(i * nb - 1, 0), 2)),
        ],
        out_specs=[
            pl.BlockSpec((1, tq, SWA_GW), lambda s, i: (s, i, 0)),
            pl.BlockSpec((1, tq, LANES), lambda s, i: (s, i, 0)),
        ],
        out_shape=[
            jax.ShapeDtypeStruct((S, M, SWA_GW), BF16),
            jax.ShapeDtypeStruct((S, M, LANES), F32),
        ],
        scratch_shapes=[
            pltpu.VMEM((SWA_SPAN + tq, SWA_GW), BF16),
            pltpu.VMEM((SWA_SPAN + tq, SWA_GW), BF16),
        ],
        compiler_params=_cparams(2),
        name="swa_attn",
    )(p, p, p, p, p)


def _proj_dn_kernel(h_ref, w_ref, cw_ref, q_ref, k_ref, v_ref, tail_ref, z_scr, carry_scr, *, tm, n_ct):
    i = pl.program_id(1)
    j = pl.program_id(2)
    nh = DN_HEADS
    ncb = z_scr.shape[1] // LANES

    z_scr[0:8, :] = jnp.where(i == 0, 0.0, carry_scr[j])
    z_scr[8:8 + tm, :] = jnp.dot(h_ref[0], w_ref[...], preferred_element_type=F32)
    last = z_scr[tm:tm + 8, :]
    carry_scr[j] = last
    tn = z_scr.shape[1]
    outs = (q_ref, k_ref, v_ref)
    for jj in range(n_ct):

        @pl.when(j == jj)
        def _(jj=jj):
            tail_ref[0, :, jj * tn:(jj + 1) * tn] = last
            for cbl in range(ncb):
                cs = slice(cbl * LANES, (cbl + 1) * LANES)
                part, h = divmod(jj * ncb + cbl, nh)
                xe = z_scr[:, cs]
                acc = (0.5 * cw_ref[0:1, cs]) * xe
                for t in range(1, DN_CONV):
                    acc = (0.5 * cw_ref[t:t + 1, cs]) * xe + pltpu.roll(acc, 1, axis=0)
                half = acc[8:]
                act = half * jnp.tanh(half) + half
                if part < 2:
                    inv = lax.rsqrt(jnp.sum(act * act, axis=-1, keepdims=True) + EPS)
                    act = act * (inv * (DN_DK ** -0.5) if part == 0 else inv)
                outs[part][0, :, h * LANES:(h + 1) * LANES] = act.astype(BF16)


def _proj_dn(h, w_dn, conv_w, *, tm):
    B, L, D = h.shape
    C = w_dn.shape[1]
    width = DN_HEADS * DN_DK
    n_ct = 2
    tn = C // n_ct
    assert L % tm == 0 and C == 3 * width and tn % LANES == 0
    qkv = pl.BlockSpec((1, tm, width), lambda b, i, j: (b, i, 0))
    return pl.pallas_call(
        functools.partial(_proj_dn_kernel, tm=tm, n_ct=n_ct),
        grid=(B, L // tm, n_ct),
        in_specs=[
            pl.BlockSpec((1, tm, D), lambda b, i, j: (b, i, 0)),
            pl.BlockSpec((D, tn), lambda b, i, j: (0, j)),
            pl.BlockSpec((DN_CONV, tn), lambda b, i, j: (0, j)),
        ],
        out_specs=[qkv, qkv, qkv, pl.BlockSpec((1, 8, C), lambda b, i, j: (b, 0, 0))],
        out_shape=[jax.ShapeDtypeStruct((B, L, width), BF16)] * 3 + [jax.ShapeDtypeStruct((B, 8, C), F32)],
        scratch_shapes=[pltpu.VMEM((8 + tm, tn), F32), pltpu.VMEM((n_ct, 8, tn), F32)],
        compiler_params=_cparams(3),
        name="proj_dn",
    )(h, w_dn, conv_w)


def _gates_kernel(h_ref, w_ref, par_ref, g_ref, gt_ref, *, tl):
    nh = DN_HEADS
    z = jnp.dot(h_ref[0], w_ref[...], preferred_element_type=F32)
    lane = lax.broadcasted_iota(jnp.int32, (tl, LANES), 1)
    ri = lax.broadcasted_iota(jnp.int32, (tl, tl), 0)
    ci = lax.broadcasted_iota(jnp.int32, (tl, tl), 1)
    tri = jnp.where((ri // DN_CHUNK == ci // DN_CHUNK) & (ci <= ri), 1.0, 0.0).astype(BF16)
    for sb in range(z.shape[0] // tl):
        rows = slice(sb * tl, (sb + 1) * tl)
        ba = z[rows]
        g = par_ref[0:1, :] * _softplus(ba + par_ref[1:2, :])
        g_hi = g.astype(BF16)
        r1 = g - g_hi.astype(F32)
        g_mid = r1.astype(BF16)
        g_lo = (r1 - g_mid.astype(F32)).astype(BF16)
        gc = sum(jnp.dot(tri, piece, preferred_element_type=F32) for piece in (g_hi, g_mid, g_lo))
        g_ref[0, rows, :] = jnp.where(lane < nh, _sigmoid(ba), gc)
        gt_ref[0, :, rows] = jnp.transpose(gc)[nh:2 * nh, :]


def _dn_gates(h, w_ba, par, *, tm, tl):
    B, L, D = h.shape
    assert L % tm == 0 and tm % tl == 0 and tl % DN_CHUNK == 0
    return pl.pallas_call(
        functools.partial(_gates_kernel, tl=tl),
        grid=(B, L // tm),
        in_specs=[pl.BlockSpec((1, tm, D), lambda b, i: (b, i, 0)), pl.BlockSpec((D, LANES), lambda b, i: (0, 0)),
                  pl.BlockSpec((2, LANES), lambda b, i: (0, 0))],
        out_specs=[pl.BlockSpec((1, tm, LANES), lambda b, i: (b, i, 0)),
                   pl.BlockSpec((1, DN_HEADS, tm), lambda b, i: (b, 0, i))],
        out_shape=[jax.ShapeDtypeStruct((B, L, LANES), F32), jax.ShapeDtypeStruct((B, DN_HEADS, L), F32)],
        compiler_params=_cparams(2),
        name="dn_gates",
    )(h, w_ba, par)


def _intra_kernel(q_ref, k_ref, v_ref, g_ref, gt_in_ref, u_ref, w_ref, qd_ref, kd_ref, a_ref, gt_ref, *, tl):
    h = pl.program_id(1)
    C = DN_CHUNK
    lane = lax.broadcasted_iota(jnp.int32, (C, LANES), 1)
    ri = lax.broadcasted_iota(jnp.int32, (C, C), 0)
    ci = lax.broadcasted_iota(jnp.int32, (C, C), 1)
    eye = jnp.where(ri == ci, 1.0, 0.0).astype(F32)
    nt_dot = lambda a, b: lax.dot_general(a.astype(BF16), b.astype(BF16), (((1,), (1,)), ((), ())),
                                          preferred_element_type=F32)
    rows = [slice(c * C, (c + 1) * C) for c in range(tl // C)]
    gv = [g_ref[0, r, :] for r in rows]
    q = [q_ref[0, r, :].astype(F32) for r in rows]
    k = [k_ref[0, r, :].astype(F32) for r in rows]
    v = [v_ref[0, r, :].astype(F32) for r in rows]
    beta = [jnp.sum(jnp.where(lane == h, x, 0.0), axis=-1, keepdims=True) for x in gv]
    gc = [jnp.sum(jnp.where(lane == h + DN_HEADS, x, 0.0), axis=-1, keepdims=True) for x in gv]
    gc_row = gt_in_ref[0, pl.ds(h, 1), :]
    decay = [jnp.exp(jnp.where(ri >= ci, a - gc_row[:, r], -jnp.inf)) for a, r in zip(gc, rows)]
    kb = [a * b for a, b in zip(k, beta)]
    kq = [nt_dot(jnp.concatenate([a, b], axis=0), c) for a, b, c in zip(kb, q, k)]
    x = [-jnp.where(ri > ci, m[:C] * d, 0.0) for m, d in zip(kq, decay)]
    t = [eye + a for a in x]
    x = [_bdot(a, a) for a in x]
    for _ in range(4):
        both = [_bdot(jnp.concatenate([a, b], axis=0), a) for a, b in zip(x, t)]
        t = [b + m[C:] for b, m in zip(t, both)]
        x = [m[:C] for m in both]
    t = [b + _bdot(b, a) for a, b in zip(x, t)]
    eg = [jnp.exp(a) for a in gc]
    glast = [a[C - 1:C, :] for a in gc]
    uw = [_bdot(a, jnp.concatenate([b * c, d * e], axis=1)) for a, b, c, d, e in zip(t, v, beta, kb, eg)]
    for c, r in enumerate(rows):
        u_ref[0, 0, r, :] = uw[c][:, :LANES]
        w_ref[0, 0, r, :] = uw[c][:, LANES:].astype(BF16)
        a_ref[0, 0, r, :] = (kq[c][C:] * decay[c]).astype(BF16)
        qd_ref[0, 0, r, :] = (q[c] * eg[c]).astype(BF16)
        kd_ref[0, 0, r, :] = (k[c] * jnp.exp(glast[c] - gc[c])).astype(BF16)
        gt_ref[0, 0, c:c + 1, :] = jnp.broadcast_to(jnp.exp(glast[c]), (1, LANES))


def _dn_intra(q, k, v, g, gt_rows, *, tl):
    B, L, _ = q.shape
    H, C = DN_HEADS, DN_CHUNK
    assert L % tl == 0 and (tl // C) % 8 == 0
    qkv_spec = pl.BlockSpec((1, tl, LANES), lambda b, h, i: (b, i, h))
    hl = lambda w: pl.BlockSpec((1, 1, tl, w), lambda b, h, i: (b, h, i, 0))
    return pl.pallas_call(
        functools.partial(_intra_kernel, tl=tl),
        grid=(B, H, L // tl),
        in_specs=[qkv_spec, qkv_spec, qkv_spec, pl.BlockSpec((1, tl, LANES), lambda b, h, i: (b, i, 0)),
                  pl.BlockSpec((1, H, tl), lambda b, h, i: (b, 0, i))],
        out_specs=[hl(LANES), hl(LANES), hl(LANES), hl(LANES), hl(C),
                   pl.BlockSpec((1, 1, tl // C, LANES), lambda b, h, i: (b, h, i, 0))],
        out_shape=[
            jax.ShapeDtypeStruct((B, H, L, LANES), F32),
            jax.ShapeDtypeStruct((B, H, L, LANES), BF16),
            jax.ShapeDtypeStruct((B, H, L, LANES), BF16),
            jax.ShapeDtypeStruct((B, H, L, LANES), BF16),
            jax.ShapeDtypeStruct((B, H, L, C), BF16),
            jax.ShapeDtypeStruct((B, H, L // C, LANES), F32),
        ],
        compiler_params=_cparams(3),
        name="dn_intra",
    )(q, k, v, g, gt_rows)


def _scan_kernel(u_ref, w_ref, qd_ref, kd_ref, a_ref, gt_ref, s0_ref, o_ref, s_ref, *, n_chunks):
    C = DN_CHUNK
    bb, H = s_ref.shape[0], s_ref.shape[1]
    seqs = [(b, h) for b in range(bb) for h in range(H)]

    @pl.when(pl.program_id(1) == 0)
    def _():
        s_ref[...] = s0_ref[...]

    def body(c, carry):
        rows = pl.ds(pl.multiple_of(c * C, C), C)
        S = [s_ref[b, h] for b, h in seqs]
        Sb = [x.astype(BF16) for x in S]
        v_new = [u_ref[b, h, rows, :] - jnp.dot(w_ref[b, h, rows, :], sb, preferred_element_type=F32)
                 for (b, h), sb in zip(seqs, Sb)]
        vb = [x.astype(BF16) for x in v_new]
        o = [jnp.dot(qd_ref[b, h, rows, :], sb, preferred_element_type=F32)
             + jnp.dot(a_ref[b, h, rows, :], v, preferred_element_type=F32) for (b, h), sb, v in zip(seqs, Sb, vb)]
        upd = [lax.dot_general(kd_ref[b, h, rows, :], v, (((0,), (0,)), ((), ())), preferred_element_type=F32)
               for (b, h), v in zip(seqs, vb)]
        for n, (b, h) in enumerate(seqs):
            o_ref[b, rows, h * LANES:(h + 1) * LANES] = o[n]
            s_ref[b, h] = S[n] * gt_ref[b, h, pl.ds(c, 1), :] + upd[n]
        return carry

    lax.fori_loop(0, n_chunks, body, 0)


def _dn_scan(u, w, qd, kd, a, gt, s0, *, tl, bb):
    B, H, L, _ = u.shape
    C = DN_CHUNK
    assert L % tl == 0 and (tl // C) % 8 == 0 and B % bb == 0
    hs = lambda wd: pl.BlockSpec((bb, H, tl, wd), lambda b, i: (b, 0, i, 0))
    s_spec = pl.BlockSpec((bb, H, DN_DK, LANES), lambda b, i: (b, 0, 0, 0))
    return pl.pallas_call(
        functools.partial(_scan_kernel, n_chunks=tl // C),
        grid=(B // bb, L // tl),
        in_specs=[hs(LANES), hs(LANES), hs(LANES), hs(LANES), hs(C),
                  pl.BlockSpec((bb, H, tl // C, LANES), lambda b, i: (b, 0, i, 0)), s_spec],
        out_specs=[pl.BlockSpec((bb, tl, H * LANES), lambda b, i: (b, i, 0)), s_spec],
        out_shape=[jax.ShapeDtypeStruct((B, L, H * LANES), F32),
                   jax.ShapeDtypeStruct((B, H, DN_DK, LANES), F32)],
        compiler_params=_cparams(2),
        name="dn_scan",
    )(u, w, qd, kd, a, gt, s0)


def _gated_mix(o_a, od, gates, dng, wa, wb, wo, x, dot):
    width = DN_HEADS * DN_DK
    parts = []
    for h in range(DN_HEADS):
        blk = od[:, h * LANES:(h + 1) * LANES]
        parts.append(blk * lax.rsqrt(jnp.mean(blk * blk, axis=-1, keepdims=True) + EPS) * dng)
    odn = jnp.concatenate(parts, axis=-1) * _silu(gates[:, 0:width].astype(F32))
    ya = dot(o_a, wa)
    yb = dot(odn, wb)
    mix = _sigmoid(gates[:, width:2 * width].astype(F32)) * ya + _sigmoid(gates[:, 2 * width:].astype(F32)) * yb
    return x + dot(mix, wo)


def _out_kernel(x_ref, o0, o1, o2, l0, l1, l2, od_ref, gates_ref, dng_ref, wa_ref, wb_ref, wo_ref, e_ref, y_ref,
                so0, so1, so2, sl0, sl1, sl2, *, tm, dils):
    o_refs, l_refs = (o0, o1, o2), (l0, l1, l2)
    so, sl = (so0, so1, so2), (sl0, sl1, sl2)
    for gi, d in enumerate(dils):
        for r in range(d):
            dst = slice(None) if d == 1 else pl.ds(r, tm // d, stride=d)
            sl[gi][dst, :] = l_refs[gi][0, r]
            for cb in range(SWA_GW // LANES):
                so[gi][cb, dst, :] = o_refs[gi][0, r, :, cb * LANES:(cb + 1) * LANES].astype(F32)
    ls = [s[...] for s in sl]
    m = jnp.maximum(jnp.maximum(ls[0], ls[1]), ls[2])
    es = [jnp.exp(l - m) for l in ls]
    tot = es[0] + es[1] + es[2]
    alphas = [jnp.dot((e / tot).astype(BF16), e_ref[...], preferred_element_type=F32) for e in es]
    parts = []
    for cb in range(SWA_GW // LANES):
        cs = slice(cb * LANES, (cb + 1) * LANES)
        parts.append(alphas[0][:, cs] * so[0][cb] + alphas[1][:, cs] * so[1][cb] + alphas[2][:, cs] * so[2][cb])
    o_a = jnp.concatenate(parts, axis=-1)
    y_ref[...] = _gated_mix(o_a, od_ref[...], gates_ref[...], dng_ref[...], wa_ref[...], wb_ref[...],
                            wo_ref[...], x_ref[...], _bdot)


def _out_proj(x2d, os_, ls_, od2d, gates, dng, wa, wb, wo, e_att, *, B, L, tm):
    N, D = x2d.shape
    nt = L // tm
    dils = tuple(d for _, d in SWA_CONFIGS)
    grp = lambda d, w: pl.BlockSpec((1, d, tm // d, w), lambda i: (i // nt, 0, i % nt, 0))
    row = lambda w: pl.BlockSpec((tm, w), lambda i: (i, 0))
    full = lambda a: pl.BlockSpec(a.shape, lambda i: (0, 0))
    return pl.pallas_call(
        functools.partial(_out_kernel, tm=tm, dils=dils),
        grid=(N // tm,),
        in_specs=[row(D)] + [grp(d, SWA_GW) for d in dils] + [grp(d, LANES) for d in dils]
        + [row(od2d.shape[1]), row(gates.shape[1]), full(dng), full(wa), full(wb), full(wo), full(e_att)],
        out_specs=row(D),
        out_shape=jax.ShapeDtypeStruct((N, D), F32),
        scratch_shapes=[pltpu.VMEM((SWA_GW // LANES, tm, LANES), F32)] * 3 + [pltpu.VMEM((tm, LANES), F32)] * 3,
        compiler_params=_cparams(1),
        name="out_proj",
    )(x2d, *os_, *ls_, od2d, gates, dng, wa, wb, wo, e_att)


def _router_kernel(x_ref, lng_ref, wr_ref, br_ref, info_ref, cnt_ref, base_scr, *, tm):
    i = pl.program_id(0)

    @pl.when(i == 0)
    def _():
        base_scr[...] = jnp.zeros_like(base_scr)

    h = _rms(x_ref[...], lng_ref[...])
    lg = _bdot(h, wr_ref[...]) + br_ref[...]
    lane = lax.broadcasted_iota(jnp.int32, (tm, LANES), 1)
    big = jnp.int32(1 << 20)
    ninf = -jnp.inf

    def argmax_lane(vals):
        mx = jnp.max(vals, axis=-1, keepdims=True)
        idx = jnp.min(jnp.where(vals == mx, lane, big), axis=-1, keepdims=True)
        return mx, idx

    lgm = jnp.where(lane < N_GROUPS, lg, ninf)
    mg, gsel = argmax_lane(lgm)
    pg = 1.0 / jnp.sum(jnp.exp(lgm - mg), axis=-1, keepdims=True)
    start = N_GROUPS + gsel * PER_GROUP
    le = jnp.where((lane >= start) & (lane < start + PER_GROUP), lg, ninf)
    m1, i1 = argmax_lane(le)
    m2, i2 = argmax_lane(jnp.where(lane == i1, ninf, le))
    e21 = jnp.exp(m2 - m1)
    w1 = pg / (1.0 + e21)
    w2 = pg * e21 / (1.0 + e21)
    oh = jnp.where(lane == i1, 1.0, 0.0) + jnp.where(lane == i2, 1.0, 0.0)
    ri = lax.broadcasted_iota(jnp.int32, (tm, tm), 0)
    ci = lax.broadcasted_iota(jnp.int32, (tm, tm), 1)
    strict = jnp.where(ci < ri, 1.0, 0.0).astype(BF16)
    pref = jnp.dot(strict, oh.astype(BF16), preferred_element_type=F32) + base_scr[...]
    r1 = jnp.sum(jnp.where(lane == i1, pref, 0.0), axis=-1, keepdims=True)
    r2 = jnp.sum(jnp.where(lane == i2, pref, 0.0), axis=-1, keepdims=True)
    base_scr[...] = base_scr[...] + jnp.sum(oh, axis=0, keepdims=True)
    cnt_ref[...] = base_scr[...]
    off = jnp.float32(N_GROUPS)
    info = jnp.where(lane == 0, i1.astype(F32) - off, 0.0)
    info = jnp.where(lane == 1, i2.astype(F32) - off, info)
    info = jnp.where(lane == 2, w1, info)
    info = jnp.where(lane == 3, w2, info)
    info = jnp.where(lane == 4, r1, info)
    info = jnp.where(lane == 5, r2, info)
    info_ref[...] = info


def _router(x2d, ln_g, wr, br, *, tm):
    N, D = x2d.shape
    assert N % tm == 0
    return pl.pallas_call(
        functools.partial(_router_kernel, tm=tm),
        grid=(N // tm,),
        in_specs=[
            pl.BlockSpec((tm, D), lambda i: (i, 0)),
            pl.BlockSpec((1, D), lambda i: (0, 0)),
            pl.BlockSpec((D, LANES), lambda i: (0, 0)),
            pl.BlockSpec((1, LANES), lambda i: (0, 0)),
        ],
        out_specs=[pl.BlockSpec((tm, LANES), lambda i: (i, 0)), pl.BlockSpec((1, LANES), lambda i: (0, 0))],
        out_shape=[jax.ShapeDtypeStruct((N, LANES), F32), jax.ShapeDtypeStruct((1, LANES), F32)],
        scratch_shapes=[pltpu.VMEM((1, LANES), F32)],
        compiler_params=_cparams(1),
        name="router",
    )(x2d, ln_g, wr, br)


def _dispatch_kernel(dest_ref, zb_ref, x_ref, lng_ref, xs_ref, zero_scr, rows_scr, sem, *, tm, tb, n_zb, n_tiles):
    i = pl.program_id(0)

    @pl.when(i == 0)
    def _():
        zero_scr[...] = jnp.zeros_like(zero_scr)

        def zero_copy(n):
            return pltpu.make_async_copy(zero_scr, xs_ref.at[pl.ds(zb_ref[n] * tb, tb)], sem.at[2])

        def zero_issue(n, carry):
            @pl.when(zb_ref[n] >= 0)
            def _():
                zero_copy(n).start()

            return carry

        def zero_wait(n, carry):
            @pl.when(zb_ref[n] >= 0)
            def _():
                zero_copy(n).wait()

            return carry

        lax.fori_loop(0, n_zb, zero_issue, 0)
        lax.fori_loop(0, n_zb, zero_wait, 0)

    buf_now = lax.rem(i, 2)
    rows_scr[buf_now] = _pack_bf16_pairs(_rms(x_ref[...], lng_ref[...])).reshape(rows_scr.shape[1:])

    def row_copy(tile, t, slot):
        buf = lax.rem(tile, 2)
        return pltpu.make_async_copy(
            rows_scr.at[buf, pl.ds(t, 1)],
            xs_ref.at[pl.ds(dest_ref[(tile * tm + t) * TOP_K + slot], 1)], sem.at[buf])

    def issue(tt, carry):
        for r in range(ROW_UNROLL):
            for slot in range(TOP_K):
                row_copy(i, tt * ROW_UNROLL + r, slot).start(priority=slot)
        return carry

    def drain(tile):
        buf = lax.rem(tile, 2)
        for _ in range(TOP_K):
            pltpu.make_async_copy(rows_scr.at[buf], rows_scr.at[buf], sem.at[buf]).wait()

    lax.fori_loop(0, tm // ROW_UNROLL, issue, 0)

    @pl.when(i > 0)
    def _():
        drain(i - 1)

    @pl.when(i == n_tiles - 1)
    def _():
        drain(i)


def _dispatch(dest, zero_blocks, x2d, ln_g, *, tm, tb, n_rows):
    N, D = x2d.shape
    row = (D // 2 // LANES, LANES)
    return pl.pallas_call(
        functools.partial(_dispatch_kernel, tm=tm, tb=tb, n_zb=zero_blocks.shape[0], n_tiles=N // tm),
        grid_spec=pltpu.PrefetchScalarGridSpec(
            num_scalar_prefetch=2,
            grid=(N // tm,),
            in_specs=[pl.BlockSpec((tm, D), lambda i, d, z: (i, 0)), pl.BlockSpec((1, D), lambda i, d, z: (0, 0))],
            out_specs=pl.BlockSpec(memory_space=pl.ANY),
            scratch_shapes=[pltpu.VMEM((tb,) + row, jnp.uint32), pltpu.VMEM((2, tm) + row, jnp.uint32),
                            pltpu.SemaphoreType.DMA((3,))],
        ),
        out_shape=jax.ShapeDtypeStruct((n_rows,) + row, jnp.uint32),
        compiler_params=_cparams(1),
        name="moe_dispatch",
    )(dest, zero_blocks, x2d, ln_g)


def _ffn_kernel(be_ref, nb_ref, xs_ref, wg_ref, wu_ref, wd_ref, y_ref, wg_scr, wu_scr, wd_scr):
    i = pl.program_id(0)
    used = i < nb_ref[0]

    @pl.when(jnp.logical_or(i == 0, be_ref[i] != be_ref[jnp.maximum(i - 1, 0)]))
    def _():
        wg_scr[...] = wg_ref[0].astype(BF16)
        wu_scr[...] = wu_ref[0].astype(BF16)
        wd_scr[...] = wd_ref[0].astype(BF16)

    @pl.when(used)
    def _():
        h = _unpack_bf16_pairs(xs_ref[...].reshape(xs_ref.shape[0], -1)).astype(BF16)
        g = jnp.dot(h, wg_scr[...], preferred_element_type=F32)
        u = jnp.dot(h, wu_scr[...], preferred_element_type=F32)
        y = jnp.dot((_silu(g) * u).astype(BF16), wd_scr[...], preferred_element_type=F32)
        y_ref[...] = _pack_bf16_pairs(y).reshape(y_ref.shape)

    @pl.when(jnp.logical_not(used))
    def _():
        y_ref[...] = jnp.zeros_like(y_ref)


def _ffn(blk_e, nb_used, xs, wg, wu, wd, layer, *, tb):
    P, S, _ = xs.shape
    D = 2 * S * LANES
    nb = P // tb
    DE = wg.shape[3]
    return pl.pallas_call(
        _ffn_kernel,
        grid_spec=pltpu.PrefetchScalarGridSpec(
            num_scalar_prefetch=2,
            grid=(nb,),
            in_specs=[
                pl.BlockSpec((tb, S, LANES), lambda i, be, nbu: (jnp.minimum(i, nbu[0] - 1), 0, 0)),
                pl.BlockSpec((None, 1, D, DE), lambda i, be, nbu: (layer, be[i], 0, 0)),
                pl.BlockSpec((None, 1, D, DE), lambda i, be, nbu: (layer, be[i], 0, 0)),
                pl.BlockSpec((None, 1, DE, D), lambda i, be, nbu: (layer, be[i], 0, 0)),
            ],
            out_specs=pl.BlockSpec((tb, S, LANES), lambda i, be, nbu: (i, 0, 0)),
            scratch_shapes=[pltpu.VMEM((D, DE), BF16), pltpu.VMEM((D, DE), BF16), pltpu.VMEM((DE, D), BF16)],
        ),
        out_shape=jax.ShapeDtypeStruct((P, S, LANES), jnp.uint32),
        compiler_params=_cparams(1),
        name="moe_ffn",
    )(blk_e, nb_used, xs, wg, wu, wd)


def _combine_kernel(dest_ref, x_ref, info_ref, yb_ref, y_ref, g_scr, sem, *, tm, n_tiles):
    i = pl.program_id(0)

    def row_copy(tile, t, slot):
        buf = lax.rem(tile, 2)
        return pltpu.make_async_copy(
            yb_ref.at[pl.ds(dest_ref[(tile * tm + t) * TOP_K + slot], 1)],
            g_scr.at[buf, slot, pl.ds(t, 1)], sem.at[buf])

    def issue_tile(tile):
        def body(tt, carry):
            for r in range(ROW_UNROLL):
                for slot in range(TOP_K):
                    row_copy(tile, tt * ROW_UNROLL + r, slot).start(priority=slot)
            return carry

        lax.fori_loop(0, tm // ROW_UNROLL, body, 0)

    @pl.when(i == 0)
    def _():
        issue_tile(i)

    @pl.when(i + 1 < n_tiles)
    def _():
        issue_tile(i + 1)

    buf = lax.rem(i, 2)
    pltpu.make_async_copy(g_scr.at[buf], g_scr.at[buf], sem.at[buf]).wait()
    info = info_ref[...]
    lane = lax.broadcasted_iota(jnp.int32, info.shape, 1)
    w1 = jnp.sum(jnp.where(lane == 2, info, 0.0), axis=-1, keepdims=True)
    w2 = jnp.sum(jnp.where(lane == 3, info, 0.0), axis=-1, keepdims=True)
    g1 = _unpack_bf16_pairs(g_scr[buf, 0].reshape(tm, -1))
    g2 = _unpack_bf16_pairs(g_scr[buf, 1].reshape(tm, -1))
    y_ref[...] = x_ref[...] + (w1 * g1 + w2 * g2)


def _combine(dest, x2d, info, yb, *, tm):
    N, D = x2d.shape
    return pl.pallas_call(
        functools.partial(_combine_kernel, tm=tm, n_tiles=N // tm),
        grid_spec=pltpu.PrefetchScalarGridSpec(
            num_scalar_prefetch=1,
            grid=(N // tm,),
            in_specs=[
                pl.BlockSpec((tm, D), lambda i, d: (i, 0)),
                pl.BlockSpec((tm, LANES), lambda i, d: (i, 0)),
                pl.BlockSpec(memory_space=pl.ANY),
            ],
            out_specs=pl.BlockSpec((tm, D), lambda i, d: (i, 0)),
            scratch_shapes=[pltpu.VMEM((2, TOP_K, tm, D // 2 // LANES, LANES), jnp.uint32),
                            pltpu.SemaphoreType.DMA((2,))],
        ),
        out_shape=jax.ShapeDtypeStruct((N, D), F32),
        compiler_params=_cparams(1),
        name="moe_combine",
    )(dest, x2d, info, yb)


def _moe(x2d, ln2_g, wr, br, wg, wu, wd, layer, *, tm):
    N, D = x2d.shape
    tb = MOE_ROWS if N * TOP_K >= N_EXPERTS * MOE_ROWS else MOE_ROWS_SMALL
    info, counts = _router(x2d, ln2_g, wr, br, tm=ROUTER_ROWS if N % ROUTER_ROWS == 0 else tm)
    counts = counts[0, N_GROUPS:N_GROUPS + N_EXPERTS].astype(jnp.int32)
    pcounts = (counts + tb - 1) // tb * tb
    pend = jnp.cumsum(pcounts)
    pstart = pend - pcounts
    e = info[:, 0:TOP_K].astype(jnp.int32)
    rank = info[:, 4:4 + TOP_K].astype(jnp.int32)
    experts = jnp.arange(N_EXPERTS, dtype=jnp.int32)
    dest = (jnp.sum(jnp.where(e[..., None] == experts, pstart, 0), axis=-1) + rank).reshape(-1)
    nb = -(-(N * TOP_K) // tb) + N_EXPERTS
    P = nb * tb
    blocks = jnp.arange(nb, dtype=jnp.int32)
    blk_e = jnp.minimum(jnp.sum((pend[None, :] <= blocks[:, None] * tb).astype(jnp.int32), axis=1), N_EXPERTS - 1)
    nb_used = (pend[-1] // tb).astype(jnp.int32).reshape(1)
    zero_blocks = jnp.concatenate([jnp.where(counts % tb != 0, pend // tb - 1, -1),
                                   jnp.where(blocks >= nb_used[0], blocks, -1)]).astype(jnp.int32)
    xs = _dispatch(dest, zero_blocks, x2d, ln2_g, tm=tm, tb=tb, n_rows=P)
    yb = _ffn(blk_e, nb_used, xs, wg, wu, wd, layer, tb=tb)
    return _combine(dest, x2d, info, yb, tm=tm)


def _rows8(x):
    return jnp.broadcast_to(x, (8, x.shape[1]))


def _row_hdot(x, m):
    return _hdot(_rows8(x), m)[0:1]


def _bf_round(x):
    return x.astype(BF16).astype(F32)


def _sample_attn_kernel(z_ref, c0, c1, c2, qg_ref, kg_ref, oa_ref, kv_ref):
    W = SWA_GW
    scale = SWA_DIM ** -0.5
    z = z_ref[0]
    sub = lax.broadcasted_iota(jnp.int32, (SWA_HEADS, W), 0)
    lane = lax.broadcasted_iota(jnp.int32, (SWA_HEADS, W), 1)
    own = lane // SWA_DIM == sub

    def heads(row):
        return jnp.where(own, jnp.broadcast_to(row, (SWA_HEADS, W)), 0.0)

    def head_sum(row):
        return jnp.sum(heads(row), axis=-1, keepdims=True)

    def spread(col):
        return jnp.sum(jnp.where(own, col, 0.0), axis=0, keepdims=True)

    def headnorm(zz, g):
        return zz * spread(lax.rsqrt(head_sum(zz * zz) * (1.0 / SWA_DIM) + EPS)) * g

    outs, lses = [], []
    for gi, (c_ref, (win, dil)) in enumerate(zip((c0, c1, c2), SWA_CONFIGS)):
        q = headnorm(z[:, gi * W:(gi + 1) * W], qg_ref[gi:gi + 1, :])
        k = headnorm(z[:, 3 * W + gi * W:3 * W + (gi + 1) * W], kg_ref[gi:gi + 1, :])
        v = z[:, 6 * W + gi * W:6 * W + (gi + 1) * W]
        kv_ref[0, :, 2 * gi * W:(2 * gi + 1) * W] = k
        kv_ref[0, :, (2 * gi + 1) * W:(2 * gi + 2) * W] = v
        kc = c_ref[0].reshape(W, win).astype(BF16)
        vc = c_ref[1].reshape(W, win).astype(BF16)
        s_c = jnp.dot(heads(q).astype(BF16), kc, preferred_element_type=F32) * scale
        row = lax.broadcasted_iota(jnp.int32, s_c.shape, 1)
        s_c = jnp.where(row % dil == 0, s_c, -jnp.inf)
        s_n = head_sum(_bf_round(k) * _bf_round(q)) * scale
        m = jnp.maximum(jnp.max(s_c, axis=-1, keepdims=True), s_n)
        p_c = jnp.exp(s_c - m)
        p_n = jnp.exp(s_n - m)
        den = jnp.sum(p_c, axis=-1, keepdims=True) + p_n
        pv = lax.dot_general(p_c.astype(BF16), vc, (((1,), (1,)), ((), ())), preferred_element_type=F32)
        num = jnp.sum(jnp.where(own, pv, 0.0), axis=0, keepdims=True) + spread(_bf_round(p_n)) * _bf_round(v)
        outs.append(num / spread(den))
        lses.append(m + jnp.log(den))
    mm = jnp.maximum(jnp.maximum(lses[0], lses[1]), lses[2])
    es = [jnp.exp(l - mm) for l in lses]
    tot = es[0] + es[1] + es[2]
    oa_ref[0] = sum(spread(_bf_round(e / tot)) * _bf_round(o) for e, o in zip(es, outs))


def _sample_attn(z3, caches, layer, qg, kg):
    Bs = z3.shape[0]
    W = SWA_GW
    cviews, cspecs = [], []
    for (win, dil), c in zip(SWA_CONFIGS, caches):
        assert c.shape[2] == win
        cviews.append(jnp.transpose(c, (0, 1, 3, 4, 5, 2)))
        cspecs.append(pl.BlockSpec((None, None, 2, SWA_HEADS, SWA_DIM, win), lambda b: (layer, b, 0, 0, 0, 0)))
    full = lambda a: pl.BlockSpec(a.shape, lambda b: (0,) * a.ndim)
    return pl.pallas_call(
        _sample_attn_kernel,
        grid=(Bs,),
        in_specs=[pl.BlockSpec((1, 1, 9 * W), lambda b: (b, 0, 0))] + cspecs + [full(qg), full(kg)],
        out_specs=[pl.BlockSpec((1, 1, W), lambda b: (b, 0, 0)), pl.BlockSpec((1, 1, 6 * W), lambda b: (b, 0, 0))],
        out_shape=[jax.ShapeDtypeStruct((Bs, 1, W), F32), jax.ShapeDtypeStruct((Bs, 1, 6 * W), F32)],
        compiler_params=_cparams(1),
        name="sample_attn",
    )(z3, *cviews, qg, kg)


def _sample_dn_kernel(raw_ref, cs_ref, cw_ref, ba_ref, par_ref, s_ref, e_ref, etb_ref, etg_ref, o_ref, so_ref):
    E, ETB, ETG = e_ref[...], etb_ref[...], etg_ref[...]
    width = DN_HEADS * DN_DK
    conv = cw_ref[DN_CONV - 1:DN_CONV, :] * raw_ref[0]
    for t in range(DN_CONV - 1):
        conv = conv + cw_ref[t:t + 1, :] * cs_ref[0, t:t + 1, :]
    act = _silu(conv)

    def l2(zz):
        return zz * _row_hdot(lax.rsqrt(_row_hdot(zz * zz, E) + EPS), ETB)

    qn = l2(act[:, 0:width]) * (DN_DK ** -0.5)
    kn = l2(act[:, width:2 * width])
    vn = act[:, 2 * width:3 * width]
    ba = ba_ref[0]
    beta = _row_hdot(_sigmoid(ba), ETB)
    eg = jnp.exp(_row_hdot(par_ref[0:1, :] * _softplus(ba + par_ref[1:2, :]), ETG))
    row0 = lax.broadcasted_iota(jnp.int32, (8, LANES), 0) == 0
    for h in range(DN_HEADS):
        sl = slice(h * LANES, (h + 1) * LANES)
        S = s_ref[0, h]
        q, k, v, b, e = qn[:, sl], kn[:, sl], vn[:, sl], beta[:, sl], eg[:, sl]
        Sb = S.astype(BF16)
        wq = jnp.concatenate([k * b * e, q * e, jnp.zeros((6, LANES), F32)], axis=0)
        both = jnp.dot(wq.astype(BF16), Sb, preferred_element_type=F32)
        v_new = v * b - both[0:1]
        a = jnp.sum(q * k, axis=-1, keepdims=True)
        o_ref[0, :, sl] = both[1:2] + a * v_new
        k8 = jnp.where(row0, _rows8(k), 0.0)
        upd = lax.dot_general(k8, _rows8(v_new), (((0,), (0,)), ((), ())), preferred_element_type=F32, precision=HI)
        so_ref[0, h] = S * e + upd


def _sample_dn(raw3, conv_state, s0, layer, conv_w, ba3, par, e_mat, etb, etg):
    Bs, _, C = raw3.shape
    H = DN_HEADS
    full = lambda a: pl.BlockSpec(a.shape, lambda b: (0,) * a.ndim)
    return pl.pallas_call(
        _sample_dn_kernel,
        grid=(Bs,),
        in_specs=[pl.BlockSpec((1, 1, C), lambda b: (b, 0, 0)),
                  pl.BlockSpec((None, 1, DN_CONV - 1, C), lambda b: (layer, b, 0, 0)),
                  full(conv_w),
                  pl.BlockSpec((1, 1, LANES), lambda b: (b, 0, 0)),
                  full(par),
                  pl.BlockSpec((None, 1, H, DN_DK, LANES), lambda b: (layer, b, 0, 0, 0)),
                  full(e_mat), full(etb), full(etg)],
        out_specs=[pl.BlockSpec((1, 1, H * LANES), lambda b: (b, 0, 0)),
                   pl.BlockSpec((1, H, DN_DK, LANES), lambda b: (b, 0, 0, 0))],
        out_shape=[jax.ShapeDtypeStruct((Bs, 1, H * LANES), F32), jax.ShapeDtypeStruct(s0.shape[1:], F32)],
        compiler_params=_cparams(1),
        name="sample_dn",
    )(raw3, conv_state, conv_w, ba3, par, s0, e_mat, etb, etg)


def _sample_out_kernel(x_ref, oa_ref, od_ref, gates_ref, dng_ref, wa_ref, wb_ref, wo_ref, y_ref):
    y_ref[...] = _gated_mix(oa_ref[...], od_ref[...], gates_ref[...], dng_ref[...], wa_ref[...], wb_ref[...],
                            wo_ref[...], x_ref[...], _bdot)


def _sample_out(x2d, oa, od, gates, dng, wa, wb, wo):
    args = (x2d, oa, od, gates, dng, wa, wb, wo)
    return pl.pallas_call(
        _sample_out_kernel,
        grid=(1,),
        in_specs=[pl.BlockSpec(a.shape, lambda i: (0, 0)) for a in args],
        out_specs=pl.BlockSpec(x2d.shape, lambda i: (0, 0)),
        out_shape=jax.ShapeDtypeStruct(x2d.shape, F32),
        compiler_params=_cparams(1),
        name="sample_out",
    )(*args)


def _head_indicator(width, head):
    c = jnp.arange(width)[:, None] // head
    return (c == jnp.arange(LANES)[None, :]).astype(F32)


def _prep_layer(l, ln1_g, w_in, q_norm_g, k_norm_g, dn_conv_w, dn_a_log, dn_dt_bias, dn_norm_g, w_out_a, w_out_b,
                w_o, ln2_g, w_rg, b_rg, w_re, b_re, w_e_gate, w_e_up, w_e_down):
    D = w_in.shape[1]
    a_w = 3 * 3 * SWA_GW
    dn_w = DN_HEADS * 3 * DN_DK
    hv = DN_HEADS * DN_DK
    w = w_in[l]
    splits = dict(att=w[:, :a_w], dn=w[:, a_w:a_w + dn_w],
                  ba=jnp.pad(w[:, a_w + dn_w:a_w + dn_w + 2 * DN_HEADS], ((0, 0), (0, LANES - 2 * DN_HEADS))),
                  gate=w[:, a_w + dn_w + 2 * DN_HEADS:])
    assert splits["gate"].shape[1] == hv + 2 * D
    tile_heads = lambda g: jnp.broadcast_to(g[:, None, :], (len(SWA_CONFIGS), SWA_HEADS, SWA_DIM)).reshape(len(SWA_CONFIGS), SWA_GW)
    qg, kg = tile_heads(q_norm_g[l]), tile_heads(k_norm_g[l])
    idx = jnp.arange(MXU) // SWA_DIM
    n_g = len(SWA_CONFIGS)
    par = jnp.zeros((2, LANES), F32)
    par = par.at[0, DN_HEADS:2 * DN_HEADS].set(-jnp.exp(dn_a_log[l].astype(F32)))
    par = par.at[1, DN_HEADS:2 * DN_HEADS].set(dn_dt_bias[l].astype(F32))
    wr = jnp.pad(jnp.concatenate([w_rg[l], w_re[l]], axis=1), ((0, 0), (0, LANES - N_GROUPS - N_EXPERTS)))
    br = jnp.pad(jnp.concatenate([b_rg[l], b_re[l]]), (0, LANES - N_GROUPS - N_EXPERTS)).reshape(1, LANES)
    e8 = _head_indicator(hv, DN_DK)
    return dict(
        bf16={k: v.astype(BF16) for k, v in splits.items()},
        ln1=ln1_g[l].reshape(1, D), ln2=ln2_g[l].reshape(1, D),
        qg=qg, kg=kg,
        w_grp=[jnp.concatenate([w[:, s * n_g * SWA_GW + g * SWA_GW:s * n_g * SWA_GW + (g + 1) * SWA_GW]
                                for s in range(3)], axis=1).astype(BF16) for g in range(n_g)],
        ng_grp=[jnp.concatenate([qg[g] * SWA_DIM ** -0.5, kg[g]]).reshape(1, 1, 2 * SWA_GW) for g in range(n_g)],
        bd=((idx[:, None] == idx[None, :]).astype(F32) / SWA_DIM).astype(BF16),
        conv_w=dn_conv_w[l], par=par, dng=dn_norm_g[l].reshape(1, DN_DK),
        wa=w_out_a[l].astype(BF16), wb=w_out_b[l].astype(BF16), wo=w_o[l].astype(BF16), wr=wr.astype(BF16), br=br,
        wg=w_e_gate, wu=w_e_up, wd=w_e_down, layer=l,
        e_dn=e8, etb=e8.T, etg=jnp.roll(e8, DN_HEADS, axis=1).T,
        e_att=_head_indicator(SWA_GW, SWA_DIM).T.astype(BF16),
    )


def _layer_prompt(x, p):
    B, L, D = x.shape
    N = B * L
    x2d = x.reshape(N, D)
    bw = p["bf16"]
    pks, tails = [], []
    hgs = _norm_permute(x, p["ln1"], tuple(d for _, d in SWA_CONFIGS), tm=min(512, L))
    for g, (win, dil) in enumerate(SWA_CONFIGS):
        assert L >= win
        tmr = min(1024, L // dil)
        pk, tail = _proj_attn(hgs[g], p["w_grp"][g], p["ng_grp"][g], p["bd"], tmr=tmr, nr=min(dil, max(1, 1024 // tmr)))
        pks.append(pk)
        tails.append(tail)
    tmp = min(1024, N)
    h0 = hgs[0].reshape(B, L, D)
    qd, kd, vd, raw_tail = _proj_dn(h0, bw["dn"], p["conv_w"], tm=min(512, L))
    gates = _matmul(h0.reshape(N, D), bw["gate"], tm=tmp, tn=bw["gate"].shape[1], out_dtype=BF16, name="proj_gate")
    os_, ls_ = [], []
    for pk in pks:
        d, M = pk.shape[1], pk.shape[2]
        o, lse = _attn(pk.reshape(B * d, M, pk.shape[3]), tq=min(512, M))
        os_.append(o.reshape(B, d, M, SWA_GW))
        ls_.append(lse.reshape(B, d, M, LANES))
    gb, gt_rows = _dn_gates(h0, bw["ba"], p["par"], tm=min(1024, L), tl=min(256, L))
    u, w, qdec, kdec, a, gt = _dn_intra(qd, kd, vd, gb, gt_rows, tl=min(2048, L))
    od, s_new = _dn_scan(u, w, qdec, kdec, a, gt, jnp.zeros((B, DN_HEADS, DN_DK, LANES), F32), tl=min(512, L),
                         bb=2 if B % 2 == 0 else 1)
    x2 = _out_proj(x2d, os_, ls_, od.reshape(N, -1), gates, p["dng"], p["wa"], p["wb"], p["wo"], p["e_att"],
                   B=B, L=L, tm=min(512, L))
    y = _moe(x2, p["ln2"], p["wr"], p["br"], p["wg"], p["wu"], p["wd"], p["layer"], tm=512)
    return y.reshape(B, L, D), tails, raw_tail[:, 8 - (DN_CONV - 1):], s_new


def _layer_sample(x, caches, conv_state, s0, layer, p):
    Bs, T, D = x.shape
    assert T == 1
    x2d = x.reshape(Bs, D)
    bw = p["bf16"]
    proj = functools.partial(_proj_plain, x2d, p["ln1"], tm=Bs, out_dtype=F32)
    z_att = proj(bw["att"], tn=1536, name="sproj_att")
    raw = proj(bw["dn"], tn=1536, name="sproj_dn")
    gates = proj(bw["gate"], tn=1536, name="sproj_gate")
    ba = proj(bw["ba"], tn=LANES, name="sproj_ba")
    oa, kv = _sample_attn(z_att.reshape(Bs, 1, -1), caches, layer, p["qg"], p["kg"])
    raw3 = raw.reshape(Bs, 1, -1)
    od, s_new = _sample_dn(raw3, conv_state, s0, layer, p["conv_w"], ba.reshape(Bs, 1, LANES), p["par"],
                           p["e_dn"], p["etb"], p["etg"])
    x2 = _sample_out(x2d, oa.reshape(Bs, -1), od.reshape(Bs, -1), gates, p["dng"], p["wa"], p["wb"], p["wo"])
    y = _moe(x2, p["ln2"], p["wr"], p["br"], p["wg"], p["wu"], p["wd"], p["layer"], tm=Bs)
    W2 = 2 * SWA_GW
    kvs = [kv[:, :, g * W2:(g + 1) * W2].reshape(Bs, 1, 2, SWA_HEADS, SWA_DIM) for g in range(len(SWA_CONFIGS))]
    new_conv = jnp.concatenate([conv_state[layer][:, 1:], raw3], axis=1)
    return y.reshape(Bs, 1, D), kvs, new_conv, s_new


def kernel(x_prompt, x_sample, cache_swa0_kv, cache_swa1_kv, cache_swa2_kv, state_dn_conv, state_dn_S, ln1_g, w_in,
           q_norm_g, k_norm_g, dn_conv_w, dn_a_log, dn_dt_bias, dn_norm_g, w_out_a, w_out_b, w_o, ln2_g, w_rg, b_rg,
           w_re, b_re, w_e_gate, w_e_up, w_e_down):
    yp, ys = x_prompt, x_sample
    outs = [[] for _ in range(10)]
    for l in range(w_in.shape[0]):
        p = _prep_layer(l, ln1_g, w_in, q_norm_g, k_norm_g, dn_conv_w, dn_a_log, dn_dt_bias, dn_norm_g, w_out_a,
                        w_out_b, w_o, ln2_g, w_rg, b_rg, w_re, b_re, w_e_gate, w_e_up, w_e_down)
        yp, pkv, pconv, ps = _layer_prompt(yp, p)
        ys, skv, sconv, ss = _layer_sample(ys, (cache_swa0_kv, cache_swa1_kv, cache_swa2_kv), state_dn_conv,
                                           state_dn_S, l, p)
        for lst, val in zip(outs, (*pkv, pconv, ps, *skv, sconv, ss)):
            lst.append(val)
    return (yp, ys, *(jnp.stack(o) for o in outs))
```

```python
import functools

import jax
import jax.numpy as jnp
from jax import lax
from jax.experimental import pallas as pl
from jax.experimental.pallas import tpu as pltpu

F32 = jnp.float32
BF16 = jnp.bfloat16
HI = lax.Precision.HIGHEST
EPS = 1e-6

SWA_CONFIGS = ((128, 1), (512, 4), (2048, 16))
SWA_HEADS = 8
SWA_DIM = 64
SWA_GW = SWA_HEADS * SWA_DIM
SWA_SPAN = 128
DN_HEADS = 8
DN_DK = 128
DN_CONV = 4
DN_CHUNK = 64
N_GROUPS = 4
PER_GROUP = 8
N_EXPERTS = N_GROUPS * PER_GROUP
TOP_K = 2

VMEM_LIMIT_BYTES = 56 * 1024 * 1024
LANES = 128
MXU = 256
MOE_ROWS = 512
MOE_ROWS_SMALL = 128
ROW_UNROLL = 8
ROUTER_ROWS = 512


def _cparams(n_axes):
    return pltpu.CompilerParams(
        dimension_semantics=("arbitrary",) * n_axes, vmem_limit_bytes=VMEM_LIMIT_BYTES
    )


def _rms(x, g):
    return x * lax.rsqrt(jnp.mean(x * x, axis=-1, keepdims=True) + EPS) * g


def _bdot(a, b):
    return jnp.dot(a.astype(BF16), b.astype(BF16), preferred_element_type=F32)


def _hdot(a, b):
    return jnp.dot(a, b, preferred_element_type=F32, precision=HI)


def _sigmoid(x):
    return 0.5 * jnp.tanh(0.5 * x) + 0.5


def _silu(x):
    half = 0.5 * x
    return half * jnp.tanh(half) + half


def _pack_bf16_pairs(x):
    w = x.shape[1] // 2
    lo = lax.bitcast_convert_type(x[:, :w].astype(BF16).astype(F32), jnp.uint32) >> 16
    hi = lax.bitcast_convert_type(x[:, w:].astype(BF16).astype(F32), jnp.uint32) & jnp.uint32(0xFFFF0000)
    return lo | hi


def _unpack_bf16_pairs(p):
    lo = lax.bitcast_convert_type(p << 16, F32)
    hi = lax.bitcast_convert_type(p & jnp.uint32(0xFFFF0000), F32)
    return jnp.concatenate([lo, hi], axis=-1)


def _softplus(x):
    return jnp.maximum(x, 0.0) + jnp.log1p(jnp.exp(-jnp.abs(x)))


def _norm_permute_kernel(x_ref, lng_ref, *refs, tm, dils):
    outs, h_scr = refs[:-1], refs[-1]
    h = _rms(x_ref[0], lng_ref[...])
    n_cb = h_scr.shape[0]
    for cb in range(n_cb):
        h_scr[cb] = h[:, cb * LANES:(cb + 1) * LANES]
    for o_ref, d in zip(outs, dils):
        for cb in range(n_cb):
            for r in range(d):
                src = h_scr[cb] if d == 1 else h_scr[cb, pl.ds(r, tm // d, stride=d), :]
                o_ref[0, r, :, cb * LANES:(cb + 1) * LANES] = src.astype(BF16)


def _norm_permute(x, ln_g, dils, *, tm):
    B, L, D = x.shape
    assert L % tm == 0 and all(tm % (16 * d) == 0 for d in dils)
    return pl.pallas_call(
        functools.partial(_norm_permute_kernel, tm=tm, dils=dils),
        grid=(B, L // tm),
        in_specs=[pl.BlockSpec((1, tm, D), lambda b, i: (b, i, 0)), pl.BlockSpec((1, D), lambda b, i: (0, 0))],
        out_specs=[pl.BlockSpec((1, d, tm // d, D), lambda b, i: (b, 0, i, 0)) for d in dils],
        out_shape=[jax.ShapeDtypeStruct((B, d, L // d, D), BF16) for d in dils],
        scratch_shapes=[pltpu.VMEM((D // LANES, tm, LANES), F32)],
        compiler_params=_cparams(2),
        name="norm_permute",
    )(x, ln_g)


def _proj_attn_kernel(h_ref, w_ref, ng_ref, bd_ref, p_ref, t_ref, *, n_tiles):
    nr, rows = h_ref.shape[1], h_ref.shape[2]
    z = jnp.dot(h_ref[0].reshape(nr * rows, -1), w_ref[...], preferred_element_type=F32)
    kv = []
    for c in range(0, 3 * SWA_GW, MXU):
        zc = z[:, c:c + MXU]
        if c < 2 * SWA_GW:
            ms = jnp.dot((zc * zc).astype(BF16), bd_ref[...], preferred_element_type=F32)
            zc = zc * lax.rsqrt(ms + EPS) * ng_ref[0, :, c:c + MXU]
        p_ref[0, :, :, c:c + MXU] = zc.astype(BF16).reshape(nr, rows, MXU)
        if c >= SWA_GW:
            kv.append(zc)

    @pl.when(pl.program_id(2) == n_tiles - 1)
    def _():
        per = SWA_GW // MXU
        for rr in range(nr):
            last = slice((rr + 1) * rows - SWA_SPAN, (rr + 1) * rows)
            for s in range(2):
                zr = jnp.concatenate([part[last, :] for part in kv[s * per:(s + 1) * per]], axis=-1)
                t_ref[0, :, rr, s] = zr.reshape(SWA_SPAN, SWA_HEADS, SWA_DIM)


def _proj_attn(hg, w_g, ng_g, bd, *, tmr, nr):
    B, dil, M, D = hg.shape
    assert M % tmr == 0 and tmr >= SWA_SPAN and dil % nr == 0
    nt = M // tmr
    W3 = 3 * SWA_GW
    keep = SWA_SPAN * dil
    p, t = pl.pallas_call(
        functools.partial(_proj_attn_kernel, n_tiles=nt),
        grid=(B, dil // nr, nt),
        in_specs=[
            pl.BlockSpec((1, nr, tmr, D), lambda b, r, i: (b, r, i, 0)),
            pl.BlockSpec((D, W3), lambda b, r, i: (0, 0)),
            pl.BlockSpec((1, 1, 2 * SWA_GW), lambda b, r, i: (0, 0, 0)),
            pl.BlockSpec((MXU, MXU), lambda b, r, i: (0, 0)),
        ],
        out_specs=[
            pl.BlockSpec((1, nr, tmr, W3), lambda b, r, i: (b, r, i, 0)),
            pl.BlockSpec((1, SWA_SPAN, nr, 2, SWA_HEADS, SWA_DIM), lambda b, r, i: (b, 0, r, 0, 0, 0)),
        ],
        out_shape=[
            jax.ShapeDtypeStruct((B, dil, M, W3), BF16),
            jax.ShapeDtypeStruct((B, SWA_SPAN, dil, 2, SWA_HEADS, SWA_DIM), F32),
        ],
        compiler_params=_cparams(3),
        name="proj_attn",
    )(hg, w_g, ng_g, bd)
    return p, t.reshape(B, keep, 2, SWA_HEADS, SWA_DIM)


def _proj_plain_kernel(x_ref, lng_ref, w_ref, o_ref, h_scr):
    @pl.when(pl.program_id(1) == 0)
    def _():
        h_scr[...] = _rms(x_ref[...], lng_ref[...]).astype(BF16)

    o_ref[...] = jnp.dot(h_scr[...], w_ref[...], preferred_element_type=F32).astype(o_ref.dtype)


def _proj_plain(x2d, ln_g, w, *, tm, tn, out_dtype, name="proj_plain"):
    N, D = x2d.shape
    C = w.shape[1]
    assert N % tm == 0 and C % tn == 0
    return pl.pallas_call(
        _proj_plain_kernel,
        grid=(N // tm, C // tn),
        in_specs=[
            pl.BlockSpec((tm, D), lambda i, j: (i, 0)),
            pl.BlockSpec((1, D), lambda i, j: (0, 0)),
            pl.BlockSpec((D, tn), lambda i, j: (0, j)),
        ],
        out_specs=pl.BlockSpec((tm, tn), lambda i, j: (i, j)),
        out_shape=jax.ShapeDtypeStruct((N, C), out_dtype),
        scratch_shapes=[pltpu.VMEM((tm, D), BF16)],
        compiler_params=_cparams(2),
        name=name,
    )(x2d, ln_g, w)


def _matmul_kernel(h_ref, w_ref, o_ref):
    o_ref[...] = jnp.dot(h_ref[...], w_ref[...], preferred_element_type=F32).astype(o_ref.dtype)


def _matmul(h2d, w, *, tm, tn, out_dtype, name):
    N, D = h2d.shape
    C = w.shape[1]
    assert N % tm == 0 and C % tn == 0
    return pl.pallas_call(
        _matmul_kernel,
        grid=(N // tm, C // tn),
        in_specs=[pl.BlockSpec((tm, D), lambda i, j: (i, 0)), pl.BlockSpec((D, tn), lambda i, j: (0, j))],
        out_specs=pl.BlockSpec((tm, tn), lambda i, j: (i, j)),
        out_shape=jax.ShapeDtypeStruct((N, C), out_dtype),
        compiler_params=_cparams(2),
        name=name,
    )(h2d, w)


def _attn_kernel(q_ref, kc_ref, vc_ref, kp_ref, vp_ref, o_ref, lse_ref, kk_scr, vv_scr, *, tq):
    i = pl.program_id(1)
    blk = SWA_SPAN
    kk_scr[0:blk, :] = kp_ref[0]
    kk_scr[blk:blk + tq, :] = kc_ref[0]
    vv_scr[0:blk, :] = vp_ref[0]
    vv_scr[blk:blk + tq, :] = vc_ref[0]
    qi = lax.broadcasted_iota(jnp.int32, (blk, 2 * blk), 0)
    ki = lax.broadcasted_iota(jnp.int32, (blk, 2 * blk), 1)
    dist = blk + qi - ki
    band = (dist >= 0) & (dist <= SWA_SPAN)
    band_first = band & ((ki >= blk) | (i > 0))
    lo = lax.broadcasted_iota(jnp.int32, (blk, LANES), 1) < SWA_DIM
    zero = jnp.zeros((blk, LANES), BF16)
    lane = lax.broadcasted_iota(jnp.int32, (blk, LANES), 1)
    for jb in range(tq // blk):
        mask = band_first if jb == 0 else band
        rows = slice(jb * blk, (jb + 1) * blk)
        lse_all = jnp.zeros((blk, LANES), F32)
        for hp in range(SWA_GW // LANES):
            cs = slice(hp * LANES, (hp + 1) * LANES)
            qb = q_ref[0, rows, cs]
            kk = kk_scr[jb * blk:(jb + 2) * blk, cs]
            vv = vv_scr[jb * blk:(jb + 2) * blk, cs]
            res_o = []
            for hh in range(2):
                qm = jnp.where(lo if hh == 0 else jnp.logical_not(lo), qb, zero)
                s = lax.dot_general(qm, kk, (((1,), (1,)), ((), ())), preferred_element_type=F32)
                s = jnp.where(mask, s, -jnp.inf)
                m = jnp.max(s, axis=-1, keepdims=True)
                p = jnp.exp(s - m)
                den = jnp.sum(p, axis=-1, keepdims=True)
                pv = jnp.dot(p.astype(BF16), vv, preferred_element_type=F32)
                res_o.append(pv / den)
                lse_all = jnp.where(lane == 2 * hp + hh, m + jnp.log(den), lse_all)
            o_ref[0, rows, cs] = jnp.where(lo, res_o[0], res_o[1]).astype(BF16)
        lse_ref[0, rows, :] = lse_all


def _attn(p, *, tq):
    S, M, _ = p.shape
    assert M % tq == 0 and tq % SWA_SPAN == 0
    nb = tq // SWA_SPAN
    return pl.pallas_call(
        functools.partial(_attn_kernel, tq=tq),
        grid=(S, M // tq),
        in_specs=[
            pl.BlockSpec((1, tq, SWA_GW), lambda s, i: (s, i, 0)),
            pl.BlockSpec((1, tq, SWA_GW), lambda s, i: (s, i, 1)),
            pl.BlockSpec((1, tq, SWA_GW), lambda s, i: (s, i, 2)),
            pl.BlockSpec((1, SWA_SPAN, SWA_GW), lambda s, i: (s, jnp.maximum(i * nb - 1, 0), 1)),
            pl.BlockSpec((1, SWA_SPAN, SWA_GW), lambda s, i: (s, jnp.maximum(i * nb - 1, 0), 2)),
        ],
        out_specs=[
            pl.BlockSpec((1, tq, SWA_GW), lambda s, i: (s, i, 0)),
            pl.BlockSpec((1, tq, LANES), lambda s, i: (s, i, 0)),
        ],
        out_shape=[
            jax.ShapeDtypeStruct((S, M, SWA_GW), BF16),
            jax.ShapeDtypeStruct((S, M, LANES), F32),
        ],
        scratch_shapes=[
            pltpu.VMEM((SWA_SPAN + tq, SWA_GW), BF16),
            pltpu.VMEM((SWA_SPAN + tq, SWA_GW), BF16),
        ],
        compiler_params=_cparams(2),
        name="swa_attn",
    )(p, p, p, p, p)


def _proj_dn_kernel(h_ref, w_ref, cw_ref, q_ref, k_ref, v_ref, tail_ref, z_scr, carry_scr, *, tm, n_ct):
    i = pl.program_id(1)
    j = pl.program_id(2)
    nh = DN_HEADS
    ncb = z_scr.shape[1] // LANES

    z_scr[0:8, :] = jnp.where(i == 0, 0.0, carry_scr[j])
    z_scr[8:8 + tm, :] = jnp.dot(h_ref[0], w_ref[...], preferred_element_type=F32)
    last = z_scr[tm:tm + 8, :]
    carry_scr[j] = last
    tn = z_scr.shape[1]
    outs = (q_ref, k_ref, v_ref)
    for jj in range(n_ct):

        @pl.when(j == jj)
        def _(jj=jj):
            tail_ref[0, :, jj * tn:(jj + 1) * tn] = last
            for cbl in range(ncb):
                cs = slice(cbl * LANES, (cbl + 1) * LANES)
                part, h = divmod(jj * ncb + cbl, nh)
                xe = z_scr[:, cs]
                acc = (0.5 * cw_ref[0:1, cs]) * xe
                for t in range(1, DN_CONV):
                    acc = (0.5 * cw_ref[t:t + 1, cs]) * xe + pltpu.roll(acc, 1, axis=0)
                half = acc[8:]
                act = half * jnp.tanh(half) + half
                if part < 2:
                    inv = lax.rsqrt(jnp.sum(act * act, axis=-1, keepdims=True) + EPS)
                    act = act * (inv * (DN_DK ** -0.5) if part == 0 else inv)
                outs[part][0, :, h * LANES:(h + 1) * LANES] = act.astype(BF16)


def _proj_dn(h, w_dn, conv_w, *, tm):
    B, L, D = h.shape
    C = w_dn.shape[1]
    width = DN_HEADS * DN_DK
    n_ct = 2
    tn = C // n_ct
    assert L % tm == 0 and C == 3 * width and tn % LANES == 0
    qkv = pl.BlockSpec((1, tm, width), lambda b, i, j: (b, i, 0))
    return pl.pallas_call(
        functools.partial(_proj_dn_kernel, tm=tm, n_ct=n_ct),
        grid=(B, L // tm, n_ct),
        in_specs=[
            pl.BlockSpec((1, tm, D), lambda b, i, j: (b, i, 0)),
            pl.BlockSpec((D, tn), lambda b, i, j: (0, j)),
            pl.BlockSpec((DN_CONV, tn), lambda b, i, j: (0, j)),
        ],
        out_specs=[qkv, qkv, qkv, pl.BlockSpec((1, 8, C), lambda b, i, j: (b, 0, 0))],
        out_shape=[jax.ShapeDtypeStruct((B, L, width), BF16)] * 3 + [jax.ShapeDtypeStruct((B, 8, C), F32)],
        scratch_shapes=[pltpu.VMEM((8 + tm, tn), F32), pltpu.VMEM((n_ct, 8, tn), F32)],
        compiler_params=_cparams(3),
        name="proj_dn",
    )(h, w_dn, conv_w)


def _gates_kernel(h_ref, w_ref, par_ref, g_ref, gt_ref, *, tl):
    nh = DN_HEADS
    z = jnp.dot(h_ref[0], w_ref[...], preferred_element_type=F32)
    lane = lax.broadcasted_iota(jnp.int32, (tl, LANES), 1)
    ri = lax.broadcasted_iota(jnp.int32, (tl, tl), 0)
    ci = lax.broadcasted_iota(jnp.int32, (tl, tl), 1)
    tri = jnp.where((ri // DN_CHUNK == ci // DN_CHUNK) & (ci <= ri), 1.0, 0.0).astype(BF16)
    for sb in range(z.shape[0] // tl):
        rows = slice(sb * tl, (sb + 1) * tl)
        ba = z[rows]
        g = par_ref[0:1, :] * _softplus(ba + par_ref[1:2, :])
        g_hi = g.astype(BF16)
        r1 = g - g_hi.astype(F32)
        g_mid = r1.astype(BF16)
        g_lo = (r1 - g_mid.astype(F32)).astype(BF16)
        gc = sum(jnp.dot(tri, piece, preferred_element_type=F32) for piece in (g_hi, g_mid, g_lo))
        g_ref[0, rows, :] = jnp.where(lane < nh, _sigmoid(ba), gc)
        gt_ref[0, :, rows] = jnp.transpose(gc)[nh:2 * nh, :]


def _dn_gates(h, w_ba, par, *, tm, tl):
    B, L, D = h.shape
    assert L % tm == 0 and tm % tl == 0 and tl % DN_CHUNK == 0
    return pl.pallas_call(
        functools.partial(_gates_kernel, tl=tl),
        grid=(B, L // tm),
        in_specs=[pl.BlockSpec((1, tm, D), lambda b, i: (b, i, 0)), pl.BlockSpec((D, LANES), lambda b, i: (0, 0)),
                  pl.BlockSpec((2, LANES), lambda b, i: (0, 0))],
        out_specs=[pl.BlockSpec((1, tm, LANES), lambda b, i: (b, i, 0)),
                   pl.BlockSpec((1, DN_HEADS, tm), lambda b, i: (b, 0, i))],
        out_shape=[jax.ShapeDtypeStruct((B, L, LANES), F32), jax.ShapeDtypeStruct((B, DN_HEADS, L), F32)],
        compiler_params=_cparams(2),
        name="dn_gates",
    )(h, w_ba, par)


def _intra_kernel(q_ref, k_ref, v_ref, g_ref, gt_in_ref, u_ref, w_ref, qd_ref, kd_ref, a_ref, gt_ref, *, tl):
    h = pl.program_id(1)
    C = DN_CHUNK
    lane = lax.broadcasted_iota(jnp.int32, (C, LANES), 1)
    ri = lax.broadcasted_iota(jnp.int32, (C, C), 0)
    ci = lax.broadcasted_iota(jnp.int32, (C, C), 1)
    eye = jnp.where(ri == ci, 1.0, 0.0).astype(F32)
    nt_dot = lambda a, b: lax.dot_general(a.astype(BF16), b.astype(BF16), (((1,), (1,)), ((), ())),
                                          preferred_element_type=F32)
    rows = [slice(c * C, (c + 1) * C) for c in range(tl // C)]
    gv = [g_ref[0, r, :] for r in rows]
    q = [q_ref[0, r, :].astype(F32) for r in rows]
    k = [k_ref[0, r, :].astype(F32) for r in rows]
    v = [v_ref[0, r, :].astype(F32) for r in rows]
    beta = [jnp.sum(jnp.where(lane == h, x, 0.0), axis=-1, keepdims=True) for x in gv]
    gc = [jnp.sum(jnp.where(lane == h + DN_HEADS, x, 0.0), axis=-1, keepdims=True) for x in gv]
    gc_row = gt_in_ref[0, pl.ds(h, 1), :]
    decay = [jnp.exp(jnp.where(ri >= ci, a - gc_row[:, r], -jnp.inf)) for a, r in zip(gc, rows)]
    kb = [a * b for a, b in zip(k, beta)]
    kq = [nt_dot(jnp.concatenate([a, b], axis=0), c) for a, b, c in zip(kb, q, k)]
    x = [-jnp.where(ri > ci, m[:C] * d, 0.0) for m, d in zip(kq, decay)]
    t = [eye + a for a in x]
    x = [_bdot(a, a) for a in x]
    for _ in range(4):
        both = [_bdot(jnp.concatenate([a, b], axis=0), a) for a, b in zip(x, t)]
        t = [b + m[C:] for b, m in zip(t, both)]
        x = [m[:C] for m in both]
    t = [b + _bdot(b, a) for a, b in zip(x, t)]
    eg = [jnp.exp(a) for a in gc]
    glast = [a[C - 1:C, :] for a in gc]
    uw = [_bdot(a, jnp.concatenate([b * c, d * e], axis=1)) for a, b, c, d, e in zip(t, v, beta, kb, eg)]
    for c, r in enumerate(rows):
        u_ref[0, 0, r, :] = uw[c][:, :LANES]
        w_ref[0, 0, r, :] = uw[c][:, LANES:].astype(BF16)
        a_ref[0, 0, r, :] = (kq[c][C:] * decay[c]).astype(BF16)
        qd_ref[0, 0, r, :] = (q[c] * eg[c]).astype(BF16)
        kd_ref[0, 0, r, :] = (k[c] * jnp.exp(glast[c] - gc[c])).astype(BF16)
        gt_ref[0, 0, c:c + 1, :] = jnp.broadcast_to(jnp.exp(glast[c]), (1, LANES))


def _dn_intra(q, k, v, g, gt_rows, *, tl):
    B, L, _ = q.shape
    H, C = DN_HEADS, DN_CHUNK
    assert L % tl == 0 and (tl // C) % 8 == 0
    qkv_spec = pl.BlockSpec((1, tl, LANES), lambda b, h, i: (b, i, h))
    hl = lambda w: pl.BlockSpec((1, 1, tl, w), lambda b, h, i: (b, h, i, 0))
    return pl.pallas_call(
        functools.partial(_intra_kernel, tl=tl),
        grid=(B, H, L // tl),
        in_specs=[qkv_spec, qkv_spec, qkv_spec, pl.BlockSpec((1, tl, LANES), lambda b, h, i: (b, i, 0)),
                  pl.BlockSpec((1, H, tl), lambda b, h, i: (b, 0, i))],
        out_specs=[hl(LANES), hl(LANES), hl(LANES), hl(LANES), hl(C),
                   pl.BlockSpec((1, 1, tl // C, LANES), lambda b, h, i: (b, h, i, 0))],
        out_shape=[
            jax.ShapeDtypeStruct((B, H, L, LANES), F32),
            jax.ShapeDtypeStruct((B, H, L, LANES), BF16),
            jax.ShapeDtypeStruct((B, H, L, LANES), BF16),
            jax.ShapeDtypeStruct((B, H, L, LANES), BF16),
            jax.ShapeDtypeStruct((B, H, L, C), BF16),
            jax.ShapeDtypeStruct((B, H, L // C, LANES), F32),
        ],
        compiler_params=_cparams(3),
        name="dn_intra",
    )(q, k, v, g, gt_rows)


def _scan_kernel(u_ref, w_ref, qd_ref, kd_ref, a_ref, gt_ref, s0_ref, o_ref, s_ref, *, n_chunks):
    C = DN_CHUNK
    bb, H = s_ref.shape[0], s_ref.shape[1]
    seqs = [(b, h) for b in range(bb) for h in range(H)]

    @pl.when(pl.program_id(1) == 0)
    def _():
        s_ref[...] = s0_ref[...]

    def body(c, carry):
        rows = pl.ds(pl.multiple_of(c * C, C), C)
        S = [s_ref[b, h] for b, h in seqs]
        Sb = [x.astype(BF16) for x in S]
        v_new = [u_ref[b, h, rows, :] - jnp.dot(w_ref[b, h, rows, :], sb, preferred_element_type=F32)
                 for (b, h), sb in zip(seqs, Sb)]
        vb = [x.astype(BF16) for x in v_new]
        o = [jnp.dot(qd_ref[b, h, rows, :], sb, preferred_element_type=F32)
             + jnp.dot(a_ref[b, h, rows, :], v, preferred_element_type=F32) for (b, h), sb, v in zip(seqs, Sb, vb)]
        upd = [lax.dot_general(kd_ref[b, h, rows, :], v, (((0,), (0,)), ((), ())), preferred_element_type=F32)
               for (b, h), v in zip(seqs, vb)]
        for n, (b, h) in enumerate(seqs):
            o_ref[b, rows, h * LANES:(h + 1) * LANES] = o[n]
            s_ref[b, h] = S[n] * gt_ref[b, h, pl.ds(c, 1), :] + upd[n]
        return carry

    lax.fori_loop(0, n_chunks, body, 0)


def _dn_scan(u, w, qd, kd, a, gt, s0, *, tl, bb):
    B, H, L, _ = u.shape
    C = DN_CHUNK
    assert L % tl == 0 and (tl // C) % 8 == 0 and B % bb == 0
    hs = lambda wd: pl.BlockSpec((bb, H, tl, wd), lambda b, i: (b, 0, i, 0))
    s_spec = pl.BlockSpec((bb, H, DN_DK, LANES), lambda b, i: (b, 0, 0, 0))
    return pl.pallas_call(
        functools.partial(_scan_kernel, n_chunks=tl // C),
        grid=(B // bb, L // tl),
        in_specs=[hs(LANES), hs(LANES), hs(LANES), hs(LANES), hs(C),
                  pl.BlockSpec((bb, H, tl // C, LANES), lambda b, i: (b, 0, i, 0)), s_spec],
        out_specs=[pl.BlockSpec((bb, tl, H * LANES), lambda b, i: (b, i, 0)), s_spec],
        out_shape=[jax.ShapeDtypeStruct((B, L, H * LANES), F32),
                   jax.ShapeDtypeStruct((B, H, DN_DK, LANES), F32)],
        compiler_params=_cparams(2),
        name="dn_scan",
    )(u, w, qd, kd, a, gt, s0)


def _gated_mix(o_a, od, gates, dng, wa, wb, wo, x, dot):
    width = DN_HEADS * DN_DK
    parts = []
    for h in range(DN_HEADS):
        blk = od[:, h * LANES:(h + 1) * LANES]
        parts.append(blk * lax.rsqrt(jnp.mean(blk * blk, axis=-1, keepdims=True) + EPS) * dng)
    odn = jnp.concatenate(parts, axis=-1) * _silu(gates[:, 0:width].astype(F32))
    ya = dot(o_a, wa)
    yb = dot(odn, wb)
    mix = _sigmoid(gates[:, width:2 * width].astype(F32)) * ya + _sigmoid(gates[:, 2 * width:].astype(F32)) * yb
    return x + dot(mix, wo)


def _out_kernel(x_ref, o0, o1, o2, l0, l1, l2, od_ref, gates_ref, dng_ref, wa_ref, wb_ref, wo_ref, e_ref, y_ref,
                so0, so1, so2, sl0, sl1, sl2, *, tm, dils):
    o_refs, l_refs = (o0, o1, o2), (l0, l1, l2)
    so, sl = (so0, so1, so2), (sl0, sl1, sl2)
    for gi, d in enumerate(dils):
        for r in range(d):
            dst = slice(None) if d == 1 else pl.ds(r, tm // d, stride=d)
            sl[gi][dst, :] = l_refs[gi][0, r]
            for cb in range(SWA_GW // LANES):
                so[gi][cb, dst, :] = o_refs[gi][0, r, :, cb * LANES:(cb + 1) * LANES].astype(F32)
    ls = [s[...] for s in sl]
    m = jnp.maximum(jnp.maximum(ls[0], ls[1]), ls[2])
    es = [jnp.exp(l - m) for l in ls]
    tot = es[0] + es[1] + es[2]
    alphas = [jnp.dot((e / tot).astype(BF16), e_ref[...], preferred_element_type=F32) for e in es]
    parts = []
    for cb in range(SWA_GW // LANES):
        cs = slice(cb * LANES, (cb + 1) * LANES)
        parts.append(alphas[0][:, cs] * so[0][cb] + alphas[1][:, cs] * so[1][cb] + alphas[2][:, cs] * so[2][cb])
    o_a = jnp.concatenate(parts, axis=-1)
    y_ref[...] = _gated_mix(o_a, od_ref[...], gates_ref[...], dng_ref[...], wa_ref[...], wb_ref[...],
                            wo_ref[...], x_ref[...], _bdot)


def _out_proj(x2d, os_, ls_, od2d, gates, dng, wa, wb, wo, e_att, *, B, L, tm):
    N, D = x2d.shape
    nt = L // tm
    dils = tuple(d for _, d in SWA_CONFIGS)
    grp = lambda d, w: pl.BlockSpec((1, d, tm // d, w), lambda i: (i // nt, 0, i % nt, 0))
    row = lambda w: pl.BlockSpec((tm, w), lambda i: (i, 0))
    full = lambda a: pl.BlockSpec(a.shape, lambda i: (0, 0))
    return pl.pallas_call(
        functools.partial(_out_kernel, tm=tm, dils=dils),
        grid=(N // tm,),
        in_specs=[row(D)] + [grp(d, SWA_GW) for d in dils] + [grp(d, LANES) for d in dils]
        + [row(od2d.shape[1]), row(gates.shape[1]), full(dng), full(wa), full(wb), full(wo), full(e_att)],
        out_specs=row(D),
        out_shape=jax.ShapeDtypeStruct((N, D), F32),
        scratch_shapes=[pltpu.VMEM((SWA_GW // LANES, tm, LANES), F32)] * 3 + [pltpu.VMEM((tm, LANES), F32)] * 3,
        compiler_params=_cparams(1),
        name="out_proj",
    )(x2d, *os_, *ls_, od2d, gates, dng, wa, wb, wo, e_att)


def _router_kernel(x_ref, lng_ref, wr_ref, br_ref, info_ref, cnt_ref, base_scr, *, tm):
    i = pl.program_id(0)

    @pl.when(i == 0)
    def _():
        base_scr[...] = jnp.zeros_like(base_scr)

    h = _rms(x_ref[...], lng_ref[...])
    lg = _bdot(h, wr_ref[...]) + br_ref[...]
    lane = lax.broadcasted_iota(jnp.int32, (tm, LANES), 1)
    big = jnp.int32(1 << 20)
    ninf = -jnp.inf

    def argmax_lane(vals):
        mx = jnp.max(vals, axis=-1, keepdims=True)
        idx = jnp.min(jnp.where(vals == mx, lane, big), axis=-1, keepdims=True)
        return mx, idx

    lgm = jnp.where(lane < N_GROUPS, lg, ninf)
    mg, gsel = argmax_lane(lgm)
    pg = 1.0 / jnp.sum(jnp.exp(lgm - mg), axis=-1, keepdims=True)
    start = N_GROUPS + gsel * PER_GROUP
    le = jnp.where((lane >= start) & (lane < start + PER_GROUP), lg, ninf)
    m1, i1 = argmax_lane(le)
    m2, i2 = argmax_lane(jnp.where(lane == i1, ninf, le))
    e21 = jnp.exp(m2 - m1)
    w1 = pg / (1.0 + e21)
    w2 = pg * e21 / (1.0 + e21)
    oh = jnp.where(lane == i1, 1.0, 0.0) + jnp.where(lane == i2, 1.0, 0.0)
    ri = lax.broadcasted_iota(jnp.int32, (tm, tm), 0)
    ci = lax.broadcasted_iota(jnp.int32, (tm, tm), 1)
    strict = jnp.where(ci < ri, 1.0, 0.0).astype(BF16)
    pref = jnp.dot(strict, oh.astype(BF16), preferred_element_type=F32) + base_scr[...]
    r1 = jnp.sum(jnp.where(lane == i1, pref, 0.0), axis=-1, keepdims=True)
    r2 = jnp.sum(jnp.where(lane == i2, pref, 0.0), axis=-1, keepdims=True)
    base_scr[...] = base_scr[...] + jnp.sum(oh, axis=0, keepdims=True)
    cnt_ref[...] = base_scr[...]
    off = jnp.float32(N_GROUPS)
    info = jnp.where(lane == 0, i1.astype(F32) - off, 0.0)
    info = jnp.where(lane == 1, i2.astype(F32) - off, info)
    info = jnp.where(lane == 2, w1, info)
    info = jnp.where(lane == 3, w2, info)
    info = jnp.where(lane == 4, r1, info)
    info = jnp.where(lane == 5, r2, info)
    info_ref[...] = info


def _router(x2d, ln_g, wr, br, *, tm):
    N, D = x2d.shape
    assert N % tm == 0
    return pl.pallas_call(
        functools.partial(_router_kernel, tm=tm),
        grid=(N // tm,),
        in_specs=[
            pl.BlockSpec((tm, D), lambda i: (i, 0)),
            pl.BlockSpec((1, D), lambda i: (0, 0)),
            pl.BlockSpec((D, LANES), lambda i: (0, 0)),
            pl.BlockSpec((1, LANES), lambda i: (0, 0)),
        ],
        out_specs=[pl.BlockSpec((tm, LANES), lambda i: (i, 0)), pl.BlockSpec((1, LANES), lambda i: (0, 0))],
        out_shape=[jax.ShapeDtypeStruct((N, LANES), F32), jax.ShapeDtypeStruct((1, LANES), F32)],
        scratch_shapes=[pltpu.VMEM((1, LANES), F32)],
        compiler_params=_cparams(1),
        name="router",
    )(x2d, ln_g, wr, br)


def _dispatch_kernel(dest_ref, zb_ref, x_ref, lng_ref, xs_ref, zero_scr, rows_scr, sem, *, tm, tb, n_zb, n_tiles):
    i = pl.program_id(0)

    @pl.when(i == 0)
    def _():
        zero_scr[...] = jnp.zeros_like(zero_scr)

        def zero_copy(n):
            return pltpu.make_async_copy(zero_scr, xs_ref.at[pl.ds(zb_ref[n] * tb, tb)], sem.at[2])

        def zero_issue(n, carry):
            @pl.when(zb_ref[n] >= 0)
            def _():
                zero_copy(n).start()

            return carry

        def zero_wait(n, carry):
            @pl.when(zb_ref[n] >= 0)
            def _():
                zero_copy(n).wait()

            return carry

        lax.fori_loop(0, n_zb, zero_issue, 0)
        lax.fori_loop(0, n_zb, zero_wait, 0)

    buf_now = lax.rem(i, 2)
    rows_scr[buf_now] = _pack_bf16_pairs(_rms(x_ref[...], lng_ref[...])).reshape(rows_scr.shape[1:])

    def row_copy(tile, t, slot):
        buf = lax.rem(tile, 2)
        return pltpu.make_async_copy(
            rows_scr.at[buf, pl.ds(t, 1)],
            xs_ref.at[pl.ds(dest_ref[(tile * tm + t) * TOP_K + slot], 1)], sem.at[buf])

    def issue(tt, carry):
        for r in range(ROW_UNROLL):
            for slot in range(TOP_K):
                row_copy(i, tt * ROW_UNROLL + r, slot).start(priority=slot)
        return carry

    def drain(tile):
        buf = lax.rem(tile, 2)
        for _ in range(TOP_K):
            pltpu.make_async_copy(rows_scr.at[buf], rows_scr.at[buf], sem.at[buf]).wait()

    lax.fori_loop(0, tm // ROW_UNROLL, issue, 0)

    @pl.when(i > 0)
    def _():
        drain(i - 1)

    @pl.when(i == n_tiles - 1)
    def _():
        drain(i)


def _dispatch(dest, zero_blocks, x2d, ln_g, *, tm, tb, n_rows):
    N, D = x2d.shape
    row = (D // 2 // LANES, LANES)
    return pl.pallas_call(
        functools.partial(_dispatch_kernel, tm=tm, tb=tb, n_zb=zero_blocks.shape[0], n_tiles=N // tm),
        grid_spec=pltpu.PrefetchScalarGridSpec(
            num_scalar_prefetch=2,
            grid=(N // tm,),
            in_specs=[pl.BlockSpec((tm, D), lambda i, d, z: (i, 0)), pl.BlockSpec((1, D), lambda i, d, z: (0, 0))],
            out_specs=pl.BlockSpec(memory_space=pl.ANY),
            scratch_shapes=[pltpu.VMEM((tb,) + row, jnp.uint32), pltpu.VMEM((2, tm) + row, jnp.uint32),
                            pltpu.SemaphoreType.DMA((3,))],
        ),
        out_shape=jax.ShapeDtypeStruct((n_rows,) + row, jnp.uint32),
        compiler_params=_cparams(1),
        name="moe_dispatch",
    )(dest, zero_blocks, x2d, ln_g)


def _ffn_kernel(be_ref, nb_ref, xs_hbm, wg_ref, wu_ref, wd_ref, y_ref, wg_scr, wu_scr, wd_scr, x_ring, ring_sem):
    i = pl.program_id(0)
    n_used = nb_ref[0]
    used = i < n_used
    n_slots, tb = x_ring.shape[0], x_ring.shape[1]

    def fetch(blk):
        slot = lax.rem(blk, n_slots)
        return pltpu.make_async_copy(xs_hbm.at[pl.ds(blk * tb, tb)], x_ring.at[slot], ring_sem.at[slot])

    @pl.when(i == 0)
    def _():
        fetch(i).start()

        @pl.when(n_used > 1)
        def _():
            fetch(i + 1).start()

    @pl.when(i + 2 < n_used)
    def _():
        fetch(i + 2).start()

    @pl.when(jnp.logical_or(i == 0, be_ref[i] != be_ref[jnp.maximum(i - 1, 0)]))
    def _():
        wg_scr[...] = wg_ref[0].astype(BF16)
        wu_scr[...] = wu_ref[0].astype(BF16)
        wd_scr[...] = wd_ref[0].astype(BF16)

    @pl.when(used)
    def _():
        fetch(i).wait()
        h = _unpack_bf16_pairs(x_ring[lax.rem(i, n_slots)].reshape(tb, -1)).astype(BF16)
        g = jnp.dot(h, wg_scr[...], preferred_element_type=F32)
        u = jnp.dot(h, wu_scr[...], preferred_element_type=F32)
        y = jnp.dot((_silu(g) * u).astype(BF16), wd_scr[...], preferred_element_type=F32)
        y_ref[...] = _pack_bf16_pairs(y).reshape(y_ref.shape)

    @pl.when(jnp.logical_not(used))
    def _():
        y_ref[...] = jnp.zeros_like(y_ref)


def _ffn(blk_e, nb_used, xs, wg, wu, wd, layer, *, tb):
    P, S, _ = xs.shape
    D = 2 * S * LANES
    nb = P // tb
    DE = wg.shape[3]
    return pl.pallas_call(
        _ffn_kernel,
        grid_spec=pltpu.PrefetchScalarGridSpec(
            num_scalar_prefetch=2,
            grid=(nb,),
            in_specs=[
                pl.BlockSpec(memory_space=pl.ANY),
                pl.BlockSpec((None, 1, D, DE), lambda i, be, nbu: (layer, be[i], 0, 0)),
                pl.BlockSpec((None, 1, D, DE), lambda i, be, nbu: (layer, be[i], 0, 0)),
                pl.BlockSpec((None, 1, DE, D), lambda i, be, nbu: (layer, be[i], 0, 0)),
            ],
            out_specs=pl.BlockSpec((tb, S, LANES), lambda i, be, nbu: (i, 0, 0)),
            scratch_shapes=[pltpu.VMEM((D, DE), BF16), pltpu.VMEM((D, DE), BF16), pltpu.VMEM((DE, D), BF16),
                            pltpu.VMEM((3, tb, S, LANES), jnp.uint32), pltpu.SemaphoreType.DMA((3,))],
        ),
        out_shape=jax.ShapeDtypeStruct((P, S, LANES), jnp.uint32),
        compiler_params=_cparams(1),
        name="moe_ffn",
    )(blk_e, nb_used, xs, wg, wu, wd)


def _combine_kernel(dest_ref, x_ref, info_ref, yb_ref, y_ref, g_scr, sem, *, tm, n_tiles):
    i = pl.program_id(0)

    def row_copy(tile, t, slot):
        buf = lax.rem(tile, 2)
        return pltpu.make_async_copy(
            yb_ref.at[pl.ds(dest_ref[(tile * tm + t) * TOP_K + slot], 1)],
            g_scr.at[buf, slot, pl.ds(t, 1)], sem.at[buf])

    def issue_tile(tile):
        def body(tt, carry):
            for r in range(ROW_UNROLL):
                for slot in range(TOP_K):
                    row_copy(tile, tt * ROW_UNROLL + r, slot).start(priority=slot)
            return carry

        lax.fori_loop(0, tm // ROW_UNROLL, body, 0)

    @pl.when(i == 0)
    def _():
        issue_tile(i)

    @pl.when(i + 1 < n_tiles)
    def _():
        issue_tile(i + 1)

    buf = lax.rem(i, 2)
    pltpu.make_async_copy(g_scr.at[buf], g_scr.at[buf], sem.at[buf]).wait()
    info = info_ref[...]
    lane = lax.broadcasted_iota(jnp.int32, info.shape, 1)
    w1 = jnp.sum(jnp.where(lane == 2, info, 0.0), axis=-1, keepdims=True)
    w2 = jnp.sum(jnp.where(lane == 3, info, 0.0), axis=-1, keepdims=True)
    g1 = _unpack_bf16_pairs(g_scr[buf, 0].reshape(tm, -1))
    g2 = _unpack_bf16_pairs(g_scr[buf, 1].reshape(tm, -1))
    y_ref[...] = x_ref[...] + (w1 * g1 + w2 * g2)


def _combine(dest, x2d, info, yb, *, tm):
    N, D = x2d.shape
    return pl.pallas_call(
        functools.partial(_combine_kernel, tm=tm, n_tiles=N // tm),
        grid_spec=pltpu.PrefetchScalarGridSpec(
            num_scalar_prefetch=1,
            grid=(N // tm,),
            in_specs=[
                pl.BlockSpec((tm, D), lambda i, d: (i, 0)),
                pl.BlockSpec((tm, LANES), lambda i, d: (i, 0)),
                pl.BlockSpec(memory_space=pl.ANY),
            ],
            out_specs=pl.BlockSpec((tm, D), lambda i, d: (i, 0)),
            scratch_shapes=[pltpu.VMEM((2, TOP_K, tm, D // 2 // LANES, LANES), jnp.uint32),
                            pltpu.SemaphoreType.DMA((2,))],
        ),
        out_shape=jax.ShapeDtypeStruct((N, D), F32),
        compiler_params=_cparams(1),
        name="moe_combine",
    )(dest, x2d, info, yb)


def _moe(x2d, ln2_g, wr, br, wg, wu, wd, layer, *, tm):
    N, D = x2d.shape
    tb = MOE_ROWS if N * TOP_K >= N_EXPERTS * MOE_ROWS else MOE_ROWS_SMALL
    info, counts = _router(x2d, ln2_g, wr, br, tm=ROUTER_ROWS if N % ROUTER_ROWS == 0 else tm)
    counts = counts[0, N_GROUPS:N_GROUPS + N_EXPERTS].astype(jnp.int32)
    pcounts = (counts + tb - 1) // tb * tb
    pend = jnp.cumsum(pcounts)
    pstart = pend - pcounts
    e = info[:, 0:TOP_K].astype(jnp.int32)
    rank = info[:, 4:4 + TOP_K].astype(jnp.int32)
    experts = jnp.arange(N_EXPERTS, dtype=jnp.int32)
    dest = (jnp.sum(jnp.where(e[..., None] == experts, pstart, 0), axis=-1) + rank).reshape(-1)
    nb = -(-(N * TOP_K) // tb) + N_EXPERTS
    P = nb * tb
    blocks = jnp.arange(nb, dtype=jnp.int32)
    blk_e = jnp.minimum(jnp.sum((pend[None, :] <= blocks[:, None] * tb).astype(jnp.int32), axis=1), N_EXPERTS - 1)
    nb_used = (pend[-1] // tb).astype(jnp.int32).reshape(1)
    zero_blocks = jnp.concatenate([jnp.where(counts % tb != 0, pend // tb - 1, -1),
                                   jnp.where(blocks >= nb_used[0], blocks, -1)]).astype(jnp.int32)
    xs = _dispatch(dest, zero_blocks, x2d, ln2_g, tm=tm, tb=tb, n_rows=P)
    yb = _ffn(blk_e, nb_used, xs, wg, wu, wd, layer, tb=tb)
    return _combine(dest, x2d, info, yb, tm=tm)


def _rows8(x):
    return jnp.broadcast_to(x, (8, x.shape[1]))


def _row_hdot(x, m):
    return _hdot(_rows8(x), m)[0:1]


def _bf_round(x):
    return x.astype(BF16).astype(F32)


def _sample_attn_kernel(z_ref, c0, c1, c2, qg_ref, kg_ref, oa_ref, kv_ref):
    W = SWA_GW
    scale = SWA_DIM ** -0.5
    z = z_ref[0]
    sub = lax.broadcasted_iota(jnp.int32, (SWA_HEADS, W), 0)
    lane = lax.broadcasted_iota(jnp.int32, (SWA_HEADS, W), 1)
    own = lane // SWA_DIM == sub

    def heads(row):
        return jnp.where(own, jnp.broadcast_to(row, (SWA_HEADS, W)), 0.0)

    def head_sum(row):
        return jnp.sum(heads(row), axis=-1, keepdims=True)

    def spread(col):
        return jnp.sum(jnp.where(own, col, 0.0), axis=0, keepdims=True)

    def headnorm(zz, g):
        return zz * spread(lax.rsqrt(head_sum(zz * zz) * (1.0 / SWA_DIM) + EPS)) * g

    outs, lses = [], []
    for gi, (c_ref, (win, dil)) in enumerate(zip((c0, c1, c2), SWA_CONFIGS)):
        q = headnorm(z[:, gi * W:(gi + 1) * W], qg_ref[gi:gi + 1, :])
        k = headnorm(z[:, 3 * W + gi * W:3 * W + (gi + 1) * W], kg_ref[gi:gi + 1, :])
        v = z[:, 6 * W + gi * W:6 * W + (gi + 1) * W]
        kv_ref[0, :, 2 * gi * W:(2 * gi + 1) * W] = k
        kv_ref[0, :, (2 * gi + 1) * W:(2 * gi + 2) * W] = v
        kc = c_ref[0].reshape(W, win).astype(BF16)
        vc = c_ref[1].reshape(W, win).astype(BF16)
        s_c = jnp.dot(heads(q).astype(BF16), kc, preferred_element_type=F32) * scale
        row = lax.broadcasted_iota(jnp.int32, s_c.shape, 1)
        s_c = jnp.where(row % dil == 0, s_c, -jnp.inf)
        s_n = head_sum(_bf_round(k) * _bf_round(q)) * scale
        m = jnp.maximum(jnp.max(s_c, axis=-1, keepdims=True), s_n)
        p_c = jnp.exp(s_c - m)
        p_n = jnp.exp(s_n - m)
        den = jnp.sum(p_c, axis=-1, keepdims=True) + p_n
        pv = lax.dot_general(p_c.astype(BF16), vc, (((1,), (1,)), ((), ())), preferred_element_type=F32)
        num = jnp.sum(jnp.where(own, pv, 0.0), axis=0, keepdims=True) + spread(_bf_round(p_n)) * _bf_round(v)
        outs.append(num / spread(den))
        lses.append(m + jnp.log(den))
    mm = jnp.maximum(jnp.maximum(lses[0], lses[1]), lses[2])
    es = [jnp.exp(l - mm) for l in lses]
    tot = es[0] + es[1] + es[2]
    oa_ref[0] = sum(spread(_bf_round(e / tot)) * _bf_round(o) for e, o in zip(es, outs))


def _sample_attn(z3, caches, layer, qg, kg):
    Bs = z3.shape[0]
    W = SWA_GW
    cviews, cspecs = [], []
    for (win, dil), c in zip(SWA_CONFIGS, caches):
        assert c.shape[2] == win
        cviews.append(jnp.transpose(c, (0, 1, 3, 4, 5, 2)))
        cspecs.append(pl.BlockSpec((None, None, 2, SWA_HEADS, SWA_DIM, win), lambda b: (layer, b, 0, 0, 0, 0)))
    full = lambda a: pl.BlockSpec(a.shape, lambda b: (0,) * a.ndim)
    return pl.pallas_call(
        _sample_attn_kernel,
        grid=(Bs,),
        in_specs=[pl.BlockSpec((1, 1, 9 * W), lambda b: (b, 0, 0))] + cspecs + [full(qg), full(kg)],
        out_specs=[pl.BlockSpec((1, 1, W), lambda b: (b, 0, 0)), pl.BlockSpec((1, 1, 6 * W), lambda b: (b, 0, 0))],
        out_shape=[jax.ShapeDtypeStruct((Bs, 1, W), F32), jax.ShapeDtypeStruct((Bs, 1, 6 * W), F32)],
        compiler_params=_cparams(1),
        name="sample_attn",
    )(z3, *cviews, qg, kg)


def _sample_dn_kernel(raw_ref, cs_ref, cw_ref, ba_ref, par_ref, s_ref, e_ref, etb_ref, etg_ref, o_ref, so_ref):
    E, ETB, ETG = e_ref[...], etb_ref[...], etg_ref[...]
    width = DN_HEADS * DN_DK
    conv = cw_ref[DN_CONV - 1:DN_CONV, :] * raw_ref[0]
    for t in range(DN_CONV - 1):
        conv = conv + cw_ref[t:t + 1, :] * cs_ref[0, t:t + 1, :]
    act = _silu(conv)

    def l2(zz):
        return zz * _row_hdot(lax.rsqrt(_row_hdot(zz * zz, E) + EPS), ETB)

    qn = l2(act[:, 0:width]) * (DN_DK ** -0.5)
    kn = l2(act[:, width:2 * width])
    vn = act[:, 2 * width:3 * width]
    ba = ba_ref[0]
    beta = _row_hdot(_sigmoid(ba), ETB)
    eg = jnp.exp(_row_hdot(par_ref[0:1, :] * _softplus(ba + par_ref[1:2, :]), ETG))
    row0 = lax.broadcasted_iota(jnp.int32, (8, LANES), 0) == 0
    for h in range(DN_HEADS):
        sl = slice(h * LANES, (h + 1) * LANES)
        S = s_ref[0, h]
        q, k, v, b, e = qn[:, sl], kn[:, sl], vn[:, sl], beta[:, sl], eg[:, sl]
        Sb = S.astype(BF16)
        wq = jnp.concatenate([k * b * e, q * e, jnp.zeros((6, LANES), F32)], axis=0)
        both = jnp.dot(wq.astype(BF16), Sb, preferred_element_type=F32)
        v_new = v * b - both[0:1]
        a = jnp.sum(q * k, axis=-1, keepdims=True)
        o_ref[0, :, sl] = both[1:2] + a * v_new
        k8 = jnp.where(row0, _rows8(k), 0.0)
        upd = lax.dot_general(k8, _rows8(v_new), (((0,), (0,)), ((), ())), preferred_element_type=F32, precision=HI)
        so_ref[0, h] = S * e + upd


def _sample_dn(raw3, conv_state, s0, layer, conv_w, ba3, par, e_mat, etb, etg):
    Bs, _, C = raw3.shape
    H = DN_HEADS
    full = lambda a: pl.BlockSpec(a.shape, lambda b: (0,) * a.ndim)
    return pl.pallas_call(
        _sample_dn_kernel,
        grid=(Bs,),
        in_specs=[pl.BlockSpec((1, 1, C), lambda b: (b, 0, 0)),
                  pl.BlockSpec((None, 1, DN_CONV - 1, C), lambda b: (layer, b, 0, 0)),
                  full(conv_w),
                  pl.BlockSpec((1, 1, LANES), lambda b: (b, 0, 0)),
                  full(par),
                  pl.BlockSpec((None, 1, H, DN_DK, LANES), lambda b: (layer, b, 0, 0, 0)),
                  full(e_mat), full(etb), full(etg)],
        out_specs=[pl.BlockSpec((1, 1, H * LANES), lambda b: (b, 0, 0)),
                   pl.BlockSpec((1, H, DN_DK, LANES), lambda b: (b, 0, 0, 0))],
        out_shape=[jax.ShapeDtypeStruct((Bs, 1, H * LANES), F32), jax.ShapeDtypeStruct(s0.shape[1:], F32)],
        compiler_params=_cparams(1),
        name="sample_dn",
    )(raw3, conv_state, conv_w, ba3, par, s0, e_mat, etb, etg)


def _sample_out_kernel(x_ref, oa_ref, od_ref, gates_ref, dng_ref, wa_ref, wb_ref, wo_ref, y_ref):
    y_ref[...] = _gated_mix(oa_ref[...], od_ref[...], gates_ref[...], dng_ref[...], wa_ref[...], wb_ref[...],
                            wo_ref[...], x_ref[...], _bdot)


def _sample_out(x2d, oa, od, gates, dng, wa, wb, wo):
    args = (x2d, oa, od, gates, dng, wa, wb, wo)
    return pl.pallas_call(
        _sample_out_kernel,
        grid=(1,),
        in_specs=[pl.BlockSpec(a.shape, lambda i: (0, 0)) for a in args],
        out_specs=pl.BlockSpec(x2d.shape, lambda i: (0, 0)),
        out_shape=jax.ShapeDtypeStruct(x2d.shape, F32),
        compiler_params=_cparams(1),
        name="sample_out",
    )(*args)


def _head_indicator(width, head):
    c = jnp.arange(width)[:, None] // head
    return (c == jnp.arange(LANES)[None, :]).astype(F32)


def _prep_layer(l, ln1_g, w_in, q_norm_g, k_norm_g, dn_conv_w, dn_a_log, dn_dt_bias, dn_norm_g, w_out_a, w_out_b,
                w_o, ln2_g, w_rg, b_rg, w_re, b_re, w_e_gate, w_e_up, w_e_down):
    D = w_in.shape[1]
    a_w = 3 * 3 * SWA_GW
    dn_w = DN_HEADS * 3 * DN_DK
    hv = DN_HEADS * DN_DK
    w = w_in[l]
    splits = dict(att=w[:, :a_w], dn=w[:, a_w:a_w + dn_w],
                  ba=jnp.pad(w[:, a_w + dn_w:a_w + dn_w + 2 * DN_HEADS], ((0, 0), (0, LANES - 2 * DN_HEADS))),
                  gate=w[:, a_w + dn_w + 2 * DN_HEADS:])
    assert splits["gate"].shape[1] == hv + 2 * D
    tile_heads = lambda g: jnp.broadcast_to(g[:, None, :], (len(SWA_CONFIGS), SWA_HEADS, SWA_DIM)).reshape(len(SWA_CONFIGS), SWA_GW)
    qg, kg = tile_heads(q_norm_g[l]), tile_heads(k_norm_g[l])
    idx = jnp.arange(MXU) // SWA_DIM
    n_g = len(SWA_CONFIGS)
    par = jnp.zeros((2, LANES), F32)
    par = par.at[0, DN_HEADS:2 * DN_HEADS].set(-jnp.exp(dn_a_log[l].astype(F32)))
    par = par.at[1, DN_HEADS:2 * DN_HEADS].set(dn_dt_bias[l].astype(F32))
    wr = jnp.pad(jnp.concatenate([w_rg[l], w_re[l]], axis=1), ((0, 0), (0, LANES - N_GROUPS - N_EXPERTS)))
    br = jnp.pad(jnp.concatenate([b_rg[l], b_re[l]]), (0, LANES - N_GROUPS - N_EXPERTS)).reshape(1, LANES)
    e8 = _head_indicator(hv, DN_DK)
    return dict(
        bf16={k: v.astype(BF16) for k, v in splits.items()},
        ln1=ln1_g[l].reshape(1, D), ln2=ln2_g[l].reshape(1, D),
        qg=qg, kg=kg,
        w_grp=[jnp.concatenate([w[:, s * n_g * SWA_GW + g * SWA_GW:s * n_g * SWA_GW + (g + 1) * SWA_GW]
                                for s in range(3)], axis=1).astype(BF16) for g in range(n_g)],
        ng_grp=[jnp.concatenate([qg[g] * SWA_DIM ** -0.5, kg[g]]).reshape(1, 1, 2 * SWA_GW) for g in range(n_g)],
        bd=((idx[:, None] == idx[None, :]).astype(F32) / SWA_DIM).astype(BF16),
        conv_w=dn_conv_w[l], par=par, dng=dn_norm_g[l].reshape(1, DN_DK),
        wa=w_out_a[l].astype(BF16), wb=w_out_b[l].astype(BF16), wo=w_o[l].astype(BF16), wr=wr.astype(BF16), br=br,
        wg=w_e_gate, wu=w_e_up, wd=w_e_down, layer=l,
        e_dn=e8, etb=e8.T, etg=jnp.roll(e8, DN_HEADS, axis=1).T,
        e_att=_head_indicator(SWA_GW, SWA_DIM).T.astype(BF16),
    )


def _layer_prompt(x, p):
    B, L, D = x.shape
    N = B * L
    x2d = x.reshape(N, D)
    bw = p["bf16"]
    pks, tails = [], []
    hgs = _norm_permute(x, p["ln1"], tuple(d for _, d in SWA_CONFIGS), tm=min(512, L))
    for g, (win, dil) in enumerate(SWA_CONFIGS):
        assert L >= win
        tmr = min(1024, L // dil)
        pk, tail = _proj_attn(hgs[g], p["w_grp"][g], p["ng_grp"][g], p["bd"], tmr=tmr, nr=min(dil, max(1, 1024 // tmr)))
        pks.append(pk)
        tails.append(tail)
    tmp = min(1024, N)
    h0 = hgs[0].reshape(B, L, D)
    qd, kd, vd, raw_tail = _proj_dn(h0, bw["dn"], p["conv_w"], tm=min(512, L))
    gates = _matmul(h0.reshape(N, D), bw["gate"], tm=tmp, tn=bw["gate"].shape[1], out_dtype=BF16, name="proj_gate")
    os_, ls_ = [], []
    for pk in pks:
        d, M = pk.shape[1], pk.shape[2]
        o, lse = _attn(pk.reshape(B * d, M, pk.shape[3]), tq=min(512, M))
        os_.append(o.reshape(B, d, M, SWA_GW))
        ls_.append(lse.reshape(B, d, M, LANES))
    gb, gt_rows = _dn_gates(h0, bw["ba"], p["par"], tm=min(1024, L), tl=min(256, L))
    u, w, qdec, kdec, a, gt = _dn_intra(qd, kd, vd, gb, gt_rows, tl=min(2048, L))
    od, s_new = _dn_scan(u, w, qdec, kdec, a, gt, jnp.zeros((B, DN_HEADS, DN_DK, LANES), F32), tl=min(512, L),
                         bb=2 if B % 2 == 0 else 1)
    x2 = _out_proj(x2d, os_, ls_, od.reshape(N, -1), gates, p["dng"], p["wa"], p["wb"], p["wo"], p["e_att"],
                   B=B, L=L, tm=min(512, L))
    y = _moe(x2, p["ln2"], p["wr"], p["br"], p["wg"], p["wu"], p["wd"], p["layer"], tm=512)
    return y.reshape(B, L, D), tails, raw_tail[:, 8 - (DN_CONV - 1):], s_new


def _layer_sample(x, caches, conv_state, s0, layer, p):
    Bs, T, D = x.shape
    assert T == 1
    x2d = x.reshape(Bs, D)
    bw = p["bf16"]
    proj = functools.partial(_proj_plain, x2d, p["ln1"], tm=Bs, out_dtype=F32)
    z_att = proj(bw["att"], tn=1536, name="sproj_att")
    raw = proj(bw["dn"], tn=1536, name="sproj_dn")
    gates = proj(bw["gate"], tn=1536, name="sproj_gate")
    ba = proj(bw["ba"], tn=LANES, name="sproj_ba")
    oa, kv = _sample_attn(z_att.reshape(Bs, 1, -1), caches, layer, p["qg"], p["kg"])
    raw3 = raw.reshape(Bs, 1, -1)
    od, s_new = _sample_dn(raw3, conv_state, s0, layer, p["conv_w"], ba.reshape(Bs, 1, LANES), p["par"],
                           p["e_dn"], p["etb"], p["etg"])
    x2 = _sample_out(x2d, oa.reshape(Bs, -1), od.reshape(Bs, -1), gates, p["dng"], p["wa"], p["wb"], p["wo"])
    y = _moe(x2, p["ln2"], p["wr"], p["br"], p["wg"], p["wu"], p["wd"], p["layer"], tm=Bs)
    W2 = 2 * SWA_GW
    kvs = [kv[:, :, g * W2:(g + 1) * W2].reshape(Bs, 1, 2, SWA_HEADS, SWA_DIM) for g in range(len(SWA_CONFIGS))]
    new_conv = jnp.concatenate([conv_state[layer][:, 1:], raw3], axis=1)
    return y.reshape(Bs, 1, D), kvs, new_conv, s_new


def kernel(x_prompt, x_sample, cache_swa0_kv, cache_swa1_kv, cache_swa2_kv, state_dn_conv, state_dn_S, ln1_g, w_in,
           q_norm_g, k_norm_g, dn_conv_w, dn_a_log, dn_dt_bias, dn_norm_g, w_out_a, w_out_b, w_o, ln2_g, w_rg, b_rg,
           w_re, b_re, w_e_gate, w_e_up, w_e_down):
    yp, ys = x_prompt, x_sample
    outs = [[] for _ in range(10)]
    for l in range(w_in.shape[0]):
        p = _prep_layer(l, ln1_g, w_in, q_norm_g, k_norm_g, dn_conv_w, dn_a_log, dn_dt_bias, dn_norm_g, w_out_a,
                        w_out_b, w_o, ln2_g, w_rg, b_rg, w_re, b_re, w_e_gate, w_e_up, w_e_down)
        yp, pkv, pconv, ps = _layer_prompt(yp, p)
        ys, skv, sconv, ss = _layer_sample(ys, (cache_swa0_kv, cache_swa1_kv, cache_swa2_kv), state_dn_conv,
                                           state_dn_S, l, p)
        for lst, val in zip(outs, (*pkv, pconv, ps, *skv, sconv, ss)):
            lst.append(val)
    return (yp, ys, *(jnp.stack(o) for o in outs))
```
